```python
import math
import jax, jax.numpy as jnp
from jax import lax
import numpy as np

D_MODEL = 1024
BATCH = 8
SEQ = 2048
DEPTH = 4

N_EVEN = (DEPTH + 1) // 2
N_ODD = DEPTH // 2
MIX_WIDTH = D_MODEL
EPS = 1e-6

HG_DIM = 128
HG_WIDTH = MIX_WIDTH // 2
HG_HEADS = HG_WIDTH // HG_DIM
DA_HEAD_DIM = 64
DA_V_DIM = 2 * DA_HEAD_DIM
DA_WIDTH = MIX_WIDTH - HG_WIDTH
DA_HEADS = DA_WIDTH // DA_V_DIM
DA_QK_WIDTH = DA_HEADS * 2 * DA_HEAD_DIM
Q_BLOCK = 128
GLA_HEADS = 4
GLA_DK = MIX_WIDTH // 2 // GLA_HEADS
GLA_DV = MIX_WIDTH // GLA_HEADS
GLA_GATE_RANK = 16
GLA_TAU = 16.0
CHUNK = 64
N_GROUPS = 4
EXPERTS_PER_GROUP = 4
N_EXPERTS = N_GROUPS * EXPERTS_PER_GROUP
TOP_K = 2
D_EXPERT = D_MODEL // 2

EVEN_SIZES = (HG_WIDTH, HG_WIDTH, HG_WIDTH, HG_WIDTH, DA_QK_WIDTH, DA_QK_WIDTH, DA_WIDTH)
EVEN_IN = 3584
ODD_SIZES = (GLA_HEADS * GLA_DK, GLA_HEADS * GLA_DK, GLA_HEADS * GLA_DV, MIX_WIDTH, GLA_GATE_RANK)
ODD_IN = 3088

kernel_name = 'hybrid_hgrn2_diffattn_gla_hmoe'


def rms_norm(x, w):
    xf = x.astype(jnp.float32)
    y = xf * lax.rsqrt(jnp.mean(xf * xf, axis=-1, keepdims=True) + EPS)
    return (y * w.astype(jnp.float32)).astype(x.dtype)


def split_cols(p, sizes):
    idx = np.cumsum(sizes)[:-1].tolist()
    return jnp.split(p, idx, axis=-1)


def split_heads(a, n_heads):
    b, t, _ = a.shape
    return a.reshape(b, t, n_heads, -1).transpose(0, 2, 1, 3)


def merge_heads(a):
    b, h, t, d = a.shape
    return a.transpose(0, 2, 1, 3).reshape(b, t, h * d)


def chunk_gated_linear_attention(q, k, v, log_f):
    b, h, t, dk = q.shape
    dv = v.shape[-1]
    n = t // CHUNK

    def to_chunks(a):
        return jnp.moveaxis(a.reshape(b, h, n, CHUNK, a.shape[-1]), 2, 0)

    qc, kc, vc, gc = (to_chunks(a.astype(jnp.float32)) for a in (q, k, v, log_f))
    causal = jnp.tril(jnp.ones((CHUNK, CHUNK), dtype=bool))

    def step(state, inp):
        qi, ki, vi, gi = inp
        cum = jnp.cumsum(gi, axis=-2)
        rel = cum[..., :, None, :] - cum[..., None, :, :]
        decay = jnp.exp(jnp.where(causal[:, :, None], rel, -jnp.inf))
        scores = jnp.sum(qi[..., :, None, :] * ki[..., None, :, :] * decay, axis=-1)
        o_intra = jnp.einsum('bhij,bhjv->bhiv', scores, vi)
        o_inter = jnp.einsum('bhid,bhdv->bhiv', qi * jnp.exp(cum), state)
        last = cum[..., -1:, :]
        k_dec = ki * jnp.exp(last - cum)
        new_state = jnp.exp(last[..., 0, :])[..., None] * state + jnp.einsum('bhjd,bhjv->bhdv', k_dec, vi)
        return new_state, o_intra + o_inter

    s0 = jnp.zeros((b, h, dk, dv), jnp.float32)
    _, o = lax.scan(step, s0, (qc, kc, vc, gc))
    o = jnp.moveaxis(o, 0, 2).reshape(b, h, t, dv)
    return o.astype(v.dtype)


def differential_attention(q1, q2, k1, k2, v, lam):
    scale = DA_HEAD_DIM ** -0.5
    t = q1.shape[2]
    outs = []
    for blk in range(t // Q_BLOCK):
        s, e = blk * Q_BLOCK, (blk + 1) * Q_BLOCK
        mask = jnp.arange(e)[None, :] <= jnp.arange(s, e)[:, None]

        def probs(q, k):
            sc = jnp.einsum('bhqd,bhkd->bhqk', q[:, :, s:e], k[:, :, :e]).astype(jnp.float32) * scale
            return jax.nn.softmax(jnp.where(mask, sc, -jnp.inf), axis=-1)

        w = probs(q1, k1) - lam * probs(q2, k2)
        outs.append(jnp.einsum('bhqk,bhkv->bhqv', w.astype(v.dtype), v[:, :, :e]))
    return jnp.concatenate(outs, axis=2)


def hgrn2_diff_mixer(h, w_in, w_out, lb, hg_o_norm, q_norm, k_norm, lam_vecs, subln, lam_init):
    b, t, _ = h.shape
    hg_q, hg_f, hg_i, hg_g, da_q, da_k, da_v = split_cols(h @ w_in, EVEN_SIZES)
    lb_h = lb.reshape(HG_HEADS, 1, HG_DIM)
    fl = split_heads(hg_f, HG_HEADS).astype(jnp.float32)
    log_f = jnp.log(lb_h + (1.0 - lb_h) * jax.nn.sigmoid(fl))
    key_in = (1.0 - lb_h) * jax.nn.sigmoid(-fl)
    q_hg = jax.nn.silu(split_heads(hg_q, HG_HEADS)) * (HG_DIM ** -0.5)
    o_hg = chunk_gated_linear_attention(q_hg, key_in, split_heads(hg_i, HG_HEADS), log_f)
    o_a = merge_heads(rms_norm(o_hg, hg_o_norm)) * jax.nn.silu(hg_g)
    dq = rms_norm(da_q.reshape(b, t, DA_HEADS, 2, DA_HEAD_DIM), q_norm)
    dk = rms_norm(da_k.reshape(b, t, DA_HEADS, 2, DA_HEAD_DIM), k_norm)
    q1, q2 = dq[..., 0, :].transpose(0, 2, 1, 3), dq[..., 1, :].transpose(0, 2, 1, 3)
    k1, k2 = dk[..., 0, :].transpose(0, 2, 1, 3), dk[..., 1, :].transpose(0, 2, 1, 3)
    lv = lam_vecs.astype(jnp.float32)
    lam = jnp.exp(jnp.sum(lv[0] * lv[1])) - jnp.exp(jnp.sum(lv[2] * lv[3])) + lam_init
    o_da = differential_attention(q1, q2, k1, k2, split_heads(da_v, DA_HEADS), lam)
    o_b = merge_heads(rms_norm(o_da, subln)) * (1.0 - lam_init)
    return jnp.concatenate([o_a, o_b], axis=-1) @ w_out


def gla_mixer(h, w_in, w_alpha, b_alpha, o_norm, w_out):
    q, k, v, g, a = split_cols(h @ w_in, ODD_SIZES)
    log_alpha = jax.nn.log_sigmoid((a @ w_alpha).astype(jnp.float32) + b_alpha.astype(jnp.float32)) / GLA_TAU
    o = chunk_gated_linear_attention(split_heads(q, GLA_HEADS) * (GLA_DK ** -0.5), split_heads(k, GLA_HEADS),
                                     split_heads(v, GLA_HEADS), split_heads(log_alpha, GLA_HEADS))
    o = merge_heads(rms_norm(o, o_norm)) * jax.nn.silu(g)
    return o @ w_out


def hierarchical_moe(h, w_group, b_group, w_router, b_router, w_gate, w_up, w_down):
    b, t, d = h.shape
    hf = h.reshape(b * t, d)
    grp_p = jax.nn.softmax((hf @ w_group).astype(jnp.float32) + b_group.astype(jnp.float32), axis=-1)
    g_sel = jnp.argmax(grp_p, axis=-1)
    g_w = jnp.take_along_axis(grp_p, g_sel[:, None], axis=-1)
    logits_all = jnp.einsum('nd,gde->nge', hf, w_router).astype(jnp.float32) + b_router.astype(jnp.float32)
    exp_logits = jnp.take_along_axis(logits_all, g_sel[:, None, None], axis=1)[:, 0]
    top_p, top_i = lax.top_k(jax.nn.softmax(exp_logits, axis=-1), TOP_K)
    weights = g_w * top_p / jnp.sum(top_p, axis=-1, keepdims=True)
    expert_id = g_sel[:, None] * EXPERTS_PER_GROUP + top_i
    combine = jnp.sum(jax.nn.one_hot(expert_id, N_EXPERTS, dtype=jnp.float32) * weights[..., None], axis=1)
    combine = combine.astype(h.dtype)
    y = jnp.zeros_like(hf)
    for e in range(N_EXPERTS):
        act = jax.nn.silu(hf @ w_gate[e]) * (hf @ w_up[e])
        y = y + combine[:, e:e + 1] * (act @ w_down[e])
    return y.reshape(b, t, d)


def setup_inputs(seed: int = 0) -> dict:
    key = jax.random.key(seed)
    ks = jax.random.split(key, 24)

    def nrm(k, shape, scale):
        return jax.random.normal(k, shape, jnp.float32) * scale

    out_scale = (2 * DEPTH) ** -0.5
    dsc = D_MODEL ** -0.5
    return {
        'x': nrm(ks[0], (BATCH, SEQ, D_MODEL), 1.0),
        'attn_norm': 1.0 + nrm(ks[1], (DEPTH, D_MODEL), 0.02),
        'ffn_norm': 1.0 + nrm(ks[2], (DEPTH, D_MODEL), 0.02),
        'even_w_in': nrm(ks[3], (N_EVEN, D_MODEL, EVEN_IN), dsc),
        'even_w_out': nrm(ks[4], (N_EVEN, MIX_WIDTH, D_MODEL), MIX_WIDTH ** -0.5 * out_scale),
        'hg_lb_logits': nrm(ks[5], (N_EVEN, HG_WIDTH), 0.5),
        'hg_o_norm': 1.0 + nrm(ks[6], (N_EVEN, HG_DIM), 0.02),
        'da_q_norm': 1.0 + nrm(ks[7], (N_EVEN, DA_HEAD_DIM), 0.02),
        'da_k_norm': 1.0 + nrm(ks[8], (N_EVEN, DA_HEAD_DIM), 0.02),
        'da_lambda': nrm(ks[9], (N_EVEN, 4, DA_HEAD_DIM), 0.1),
        'da_subln': 1.0 + nrm(ks[10], (N_EVEN, DA_V_DIM), 0.02),
        'odd_w_in': nrm(ks[11], (N_ODD, D_MODEL, ODD_IN), dsc),
        'gla_w_alpha': nrm(ks[12], (N_ODD, GLA_GATE_RANK, GLA_HEADS * GLA_DK), GLA_GATE_RANK ** -0.5),
        'gla_b_alpha': nrm(ks[13], (N_ODD, GLA_HEADS * GLA_DK), 0.1),
        'gla_o_norm': 1.0 + nrm(ks[14], (N_ODD, GLA_DV), 0.02),
        'odd_w_out': nrm(ks[15], (N_ODD, MIX_WIDTH, D_MODEL), MIX_WIDTH ** -0.5 * out_scale),
        'moe_w_group': nrm(ks[16], (DEPTH, D_MODEL, N_GROUPS), dsc),
        'moe_b_group': nrm(ks[17], (DEPTH, N_GROUPS), 0.01),
        'moe_w_router': nrm(ks[18], (DEPTH, N_GROUPS, D_MODEL, EXPERTS_PER_GROUP), dsc),
        'moe_b_router': nrm(ks[19], (DEPTH, N_GROUPS, EXPERTS_PER_GROUP), 0.01),
        'moe_w_gate': nrm(ks[20], (DEPTH, N_EXPERTS, D_MODEL, D_EXPERT), dsc),
        'moe_w_up': nrm(ks[21], (DEPTH, N_EXPERTS, D_MODEL, D_EXPERT), dsc),
        'moe_w_down': nrm(ks[22], (DEPTH, N_EXPERTS, D_EXPERT, D_MODEL), D_EXPERT ** -0.5 * out_scale),
    }


def reference(x, attn_norm, ffn_norm, even_w_in, even_w_out, hg_lb_logits, hg_o_norm, da_q_norm, da_k_norm,
              da_lambda, da_subln, odd_w_in, gla_w_alpha, gla_b_alpha, gla_o_norm, odd_w_out,
              moe_w_group, moe_b_group, moe_w_router, moe_b_router, moe_w_gate, moe_w_up, moe_w_down):
    lb_sm = jax.nn.softmax(hg_lb_logits.astype(jnp.float32), axis=0)
    lb_all = jnp.cumsum(lb_sm, axis=0) - lb_sm[0:1]
    for layer in range(DEPTH):
        j = layer // 2
        h = rms_norm(x, attn_norm[layer])
        if layer % 2 == 0:
            lam_init = 0.8 - 0.6 * math.exp(-0.3 * layer)
            mix = hgrn2_diff_mixer(h, even_w_in[j], even_w_out[j], lb_all[j], hg_o_norm[j], da_q_norm[j],
                                   da_k_norm[j], da_lambda[j], da_subln[j], lam_init)
        else:
            mix = gla_mixer(h, odd_w_in[j], gla_w_alpha[j], gla_b_alpha[j], gla_o_norm[j], odd_w_out[j])
        x = x + mix
        h = rms_norm(x, ffn_norm[layer])
        x = x + hierarchical_moe(h, moe_w_group[layer], moe_b_group[layer], moe_w_router[layer],
                                 moe_b_router[layer], moe_w_gate[layer], moe_w_up[layer], moe_w_down[layer])
    return x
```

```python
import functools
import math

import numpy as np
import jax
import jax.numpy as jnp
from jax import lax
from jax.experimental import pallas as pl
from jax.experimental.pallas import tpu as pltpu

F32 = jnp.float32
BF16 = jnp.bfloat16
EPS = 1e-6

D_MODEL = 1024
HEADS = 4
DK = 128
HG_WIDTH = 512
DA_HEAD_DIM = 64
GLA_DV = 256
GLA_RANK_PAD = 128
GLA_TAU = 16.0
N_GROUPS = 4
EXPERTS_PER_GROUP = 4
N_EXPERTS = 16
D_EXPERT = 512
N_PAIRS = 6
N_BUCKETS = N_GROUPS * N_PAIRS
ROUTER_ROWS = 32

CHUNK = 128
LEVELS = 7
ATTN_BLOCK = 512
ROW_TILE = 512
MOE_TILE = 256
VMEM_LIMIT = 56 * 1024 * 1024

PAIR_LO = (0, 0, 0, 1, 1, 2)
PAIR_HI = (1, 2, 3, 2, 3, 3)


def _dot(a, b):
    return jnp.dot(a, b, preferred_element_type=F32)


def _dot_nt(a, b):
    return lax.dot_general(a, b, (((1,), (1,)), ((), ())), preferred_element_type=F32)


def _dot_tn(a, b):
    return lax.dot_general(a, b, (((0,), (0,)), ((), ())), preferred_element_type=F32)


def _rms(x, w):
    return x * lax.rsqrt(jnp.mean(x * x, axis=-1, keepdims=True) + EPS) * w


def _silu(x):
    return x * jax.nn.sigmoid(x)


def _params(*sem):
    return pltpu.CompilerParams(dimension_semantics=sem, vmem_limit_bytes=VMEM_LIMIT)


def _inproj_kernel(x_ref, nw_ref, w_ref, o_ref):
    h = _rms(x_ref[...], nw_ref[...]).astype(BF16)
    o_ref[...] = _dot(h, w_ref[...])


def _inproj(x, norm_w, w):
    n, d = x.shape
    n_out = w.shape[1]
    return pl.pallas_call(
        _inproj_kernel,
        grid=(n // ROW_TILE,),
        in_specs=[pl.BlockSpec((ROW_TILE, d), lambda i: (i, 0)),
                  pl.BlockSpec((1, d), lambda i: (0, 0)),
                  pl.BlockSpec((d, n_out), lambda i: (0, 0))],
        out_specs=pl.BlockSpec((ROW_TILE, n_out), lambda i: (i, 0)),
        out_shape=jax.ShapeDtypeStruct((n, n_out), F32),
        compiler_params=_params("parallel"),
        name="inproj",
    )(x, norm_w.reshape(1, d), w)


def _decay_tables():
    c = CHUNK
    a = np.zeros((LEVELS + 2, c, c), np.float32)
    t = np.arange(c)
    for l in range(LEVELS):
        s = 1 << l
        for r in range(c):
            mid = (r // (2 * s)) * 2 * s + s - 1
            if (r >> l) & 1:
                a[l, r] = (t > mid) & (t <= r)
            else:
                a[l, r] = (t > r) & (t <= mid)
    a[LEVELS] = t[None, :] <= t[:, None]
    a[LEVELS + 1] = t[None, :] > t[:, None]
    a = a.reshape((LEVELS + 2) * c, c)
    a3 = np.concatenate([a, a, a], axis=1)
    i, j = np.meshgrid(t, t, indexing="ij")
    x = np.maximum(i ^ j, 1)
    lvl = np.where(j < i, np.floor(np.log2(x)).astype(np.int32), np.where(i == j, LEVELS, -1))
    return jnp.asarray(a3, BF16), jnp.asarray(lvl, jnp.int32)


def _gla_chunk(q, k, v, g, st_ref, a3, lvl, rowid):
    c = CHUNK
    g_hi = g.astype(BF16)
    r1 = g - g_hi.astype(F32)
    g_mid = r1.astype(BF16)
    g_lo = (r1 - g_mid.astype(F32)).astype(BF16)
    g3 = jnp.concatenate([g_hi, g_mid, g_lo], axis=0)
    f = jnp.exp(_dot(a3, g3))
    sc = jnp.where(lvl == LEVELS, jnp.sum(q * k, axis=-1, keepdims=True), 0.0)
    for l in range(LEVELS):
        is_query = ((rowid >> l) & 1) == 1
        m = (jnp.where(is_query, q, k) * f[l * c:(l + 1) * c]).astype(BF16)
        sc = jnp.where(lvl == l, _dot_nt(m, m), sc)
    qd = (q * f[LEVELS * c:(LEVELS + 1) * c]).astype(BF16)
    kd = (k * f[(LEVELS + 1) * c:(LEVELS + 2) * c]).astype(BF16)
    vb = v.astype(BF16)
    st = st_ref[...]
    o = _dot(sc.astype(BF16), vb) + _dot_nt(qd, st.astype(BF16))
    chunk_decay = f[(LEVELS + 1) * c - 1:(LEVELS + 1) * c, :]
    st_ref[...] = st * chunk_decay + _dot_tn(vb, kd)
    return o


def _hgrn_kernel(q_ref, f_ref, i_ref, g_ref, lb_ref, on_ref, a3_ref, lvl_ref, o_ref, st_ref):
    @pl.when(pl.program_id(1) == 0)
    def _():
        st_ref[...] = jnp.zeros_like(st_ref)

    a3 = a3_ref[...]
    lvl = lvl_ref[...]
    rowid = lax.broadcasted_iota(jnp.int32, (CHUNK, 1), 0)
    for h in range(HEADS):
        sl = slice(h * DK, (h + 1) * DK)
        fl = f_ref[:, sl]
        lb = lb_ref[:, sl]
        log_f = jnp.log(lb + (1.0 - lb) * jax.nn.sigmoid(fl))
        key_in = (1.0 - lb) * jax.nn.sigmoid(-fl)
        q = _silu(q_ref[:, sl]) * (DK ** -0.5)
        o = _gla_chunk(q, key_in, i_ref[:, sl], log_f, st_ref.at[h], a3, lvl, rowid)
        o = _rms(o, on_ref[...])
        o_ref[:, sl] = (o * _silu(g_ref[:, sl])).astype(BF16)


def _hgrn(proj, lb, o_norm, batch, seq):
    n = proj.shape[0]
    nt = seq // CHUNK
    a3, lvl = _decay_tables()
    col = lambda j: pl.BlockSpec((CHUNK, HG_WIDTH), lambda b, t: (b * nt + t, j))
    const = lambda shape: pl.BlockSpec(shape, lambda b, t: (0, 0))
    return pl.pallas_call(
        _hgrn_kernel,
        grid=(batch, nt),
        in_specs=[col(0), col(1), col(2), col(3), const((1, HG_WIDTH)), const((1, DK)),
                  const(a3.shape), const(lvl.shape)],
        out_specs=pl.BlockSpec((CHUNK, HG_WIDTH), lambda b, t: (b * nt + t, 0)),
        out_shape=jax.ShapeDtypeStruct((n, HG_WIDTH), BF16),
        scratch_shapes=[pltpu.VMEM((HEADS, DK, DK), F32)],
        compiler_params=_params("parallel", "arbitrary"),
        name="hgrn",
    )(proj, proj, proj, proj, lb.reshape(1, HG_WIDTH), o_norm.reshape(1, DK), a3, lvl)


def _log_sigmoid(x):
    return jnp.minimum(x, 0.0) - jnp.log1p(jnp.exp(-jnp.abs(x)))


def _gla_kernel(q_ref, k_ref, v_ref, g_ref, a_ref, wa_ref, ba_ref, on_ref, a3_ref, lvl_ref, o_ref, st_ref):
    @pl.when(pl.program_id(1) == 0)
    def _():
        st_ref[...] = jnp.zeros_like(st_ref)

    a3 = a3_ref[...]
    lvl = lvl_ref[...]
    rowid = lax.broadcasted_iota(jnp.int32, (CHUNK, 1), 0)
    gate_logit = _dot(a_ref[...].astype(BF16), wa_ref[...]) + ba_ref[...]
    log_alpha = _log_sigmoid(gate_logit) * (1.0 / GLA_TAU)
    for h in range(HEADS):
        sl = slice(h * DK, (h + 1) * DK)
        sv = slice(h * GLA_DV, (h + 1) * GLA_DV)
        q = q_ref[:, sl] * (DK ** -0.5)
        o = _gla_chunk(q, k_ref[:, sl], v_ref[:, sv], log_alpha[:, sl], st_ref.at[h], a3, lvl, rowid)
        o = _rms(o, on_ref[...])
        o_ref[:, sv] = (o * _silu(g_ref[:, sv])).astype(BF16)


def _gla(proj, w_alpha, b_alpha, o_norm, batch, seq):
    n = proj.shape[0]
    nt = seq // CHUNK
    qk_w = HEADS * DK
    v_w = HEADS * GLA_DV
    a3, lvl = _decay_tables()
    row = lambda b, t: b * nt + t
    const = lambda shape: pl.BlockSpec(shape, lambda b, t: (0, 0))
    return pl.pallas_call(
        _gla_kernel,
        grid=(batch, nt),
        in_specs=[pl.BlockSpec((CHUNK, qk_w), lambda b, t: (row(b, t), 0)),
                  pl.BlockSpec((CHUNK, qk_w), lambda b, t: (row(b, t), 1)),
                  pl.BlockSpec((CHUNK, v_w), lambda b, t: (row(b, t), 1)),
                  pl.BlockSpec((CHUNK, v_w), lambda b, t: (row(b, t), 2)),
                  pl.BlockSpec((CHUNK, GLA_RANK_PAD), lambda b, t: (row(b, t), (2 * qk_w + 2 * v_w) // GLA_RANK_PAD)),
                  const((GLA_RANK_PAD, qk_w)), const((1, qk_w)), const((1, GLA_DV)),
                  const(a3.shape), const(lvl.shape)],
        out_specs=pl.BlockSpec((CHUNK, v_w), lambda b, t: (row(b, t), 0)),
        out_shape=jax.ShapeDtypeStruct((n, v_w), BF16),
        scratch_shapes=[pltpu.VMEM((HEADS, GLA_DV, DK), F32)],
        compiler_params=_params("parallel", "arbitrary"),
        name="gla",
    )(proj, proj, proj, proj, proj, w_alpha, b_alpha.reshape(1, qk_w), o_norm.reshape(1, GLA_DV), a3, lvl)


def _qk_norm(x, w2):
    lane = lax.broadcasted_iota(jnp.int32, x.shape, 1)
    first = lane < DA_HEAD_DIM
    sq = x * x
    ms1 = jnp.sum(jnp.where(first, sq, 0.0), axis=-1, keepdims=True) * (1.0 / DA_HEAD_DIM)
    ms2 = jnp.sum(jnp.where(first, 0.0, sq), axis=-1, keepdims=True) * (1.0 / DA_HEAD_DIM)
    r = jnp.where(first, lax.rsqrt(ms1 + EPS), lax.rsqrt(ms2 + EPS))
    return x * r * w2, first


def _attn_kernel(qi_tab, ki_tab, q_ref, k_ref, v_ref, qn_ref, kn_ref, lam_ref, sub_ref, o_ref,
                 qs_ref, m_ref, l_ref, acc_ref, *, lam_init):
    p = pl.program_id(1)
    qi = qi_tab[p]
    ki = ki_tab[p]
    tq = ATTN_BLOCK
    dv = 2 * DA_HEAD_DIM

    @pl.when(ki == 0)
    def _():
        for h in range(HEADS):
            sl = slice(h * dv, (h + 1) * dv)
            qn, first = _qk_norm(q_ref[:, sl], qn_ref[...])
            qn = qn * (DA_HEAD_DIM ** -0.5)
            qs_ref[h, :tq] = jnp.where(first, qn, 0.0).astype(BF16)
            qs_ref[h, tq:] = jnp.where(first, 0.0, qn).astype(BF16)
        m_ref[...] = jnp.full_like(m_ref, -jnp.inf)
        l_ref[...] = jnp.zeros_like(l_ref)
        acc_ref[...] = jnp.zeros_like(acc_ref)

    def step(masked):
        for h in range(HEADS):
            sl = slice(h * dv, (h + 1) * dv)
            kn, _ = _qk_norm(k_ref[:, sl], kn_ref[...])
            s = _dot_nt(qs_ref[h], kn.astype(BF16))
            if masked:
                row = lax.broadcasted_iota(jnp.int32, s.shape, 0)
                row = jnp.where(row >= tq, row - tq, row)
                colid = lax.broadcasted_iota(jnp.int32, s.shape, 1)
                s = jnp.where(colid <= row, s, -jnp.inf)
            m_prev = m_ref[h]
            m_new = jnp.maximum(m_prev, jnp.max(s, axis=-1, keepdims=True))
            alpha = jnp.exp(m_prev - m_new)
            pr = jnp.exp(s - m_new)
            l_ref[h] = alpha * l_ref[h] + jnp.sum(pr, axis=-1, keepdims=True)
            acc_ref[h] = alpha * acc_ref[h] + _dot(pr.astype(BF16), v_ref[:, sl].astype(BF16))
            m_ref[h] = m_new

    @pl.when(ki < qi)
    def _():
        step(False)

    @pl.when(ki == qi)
    def _():
        step(True)
        lv = lam_ref[...]
        lam = (jnp.exp(jnp.sum(lv[0:1] * lv[1:2], axis=-1, keepdims=True))
               - jnp.exp(jnp.sum(lv[2:3] * lv[3:4], axis=-1, keepdims=True)) + lam_init)
        for h in range(HEADS):
            sl = slice(h * dv, (h + 1) * dv)
            o1 = acc_ref[h, :tq] / l_ref[h, :tq]
            o2 = acc_ref[h, tq:] / l_ref[h, tq:]
            o = _rms(o1 - lam * o2, sub_ref[...]) * (1.0 - lam_init)
            o_ref[:, sl] = o.astype(BF16)


def _diff_attn(proj, q_norm, k_norm, lam_vecs, subln, lam_init, batch, seq):
    n = proj.shape[0]
    blk = ATTN_BLOCK
    nq = seq // blk
    width = HEADS * 2 * DA_HEAD_DIM
    pairs = [(qi, ki) for qi in range(nq) for ki in range(qi + 1)]
    qi_tab = jnp.asarray([p[0] for p in pairs], jnp.int32)
    ki_tab = jnp.asarray([p[1] for p in pairs], jnp.int32)
    q_col = (4 * HG_WIDTH) // width
    const = lambda shape: pl.BlockSpec(shape, lambda b, p, qt, kt: (0, 0))
    grid_spec = pltpu.PrefetchScalarGridSpec(
        num_scalar_prefetch=2,
        grid=(batch, len(pairs)),
        in_specs=[pl.BlockSpec((blk, width), lambda b, p, qt, kt: (b * nq + qt[p], q_col)),
                  pl.BlockSpec((blk, width), lambda b, p, qt, kt: (b * nq + kt[p], q_col + 1)),
                  pl.BlockSpec((blk, width), lambda b, p, qt, kt: (b * nq + kt[p], q_col + 2)),
                  const((1, 2 * DA_HEAD_DIM)), const((1, 2 * DA_HEAD_DIM)),
                  const((4, DA_HEAD_DIM)), const((1, 2 * DA_HEAD_DIM))],
        out_specs=pl.BlockSpec((blk, width), lambda b, p, qt, kt: (b * nq + qt[p], 0)),
        scratch_shapes=[pltpu.VMEM((HEADS, 2 * blk, 2 * DA_HEAD_DIM), BF16),
                        pltpu.VMEM((HEADS, 2 * blk, 1), F32),
                        pltpu.VMEM((HEADS, 2 * blk, 1), F32),
                        pltpu.VMEM((HEADS, 2 * blk, 2 * DA_HEAD_DIM), F32)])
    return pl.pallas_call(
        functools.partial(_attn_kernel, lam_init=lam_init),
        grid_spec=grid_spec,
        out_shape=jax.ShapeDtypeStruct((n, width), BF16),
        compiler_params=_params("parallel", "arbitrary"),
        name="diff_attn",
    )(qi_tab, ki_tab, proj, proj, proj,
      jnp.tile(q_norm, 2).reshape(1, -1), jnp.tile(k_norm, 2).reshape(1, -1),
      lam_vecs, subln.reshape(1, -1))


def _outproj_router_kernel(*refs, n_a):
    a_refs = refs[:n_a]
    (w_ref, x_ref, nw_ref, wr_hi_ref, wr_lo_ref, br_ref, tri_ref,
     x2_ref, meta_ref, cnt_ref, carry_ref) = refs[n_a:]

    @pl.when(pl.program_id(0) == 0)
    def _():
        carry_ref[...] = jnp.zeros_like(carry_ref)

    x2 = x_ref[...]
    off = 0
    for a_ref in a_refs:
        kk = a_ref.shape[1]
        x2 = x2 + _dot(a_ref[...], w_ref[off:off + kk, :])
        off += kk
    x2_ref[...] = x2

    h = _rms(x2, nw_ref[...])
    h_hi = h.astype(BF16)
    h_lo = (h - h_hi.astype(F32)).astype(BF16)
    wr_hi = wr_hi_ref[...]
    lt = _dot_nt(wr_hi, h_hi) + _dot_nt(wr_hi, h_lo) + _dot_nt(wr_lo_ref[...], h_hi) + br_ref[...]
    row = lambda r: lt[r:r + 1, :]

    gl = [row(g) for g in range(N_GROUPS)]
    gmax = functools.reduce(jnp.maximum, gl)
    gsel = jnp.where(gl[0] == gmax, 0.0, jnp.where(gl[1] == gmax, 1.0, jnp.where(gl[2] == gmax, 2.0, 3.0)))
    g_w = 1.0 / functools.reduce(jnp.add, [jnp.exp(x - gmax) for x in gl])

    el = []
    for e in range(EXPERTS_PER_GROUP):
        v = row(N_GROUPS + e)
        for g in range(1, N_GROUPS):
            v = jnp.where(gsel == float(g), row(N_GROUPS + EXPERTS_PER_GROUP * g + e), v)
        el.append(v)
    emax = functools.reduce(jnp.maximum, el)
    sel = []
    for e in range(EXPERTS_PER_GROUP):
        ahead = jnp.zeros_like(emax)
        for o in range(EXPERTS_PER_GROUP):
            if o < e:
                ahead = ahead + jnp.where(el[o] >= el[e], 1.0, 0.0)
            elif o > e:
                ahead = ahead + jnp.where(el[o] > el[e], 1.0, 0.0)
        sel.append(ahead < 2.0)
    pe = [jnp.where(sel[e], jnp.exp(el[e] - emax), 0.0) for e in range(EXPERTS_PER_GROUP)]
    scale = g_w / functools.reduce(jnp.add, pe)
    zero = jnp.zeros_like(emax)
    seen = zero
    lo_idx, hi_idx, c_lo, c_hi = zero, zero, zero, zero
    for e in range(EXPERTS_PER_GROUP):
        is_lo = sel[e] & (seen == 0.0)
        is_hi = sel[e] & (seen == 1.0)
        lo_idx = jnp.where(is_lo, float(e), lo_idx)
        hi_idx = jnp.where(is_hi, float(e), hi_idx)
        c_lo = jnp.where(is_lo, pe[e] * scale, c_lo)
        c_hi = jnp.where(is_hi, pe[e] * scale, c_hi)
        seen = seen + jnp.where(sel[e], 1.0, 0.0)
    pair = lo_idx * (7.0 - lo_idx) * 0.5 + (hi_idx - lo_idx - 1.0)
    bucket = gsel * float(N_PAIRS) + pair

    bid = lax.broadcasted_iota(jnp.int32, (ROUTER_ROWS, bucket.shape[1]), 0).astype(F32)
    onehot = jnp.where(bid == bucket, 1.0, 0.0)
    before = _dot(onehot.astype(BF16), tri_ref[...]) + carry_ref[...]
    rank = jnp.sum(onehot * before, axis=0, keepdims=True)
    carry = carry_ref[...] + jnp.sum(onehot, axis=1, keepdims=True)
    carry_ref[...] = carry
    cnt_ref[...] = carry
    meta_ref[...] = jnp.concatenate([bucket, rank, c_lo, c_hi, zero, zero, zero, zero], axis=0)


def _outproj_router(a_list, w_out, x, norm_w, w_group, b_group, w_router, b_router):
    n, d = x.shape
    tm = ROW_TILE
    wr = jnp.concatenate([w_group, w_router.transpose(1, 0, 2).reshape(d, N_EXPERTS)], axis=1)
    wr_t = jnp.zeros((ROUTER_ROWS, d), F32).at[:wr.shape[1]].set(wr.T)
    wr_hi = wr_t.astype(BF16)
    wr_lo = (wr_t - wr_hi.astype(F32)).astype(BF16)
    br = jnp.zeros((ROUTER_ROWS, 1), F32).at[:wr.shape[1], 0].set(
        jnp.concatenate([b_group, b_router.reshape(-1)]))
    tri = jnp.asarray(np.triu(np.ones((tm, tm), np.float32), 1), BF16)
    const = lambda shape: pl.BlockSpec(shape, lambda i: (0, 0))
    in_specs = [pl.BlockSpec((tm, a.shape[1]), lambda i: (i, 0)) for a in a_list]
    in_specs += [const((d, d)), pl.BlockSpec((tm, d), lambda i: (i, 0)), const((1, d)),
                 const((ROUTER_ROWS, d)), const((ROUTER_ROWS, d)), const((ROUTER_ROWS, 1)), const((tm, tm))]
    return pl.pallas_call(
        functools.partial(_outproj_router_kernel, n_a=len(a_list)),
        grid=(n // tm,),
        in_specs=in_specs,
        out_specs=[pl.BlockSpec((tm, d), lambda i: (i, 0)),
                   pl.BlockSpec((8, tm), lambda i: (0, i)),
                   const((ROUTER_ROWS, 1))],
        out_shape=[jax.ShapeDtypeStruct((n, d), F32),
                   jax.ShapeDtypeStruct((8, n), F32),
                   jax.ShapeDtypeStruct((ROUTER_ROWS, 1), F32)],
        scratch_shapes=[pltpu.VMEM((ROUTER_ROWS, 1), F32)],
        compiler_params=_params("arbitrary"),
        name="outproj_router",
    )(*a_list, w_out, x, norm_w.reshape(1, d), wr_hi, wr_lo, br, tri)


def _moe_kernel(tid, e_lo, e_hi, n_tiles, x_ref, cw_ref, nw_ref, gu_lo_ref, gu_hi_ref, d_lo_ref, d_hi_ref, o_ref):
    del tid, e_lo, e_hi

    @pl.when(pl.program_id(0) < n_tiles[0])
    def _():
        x = x_ref[...]
        h = _rms(x, nw_ref[...]).astype(BF16)
        cw = cw_ref[...]

        def expert(gu_ref, d_ref):
            gu = _dot(h, gu_ref[0])
            act = _silu(gu[:, :D_EXPERT]) * gu[:, D_EXPERT:]
            return _dot(act.astype(BF16), d_ref[0])

        y = cw[:, 1:2] * expert(gu_lo_ref, d_lo_ref) + cw[:, 2:3] * expert(gu_hi_ref, d_hi_ref)
        o_ref[...] = x + y


def _moe(x2, meta, cnt, norm_w, w_gu, w_down):
    n, d = x2.shape
    tm = MOE_TILE
    max_tiles = n // tm + N_BUCKETS
    bucket = meta[0].astype(jnp.int32)
    rank = meta[1].astype(jnp.int32)
    counts = cnt[:N_BUCKETS, 0].astype(jnp.int32)
    tiles_per_bucket = (counts + tm - 1) // tm
    tile_end = jnp.cumsum(tiles_per_bucket)
    row_start = (tile_end - tiles_per_bucket) * tm
    dest = row_start[bucket] + rank
    n_tiles = tile_end[-1]
    packed = jnp.stack([jnp.arange(n, dtype=F32), meta[2], meta[3], jnp.ones((n,), F32)], axis=1)
    slots = jnp.zeros((max_tiles * tm, 4), F32).at[dest].set(packed)
    src = slots[:, 0].astype(jnp.int32)
    x_sorted = jnp.take(x2, src, axis=0)

    tile_ids = jnp.minimum(jnp.arange(max_tiles, dtype=jnp.int32), n_tiles - 1)
    tile_bucket = jnp.searchsorted(tile_end, tile_ids, side="right").astype(jnp.int32)
    group = tile_bucket // N_PAIRS
    pair = tile_bucket % N_PAIRS
    e_lo = group * EXPERTS_PER_GROUP + jnp.asarray(PAIR_LO, jnp.int32)[pair]
    e_hi = group * EXPERTS_PER_GROUP + jnp.asarray(PAIR_HI, jnp.int32)[pair]

    grid_spec = pltpu.PrefetchScalarGridSpec(
        num_scalar_prefetch=4,
        grid=(max_tiles,),
        in_specs=[pl.BlockSpec((tm, d), lambda t, tid, lo, hi, nt: (tid[t], 0)),
                  pl.BlockSpec((tm, 4), lambda t, tid, lo, hi, nt: (tid[t], 0)),
                  pl.BlockSpec((1, d), lambda t, tid, lo, hi, nt: (0, 0)),
                  pl.BlockSpec((1, d, 2 * D_EXPERT), lambda t, tid, lo, hi, nt: (lo[t], 0, 0)),
                  pl.BlockSpec((1, d, 2 * D_EXPERT), lambda t, tid, lo, hi, nt: (hi[t], 0, 0)),
                  pl.BlockSpec((1, D_EXPERT, d), lambda t, tid, lo, hi, nt: (lo[t], 0, 0)),
                  pl.BlockSpec((1, D_EXPERT, d), lambda t, tid, lo, hi, nt: (hi[t], 0, 0))],
        out_specs=pl.BlockSpec((tm, d), lambda t, tid, lo, hi, nt: (tid[t], 0)))
    out_sorted = pl.pallas_call(
        _moe_kernel,
        grid_spec=grid_spec,
        out_shape=jax.ShapeDtypeStruct((max_tiles * tm, d), F32),
        compiler_params=_params("arbitrary"),
        name="moe",
    )(tile_ids, e_lo, e_hi, n_tiles.reshape(1), x_sorted, slots, norm_w.reshape(1, d), w_gu, w_gu, w_down, w_down)
    return jnp.take(out_sorted, dest, axis=0)


def kernel(x, attn_norm, ffn_norm, even_w_in, even_w_out, hg_lb_logits, hg_o_norm, da_q_norm, da_k_norm, da_lambda, da_subln, odd_w_in, gla_w_alpha, gla_b_alpha, gla_o_norm, odd_w_out, moe_w_group, moe_b_group, moe_w_router, moe_b_router, moe_w_gate, moe_w_up, moe_w_down):
    batch, seq, d = x.shape
    depth = attn_norm.shape[0]
    xf = x.reshape(batch * seq, d)

    lb_sm = jax.nn.softmax(hg_lb_logits.astype(F32), axis=0)
    lb_all = jnp.cumsum(lb_sm, axis=0) - lb_sm[0:1]
    rank = gla_w_alpha.shape[1]

    for layer in range(depth):
        j = layer // 2
        if layer % 2 == 0:
            lam_init = 0.8 - 0.6 * math.exp(-0.3 * layer)
            proj = _inproj(xf, attn_norm[layer], even_w_in[j].astype(BF16))
            o_a = _hgrn(proj, lb_all[j], hg_o_norm[j], batch, seq)
            o_b = _diff_attn(proj, da_q_norm[j], da_k_norm[j], da_lambda[j], da_subln[j], lam_init, batch, seq)
            mix = [o_a, o_b]
            w_out = even_w_out[j]
        else:
            w_in = jnp.pad(odd_w_in[j], ((0, 0), (0, GLA_RANK_PAD - rank))).astype(BF16)
            w_alpha = jnp.pad(gla_w_alpha[j], ((0, GLA_RANK_PAD - rank), (0, 0))).astype(BF16)
            proj = _inproj(xf, attn_norm[layer], w_in)
            mix = [_gla(proj, w_alpha, gla_b_alpha[j], gla_o_norm[j], batch, seq)]
            w_out = odd_w_out[j]
        x2, meta, cnt = _outproj_router(mix, w_out.astype(BF16), xf, ffn_norm[layer], moe_w_group[layer],
                                        moe_b_group[layer], moe_w_router[layer], moe_b_router[layer])
        w_gu = jnp.concatenate([moe_w_gate[layer], moe_w_up[layer]], axis=-1).astype(BF16)
        xf = _moe(x2, meta, cnt, ffn_norm[layer], w_gu, moe_w_down[layer].astype(BF16))
    return xf.reshape(batch, seq, d)
```

```python
import functools
import math

import numpy as np
import jax
import jax.numpy as jnp
from jax import lax
from jax.experimental import pallas as pl
from jax.experimental.pallas import tpu as pltpu

F32 = jnp.float32
BF16 = jnp.bfloat16
EPS = 1e-6
LOG2_E = 1.4426950408889634

D_MODEL = 1024
HEADS = 4
DK = 128
HG_WIDTH = 512
DA_HEAD_DIM = 64
GLA_DV = 256
GLA_RANK_PAD = 128
GLA_TAU = 16.0
N_GROUPS = 4
EXPERTS_PER_GROUP = 4
N_EXPERTS = 16
D_EXPERT = 512
N_PAIRS = 6
N_BUCKETS = N_GROUPS * N_PAIRS
ROUTER_ROWS = 32

CHUNK = 128
LEVELS = 7
ATTN_BLOCK = 512
ROW_TILE = 512
MOE_TILE = 256
VMEM_LIMIT = 56 * 1024 * 1024

PAIR_LO = (0, 0, 0, 1, 1, 2)
PAIR_HI = (1, 2, 3, 2, 3, 3)


def _dot(a, b):
    return jnp.dot(a, b, preferred_element_type=F32)


def _dot_nt(a, b):
    return lax.dot_general(a, b, (((1,), (1,)), ((), ())), preferred_element_type=F32)


def _dot_tn(a, b):
    return lax.dot_general(a, b, (((0,), (0,)), ((), ())), preferred_element_type=F32)


def _rms(x, w):
    return x * lax.rsqrt(jnp.mean(x * x, axis=-1, keepdims=True) + EPS) * w


def _silu(x):
    return x * jax.nn.sigmoid(x)


def _params(*sem):
    return pltpu.CompilerParams(dimension_semantics=sem, vmem_limit_bytes=VMEM_LIMIT)


def _inproj_kernel(x_ref, nw_ref, w_ref, o_ref):
    h = _rms(x_ref[...], nw_ref[...]).astype(BF16)
    o_ref[...] = _dot(h, w_ref[...])


def _inproj(x, n, norm_w, w):
    d = x.shape[1]
    n_out = w.shape[1]
    return pl.pallas_call(
        _inproj_kernel,
        grid=(n // ROW_TILE,),
        in_specs=[pl.BlockSpec((ROW_TILE, d), lambda i: (i, 0)),
                  pl.BlockSpec((1, d), lambda i: (0, 0)),
                  pl.BlockSpec((d, n_out), lambda i: (0, 0))],
        out_specs=pl.BlockSpec((ROW_TILE, n_out), lambda i: (i, 0)),
        out_shape=jax.ShapeDtypeStruct((n, n_out), F32),
        compiler_params=_params("parallel"),
        name="inproj",
    )(x, norm_w.reshape(1, d), w)


def _decay_tables():
    c = CHUNK
    a = np.zeros((LEVELS + 2, c, c), np.float32)
    t = np.arange(c)
    for l in range(LEVELS):
        s = 1 << l
        for r in range(c):
            mid = (r // (2 * s)) * 2 * s + s - 1
            if (r >> l) & 1:
                a[l, r] = (t > mid) & (t <= r)
            else:
                a[l, r] = (t > r) & (t <= mid)
    a[LEVELS] = t[None, :] <= t[:, None]
    a[LEVELS + 1] = t[None, :] > t[:, None]
    a = a.reshape((LEVELS + 2) * c, c)
    a3 = np.concatenate([a, a, a], axis=1)
    i, j = np.meshgrid(t, t, indexing="ij")
    x = np.maximum(i ^ j, 1)
    lvl = np.where(j < i, np.floor(np.log2(x)).astype(np.int32), np.where(i == j, LEVELS, -1))
    return jnp.asarray(a3, BF16), jnp.asarray(lvl, jnp.int32)


def _gla_chunk(q, k, v, g, st_ref, a3, lvl, rowid):
    c = CHUNK
    g_hi = g.astype(BF16)
    r1 = g - g_hi.astype(F32)
    g_mid = r1.astype(BF16)
    g_lo = (r1 - g_mid.astype(F32)).astype(BF16)
    g3 = jnp.concatenate([g_hi, g_mid, g_lo], axis=0)
    f = jnp.exp(_dot(a3, g3))
    sc = jnp.where(lvl == LEVELS, jnp.sum(q * k, axis=-1, keepdims=True), 0.0)
    for l in range(LEVELS):
        is_query = ((rowid >> l) & 1) == 1
        m = (jnp.where(is_query, q, k) * f[l * c:(l + 1) * c]).astype(BF16)
        sc = jnp.where(lvl == l, _dot_nt(m, m), sc)
    qd = (q * f[LEVELS * c:(LEVELS + 1) * c]).astype(BF16)
    kd = (k * f[(LEVELS + 1) * c:(LEVELS + 2) * c]).astype(BF16)
    vb = v.astype(BF16)
    st = st_ref[...]
    o = _dot(sc.astype(BF16), vb) + _dot_nt(qd, st.astype(BF16))
    chunk_decay = f[(LEVELS + 1) * c - 1:(LEVELS + 1) * c, :]
    st_ref[...] = st * chunk_decay + _dot_tn(vb, kd)
    return o


def _hgrn_kernel(q_ref, f_ref, i_ref, g_ref, lb_ref, on_ref, a3_ref, lvl_ref, o_ref, st_ref):
    @pl.when(pl.program_id(1) == 0)
    def _():
        st_ref[...] = jnp.zeros_like(st_ref)

    a3 = a3_ref[...]
    lvl = lvl_ref[...]
    rowid = lax.broadcasted_iota(jnp.int32, (CHUNK, 1), 0)
    for h in range(HEADS):
        sl = slice(h * DK, (h + 1) * DK)
        fl = f_ref[:, sl]
        lb = lb_ref[:, sl]
        log_f = jnp.log(lb + (1.0 - lb) * jax.nn.sigmoid(fl))
        key_in = (1.0 - lb) * jax.nn.sigmoid(-fl)
        q = _silu(q_ref[:, sl]) * (DK ** -0.5)
        o = _gla_chunk(q, key_in, i_ref[:, sl], log_f, st_ref.at[h], a3, lvl, rowid)
        o = _rms(o, on_ref[...])
        o_ref[:, sl] = (o * _silu(g_ref[:, sl])).astype(BF16)


def _hgrn(proj, lb, o_norm, batch, seq):
    n = proj.shape[0]
    nt = seq // CHUNK
    a3, lvl = _decay_tables()
    col = lambda j: pl.BlockSpec((CHUNK, HG_WIDTH), lambda b, t: (b * nt + t, j))
    const = lambda shape: pl.BlockSpec(shape, lambda b, t: (0, 0))
    return pl.pallas_call(
        _hgrn_kernel,
        grid=(batch, nt),
        in_specs=[col(0), col(1), col(2), col(3), const((1, HG_WIDTH)), const((1, DK)),
                  const(a3.shape), const(lvl.shape)],
        out_specs=pl.BlockSpec((CHUNK, HG_WIDTH), lambda b, t: (b * nt + t, 0)),
        out_shape=jax.ShapeDtypeStruct((n, HG_WIDTH), BF16),
        scratch_shapes=[pltpu.VMEM((HEADS, DK, DK), F32)],
        compiler_params=_params("parallel", "arbitrary"),
        name="hgrn",
    )(proj, proj, proj, proj, lb.reshape(1, HG_WIDTH), o_norm.reshape(1, DK), a3, lvl)


def _log_sigmoid(x):
    return jnp.minimum(x, 0.0) - jnp.log1p(jnp.exp(-jnp.abs(x)))


def _gla_kernel(q_ref, k_ref, v_ref, g_ref, a_ref, wa_ref, ba_ref, on_ref, a3_ref, lvl_ref, o_ref, st_ref):
    @pl.when(pl.program_id(1) == 0)
    def _():
        st_ref[...] = jnp.zeros_like(st_ref)

    a3 = a3_ref[...]
    lvl = lvl_ref[...]
    rowid = lax.broadcasted_iota(jnp.int32, (CHUNK, 1), 0)
    gate_logit = _dot(a_ref[...].astype(BF16), wa_ref[...]) + ba_ref[...]
    log_alpha = _log_sigmoid(gate_logit) * (1.0 / GLA_TAU)
    for h in range(HEADS):
        sl = slice(h * DK, (h + 1) * DK)
        sv = slice(h * GLA_DV, (h + 1) * GLA_DV)
        q = q_ref[:, sl] * (DK ** -0.5)
        o = _gla_chunk(q, k_ref[:, sl], v_ref[:, sv], log_alpha[:, sl], st_ref.at[h], a3, lvl, rowid)
        o = _rms(o, on_ref[...])
        o_ref[:, sv] = (o * _silu(g_ref[:, sv])).astype(BF16)


def _gla(proj, w_alpha, b_alpha, o_norm, batch, seq):
    n = proj.shape[0]
    nt = seq // CHUNK
    qk_w = HEADS * DK
    v_w = HEADS * GLA_DV
    a3, lvl = _decay_tables()
    row = lambda b, t: b * nt + t
    const = lambda shape: pl.BlockSpec(shape, lambda b, t: (0, 0))
    return pl.pallas_call(
        _gla_kernel,
        grid=(batch, nt),
        in_specs=[pl.BlockSpec((CHUNK, qk_w), lambda b, t: (row(b, t), 0)),
                  pl.BlockSpec((CHUNK, qk_w), lambda b, t: (row(b, t), 1)),
                  pl.BlockSpec((CHUNK, v_w), lambda b, t: (row(b, t), 1)),
                  pl.BlockSpec((CHUNK, v_w), lambda b, t: (row(b, t), 2)),
                  pl.BlockSpec((CHUNK, GLA_RANK_PAD), lambda b, t: (row(b, t), (2 * qk_w + 2 * v_w) // GLA_RANK_PAD)),
                  const((GLA_RANK_PAD, qk_w)), const((1, qk_w)), const((1, GLA_DV)),
                  const(a3.shape), const(lvl.shape)],
        out_specs=pl.BlockSpec((CHUNK, v_w), lambda b, t: (row(b, t), 0)),
        out_shape=jax.ShapeDtypeStruct((n, v_w), BF16),
        scratch_shapes=[pltpu.VMEM((HEADS, GLA_DV, DK), F32)],
        compiler_params=_params("parallel", "arbitrary"),
        name="gla",
    )(proj, proj, proj, proj, proj, w_alpha, b_alpha.reshape(1, qk_w), o_norm.reshape(1, GLA_DV), a3, lvl)


def _qk_norm(x, w2):
    lane = lax.broadcasted_iota(jnp.int32, x.shape, 1)
    first = lane < DA_HEAD_DIM
    sq = x * x
    ms1 = jnp.sum(jnp.where(first, sq, 0.0), axis=-1, keepdims=True) * (1.0 / DA_HEAD_DIM)
    ms2 = jnp.sum(jnp.where(first, 0.0, sq), axis=-1, keepdims=True) * (1.0 / DA_HEAD_DIM)
    r = jnp.where(first, lax.rsqrt(ms1 + EPS), lax.rsqrt(ms2 + EPS))
    return x * r * w2, first


def _attn_kernel(qi_tab, ki_tab, q_ref, k_ref, v_ref, qn_ref, kn_ref, lam_ref, sub_ref, o_ref,
                 qs_ref, m_ref, l_ref, acc_ref, *, lam_init):
    p = pl.program_id(2)
    qi = qi_tab[p]
    ki = ki_tab[p]
    tq = ATTN_BLOCK

    @pl.when(ki == 0)
    def _():
        qn, first = _qk_norm(q_ref[...], qn_ref[...])
        qn = qn * (DA_HEAD_DIM ** -0.5 * LOG2_E)
        qs_ref[:, :tq] = jnp.where(first, qn, 0.0).T.astype(BF16)
        qs_ref[:, tq:] = jnp.where(first, 0.0, qn).T.astype(BF16)
        m_ref[...] = jnp.full_like(m_ref, -jnp.inf)
        l_ref[...] = jnp.zeros_like(l_ref)
        acc_ref[...] = jnp.zeros_like(acc_ref)

    def step(masked):
        kn, _ = _qk_norm(k_ref[...], kn_ref[...])
        s = _dot(kn.astype(BF16), qs_ref[...])
        if masked:
            key = lax.broadcasted_iota(jnp.int32, s.shape, 0)
            qry = lax.broadcasted_iota(jnp.int32, s.shape, 1)
            qry = jnp.where(qry >= tq, qry - tq, qry)
            s = jnp.where(key <= qry, s, -jnp.inf)
        m_prev = m_ref[...]
        m_new = jnp.maximum(m_prev, jnp.max(s, axis=0, keepdims=True))
        alpha = jnp.exp2(m_prev - m_new)
        pr = jnp.exp2(s - m_new)
        l_ref[...] = alpha * l_ref[...] + jnp.sum(pr, axis=0, keepdims=True)
        acc_ref[...] = alpha * acc_ref[...] + _dot_tn(v_ref[...].astype(BF16), pr.astype(BF16))
        m_ref[...] = m_new

    @pl.when(ki < qi)
    def _():
        step(False)

    @pl.when(ki == qi)
    def _():
        step(True)
        lv = lam_ref[...]
        lam = (jnp.exp(jnp.sum(lv[0:1] * lv[1:2], axis=-1, keepdims=True))
               - jnp.exp(jnp.sum(lv[2:3] * lv[3:4], axis=-1, keepdims=True)) + lam_init)
        o1 = acc_ref[:, :tq] / l_ref[:, :tq]
        o2 = acc_ref[:, tq:] / l_ref[:, tq:]
        o = o1 - lam * o2
        o = o * lax.rsqrt(jnp.mean(o * o, axis=0, keepdims=True) + EPS) * sub_ref[...] * (1.0 - lam_init)
        o_ref[...] = o.T.astype(BF16)


def _diff_attn(proj, q_norm, k_norm, lam_vecs, subln, lam_init, batch, seq):
    n = proj.shape[0]
    blk = ATTN_BLOCK
    nq = seq // blk
    dv = 2 * DA_HEAD_DIM
    pairs = [(qi, ki) for qi in range(nq) for ki in range(qi + 1)]
    qi_tab = jnp.asarray([p[0] for p in pairs], jnp.int32)
    ki_tab = jnp.asarray([p[1] for p in pairs], jnp.int32)
    q_col = (4 * HG_WIDTH) // dv
    const = lambda shape: pl.BlockSpec(shape, lambda b, h, p, qt, kt: (0, 0))
    grid_spec = pltpu.PrefetchScalarGridSpec(
        num_scalar_prefetch=2,
        grid=(batch, HEADS, len(pairs)),
        in_specs=[pl.BlockSpec((blk, dv), lambda b, h, p, qt, kt: (b * nq + qt[p], q_col + h)),
                  pl.BlockSpec((blk, dv), lambda b, h, p, qt, kt: (b * nq + kt[p], q_col + HEADS + h)),
                  pl.BlockSpec((blk, dv), lambda b, h, p, qt, kt: (b * nq + kt[p], q_col + 2 * HEADS + h)),
                  const((1, dv)), const((1, dv)), const((4, DA_HEAD_DIM)), const((dv, 1))],
        out_specs=pl.BlockSpec((blk, dv), lambda b, h, p, qt, kt: (b * nq + qt[p], h)),
        scratch_shapes=[pltpu.VMEM((dv, 2 * blk), BF16),
                        pltpu.VMEM((1, 2 * blk), F32),
                        pltpu.VMEM((1, 2 * blk), F32),
                        pltpu.VMEM((dv, 2 * blk), F32)])
    return pl.pallas_call(
        functools.partial(_attn_kernel, lam_init=lam_init),
        grid_spec=grid_spec,
        out_shape=jax.ShapeDtypeStruct((n, HEADS * dv), BF16),
        compiler_params=_params("parallel", "parallel", "arbitrary"),
        name="diff_attn",
    )(qi_tab, ki_tab, proj, proj, proj,
      jnp.tile(q_norm, 2).reshape(1, -1), jnp.tile(k_norm, 2).reshape(1, -1),
      lam_vecs, subln.reshape(-1, 1))


def _outproj_router_kernel(*refs, n_a):
    a_refs = refs[:n_a]
    (w_ref, x_ref, nw_ref, wr_hi_ref, wr_lo_ref, br_ref, tri_ref,
     x2_ref, meta_ref, cnt_ref, carry_ref) = refs[n_a:]

    @pl.when(pl.program_id(0) == 0)
    def _():
        carry_ref[...] = jnp.zeros_like(carry_ref)

    x2 = x_ref[...]
    off = 0
    for a_ref in a_refs:
        kk = a_ref.shape[1]
        x2 = x2 + _dot(a_ref[...], w_ref[off:off + kk, :])
        off += kk
    x2_ref[...] = x2

    h = _rms(x2, nw_ref[...])
    h_hi = h.astype(BF16)
    h_lo = (h - h_hi.astype(F32)).astype(BF16)
    wr_hi = wr_hi_ref[...]
    lt = _dot_nt(wr_hi, h_hi) + _dot_nt(wr_hi, h_lo) + _dot_nt(wr_lo_ref[...], h_hi) + br_ref[...]
    row = lambda r: lt[r:r + 1, :]

    gl = [row(g) for g in range(N_GROUPS)]
    gmax = functools.reduce(jnp.maximum, gl)
    gsel = jnp.where(gl[0] == gmax, 0.0, jnp.where(gl[1] == gmax, 1.0, jnp.where(gl[2] == gmax, 2.0, 3.0)))
    g_w = 1.0 / functools.reduce(jnp.add, [jnp.exp(x - gmax) for x in gl])

    el = []
    for e in range(EXPERTS_PER_GROUP):
        v = row(N_GROUPS + e)
        for g in range(1, N_GROUPS):
            v = jnp.where(gsel == float(g), row(N_GROUPS + EXPERTS_PER_GROUP * g + e), v)
        el.append(v)
    emax = functools.reduce(jnp.maximum, el)
    sel = []
    for e in range(EXPERTS_PER_GROUP):
        ahead = jnp.zeros_like(emax)
        for o in range(EXPERTS_PER_GROUP):
            if o < e:
                ahead = ahead + jnp.where(el[o] >= el[e], 1.0, 0.0)
            elif o > e:
                ahead = ahead + jnp.where(el[o] > el[e], 1.0, 0.0)
        sel.append(ahead < 2.0)
    pe = [jnp.where(sel[e], jnp.exp(el[e] - emax), 0.0) for e in range(EXPERTS_PER_GROUP)]
    scale = g_w / functools.reduce(jnp.add, pe)
    zero = jnp.zeros_like(emax)
    seen = zero
    lo_idx, hi_idx, c_lo, c_hi = zero, zero, zero, zero
    for e in range(EXPERTS_PER_GROUP):
        is_lo = sel[e] & (seen == 0.0)
        is_hi = sel[e] & (seen == 1.0)
        lo_idx = jnp.where(is_lo, float(e), lo_idx)
        hi_idx = jnp.where(is_hi, float(e), hi_idx)
        c_lo = jnp.where(is_lo, pe[e] * scale, c_lo)
        c_hi = jnp.where(is_hi, pe[e] * scale, c_hi)
        seen = seen + jnp.where(sel[e], 1.0, 0.0)
    pair = lo_idx * (7.0 - lo_idx) * 0.5 + (hi_idx - lo_idx - 1.0)
    bucket = gsel * float(N_PAIRS) + pair

    bid = lax.broadcasted_iota(jnp.int32, (ROUTER_ROWS, bucket.shape[1]), 0).astype(F32)
    onehot = jnp.where(bid == bucket, 1.0, 0.0)
    before = _dot(onehot.astype(BF16), tri_ref[...]) + carry_ref[...]
    rank = jnp.sum(onehot * before, axis=0, keepdims=True)
    carry = carry_ref[...] + jnp.sum(onehot, axis=1, keepdims=True)
    carry_ref[...] = carry
    cnt_ref[...] = carry
    meta_ref[...] = jnp.concatenate([bucket, rank, c_lo, c_hi, zero, zero, zero, zero], axis=0)


def _outproj_router(a_list, w_out, x, n, norm_w, w_group, b_group, w_router, b_router):
    d = x.shape[1]
    tm = ROW_TILE
    wr = jnp.concatenate([w_group, w_router.transpose(1, 0, 2).reshape(d, N_EXPERTS)], axis=1)
    wr_t = jnp.zeros((ROUTER_ROWS, d), F32).at[:wr.shape[1]].set(wr.T)
    wr_hi = wr_t.astype(BF16)
    wr_lo = (wr_t - wr_hi.astype(F32)).astype(BF16)
    br = jnp.zeros((ROUTER_ROWS, 1), F32).at[:wr.shape[1], 0].set(
        jnp.concatenate([b_group, b_router.reshape(-1)]))
    tri = jnp.asarray(np.triu(np.ones((tm, tm), np.float32), 1), BF16)
    const = lambda shape: pl.BlockSpec(shape, lambda i: (0, 0))
    in_specs = [pl.BlockSpec((tm, a.shape[1]), lambda i: (i, 0)) for a in a_list]
    in_specs += [const((d, d)), pl.BlockSpec((tm, d), lambda i: (i, 0)), const((1, d)),
                 const((ROUTER_ROWS, d)), const((ROUTER_ROWS, d)), const((ROUTER_ROWS, 1)), const((tm, tm))]
    return pl.pallas_call(
        functools.partial(_outproj_router_kernel, n_a=len(a_list)),
        grid=(n // tm,),
        in_specs=in_specs,
        out_specs=[pl.BlockSpec((tm, d), lambda i: (i, 0)),
                   pl.BlockSpec((8, tm), lambda i: (0, i)),
                   const((ROUTER_ROWS, 1))],
        out_shape=[jax.ShapeDtypeStruct((n, d), F32),
                   jax.ShapeDtypeStruct((8, n), F32),
                   jax.ShapeDtypeStruct((ROUTER_ROWS, 1), F32)],
        scratch_shapes=[pltpu.VMEM((ROUTER_ROWS, 1), F32)],
        compiler_params=_params("arbitrary"),
        name="outproj_router",
    )(*a_list, w_out, x, norm_w.reshape(1, d), wr_hi, wr_lo, br, tri)


def _moe_kernel(e_lo, e_hi, n_tiles, src, dst, x_hbm, cw_ref, nw_ref, gu_lo_ref, gu_hi_ref, d_lo_ref, d_hi_ref,
                o_hbm, xbuf, obuf, gsem, ssem):
    del e_lo, e_hi
    tm = MOE_TILE
    t = pl.program_id(0)
    nt = n_tiles[0]
    slot = t % 2

    def gather_row(tile, buf, r):
        return pltpu.make_async_copy(x_hbm.at[pl.ds(src[tile * tm + r], 1)], xbuf.at[buf, pl.ds(r, 1)], gsem.at[buf])

    def gather_tile(buf):
        return pltpu.make_async_copy(x_hbm.at[pl.ds(0, tm)], xbuf.at[buf], gsem.at[buf])

    def scatter_row(tile, buf, r):
        return pltpu.make_async_copy(obuf.at[buf, pl.ds(r, 1)], o_hbm.at[pl.ds(dst[tile * tm + r], 1)], ssem.at[buf])

    def scatter_tile(buf):
        return pltpu.make_async_copy(obuf.at[buf], o_hbm.at[pl.ds(0, tm)], ssem.at[buf])

    def start_rows(make_row, tile, buf):
        def body(r, carry):
            make_row(tile, buf, r).start()
            return carry
        lax.fori_loop(0, tm, body, 0, unroll=8)

    @pl.when(t == 0)
    def _():
        n_tok = o_hbm.shape[0] - 2 * tm
        obuf[...] = jnp.zeros_like(obuf)
        spare = [pltpu.make_async_copy(obuf.at[b], o_hbm.at[pl.ds(n_tok + b * tm, tm)], ssem.at[b]) for b in range(2)]
        for c in spare:
            c.start()
        for c in spare:
            c.wait()
        start_rows(gather_row, 0, 0)

    @pl.when(t + 1 < nt)
    def _():
        start_rows(gather_row, t + 1, 1 - slot)

    @pl.when(t < nt)
    def _():
        gather_tile(slot).wait()
        x = xbuf[slot]
        h = _rms(x, nw_ref[...]).astype(BF16)
        cw = cw_ref[...]

        def expert(gu_ref, d_ref):
            gu = _dot(h, gu_ref[0])
            act = _silu(gu[:, :D_EXPERT]) * gu[:, D_EXPERT:]
            return _dot(act.astype(BF16), d_ref[0])

        y = cw[:, 1:2] * expert(gu_lo_ref, d_lo_ref) + cw[:, 2:3] * expert(gu_hi_ref, d_hi_ref)

        @pl.when(t >= 2)
        def _():
            scatter_tile(slot).wait()

        obuf[slot] = x + y
        start_rows(scatter_row, t, slot)

    @pl.when(t == pl.num_programs(0) - 1)
    def _():
        scatter_tile(0).wait()
        scatter_tile(1).wait()


def _moe(x2, n, meta, cnt, norm_w, w_gu, w_down):
    d = x2.shape[1]
    tm = MOE_TILE
    max_tiles = n // tm + N_BUCKETS
    n_slots = max_tiles * tm
    counts = cnt[:N_BUCKETS, 0].astype(jnp.int32)
    tiles_per_bucket = (counts + tm - 1) // tm
    tile_end = jnp.cumsum(tiles_per_bucket)
    row_start = (tile_end - tiles_per_bucket) * tm
    n_tiles = tile_end[-1]
    in_bucket = meta[0][:, None] == jnp.arange(N_BUCKETS, dtype=F32)[None, :]
    dest = meta[1].astype(jnp.int32) + jnp.sum(jnp.where(in_bucket, row_start[None, :], 0), axis=1)
    packed = jnp.stack([jnp.arange(n, dtype=F32), meta[2], meta[3], jnp.ones((n,), F32)], axis=1)
    slots = jnp.zeros((n_slots, 4), F32).at[dest].set(packed, unique_indices=True, mode="promise_in_bounds")
    src = slots[:, 0].astype(jnp.int32)
    sidx = jnp.arange(n_slots, dtype=jnp.int32)
    spare = n + ((sidx // tm) % 2) * tm + sidx % tm
    dst = jnp.where(slots[:, 3] > 0.0, src, spare)

    tile_ids = jnp.minimum(jnp.arange(max_tiles, dtype=jnp.int32), n_tiles - 1)
    tile_bucket = jnp.sum((tile_ids[:, None] >= tile_end[None, :]).astype(jnp.int32), axis=1)
    group = tile_bucket // N_PAIRS
    pair = tile_bucket % N_PAIRS
    pair_lo = jnp.sum(jnp.where(pair[:, None] == jnp.arange(N_PAIRS)[None, :], jnp.asarray(PAIR_LO, jnp.int32)[None, :], 0), axis=1)
    pair_hi = jnp.sum(jnp.where(pair[:, None] == jnp.arange(N_PAIRS)[None, :], jnp.asarray(PAIR_HI, jnp.int32)[None, :], 0), axis=1)
    e_lo = group * EXPERTS_PER_GROUP + pair_lo
    e_hi = group * EXPERTS_PER_GROUP + pair_hi

    last = lambda t, nt: jnp.minimum(t, nt[0] - 1)
    grid_spec = pltpu.PrefetchScalarGridSpec(
        num_scalar_prefetch=5,
        grid=(max_tiles,),
        in_specs=[pl.BlockSpec(memory_space=pl.ANY),
                  pl.BlockSpec((tm, 4), lambda t, lo, hi, nt, s, ds: (last(t, nt), 0)),
                  pl.BlockSpec((1, d), lambda t, lo, hi, nt, s, ds: (0, 0)),
                  pl.BlockSpec((1, d, 2 * D_EXPERT), lambda t, lo, hi, nt, s, ds: (lo[t], 0, 0)),
                  pl.BlockSpec((1, d, 2 * D_EXPERT), lambda t, lo, hi, nt, s, ds: (hi[t], 0, 0)),
                  pl.BlockSpec((1, D_EXPERT, d), lambda t, lo, hi, nt, s, ds: (lo[t], 0, 0)),
                  pl.BlockSpec((1, D_EXPERT, d), lambda t, lo, hi, nt, s, ds: (hi[t], 0, 0))],
        out_specs=pl.BlockSpec(memory_space=pl.ANY),
        scratch_shapes=[pltpu.VMEM((2, tm, d), F32), pltpu.VMEM((2, tm, d), F32),
                        pltpu.SemaphoreType.DMA((2,)), pltpu.SemaphoreType.DMA((2,))])
    return pl.pallas_call(
        _moe_kernel,
        grid_spec=grid_spec,
        out_shape=jax.ShapeDtypeStruct((n + 2 * tm, d), F32),
        compiler_params=_params("arbitrary"),
        name="moe",
    )(e_lo, e_hi, n_tiles.reshape(1), src, dst, x2, slots, norm_w.reshape(1, d), w_gu, w_gu, w_down, w_down)


def kernel(x, attn_norm, ffn_norm, even_w_in, even_w_out, hg_lb_logits, hg_o_norm, da_q_norm, da_k_norm, da_lambda, da_subln, odd_w_in, gla_w_alpha, gla_b_alpha, gla_o_norm, odd_w_out, moe_w_group, moe_b_group, moe_w_router, moe_b_router, moe_w_gate, moe_w_up, moe_w_down):
    batch, seq, d = x.shape
    depth = attn_norm.shape[0]
    n = batch * seq
    xf = x.reshape(n, d)

    lb_sm = jax.nn.softmax(hg_lb_logits.astype(F32), axis=0)
    lb_all = jnp.cumsum(lb_sm, axis=0) - lb_sm[0:1]
    rank = gla_w_alpha.shape[1]

    for layer in range(depth):
        j = layer // 2
        if layer % 2 == 0:
            lam_init = 0.8 - 0.6 * math.exp(-0.3 * layer)
            proj = _inproj(xf, n, attn_norm[layer], even_w_in[j].astype(BF16))
            o_a = _hgrn(proj, lb_all[j], hg_o_norm[j], batch, seq)
            o_b = _diff_attn(proj, da_q_norm[j], da_k_norm[j], da_lambda[j], da_subln[j], lam_init, batch, seq)
            mix = [o_a, o_b]
            w_out = even_w_out[j]
        else:
            w_in = jnp.pad(odd_w_in[j], ((0, 0), (0, GLA_RANK_PAD - rank))).astype(BF16)
            w_alpha = jnp.pad(gla_w_alpha[j], ((0, GLA_RANK_PAD - rank), (0, 0))).astype(BF16)
            proj = _inproj(xf, n, attn_norm[layer], w_in)
            mix = [_gla(proj, w_alpha, gla_b_alpha[j], gla_o_norm[j], batch, seq)]
            w_out = odd_w_out[j]
        x2, meta, cnt = _outproj_router(mix, w_out.astype(BF16), xf, n, ffn_norm[layer], moe_w_group[layer],
                                        moe_b_group[layer], moe_w_router[layer], moe_b_router[layer])
        w_gu = jnp.concatenate([moe_w_gate[layer], moe_w_up[layer]], axis=-1).astype(BF16)
        xf = _moe(x2, n, meta, cnt, ffn_norm[layer], w_gu, moe_w_down[layer].astype(BF16))
    return xf[:n].reshape(batch, seq, d)
```

```python
import functools
import math

import numpy as np
import jax
import jax.numpy as jnp
from jax import lax
from jax.experimental import pallas as pl
from jax.experimental.pallas import tpu as pltpu

F32 = jnp.float32
BF16 = jnp.bfloat16
EPS = 1e-6
LOG2_E = 1.4426950408889634

D_MODEL = 1024
HEADS = 4
DK = 128
HG_WIDTH = 512
DA_HEAD_DIM = 64
GLA_DV = 256
GLA_RANK_PAD = 128
GLA_TAU = 16.0
N_GROUPS = 4
EXPERTS_PER_GROUP = 4
N_EXPERTS = 16
D_EXPERT = 512
N_PAIRS = 6
N_BUCKETS = N_GROUPS * N_PAIRS
ROUTER_ROWS = 32

CHUNK = 128
LEVELS = 7
ATTN_BLOCK = 512
ROW_TILE = 512
MOE_TILE = 256
VMEM_LIMIT = 56 * 1024 * 1024

PAIR_LO = (0, 0, 0, 1, 1, 2)
PAIR_HI = (1, 2, 3, 2, 3, 3)


def _dot(a, b):
    return jnp.dot(a, b, preferred_element_type=F32)


def _dot_nt(a, b):
    return lax.dot_general(a, b, (((1,), (1,)), ((), ())), preferred_element_type=F32)


def _dot_tn(a, b):
    return lax.dot_general(a, b, (((0,), (0,)), ((), ())), preferred_element_type=F32)


def _rms(x, w):
    return x * lax.rsqrt(jnp.mean(x * x, axis=-1, keepdims=True) + EPS) * w


def _silu(x):
    return x * jax.nn.sigmoid(x)


def _params(*sem):
    return pltpu.CompilerParams(dimension_semantics=sem, vmem_limit_bytes=VMEM_LIMIT)


def _inproj_kernel(x_ref, nw_ref, w_ref, o_ref):
    h = _rms(x_ref[...], nw_ref[...]).astype(BF16)
    o_ref[...] = _dot(h, w_ref[...])


def _inproj(x, n, norm_w, w):
    d = x.shape[1]
    n_out = w.shape[1]
    return pl.pallas_call(
        _inproj_kernel,
        grid=(n // ROW_TILE,),
        in_specs=[pl.BlockSpec((ROW_TILE, d), lambda i: (i, 0)),
                  pl.BlockSpec((1, d), lambda i: (0, 0)),
                  pl.BlockSpec((d, n_out), lambda i: (0, 0))],
        out_specs=pl.BlockSpec((ROW_TILE, n_out), lambda i: (i, 0)),
        out_shape=jax.ShapeDtypeStruct((n, n_out), F32),
        compiler_params=_params("parallel"),
        name="inproj",
    )(x, norm_w.reshape(1, d), w)


def _decay_tables():
    c = CHUNK
    a = np.zeros((LEVELS + 2, c, c), np.float32)
    t = np.arange(c)
    for l in range(LEVELS):
        s = 1 << l
        for r in range(c):
            mid = (r // (2 * s)) * 2 * s + s - 1
            if (r >> l) & 1:
                a[l, r] = (t > mid) & (t <= r)
            else:
                a[l, r] = (t > r) & (t <= mid)
    a[LEVELS] = t[None, :] <= t[:, None]
    a[LEVELS + 1] = t[None, :] > t[:, None]
    a = a.reshape((LEVELS + 2) * c, c)
    a3 = np.concatenate([a, a, a], axis=1)
    i, j = np.meshgrid(t, t, indexing="ij")
    x = np.maximum(i ^ j, 1)
    lvl = np.where(j < i, np.floor(np.log2(x)).astype(np.int32), np.where(i == j, LEVELS, -1))
    return jnp.asarray(a3, BF16), jnp.asarray(lvl, jnp.int32)


def _gla_chunk(q, k, v, g, st_ref, a3, lvl, rowid):
    c = CHUNK
    g_hi = g.astype(BF16)
    r1 = g - g_hi.astype(F32)
    g_mid = r1.astype(BF16)
    g_lo = (r1 - g_mid.astype(F32)).astype(BF16)
    g3 = jnp.concatenate([g_hi, g_mid, g_lo], axis=0)
    f = jnp.exp(_dot(a3, g3))
    sc = jnp.where(lvl == LEVELS, jnp.sum(q * k, axis=-1, keepdims=True), 0.0)
    for l in range(LEVELS):
        is_query = ((rowid >> l) & 1) == 1
        m = (jnp.where(is_query, q, k) * f[l * c:(l + 1) * c]).astype(BF16)
        sc = jnp.where(lvl == l, _dot_nt(m, m), sc)
    qd = (q * f[LEVELS * c:(LEVELS + 1) * c]).astype(BF16)
    kd = (k * f[(LEVELS + 1) * c:(LEVELS + 2) * c]).astype(BF16)
    vb = v.astype(BF16)
    st = st_ref[...]
    o = _dot(sc.astype(BF16), vb) + _dot_nt(qd, st.astype(BF16))
    chunk_decay = f[(LEVELS + 1) * c - 1:(LEVELS + 1) * c, :]
    st_ref[...] = st * chunk_decay + _dot_tn(vb, kd)
    return o


def _hgrn_kernel(q_ref, f_ref, i_ref, g_ref, lb_ref, on_ref, a3_ref, lvl_ref, o_ref, st_ref):
    @pl.when(pl.program_id(1) == 0)
    def _():
        st_ref[...] = jnp.zeros_like(st_ref)

    a3 = a3_ref[...]
    lvl = lvl_ref[...]
    rowid = lax.broadcasted_iota(jnp.int32, (CHUNK, 1), 0)
    for h in range(HEADS):
        sl = slice(h * DK, (h + 1) * DK)
        fl = f_ref[:, sl]
        lb = lb_ref[:, sl]
        log_f = jnp.log(lb + (1.0 - lb) * jax.nn.sigmoid(fl))
        key_in = (1.0 - lb) * jax.nn.sigmoid(-fl)
        q = _silu(q_ref[:, sl]) * (DK ** -0.5)
        o = _gla_chunk(q, key_in, i_ref[:, sl], log_f, st_ref.at[h], a3, lvl, rowid)
        o = _rms(o, on_ref[...])
        o_ref[:, sl] = (o * _silu(g_ref[:, sl])).astype(BF16)


def _hgrn(proj, lb, o_norm, batch, seq):
    n = proj.shape[0]
    nt = seq // CHUNK
    a3, lvl = _decay_tables()
    col = lambda j: pl.BlockSpec((CHUNK, HG_WIDTH), lambda b, t: (b * nt + t, j))
    const = lambda shape: pl.BlockSpec(shape, lambda b, t: (0, 0))
    return pl.pallas_call(
        _hgrn_kernel,
        grid=(batch, nt),
        in_specs=[col(0), col(1), col(2), col(3), const((1, HG_WIDTH)), const((1, DK)),
                  const(a3.shape), const(lvl.shape)],
        out_specs=pl.BlockSpec((CHUNK, HG_WIDTH), lambda b, t: (b * nt + t, 0)),
        out_shape=jax.ShapeDtypeStruct((n, HG_WIDTH), BF16),
        scratch_shapes=[pltpu.VMEM((HEADS, DK, DK), F32)],
        compiler_params=_params("parallel", "arbitrary"),
        name="hgrn",
    )(proj, proj, proj, proj, lb.reshape(1, HG_WIDTH), o_norm.reshape(1, DK), a3, lvl)


def _log_sigmoid(x):
    return jnp.minimum(x, 0.0) - jnp.log1p(jnp.exp(-jnp.abs(x)))


def _gla_kernel(q_ref, k_ref, v_ref, g_ref, a_ref, wa_ref, ba_ref, on_ref, a3_ref, lvl_ref, o_ref, st_ref):
    @pl.when(pl.program_id(1) == 0)
    def _():
        st_ref[...] = jnp.zeros_like(st_ref)

    a3 = a3_ref[...]
    lvl = lvl_ref[...]
    rowid = lax.broadcasted_iota(jnp.int32, (CHUNK, 1), 0)
    gate_logit = _dot(a_ref[...].astype(BF16), wa_ref[...]) + ba_ref[...]
    log_alpha = _log_sigmoid(gate_logit) * (1.0 / GLA_TAU)
    for h in range(HEADS):
        sl = slice(h * DK, (h + 1) * DK)
        sv = slice(h * GLA_DV, (h + 1) * GLA_DV)
        q = q_ref[:, sl] * (DK ** -0.5)
        o = _gla_chunk(q, k_ref[:, sl], v_ref[:, sv], log_alpha[:, sl], st_ref.at[h], a3, lvl, rowid)
        o = _rms(o, on_ref[...])
        o_ref[:, sv] = (o * _silu(g_ref[:, sv])).astype(BF16)


def _gla(proj, w_alpha, b_alpha, o_norm, batch, seq):
    n = proj.shape[0]
    nt = seq // CHUNK
    qk_w = HEADS * DK
    v_w = HEADS * GLA_DV
    a3, lvl = _decay_tables()
    row = lambda b, t: b * nt + t
    const = lambda shape: pl.BlockSpec(shape, lambda b, t: (0, 0))
    return pl.pallas_call(
        _gla_kernel,
        grid=(batch, nt),
        in_specs=[pl.BlockSpec((CHUNK, qk_w), lambda b, t: (row(b, t), 0)),
                  pl.BlockSpec((CHUNK, qk_w), lambda b, t: (row(b, t), 1)),
                  pl.BlockSpec((CHUNK, v_w), lambda b, t: (row(b, t), 1)),
                  pl.BlockSpec((CHUNK, v_w), lambda b, t: (row(b, t), 2)),
                  pl.BlockSpec((CHUNK, GLA_RANK_PAD), lambda b, t: (row(b, t), (2 * qk_w + 2 * v_w) // GLA_RANK_PAD)),
                  const((GLA_RANK_PAD, qk_w)), const((1, qk_w)), const((1, GLA_DV)),
                  const(a3.shape), const(lvl.shape)],
        out_specs=pl.BlockSpec((CHUNK, v_w), lambda b, t: (row(b, t), 0)),
        out_shape=jax.ShapeDtypeStruct((n, v_w), BF16),
        scratch_shapes=[pltpu.VMEM((HEADS, GLA_DV, DK), F32)],
        compiler_params=_params("parallel", "arbitrary"),
        name="gla",
    )(proj, proj, proj, proj, proj, w_alpha, b_alpha.reshape(1, qk_w), o_norm.reshape(1, GLA_DV), a3, lvl)


def _qk_norm(x, w2):
    lane = lax.broadcasted_iota(jnp.int32, x.shape, 1)
    first = lane < DA_HEAD_DIM
    sq = x * x
    ms1 = jnp.sum(jnp.where(first, sq, 0.0), axis=-1, keepdims=True) * (1.0 / DA_HEAD_DIM)
    ms2 = jnp.sum(jnp.where(first, 0.0, sq), axis=-1, keepdims=True) * (1.0 / DA_HEAD_DIM)
    r = jnp.where(first, lax.rsqrt(ms1 + EPS), lax.rsqrt(ms2 + EPS))
    return x * r * w2, first


def _attn_kernel(qi_tab, ki_tab, q_ref, k_ref, v_ref, qn_ref, kn_ref, lam_ref, sub_ref, o_ref,
                 qs_ref, m_ref, l_ref, acc_ref, *, lam_init):
    p = pl.program_id(2)
    qi = qi_tab[p]
    ki = ki_tab[p]
    tq = ATTN_BLOCK

    @pl.when(ki == 0)
    def _():
        qn, first = _qk_norm(q_ref[...], qn_ref[...])
        qn = qn * (DA_HEAD_DIM ** -0.5 * LOG2_E)
        qs_ref[:, :tq] = jnp.where(first, qn, 0.0).T.astype(BF16)
        qs_ref[:, tq:] = jnp.where(first, 0.0, qn).T.astype(BF16)
        m_ref[...] = jnp.full_like(m_ref, -jnp.inf)
        l_ref[...] = jnp.zeros_like(l_ref)
        acc_ref[...] = jnp.zeros_like(acc_ref)

    def step(masked):
        kn, _ = _qk_norm(k_ref[...], kn_ref[...])
        s = _dot(kn.astype(BF16), qs_ref[...])
        if masked:
            key = lax.broadcasted_iota(jnp.int32, s.shape, 0)
            qry = lax.broadcasted_iota(jnp.int32, s.shape, 1)
            qry = jnp.where(qry >= tq, qry - tq, qry)
            s = jnp.where(key <= qry, s, -jnp.inf)
        m_prev = m_ref[...]
        m_new = jnp.maximum(m_prev, jnp.max(s, axis=0, keepdims=True))
        alpha = jnp.exp2(m_prev - m_new)
        pr = jnp.exp2(s - m_new)
        l_ref[...] = alpha * l_ref[...] + jnp.sum(pr, axis=0, keepdims=True)
        acc_ref[...] = alpha * acc_ref[...] + _dot_tn(v_ref[...].astype(BF16), pr.astype(BF16))
        m_ref[...] = m_new

    @pl.when(ki < qi)
    def _():
        step(False)

    @pl.when(ki == qi)
    def _():
        step(True)
        lv = lam_ref[...]
        lam = (jnp.exp(jnp.sum(lv[0:1] * lv[1:2], axis=-1, keepdims=True))
               - jnp.exp(jnp.sum(lv[2:3] * lv[3:4], axis=-1, keepdims=True)) + lam_init)
        o1 = acc_ref[:, :tq] / l_ref[:, :tq]
        o2 = acc_ref[:, tq:] / l_ref[:, tq:]
        o = o1 - lam * o2
        o = o * lax.rsqrt(jnp.mean(o * o, axis=0, keepdims=True) + EPS) * sub_ref[...] * (1.0 - lam_init)
        o_ref[...] = o.T.astype(BF16)


def _diff_attn(proj, q_norm, k_norm, lam_vecs, subln, lam_init, batch, seq):
    n = proj.shape[0]
    blk = ATTN_BLOCK
    nq = seq // blk
    dv = 2 * DA_HEAD_DIM
    pairs = [(qi, ki) for qi in range(nq) for ki in range(qi + 1)]
    qi_tab = jnp.asarray([p[0] for p in pairs], jnp.int32)
    ki_tab = jnp.asarray([p[1] for p in pairs], jnp.int32)
    q_col = (4 * HG_WIDTH) // dv
    const = lambda shape: pl.BlockSpec(shape, lambda b, h, p, qt, kt: (0, 0))
    grid_spec = pltpu.PrefetchScalarGridSpec(
        num_scalar_prefetch=2,
        grid=(batch, HEADS, len(pairs)),
        in_specs=[pl.BlockSpec((blk, dv), lambda b, h, p, qt, kt: (b * nq + qt[p], q_col + h)),
                  pl.BlockSpec((blk, dv), lambda b, h, p, qt, kt: (b * nq + kt[p], q_col + HEADS + h)),
                  pl.BlockSpec((blk, dv), lambda b, h, p, qt, kt: (b * nq + kt[p], q_col + 2 * HEADS + h)),
                  const((1, dv)), const((1, dv)), const((4, DA_HEAD_DIM)), const((dv, 1))],
        out_specs=pl.BlockSpec((blk, dv), lambda b, h, p, qt, kt: (b * nq + qt[p], h)),
        scratch_shapes=[pltpu.VMEM((dv, 2 * blk), BF16),
                        pltpu.VMEM((1, 2 * blk), F32),
                        pltpu.VMEM((1, 2 * blk), F32),
                        pltpu.VMEM((dv, 2 * blk), F32)])
    return pl.pallas_call(
        functools.partial(_attn_kernel, lam_init=lam_init),
        grid_spec=grid_spec,
        out_shape=jax.ShapeDtypeStruct((n, HEADS * dv), BF16),
        compiler_params=_params("parallel", "parallel", "arbitrary"),
        name="diff_attn",
    )(qi_tab, ki_tab, proj, proj, proj,
      jnp.tile(q_norm, 2).reshape(1, -1), jnp.tile(k_norm, 2).reshape(1, -1),
      lam_vecs, subln.reshape(-1, 1))


def _outproj_router_kernel(*refs, n_a):
    a_refs = refs[:n_a]
    (w_ref, x_ref, nw_ref, wr_hi_ref, wr_lo_ref, br_ref, tri_ref,
     x2_ref, meta_ref, cnt_ref, carry_ref) = refs[n_a:]

    @pl.when(pl.program_id(0) == 0)
    def _():
        carry_ref[...] = jnp.zeros_like(carry_ref)

    x2 = x_ref[...]
    off = 0
    for a_ref in a_refs:
        kk = a_ref.shape[1]
        x2 = x2 + _dot(a_ref[...], w_ref[off:off + kk, :])
        off += kk
    x2_ref[...] = x2

    h = _rms(x2, nw_ref[...])
    h_hi = h.astype(BF16)
    h_lo = (h - h_hi.astype(F32)).astype(BF16)
    wr_hi = wr_hi_ref[...]
    lt = _dot_nt(wr_hi, h_hi) + _dot_nt(wr_hi, h_lo) + _dot_nt(wr_lo_ref[...], h_hi) + br_ref[...]
    row = lambda r: lt[r:r + 1, :]

    gl = [row(g) for g in range(N_GROUPS)]
    gmax = functools.reduce(jnp.maximum, gl)
    gsel = jnp.where(gl[0] == gmax, 0.0, jnp.where(gl[1] == gmax, 1.0, jnp.where(gl[2] == gmax, 2.0, 3.0)))
    g_w = 1.0 / functools.reduce(jnp.add, [jnp.exp(x - gmax) for x in gl])

    el = []
    for e in range(EXPERTS_PER_GROUP):
        v = row(N_GROUPS + e)
        for g in range(1, N_GROUPS):
            v = jnp.where(gsel == float(g), row(N_GROUPS + EXPERTS_PER_GROUP * g + e), v)
        el.append(v)
    emax = functools.reduce(jnp.maximum, el)
    sel = []
    for e in range(EXPERTS_PER_GROUP):
        ahead = jnp.zeros_like(emax)
        for o in range(EXPERTS_PER_GROUP):
            if o < e:
                ahead = ahead + jnp.where(el[o] >= el[e], 1.0, 0.0)
            elif o > e:
                ahead = ahead + jnp.where(el[o] > el[e], 1.0, 0.0)
        sel.append(ahead < 2.0)
    pe = [jnp.where(sel[e], jnp.exp(el[e] - emax), 0.0) for e in range(EXPERTS_PER_GROUP)]
    scale = g_w / functools.reduce(jnp.add, pe)
    zero = jnp.zeros_like(emax)
    seen = zero
    lo_idx, hi_idx, c_lo, c_hi = zero, zero, zero, zero
    for e in range(EXPERTS_PER_GROUP):
        is_lo = sel[e] & (seen == 0.0)
        is_hi = sel[e] & (seen == 1.0)
        lo_idx = jnp.where(is_lo, float(e), lo_idx)
        hi_idx = jnp.where(is_hi, float(e), hi_idx)
        c_lo = jnp.where(is_lo, pe[e] * scale, c_lo)
        c_hi = jnp.where(is_hi, pe[e] * scale, c_hi)
        seen = seen + jnp.where(sel[e], 1.0, 0.0)
    pair = lo_idx * (7.0 - lo_idx) * 0.5 + (hi_idx - lo_idx - 1.0)
    bucket = gsel * float(N_PAIRS) + pair

    bid = lax.broadcasted_iota(jnp.int32, (ROUTER_ROWS, bucket.shape[1]), 0).astype(F32)
    onehot = jnp.where(bid == bucket, 1.0, 0.0)
    before = _dot(onehot.astype(BF16), tri_ref[...]) + carry_ref[...]
    rank = jnp.sum(onehot * before, axis=0, keepdims=True)
    carry = carry_ref[...] + jnp.sum(onehot, axis=1, keepdims=True)
    carry_ref[...] = carry
    cnt_ref[...] = carry
    meta_ref[...] = jnp.concatenate([bucket, rank, c_lo, c_hi, zero, zero, zero, zero], axis=0)


def _outproj_router(a_list, w_out, x, n, norm_w, w_group, b_group, w_router, b_router):
    d = x.shape[1]
    tm = ROW_TILE
    wr = jnp.concatenate([w_group, w_router.transpose(1, 0, 2).reshape(d, N_EXPERTS)], axis=1)
    wr_t = jnp.zeros((ROUTER_ROWS, d), F32).at[:wr.shape[1]].set(wr.T)
    wr_hi = wr_t.astype(BF16)
    wr_lo = (wr_t - wr_hi.astype(F32)).astype(BF16)
    br = jnp.zeros((ROUTER_ROWS, 1), F32).at[:wr.shape[1], 0].set(
        jnp.concatenate([b_group, b_router.reshape(-1)]))
    tri = jnp.asarray(np.triu(np.ones((tm, tm), np.float32), 1), BF16)
    const = lambda shape: pl.BlockSpec(shape, lambda i: (0, 0))
    in_specs = [pl.BlockSpec((tm, a.shape[1]), lambda i: (i, 0)) for a in a_list]
    in_specs += [const((d, d)), pl.BlockSpec((tm, d), lambda i: (i, 0)), const((1, d)),
                 const((ROUTER_ROWS, d)), const((ROUTER_ROWS, d)), const((ROUTER_ROWS, 1)), const((tm, tm))]
    return pl.pallas_call(
        functools.partial(_outproj_router_kernel, n_a=len(a_list)),
        grid=(n // tm,),
        in_specs=in_specs,
        out_specs=[pl.BlockSpec((tm, d), lambda i: (i, 0)),
                   pl.BlockSpec((8, tm), lambda i: (0, i)),
                   const((ROUTER_ROWS, 1))],
        out_shape=[jax.ShapeDtypeStruct((n, d), F32),
                   jax.ShapeDtypeStruct((8, n), F32),
                   jax.ShapeDtypeStruct((ROUTER_ROWS, 1), F32)],
        scratch_shapes=[pltpu.VMEM((ROUTER_ROWS, 1), F32)],
        compiler_params=_params("arbitrary"),
        name="outproj_router",
    )(*a_list, w_out, x, norm_w.reshape(1, d), wr_hi, wr_lo, br, tri)


def _moe_kernel(e_lo, e_hi, n_tiles, src, dst, x_hbm, cw_ref, nw_ref, gu_lo_ref, gu_hi_ref, d_lo_ref, d_hi_ref,
                o_hbm, xbuf0, xbuf1, obuf0, obuf1, gsem, ssem):
    del e_lo, e_hi
    tm = MOE_TILE
    t = pl.program_id(0)
    nt = n_tiles[0]
    n_tok = o_hbm.shape[0] - 2 * tm
    xbuf = (xbuf0, xbuf1)
    obuf = (obuf0, obuf1)

    def gather_row(base, buf, r):
        return pltpu.make_async_copy(x_hbm.at[pl.ds(src[base + r], 1)], xbuf[buf].at[pl.ds(r, 1)], gsem.at[buf])

    def gather_tile(buf):
        return pltpu.make_async_copy(x_hbm.at[pl.ds(0, tm)], xbuf[buf], gsem.at[buf])

    def scatter_row(base, buf, r):
        return pltpu.make_async_copy(obuf[buf].at[pl.ds(r, 1)], o_hbm.at[pl.ds(dst[base + r], 1)], ssem.at[buf])

    def scatter_tile(buf):
        return pltpu.make_async_copy(obuf[buf], o_hbm.at[pl.ds(0, tm)], ssem.at[buf])

    def start_rows(make_row, base, buf):
        def body(r, carry):
            make_row(base, buf, r).start()
            return carry
        lax.fori_loop(0, tm, body, 0, unroll=8)

    @pl.when(t == 0)
    def _():
        obuf0[...] = jnp.zeros_like(obuf0)
        obuf1[...] = jnp.zeros_like(obuf1)
        pltpu.make_async_copy(obuf0, o_hbm.at[pl.ds(n_tok, tm)], ssem.at[0]).start()
        start_rows(gather_row, 0, 0)

    def step(buf):
        other = 1 - buf
        gather_tile(buf).wait()
        out_base = t * tm
        for r in range(tm):
            scatter_row(out_base, other, r).start()
        in_base = jnp.minimum(t + 1, nt - 1) * tm
        for r in range(tm):
            gather_row(in_base, other, r).start()
        x = xbuf[buf][...]
        h = _rms(x, nw_ref[...]).astype(BF16)
        cw = cw_ref[...]

        def expert(gu_ref, d_ref):
            gu = _dot(h, gu_ref[0])
            act = _silu(gu[:, :D_EXPERT]) * gu[:, D_EXPERT:]
            return _dot(act.astype(BF16), d_ref[0])

        y = cw[:, 1:2] * expert(gu_lo_ref, d_lo_ref) + cw[:, 2:3] * expert(gu_hi_ref, d_hi_ref)
        scatter_tile(buf).wait()
        obuf[buf][...] = x + y

    for b in range(2):
        @pl.when((t < nt) & (t % 2 == b))
        def _(b=b):
            step(b)

    for b in range(2):
        @pl.when((t == nt) & ((nt - 1) % 2 == b))
        def _(b=b):
            start_rows(scatter_row, t * tm, b)
            gather_tile(1 - b).wait()
            scatter_tile(0).wait()
            scatter_tile(1).wait()


def _moe(x2, n, meta, cnt, norm_w, w_gu, w_down):
    d = x2.shape[1]
    tm = MOE_TILE
    max_tiles = n // tm + N_BUCKETS
    n_slots = max_tiles * tm
    counts = cnt[:N_BUCKETS, 0].astype(jnp.int32)
    tiles_per_bucket = (counts + tm - 1) // tm
    tile_end = jnp.cumsum(tiles_per_bucket)
    row_start = (tile_end - tiles_per_bucket) * tm
    n_tiles = tile_end[-1]
    in_bucket = meta[0][:, None] == jnp.arange(N_BUCKETS, dtype=F32)[None, :]
    dest = meta[1].astype(jnp.int32) + jnp.sum(jnp.where(in_bucket, row_start[None, :], 0), axis=1)
    packed = jnp.stack([jnp.arange(n, dtype=F32), meta[2], meta[3], jnp.ones((n,), F32)], axis=1)
    slots = jnp.zeros((n_slots, 4), F32).at[dest].set(packed, unique_indices=True, mode="promise_in_bounds")
    src = slots[:, 0].astype(jnp.int32)
    sidx = jnp.arange(n_slots, dtype=jnp.int32)
    spare = n + ((sidx // tm) % 2) * tm + sidx % tm
    dst = jnp.where(slots[:, 3] > 0.0, src, spare)
    dst = jnp.concatenate([n + tm + jnp.arange(tm, dtype=jnp.int32), dst])

    tile_ids = jnp.minimum(jnp.arange(max_tiles + 1, dtype=jnp.int32), n_tiles - 1)
    tile_bucket = jnp.sum((tile_ids[:, None] >= tile_end[None, :]).astype(jnp.int32), axis=1)
    group = tile_bucket // N_PAIRS
    pair = tile_bucket % N_PAIRS
    pair_lo = jnp.sum(jnp.where(pair[:, None] == jnp.arange(N_PAIRS)[None, :], jnp.asarray(PAIR_LO, jnp.int32)[None, :], 0), axis=1)
    pair_hi = jnp.sum(jnp.where(pair[:, None] == jnp.arange(N_PAIRS)[None, :], jnp.asarray(PAIR_HI, jnp.int32)[None, :], 0), axis=1)
    e_lo = group * EXPERTS_PER_GROUP + pair_lo
    e_hi = group * EXPERTS_PER_GROUP + pair_hi

    last = lambda t, nt: jnp.minimum(t, nt[0] - 1)
    grid_spec = pltpu.PrefetchScalarGridSpec(
        num_scalar_prefetch=5,
        grid=(max_tiles + 1,),
        in_specs=[pl.BlockSpec(memory_space=pl.ANY),
                  pl.BlockSpec((tm, 4), lambda t, lo, hi, nt, s, ds: (last(t, nt), 0)),
                  pl.BlockSpec((1, d), lambda t, lo, hi, nt, s, ds: (0, 0)),
                  pl.BlockSpec((1, d, 2 * D_EXPERT), lambda t, lo, hi, nt, s, ds: (lo[t], 0, 0)),
                  pl.BlockSpec((1, d, 2 * D_EXPERT), lambda t, lo, hi, nt, s, ds: (hi[t], 0, 0)),
                  pl.BlockSpec((1, D_EXPERT, d), lambda t, lo, hi, nt, s, ds: (lo[t], 0, 0)),
                  pl.BlockSpec((1, D_EXPERT, d), lambda t, lo, hi, nt, s, ds: (hi[t], 0, 0))],
        out_specs=pl.BlockSpec(memory_space=pl.ANY),
        scratch_shapes=[pltpu.VMEM((tm, d), F32), pltpu.VMEM((tm, d), F32),
                        pltpu.VMEM((tm, d), F32), pltpu.VMEM((tm, d), F32),
                        pltpu.SemaphoreType.DMA((2,)), pltpu.SemaphoreType.DMA((2,))])
    return pl.pallas_call(
        _moe_kernel,
        grid_spec=grid_spec,
        out_shape=jax.ShapeDtypeStruct((n + 2 * tm, d), F32),
        compiler_params=_params("arbitrary"),
        name="moe",
    )(e_lo, e_hi, n_tiles.reshape(1), src, dst, x2, slots, norm_w.reshape(1, d), w_gu, w_gu, w_down, w_down)


def kernel(x, attn_norm, ffn_norm, even_w_in, even_w_out, hg_lb_logits, hg_o_norm, da_q_norm, da_k_norm, da_lambda, da_subln, odd_w_in, gla_w_alpha, gla_b_alpha, gla_o_norm, odd_w_out, moe_w_group, moe_b_group, moe_w_router, moe_b_router, moe_w_gate, moe_w_up, moe_w_down):
    batch, seq, d = x.shape
    depth = attn_norm.shape[0]
    n = batch * seq
    xf = x.reshape(n, d)

    lb_sm = jax.nn.softmax(hg_lb_logits.astype(F32), axis=0)
    lb_all = jnp.cumsum(lb_sm, axis=0) - lb_sm[0:1]
    rank = gla_w_alpha.shape[1]

    for layer in range(depth):
        j = layer // 2
        if layer % 2 == 0:
            lam_init = 0.8 - 0.6 * math.exp(-0.3 * layer)
            proj = _inproj(xf, n, attn_norm[layer], even_w_in[j].astype(BF16))
            o_a = _hgrn(proj, lb_all[j], hg_o_norm[j], batch, seq)
            o_b = _diff_attn(proj, da_q_norm[j], da_k_norm[j], da_lambda[j], da_subln[j], lam_init, batch, seq)
            mix = [o_a, o_b]
            w_out = even_w_out[j]
        else:
            w_in = jnp.pad(odd_w_in[j], ((0, 0), (0, GLA_RANK_PAD - rank))).astype(BF16)
            w_alpha = jnp.pad(gla_w_alpha[j], ((0, GLA_RANK_PAD - rank), (0, 0))).astype(BF16)
            proj = _inproj(xf, n, attn_norm[layer], w_in)
            mix = [_gla(proj, w_alpha, gla_b_alpha[j], gla_o_norm[j], batch, seq)]
            w_out = odd_w_out[j]
        x2, meta, cnt = _outproj_router(mix, w_out.astype(BF16), xf, n, ffn_norm[layer], moe_w_group[layer],
                                        moe_b_group[layer], moe_w_router[layer], moe_b_router[layer])
        w_gu = jnp.concatenate([moe_w_gate[layer], moe_w_up[layer]], axis=-1).astype(BF16)
        xf = _moe(x2, n, meta, cnt, ffn_norm[layer], w_gu, moe_w_down[layer].astype(BF16))
    return xf[:n].reshape(batch, seq, d)
```

```python
import functools
import math

import numpy as np
import jax
import jax.numpy as jnp
from jax import lax
from jax.experimental import pallas as pl
from jax.experimental.pallas import tpu as pltpu

F32 = jnp.float32
BF16 = jnp.bfloat16
EPS = 1e-6
LOG2_E = 1.4426950408889634

D_MODEL = 1024
LANE = 128
LANE_TILES = D_MODEL // LANE
HEADS = 4
DK = 128
HG_WIDTH = 512
DA_HEAD_DIM = 64
GLA_DV = 256
GLA_RANK_PAD = 128
GLA_TAU = 16.0
N_GROUPS = 4
EXPERTS_PER_GROUP = 4
N_EXPERTS = 16
D_EXPERT = 512
N_PAIRS = 6
N_BUCKETS = N_GROUPS * N_PAIRS
ROUTER_ROWS = 32

CHUNK = 128
LEVELS = 7
ATTN_BLOCK = 512
ROW_TILE = 512
MOE_TILE = 256
VMEM_LIMIT = 56 * 1024 * 1024

PAIR_LO = (0, 0, 0, 1, 1, 2)
PAIR_HI = (1, 2, 3, 2, 3, 3)


def _dot(a, b):
    return jnp.dot(a, b, preferred_element_type=F32)


def _dot_nt(a, b):
    return lax.dot_general(a, b, (((1,), (1,)), ((), ())), preferred_element_type=F32)


def _dot_tn(a, b):
    return lax.dot_general(a, b, (((0,), (0,)), ((), ())), preferred_element_type=F32)


def _rms(x, w):
    return x * lax.rsqrt(jnp.mean(x * x, axis=-1, keepdims=True) + EPS) * w


def _silu(x):
    return x * jax.nn.sigmoid(x)


def _params(*sem):
    return pltpu.CompilerParams(dimension_semantics=sem, vmem_limit_bytes=VMEM_LIMIT)


def _load_tokens(ref):
    rows = ref.shape[0] // LANE_TILES
    return jnp.concatenate([ref[pl.ds(c, rows, stride=LANE_TILES), :] for c in range(LANE_TILES)], axis=1)


def _store_tokens(ref, val):
    rows = ref.shape[0] // LANE_TILES
    for c in range(LANE_TILES):
        ref[pl.ds(c, rows, stride=LANE_TILES), :] = val[:, c * LANE:(c + 1) * LANE]


def _inproj_kernel(x_ref, nw_ref, w_ref, o_ref):
    h = _rms(_load_tokens(x_ref), nw_ref[...]).astype(BF16)
    o_ref[...] = _dot(h, w_ref[...])


def _inproj(x, n, norm_w, w):
    d = D_MODEL
    n_out = w.shape[1]
    return pl.pallas_call(
        _inproj_kernel,
        grid=(n // ROW_TILE,),
        in_specs=[pl.BlockSpec((ROW_TILE * LANE_TILES, LANE), lambda i: (i, 0)),
                  pl.BlockSpec((1, d), lambda i: (0, 0)),
                  pl.BlockSpec((d, n_out), lambda i: (0, 0))],
        out_specs=pl.BlockSpec((ROW_TILE, n_out), lambda i: (i, 0)),
        out_shape=jax.ShapeDtypeStruct((n, n_out), F32),
        compiler_params=_params("parallel"),
        name="inproj",
    )(x, norm_w.reshape(1, d), w)


def _decay_tables():
    c = CHUNK
    a = np.zeros((LEVELS + 2, c, c), np.float32)
    t = np.arange(c)
    for l in range(LEVELS):
        s = 1 << l
        for r in range(c):
            mid = (r // (2 * s)) * 2 * s + s - 1
            if (r >> l) & 1:
                a[l, r] = (t > mid) & (t <= r)
            else:
                a[l, r] = (t > r) & (t <= mid)
    a[LEVELS] = t[None, :] <= t[:, None]
    a[LEVELS + 1] = t[None, :] > t[:, None]
    a = a.reshape((LEVELS + 2) * c, c)
    a3 = np.concatenate([a, a], axis=1)
    i, j = np.meshgrid(t, t, indexing="ij")
    x = np.maximum(i ^ j, 1)
    lvl = np.where(j < i, np.floor(np.log2(x)).astype(np.int32), np.where(i == j, LEVELS, -1))
    return jnp.asarray(a3, BF16), jnp.asarray(lvl, jnp.int32)


def _gla_chunk(q, k, v, g, st_ref, a3, lvl, rowid):
    c = CHUNK
    g_hi = g.astype(BF16)
    g_lo = (g - g_hi.astype(F32)).astype(BF16)
    f = jnp.exp(_dot(a3, jnp.concatenate([g_hi, g_lo], axis=0)))
    sc = jnp.where(lvl == LEVELS, jnp.sum(q * k, axis=-1, keepdims=True), 0.0)
    for l in range(LEVELS):
        is_query = ((rowid >> l) & 1) == 1
        m = (jnp.where(is_query, q, k) * f[l * c:(l + 1) * c]).astype(BF16)
        sc = jnp.where(lvl == l, _dot_nt(m, m), sc)
    qd = (q * f[LEVELS * c:(LEVELS + 1) * c]).astype(BF16)
    kd = (k * f[(LEVELS + 1) * c:(LEVELS + 2) * c]).astype(BF16)
    vb = v.astype(BF16)
    st = st_ref[...]
    o = _dot(sc.astype(BF16), vb) + _dot_nt(qd, st.astype(BF16))
    chunk_decay = f[(LEVELS + 1) * c - 1:(LEVELS + 1) * c, :]
    st_ref[...] = st * chunk_decay + _dot_tn(vb, kd)
    return o


def _hgrn_kernel(q_ref, f_ref, i_ref, g_ref, lb_ref, on_ref, a3_ref, lvl_ref, o_ref, st_ref):
    @pl.when(pl.program_id(1) == 0)
    def _():
        st_ref[...] = jnp.zeros_like(st_ref)

    a3 = a3_ref[...]
    lvl = lvl_ref[...]
    rowid = lax.broadcasted_iota(jnp.int32, (CHUNK, 1), 0)
    for h in range(HEADS):
        sl = slice(h * DK, (h + 1) * DK)
        fl = f_ref[:, sl]
        lb = lb_ref[:, sl]
        log_f = jnp.log(lb + (1.0 - lb) * jax.nn.sigmoid(fl))
        key_in = (1.0 - lb) * jax.nn.sigmoid(-fl)
        q = _silu(q_ref[:, sl]) * (DK ** -0.5)
        o = _gla_chunk(q, key_in, i_ref[:, sl], log_f, st_ref.at[h], a3, lvl, rowid)
        o = _rms(o, on_ref[...])
        o_ref[:, sl] = (o * _silu(g_ref[:, sl])).astype(BF16)


def _hgrn(proj, lb, o_norm, batch, seq):
    n = proj.shape[0]
    nt = seq // CHUNK
    a3, lvl = _decay_tables()
    col = lambda j: pl.BlockSpec((CHUNK, HG_WIDTH), lambda b, t: (b * nt + t, j))
    const = lambda shape: pl.BlockSpec(shape, lambda b, t: (0, 0))
    return pl.pallas_call(
        _hgrn_kernel,
        grid=(batch, nt),
        in_specs=[col(0), col(1), col(2), col(3), const((1, HG_WIDTH)), const((1, DK)),
                  const(a3.shape), const(lvl.shape)],
        out_specs=pl.BlockSpec((CHUNK, HG_WIDTH), lambda b, t: (b * nt + t, 0)),
        out_shape=jax.ShapeDtypeStruct((n, HG_WIDTH), BF16),
        scratch_shapes=[pltpu.VMEM((HEADS, DK, DK), F32)],
        compiler_params=_params("parallel", "arbitrary"),
        name="hgrn",
    )(proj, proj, proj, proj, lb.reshape(1, HG_WIDTH), o_norm.reshape(1, DK), a3, lvl)


def _log_sigmoid(x):
    return jnp.minimum(x, 0.0) - jnp.log1p(jnp.exp(-jnp.abs(x)))


def _gla_kernel(q_ref, k_ref, v_ref, g_ref, a_ref, wa_ref, ba_ref, on_ref, a3_ref, lvl_ref, o_ref, st_ref):
    @pl.when(pl.program_id(1) == 0)
    def _():
        st_ref[...] = jnp.zeros_like(st_ref)

    a3 = a3_ref[...]
    lvl = lvl_ref[...]
    rowid = lax.broadcasted_iota(jnp.int32, (CHUNK, 1), 0)
    gate_logit = _dot(a_ref[...].astype(BF16), wa_ref[...]) + ba_ref[...]
    log_alpha = _log_sigmoid(gate_logit) * (1.0 / GLA_TAU)
    for h in range(HEADS):
        sl = slice(h * DK, (h + 1) * DK)
        sv = slice(h * GLA_DV, (h + 1) * GLA_DV)
        q = q_ref[:, sl] * (DK ** -0.5)
        o = _gla_chunk(q, k_ref[:, sl], v_ref[:, sv], log_alpha[:, sl], st_ref.at[h], a3, lvl, rowid)
        o = _rms(o, on_ref[...])
        o_ref[:, sv] = (o * _silu(g_ref[:, sv])).astype(BF16)


def _gla(proj, w_alpha, b_alpha, o_norm, batch, seq):
    n = proj.shape[0]
    nt = seq // CHUNK
    qk_w = HEADS * DK
    v_w = HEADS * GLA_DV
    a3, lvl = _decay_tables()
    row = lambda b, t: b * nt + t
    const = lambda shape: pl.BlockSpec(shape, lambda b, t: (0, 0))
    return pl.pallas_call(
        _gla_kernel,
        grid=(batch, nt),
        in_specs=[pl.BlockSpec((CHUNK, qk_w), lambda b, t: (row(b, t), 0)),
                  pl.BlockSpec((CHUNK, qk_w), lambda b, t: (row(b, t), 1)),
                  pl.BlockSpec((CHUNK, v_w), lambda b, t: (row(b, t), 1)),
                  pl.BlockSpec((CHUNK, v_w), lambda b, t: (row(b, t), 2)),
                  pl.BlockSpec((CHUNK, GLA_RANK_PAD), lambda b, t: (row(b, t), (2 * qk_w + 2 * v_w) // GLA_RANK_PAD)),
                  const((GLA_RANK_PAD, qk_w)), const((1, qk_w)), const((1, GLA_DV)),
                  const(a3.shape), const(lvl.shape)],
        out_specs=pl.BlockSpec((CHUNK, v_w), lambda b, t: (row(b, t), 0)),
        out_shape=jax.ShapeDtypeStruct((n, v_w), BF16),
        scratch_shapes=[pltpu.VMEM((HEADS, GLA_DV, DK), F32)],
        compiler_params=_params("parallel", "arbitrary"),
        name="gla",
    )(proj, proj, proj, proj, proj, w_alpha, b_alpha.reshape(1, qk_w), o_norm.reshape(1, GLA_DV), a3, lvl)


def _qk_norm(x, w2):
    lane = lax.broadcasted_iota(jnp.int32, x.shape, 1)
    first = lane < DA_HEAD_DIM
    sq = x * x
    ms1 = jnp.sum(jnp.where(first, sq, 0.0), axis=-1, keepdims=True) * (1.0 / DA_HEAD_DIM)
    ms2 = jnp.sum(jnp.where(first, 0.0, sq), axis=-1, keepdims=True) * (1.0 / DA_HEAD_DIM)
    r = jnp.where(first, lax.rsqrt(ms1 + EPS), lax.rsqrt(ms2 + EPS))
    return x * r * w2, first


def _attn_kernel(qi_tab, ki_tab, q_ref, k_ref, v_ref, qn_ref, kn_ref, lam_ref, sub_ref, o_ref,
                 qs_ref, m_ref, l_ref, acc_ref, *, lam_init):
    p = pl.program_id(2)
    qi = qi_tab[p]
    ki = ki_tab[p]
    tq = ATTN_BLOCK

    @pl.when(ki == 0)
    def _():
        qn, first = _qk_norm(q_ref[...], qn_ref[...])
        qn = qn * (DA_HEAD_DIM ** -0.5 * LOG2_E)
        qs_ref[:, :tq] = jnp.where(first, qn, 0.0).T.astype(BF16)
        qs_ref[:, tq:] = jnp.where(first, 0.0, qn).T.astype(BF16)
        m_ref[...] = jnp.full_like(m_ref, -jnp.inf)
        l_ref[...] = jnp.zeros_like(l_ref)
        acc_ref[...] = jnp.zeros_like(acc_ref)

    def step(masked):
        kn, _ = _qk_norm(k_ref[...], kn_ref[...])
        s = _dot(kn.astype(BF16), qs_ref[...])
        if masked:
            key = lax.broadcasted_iota(jnp.int32, s.shape, 0)
            qry = lax.broadcasted_iota(jnp.int32, s.shape, 1)
            qry = jnp.where(qry >= tq, qry - tq, qry)
            s = jnp.where(key <= qry, s, -jnp.inf)
        m_prev = m_ref[...]
        m_new = jnp.maximum(m_prev, jnp.max(s, axis=0, keepdims=True))
        alpha = jnp.exp2(m_prev - m_new)
        pr = jnp.exp2(s - m_new)
        l_ref[...] = alpha * l_ref[...] + jnp.sum(pr, axis=0, keepdims=True)
        acc_ref[...] = alpha * acc_ref[...] + _dot_tn(v_ref[...].astype(BF16), pr.astype(BF16))
        m_ref[...] = m_new

    @pl.when(ki < qi)
    def _():
        step(False)

    @pl.when(ki == qi)
    def _():
        step(True)
        lv = lam_ref[...]
        lam = (jnp.exp(jnp.sum(lv[0:1] * lv[1:2], axis=-1, keepdims=True))
               - jnp.exp(jnp.sum(lv[2:3] * lv[3:4], axis=-1, keepdims=True)) + lam_init)
        o1 = acc_ref[:, :tq] / l_ref[:, :tq]
        o2 = acc_ref[:, tq:] / l_ref[:, tq:]
        o = o1 - lam * o2
        o = o * lax.rsqrt(jnp.mean(o * o, axis=0, keepdims=True) + EPS) * sub_ref[...] * (1.0 - lam_init)
        o_ref[...] = o.T.astype(BF16)


def _diff_attn(proj, q_norm, k_norm, lam_vecs, subln, lam_init, batch, seq):
    n = proj.shape[0]
    blk = ATTN_BLOCK
    nq = seq // blk
    dv = 2 * DA_HEAD_DIM
    pairs = [(qi, ki) for qi in range(nq) for ki in range(qi + 1)]
    qi_tab = jnp.asarray([p[0] for p in pairs], jnp.int32)
    ki_tab = jnp.asarray([p[1] for p in pairs], jnp.int32)
    q_col = (4 * HG_WIDTH) // dv
    const = lambda shape: pl.BlockSpec(shape, lambda b, h, p, qt, kt: (0, 0))
    grid_spec = pltpu.PrefetchScalarGridSpec(
        num_scalar_prefetch=2,
        grid=(batch, HEADS, len(pairs)),
        in_specs=[pl.BlockSpec((blk, dv), lambda b, h, p, qt, kt: (b * nq + qt[p], q_col + h)),
                  pl.BlockSpec((blk, dv), lambda b, h, p, qt, kt: (b * nq + kt[p], q_col + HEADS + h)),
                  pl.BlockSpec((blk, dv), lambda b, h, p, qt, kt: (b * nq + kt[p], q_col + 2 * HEADS + h)),
                  const((1, dv)), const((1, dv)), const((4, DA_HEAD_DIM)), const((dv, 1))],
        out_specs=pl.BlockSpec((blk, dv), lambda b, h, p, qt, kt: (b * nq + qt[p], h)),
        scratch_shapes=[pltpu.VMEM((dv, 2 * blk), BF16),
                        pltpu.VMEM((1, 2 * blk), F32),
                        pltpu.VMEM((1, 2 * blk), F32),
                        pltpu.VMEM((dv, 2 * blk), F32)])
    return pl.pallas_call(
        functools.partial(_attn_kernel, lam_init=lam_init),
        grid_spec=grid_spec,
        out_shape=jax.ShapeDtypeStruct((n, HEADS * dv), BF16),
        compiler_params=_params("parallel", "parallel", "arbitrary"),
        name="diff_attn",
    )(qi_tab, ki_tab, proj, proj, proj,
      jnp.tile(q_norm, 2).reshape(1, -1), jnp.tile(k_norm, 2).reshape(1, -1),
      lam_vecs, subln.reshape(-1, 1))


def _outproj_router_kernel(*refs, n_a):
    a_refs = refs[:n_a]
    (w_ref, x_ref, nw_ref, wr_hi_ref, wr_lo_ref, br_ref, tri_ref,
     x2_ref, meta_ref, cnt_ref, carry_ref) = refs[n_a:]

    @pl.when(pl.program_id(0) == 0)
    def _():
        carry_ref[...] = jnp.zeros_like(carry_ref)

    x2 = _load_tokens(x_ref)
    off = 0
    for a_ref in a_refs:
        kk = a_ref.shape[1]
        x2 = x2 + _dot(a_ref[...], w_ref[off:off + kk, :])
        off += kk
    _store_tokens(x2_ref, x2)

    h = _rms(x2, nw_ref[...])
    h_hi = h.astype(BF16)
    h_lo = (h - h_hi.astype(F32)).astype(BF16)
    wr_hi = wr_hi_ref[...]
    lt = _dot_nt(wr_hi, h_hi) + _dot_nt(wr_hi, h_lo) + _dot_nt(wr_lo_ref[...], h_hi) + br_ref[...]
    row = lambda r: lt[r:r + 1, :]

    gl = [row(g) for g in range(N_GROUPS)]
    gmax = functools.reduce(jnp.maximum, gl)
    gsel = jnp.where(gl[0] == gmax, 0.0, jnp.where(gl[1] == gmax, 1.0, jnp.where(gl[2] == gmax, 2.0, 3.0)))
    g_w = 1.0 / functools.reduce(jnp.add, [jnp.exp(x - gmax) for x in gl])

    el = []
    for e in range(EXPERTS_PER_GROUP):
        v = row(N_GROUPS + e)
        for g in range(1, N_GROUPS):
            v = jnp.where(gsel == float(g), row(N_GROUPS + EXPERTS_PER_GROUP * g + e), v)
        el.append(v)
    emax = functools.reduce(jnp.maximum, el)
    sel = []
    for e in range(EXPERTS_PER_GROUP):
        ahead = jnp.zeros_like(emax)
        for o in range(EXPERTS_PER_GROUP):
            if o < e:
                ahead = ahead + jnp.where(el[o] >= el[e], 1.0, 0.0)
            elif o > e:
                ahead = ahead + jnp.where(el[o] > el[e], 1.0, 0.0)
        sel.append(ahead < 2.0)
    pe = [jnp.where(sel[e], jnp.exp(el[e] - emax), 0.0) for e in range(EXPERTS_PER_GROUP)]
    scale = g_w / functools.reduce(jnp.add, pe)
    zero = jnp.zeros_like(emax)
    seen = zero
    lo_idx, hi_idx, c_lo, c_hi = zero, zero, zero, zero
    for e in range(EXPERTS_PER_GROUP):
        is_lo = sel[e] & (seen == 0.0)
        is_hi = sel[e] & (seen == 1.0)
        lo_idx = jnp.where(is_lo, float(e), lo_idx)
        hi_idx = jnp.where(is_hi, float(e), hi_idx)
        c_lo = jnp.where(is_lo, pe[e] * scale, c_lo)
        c_hi = jnp.where(is_hi, pe[e] * scale, c_hi)
        seen = seen + jnp.where(sel[e], 1.0, 0.0)
    pair = lo_idx * (7.0 - lo_idx) * 0.5 + (hi_idx - lo_idx - 1.0)
    bucket = gsel * float(N_PAIRS) + pair

    bid = lax.broadcasted_iota(jnp.int32, (ROUTER_ROWS, bucket.shape[1]), 0).astype(F32)
    onehot = jnp.where(bid == bucket, 1.0, 0.0)
    before = _dot(onehot.astype(BF16), tri_ref[...]) + carry_ref[...]
    rank = jnp.sum(onehot * before, axis=0, keepdims=True)
    carry = carry_ref[...] + jnp.sum(onehot, axis=1, keepdims=True)
    carry_ref[...] = carry
    cnt_ref[...] = carry
    meta_ref[...] = jnp.concatenate([bucket, rank, c_lo, c_hi, zero, zero, zero, zero], axis=0)


def _outproj_router(a_list, w_out, x, n, norm_w, w_group, b_group, w_router, b_router):
    d = D_MODEL
    tm = ROW_TILE
    tok = lambda: pl.BlockSpec((tm * LANE_TILES, LANE), lambda i: (i, 0))
    wr =jnp.concatenate([w_group, w_router.transpose(1, 0, 2).reshape(d, N_EXPERTS)], axis=1)
    wr_t = jnp.zeros((ROUTER_ROWS, d), F32).at[:wr.shape[1]].set(wr.T)
    wr_hi = wr_t.astype(BF16)
    wr_lo = (wr_t - wr_hi.astype(F32)).astype(BF16)
    br = jnp.zeros((ROUTER_ROWS, 1), F32).at[:wr.shape[1], 0].set(
        jnp.concatenate([b_group, b_router.reshape(-1)]))
    tri = jnp.asarray(np.triu(np.ones((tm, tm), np.float32), 1), BF16)
    const = lambda shape: pl.BlockSpec(shape, lambda i: (0, 0))
    in_specs = [pl.BlockSpec((tm, a.shape[1]), lambda i: (i, 0)) for a in a_list]
    in_specs += [const((d, d)), tok(), const((1, d)),
                 const((ROUTER_ROWS, d)), const((ROUTER_ROWS, d)), const((ROUTER_ROWS, 1)), const((tm, tm))]
    return pl.pallas_call(
        functools.partial(_outproj_router_kernel, n_a=len(a_list)),
        grid=(n // tm,),
        in_specs=in_specs,
        out_specs=[tok(),
                   pl.BlockSpec((8, tm), lambda i: (0, i)),
                   const((ROUTER_ROWS, 1))],
        out_shape=[jax.ShapeDtypeStruct((n * LANE_TILES, LANE), F32),
                   jax.ShapeDtypeStruct((8, n), F32),
                   jax.ShapeDtypeStruct((ROUTER_ROWS, 1), F32)],
        scratch_shapes=[pltpu.VMEM((ROUTER_ROWS, 1), F32)],
        compiler_params=_params("arbitrary"),
        name="outproj_router",
    )(*a_list, w_out, x, norm_w.reshape(1, d), wr_hi, wr_lo, br, tri)


def _moe_kernel(e_lo, e_hi, n_tiles, src, dst, x_hbm, cw_ref, nw_ref, gu_lo_ref, gu_hi_ref, d_lo_ref, d_hi_ref,
                o_hbm, xbuf0, xbuf1, obuf0, obuf1, gsem, ssem):
    del e_lo, e_hi
    tm = MOE_TILE
    t = pl.program_id(0)
    nt = n_tiles[0]
    lt = LANE_TILES
    n_tok = o_hbm.shape[0] // lt - 2 * tm
    xbuf = (xbuf0, xbuf1)
    obuf = (obuf0, obuf1)

    def token(ref, row):
        return ref.at[pl.ds(row if isinstance(row, int) else pl.multiple_of(row, lt), lt)]

    def gather_row(base, buf, r):
        return pltpu.make_async_copy(token(x_hbm, src[base + r]), token(xbuf[buf], r * lt), gsem.at[buf])

    def gather_tile(buf):
        return pltpu.make_async_copy(x_hbm.at[pl.ds(0, tm * lt)], xbuf[buf], gsem.at[buf])

    def scatter_row(base, buf, r):
        return pltpu.make_async_copy(token(obuf[buf], r * lt), token(o_hbm, dst[base + r]), ssem.at[buf])

    def scatter_tile(buf):
        return pltpu.make_async_copy(obuf[buf], o_hbm.at[pl.ds(0, tm * lt)], ssem.at[buf])

    def start_rows(make_row, base, buf):
        def body(r, carry):
            make_row(base, buf, r).start()
            return carry
        lax.fori_loop(0, tm, body, 0, unroll=8)

    @pl.when(t == 0)
    def _():
        obuf0[...] = jnp.zeros_like(obuf0)
        obuf1[...] = jnp.zeros_like(obuf1)
        pltpu.make_async_copy(obuf0, o_hbm.at[pl.ds(n_tok * lt, tm * lt)], ssem.at[0]).start()
        start_rows(gather_row, 0, 0)

    def step(buf):
        other = 1 - buf
        gather_tile(buf).wait()
        out_base = t * tm
        for r in range(tm):
            scatter_row(out_base, other, r).start()
        in_base = jnp.minimum(t + 1, nt - 1) * tm
        for r in range(tm):
            gather_row(in_base, other, r).start()
        x = _load_tokens(xbuf[buf])
        h = _rms(x, nw_ref[...]).astype(BF16)
        cw = cw_ref[...]

        def expert(gu_ref, d_ref):
            gu = _dot(h, gu_ref[0])
            act = _silu(gu[:, :D_EXPERT]) * gu[:, D_EXPERT:]
            return _dot(act.astype(BF16), d_ref[0])

        y = cw[:, 1:2] * expert(gu_lo_ref, d_lo_ref) + cw[:, 2:3] * expert(gu_hi_ref, d_hi_ref)
        scatter_tile(buf).wait()
        _store_tokens(obuf[buf], x + y)

    for b in range(2):
        @pl.when((t < nt) & (t % 2 == b))
        def _(b=b):
            step(b)

    for b in range(2):
        @pl.when((t == nt) & ((nt - 1) % 2 == b))
        def _(b=b):
            start_rows(scatter_row, t * tm, b)
            gather_tile(1 - b).wait()
            scatter_tile(0).wait()
            scatter_tile(1).wait()


def _moe(x2, n, meta, cnt, norm_w, w_gu, w_down):
    d = D_MODEL
    lt = LANE_TILES
    tm = MOE_TILE
    max_tiles = n // tm + N_BUCKETS
    n_slots = max_tiles * tm
    counts = cnt[:N_BUCKETS, 0].astype(jnp.int32)
    tiles_per_bucket = (counts + tm - 1) // tm
    tile_end = jnp.cumsum(tiles_per_bucket)
    row_start = (tile_end - tiles_per_bucket) * tm
    n_tiles = tile_end[-1]
    in_bucket = meta[0][:, None] == jnp.arange(N_BUCKETS, dtype=F32)[None, :]
    dest = meta[1].astype(jnp.int32) + jnp.sum(jnp.where(in_bucket, row_start[None, :], 0), axis=1)
    packed = jnp.stack([jnp.arange(n, dtype=F32), meta[2], meta[3], jnp.ones((n,), F32)], axis=1)
    slots = jnp.zeros((n_slots, 4), F32).at[dest].set(packed, unique_indices=True, mode="promise_in_bounds")
    src = slots[:, 0].astype(jnp.int32)
    sidx = jnp.arange(n_slots, dtype=jnp.int32)
    spare = n + ((sidx // tm) % 2) * tm + sidx % tm
    dst = jnp.where(slots[:, 3] > 0.0, src, spare)
    dst = jnp.concatenate([n + tm + jnp.arange(tm, dtype=jnp.int32), dst])
    src, dst = src * lt, dst * lt

    tile_ids = jnp.minimum(jnp.arange(max_tiles + 1, dtype=jnp.int32), n_tiles - 1)
    tile_bucket = jnp.sum((tile_ids[:, None] >= tile_end[None, :]).astype(jnp.int32), axis=1)
    group = tile_bucket // N_PAIRS
    pair = tile_bucket % N_PAIRS
    pair_lo = jnp.sum(jnp.where(pair[:, None] == jnp.arange(N_PAIRS)[None, :], jnp.asarray(PAIR_LO, jnp.int32)[None, :], 0), axis=1)
    pair_hi = jnp.sum(jnp.where(pair[:, None] == jnp.arange(N_PAIRS)[None, :], jnp.asarray(PAIR_HI, jnp.int32)[None, :], 0), axis=1)
    e_lo = group * EXPERTS_PER_GROUP + pair_lo
    e_hi = group * EXPERTS_PER_GROUP + pair_hi

    last = lambda t, nt: jnp.minimum(t, nt[0] - 1)
    grid_spec = pltpu.PrefetchScalarGridSpec(
        num_scalar_prefetch=5,
        grid=(max_tiles + 1,),
        in_specs=[pl.BlockSpec(memory_space=pl.ANY),
                  pl.BlockSpec((tm, 4), lambda t, lo, hi, nt, s, ds: (last(t, nt), 0)),
                  pl.BlockSpec((1, d), lambda t, lo, hi, nt, s, ds: (0, 0)),
                  pl.BlockSpec((1, d, 2 * D_EXPERT), lambda t, lo, hi, nt, s, ds: (lo[t], 0, 0)),
                  pl.BlockSpec((1, d, 2 * D_EXPERT), lambda t, lo, hi, nt, s, ds: (hi[t], 0, 0)),
                  pl.BlockSpec((1, D_EXPERT, d), lambda t, lo, hi, nt, s, ds: (lo[t], 0, 0)),
                  pl.BlockSpec((1, D_EXPERT, d), lambda t, lo, hi, nt, s, ds: (hi[t], 0, 0))],
        out_specs=pl.BlockSpec(memory_space=pl.ANY),
        scratch_shapes=[pltpu.VMEM((tm * lt, LANE), F32), pltpu.VMEM((tm * lt, LANE), F32),
                        pltpu.VMEM((tm * lt, LANE), F32), pltpu.VMEM((tm * lt, LANE), F32),
                        pltpu.SemaphoreType.DMA((2,)), pltpu.SemaphoreType.DMA((2,))])
    return pl.pallas_call(
        _moe_kernel,
        grid_spec=grid_spec,
        out_shape=jax.ShapeDtypeStruct(((n + 2 * tm) * lt, LANE), F32),
        compiler_params=_params("arbitrary"),
        name="moe",
    )(e_lo, e_hi, n_tiles.reshape(1), src, dst, x2, slots, norm_w.reshape(1, d), w_gu, w_gu, w_down, w_down)


def kernel(x, attn_norm, ffn_norm, even_w_in, even_w_out, hg_lb_logits, hg_o_norm, da_q_norm, da_k_norm, da_lambda, da_subln, odd_w_in, gla_w_alpha, gla_b_alpha, gla_o_norm, odd_w_out, moe_w_group, moe_b_group, moe_w_router, moe_b_router, moe_w_gate, moe_w_up, moe_w_down):
    batch, seq, d = x.shape
    depth = attn_norm.shape[0]
    n = batch * seq
    xf = x.reshape(n * LANE_TILES, LANE)

    lb_sm = jax.nn.softmax(hg_lb_logits.astype(F32), axis=0)
    lb_all = jnp.cumsum(lb_sm, axis=0) - lb_sm[0:1]
    rank = gla_w_alpha.shape[1]

    for layer in range(depth):
        j = layer // 2
        if layer % 2 == 0:
            lam_init = 0.8 - 0.6 * math.exp(-0.3 * layer)
            proj = _inproj(xf, n, attn_norm[layer], even_w_in[j].astype(BF16))
            o_a = _hgrn(proj, lb_all[j], hg_o_norm[j], batch, seq)
            o_b = _diff_attn(proj, da_q_norm[j], da_k_norm[j], da_lambda[j], da_subln[j], lam_init, batch, seq)
            mix = [o_a, o_b]
            w_out = even_w_out[j]
        else:
            w_in = jnp.pad(odd_w_in[j], ((0, 0), (0, GLA_RANK_PAD - rank))).astype(BF16)
            w_alpha = jnp.pad(gla_w_alpha[j], ((0, GLA_RANK_PAD - rank), (0, 0))).astype(BF16)
            proj = _inproj(xf, n, attn_norm[layer], w_in)
            mix = [_gla(proj, w_alpha, gla_b_alpha[j], gla_o_norm[j], batch, seq)]
            w_out = odd_w_out[j]
        x2, meta, cnt = _outproj_router(mix, w_out.astype(BF16), xf, n, ffn_norm[layer], moe_w_group[layer],
                                        moe_b_group[layer], moe_w_router[layer], moe_b_router[layer])
        w_gu = jnp.concatenate([moe_w_gate[layer], moe_w_up[layer]], axis=-1).astype(BF16)
        xf = _moe(x2, n, meta, cnt, ffn_norm[layer], w_gu, moe_w_down[layer].astype(BF16))
    return xf[:n * LANE_TILES].reshape(batch, seq, d)
```

```python
import functools
import math

import numpy as np
import jax
import jax.numpy as jnp
from jax import lax
from jax.experimental import pallas as pl
from jax.experimental.pallas import tpu as pltpu

F32 = jnp.float32
BF16 = jnp.bfloat16
EPS = 1e-6
LOG2_E = 1.4426950408889634

D_MODEL = 1024
LANE = 128
LANE_TILES = D_MODEL // LANE
HEADS = 4
DK = 128
HG_WIDTH = 512
DA_HEAD_DIM = 64
GLA_DV = 256
GLA_RANK_PAD = 128
GLA_TAU = 16.0
N_GROUPS = 4
EXPERTS_PER_GROUP = 4
N_EXPERTS = 16
D_EXPERT = 512
N_PAIRS = 6
N_BUCKETS = N_GROUPS * N_PAIRS
ROUTER_ROWS = 32

CHUNK = 128
LEVELS = 7
ATTN_BLOCK = 512
ROW_TILE = 512
MOE_TILE = 256
VMEM_LIMIT = 56 * 1024 * 1024

PAIR_LO = (0, 0, 0, 1, 1, 2)
PAIR_HI = (1, 2, 3, 2, 3, 3)


def _dot(a, b):
    return jnp.dot(a, b, preferred_element_type=F32)


def _dot_nt(a, b):
    return lax.dot_general(a, b, (((1,), (1,)), ((), ())), preferred_element_type=F32)


def _dot_tn(a, b):
    return lax.dot_general(a, b, (((0,), (0,)), ((), ())), preferred_element_type=F32)


def _rms(x, w):
    return x * lax.rsqrt(jnp.mean(x * x, axis=-1, keepdims=True) + EPS) * w


def _silu(x):
    return x * jax.nn.sigmoid(x)


def _params(*sem):
    return pltpu.CompilerParams(dimension_semantics=sem, vmem_limit_bytes=VMEM_LIMIT)


def _load_tokens(ref):
    rows = ref.shape[0] // LANE_TILES
    return jnp.concatenate([ref[pl.ds(c, rows, stride=LANE_TILES), :] for c in range(LANE_TILES)], axis=1)


def _store_tokens(ref, val):
    rows = ref.shape[0] // LANE_TILES
    for c in range(LANE_TILES):
        ref[pl.ds(c, rows, stride=LANE_TILES), :] = val[:, c * LANE:(c + 1) * LANE]


def _inproj_kernel(x_ref, nw_ref, w_ref, o_ref):
    h = _rms(_load_tokens(x_ref), nw_ref[...]).astype(BF16)
    o_ref[...] = _dot(h, w_ref[...])


def _inproj(x, n, norm_w, w):
    d = D_MODEL
    n_out = w.shape[1]
    return pl.pallas_call(
        _inproj_kernel,
        grid=(n // ROW_TILE,),
        in_specs=[pl.BlockSpec((ROW_TILE * LANE_TILES, LANE), lambda i: (i, 0)),
                  pl.BlockSpec((1, d), lambda i: (0, 0)),
                  pl.BlockSpec((d, n_out), lambda i: (0, 0))],
        out_specs=pl.BlockSpec((ROW_TILE, n_out), lambda i: (i, 0)),
        out_shape=jax.ShapeDtypeStruct((n, n_out), F32),
        compiler_params=_params("parallel"),
        name="inproj",
    )(x, norm_w.reshape(1, d), w)


def _decay_tables():
    c = CHUNK
    a = np.zeros((LEVELS + 2, c, c), np.float32)
    t = np.arange(c)
    for l in range(LEVELS):
        s = 1 << l
        for r in range(c):
            mid = (r // (2 * s)) * 2 * s + s - 1
            if (r >> l) & 1:
                a[l, r] = (t > mid) & (t <= r)
            else:
                a[l, r] = (t > r) & (t <= mid)
    a[LEVELS] = t[None, :] <= t[:, None]
    a[LEVELS + 1] = t[None, :] > t[:, None]
    a = a.reshape((LEVELS + 2) * c, c)
    a3 = np.concatenate([a, a], axis=1)
    i, j = np.meshgrid(t, t, indexing="ij")
    x = np.maximum(i ^ j, 1)
    lvl = np.where(j < i, np.floor(np.log2(x)).astype(np.int32), np.where(i == j, LEVELS, -1))
    return jnp.asarray(a3, BF16), jnp.asarray(lvl, jnp.int32)


def _gla_chunk(q, k, v, g, st_ref, a3, lvl, rowid):
    c = CHUNK
    g_hi = g.astype(BF16)
    g_lo = (g - g_hi.astype(F32)).astype(BF16)
    f = jnp.exp(_dot(a3, jnp.concatenate([g_hi, g_lo], axis=0)))
    sc = jnp.where(lvl == LEVELS, jnp.sum(q * k, axis=-1, keepdims=True), 0.0)
    for l in range(LEVELS):
        is_query = ((rowid >> l) & 1) == 1
        m = (jnp.where(is_query, q, k) * f[l * c:(l + 1) * c]).astype(BF16)
        sc = jnp.where(lvl == l, _dot_nt(m, m), sc)
    qd = (q * f[LEVELS * c:(LEVELS + 1) * c]).astype(BF16)
    kd = (k * f[(LEVELS + 1) * c:(LEVELS + 2) * c]).astype(BF16)
    vb = v.astype(BF16)
    st = st_ref[...]
    o = _dot(sc.astype(BF16), vb) + _dot_nt(qd, st.astype(BF16))
    chunk_decay = f[(LEVELS + 1) * c - 1:(LEVELS + 1) * c, :]
    st_ref[...] = st * chunk_decay + _dot_tn(vb, kd)
    return o


def _hgrn_kernel(q_ref, f_ref, i_ref, g_ref, lb_ref, on_ref, a3_ref, lvl_ref, o_ref, st_ref):
    @pl.when(pl.program_id(1) == 0)
    def _():
        st_ref[...] = jnp.zeros_like(st_ref)

    a3 = a3_ref[...]
    lvl = lvl_ref[...]
    rowid = lax.broadcasted_iota(jnp.int32, (CHUNK, 1), 0)
    for h in range(HEADS):
        sl = slice(h * DK, (h + 1) * DK)
        fl = f_ref[:, sl]
        lb = lb_ref[:, sl]
        log_f = jnp.log(lb + (1.0 - lb) * jax.nn.sigmoid(fl))
        key_in = (1.0 - lb) * jax.nn.sigmoid(-fl)
        q = _silu(q_ref[:, sl]) * (DK ** -0.5)
        o = _gla_chunk(q, key_in, i_ref[:, sl], log_f, st_ref.at[h], a3, lvl, rowid)
        o = _rms(o, on_ref[...])
        o_ref[:, sl] = (o * _silu(g_ref[:, sl])).astype(BF16)


def _hgrn(proj, lb, o_norm, batch, seq):
    n = proj.shape[0]
    nt = seq // CHUNK
    a3, lvl = _decay_tables()
    col = lambda j: pl.BlockSpec((CHUNK, HG_WIDTH), lambda b, t: (b * nt + t, j))
    const = lambda shape: pl.BlockSpec(shape, lambda b, t: (0, 0))
    return pl.pallas_call(
        _hgrn_kernel,
        grid=(batch, nt),
        in_specs=[col(0), col(1), col(2), col(3), const((1, HG_WIDTH)), const((1, DK)),
                  const(a3.shape), const(lvl.shape)],
        out_specs=pl.BlockSpec((CHUNK, HG_WIDTH), lambda b, t: (b * nt + t, 0)),
        out_shape=jax.ShapeDtypeStruct((n, HG_WIDTH), BF16),
        scratch_shapes=[pltpu.VMEM((HEADS, DK, DK), F32)],
        compiler_params=_params("parallel", "arbitrary"),
        name="hgrn",
    )(proj, proj, proj, proj, lb.reshape(1, HG_WIDTH), o_norm.reshape(1, DK), a3, lvl)


def _log_sigmoid(x):
    return jnp.minimum(x, 0.0) - jnp.log1p(jnp.exp(-jnp.abs(x)))


def _gla_kernel(q_ref, k_ref, v_ref, g_ref, a_ref, wa_ref, ba_ref, on_ref, a3_ref, lvl_ref, o_ref, st_ref):
    @pl.when(pl.program_id(1) == 0)
    def _():
        st_ref[...] = jnp.zeros_like(st_ref)

    a3 = a3_ref[...]
    lvl = lvl_ref[...]
    rowid = lax.broadcasted_iota(jnp.int32, (CHUNK, 1), 0)
    gate_logit = _dot(a_ref[...].astype(BF16), wa_ref[...]) + ba_ref[...]
    log_alpha = _log_sigmoid(gate_logit) * (1.0 / GLA_TAU)
    for h in range(HEADS):
        sl = slice(h * DK, (h + 1) * DK)
        sv = slice(h * GLA_DV, (h + 1) * GLA_DV)
        q = q_ref[:, sl] * (DK ** -0.5)
        o = _gla_chunk(q, k_ref[:, sl], v_ref[:, sv], log_alpha[:, sl], st_ref.at[h], a3, lvl, rowid)
        o = _rms(o, on_ref[...])
        o_ref[:, sv] = (o * _silu(g_ref[:, sv])).astype(BF16)


def _gla(proj, w_alpha, b_alpha, o_norm, batch, seq):
    n = proj.shape[0]
    nt = seq // CHUNK
    qk_w = HEADS * DK
    v_w = HEADS * GLA_DV
    a3, lvl = _decay_tables()
    row = lambda b, t: b * nt + t
    const = lambda shape: pl.BlockSpec(shape, lambda b, t: (0, 0))
    return pl.pallas_call(
        _gla_kernel,
        grid=(batch, nt),
        in_specs=[pl.BlockSpec((CHUNK, qk_w), lambda b, t: (row(b, t), 0)),
                  pl.BlockSpec((CHUNK, qk_w), lambda b, t: (row(b, t), 1)),
                  pl.BlockSpec((CHUNK, v_w), lambda b, t: (row(b, t), 1)),
                  pl.BlockSpec((CHUNK, v_w), lambda b, t: (row(b, t), 2)),
                  pl.BlockSpec((CHUNK, GLA_RANK_PAD), lambda b, t: (row(b, t), (2 * qk_w + 2 * v_w) // GLA_RANK_PAD)),
                  const((GLA_RANK_PAD, qk_w)), const((1, qk_w)), const((1, GLA_DV)),
                  const(a3.shape), const(lvl.shape)],
        out_specs=pl.BlockSpec((CHUNK, v_w), lambda b, t: (row(b, t), 0)),
        out_shape=jax.ShapeDtypeStruct((n, v_w), BF16),
        scratch_shapes=[pltpu.VMEM((HEADS, GLA_DV, DK), F32)],
        compiler_params=_params("parallel", "arbitrary"),
        name="gla",
    )(proj, proj, proj, proj, proj, w_alpha, b_alpha.reshape(1, qk_w), o_norm.reshape(1, GLA_DV), a3, lvl)


def _qk_norm(x, w2):
    lane = lax.broadcasted_iota(jnp.int32, x.shape, 1)
    first = lane < DA_HEAD_DIM
    sq = x * x
    ms1 = jnp.sum(jnp.where(first, sq, 0.0), axis=-1, keepdims=True) * (1.0 / DA_HEAD_DIM)
    ms2 = jnp.sum(jnp.where(first, 0.0, sq), axis=-1, keepdims=True) * (1.0 / DA_HEAD_DIM)
    r = jnp.where(first, lax.rsqrt(ms1 + EPS), lax.rsqrt(ms2 + EPS))
    return x * r * w2, first


def _attn_kernel(qi_tab, ki_tab, q_ref, k_ref, v_ref, qn_ref, kn_ref, lam_ref, sub_ref, o_ref,
                 qs_ref, m_ref, l_ref, acc_ref, *, lam_init):
    p = pl.program_id(2)
    qi = qi_tab[p]
    ki = ki_tab[p]
    tq = ATTN_BLOCK

    @pl.when(ki == 0)
    def _():
        qn, first = _qk_norm(q_ref[...], qn_ref[...])
        qn = qn * (DA_HEAD_DIM ** -0.5 * LOG2_E)
        qs_ref[:, :tq] = jnp.where(first, qn, 0.0).T.astype(BF16)
        qs_ref[:, tq:] = jnp.where(first, 0.0, qn).T.astype(BF16)
        m_ref[...] = jnp.full_like(m_ref, -jnp.inf)
        l_ref[...] = jnp.zeros_like(l_ref)
        acc_ref[...] = jnp.zeros_like(acc_ref)

    def step(masked):
        kn, _ = _qk_norm(k_ref[...], kn_ref[...])
        s = _dot(kn.astype(BF16), qs_ref[...])
        if masked:
            key = lax.broadcasted_iota(jnp.int32, s.shape, 0)
            qry = lax.broadcasted_iota(jnp.int32, s.shape, 1)
            qry = jnp.where(qry >= tq, qry - tq, qry)
            s = jnp.where(key <= qry, s, -jnp.inf)
        m_prev = m_ref[...]
        m_new = jnp.maximum(m_prev, jnp.max(s, axis=0, keepdims=True))
        alpha = jnp.exp2(m_prev - m_new)
        pr = jnp.exp2(s - m_new)
        l_ref[...] = alpha * l_ref[...] + jnp.sum(pr, axis=0, keepdims=True)
        acc_ref[...] = alpha * acc_ref[...] + _dot_tn(v_ref[...].astype(BF16), pr.astype(BF16))
        m_ref[...] = m_new

    @pl.when(ki < qi)
    def _():
        step(False)

    @pl.when(ki == qi)
    def _():
        step(True)
        lv = lam_ref[...]
        lam = (jnp.exp(jnp.sum(lv[0:1] * lv[1:2], axis=-1, keepdims=True))
               - jnp.exp(jnp.sum(lv[2:3] * lv[3:4], axis=-1, keepdims=True)) + lam_init)
        o1 = acc_ref[:, :tq] / l_ref[:, :tq]
        o2 = acc_ref[:, tq:] / l_ref[:, tq:]
        o = o1 - lam * o2
        o = o * lax.rsqrt(jnp.mean(o * o, axis=0, keepdims=True) + EPS) * sub_ref[...] * (1.0 - lam_init)
        o_ref[...] = o.T.astype(BF16)


def _diff_attn(proj, q_norm, k_norm, lam_vecs, subln, lam_init, batch, seq):
    n = proj.shape[0]
    blk = ATTN_BLOCK
    nq = seq // blk
    dv = 2 * DA_HEAD_DIM
    pairs = [(qi, ki) for qi in range(nq) for ki in range(qi + 1)]
    qi_tab = jnp.asarray([p[0] for p in pairs], jnp.int32)
    ki_tab = jnp.asarray([p[1] for p in pairs], jnp.int32)
    q_col = (4 * HG_WIDTH) // dv
    const = lambda shape: pl.BlockSpec(shape, lambda b, h, p, qt, kt: (0, 0))
    grid_spec = pltpu.PrefetchScalarGridSpec(
        num_scalar_prefetch=2,
        grid=(batch, HEADS, len(pairs)),
        in_specs=[pl.BlockSpec((blk, dv), lambda b, h, p, qt, kt: (b * nq + qt[p], q_col + h)),
                  pl.BlockSpec((blk, dv), lambda b, h, p, qt, kt: (b * nq + kt[p], q_col + HEADS + h)),
                  pl.BlockSpec((blk, dv), lambda b, h, p, qt, kt: (b * nq + kt[p], q_col + 2 * HEADS + h)),
                  const((1, dv)), const((1, dv)), const((4, DA_HEAD_DIM)), const((dv, 1))],
        out_specs=pl.BlockSpec((blk, dv), lambda b, h, p, qt, kt: (b * nq + qt[p], h)),
        scratch_shapes=[pltpu.VMEM((dv, 2 * blk), BF16),
                        pltpu.VMEM((1, 2 * blk), F32),
                        pltpu.VMEM((1, 2 * blk), F32),
                        pltpu.VMEM((dv, 2 * blk), F32)])
    return pl.pallas_call(
        functools.partial(_attn_kernel, lam_init=lam_init),
        grid_spec=grid_spec,
        out_shape=jax.ShapeDtypeStruct((n, HEADS * dv), BF16),
        compiler_params=_params("parallel", "parallel", "arbitrary"),
        name="diff_attn",
    )(qi_tab, ki_tab, proj, proj, proj,
      jnp.tile(q_norm, 2).reshape(1, -1), jnp.tile(k_norm, 2).reshape(1, -1),
      lam_vecs, subln.reshape(-1, 1))


def _outproj_router_kernel(*refs, n_a):
    a_refs = refs[:n_a]
    (w_ref, x_ref, nw_ref, wr_hi_ref, wr_lo_ref, br_ref, tri_ref,
     x2_ref, meta_ref, cnt_ref, carry_ref) = refs[n_a:]

    @pl.when(pl.program_id(0) == 0)
    def _():
        carry_ref[...] = jnp.zeros_like(carry_ref)

    x2 = _load_tokens(x_ref)
    off = 0
    for a_ref in a_refs:
        kk = a_ref.shape[1]
        x2 = x2 + _dot(a_ref[...], w_ref[off:off + kk, :])
        off += kk
    _store_tokens(x2_ref, x2)

    h = _rms(x2, nw_ref[...])
    h_hi = h.astype(BF16)
    h_lo = (h - h_hi.astype(F32)).astype(BF16)
    wr_hi = wr_hi_ref[...]
    lt = _dot_nt(wr_hi, h_hi) + _dot_nt(wr_hi, h_lo) + _dot_nt(wr_lo_ref[...], h_hi) + br_ref[...]
    row = lambda r: lt[r:r + 1, :]

    gl = [row(g) for g in range(N_GROUPS)]
    gmax = functools.reduce(jnp.maximum, gl)
    gsel = jnp.where(gl[0] == gmax, 0.0, jnp.where(gl[1] == gmax, 1.0, jnp.where(gl[2] == gmax, 2.0, 3.0)))
    g_w = 1.0 / functools.reduce(jnp.add, [jnp.exp(x - gmax) for x in gl])

    el = []
    for e in range(EXPERTS_PER_GROUP):
        v = row(N_GROUPS + e)
        for g in range(1, N_GROUPS):
            v = jnp.where(gsel == float(g), row(N_GROUPS + EXPERTS_PER_GROUP * g + e), v)
        el.append(v)
    emax = functools.reduce(jnp.maximum, el)
    sel = []
    for e in range(EXPERTS_PER_GROUP):
        ahead = jnp.zeros_like(emax)
        for o in range(EXPERTS_PER_GROUP):
            if o < e:
                ahead = ahead + jnp.where(el[o] >= el[e], 1.0, 0.0)
            elif o > e:
                ahead = ahead + jnp.where(el[o] > el[e], 1.0, 0.0)
        sel.append(ahead < 2.0)
    pe = [jnp.where(sel[e], jnp.exp(el[e] - emax), 0.0) for e in range(EXPERTS_PER_GROUP)]
    scale = g_w / functools.reduce(jnp.add, pe)
    zero = jnp.zeros_like(emax)
    seen = zero
    lo_idx, hi_idx, c_lo, c_hi = zero, zero, zero, zero
    for e in range(EXPERTS_PER_GROUP):
        is_lo = sel[e] & (seen == 0.0)
        is_hi = sel[e] & (seen == 1.0)
        lo_idx = jnp.where(is_lo, float(e), lo_idx)
        hi_idx = jnp.where(is_hi, float(e), hi_idx)
        c_lo = jnp.where(is_lo, pe[e] * scale, c_lo)
        c_hi = jnp.where(is_hi, pe[e] * scale, c_hi)
        seen = seen + jnp.where(sel[e], 1.0, 0.0)
    pair = lo_idx * (7.0 - lo_idx) * 0.5 + (hi_idx - lo_idx - 1.0)
    bucket = gsel * float(N_PAIRS) + pair

    bid = lax.broadcasted_iota(jnp.int32, (ROUTER_ROWS, bucket.shape[1]), 0).astype(F32)
    onehot = jnp.where(bid == bucket, 1.0, 0.0)
    before = _dot(onehot.astype(BF16), tri_ref[...]) + carry_ref[...]
    rank = jnp.sum(onehot * before, axis=0, keepdims=True)
    carry = carry_ref[...] + jnp.sum(onehot, axis=1, keepdims=True)
    carry_ref[...] = carry
    cnt_ref[...] = carry
    meta_ref[...] = jnp.concatenate([bucket, rank, c_lo, c_hi, zero, zero, zero, zero], axis=0)


def _outproj_router(a_list, w_out, x, n, norm_w, w_group, b_group, w_router, b_router):
    d = D_MODEL
    tm = ROW_TILE
    tok = lambda: pl.BlockSpec((tm * LANE_TILES, LANE), lambda i: (i, 0))
    wr =jnp.concatenate([w_group, w_router.transpose(1, 0, 2).reshape(d, N_EXPERTS)], axis=1)
    wr_t = jnp.zeros((ROUTER_ROWS, d), F32).at[:wr.shape[1]].set(wr.T)
    wr_hi = wr_t.astype(BF16)
    wr_lo = (wr_t - wr_hi.astype(F32)).astype(BF16)
    br = jnp.zeros((ROUTER_ROWS, 1), F32).at[:wr.shape[1], 0].set(
        jnp.concatenate([b_group, b_router.reshape(-1)]))
    tri = jnp.asarray(np.triu(np.ones((tm, tm), np.float32), 1), BF16)
    const = lambda shape: pl.BlockSpec(shape, lambda i: (0, 0))
    in_specs = [pl.BlockSpec((tm, a.shape[1]), lambda i: (i, 0)) for a in a_list]
    in_specs += [const((d, d)), tok(), const((1, d)),
                 const((ROUTER_ROWS, d)), const((ROUTER_ROWS, d)), const((ROUTER_ROWS, 1)), const((tm, tm))]
    return pl.pallas_call(
        functools.partial(_outproj_router_kernel, n_a=len(a_list)),
        grid=(n // tm,),
        in_specs=in_specs,
        out_specs=[tok(),
                   pl.BlockSpec((8, tm), lambda i: (0, i)),
                   const((ROUTER_ROWS, 1))],
        out_shape=[jax.ShapeDtypeStruct((n * LANE_TILES, LANE), F32),
                   jax.ShapeDtypeStruct((8, n), F32),
                   jax.ShapeDtypeStruct((ROUTER_ROWS, 1), F32)],
        scratch_shapes=[pltpu.VMEM((ROUTER_ROWS, 1), F32)],
        compiler_params=_params("arbitrary"),
        name="outproj_router",
    )(*a_list, w_out, x, norm_w.reshape(1, d), wr_hi, wr_lo, br, tri)


def _moe_kernel(e_lo, e_hi, n_tiles, src, dst, x_hbm, cw_ref, nw_ref, gu_lo_ref, gu_hi_ref, d_lo_ref, d_hi_ref,
                o_hbm, xbuf0, xbuf1, obuf0, obuf1, gsem, ssem):
    del e_lo, e_hi
    tm = MOE_TILE
    t = pl.program_id(0)
    nt = n_tiles[0]
    lt = LANE_TILES
    n_tok = o_hbm.shape[0] // lt - 2 * tm
    xbuf = (xbuf0, xbuf1)
    obuf = (obuf0, obuf1)

    def token(ref, row):
        return ref.at[pl.ds(row if isinstance(row, int) else pl.multiple_of(row, lt), lt)]

    def gather_row(base, buf, r):
        return pltpu.make_async_copy(token(x_hbm, src[base + r]), token(xbuf[buf], r * lt), gsem.at[buf])

    def gather_tile(buf):
        return pltpu.make_async_copy(x_hbm.at[pl.ds(0, tm * lt)], xbuf[buf], gsem.at[buf])

    def scatter_row(base, buf, r):
        return pltpu.make_async_copy(token(obuf[buf], r * lt), token(o_hbm, dst[base + r]), ssem.at[buf])

    def scatter_tile(buf):
        return pltpu.make_async_copy(obuf[buf], o_hbm.at[pl.ds(0, tm * lt)], ssem.at[buf])

    def start_rows(make_row, base, buf):
        def body(r, carry):
            make_row(base, buf, r).start()
            return carry
        lax.fori_loop(0, tm, body, 0, unroll=8)

    @pl.when(t == 0)
    def _():
        obuf0[...] = jnp.zeros_like(obuf0)
        obuf1[...] = jnp.zeros_like(obuf1)
        pltpu.make_async_copy(obuf0, o_hbm.at[pl.ds(n_tok * lt, tm * lt)], ssem.at[0]).start()
        start_rows(gather_row, 0, 0)

    def step(buf):
        other = 1 - buf
        gather_tile(buf).wait()
        out_base = t * tm
        for r in range(tm):
            scatter_row(out_base, other, r).start(priority=1)
        in_base = jnp.minimum(t + 1, nt - 1) * tm
        for r in range(tm):
            gather_row(in_base, other, r).start(priority=0)
        x = _load_tokens(xbuf[buf])
        h = _rms(x, nw_ref[...]).astype(BF16)
        cw = cw_ref[...]

        def expert(gu_ref, d_ref):
            gu = _dot(h, gu_ref[0])
            act = _silu(gu[:, :D_EXPERT]) * gu[:, D_EXPERT:]
            return _dot(act.astype(BF16), d_ref[0])

        y = cw[:, 1:2] * expert(gu_lo_ref, d_lo_ref) + cw[:, 2:3] * expert(gu_hi_ref, d_hi_ref)
        scatter_tile(buf).wait()
        _store_tokens(obuf[buf], x + y)

    for b in range(2):
        @pl.when((t < nt) & (t % 2 == b))
        def _(b=b):
            step(b)

    for b in range(2):
        @pl.when((t == nt) & ((nt - 1) % 2 == b))
        def _(b=b):
            start_rows(scatter_row, t * tm, b)
            gather_tile(1 - b).wait()
            scatter_tile(0).wait()
            scatter_tile(1).wait()


def _moe(x2, n, meta, cnt, norm_w, w_gu, w_down):
    d = D_MODEL
    lt = LANE_TILES
    tm = MOE_TILE
    max_tiles = n // tm + N_BUCKETS
    n_slots = max_tiles * tm
    counts = cnt[:N_BUCKETS, 0].astype(jnp.int32)
    tiles_per_bucket = (counts + tm - 1) // tm
    tile_end = jnp.cumsum(tiles_per_bucket)
    row_start = (tile_end - tiles_per_bucket) * tm
    n_tiles = tile_end[-1]
    in_bucket = meta[0][:, None] == jnp.arange(N_BUCKETS, dtype=F32)[None, :]
    dest = meta[1].astype(jnp.int32) + jnp.sum(jnp.where(in_bucket, row_start[None, :], 0), axis=1)
    packed = jnp.stack([jnp.arange(n, dtype=F32), meta[2], meta[3], jnp.ones((n,), F32)], axis=1)
    slots = jnp.zeros((n_slots, 4), F32).at[dest].set(packed, unique_indices=True, mode="promise_in_bounds")
    src = slots[:, 0].astype(jnp.int32)
    sidx = jnp.arange(n_slots, dtype=jnp.int32)
    spare = n + ((sidx // tm) % 2) * tm + sidx % tm
    dst = jnp.where(slots[:, 3] > 0.0, src, spare)
    dst = jnp.concatenate([n + tm + jnp.arange(tm, dtype=jnp.int32), dst])
    src, dst = src * lt, dst * lt

    tile_ids = jnp.minimum(jnp.arange(max_tiles + 1, dtype=jnp.int32), n_tiles - 1)
    tile_bucket = jnp.sum((tile_ids[:, None] >= tile_end[None, :]).astype(jnp.int32), axis=1)
    group = tile_bucket // N_PAIRS
    pair = tile_bucket % N_PAIRS
    pair_lo = jnp.sum(jnp.where(pair[:, None] == jnp.arange(N_PAIRS)[None, :], jnp.asarray(PAIR_LO, jnp.int32)[None, :], 0), axis=1)
    pair_hi = jnp.sum(jnp.where(pair[:, None] == jnp.arange(N_PAIRS)[None, :], jnp.asarray(PAIR_HI, jnp.int32)[None, :], 0), axis=1)
    e_lo = group * EXPERTS_PER_GROUP + pair_lo
    e_hi = group * EXPERTS_PER_GROUP + pair_hi

    last = lambda t, nt: jnp.minimum(t, nt[0] - 1)
    grid_spec = pltpu.PrefetchScalarGridSpec(
        num_scalar_prefetch=5,
        grid=(max_tiles + 1,),
        in_specs=[pl.BlockSpec(memory_space=pl.ANY),
                  pl.BlockSpec((tm, 4), lambda t, lo, hi, nt, s, ds: (last(t, nt), 0)),
                  pl.BlockSpec((1, d), lambda t, lo, hi, nt, s, ds: (0, 0)),
                  pl.BlockSpec((1, d, 2 * D_EXPERT), lambda t, lo, hi, nt, s, ds: (lo[t], 0, 0)),
                  pl.BlockSpec((1, d, 2 * D_EXPERT), lambda t, lo, hi, nt, s, ds: (hi[t], 0, 0)),
                  pl.BlockSpec((1, D_EXPERT, d), lambda t, lo, hi, nt, s, ds: (lo[t], 0, 0)),
                  pl.BlockSpec((1, D_EXPERT, d), lambda t, lo, hi, nt, s, ds: (hi[t], 0, 0))],
        out_specs=pl.BlockSpec(memory_space=pl.ANY),
        scratch_shapes=[pltpu.VMEM((tm * lt, LANE), F32), pltpu.VMEM((tm * lt, LANE), F32),
                        pltpu.VMEM((tm * lt, LANE), F32), pltpu.VMEM((tm * lt, LANE), F32),
                        pltpu.SemaphoreType.DMA((2,)), pltpu.SemaphoreType.DMA((2,))])
    return pl.pallas_call(
        _moe_kernel,
        grid_spec=grid_spec,
        out_shape=jax.ShapeDtypeStruct(((n + 2 * tm) * lt, LANE), F32),
        compiler_params=_params("arbitrary"),
        name="moe",
    )(e_lo, e_hi, n_tiles.reshape(1), src, dst, x2, slots, norm_w.reshape(1, d), w_gu, w_gu, w_down, w_down)


def kernel(x, attn_norm, ffn_norm, even_w_in, even_w_out, hg_lb_logits, hg_o_norm, da_q_norm, da_k_norm, da_lambda, da_subln, odd_w_in, gla_w_alpha, gla_b_alpha, gla_o_norm, odd_w_out, moe_w_group, moe_b_group, moe_w_router, moe_b_router, moe_w_gate, moe_w_up, moe_w_down):
    batch, seq, d = x.shape
    depth = attn_norm.shape[0]
    n = batch * seq
    xf = x.reshape(n * LANE_TILES, LANE)

    lb_sm = jax.nn.softmax(hg_lb_logits.astype(F32), axis=0)
    lb_all = jnp.cumsum(lb_sm, axis=0) - lb_sm[0:1]
    rank = gla_w_alpha.shape[1]

    for layer in range(depth):
        j = layer // 2
        if layer % 2 == 0:
            lam_init = 0.8 - 0.6 * math.exp(-0.3 * layer)
            proj = _inproj(xf, n, attn_norm[layer], even_w_in[j].astype(BF16))
            o_a = _hgrn(proj, lb_all[j], hg_o_norm[j], batch, seq)
            o_b = _diff_attn(proj, da_q_norm[j], da_k_norm[j], da_lambda[j], da_subln[j], lam_init, batch, seq)
            mix = [o_a, o_b]
            w_out = even_w_out[j]
        else:
            w_in = jnp.pad(odd_w_in[j], ((0, 0), (0, GLA_RANK_PAD - rank))).astype(BF16)
            w_alpha = jnp.pad(gla_w_alpha[j], ((0, GLA_RANK_PAD - rank), (0, 0))).astype(BF16)
            proj = _inproj(xf, n, attn_norm[layer], w_in)
            mix = [_gla(proj, w_alpha, gla_b_alpha[j], gla_o_norm[j], batch, seq)]
            w_out = odd_w_out[j]
        x2, meta, cnt = _outproj_router(mix, w_out.astype(BF16), xf, n, ffn_norm[layer], moe_w_group[layer],
                                        moe_b_group[layer], moe_w_router[layer], moe_b_router[layer])
        w_gu = jnp.concatenate([moe_w_gate[layer], moe_w_up[layer]], axis=-1).astype(BF16)
        xf = _moe(x2, n, meta, cnt, ffn_norm[layer], w_gu, moe_w_down[layer].astype(BF16))
    return xf[:n * LANE_TILES].reshape(batch, seq, d)
```

```python
import functools
import math

import numpy as np
import jax
import jax.numpy as jnp
from jax import lax
from jax.experimental import pallas as pl
from jax.experimental.pallas import tpu as pltpu
from jax.experimental.pallas import tpu_sc as plsc

F32 = jnp.float32
BF16 = jnp.bfloat16
EPS = 1e-6
LOG2_E = 1.4426950408889634

D_MODEL = 1024
LANE = 128
LANE_TILES = D_MODEL // LANE
HEADS = 4
DK = 128
HG_WIDTH = 512
DA_HEAD_DIM = 64
GLA_DV = 256
GLA_RANK_PAD = 128
GLA_TAU = 16.0
N_GROUPS = 4
EXPERTS_PER_GROUP = 4
N_EXPERTS = 16
D_EXPERT = 512
N_PAIRS = 6
N_BUCKETS = N_GROUPS * N_PAIRS
ROUTER_ROWS = 32

CHUNK = 128
LEVELS = 7
ATTN_BLOCK = 512
ROW_TILE = 512
MOE_TILE = 256
SC_CHUNK = 32
VMEM_LIMIT = 56 * 1024 * 1024

PAIR_LO = (0, 0, 0, 1, 1, 2)
PAIR_HI = (1, 2, 3, 2, 3, 3)


def _dot(a, b):
    return jnp.dot(a, b, preferred_element_type=F32)


def _dot_nt(a, b):
    return lax.dot_general(a, b, (((1,), (1,)), ((), ())), preferred_element_type=F32)


def _dot_tn(a, b):
    return lax.dot_general(a, b, (((0,), (0,)), ((), ())), preferred_element_type=F32)


def _rms(x, w):
    return x * lax.rsqrt(jnp.mean(x * x, axis=-1, keepdims=True) + EPS) * w


def _silu(x):
    return x * jax.nn.sigmoid(x)


def _params(*sem):
    return pltpu.CompilerParams(dimension_semantics=sem, vmem_limit_bytes=VMEM_LIMIT)


def _load_tokens(ref):
    rows = ref.shape[0] // LANE_TILES
    return jnp.concatenate([ref[pl.ds(c, rows, stride=LANE_TILES), :] for c in range(LANE_TILES)], axis=1)


def _store_tokens(ref, val):
    rows = ref.shape[0] // LANE_TILES
    for c in range(LANE_TILES):
        ref[pl.ds(c, rows, stride=LANE_TILES), :] = val[:, c * LANE:(c + 1) * LANE]


def _inproj_kernel(x_ref, nw_ref, w_ref, o_ref):
    h = _rms(_load_tokens(x_ref), nw_ref[...]).astype(BF16)
    o_ref[...] = _dot(h, w_ref[...])


def _inproj(x, n, norm_w, w):
    d = D_MODEL
    n_out = w.shape[1]
    return pl.pallas_call(
        _inproj_kernel,
        grid=(n // ROW_TILE,),
        in_specs=[pl.BlockSpec((ROW_TILE * LANE_TILES, LANE), lambda i: (i, 0)),
                  pl.BlockSpec((1, d), lambda i: (0, 0)),
                  pl.BlockSpec((d, n_out), lambda i: (0, 0))],
        out_specs=pl.BlockSpec((ROW_TILE, n_out), lambda i: (i, 0)),
        out_shape=jax.ShapeDtypeStruct((n, n_out), F32),
        compiler_params=_params("parallel"),
        name="inproj",
    )(x, norm_w.reshape(1, d), w)


def _decay_tables():
    c = CHUNK
    a = np.zeros((LEVELS + 2, c, c), np.float32)
    t = np.arange(c)
    for l in range(LEVELS):
        s = 1 << l
        for r in range(c):
            mid = (r // (2 * s)) * 2 * s + s - 1
            if (r >> l) & 1:
                a[l, r] = (t > mid) & (t <= r)
            else:
                a[l, r] = (t > r) & (t <= mid)
    a[LEVELS] = t[None, :] <= t[:, None]
    a[LEVELS + 1] = t[None, :] > t[:, None]
    a = a.reshape((LEVELS + 2) * c, c)
    a3 = np.concatenate([a, a], axis=1)
    i, j = np.meshgrid(t, t, indexing="ij")
    x = np.maximum(i ^ j, 1)
    lvl = np.where(j < i, np.floor(np.log2(x)).astype(np.int32), np.where(i == j, LEVELS, -1))
    return jnp.asarray(a3, BF16), jnp.asarray(lvl, jnp.int32)


def _gla_chunk(q, k, v, g, st_ref, a3, lvl, rowid):
    c = CHUNK
    g_hi = g.astype(BF16)
    g_lo = (g - g_hi.astype(F32)).astype(BF16)
    f = jnp.exp(_dot(a3, jnp.concatenate([g_hi, g_lo], axis=0)))
    sc = jnp.where(lvl == LEVELS, jnp.sum(q * k, axis=-1, keepdims=True), 0.0)
    for l in range(LEVELS):
        is_query = ((rowid >> l) & 1) == 1
        m = (jnp.where(is_query, q, k) * f[l * c:(l + 1) * c]).astype(BF16)
        sc = jnp.where(lvl == l, _dot_nt(m, m), sc)
    qd = (q * f[LEVELS * c:(LEVELS + 1) * c]).astype(BF16)
    kd = (k * f[(LEVELS + 1) * c:(LEVELS + 2) * c]).astype(BF16)
    vb = v.astype(BF16)
    st = st_ref[...]
    o = _dot(sc.astype(BF16), vb) + _dot_nt(qd, st.astype(BF16))
    chunk_decay = f[(LEVELS + 1) * c - 1:(LEVELS + 1) * c, :]
    st_ref[...] = st * chunk_decay + _dot_tn(vb, kd)
    return o


def _hgrn_kernel(q_ref, f_ref, i_ref, g_ref, lb_ref, on_ref, a3_ref, lvl_ref, o_ref, st_ref):
    @pl.when(pl.program_id(1) == 0)
    def _():
        st_ref[...] = jnp.zeros_like(st_ref)

    a3 = a3_ref[...]
    lvl = lvl_ref[...]
    rowid = lax.broadcasted_iota(jnp.int32, (CHUNK, 1), 0)
    for h in range(HEADS):
        sl = slice(h * DK, (h + 1) * DK)
        fl = f_ref[:, sl]
        lb = lb_ref[:, sl]
        log_f = jnp.log(lb + (1.0 - lb) * jax.nn.sigmoid(fl))
        key_in = (1.0 - lb) * jax.nn.sigmoid(-fl)
        q = _silu(q_ref[:, sl]) * (DK ** -0.5)
        o = _gla_chunk(q, key_in, i_ref[:, sl], log_f, st_ref.at[h], a3, lvl, rowid)
        o = _rms(o, on_ref[...])
        o_ref[:, sl] = (o * _silu(g_ref[:, sl])).astype(BF16)


def _hgrn(proj, lb, o_norm, batch, seq):
    n = proj.shape[0]
    nt = seq // CHUNK
    a3, lvl = _decay_tables()
    col = lambda j: pl.BlockSpec((CHUNK, HG_WIDTH), lambda b, t: (b * nt + t, j))
    const = lambda shape: pl.BlockSpec(shape, lambda b, t: (0, 0))
    return pl.pallas_call(
        _hgrn_kernel,
        grid=(batch, nt),
        in_specs=[col(0), col(1), col(2), col(3), const((1, HG_WIDTH)), const((1, DK)),
                  const(a3.shape), const(lvl.shape)],
        out_specs=pl.BlockSpec((CHUNK, HG_WIDTH), lambda b, t: (b * nt + t, 0)),
        out_shape=jax.ShapeDtypeStruct((n, HG_WIDTH), BF16),
        scratch_shapes=[pltpu.VMEM((HEADS, DK, DK), F32)],
        compiler_params=_params("parallel", "arbitrary"),
        name="hgrn",
    )(proj, proj, proj, proj, lb.reshape(1, HG_WIDTH), o_norm.reshape(1, DK), a3, lvl)


def _log_sigmoid(x):
    return jnp.minimum(x, 0.0) - jnp.log1p(jnp.exp(-jnp.abs(x)))


def _gla_kernel(q_ref, k_ref, v_ref, g_ref, a_ref, wa_ref, ba_ref, on_ref, a3_ref, lvl_ref, o_ref, st_ref):
    @pl.when(pl.program_id(1) == 0)
    def _():
        st_ref[...] = jnp.zeros_like(st_ref)

    a3 = a3_ref[...]
    lvl = lvl_ref[...]
    rowid = lax.broadcasted_iota(jnp.int32, (CHUNK, 1), 0)
    gate_logit = _dot(a_ref[...].astype(BF16), wa_ref[...]) + ba_ref[...]
    log_alpha = _log_sigmoid(gate_logit) * (1.0 / GLA_TAU)
    for h in range(HEADS):
        sl = slice(h * DK, (h + 1) * DK)
        sv = slice(h * GLA_DV, (h + 1) * GLA_DV)
        q = q_ref[:, sl] * (DK ** -0.5)
        o = _gla_chunk(q, k_ref[:, sl], v_ref[:, sv], log_alpha[:, sl], st_ref.at[h], a3, lvl, rowid)
        o = _rms(o, on_ref[...])
        o_ref[:, sv] = (o * _silu(g_ref[:, sv])).astype(BF16)


def _gla(proj, w_alpha, b_alpha, o_norm, batch, seq):
    n = proj.shape[0]
    nt = seq // CHUNK
    qk_w = HEADS * DK
    v_w = HEADS * GLA_DV
    a3, lvl = _decay_tables()
    row = lambda b, t: b * nt + t
    const = lambda shape: pl.BlockSpec(shape, lambda b, t: (0, 0))
    return pl.pallas_call(
        _gla_kernel,
        grid=(batch, nt),
        in_specs=[pl.BlockSpec((CHUNK, qk_w), lambda b, t: (row(b, t), 0)),
                  pl.BlockSpec((CHUNK, qk_w), lambda b, t: (row(b, t), 1)),
                  pl.BlockSpec((CHUNK, v_w), lambda b, t: (row(b, t), 1)),
                  pl.BlockSpec((CHUNK, v_w), lambda b, t: (row(b, t), 2)),
                  pl.BlockSpec((CHUNK, GLA_RANK_PAD), lambda b, t: (row(b, t), (2 * qk_w + 2 * v_w) // GLA_RANK_PAD)),
                  const((GLA_RANK_PAD, qk_w)), const((1, qk_w)), const((1, GLA_DV)),
                  const(a3.shape), const(lvl.shape)],
        out_specs=pl.BlockSpec((CHUNK, v_w), lambda b, t: (row(b, t), 0)),
        out_shape=jax.ShapeDtypeStruct((n, v_w), BF16),
        scratch_shapes=[pltpu.VMEM((HEADS, GLA_DV, DK), F32)],
        compiler_params=_params("parallel", "arbitrary"),
        name="gla",
    )(proj, proj, proj, proj, proj, w_alpha, b_alpha.reshape(1, qk_w), o_norm.reshape(1, GLA_DV), a3, lvl)


def _qk_norm(x, w2):
    lane = lax.broadcasted_iota(jnp.int32, x.shape, 1)
    first = lane < DA_HEAD_DIM
    sq = x * x
    ms1 = jnp.sum(jnp.where(first, sq, 0.0), axis=-1, keepdims=True) * (1.0 / DA_HEAD_DIM)
    ms2 = jnp.sum(jnp.where(first, 0.0, sq), axis=-1, keepdims=True) * (1.0 / DA_HEAD_DIM)
    r = jnp.where(first, lax.rsqrt(ms1 + EPS), lax.rsqrt(ms2 + EPS))
    return x * r * w2, first


def _attn_kernel(qi_tab, ki_tab, q_ref, k_ref, v_ref, qn_ref, kn_ref, lam_ref, sub_ref, o_ref,
                 qs_ref, m_ref, l_ref, acc_ref, *, lam_init):
    p = pl.program_id(2)
    qi = qi_tab[p]
    ki = ki_tab[p]
    tq = ATTN_BLOCK

    @pl.when(ki == 0)
    def _():
        qn, first = _qk_norm(q_ref[...], qn_ref[...])
        qn = qn * (DA_HEAD_DIM ** -0.5 * LOG2_E)
        q1 = jnp.where(first, qn, 0.0).T.astype(BF16)
        q2 = jnp.where(first, 0.0, qn).T.astype(BF16)
        for copy in range(2):
            qs_ref[copy, :, :tq] = q1
            qs_ref[copy, :, tq:] = q2
        m_ref[...] = jnp.full_like(m_ref, -jnp.inf)
        l_ref[...] = jnp.zeros_like(l_ref)
        acc_ref[...] = jnp.zeros_like(acc_ref)

    def step(masked):
        kn, _ = _qk_norm(k_ref[...], kn_ref[...])
        kn = kn.astype(BF16)

        def scores(copy):
            s = _dot(kn, qs_ref[copy])
            if masked:
                key = lax.broadcasted_iota(jnp.int32, s.shape, 0)
                qry = lax.broadcasted_iota(jnp.int32, s.shape, 1)
                qry = jnp.where(qry >= tq, qry - tq, qry)
                s = jnp.where(key <= qry, s, -jnp.inf)
            return s

        m_prev = m_ref[...]
        m_new = jnp.maximum(m_prev, jnp.max(scores(0), axis=0, keepdims=True))
        alpha = jnp.exp2(m_prev - m_new)
        pr = jnp.exp2(scores(1) - m_new)
        l_ref[...] = alpha * l_ref[...] + jnp.sum(pr, axis=0, keepdims=True)
        acc_ref[...] = alpha * acc_ref[...] + _dot_tn(v_ref[...].astype(BF16), pr.astype(BF16))
        m_ref[...] = m_new

    @pl.when(ki < qi)
    def _():
        step(False)

    @pl.when(ki == qi)
    def _():
        step(True)
        lv = lam_ref[...]
        lam = (jnp.exp(jnp.sum(lv[0:1] * lv[1:2], axis=-1, keepdims=True))
               - jnp.exp(jnp.sum(lv[2:3] * lv[3:4], axis=-1, keepdims=True)) + lam_init)
        o1 = acc_ref[:, :tq] / l_ref[:, :tq]
        o2 = acc_ref[:, tq:] / l_ref[:, tq:]
        o = o1 - lam * o2
        o = o * lax.rsqrt(jnp.mean(o * o, axis=0, keepdims=True) + EPS) * sub_ref[...] * (1.0 - lam_init)
        o_ref[...] = o.T.astype(BF16)


def _diff_attn(proj, q_norm, k_norm, lam_vecs, subln, lam_init, batch, seq):
    n = proj.shape[0]
    blk = ATTN_BLOCK
    nq = seq // blk
    dv = 2 * DA_HEAD_DIM
    pairs = [(qi, ki) for qi in range(nq) for ki in range(qi + 1)]
    qi_tab = jnp.asarray([p[0] for p in pairs], jnp.int32)
    ki_tab = jnp.asarray([p[1] for p in pairs], jnp.int32)
    q_col = (4 * HG_WIDTH) // dv
    const = lambda shape: pl.BlockSpec(shape, lambda b, h, p, qt, kt: (0, 0))
    grid_spec = pltpu.PrefetchScalarGridSpec(
        num_scalar_prefetch=2,
        grid=(batch, HEADS, len(pairs)),
        in_specs=[pl.BlockSpec((blk, dv), lambda b, h, p, qt, kt: (b * nq + qt[p], q_col + h)),
                  pl.BlockSpec((blk, dv), lambda b, h, p, qt, kt: (b * nq + kt[p], q_col + HEADS + h)),
                  pl.BlockSpec((blk, dv), lambda b, h, p, qt, kt: (b * nq + kt[p], q_col + 2 * HEADS + h)),
                  const((1, dv)), const((1, dv)), const((4, DA_HEAD_DIM)), const((dv, 1))],
        out_specs=pl.BlockSpec((blk, dv), lambda b, h, p, qt, kt: (b * nq + qt[p], h)),
        scratch_shapes=[pltpu.VMEM((2, dv, 2 * blk), BF16),
                        pltpu.VMEM((1, 2 * blk), F32),
                        pltpu.VMEM((1, 2 * blk), F32),
                        pltpu.VMEM((dv, 2 * blk), F32)])
    return pl.pallas_call(
        functools.partial(_attn_kernel, lam_init=lam_init),
        grid_spec=grid_spec,
        out_shape=jax.ShapeDtypeStruct((n, HEADS * dv), BF16),
        compiler_params=_params("parallel", "parallel", "arbitrary"),
        name="diff_attn",
    )(qi_tab, ki_tab, proj, proj, proj,
      jnp.tile(q_norm, 2).reshape(1, -1), jnp.tile(k_norm, 2).reshape(1, -1),
      lam_vecs, subln.reshape(-1, 1))


def _outproj_router_kernel(*refs, n_a):
    a_refs = refs[:n_a]
    (w_ref, x_ref, nw_ref, wr_hi_ref, wr_lo_ref, br_ref, tri_ref,
     x2_ref, meta_ref, cnt_ref, carry_ref) = refs[n_a:]

    @pl.when(pl.program_id(0) == 0)
    def _():
        carry_ref[...] = jnp.zeros_like(carry_ref)

    x2 = _load_tokens(x_ref)
    off = 0
    for a_ref in a_refs:
        kk = a_ref.shape[1]
        x2 = x2 + _dot(a_ref[...], w_ref[off:off + kk, :])
        off += kk
    _store_tokens(x2_ref, x2)

    h = _rms(x2, nw_ref[...])
    h_hi = h.astype(BF16)
    h_lo = (h - h_hi.astype(F32)).astype(BF16)
    wr_hi = wr_hi_ref[...]
    lt = _dot_nt(wr_hi, h_hi) + _dot_nt(wr_hi, h_lo) + _dot_nt(wr_lo_ref[...], h_hi) + br_ref[...]
    row = lambda r: lt[r:r + 1, :]

    gl = [row(g) for g in range(N_GROUPS)]
    gmax = functools.reduce(jnp.maximum, gl)
    gsel = jnp.where(gl[0] == gmax, 0.0, jnp.where(gl[1] == gmax, 1.0, jnp.where(gl[2] == gmax, 2.0, 3.0)))
    g_w = 1.0 / functools.reduce(jnp.add, [jnp.exp(x - gmax) for x in gl])

    el = []
    for e in range(EXPERTS_PER_GROUP):
        v = row(N_GROUPS + e)
        for g in range(1, N_GROUPS):
            v = jnp.where(gsel == float(g), row(N_GROUPS + EXPERTS_PER_GROUP * g + e), v)
        el.append(v)
    emax = functools.reduce(jnp.maximum, el)
    sel = []
    for e in range(EXPERTS_PER_GROUP):
        ahead = jnp.zeros_like(emax)
        for o in range(EXPERTS_PER_GROUP):
            if o < e:
                ahead = ahead + jnp.where(el[o] >= el[e], 1.0, 0.0)
            elif o > e:
                ahead = ahead + jnp.where(el[o] > el[e], 1.0, 0.0)
        sel.append(ahead < 2.0)
    pe = [jnp.where(sel[e], jnp.exp(el[e] - emax), 0.0) for e in range(EXPERTS_PER_GROUP)]
    scale = g_w / functools.reduce(jnp.add, pe)
    zero = jnp.zeros_like(emax)
    seen = zero
    lo_idx, hi_idx, c_lo, c_hi = zero, zero, zero, zero
    for e in range(EXPERTS_PER_GROUP):
        is_lo = sel[e] & (seen == 0.0)
        is_hi = sel[e] & (seen == 1.0)
        lo_idx = jnp.where(is_lo, float(e), lo_idx)
        hi_idx = jnp.where(is_hi, float(e), hi_idx)
        c_lo = jnp.where(is_lo, pe[e] * scale, c_lo)
        c_hi = jnp.where(is_hi, pe[e] * scale, c_hi)
        seen = seen + jnp.where(sel[e], 1.0, 0.0)
    pair = lo_idx * (7.0 - lo_idx) * 0.5 + (hi_idx - lo_idx - 1.0)
    bucket = gsel * float(N_PAIRS) + pair

    bid = lax.broadcasted_iota(jnp.int32, (ROUTER_ROWS, bucket.shape[1]), 0).astype(F32)
    onehot = jnp.where(bid == bucket, 1.0, 0.0)
    before = _dot(onehot.astype(BF16), tri_ref[...]) + carry_ref[...]
    rank = jnp.sum(onehot * before, axis=0, keepdims=True)
    carry = carry_ref[...] + jnp.sum(onehot, axis=1, keepdims=True)
    carry_ref[...] = carry
    cnt_ref[...] = carry
    meta_ref[...] = jnp.concatenate([bucket, rank, c_lo, c_hi, zero, zero, zero, zero], axis=0)


def _outproj_router(a_list, w_out, x, n, norm_w, w_group, b_group, w_router, b_router):
    d = D_MODEL
    tm = ROW_TILE
    tok = lambda: pl.BlockSpec((tm * LANE_TILES, LANE), lambda i: (i, 0))
    wr =jnp.concatenate([w_group, w_router.transpose(1, 0, 2).reshape(d, N_EXPERTS)], axis=1)
    wr_t = jnp.zeros((ROUTER_ROWS, d), F32).at[:wr.shape[1]].set(wr.T)
    wr_hi = wr_t.astype(BF16)
    wr_lo = (wr_t - wr_hi.astype(F32)).astype(BF16)
    br = jnp.zeros((ROUTER_ROWS, 1), F32).at[:wr.shape[1], 0].set(
        jnp.concatenate([b_group, b_router.reshape(-1)]))
    tri = jnp.asarray(np.triu(np.ones((tm, tm), np.float32), 1), BF16)
    const = lambda shape: pl.BlockSpec(shape, lambda i: (0, 0))
    in_specs = [pl.BlockSpec((tm, a.shape[1]), lambda i: (i, 0)) for a in a_list]
    in_specs += [const((d, d)), tok(), const((1, d)),
                 const((ROUTER_ROWS, d)), const((ROUTER_ROWS, d)), const((ROUTER_ROWS, 1)), const((tm, tm))]
    return pl.pallas_call(
        functools.partial(_outproj_router_kernel, n_a=len(a_list)),
        grid=(n // tm,),
        in_specs=in_specs,
        out_specs=[tok(),
                   pl.BlockSpec((8, tm), lambda i: (0, i)),
                   const((ROUTER_ROWS, 1))],
        out_shape=[jax.ShapeDtypeStruct((n * LANE_TILES, LANE), F32),
                   jax.ShapeDtypeStruct((8, n), F32),
                   jax.ShapeDtypeStruct((ROUTER_ROWS, 1), F32)],
        scratch_shapes=[pltpu.VMEM((ROUTER_ROWS, 1), F32)],
        compiler_params=_params("arbitrary"),
        name="outproj_router",
    )(*a_list, w_out, x, norm_w.reshape(1, d), wr_hi, wr_lo, br, tri)


def _gather_tokens(table, idx):
    m = idx.shape[0]
    info = plsc.get_sparse_core_info()
    n_workers = info.num_cores * info.num_subcores
    per_worker = m // n_workers
    n_chunks = per_worker // SC_CHUNK
    assert per_worker * n_workers == m and n_chunks * SC_CHUNK == per_worker
    mesh = plsc.VectorSubcoreMesh(core_axis_name="core", subcore_axis_name="subcore")

    @functools.partial(
        pl.kernel, mesh=mesh,
        out_type=jax.ShapeDtypeStruct((m,) + table.shape[1:], table.dtype),
        scratch_types=[pltpu.VMEM((SC_CHUNK,), jnp.int32),
                       pltpu.VMEM((SC_CHUNK,) + table.shape[1:], table.dtype),
                       pltpu.SemaphoreType.DMA],
        name="gather_tokens")
    def gather(table_hbm, idx_hbm, out_hbm, idx_v, rows_v, sem):
        worker = lax.axis_index("subcore") * info.num_cores + lax.axis_index("core")

        @pl.loop(0, n_chunks)
        def _(j):
            off = pl.multiple_of(worker * per_worker + j * SC_CHUNK, SC_CHUNK)
            pltpu.sync_copy(idx_hbm.at[pl.ds(off, SC_CHUNK)], idx_v)
            pltpu.async_copy(table_hbm.at[idx_v], rows_v, sem).wait()
            pltpu.sync_copy(rows_v, out_hbm.at[pl.ds(off, SC_CHUNK)])

    return gather(table, idx)


def _moe_kernel(tid, e_lo, e_hi, n_tiles, x_ref, cw_ref, nw_ref, gu_lo_ref, gu_hi_ref, d_lo_ref, d_hi_ref, o_ref):
    del tid, e_lo, e_hi

    @pl.when(pl.program_id(0) < n_tiles[0])
    def _():
        x = _load_tokens(x_ref)
        h = _rms(x, nw_ref[...]).astype(BF16)
        cw = cw_ref[...]

        def expert(gu_ref, d_ref):
            gu = _dot(h, gu_ref[0])
            act = _silu(gu[:, :D_EXPERT]) * gu[:, D_EXPERT:]
            return _dot(act.astype(BF16), d_ref[0])

        y = cw[:, 1:2] * expert(gu_lo_ref, d_lo_ref) + cw[:, 2:3] * expert(gu_hi_ref, d_hi_ref)
        _store_tokens(o_ref, x + y)


def _moe(x2, n, meta, cnt, norm_w, w_gu, w_down):
    d = D_MODEL
    lt = LANE_TILES
    tm = MOE_TILE
    max_tiles = n // tm + N_BUCKETS
    n_slots = max_tiles * tm
    counts = cnt[:N_BUCKETS, 0].astype(jnp.int32)
    tiles_per_bucket = (counts + tm - 1) // tm
    tile_end = jnp.cumsum(tiles_per_bucket)
    row_start = (tile_end - tiles_per_bucket) * tm
    n_tiles = tile_end[-1]
    in_bucket = meta[0][:, None] == jnp.arange(N_BUCKETS, dtype=F32)[None, :]
    dest = meta[1].astype(jnp.int32) + jnp.sum(jnp.where(in_bucket, row_start[None, :], 0), axis=1)
    packed = jnp.stack([jnp.arange(n, dtype=F32), meta[2], meta[3], jnp.ones((n,), F32)], axis=1)
    slots = jnp.zeros((n_slots, 4), F32).at[dest].set(packed, unique_indices=True, mode="promise_in_bounds")
    src = jnp.where(slots[:, 3] > 0.0, slots[:, 0].astype(jnp.int32), jnp.arange(n_slots, dtype=jnp.int32) % n)
    x_sorted = _gather_tokens(x2.reshape(n, lt, LANE), src).reshape(n_slots * lt, LANE)

    tile_ids = jnp.minimum(jnp.arange(max_tiles, dtype=jnp.int32), n_tiles - 1)
    tile_bucket = jnp.sum((tile_ids[:, None] >= tile_end[None, :]).astype(jnp.int32), axis=1)
    group = tile_bucket // N_PAIRS
    pair = tile_bucket % N_PAIRS
    pair_lo = jnp.sum(jnp.where(pair[:, None] == jnp.arange(N_PAIRS)[None, :], jnp.asarray(PAIR_LO, jnp.int32)[None, :], 0), axis=1)
    pair_hi = jnp.sum(jnp.where(pair[:, None] == jnp.arange(N_PAIRS)[None, :], jnp.asarray(PAIR_HI, jnp.int32)[None, :], 0), axis=1)
    e_lo = group * EXPERTS_PER_GROUP + pair_lo
    e_hi = group * EXPERTS_PER_GROUP + pair_hi

    grid_spec = pltpu.PrefetchScalarGridSpec(
        num_scalar_prefetch=4,
        grid=(max_tiles,),
        in_specs=[pl.BlockSpec((tm * lt, LANE), lambda t, tid, lo, hi, nt: (tid[t], 0)),
                  pl.BlockSpec((tm, 4), lambda t, tid, lo, hi, nt: (tid[t], 0)),
                  pl.BlockSpec((1, d), lambda t, tid, lo, hi, nt: (0, 0)),
                  pl.BlockSpec((1, d, 2 * D_EXPERT), lambda t, tid, lo, hi, nt: (lo[t], 0, 0)),
                  pl.BlockSpec((1, d, 2 * D_EXPERT), lambda t, tid, lo, hi, nt: (hi[t], 0, 0)),
                  pl.BlockSpec((1, D_EXPERT, d), lambda t, tid, lo, hi, nt: (lo[t], 0, 0)),
                  pl.BlockSpec((1, D_EXPERT, d), lambda t, tid, lo, hi, nt: (hi[t], 0, 0))],
        out_specs=pl.BlockSpec((tm * lt, LANE), lambda t, tid, lo, hi, nt: (tid[t], 0)))
    out_sorted = pl.pallas_call(
        _moe_kernel,
        grid_spec=grid_spec,
        out_shape=jax.ShapeDtypeStruct((n_slots * lt, LANE), F32),
        compiler_params=_params("arbitrary"),
        name="moe",
    )(tile_ids, e_lo, e_hi, n_tiles.reshape(1), x_sorted, slots, norm_w.reshape(1, d), w_gu, w_gu, w_down, w_down)
    return _gather_tokens(out_sorted.reshape(n_slots, lt, LANE), dest).reshape(n * lt, LANE)


def kernel(x, attn_norm, ffn_norm, even_w_in, even_w_out, hg_lb_logits, hg_o_norm, da_q_norm, da_k_norm, da_lambda, da_subln, odd_w_in, gla_w_alpha, gla_b_alpha, gla_o_norm, odd_w_out, moe_w_group, moe_b_group, moe_w_router, moe_b_router, moe_w_gate, moe_w_up, moe_w_down):
    batch, seq, d = x.shape
    depth = attn_norm.shape[0]
    n = batch * seq
    xf = x.reshape(n * LANE_TILES, LANE)

    lb_sm = jax.nn.softmax(hg_lb_logits.astype(F32), axis=0)
    lb_all = jnp.cumsum(lb_sm, axis=0) - lb_sm[0:1]
    rank = gla_w_alpha.shape[1]

    for layer in range(depth):
        j = layer // 2
        if layer % 2 == 0:
            lam_init = 0.8 - 0.6 * math.exp(-0.3 * layer)
            proj = _inproj(xf, n, attn_norm[layer], even_w_in[j].astype(BF16))
            o_a = _hgrn(proj, lb_all[j], hg_o_norm[j], batch, seq)
            o_b = _diff_attn(proj, da_q_norm[j], da_k_norm[j], da_lambda[j], da_subln[j], lam_init, batch, seq)
            mix = [o_a, o_b]
            w_out = even_w_out[j]
        else:
            w_in = jnp.pad(odd_w_in[j], ((0, 0), (0, GLA_RANK_PAD - rank))).astype(BF16)
            w_alpha = jnp.pad(gla_w_alpha[j], ((0, GLA_RANK_PAD - rank), (0, 0))).astype(BF16)
            proj = _inproj(xf, n, attn_norm[layer], w_in)
            mix = [_gla(proj, w_alpha, gla_b_alpha[j], gla_o_norm[j], batch, seq)]
            w_out = odd_w_out[j]
        x2, meta, cnt = _outproj_router(mix, w_out.astype(BF16), xf, n, ffn_norm[layer], moe_w_group[layer],
                                        moe_b_group[layer], moe_w_router[layer], moe_b_router[layer])
        w_gu = jnp.concatenate([moe_w_gate[layer], moe_w_up[layer]], axis=-1).astype(BF16)
        xf = _moe(x2, n, meta, cnt, ffn_norm[layer], w_gu, moe_w_down[layer].astype(BF16))
    return xf.reshape(batch, seq, d)
```

```python
import functools
import math

import numpy as np
import jax
import jax.numpy as jnp
from jax import lax
from jax.experimental import pallas as pl
from jax.experimental.pallas import tpu as pltpu
from jax.experimental.pallas import tpu_sc as plsc

F32 = jnp.float32
BF16 = jnp.bfloat16
EPS = 1e-6
LOG2_E = 1.4426950408889634

D_MODEL = 1024
LANE = 128
LANE_TILES = D_MODEL // LANE
HEADS = 4
DK = 128
HG_WIDTH = 512
DA_HEAD_DIM = 64
GLA_DV = 256
GLA_RANK_PAD = 128
GLA_TAU = 16.0
N_GROUPS = 4
EXPERTS_PER_GROUP = 4
N_EXPERTS = 16
D_EXPERT = 512
N_PAIRS = 6
N_BUCKETS = N_GROUPS * N_PAIRS
ROUTER_ROWS = 32

CHUNK = 128
LEVELS = 7
ATTN_BLOCK = 512
ROW_TILE = 512
MOE_TILE = 256
SC_CHUNK = 32
VMEM_LIMIT = 56 * 1024 * 1024

PAIR_LO = (0, 0, 0, 1, 1, 2)
PAIR_HI = (1, 2, 3, 2, 3, 3)


def _dot(a, b):
    return jnp.dot(a, b, preferred_element_type=F32)


def _dot_nt(a, b):
    return lax.dot_general(a, b, (((1,), (1,)), ((), ())), preferred_element_type=F32)


def _dot_tn(a, b):
    return lax.dot_general(a, b, (((0,), (0,)), ((), ())), preferred_element_type=F32)


def _rms(x, w):
    return x * lax.rsqrt(jnp.mean(x * x, axis=-1, keepdims=True) + EPS) * w


def _silu(x):
    return x * jax.nn.sigmoid(x)


def _params(*sem):
    return pltpu.CompilerParams(dimension_semantics=sem, vmem_limit_bytes=VMEM_LIMIT)


def _load_tokens(ref):
    if ref.shape[1] == D_MODEL:
        return ref[...]
    rows = ref.shape[0] // LANE_TILES
    return jnp.concatenate([ref[pl.ds(c, rows, stride=LANE_TILES), :] for c in range(LANE_TILES)], axis=1)


def _token_spec(x, rows):
    shape = (rows, D_MODEL) if x.shape[1] == D_MODEL else (rows * LANE_TILES, LANE)
    return pl.BlockSpec(shape, lambda i: (i, 0))


def _store_tokens(ref, val):
    rows = ref.shape[0] // LANE_TILES
    for c in range(LANE_TILES):
        ref[pl.ds(c, rows, stride=LANE_TILES), :] = val[:, c * LANE:(c + 1) * LANE]


def _inproj_kernel(x_ref, nw_ref, w_ref, o_ref):
    h = _rms(_load_tokens(x_ref), nw_ref[...]).astype(BF16)
    o_ref[...] = _dot(h, w_ref[...])


def _inproj(x, n, norm_w, w):
    d = D_MODEL
    n_out = w.shape[1]
    return pl.pallas_call(
        _inproj_kernel,
        grid=(n // ROW_TILE,),
        in_specs=[_token_spec(x, ROW_TILE),
                  pl.BlockSpec((1, d), lambda i: (0, 0)),
                  pl.BlockSpec((d, n_out), lambda i: (0, 0))],
        out_specs=pl.BlockSpec((ROW_TILE, n_out), lambda i: (i, 0)),
        out_shape=jax.ShapeDtypeStruct((n, n_out), F32),
        compiler_params=_params("parallel"),
        name="inproj",
    )(x, norm_w.reshape(1, d), w)


def _decay_tables():
    c = CHUNK
    a = np.zeros((LEVELS + 2, c, c), np.float32)
    t = np.arange(c)
    for l in range(LEVELS):
        s = 1 << l
        for r in range(c):
            mid = (r // (2 * s)) * 2 * s + s - 1
            if (r >> l) & 1:
                a[l, r] = (t > mid) & (t <= r)
            else:
                a[l, r] = (t > r) & (t <= mid)
    a[LEVELS] = t[None, :] <= t[:, None]
    a[LEVELS + 1] = t[None, :] > t[:, None]
    a = a.reshape((LEVELS + 2) * c, c)
    a3 = np.concatenate([a, a], axis=1)
    i, j = np.meshgrid(t, t, indexing="ij")
    x = np.maximum(i ^ j, 1)
    lvl = np.where(j < i, np.floor(np.log2(x)).astype(np.int32), np.where(i == j, LEVELS, -1))
    return jnp.asarray(a3, BF16), jnp.asarray(lvl, jnp.int32)


def _gla_chunk(q, k, v, g, st_ref, a3, lvl, rowid):
    c = CHUNK
    g_hi = g.astype(BF16)
    g_lo = (g - g_hi.astype(F32)).astype(BF16)
    f = jnp.exp(_dot(a3, jnp.concatenate([g_hi, g_lo], axis=0)))
    sc = jnp.where(lvl == LEVELS, jnp.sum(q * k, axis=-1, keepdims=True), 0.0)
    for l in range(LEVELS):
        is_query = ((rowid >> l) & 1) == 1
        m = (jnp.where(is_query, q, k) * f[l * c:(l + 1) * c]).astype(BF16)
        sc = jnp.where(lvl == l, _dot_nt(m, m), sc)
    qd = (q * f[LEVELS * c:(LEVELS + 1) * c]).astype(BF16)
    kd = (k * f[(LEVELS + 1) * c:(LEVELS + 2) * c]).astype(BF16)
    vb = v.astype(BF16)
    st = st_ref[...]
    o = _dot(sc.astype(BF16), vb) + _dot_nt(qd, st.astype(BF16))
    chunk_decay = f[(LEVELS + 1) * c - 1:(LEVELS + 1) * c, :]
    st_ref[...] = st * chunk_decay + _dot_tn(vb, kd)
    return o


def _hgrn_kernel(q_ref, f_ref, i_ref, g_ref, lb_ref, on_ref, a3_ref, lvl_ref, o_ref, st_ref):
    @pl.when(pl.program_id(1) == 0)
    def _():
        st_ref[...] = jnp.zeros_like(st_ref)

    a3 = a3_ref[...]
    lvl = lvl_ref[...]
    rowid = lax.broadcasted_iota(jnp.int32, (CHUNK, 1), 0)
    for h in range(HEADS):
        sl = slice(h * DK, (h + 1) * DK)
        fl = f_ref[:, sl]
        lb = lb_ref[:, sl]
        log_f = jnp.log(lb + (1.0 - lb) * jax.nn.sigmoid(fl))
        key_in = (1.0 - lb) * jax.nn.sigmoid(-fl)
        q = _silu(q_ref[:, sl]) * (DK ** -0.5)
        o = _gla_chunk(q, key_in, i_ref[:, sl], log_f, st_ref.at[h], a3, lvl, rowid)
        o = _rms(o, on_ref[...])
        o_ref[:, sl] = (o * _silu(g_ref[:, sl])).astype(BF16)


def _hgrn(proj, lb, o_norm, batch, seq):
    n = proj.shape[0]
    nt = seq // CHUNK
    a3, lvl = _decay_tables()
    col = lambda j: pl.BlockSpec((CHUNK, HG_WIDTH), lambda b, t: (b * nt + t, j))
    const = lambda shape: pl.BlockSpec(shape, lambda b, t: (0, 0))
    return pl.pallas_call(
        _hgrn_kernel,
        grid=(batch, nt),
        in_specs=[col(0), col(1), col(2), col(3), const((1, HG_WIDTH)), const((1, DK)),
                  const(a3.shape), const(lvl.shape)],
        out_specs=pl.BlockSpec((CHUNK, HG_WIDTH), lambda b, t: (b * nt + t, 0)),
        out_shape=jax.ShapeDtypeStruct((n, HG_WIDTH), BF16),
        scratch_shapes=[pltpu.VMEM((HEADS, DK, DK), F32)],
        compiler_params=_params("parallel", "arbitrary"),
        name="hgrn",
    )(proj, proj, proj, proj, lb.reshape(1, HG_WIDTH), o_norm.reshape(1, DK), a3, lvl)


def _log_sigmoid(x):
    return jnp.minimum(x, 0.0) - jnp.log1p(jnp.exp(-jnp.abs(x)))


def _gla_kernel(q_ref, k_ref, v_ref, g_ref, a_ref, wa_ref, ba_ref, on_ref, a3_ref, lvl_ref, o_ref, st_ref):
    @pl.when(pl.program_id(1) == 0)
    def _():
        st_ref[...] = jnp.zeros_like(st_ref)

    a3 = a3_ref[...]
    lvl = lvl_ref[...]
    rowid = lax.broadcasted_iota(jnp.int32, (CHUNK, 1), 0)
    gate_logit = _dot(a_ref[...].astype(BF16), wa_ref[...]) + ba_ref[...]
    log_alpha = _log_sigmoid(gate_logit) * (1.0 / GLA_TAU)
    for h in range(HEADS):
        sl = slice(h * DK, (h + 1) * DK)
        sv = slice(h * GLA_DV, (h + 1) * GLA_DV)
        q = q_ref[:, sl] * (DK ** -0.5)
        o = _gla_chunk(q, k_ref[:, sl], v_ref[:, sv], log_alpha[:, sl], st_ref.at[h], a3, lvl, rowid)
        o = _rms(o, on_ref[...])
        o_ref[:, sv] = (o * _silu(g_ref[:, sv])).astype(BF16)


def _gla(proj, w_alpha, b_alpha, o_norm, batch, seq):
    n = proj.shape[0]
    nt = seq // CHUNK
    qk_w = HEADS * DK
    v_w = HEADS * GLA_DV
    a3, lvl = _decay_tables()
    row = lambda b, t: b * nt + t
    const = lambda shape: pl.BlockSpec(shape, lambda b, t: (0, 0))
    return pl.pallas_call(
        _gla_kernel,
        grid=(batch, nt),
        in_specs=[pl.BlockSpec((CHUNK, qk_w), lambda b, t: (row(b, t), 0)),
                  pl.BlockSpec((CHUNK, qk_w), lambda b, t: (row(b, t), 1)),
                  pl.BlockSpec((CHUNK, v_w), lambda b, t: (row(b, t), 1)),
                  pl.BlockSpec((CHUNK, v_w), lambda b, t: (row(b, t), 2)),
                  pl.BlockSpec((CHUNK, GLA_RANK_PAD), lambda b, t: (row(b, t), (2 * qk_w + 2 * v_w) // GLA_RANK_PAD)),
                  const((GLA_RANK_PAD, qk_w)), const((1, qk_w)), const((1, GLA_DV)),
                  const(a3.shape), const(lvl.shape)],
        out_specs=pl.BlockSpec((CHUNK, v_w), lambda b, t: (row(b, t), 0)),
        out_shape=jax.ShapeDtypeStruct((n, v_w), BF16),
        scratch_shapes=[pltpu.VMEM((HEADS, GLA_DV, DK), F32)],
        compiler_params=_params("parallel", "arbitrary"),
        name="gla",
    )(proj, proj, proj, proj, proj, w_alpha, b_alpha.reshape(1, qk_w), o_norm.reshape(1, GLA_DV), a3, lvl)


def _qk_norm(x, w2):
    lane = lax.broadcasted_iota(jnp.int32, x.shape, 1)
    first = lane < DA_HEAD_DIM
    sq = x * x
    ms1 = jnp.sum(jnp.where(first, sq, 0.0), axis=-1, keepdims=True) * (1.0 / DA_HEAD_DIM)
    ms2 = jnp.sum(jnp.where(first, 0.0, sq), axis=-1, keepdims=True) * (1.0 / DA_HEAD_DIM)
    r = jnp.where(first, lax.rsqrt(ms1 + EPS), lax.rsqrt(ms2 + EPS))
    return x * r * w2, first


def _attn_kernel(qi_tab, ki_tab, q_ref, k_ref, v_ref, qn_ref, kn_ref, lam_ref, sub_ref, o_ref,
                 qs_ref, m_ref, l_ref, acc_ref, *, lam_init):
    p = pl.program_id(2)
    qi = qi_tab[p]
    ki = ki_tab[p]
    tq = ATTN_BLOCK

    @pl.when(ki == 0)
    def _():
        qn, first = _qk_norm(q_ref[...], qn_ref[...])
        qn = qn * (DA_HEAD_DIM ** -0.5 * LOG2_E)
        q1 = jnp.where(first, qn, 0.0).T.astype(BF16)
        q2 = jnp.where(first, 0.0, qn).T.astype(BF16)
        for copy in range(2):
            qs_ref[copy, :, :tq] = q1
            qs_ref[copy, :, tq:] = q2
        m_ref[...] = jnp.full_like(m_ref, -jnp.inf)
        l_ref[...] = jnp.zeros_like(l_ref)
        acc_ref[...] = jnp.zeros_like(acc_ref)

    def step(masked):
        kn, _ = _qk_norm(k_ref[...], kn_ref[...])
        kn = kn.astype(BF16)

        def scores(copy):
            s = _dot(kn, qs_ref[copy])
            if masked:
                key = lax.broadcasted_iota(jnp.int32, s.shape, 0)
                qry = lax.broadcasted_iota(jnp.int32, s.shape, 1)
                qry = jnp.where(qry >= tq, qry - tq, qry)
                s = jnp.where(key <= qry, s, -jnp.inf)
            return s

        m_prev = m_ref[...]
        m_new = jnp.maximum(m_prev, jnp.max(scores(0), axis=0, keepdims=True))
        alpha = jnp.exp2(m_prev - m_new)
        pr = jnp.exp2(scores(1) - m_new)
        l_ref[...] = alpha * l_ref[...] + jnp.sum(pr, axis=0, keepdims=True)
        acc_ref[...] = alpha * acc_ref[...] + _dot_tn(v_ref[...].astype(BF16), pr.astype(BF16))
        m_ref[...] = m_new

    @pl.when(ki < qi)
    def _():
        step(False)

    @pl.when(ki == qi)
    def _():
        step(True)
        lv = lam_ref[...]
        lam = (jnp.exp(jnp.sum(lv[0:1] * lv[1:2], axis=-1, keepdims=True))
               - jnp.exp(jnp.sum(lv[2:3] * lv[3:4], axis=-1, keepdims=True)) + lam_init)
        o1 = acc_ref[:, :tq] / l_ref[:, :tq]
        o2 = acc_ref[:, tq:] / l_ref[:, tq:]
        o = o1 - lam * o2
        o = o * lax.rsqrt(jnp.mean(o * o, axis=0, keepdims=True) + EPS) * sub_ref[...] * (1.0 - lam_init)
        o_ref[...] = o.T.astype(BF16)


def _diff_attn(proj, q_norm, k_norm, lam_vecs, subln, lam_init, batch, seq):
    n = proj.shape[0]
    blk = ATTN_BLOCK
    nq = seq // blk
    dv = 2 * DA_HEAD_DIM
    pairs = [(qi, ki) for qi in range(nq) for ki in range(qi + 1)]
    qi_tab = jnp.asarray([p[0] for p in pairs], jnp.int32)
    ki_tab = jnp.asarray([p[1] for p in pairs], jnp.int32)
    q_col = (4 * HG_WIDTH) // dv
    const = lambda shape: pl.BlockSpec(shape, lambda b, h, p, qt, kt: (0, 0))
    grid_spec = pltpu.PrefetchScalarGridSpec(
        num_scalar_prefetch=2,
        grid=(batch, HEADS, len(pairs)),
        in_specs=[pl.BlockSpec((blk, dv), lambda b, h, p, qt, kt: (b * nq + qt[p], q_col + h)),
                  pl.BlockSpec((blk, dv), lambda b, h, p, qt, kt: (b * nq + kt[p], q_col + HEADS + h)),
                  pl.BlockSpec((blk, dv), lambda b, h, p, qt, kt: (b * nq + kt[p], q_col + 2 * HEADS + h)),
                  const((1, dv)), const((1, dv)), const((4, DA_HEAD_DIM)), const((dv, 1))],
        out_specs=pl.BlockSpec((blk, dv), lambda b, h, p, qt, kt: (b * nq + qt[p], h)),
        scratch_shapes=[pltpu.VMEM((2, dv, 2 * blk), BF16),
                        pltpu.VMEM((1, 2 * blk), F32),
                        pltpu.VMEM((1, 2 * blk), F32),
                        pltpu.VMEM((dv, 2 * blk), F32)])
    return pl.pallas_call(
        functools.partial(_attn_kernel, lam_init=lam_init),
        grid_spec=grid_spec,
        out_shape=jax.ShapeDtypeStruct((n, HEADS * dv), BF16),
        compiler_params=_params("parallel", "parallel", "arbitrary"),
        name="diff_attn",
    )(qi_tab, ki_tab, proj, proj, proj,
      jnp.tile(q_norm, 2).reshape(1, -1), jnp.tile(k_norm, 2).reshape(1, -1),
      lam_vecs, subln.reshape(-1, 1))


def _outproj_router_kernel(*refs, n_a):
    a_refs = refs[:n_a]
    (w_ref, x_ref, nw_ref, wr_hi_ref, wr_lo_ref, br_ref, tri_ref,
     x2_ref, meta_ref, cnt_ref, carry_ref) = refs[n_a:]

    @pl.when(pl.program_id(0) == 0)
    def _():
        carry_ref[...] = jnp.zeros_like(carry_ref)

    x2 = _load_tokens(x_ref)
    off = 0
    for a_ref in a_refs:
        kk = a_ref.shape[1]
        x2 = x2 + _dot(a_ref[...], w_ref[off:off + kk, :])
        off += kk
    _store_tokens(x2_ref, x2)

    h = _rms(x2, nw_ref[...])
    h_hi = h.astype(BF16)
    h_lo = (h - h_hi.astype(F32)).astype(BF16)
    wr_hi = wr_hi_ref[...]
    lt = _dot_nt(wr_hi, h_hi) + _dot_nt(wr_hi, h_lo) + _dot_nt(wr_lo_ref[...], h_hi) + br_ref[...]
    row = lambda r: lt[r:r + 1, :]

    gl = [row(g) for g in range(N_GROUPS)]
    gmax = functools.reduce(jnp.maximum, gl)
    gsel = jnp.where(gl[0] == gmax, 0.0, jnp.where(gl[1] == gmax, 1.0, jnp.where(gl[2] == gmax, 2.0, 3.0)))
    g_w = 1.0 / functools.reduce(jnp.add, [jnp.exp(x - gmax) for x in gl])

    el = []
    for e in range(EXPERTS_PER_GROUP):
        v = row(N_GROUPS + e)
        for g in range(1, N_GROUPS):
            v = jnp.where(gsel == float(g), row(N_GROUPS + EXPERTS_PER_GROUP * g + e), v)
        el.append(v)
    emax = functools.reduce(jnp.maximum, el)
    sel = []
    for e in range(EXPERTS_PER_GROUP):
        ahead = jnp.zeros_like(emax)
        for o in range(EXPERTS_PER_GROUP):
            if o < e:
                ahead = ahead + jnp.where(el[o] >= el[e], 1.0, 0.0)
            elif o > e:
                ahead = ahead + jnp.where(el[o] > el[e], 1.0, 0.0)
        sel.append(ahead < 2.0)
    pe = [jnp.where(sel[e], jnp.exp(el[e] - emax), 0.0) for e in range(EXPERTS_PER_GROUP)]
    scale = g_w / functools.reduce(jnp.add, pe)
    zero = jnp.zeros_like(emax)
    seen = zero
    lo_idx, hi_idx, c_lo, c_hi = zero, zero, zero, zero
    for e in range(EXPERTS_PER_GROUP):
        is_lo = sel[e] & (seen == 0.0)
        is_hi = sel[e] & (seen == 1.0)
        lo_idx = jnp.where(is_lo, float(e), lo_idx)
        hi_idx = jnp.where(is_hi, float(e), hi_idx)
        c_lo = jnp.where(is_lo, pe[e] * scale, c_lo)
        c_hi = jnp.where(is_hi, pe[e] * scale, c_hi)
        seen = seen + jnp.where(sel[e], 1.0, 0.0)
    pair = lo_idx * (7.0 - lo_idx) * 0.5 + (hi_idx - lo_idx - 1.0)
    bucket = gsel * float(N_PAIRS) + pair

    bid = lax.broadcasted_iota(jnp.int32, (ROUTER_ROWS, bucket.shape[1]), 0).astype(F32)
    onehot = jnp.where(bid == bucket, 1.0, 0.0)
    before = _dot(onehot.astype(BF16), tri_ref[...]) + carry_ref[...]
    rank = jnp.sum(onehot * before, axis=0, keepdims=True)
    carry = carry_ref[...] + jnp.sum(onehot, axis=1, keepdims=True)
    carry_ref[...] = carry
    cnt_ref[...] = carry
    meta_ref[...] = jnp.concatenate([bucket, rank, c_lo, c_hi, zero, zero, zero, zero], axis=0)


def _outproj_router(a_list, w_out, x, n, norm_w, w_group, b_group, w_router, b_router):
    d = D_MODEL
    tm = ROW_TILE
    tok = lambda: pl.BlockSpec((tm * LANE_TILES, LANE), lambda i: (i, 0))
    wr =jnp.concatenate([w_group, w_router.transpose(1, 0, 2).reshape(d, N_EXPERTS)], axis=1)
    wr_t = jnp.zeros((ROUTER_ROWS, d), F32).at[:wr.shape[1]].set(wr.T)
    wr_hi = wr_t.astype(BF16)
    wr_lo = (wr_t - wr_hi.astype(F32)).astype(BF16)
    br = jnp.zeros((ROUTER_ROWS, 1), F32).at[:wr.shape[1], 0].set(
        jnp.concatenate([b_group, b_router.reshape(-1)]))
    tri = jnp.asarray(np.triu(np.ones((tm, tm), np.float32), 1), BF16)
    const = lambda shape: pl.BlockSpec(shape, lambda i: (0, 0))
    in_specs = [pl.BlockSpec((tm, a.shape[1]), lambda i: (i, 0)) for a in a_list]
    in_specs += [const((d, d)), _token_spec(x, tm), const((1, d)),
                 const((ROUTER_ROWS, d)), const((ROUTER_ROWS, d)), const((ROUTER_ROWS, 1)), const((tm, tm))]
    return pl.pallas_call(
        functools.partial(_outproj_router_kernel, n_a=len(a_list)),
        grid=(n // tm,),
        in_specs=in_specs,
        out_specs=[tok(),
                   pl.BlockSpec((8, tm), lambda i: (0, i)),
                   const((ROUTER_ROWS, 1))],
        out_shape=[jax.ShapeDtypeStruct((n * LANE_TILES, LANE), F32),
                   jax.ShapeDtypeStruct((8, n), F32),
                   jax.ShapeDtypeStruct((ROUTER_ROWS, 1), F32)],
        scratch_shapes=[pltpu.VMEM((ROUTER_ROWS, 1), F32)],
        compiler_params=_params("arbitrary"),
        name="outproj_router",
    )(*a_list, w_out, x, norm_w.reshape(1, d), wr_hi, wr_lo, br, tri)


def _gather_tokens(table, idx):
    m = idx.shape[0]
    info = plsc.get_sparse_core_info()
    n_workers = info.num_cores * info.num_subcores
    per_worker = m // n_workers
    n_chunks = per_worker // SC_CHUNK
    assert per_worker * n_workers == m and n_chunks * SC_CHUNK == per_worker
    mesh = plsc.VectorSubcoreMesh(core_axis_name="core", subcore_axis_name="subcore")

    @functools.partial(
        pl.kernel, mesh=mesh,
        out_type=jax.ShapeDtypeStruct((m,) + table.shape[1:], table.dtype),
        scratch_types=[pltpu.VMEM((SC_CHUNK,), jnp.int32),
                       pltpu.VMEM((SC_CHUNK,) + table.shape[1:], table.dtype),
                       pltpu.SemaphoreType.DMA],
        name="gather_tokens")
    def gather(table_hbm, idx_hbm, out_hbm, idx_v, rows_v, sem):
        worker = lax.axis_index("subcore") * info.num_cores + lax.axis_index("core")

        @pl.loop(0, n_chunks)
        def _(j):
            off = pl.multiple_of(worker * per_worker + j * SC_CHUNK, SC_CHUNK)
            pltpu.sync_copy(idx_hbm.at[pl.ds(off, SC_CHUNK)], idx_v)
            pltpu.async_copy(table_hbm.at[idx_v], rows_v, sem).wait()
            pltpu.sync_copy(rows_v, out_hbm.at[pl.ds(off, SC_CHUNK)])

    return gather(table, idx)


def _moe_kernel(tid, e_lo, e_hi, n_tiles, x_ref, cw_ref, nw_ref, g_lo_ref, g_hi_ref, u_lo_ref, u_hi_ref,
                d_lo_ref, d_hi_ref, o_ref):
    del tid, e_lo, e_hi

    @pl.when(pl.program_id(0) < n_tiles[0])
    def _():
        x = _load_tokens(x_ref)
        h = _rms(x, nw_ref[...]).astype(BF16)
        cw = cw_ref[...]

        def expert(g_ref, u_ref, d_ref):
            act = _silu(_dot(h, g_ref[0])) * _dot(h, u_ref[0])
            return _dot(act.astype(BF16), d_ref[0])

        y = (cw[:, 1:2] * expert(g_lo_ref, u_lo_ref, d_lo_ref)
             + cw[:, 2:3] * expert(g_hi_ref, u_hi_ref, d_hi_ref))
        _store_tokens(o_ref, x + y)


def _moe(x2, n, meta, cnt, norm_w, w_gate, w_up, w_down):
    d = D_MODEL
    lt = LANE_TILES
    tm = MOE_TILE
    max_tiles = n // tm + N_BUCKETS
    n_slots = max_tiles * tm
    counts = cnt[:N_BUCKETS, 0].astype(jnp.int32)
    tiles_per_bucket = (counts + tm - 1) // tm
    tile_end = jnp.cumsum(tiles_per_bucket)
    row_start = (tile_end - tiles_per_bucket) * tm
    n_tiles = tile_end[-1]
    in_bucket = meta[0][:, None] == jnp.arange(N_BUCKETS, dtype=F32)[None, :]
    dest = meta[1].astype(jnp.int32) + jnp.sum(jnp.where(in_bucket, row_start[None, :], 0), axis=1)
    packed = jnp.stack([jnp.arange(n, dtype=F32), meta[2], meta[3], jnp.ones((n,), F32)], axis=1)
    slots = jnp.zeros((n_slots, 4), F32).at[dest].set(packed, unique_indices=True, mode="promise_in_bounds")
    src = jnp.where(slots[:, 3] > 0.0, slots[:, 0].astype(jnp.int32), jnp.arange(n_slots, dtype=jnp.int32) % n)
    x_sorted = _gather_tokens(x2.reshape(n, lt, LANE), src).reshape(n_slots * lt, LANE)

    tile_ids = jnp.minimum(jnp.arange(max_tiles, dtype=jnp.int32), n_tiles - 1)
    tile_bucket = jnp.sum((tile_ids[:, None] >= tile_end[None, :]).astype(jnp.int32), axis=1)
    group = tile_bucket // N_PAIRS
    pair = tile_bucket % N_PAIRS
    pair_lo = jnp.sum(jnp.where(pair[:, None] == jnp.arange(N_PAIRS)[None, :], jnp.asarray(PAIR_LO, jnp.int32)[None, :], 0), axis=1)
    pair_hi = jnp.sum(jnp.where(pair[:, None] == jnp.arange(N_PAIRS)[None, :], jnp.asarray(PAIR_HI, jnp.int32)[None, :], 0), axis=1)
    e_lo = group * EXPERTS_PER_GROUP + pair_lo
    e_hi = group * EXPERTS_PER_GROUP + pair_hi

    grid_spec = pltpu.PrefetchScalarGridSpec(
        num_scalar_prefetch=4,
        grid=(max_tiles,),
        in_specs=[pl.BlockSpec((tm * lt, LANE), lambda t, tid, lo, hi, nt: (tid[t], 0)),
                  pl.BlockSpec((tm, 4), lambda t, tid, lo, hi, nt: (tid[t], 0)),
                  pl.BlockSpec((1, d), lambda t, tid, lo, hi, nt: (0, 0)),
                  pl.BlockSpec((1, d, D_EXPERT), lambda t, tid, lo, hi, nt: (lo[t], 0, 0)),
                  pl.BlockSpec((1, d, D_EXPERT), lambda t, tid, lo, hi, nt: (hi[t], 0, 0)),
                  pl.BlockSpec((1, d, D_EXPERT), lambda t, tid, lo, hi, nt: (lo[t], 0, 0)),
                  pl.BlockSpec((1, d, D_EXPERT), lambda t, tid, lo, hi, nt: (hi[t], 0, 0)),
                  pl.BlockSpec((1, D_EXPERT, d), lambda t, tid, lo, hi, nt: (lo[t], 0, 0)),
                  pl.BlockSpec((1, D_EXPERT, d), lambda t, tid, lo, hi, nt: (hi[t], 0, 0))],
        out_specs=pl.BlockSpec((tm * lt, LANE), lambda t, tid, lo, hi, nt: (tid[t], 0)))
    out_sorted = pl.pallas_call(
        _moe_kernel,
        grid_spec=grid_spec,
        out_shape=jax.ShapeDtypeStruct((n_slots * lt, LANE), F32),
        compiler_params=_params("arbitrary"),
        name="moe",
    )(tile_ids, e_lo, e_hi, n_tiles.reshape(1), x_sorted, slots, norm_w.reshape(1, d),
      w_gate, w_gate, w_up, w_up, w_down, w_down)
    return _gather_tokens(out_sorted.reshape(n_slots, lt, LANE), dest).reshape(n * lt, LANE)


def kernel(x, attn_norm, ffn_norm, even_w_in, even_w_out, hg_lb_logits, hg_o_norm, da_q_norm, da_k_norm, da_lambda, da_subln, odd_w_in, gla_w_alpha, gla_b_alpha, gla_o_norm, odd_w_out, moe_w_group, moe_b_group, moe_w_router, moe_b_router, moe_w_gate, moe_w_up, moe_w_down):
    batch, seq, d = x.shape
    depth = attn_norm.shape[0]
    n = batch * seq
    xf = x.reshape(n, d)

    lb_sm = jax.nn.softmax(hg_lb_logits.astype(F32), axis=0)
    lb_all = jnp.cumsum(lb_sm, axis=0) - lb_sm[0:1]
    rank = gla_w_alpha.shape[1]

    for layer in range(depth):
        j = layer // 2
        if layer % 2 == 0:
            lam_init = 0.8 - 0.6 * math.exp(-0.3 * layer)
            proj = _inproj(xf, n, attn_norm[layer], even_w_in[j].astype(BF16))
            o_a = _hgrn(proj, lb_all[j], hg_o_norm[j], batch, seq)
            o_b = _diff_attn(proj, da_q_norm[j], da_k_norm[j], da_lambda[j], da_subln[j], lam_init, batch, seq)
            mix = [o_a, o_b]
            w_out = even_w_out[j]
        else:
            w_in = jnp.pad(odd_w_in[j], ((0, 0), (0, GLA_RANK_PAD - rank))).astype(BF16)
            w_alpha = jnp.pad(gla_w_alpha[j], ((0, GLA_RANK_PAD - rank), (0, 0))).astype(BF16)
            proj = _inproj(xf, n, attn_norm[layer], w_in)
            mix = [_gla(proj, w_alpha, gla_b_alpha[j], gla_o_norm[j], batch, seq)]
            w_out = odd_w_out[j]
        x2, meta, cnt = _outproj_router(mix, w_out.astype(BF16), xf, n, ffn_norm[layer], moe_w_group[layer],
                                        moe_b_group[layer], moe_w_router[layer], moe_b_router[layer])
        xf = _moe(x2, n, meta, cnt, ffn_norm[layer], moe_w_gate[layer].astype(BF16), moe_w_up[layer].astype(BF16),
                  moe_w_down[layer].astype(BF16))
    return xf.reshape(batch, seq, d)
```

```python
import functools
import math

import numpy as np
import jax
import jax.numpy as jnp
from jax import lax
from jax.experimental import pallas as pl
from jax.experimental.pallas import tpu as pltpu
from jax.experimental.pallas import tpu_sc as plsc

F32 = jnp.float32
BF16 = jnp.bfloat16
EPS = 1e-6
LOG2_E = 1.4426950408889634

D_MODEL = 1024
LANE = 128
LANE_TILES = D_MODEL // LANE
HEADS = 4
DK = 128
HG_WIDTH = 512
DA_HEAD_DIM = 64
GLA_DV = 256
GLA_RANK_PAD = 128
GLA_TAU = 16.0
N_GROUPS = 4
EXPERTS_PER_GROUP = 4
N_EXPERTS = 16
D_EXPERT = 512
N_PAIRS = 6
N_BUCKETS = N_GROUPS * N_PAIRS
ROUTER_ROWS = 32

CHUNK = 128
LEVELS = 7
ATTN_BLOCK = 512
ATTN_HEADS = 2
ROW_TILE = 512
MOE_TILE = 256
SC_CHUNK = 32
VMEM_LIMIT = 56 * 1024 * 1024

PAIR_LO = (0, 0, 0, 1, 1, 2)
PAIR_HI = (1, 2, 3, 2, 3, 3)


def _dot(a, b):
    return jnp.dot(a, b, preferred_element_type=F32)


def _dot_nt(a, b):
    return lax.dot_general(a, b, (((1,), (1,)), ((), ())), preferred_element_type=F32)


def _dot_tn(a, b):
    return lax.dot_general(a, b, (((0,), (0,)), ((), ())), preferred_element_type=F32)


def _rms(x, w):
    return x * lax.rsqrt(jnp.mean(x * x, axis=-1, keepdims=True) + EPS) * w


def _silu(x):
    return x * jax.nn.sigmoid(x)


def _params(*sem):
    return pltpu.CompilerParams(dimension_semantics=sem, vmem_limit_bytes=VMEM_LIMIT)


def _load_tokens(ref):
    if ref.shape[1] == D_MODEL:
        return ref[...]
    rows = ref.shape[0] // LANE_TILES
    return jnp.concatenate([ref[pl.ds(c, rows, stride=LANE_TILES), :] for c in range(LANE_TILES)], axis=1)


def _token_spec(x, rows):
    shape = (rows, D_MODEL) if x.shape[1] == D_MODEL else (rows * LANE_TILES, LANE)
    return pl.BlockSpec(shape, lambda i: (i, 0))


def _store_tokens(ref, val):
    rows = ref.shape[0] // LANE_TILES
    for c in range(LANE_TILES):
        ref[pl.ds(c, rows, stride=LANE_TILES), :] = val[:, c * LANE:(c + 1) * LANE]


def _inproj_kernel(x_ref, nw_ref, w_ref, o_ref):
    h = _rms(_load_tokens(x_ref), nw_ref[...]).astype(BF16)
    o_ref[...] = _dot(h, w_ref[...])


def _inproj(x, n, norm_w, w):
    d = D_MODEL
    n_out = w.shape[1]
    return pl.pallas_call(
        _inproj_kernel,
        grid=(n // ROW_TILE,),
        in_specs=[_token_spec(x, ROW_TILE),
                  pl.BlockSpec((1, d), lambda i: (0, 0)),
                  pl.BlockSpec((d, n_out), lambda i: (0, 0))],
        out_specs=pl.BlockSpec((ROW_TILE, n_out), lambda i: (i, 0)),
        out_shape=jax.ShapeDtypeStruct((n, n_out), F32),
        compiler_params=_params("parallel"),
        name="inproj",
    )(x, norm_w.reshape(1, d), w)


def _decay_tables():
    c = CHUNK
    a = np.zeros((LEVELS + 2, c, c), np.float32)
    t = np.arange(c)
    for l in range(LEVELS):
        s = 1 << l
        for r in range(c):
            mid = (r // (2 * s)) * 2 * s + s - 1
            if (r >> l) & 1:
                a[l, r] = (t > mid) & (t <= r)
            else:
                a[l, r] = (t > r) & (t <= mid)
    a[LEVELS] = t[None, :] <= t[:, None]
    a[LEVELS + 1] = t[None, :] > t[:, None]
    a = a.reshape((LEVELS + 2) * c, c)
    a3 = np.concatenate([a, a], axis=1)
    i, j = np.meshgrid(t, t, indexing="ij")
    x = np.maximum(i ^ j, 1)
    lvl = np.where(j < i, np.floor(np.log2(x)).astype(np.int32), np.where(i == j, LEVELS, -1))
    return jnp.asarray(a3, BF16), jnp.asarray(lvl, jnp.int32)


def _gla_chunk(q, k, v, g, st_ref, a3, lvl, rowid):
    c = CHUNK
    g_hi = g.astype(BF16)
    g_lo = (g - g_hi.astype(F32)).astype(BF16)
    f = jnp.exp(_dot(a3, jnp.concatenate([g_hi, g_lo], axis=0)))
    sc = jnp.where(lvl == LEVELS, jnp.sum(q * k, axis=-1, keepdims=True), 0.0)
    for l in range(LEVELS):
        is_query = ((rowid >> l) & 1) == 1
        m = (jnp.where(is_query, q, k) * f[l * c:(l + 1) * c]).astype(BF16)
        sc = jnp.where(lvl == l, _dot_nt(m, m), sc)
    qd = (q * f[LEVELS * c:(LEVELS + 1) * c]).astype(BF16)
    kd = (k * f[(LEVELS + 1) * c:(LEVELS + 2) * c]).astype(BF16)
    vb = v.astype(BF16)
    st = st_ref[...]
    o = _dot(sc.astype(BF16), vb) + _dot_nt(qd, st.astype(BF16))
    chunk_decay = f[(LEVELS + 1) * c - 1:(LEVELS + 1) * c, :]
    st_ref[...] = st * chunk_decay + _dot_tn(vb, kd)
    return o


def _hgrn_kernel(q_ref, f_ref, i_ref, g_ref, lb_ref, on_ref, a3_ref, lvl_ref, o_ref, st_ref):
    @pl.when(pl.program_id(1) == 0)
    def _():
        st_ref[...] = jnp.zeros_like(st_ref)

    a3 = a3_ref[...]
    lvl = lvl_ref[...]
    rowid = lax.broadcasted_iota(jnp.int32, (CHUNK, 1), 0)
    for h in range(HEADS):
        sl = slice(h * DK, (h + 1) * DK)
        fl = f_ref[:, sl]
        lb = lb_ref[:, sl]
        log_f = jnp.log(lb + (1.0 - lb) * jax.nn.sigmoid(fl))
        key_in = (1.0 - lb) * jax.nn.sigmoid(-fl)
        q = _silu(q_ref[:, sl]) * (DK ** -0.5)
        o = _gla_chunk(q, key_in, i_ref[:, sl], log_f, st_ref.at[h], a3, lvl, rowid)
        o = _rms(o, on_ref[...])
        o_ref[:, sl] = (o * _silu(g_ref[:, sl])).astype(BF16)


def _hgrn(proj, lb, o_norm, batch, seq):
    n = proj.shape[0]
    nt = seq // CHUNK
    a3, lvl = _decay_tables()
    col = lambda j: pl.BlockSpec((CHUNK, HG_WIDTH), lambda b, t: (b * nt + t, j))
    const = lambda shape: pl.BlockSpec(shape, lambda b, t: (0, 0))
    return pl.pallas_call(
        _hgrn_kernel,
        grid=(batch, nt),
        in_specs=[col(0), col(1), col(2), col(3), const((1, HG_WIDTH)), const((1, DK)),
                  const(a3.shape), const(lvl.shape)],
        out_specs=pl.BlockSpec((CHUNK, HG_WIDTH), lambda b, t: (b * nt + t, 0)),
        out_shape=jax.ShapeDtypeStruct((n, HG_WIDTH), BF16),
        scratch_shapes=[pltpu.VMEM((HEADS, DK, DK), F32)],
        compiler_params=_params("parallel", "arbitrary"),
        name="hgrn",
    )(proj, proj, proj, proj, lb.reshape(1, HG_WIDTH), o_norm.reshape(1, DK), a3, lvl)


def _log_sigmoid(x):
    return jnp.minimum(x, 0.0) - jnp.log1p(jnp.exp(-jnp.abs(x)))


def _gla_kernel(q_ref, k_ref, v_ref, g_ref, a_ref, wa_ref, ba_ref, on_ref, a3_ref, lvl_ref, o_ref, st_ref):
    @pl.when(pl.program_id(1) == 0)
    def _():
        st_ref[...] = jnp.zeros_like(st_ref)

    a3 = a3_ref[...]
    lvl = lvl_ref[...]
    rowid = lax.broadcasted_iota(jnp.int32, (CHUNK, 1), 0)
    gate_logit = _dot(a_ref[...].astype(BF16), wa_ref[...]) + ba_ref[...]
    log_alpha = _log_sigmoid(gate_logit) * (1.0 / GLA_TAU)
    for h in range(HEADS):
        sl = slice(h * DK, (h + 1) * DK)
        sv = slice(h * GLA_DV, (h + 1) * GLA_DV)
        q = q_ref[:, sl] * (DK ** -0.5)
        o = _gla_chunk(q, k_ref[:, sl], v_ref[:, sv], log_alpha[:, sl], st_ref.at[h], a3, lvl, rowid)
        o = _rms(o, on_ref[...])
        o_ref[:, sv] = (o * _silu(g_ref[:, sv])).astype(BF16)


def _gla(proj, w_alpha, b_alpha, o_norm, batch, seq):
    n = proj.shape[0]
    nt = seq // CHUNK
    qk_w = HEADS * DK
    v_w = HEADS * GLA_DV
    a3, lvl = _decay_tables()
    row = lambda b, t: b * nt + t
    const = lambda shape: pl.BlockSpec(shape, lambda b, t: (0, 0))
    return pl.pallas_call(
        _gla_kernel,
        grid=(batch, nt),
        in_specs=[pl.BlockSpec((CHUNK, qk_w), lambda b, t: (row(b, t), 0)),
                  pl.BlockSpec((CHUNK, qk_w), lambda b, t: (row(b, t), 1)),
                  pl.BlockSpec((CHUNK, v_w), lambda b, t: (row(b, t), 1)),
                  pl.BlockSpec((CHUNK, v_w), lambda b, t: (row(b, t), 2)),
                  pl.BlockSpec((CHUNK, GLA_RANK_PAD), lambda b, t: (row(b, t), (2 * qk_w + 2 * v_w) // GLA_RANK_PAD)),
                  const((GLA_RANK_PAD, qk_w)), const((1, qk_w)), const((1, GLA_DV)),
                  const(a3.shape), const(lvl.shape)],
        out_specs=pl.BlockSpec((CHUNK, v_w), lambda b, t: (row(b, t), 0)),
        out_shape=jax.ShapeDtypeStruct((n, v_w), BF16),
        scratch_shapes=[pltpu.VMEM((HEADS, GLA_DV, DK), F32)],
        compiler_params=_params("parallel", "arbitrary"),
        name="gla",
    )(proj, proj, proj, proj, proj, w_alpha, b_alpha.reshape(1, qk_w), o_norm.reshape(1, GLA_DV), a3, lvl)


def _qk_norm(x, w2):
    lane = lax.broadcasted_iota(jnp.int32, x.shape, 1)
    first = lane < DA_HEAD_DIM
    sq = x * x
    ms1 = jnp.sum(jnp.where(first, sq, 0.0), axis=-1, keepdims=True) * (1.0 / DA_HEAD_DIM)
    ms2 = jnp.sum(jnp.where(first, 0.0, sq), axis=-1, keepdims=True) * (1.0 / DA_HEAD_DIM)
    r = jnp.where(first, lax.rsqrt(ms1 + EPS), lax.rsqrt(ms2 + EPS))
    return x * r * w2, first


def _attn_kernel(qi_tab, ki_tab, q_ref, k_ref, v_ref, qn_ref, kn_ref, lam_ref, sub_ref, o_ref,
                 qs_ref, m_ref, l_ref, acc_ref, *, lam_init):
    p = pl.program_id(2)
    qi = qi_tab[p]
    ki = ki_tab[p]
    tq = ATTN_BLOCK
    dv = 2 * DA_HEAD_DIM
    heads = [slice(h * dv, (h + 1) * dv) for h in range(ATTN_HEADS)]

    @pl.when(ki == 0)
    def _():
        for h, sl in enumerate(heads):
            qn, first = _qk_norm(q_ref[:, sl], qn_ref[...])
            qn = qn * (DA_HEAD_DIM ** -0.5 * LOG2_E)
            q1 = jnp.where(first, qn, 0.0).T.astype(BF16)
            q2 = jnp.where(first, 0.0, qn).T.astype(BF16)
            for copy in range(2):
                qs_ref[h, copy, :, :tq] = q1
                qs_ref[h, copy, :, tq:] = q2
        m_ref[...] = jnp.full_like(m_ref, -jnp.inf)
        l_ref[...] = jnp.zeros_like(l_ref)
        acc_ref[...] = jnp.zeros_like(acc_ref)

    def step(masked):
        ones = jnp.ones((16, k_ref.shape[0]), BF16)
        for h, sl in enumerate(heads):
            kn, _ = _qk_norm(k_ref[:, sl], kn_ref[...])
            kn = kn.astype(BF16)

            def scores(copy):
                s = _dot(kn, qs_ref[h, copy])
                if masked:
                    key = lax.broadcasted_iota(jnp.int32, s.shape, 0)
                    qry = lax.broadcasted_iota(jnp.int32, s.shape, 1)
                    qry = jnp.where(qry >= tq, qry - tq, qry)
                    s = jnp.where(key <= qry, s, -jnp.inf)
                return s

            m_prev = m_ref[h]
            m_new = jnp.maximum(m_prev, jnp.max(scores(0), axis=0, keepdims=True))
            alpha = jnp.exp2(m_prev - m_new)
            pr = jnp.exp2((scores(1) - m_new).astype(BF16))
            vt = jnp.concatenate([v_ref[:, sl].T.astype(BF16), ones], axis=0)
            pv = _dot(vt, pr)
            l_ref[h] = alpha * l_ref[h] + pv[dv:dv + 1]
            acc_ref[h] = alpha * acc_ref[h] + pv[:dv]
            m_ref[h] = m_new

    @pl.when(ki < qi)
    def _():
        step(False)

    @pl.when(ki == qi)
    def _():
        step(True)
        lv = lam_ref[...]
        lam = (jnp.exp(jnp.sum(lv[0:1] * lv[1:2], axis=-1, keepdims=True))
               - jnp.exp(jnp.sum(lv[2:3] * lv[3:4], axis=-1, keepdims=True)) + lam_init)
        for h, sl in enumerate(heads):
            o1 = acc_ref[h, :, :tq] / l_ref[h, :, :tq]
            o2 = acc_ref[h, :, tq:] / l_ref[h, :, tq:]
            o = o1 - lam * o2
            o = o * lax.rsqrt(jnp.mean(o * o, axis=0, keepdims=True) + EPS) * sub_ref[...] * (1.0 - lam_init)
            o_ref[:, sl] = o.T.astype(BF16)


def _diff_attn(proj, q_norm, k_norm, lam_vecs, subln, lam_init, batch, seq):
    n = proj.shape[0]
    blk = ATTN_BLOCK
    nq = seq // blk
    dv = 2 * DA_HEAD_DIM
    pairs = [(qi, ki) for qi in range(nq) for ki in range(qi + 1)]
    qi_tab = jnp.asarray([p[0] for p in pairs], jnp.int32)
    ki_tab = jnp.asarray([p[1] for p in pairs], jnp.int32)
    width = ATTN_HEADS * dv
    groups = HEADS // ATTN_HEADS
    q_col = (4 * HG_WIDTH) // width
    const = lambda shape: pl.BlockSpec(shape, lambda b, g, p, qt, kt: (0, 0))
    grid_spec = pltpu.PrefetchScalarGridSpec(
        num_scalar_prefetch=2,
        grid=(batch, groups, len(pairs)),
        in_specs=[pl.BlockSpec((blk, width), lambda b, g, p, qt, kt: (b * nq + qt[p], q_col + g)),
                  pl.BlockSpec((blk, width), lambda b, g, p, qt, kt: (b * nq + kt[p], q_col + groups + g)),
                  pl.BlockSpec((blk, width), lambda b, g, p, qt, kt: (b * nq + kt[p], q_col + 2 * groups + g)),
                  const((1, dv)), const((1, dv)), const((4, DA_HEAD_DIM)), const((dv, 1))],
        out_specs=pl.BlockSpec((blk, width), lambda b, g, p, qt, kt: (b * nq + qt[p], g)),
        scratch_shapes=[pltpu.VMEM((ATTN_HEADS, 2, dv, 2 * blk), BF16),
                        pltpu.VMEM((ATTN_HEADS, 1, 2 * blk), F32),
                        pltpu.VMEM((ATTN_HEADS, 1, 2 * blk), F32),
                        pltpu.VMEM((ATTN_HEADS, dv, 2 * blk), F32)])
    return pl.pallas_call(
        functools.partial(_attn_kernel, lam_init=lam_init),
        grid_spec=grid_spec,
        out_shape=jax.ShapeDtypeStruct((n, HEADS * dv), BF16),
        compiler_params=_params("parallel", "parallel", "arbitrary"),
        name="diff_attn",
    )(qi_tab, ki_tab, proj, proj, proj,
      jnp.tile(q_norm, 2).reshape(1, -1), jnp.tile(k_norm, 2).reshape(1, -1),
      lam_vecs, subln.reshape(-1, 1))


def _outproj_router_kernel(*refs, n_a):
    a_refs = refs[:n_a]
    (w_ref, x_ref, nw_ref, wr_hi_ref, wr_lo_ref, br_ref, tri_ref,
     x2_ref, meta_ref, cnt_ref, carry_ref) = refs[n_a:]

    @pl.when(pl.program_id(0) == 0)
    def _():
        carry_ref[...] = jnp.zeros_like(carry_ref)

    x2 = _load_tokens(x_ref)
    off = 0
    for a_ref in a_refs:
        kk = a_ref.shape[1]
        x2 = x2 + _dot(a_ref[...], w_ref[off:off + kk, :])
        off += kk
    _store_tokens(x2_ref, x2)

    h = _rms(x2, nw_ref[...])
    h_hi = h.astype(BF16)
    h_lo = (h - h_hi.astype(F32)).astype(BF16)
    wr_hi = wr_hi_ref[...]
    lt = _dot_nt(wr_hi, h_hi) + _dot_nt(wr_hi, h_lo) + _dot_nt(wr_lo_ref[...], h_hi) + br_ref[...]
    row = lambda r: lt[r:r + 1, :]

    gl = [row(g) for g in range(N_GROUPS)]
    gmax = functools.reduce(jnp.maximum, gl)
    gsel = jnp.where(gl[0] == gmax, 0.0, jnp.where(gl[1] == gmax, 1.0, jnp.where(gl[2] == gmax, 2.0, 3.0)))
    g_w = 1.0 / functools.reduce(jnp.add, [jnp.exp(x - gmax) for x in gl])

    el = []
    for e in range(EXPERTS_PER_GROUP):
        v = row(N_GROUPS + e)
        for g in range(1, N_GROUPS):
            v = jnp.where(gsel == float(g), row(N_GROUPS + EXPERTS_PER_GROUP * g + e), v)
        el.append(v)
    emax = functools.reduce(jnp.maximum, el)
    sel = []
    for e in range(EXPERTS_PER_GROUP):
        ahead = jnp.zeros_like(emax)
        for o in range(EXPERTS_PER_GROUP):
            if o < e:
                ahead = ahead + jnp.where(el[o] >= el[e], 1.0, 0.0)
            elif o > e:
                ahead = ahead + jnp.where(el[o] > el[e], 1.0, 0.0)
        sel.append(ahead < 2.0)
    pe = [jnp.where(sel[e], jnp.exp(el[e] - emax), 0.0) for e in range(EXPERTS_PER_GROUP)]
    scale = g_w / functools.reduce(jnp.add, pe)
    zero = jnp.zeros_like(emax)
    seen = zero
    lo_idx, hi_idx, c_lo, c_hi = zero, zero, zero, zero
    for e in range(EXPERTS_PER_GROUP):
        is_lo = sel[e] & (seen == 0.0)
        is_hi = sel[e] & (seen == 1.0)
        lo_idx = jnp.where(is_lo, float(e), lo_idx)
        hi_idx = jnp.where(is_hi, float(e), hi_idx)
        c_lo = jnp.where(is_lo, pe[e] * scale, c_lo)
        c_hi = jnp.where(is_hi, pe[e] * scale, c_hi)
        seen = seen + jnp.where(sel[e], 1.0, 0.0)
    pair = lo_idx * (7.0 - lo_idx) * 0.5 + (hi_idx - lo_idx - 1.0)
    bucket = gsel * float(N_PAIRS) + pair

    bid = lax.broadcasted_iota(jnp.int32, (ROUTER_ROWS, bucket.shape[1]), 0).astype(F32)
    onehot = jnp.where(bid == bucket, 1.0, 0.0)
    before = _dot(onehot.astype(BF16), tri_ref[...]) + carry_ref[...]
    rank = jnp.sum(onehot * before, axis=0, keepdims=True)
    carry = carry_ref[...] + jnp.sum(onehot, axis=1, keepdims=True)
    carry_ref[...] = carry
    cnt_ref[...] = carry
    meta_ref[...] = jnp.concatenate([bucket, rank, c_lo, c_hi, zero, zero, zero, zero], axis=0)


def _outproj_router(a_list, w_out, x, n, norm_w, w_group, b_group, w_router, b_router):
    d = D_MODEL
    tm = ROW_TILE
    tok = lambda: pl.BlockSpec((tm * LANE_TILES, LANE), lambda i: (i, 0))
    wr =jnp.concatenate([w_group, w_router.transpose(1, 0, 2).reshape(d, N_EXPERTS)], axis=1)
    wr_t = jnp.zeros((ROUTER_ROWS, d), F32).at[:wr.shape[1]].set(wr.T)
    wr_hi = wr_t.astype(BF16)
    wr_lo = (wr_t - wr_hi.astype(F32)).astype(BF16)
    br = jnp.zeros((ROUTER_ROWS, 1), F32).at[:wr.shape[1], 0].set(
        jnp.concatenate([b_group, b_router.reshape(-1)]))
    tri = jnp.asarray(np.triu(np.ones((tm, tm), np.float32), 1), BF16)
    const = lambda shape: pl.BlockSpec(shape, lambda i: (0, 0))
    in_specs = [pl.BlockSpec((tm, a.shape[1]), lambda i: (i, 0)) for a in a_list]
    in_specs += [const((d, d)), _token_spec(x, tm), const((1, d)),
                 const((ROUTER_ROWS, d)), const((ROUTER_ROWS, d)), const((ROUTER_ROWS, 1)), const((tm, tm))]
    return pl.pallas_call(
        functools.partial(_outproj_router_kernel, n_a=len(a_list)),
        grid=(n // tm,),
        in_specs=in_specs,
        out_specs=[tok(),
                   pl.BlockSpec((8, tm), lambda i: (0, i)),
                   const((ROUTER_ROWS, 1))],
        out_shape=[jax.ShapeDtypeStruct((n * LANE_TILES, LANE), F32),
                   jax.ShapeDtypeStruct((8, n), F32),
                   jax.ShapeDtypeStruct((ROUTER_ROWS, 1), F32)],
        scratch_shapes=[pltpu.VMEM((ROUTER_ROWS, 1), F32)],
        compiler_params=_params("arbitrary"),
        name="outproj_router",
    )(*a_list, w_out, x, norm_w.reshape(1, d), wr_hi, wr_lo, br, tri)


def _gather_tokens(table, idx):
    m = idx.shape[0]
    info = plsc.get_sparse_core_info()
    n_workers = info.num_cores * info.num_subcores
    per_worker = m // n_workers
    n_chunks = per_worker // SC_CHUNK
    assert per_worker * n_workers == m and n_chunks * SC_CHUNK == per_worker and n_chunks % 2 == 0
    mesh = plsc.VectorSubcoreMesh(core_axis_name="core", subcore_axis_name="subcore")
    rows = pltpu.VMEM((SC_CHUNK,) + table.shape[1:], table.dtype)

    @functools.partial(
        pl.kernel, mesh=mesh,
        out_type=jax.ShapeDtypeStruct((m,) + table.shape[1:], table.dtype),
        scratch_types=[pltpu.VMEM((n_chunks, SC_CHUNK), jnp.int32), rows, rows,
                       pltpu.SemaphoreType.DMA, pltpu.SemaphoreType.DMA],
        name="gather_tokens")
    def gather(table_hbm, idx_hbm, out_hbm, idx_v, rows_a, rows_b, sem_a, sem_b):
        worker = lax.axis_index("subcore") * info.num_cores + lax.axis_index("core")
        bufs = ((rows_a, sem_a), (rows_b, sem_b))

        def fetch(j, buf):
            return pltpu.make_async_copy(table_hbm.at[idx_v.at[j]], bufs[buf][0], bufs[buf][1])

        pltpu.sync_copy(idx_hbm.at[worker], idx_v)
        fetch(0, 0).start()

        @pl.loop(0, n_chunks, step=2)
        def _(j0):
            for buf in range(2):
                j = j0 + buf
                fetch(j, buf).wait()

                @pl.when(j + 1 < n_chunks)
                def _():
                    fetch(j + 1, 1 - buf).start()

                off = pl.multiple_of(worker * per_worker + j * SC_CHUNK, SC_CHUNK)
                pltpu.sync_copy(bufs[buf][0], out_hbm.at[pl.ds(off, SC_CHUNK)])

    return gather(table, idx.reshape(n_workers, n_chunks, SC_CHUNK))


def _moe_kernel(tid, e_lo, e_hi, n_tiles, x_ref, cw_ref, nw_ref, g_lo_ref, g_hi_ref, u_lo_ref, u_hi_ref,
                d_lo_ref, d_hi_ref, o_ref):
    del tid, e_lo, e_hi

    @pl.when(pl.program_id(0) < n_tiles[0])
    def _():
        x = _load_tokens(x_ref)
        h = _rms(x, nw_ref[...]).astype(BF16)
        cw = cw_ref[...]

        def expert(g_ref, u_ref, d_ref):
            act = _silu(_dot(h, g_ref[0])) * _dot(h, u_ref[0])
            return _dot(act.astype(BF16), d_ref[0])

        y = (cw[:, 1:2] * expert(g_lo_ref, u_lo_ref, d_lo_ref)
             + cw[:, 2:3] * expert(g_hi_ref, u_hi_ref, d_hi_ref))
        _store_tokens(o_ref, x + y)


def _moe(x2, n, meta, cnt, norm_w, w_gate, w_up, w_down):
    d = D_MODEL
    lt = LANE_TILES
    tm = MOE_TILE
    max_tiles = n // tm + N_BUCKETS
    n_slots = max_tiles * tm
    counts = cnt[:N_BUCKETS, 0].astype(jnp.int32)
    tiles_per_bucket = (counts + tm - 1) // tm
    tile_end = jnp.cumsum(tiles_per_bucket)
    row_start = (tile_end - tiles_per_bucket) * tm
    n_tiles = tile_end[-1]
    in_bucket = meta[0][:, None] == jnp.arange(N_BUCKETS, dtype=F32)[None, :]
    dest = meta[1].astype(jnp.int32) + jnp.sum(jnp.where(in_bucket, row_start[None, :], 0), axis=1)
    packed = jnp.stack([jnp.arange(n, dtype=F32), meta[2], meta[3], jnp.ones((n,), F32)], axis=1)
    slots = jnp.zeros((n_slots, 4), F32).at[dest].set(packed, unique_indices=True, mode="promise_in_bounds")
    src = jnp.where(slots[:, 3] > 0.0, slots[:, 0].astype(jnp.int32), jnp.arange(n_slots, dtype=jnp.int32) % n)
    x_sorted = _gather_tokens(x2.reshape(n, lt, LANE), src).reshape(n_slots * lt, LANE)

    tile_ids = jnp.minimum(jnp.arange(max_tiles, dtype=jnp.int32), n_tiles - 1)
    tile_bucket = jnp.sum((tile_ids[:, None] >= tile_end[None, :]).astype(jnp.int32), axis=1)
    group = tile_bucket // N_PAIRS
    pair = tile_bucket % N_PAIRS
    pair_lo = jnp.sum(jnp.where(pair[:, None] == jnp.arange(N_PAIRS)[None, :], jnp.asarray(PAIR_LO, jnp.int32)[None, :], 0), axis=1)
    pair_hi = jnp.sum(jnp.where(pair[:, None] == jnp.arange(N_PAIRS)[None, :], jnp.asarray(PAIR_HI, jnp.int32)[None, :], 0), axis=1)
    e_lo = group * EXPERTS_PER_GROUP + pair_lo
    e_hi = group * EXPERTS_PER_GROUP + pair_hi

    grid_spec = pltpu.PrefetchScalarGridSpec(
        num_scalar_prefetch=4,
        grid=(max_tiles,),
        in_specs=[pl.BlockSpec((tm * lt, LANE), lambda t, tid, lo, hi, nt: (tid[t], 0)),
                  pl.BlockSpec((tm, 4), lambda t, tid, lo, hi, nt: (tid[t], 0)),
                  pl.BlockSpec((1, d), lambda t, tid, lo, hi, nt: (0, 0)),
                  pl.BlockSpec((1, d, D_EXPERT), lambda t, tid, lo, hi, nt: (lo[t], 0, 0)),
                  pl.BlockSpec((1, d, D_EXPERT), lambda t, tid, lo, hi, nt: (hi[t], 0, 0)),
                  pl.BlockSpec((1, d, D_EXPERT), lambda t, tid, lo, hi, nt: (lo[t], 0, 0)),
                  pl.BlockSpec((1, d, D_EXPERT), lambda t, tid, lo, hi, nt: (hi[t], 0, 0)),
                  pl.BlockSpec((1, D_EXPERT, d), lambda t, tid, lo, hi, nt: (lo[t], 0, 0)),
                  pl.BlockSpec((1, D_EXPERT, d), lambda t, tid, lo, hi, nt: (hi[t], 0, 0))],
        out_specs=pl.BlockSpec((tm * lt, LANE), lambda t, tid, lo, hi, nt: (tid[t], 0)))
    out_sorted = pl.pallas_call(
        _moe_kernel,
        grid_spec=grid_spec,
        out_shape=jax.ShapeDtypeStruct((n_slots * lt, LANE), F32),
        compiler_params=_params("arbitrary"),
        name="moe",
    )(tile_ids, e_lo, e_hi, n_tiles.reshape(1), x_sorted, slots, norm_w.reshape(1, d),
      w_gate, w_gate, w_up, w_up, w_down, w_down)
    return _gather_tokens(out_sorted.reshape(n_slots, lt, LANE), dest).reshape(n * lt, LANE)


def kernel(x, attn_norm, ffn_norm, even_w_in, even_w_out, hg_lb_logits, hg_o_norm, da_q_norm, da_k_norm, da_lambda, da_subln, odd_w_in, gla_w_alpha, gla_b_alpha, gla_o_norm, odd_w_out, moe_w_group, moe_b_group, moe_w_router, moe_b_router, moe_w_gate, moe_w_up, moe_w_down):
    batch, seq, d = x.shape
    depth = attn_norm.shape[0]
    n = batch * seq
    xf = x.reshape(n, d)

    lb_sm = jax.nn.softmax(hg_lb_logits.astype(F32), axis=0)
    lb_all = jnp.cumsum(lb_sm, axis=0) - lb_sm[0:1]
    rank = gla_w_alpha.shape[1]

    for layer in range(depth):
        j = layer // 2
        if layer % 2 == 0:
            lam_init = 0.8 - 0.6 * math.exp(-0.3 * layer)
            proj = _inproj(xf, n, attn_norm[layer], even_w_in[j].astype(BF16))
            o_a = _hgrn(proj, lb_all[j], hg_o_norm[j], batch, seq)
            o_b = _diff_attn(proj, da_q_norm[j], da_k_norm[j], da_lambda[j], da_subln[j], lam_init, batch, seq)
            mix = [o_a, o_b]
            w_out = even_w_out[j]
        else:
            w_in = jnp.pad(odd_w_in[j], ((0, 0), (0, GLA_RANK_PAD - rank))).astype(BF16)
            w_alpha = jnp.pad(gla_w_alpha[j], ((0, GLA_RANK_PAD - rank), (0, 0))).astype(BF16)
            proj = _inproj(xf, n, attn_norm[layer], w_in)
            mix = [_gla(proj, w_alpha, gla_b_alpha[j], gla_o_norm[j], batch, seq)]
            w_out = odd_w_out[j]
        x2, meta, cnt = _outproj_router(mix, w_out.astype(BF16), xf, n, ffn_norm[layer], moe_w_group[layer],
                                        moe_b_group[layer], moe_w_router[layer], moe_b_router[layer])
        xf = _moe(x2, n, meta, cnt, ffn_norm[layer], moe_w_gate[layer].astype(BF16), moe_w_up[layer].astype(BF16),
                  moe_w_down[layer].astype(BF16))
    return xf.reshape(batch, seq, d)
```

```python
import functools
import math

import numpy as np
import jax
import jax.numpy as jnp
from jax import lax
from jax.experimental import pallas as pl
from jax.experimental.pallas import tpu as pltpu
from jax.experimental.pallas import tpu_sc as plsc

F32 = jnp.float32
BF16 = jnp.bfloat16
EPS = 1e-6
LOG2_E = 1.4426950408889634

D_MODEL = 1024
LANE = 128
LANE_TILES = D_MODEL // LANE
HEADS = 4
DK = 128
HG_WIDTH = 512
DA_HEAD_DIM = 64
GLA_DV = 256
GLA_RANK_PAD = 128
GLA_TAU = 16.0
N_GROUPS = 4
EXPERTS_PER_GROUP = 4
N_EXPERTS = 16
D_EXPERT = 512
N_PAIRS = 6
N_BUCKETS = N_GROUPS * N_PAIRS
ROUTER_ROWS = 32

CHUNK = 128
LEVELS = 7
ATTN_BLOCK = 512
ATTN_HEADS = 2
SCORE_BOUND_MARGIN = 1.05
MAX_SCORE_BOUND = 60.0
ROW_TILE = 512
MOE_TILE = 256
SC_CHUNK = 32
VMEM_LIMIT = 56 * 1024 * 1024

PAIR_LO = (0, 0, 0, 1, 1, 2)
PAIR_HI = (1, 2, 3, 2, 3, 3)


def _dot(a, b):
    return jnp.dot(a, b, preferred_element_type=F32)


def _dot_nt(a, b):
    return lax.dot_general(a, b, (((1,), (1,)), ((), ())), preferred_element_type=F32)


def _dot_tn(a, b):
    return lax.dot_general(a, b, (((0,), (0,)), ((), ())), preferred_element_type=F32)


def _rms(x, w):
    return x * lax.rsqrt(jnp.mean(x * x, axis=-1, keepdims=True) + EPS) * w


def _silu(x):
    return x * jax.nn.sigmoid(x)


def _params(*sem):
    return pltpu.CompilerParams(dimension_semantics=sem, vmem_limit_bytes=VMEM_LIMIT)


def _load_tokens(ref):
    if ref.shape[1] == D_MODEL:
        return ref[...]
    rows = ref.shape[0] // LANE_TILES
    return jnp.concatenate([ref[pl.ds(c, rows, stride=LANE_TILES), :] for c in range(LANE_TILES)], axis=1)


def _token_spec(x, rows):
    shape = (rows, D_MODEL) if x.shape[1] == D_MODEL else (rows * LANE_TILES, LANE)
    return pl.BlockSpec(shape, lambda i: (i, 0))


def _store_tokens(ref, val):
    rows = ref.shape[0] // LANE_TILES
    for c in range(LANE_TILES):
        ref[pl.ds(c, rows, stride=LANE_TILES), :] = val[:, c * LANE:(c + 1) * LANE]


def _inproj_kernel(x_ref, nw_ref, w_ref, o_ref):
    h = _rms(_load_tokens(x_ref), nw_ref[...]).astype(BF16)
    o_ref[...] = _dot(h, w_ref[...])


def _inproj(x, n, norm_w, w):
    d = D_MODEL
    n_out = w.shape[1]
    return pl.pallas_call(
        _inproj_kernel,
        grid=(n // ROW_TILE,),
        in_specs=[_token_spec(x, ROW_TILE),
                  pl.BlockSpec((1, d), lambda i: (0, 0)),
                  pl.BlockSpec((d, n_out), lambda i: (0, 0))],
        out_specs=pl.BlockSpec((ROW_TILE, n_out), lambda i: (i, 0)),
        out_shape=jax.ShapeDtypeStruct((n, n_out), F32),
        compiler_params=_params("parallel"),
        name="inproj",
    )(x, norm_w.reshape(1, d), w)


def _decay_tables():
    c = CHUNK
    a = np.zeros((LEVELS + 2, c, c), np.float32)
    t = np.arange(c)
    for l in range(LEVELS):
        s = 1 << l
        for r in range(c):
            mid = (r // (2 * s)) * 2 * s + s - 1
            if (r >> l) & 1:
                a[l, r] = (t > mid) & (t <= r)
            else:
                a[l, r] = (t > r) & (t <= mid)
    a[LEVELS] = t[None, :] <= t[:, None]
    a[LEVELS + 1] = t[None, :] > t[:, None]
    a = a.reshape((LEVELS + 2) * c, c)
    a3 = np.concatenate([a, a], axis=1)
    i, j = np.meshgrid(t, t, indexing="ij")
    x = np.maximum(i ^ j, 1)
    lvl = np.where(j < i, np.floor(np.log2(x)).astype(np.int32), np.where(i == j, LEVELS, -1))
    return jnp.asarray(a3, BF16), jnp.asarray(lvl, jnp.int32)


def _decay_factors(a3, g):
    g_hi = g.astype(BF16)
    g_lo = (g - g_hi.astype(F32)).astype(BF16)
    return jnp.exp(_dot(a3, jnp.concatenate([g_hi, g_lo], axis=0)))


def _gla_chunk(q, k, v, f, st_ref, lvl, rowid):
    c = CHUNK
    sc = jnp.where(lvl == LEVELS, jnp.sum(q * k, axis=-1, keepdims=True), 0.0)
    for l in range(LEVELS):
        is_query = ((rowid >> l) & 1) == 1
        m = (jnp.where(is_query, q, k) * f[l * c:(l + 1) * c]).astype(BF16)
        sc = jnp.where(lvl == l, _dot_nt(m, m), sc)
    qd = (q * f[LEVELS * c:(LEVELS + 1) * c]).astype(BF16)
    kd = (k * f[(LEVELS + 1) * c:(LEVELS + 2) * c]).astype(BF16)
    vb = v.astype(BF16)
    st = st_ref[...]
    o = _dot(sc.astype(BF16), vb) + _dot_nt(qd, st.astype(BF16))
    chunk_decay = f[(LEVELS + 1) * c - 1:(LEVELS + 1) * c, :]
    st_ref[...] = st * chunk_decay + _dot_tn(vb, kd)
    return o


def _hgrn_kernel(q_ref, f_ref, i_ref, g_ref, lb_ref, on_ref, a3_ref, lvl_ref, o_ref, st_ref):
    @pl.when(pl.program_id(1) == 0)
    def _():
        st_ref[...] = jnp.zeros_like(st_ref)

    lvl = lvl_ref[...]
    rowid = lax.broadcasted_iota(jnp.int32, (CHUNK, 1), 0)
    fl = f_ref[...]
    lb = lb_ref[...]
    f = _decay_factors(a3_ref[...], jnp.log(lb + (1.0 - lb) * jax.nn.sigmoid(fl)))
    key_in = (1.0 - lb) * jax.nn.sigmoid(-fl)
    for h in range(HEADS):
        sl = slice(h * DK, (h + 1) * DK)
        q = _silu(q_ref[:, sl]) * (DK ** -0.5)
        o = _gla_chunk(q, key_in[:, sl], i_ref[:, sl], f[:, sl], st_ref.at[h], lvl, rowid)
        o = _rms(o, on_ref[...])
        o_ref[:, sl] = (o * _silu(g_ref[:, sl])).astype(BF16)


def _hgrn(proj, lb, o_norm, batch, seq):
    n = proj.shape[0]
    nt = seq // CHUNK
    a3, lvl = _decay_tables()
    col = lambda j: pl.BlockSpec((CHUNK, HG_WIDTH), lambda b, t: (b * nt + t, j))
    const = lambda shape: pl.BlockSpec(shape, lambda b, t: (0, 0))
    return pl.pallas_call(
        _hgrn_kernel,
        grid=(batch, nt),
        in_specs=[col(0), col(1), col(2), col(3), const((1, HG_WIDTH)), const((1, DK)),
                  const(a3.shape), const(lvl.shape)],
        out_specs=pl.BlockSpec((CHUNK, HG_WIDTH), lambda b, t: (b * nt + t, 0)),
        out_shape=jax.ShapeDtypeStruct((n, HG_WIDTH), BF16),
        scratch_shapes=[pltpu.VMEM((HEADS, DK, DK), F32)],
        compiler_params=_params("parallel", "arbitrary"),
        name="hgrn",
    )(proj, proj, proj, proj, lb.reshape(1, HG_WIDTH), o_norm.reshape(1, DK), a3, lvl)


def _log_sigmoid(x):
    return jnp.minimum(x, 0.0) - jnp.log1p(jnp.exp(-jnp.abs(x)))


def _gla_kernel(q_ref, k_ref, v_ref, g_ref, a_ref, wa_ref, ba_ref, on_ref, a3_ref, lvl_ref, o_ref, st_ref):
    @pl.when(pl.program_id(1) == 0)
    def _():
        st_ref[...] = jnp.zeros_like(st_ref)

    lvl = lvl_ref[...]
    rowid = lax.broadcasted_iota(jnp.int32, (CHUNK, 1), 0)
    gate_logit = _dot(a_ref[...].astype(BF16), wa_ref[...]) + ba_ref[...]
    f = _decay_factors(a3_ref[...], _log_sigmoid(gate_logit) * (1.0 / GLA_TAU))
    for h in range(HEADS):
        sl = slice(h * DK, (h + 1) * DK)
        sv = slice(h * GLA_DV, (h + 1) * GLA_DV)
        q = q_ref[:, sl] * (DK ** -0.5)
        o = _gla_chunk(q, k_ref[:, sl], v_ref[:, sv], f[:, sl], st_ref.at[h], lvl, rowid)
        o = _rms(o, on_ref[...])
        o_ref[:, sv] = (o * _silu(g_ref[:, sv])).astype(BF16)


def _gla(proj, w_alpha, b_alpha, o_norm, batch, seq):
    n = proj.shape[0]
    nt = seq // CHUNK
    qk_w = HEADS * DK
    v_w = HEADS * GLA_DV
    a3, lvl = _decay_tables()
    row = lambda b, t: b * nt + t
    const = lambda shape: pl.BlockSpec(shape, lambda b, t: (0, 0))
    return pl.pallas_call(
        _gla_kernel,
        grid=(batch, nt),
        in_specs=[pl.BlockSpec((CHUNK, qk_w), lambda b, t: (row(b, t), 0)),
                  pl.BlockSpec((CHUNK, qk_w), lambda b, t: (row(b, t), 1)),
                  pl.BlockSpec((CHUNK, v_w), lambda b, t: (row(b, t), 1)),
                  pl.BlockSpec((CHUNK, v_w), lambda b, t: (row(b, t), 2)),
                  pl.BlockSpec((CHUNK, GLA_RANK_PAD), lambda b, t: (row(b, t), (2 * qk_w + 2 * v_w) // GLA_RANK_PAD)),
                  const((GLA_RANK_PAD, qk_w)), const((1, qk_w)), const((1, GLA_DV)),
                  const(a3.shape), const(lvl.shape)],
        out_specs=pl.BlockSpec((CHUNK, v_w), lambda b, t: (row(b, t), 0)),
        out_shape=jax.ShapeDtypeStruct((n, v_w), BF16),
        scratch_shapes=[pltpu.VMEM((HEADS, GLA_DV, DK), F32)],
        compiler_params=_params("parallel", "arbitrary"),
        name="gla",
    )(proj, proj, proj, proj, proj, w_alpha, b_alpha.reshape(1, qk_w), o_norm.reshape(1, GLA_DV), a3, lvl)


def _qk_norm(x, w2):
    lane = lax.broadcasted_iota(jnp.int32, x.shape, 1)
    first = lane < DA_HEAD_DIM
    sq = x * x
    ms1 = jnp.sum(jnp.where(first, sq, 0.0), axis=-1, keepdims=True) * (1.0 / DA_HEAD_DIM)
    ms2 = jnp.sum(jnp.where(first, 0.0, sq), axis=-1, keepdims=True) * (1.0 / DA_HEAD_DIM)
    r = jnp.where(first, lax.rsqrt(ms1 + EPS), lax.rsqrt(ms2 + EPS))
    return x * r * w2, first


def _score_bound(q_norm_w, k_norm_w):
    return (jnp.max(jnp.abs(q_norm_w)) * jnp.max(jnp.abs(k_norm_w))
            * (DA_HEAD_DIM * DA_HEAD_DIM ** -0.5 * LOG2_E * SCORE_BOUND_MARGIN))


def _attn_kernel(qi_tab, ki_tab, q_ref, k_ref, v_ref, qn_ref, kn_ref, lam_ref, sub_ref, o_ref,
                 qs_ref, m_ref, l_ref, acc_ref, *, lam_init, bounded):
    p = pl.program_id(2)
    qi = qi_tab[p]
    ki = ki_tab[p]
    tq = ATTN_BLOCK
    dv = 2 * DA_HEAD_DIM
    heads = [slice(h * dv, (h + 1) * dv) for h in range(ATTN_HEADS)]

    @pl.when(ki == 0)
    def _():
        for h, sl in enumerate(heads):
            qn, first = _qk_norm(q_ref[:, sl], qn_ref[...])
            qn = qn * (DA_HEAD_DIM ** -0.5 * LOG2_E)
            q1 = jnp.where(first, qn, 0.0).T.astype(BF16)
            q2 = jnp.where(first, 0.0, qn).T.astype(BF16)
            for copy in range(1 if bounded else 2):
                qs_ref[h, copy, :, :tq] = q1
                qs_ref[h, copy, :, tq:] = q2
            if bounded:
                bound = _score_bound(qn_ref[...], kn_ref[...])
                row = lax.broadcasted_iota(jnp.int32, (dv, 2 * tq), 0)
                qs_ref[h, 1] = jnp.where(row == 0, -bound, 0.0).astype(BF16)
        if not bounded:
            m_ref[...] = jnp.full_like(m_ref, -jnp.inf)
        l_ref[...] = jnp.zeros_like(l_ref)
        acc_ref[...] = jnp.zeros_like(acc_ref)

    def step(masked):
        tk = k_ref.shape[0]
        ones = jnp.ones((16, tk), BF16)
        unit = jnp.where(lax.broadcasted_iota(jnp.int32, (tk, dv), 1) == 0, 1.0, 0.0).astype(BF16)
        for h, sl in enumerate(heads):
            kn, _ = _qk_norm(k_ref[:, sl], kn_ref[...])
            kn = kn.astype(BF16)

            def mask(s):
                if not masked:
                    return s
                key = lax.broadcasted_iota(jnp.int32, s.shape, 0)
                qry = lax.broadcasted_iota(jnp.int32, s.shape, 1)
                qry = jnp.where(qry >= tq, qry - tq, qry)
                return jnp.where(key <= qry, s, -jnp.inf)

            if bounded:
                shifted = mask(_dot(jnp.concatenate([kn, unit], axis=1), qs_ref[h].reshape(2 * dv, 2 * tq)))
                alpha = 1.0
            else:
                m_prev = m_ref[h]
                m_new = jnp.maximum(m_prev, jnp.max(mask(_dot(kn, qs_ref[h, 0])), axis=0, keepdims=True))
                alpha = jnp.exp2(m_prev - m_new)
                shifted = mask(_dot(kn, qs_ref[h, 1])) - m_new
                m_ref[h] = m_new
            pr = jnp.exp2(shifted).astype(BF16) if bounded else jnp.exp2(shifted.astype(BF16))
            vt = jnp.concatenate([v_ref[:, sl].T.astype(BF16), ones], axis=0)
            pv = _dot(vt, pr)
            l_ref[h] = alpha * l_ref[h] + pv[dv:dv + 1]
            acc_ref[h] = alpha * acc_ref[h] + pv[:dv]

    @pl.when(ki < qi)
    def _():
        step(False)

    @pl.when(ki == qi)
    def _():
        step(True)
        lv = lam_ref[...]
        lam = (jnp.exp(jnp.sum(lv[0:1] * lv[1:2], axis=-1, keepdims=True))
               - jnp.exp(jnp.sum(lv[2:3] * lv[3:4], axis=-1, keepdims=True)) + lam_init)
        for h, sl in enumerate(heads):
            o1 = acc_ref[h, :, :tq] / l_ref[h, :, :tq]
            o2 = acc_ref[h, :, tq:] / l_ref[h, :, tq:]
            o = o1 - lam * o2
            o = o * lax.rsqrt(jnp.mean(o * o, axis=0, keepdims=True) + EPS) * sub_ref[...] * (1.0 - lam_init)
            o_ref[:, sl] = o.T.astype(BF16)


def _diff_attn(proj, q_norm, k_norm, lam_vecs, subln, lam_init, batch, seq):
    n = proj.shape[0]
    blk = ATTN_BLOCK
    nq = seq // blk
    dv = 2 * DA_HEAD_DIM
    pairs = [(qi, ki) for qi in range(nq) for ki in range(qi + 1)]
    qi_tab = jnp.asarray([p[0] for p in pairs], jnp.int32)
    ki_tab = jnp.asarray([p[1] for p in pairs], jnp.int32)
    width = ATTN_HEADS * dv
    groups = HEADS // ATTN_HEADS
    q_col = (4 * HG_WIDTH) // width
    const = lambda shape: pl.BlockSpec(shape, lambda b, g, p, qt, kt: (0, 0))
    grid_spec = pltpu.PrefetchScalarGridSpec(
        num_scalar_prefetch=2,
        grid=(batch, groups, len(pairs)),
        in_specs=[pl.BlockSpec((blk, width), lambda b, g, p, qt, kt: (b * nq + qt[p], q_col + g)),
                  pl.BlockSpec((blk, width), lambda b, g, p, qt, kt: (b * nq + kt[p], q_col + groups + g)),
                  pl.BlockSpec((blk, width), lambda b, g, p, qt, kt: (b * nq + kt[p], q_col + 2 * groups + g)),
                  const((1, dv)), const((1, dv)), const((4, DA_HEAD_DIM)), const((dv, 1))],
        out_specs=pl.BlockSpec((blk, width), lambda b, g, p, qt, kt: (b * nq + qt[p], g)),
        scratch_shapes=[pltpu.VMEM((ATTN_HEADS, 2, dv, 2 * blk), BF16),
                        pltpu.VMEM((ATTN_HEADS, 1, 2 * blk), F32),
                        pltpu.VMEM((ATTN_HEADS, 1, 2 * blk), F32),
                        pltpu.VMEM((ATTN_HEADS, dv, 2 * blk), F32)])
    args = (qi_tab, ki_tab, proj, proj, proj, jnp.tile(q_norm, 2).reshape(1, -1), jnp.tile(k_norm, 2).reshape(1, -1),
            lam_vecs, subln.reshape(-1, 1))

    def run(bounded):
        return pl.pallas_call(
            functools.partial(_attn_kernel, lam_init=lam_init, bounded=bounded),
            grid_spec=grid_spec,
            out_shape=jax.ShapeDtypeStruct((n, HEADS * dv), BF16),
            compiler_params=_params("parallel", "parallel", "arbitrary"),
            name="diff_attn_bounded" if bounded else "diff_attn",
        )(*args)

    return lax.cond(_score_bound(q_norm, k_norm) <= MAX_SCORE_BOUND, lambda: run(True), lambda: run(False))


def _outproj_router_kernel(*refs, n_a):
    a_refs = refs[:n_a]
    (w_ref, x_ref, nw_ref, wr_hi_ref, wr_lo_ref, br_ref, tri_ref,
     x2_ref, meta_ref, cnt_ref, carry_ref) = refs[n_a:]

    @pl.when(pl.program_id(0) == 0)
    def _():
        carry_ref[...] = jnp.zeros_like(carry_ref)

    x2 = _load_tokens(x_ref)
    off = 0
    for a_ref in a_refs:
        kk = a_ref.shape[1]
        x2 = x2 + _dot(a_ref[...], w_ref[off:off + kk, :])
        off += kk
    _store_tokens(x2_ref, x2)

    h = _rms(x2, nw_ref[...])
    h_hi = h.astype(BF16)
    h_lo = (h - h_hi.astype(F32)).astype(BF16)
    wr_hi = wr_hi_ref[...]
    lt = _dot_nt(wr_hi, h_hi) + _dot_nt(wr_hi, h_lo) + _dot_nt(wr_lo_ref[...], h_hi) + br_ref[...]
    row = lambda r: lt[r:r + 1, :]

    gl = [row(g) for g in range(N_GROUPS)]
    gmax = functools.reduce(jnp.maximum, gl)
    gsel = jnp.where(gl[0] == gmax, 0.0, jnp.where(gl[1] == gmax, 1.0, jnp.where(gl[2] == gmax, 2.0, 3.0)))
    g_w = 1.0 / functools.reduce(jnp.add, [jnp.exp(x - gmax) for x in gl])

    el = []
    for e in range(EXPERTS_PER_GROUP):
        v = row(N_GROUPS + e)
        for g in range(1, N_GROUPS):
            v = jnp.where(gsel == float(g), row(N_GROUPS + EXPERTS_PER_GROUP * g + e), v)
        el.append(v)
    emax = functools.reduce(jnp.maximum, el)
    sel = []
    for e in range(EXPERTS_PER_GROUP):
        ahead = jnp.zeros_like(emax)
        for o in range(EXPERTS_PER_GROUP):
            if o < e:
                ahead = ahead + jnp.where(el[o] >= el[e], 1.0, 0.0)
            elif o > e:
                ahead = ahead + jnp.where(el[o] > el[e], 1.0, 0.0)
        sel.append(ahead < 2.0)
    pe = [jnp.where(sel[e], jnp.exp(el[e] - emax), 0.0) for e in range(EXPERTS_PER_GROUP)]
    scale = g_w / functools.reduce(jnp.add, pe)
    zero = jnp.zeros_like(emax)
    seen = zero
    lo_idx, hi_idx, c_lo, c_hi = zero, zero, zero, zero
    for e in range(EXPERTS_PER_GROUP):
        is_lo = sel[e] & (seen == 0.0)
        is_hi = sel[e] & (seen == 1.0)
        lo_idx = jnp.where(is_lo, float(e), lo_idx)
        hi_idx = jnp.where(is_hi, float(e), hi_idx)
        c_lo = jnp.where(is_lo, pe[e] * scale, c_lo)
        c_hi = jnp.where(is_hi, pe[e] * scale, c_hi)
        seen = seen + jnp.where(sel[e], 1.0, 0.0)
    pair = lo_idx * (7.0 - lo_idx) * 0.5 + (hi_idx - lo_idx - 1.0)
    bucket = gsel * float(N_PAIRS) + pair

    bid = lax.broadcasted_iota(jnp.int32, (ROUTER_ROWS, bucket.shape[1]), 0).astype(F32)
    onehot = jnp.where(bid == bucket, 1.0, 0.0)
    before = _dot(onehot.astype(BF16), tri_ref[...]) + carry_ref[...]
    rank = jnp.sum(onehot * before, axis=0, keepdims=True)
    carry = carry_ref[...] + jnp.sum(onehot, axis=1, keepdims=True)
    carry_ref[...] = carry
    cnt_ref[...] = carry
    meta_ref[...] = jnp.concatenate([bucket, rank, c_lo, c_hi, zero, zero, zero, zero], axis=0)


def _outproj_router(a_list, w_out, x, n, norm_w, w_group, b_group, w_router, b_router):
    d = D_MODEL
    tm = ROW_TILE
    tok = lambda: pl.BlockSpec((tm * LANE_TILES, LANE), lambda i: (i, 0))
    wr =jnp.concatenate([w_group, w_router.transpose(1, 0, 2).reshape(d, N_EXPERTS)], axis=1)
    wr_t = jnp.zeros((ROUTER_ROWS, d), F32).at[:wr.shape[1]].set(wr.T)
    wr_hi = wr_t.astype(BF16)
    wr_lo = (wr_t - wr_hi.astype(F32)).astype(BF16)
    br = jnp.zeros((ROUTER_ROWS, 1), F32).at[:wr.shape[1], 0].set(
        jnp.concatenate([b_group, b_router.reshape(-1)]))
    tri = jnp.asarray(np.triu(np.ones((tm, tm), np.float32), 1), BF16)
    const = lambda shape: pl.BlockSpec(shape, lambda i: (0, 0))
    in_specs = [pl.BlockSpec((tm, a.shape[1]), lambda i: (i, 0)) for a in a_list]
    in_specs += [const((d, d)), _token_spec(x, tm), const((1, d)),
                 const((ROUTER_ROWS, d)), const((ROUTER_ROWS, d)), const((ROUTER_ROWS, 1)), const((tm, tm))]
    return pl.pallas_call(
        functools.partial(_outproj_router_kernel, n_a=len(a_list)),
        grid=(n // tm,),
        in_specs=in_specs,
        out_specs=[tok(),
                   pl.BlockSpec((8, tm), lambda i: (0, i)),
                   const((ROUTER_ROWS, 1))],
        out_shape=[jax.ShapeDtypeStruct((n * LANE_TILES, LANE), F32),
                   jax.ShapeDtypeStruct((8, n), F32),
                   jax.ShapeDtypeStruct((ROUTER_ROWS, 1), F32)],
        scratch_shapes=[pltpu.VMEM((ROUTER_ROWS, 1), F32)],
        compiler_params=_params("arbitrary"),
        name="outproj_router",
    )(*a_list, w_out, x, norm_w.reshape(1, d), wr_hi, wr_lo, br, tri)


def _gather_tokens(table, idx):
    m = idx.shape[0]
    info = plsc.get_sparse_core_info()
    n_workers = info.num_cores * info.num_subcores
    per_worker = m // n_workers
    n_chunks = per_worker // SC_CHUNK
    assert per_worker * n_workers == m and n_chunks * SC_CHUNK == per_worker and n_chunks % 2 == 0
    mesh = plsc.VectorSubcoreMesh(core_axis_name="core", subcore_axis_name="subcore")
    rows = pltpu.VMEM((SC_CHUNK,) + table.shape[1:], table.dtype)

    @functools.partial(
        pl.kernel, mesh=mesh,
        out_type=jax.ShapeDtypeStruct((m,) + table.shape[1:], table.dtype),
        scratch_types=[pltpu.VMEM((n_chunks, SC_CHUNK), jnp.int32), rows, rows,
                       pltpu.SemaphoreType.DMA, pltpu.SemaphoreType.DMA],
        name="gather_tokens")
    def gather(table_hbm, idx_hbm, out_hbm, idx_v, rows_a, rows_b, sem_a, sem_b):
        worker = lax.axis_index("subcore") * info.num_cores + lax.axis_index("core")
        bufs = ((rows_a, sem_a), (rows_b, sem_b))

        def fetch(j, buf):
            return pltpu.make_async_copy(table_hbm.at[idx_v.at[j]], bufs[buf][0], bufs[buf][1])

        pltpu.sync_copy(idx_hbm.at[worker], idx_v)
        fetch(0, 0).start()

        @pl.loop(0, n_chunks, step=2)
        def _(j0):
            for buf in range(2):
                j = j0 + buf
                fetch(j, buf).wait()

                @pl.when(j + 1 < n_chunks)
                def _():
                    fetch(j + 1, 1 - buf).start()

                off = pl.multiple_of(worker * per_worker + j * SC_CHUNK, SC_CHUNK)
                pltpu.sync_copy(bufs[buf][0], out_hbm.at[pl.ds(off, SC_CHUNK)])

    return gather(table, idx.reshape(n_workers, n_chunks, SC_CHUNK))


def _moe_kernel(tid, e_lo, e_hi, n_tiles, x_ref, cw_ref, nw_ref, g_lo_ref, g_hi_ref, u_lo_ref, u_hi_ref,
                d_lo_ref, d_hi_ref, o_ref):
    del tid, e_lo, e_hi

    @pl.when(pl.program_id(0) < n_tiles[0])
    def _():
        x = _load_tokens(x_ref)
        h = _rms(x, nw_ref[...]).astype(BF16)
        cw = cw_ref[...]

        def expert(g_ref, u_ref, d_ref):
            act = _silu(_dot(h, g_ref[0])) * _dot(h, u_ref[0])
            return _dot(act.astype(BF16), d_ref[0])

        y = (cw[:, 1:2] * expert(g_lo_ref, u_lo_ref, d_lo_ref)
             + cw[:, 2:3] * expert(g_hi_ref, u_hi_ref, d_hi_ref))
        _store_tokens(o_ref, x + y)


def _moe(x2, n, meta, cnt, norm_w, w_gate, w_up, w_down):
    d = D_MODEL
    lt = LANE_TILES
    tm = MOE_TILE
    max_tiles = n // tm + N_BUCKETS
    n_slots = max_tiles * tm
    counts = cnt[:N_BUCKETS, 0].astype(jnp.int32)
    tiles_per_bucket = (counts + tm - 1) // tm
    tile_end = jnp.cumsum(tiles_per_bucket)
    row_start = (tile_end - tiles_per_bucket) * tm
    n_tiles = tile_end[-1]
    in_bucket = meta[0][:, None] == jnp.arange(N_BUCKETS, dtype=F32)[None, :]
    dest = meta[1].astype(jnp.int32) + jnp.sum(jnp.where(in_bucket, row_start[None, :], 0), axis=1)
    packed = jnp.stack([jnp.arange(n, dtype=F32), meta[2], meta[3], jnp.ones((n,), F32)], axis=1)
    slots = jnp.zeros((n_slots, 4), F32).at[dest].set(packed, unique_indices=True, mode="promise_in_bounds")
    src = jnp.where(slots[:, 3] > 0.0, slots[:, 0].astype(jnp.int32), jnp.arange(n_slots, dtype=jnp.int32) % n)
    x_sorted = _gather_tokens(x2.reshape(n, lt, LANE), src).reshape(n_slots * lt, LANE)

    tile_ids = jnp.minimum(jnp.arange(max_tiles, dtype=jnp.int32), n_tiles - 1)
    tile_bucket = jnp.sum((tile_ids[:, None] >= tile_end[None, :]).astype(jnp.int32), axis=1)
    group = tile_bucket // N_PAIRS
    pair = tile_bucket % N_PAIRS
    pair_lo = jnp.sum(jnp.where(pair[:, None] == jnp.arange(N_PAIRS)[None, :], jnp.asarray(PAIR_LO, jnp.int32)[None, :], 0), axis=1)
    pair_hi = jnp.sum(jnp.where(pair[:, None] == jnp.arange(N_PAIRS)[None, :], jnp.asarray(PAIR_HI, jnp.int32)[None, :], 0), axis=1)
    e_lo = group * EXPERTS_PER_GROUP + pair_lo
    e_hi = group * EXPERTS_PER_GROUP + pair_hi

    grid_spec = pltpu.PrefetchScalarGridSpec(
        num_scalar_prefetch=4,
        grid=(max_tiles,),
        in_specs=[pl.BlockSpec((tm * lt, LANE), lambda t, tid, lo, hi, nt: (tid[t], 0)),
                  pl.BlockSpec((tm, 4), lambda t, tid, lo, hi, nt: (tid[t], 0)),
                  pl.BlockSpec((1, d), lambda t, tid, lo, hi, nt: (0, 0)),
                  pl.BlockSpec((1, d, D_EXPERT), lambda t, tid, lo, hi, nt: (lo[t], 0, 0)),
                  pl.BlockSpec((1, d, D_EXPERT), lambda t, tid, lo, hi, nt: (hi[t], 0, 0)),
                  pl.BlockSpec((1, d, D_EXPERT), lambda t, tid, lo, hi, nt: (lo[t], 0, 0)),
                  pl.BlockSpec((1, d, D_EXPERT), lambda t, tid, lo, hi, nt: (hi[t], 0, 0)),
                  pl.BlockSpec((1, D_EXPERT, d), lambda t, tid, lo, hi, nt: (lo[t], 0, 0)),
                  pl.BlockSpec((1, D_EXPERT, d), lambda t, tid, lo, hi, nt: (hi[t], 0, 0))],
        out_specs=pl.BlockSpec((tm * lt, LANE), lambda t, tid, lo, hi, nt: (tid[t], 0)))
    out_sorted = pl.pallas_call(
        _moe_kernel,
        grid_spec=grid_spec,
        out_shape=jax.ShapeDtypeStruct((n_slots * lt, LANE), F32),
        compiler_params=_params("arbitrary"),
        name="moe",
    )(tile_ids, e_lo, e_hi, n_tiles.reshape(1), x_sorted, slots, norm_w.reshape(1, d),
      w_gate, w_gate, w_up, w_up, w_down, w_down)
    return _gather_tokens(out_sorted.reshape(n_slots, lt, LANE), dest).reshape(n * lt, LANE)


def kernel(x, attn_norm, ffn_norm, even_w_in, even_w_out, hg_lb_logits, hg_o_norm, da_q_norm, da_k_norm, da_lambda, da_subln, odd_w_in, gla_w_alpha, gla_b_alpha, gla_o_norm, odd_w_out, moe_w_group, moe_b_group, moe_w_router, moe_b_router, moe_w_gate, moe_w_up, moe_w_down):
    batch, seq, d = x.shape
    depth = attn_norm.shape[0]
    n = batch * seq
    xf = x.reshape(n, d)

    lb_sm = jax.nn.softmax(hg_lb_logits.astype(F32), axis=0)
    lb_all = jnp.cumsum(lb_sm, axis=0) - lb_sm[0:1]
    rank = gla_w_alpha.shape[1]

    for layer in range(depth):
        j = layer // 2
        if layer % 2 == 0:
            lam_init = 0.8 - 0.6 * math.exp(-0.3 * layer)
            proj = _inproj(xf, n, attn_norm[layer], even_w_in[j].astype(BF16))
            o_a = _hgrn(proj, lb_all[j], hg_o_norm[j], batch, seq)
            o_b = _diff_attn(proj, da_q_norm[j], da_k_norm[j], da_lambda[j], da_subln[j], lam_init, batch, seq)
            mix = [o_a, o_b]
            w_out = even_w_out[j]
        else:
            w_in = jnp.pad(odd_w_in[j], ((0, 0), (0, GLA_RANK_PAD - rank))).astype(BF16)
            w_alpha = jnp.pad(gla_w_alpha[j], ((0, GLA_RANK_PAD - rank), (0, 0))).astype(BF16)
            proj = _inproj(xf, n, attn_norm[layer], w_in)
            mix = [_gla(proj, w_alpha, gla_b_alpha[j], gla_o_norm[j], batch, seq)]
            w_out = odd_w_out[j]
        x2, meta, cnt = _outproj_router(mix, w_out.astype(BF16), xf, n, ffn_norm[layer], moe_w_group[layer],
                                        moe_b_group[layer], moe_w_router[layer], moe_b_router[layer])
        xf = _moe(x2, n, meta, cnt, ffn_norm[layer], moe_w_gate[layer].astype(BF16), moe_w_up[layer].astype(BF16),
                  moe_w_down[layer].astype(BF16))
    return xf.reshape(batch, seq, d)
```

```python
import functools
import math

import numpy as np
import jax
import jax.numpy as jnp
from jax import lax
from jax.experimental import pallas as pl
from jax.experimental.pallas import tpu as pltpu
from jax.experimental.pallas import tpu_sc as plsc

F32 = jnp.float32
BF16 = jnp.bfloat16
EPS = 1e-6
LOG2_E = 1.4426950408889634

D_MODEL = 1024
LANE = 128
LANE_TILES = D_MODEL // LANE
HEADS = 4
DK = 128
HG_WIDTH = 512
DA_HEAD_DIM = 64
GLA_DV = 256
GLA_RANK_PAD = 128
GLA_TAU = 16.0
N_GROUPS = 4
EXPERTS_PER_GROUP = 4
N_EXPERTS = 16
D_EXPERT = 512
N_PAIRS = 6
N_BUCKETS = N_GROUPS * N_PAIRS
ROUTER_ROWS = 32

CHUNK = 128
GLA_STEP_CHUNKS = 4
LEVELS = 7
ATTN_BLOCK = 512
ATTN_HEADS = 2
SCORE_BOUND_MARGIN = 1.05
MAX_SCORE_BOUND = 60.0
ROW_TILE = 512
MOE_TILE = 256
SC_CHUNK = 32
VMEM_LIMIT = 56 * 1024 * 1024

PAIR_LO = (0, 0, 0, 1, 1, 2)
PAIR_HI = (1, 2, 3, 2, 3, 3)


def _dot(a, b):
    return jnp.dot(a, b, preferred_element_type=F32)


def _dot_nt(a, b):
    return lax.dot_general(a, b, (((1,), (1,)), ((), ())), preferred_element_type=F32)


def _dot_tn(a, b):
    return lax.dot_general(a, b, (((0,), (0,)), ((), ())), preferred_element_type=F32)


def _rms(x, w):
    return x * lax.rsqrt(jnp.mean(x * x, axis=-1, keepdims=True) + EPS) * w


def _silu(x):
    return x * jax.nn.sigmoid(x)


def _params(*sem):
    return pltpu.CompilerParams(dimension_semantics=sem, vmem_limit_bytes=VMEM_LIMIT)


def _load_tokens(ref):
    if ref.shape[1] == D_MODEL:
        return ref[...]
    rows = ref.shape[0] // LANE_TILES
    return jnp.concatenate([ref[pl.ds(c, rows, stride=LANE_TILES), :] for c in range(LANE_TILES)], axis=1)


def _token_spec(x, rows):
    shape = (rows, D_MODEL) if x.shape[1] == D_MODEL else (rows * LANE_TILES, LANE)
    return pl.BlockSpec(shape, lambda i: (i, 0))


def _store_tokens(ref, val):
    rows = ref.shape[0] // LANE_TILES
    for c in range(LANE_TILES):
        ref[pl.ds(c, rows, stride=LANE_TILES), :] = val[:, c * LANE:(c + 1) * LANE]


def _inproj_kernel(x_ref, nw_ref, w_ref, o_ref):
    h = _rms(_load_tokens(x_ref), nw_ref[...]).astype(BF16)
    o_ref[...] = _dot(h, w_ref[...])


def _inproj(x, n, norm_w, w):
    d = D_MODEL
    n_out = w.shape[1]
    return pl.pallas_call(
        _inproj_kernel,
        grid=(n // ROW_TILE,),
        in_specs=[_token_spec(x, ROW_TILE),
                  pl.BlockSpec((1, d), lambda i: (0, 0)),
                  pl.BlockSpec((d, n_out), lambda i: (0, 0))],
        out_specs=pl.BlockSpec((ROW_TILE, n_out), lambda i: (i, 0)),
        out_shape=jax.ShapeDtypeStruct((n, n_out), F32),
        compiler_params=_params("parallel"),
        name="inproj",
    )(x, norm_w.reshape(1, d), w)


def _decay_tables():
    c = CHUNK
    a = np.zeros((LEVELS + 2, c, c), np.float32)
    t = np.arange(c)
    for l in range(LEVELS):
        s = 1 << l
        for r in range(c):
            mid = (r // (2 * s)) * 2 * s + s - 1
            if (r >> l) & 1:
                a[l, r] = (t > mid) & (t <= r)
            else:
                a[l, r] = (t > r) & (t <= mid)
    a[LEVELS] = t[None, :] <= t[:, None]
    a[LEVELS + 1] = t[None, :] > t[:, None]
    a = a.reshape((LEVELS + 2) * c, c)
    a3 = np.concatenate([a, a], axis=1)
    i, j = np.meshgrid(t, t, indexing="ij")
    x = np.maximum(i ^ j, 1)
    lvl = np.where(j < i, np.floor(np.log2(x)).astype(np.int32), np.where(i == j, LEVELS, -1))
    return jnp.asarray(a3, BF16), jnp.asarray(lvl, jnp.int32)


def _decay_factors(a3, g):
    g_hi = g.astype(BF16)
    g_lo = (g - g_hi.astype(F32)).astype(BF16)
    return jnp.exp(_dot(a3, jnp.concatenate([g_hi, g_lo], axis=0)))


def _gla_chunk(q, k, v, f, st_ref, lvl, rowid):
    c = CHUNK
    sc = jnp.where(lvl == LEVELS, jnp.sum(q * k, axis=-1, keepdims=True), 0.0)
    for l in range(LEVELS):
        is_query = ((rowid >> l) & 1) == 1
        m = (jnp.where(is_query, q, k) * f[l * c:(l + 1) * c]).astype(BF16)
        sc = jnp.where(lvl == l, _dot_nt(m, m), sc)
    qd = (q * f[LEVELS * c:(LEVELS + 1) * c]).astype(BF16)
    kd = (k * f[(LEVELS + 1) * c:(LEVELS + 2) * c]).astype(BF16)
    vb = v.astype(BF16)
    st = st_ref[...]
    o = _dot(sc.astype(BF16), vb) + _dot_nt(qd, st.astype(BF16))
    chunk_decay = f[(LEVELS + 1) * c - 1:(LEVELS + 1) * c, :]
    st_ref[...] = st * chunk_decay + _dot_tn(vb, kd)
    return o


def _hgrn_kernel(q_ref, f_ref, i_ref, g_ref, lb_ref, on_ref, a3_ref, lvl_ref, o_ref, st_ref):
    @pl.when(pl.program_id(1) == 0)
    def _():
        st_ref[...] = jnp.zeros_like(st_ref)

    lvl = lvl_ref[...]
    rowid = lax.broadcasted_iota(jnp.int32, (CHUNK, 1), 0)
    lb = lb_ref[...]
    for c in range(GLA_STEP_CHUNKS):
        rows = slice(c * CHUNK, (c + 1) * CHUNK)
        fl = f_ref[rows, :]
        f = _decay_factors(a3_ref[...], jnp.log(lb + (1.0 - lb) * jax.nn.sigmoid(fl)))
        key_in = (1.0 - lb) * jax.nn.sigmoid(-fl)
        for h in range(HEADS):
            sl = slice(h * DK, (h + 1) * DK)
            q = _silu(q_ref[rows, sl]) * (DK ** -0.5)
            o = _gla_chunk(q, key_in[:, sl], i_ref[rows, sl], f[:, sl], st_ref.at[h], lvl, rowid)
            o = _rms(o, on_ref[...])
            o_ref[rows, sl] = (o * _silu(g_ref[rows, sl])).astype(BF16)


def _hgrn(proj, lb, o_norm, batch, seq):
    n = proj.shape[0]
    rows = CHUNK * GLA_STEP_CHUNKS
    nt = seq // rows
    a3, lvl = _decay_tables()
    col = lambda j: pl.BlockSpec((rows, HG_WIDTH), lambda b, t: (b * nt + t, j))
    const = lambda shape: pl.BlockSpec(shape, lambda b, t: (0, 0))
    return pl.pallas_call(
        _hgrn_kernel,
        grid=(batch, nt),
        in_specs=[col(0), col(1), col(2), col(3), const((1, HG_WIDTH)), const((1, DK)),
                  const(a3.shape), const(lvl.shape)],
        out_specs=pl.BlockSpec((rows, HG_WIDTH), lambda b, t: (b * nt + t, 0)),
        out_shape=jax.ShapeDtypeStruct((n, HG_WIDTH), BF16),
        scratch_shapes=[pltpu.VMEM((HEADS, DK, DK), F32)],
        compiler_params=_params("parallel", "arbitrary"),
        name="hgrn",
    )(proj, proj, proj, proj, lb.reshape(1, HG_WIDTH), o_norm.reshape(1, DK), a3, lvl)


def _log_sigmoid(x):
    return jnp.minimum(x, 0.0) - jnp.log1p(jnp.exp(-jnp.abs(x)))


def _gla_kernel(q_ref, k_ref, v_ref, g_ref, a_ref, wa_ref, ba_ref, on_ref, a3_ref, lvl_ref, o_ref, st_ref):
    @pl.when(pl.program_id(1) == 0)
    def _():
        st_ref[...] = jnp.zeros_like(st_ref)

    lvl = lvl_ref[...]
    rowid = lax.broadcasted_iota(jnp.int32, (CHUNK, 1), 0)
    for c in range(GLA_STEP_CHUNKS):
        rows = slice(c * CHUNK, (c + 1) * CHUNK)
        gate_logit = _dot(a_ref[rows, :].astype(BF16), wa_ref[...]) + ba_ref[...]
        f = _decay_factors(a3_ref[...], _log_sigmoid(gate_logit) * (1.0 / GLA_TAU))
        for h in range(HEADS):
            sl = slice(h * DK, (h + 1) * DK)
            sv = slice(h * GLA_DV, (h + 1) * GLA_DV)
            q = q_ref[rows, sl] * (DK ** -0.5)
            o = _gla_chunk(q, k_ref[rows, sl], v_ref[rows, sv], f[:, sl], st_ref.at[h], lvl, rowid)
            o = _rms(o, on_ref[...])
            o_ref[rows, sv] = (o * _silu(g_ref[rows, sv])).astype(BF16)


def _gla(proj, w_alpha, b_alpha, o_norm, batch, seq):
    n = proj.shape[0]
    rows = CHUNK * GLA_STEP_CHUNKS
    nt = seq // rows
    qk_w = HEADS * DK
    v_w = HEADS * GLA_DV
    a3, lvl = _decay_tables()
    row = lambda b, t: b * nt + t
    const = lambda shape: pl.BlockSpec(shape, lambda b, t: (0, 0))
    return pl.pallas_call(
        _gla_kernel,
        grid=(batch, nt),
        in_specs=[pl.BlockSpec((rows, qk_w), lambda b, t: (row(b, t), 0)),
                  pl.BlockSpec((rows, qk_w), lambda b, t: (row(b, t), 1)),
                  pl.BlockSpec((rows, v_w), lambda b, t: (row(b, t), 1)),
                  pl.BlockSpec((rows, v_w), lambda b, t: (row(b, t), 2)),
                  pl.BlockSpec((rows, GLA_RANK_PAD), lambda b, t: (row(b, t), (2 * qk_w + 2 * v_w) // GLA_RANK_PAD)),
                  const((GLA_RANK_PAD, qk_w)), const((1, qk_w)), const((1, GLA_DV)),
                  const(a3.shape), const(lvl.shape)],
        out_specs=pl.BlockSpec((rows, v_w), lambda b, t: (row(b, t), 0)),
        out_shape=jax.ShapeDtypeStruct((n, v_w), BF16),
        scratch_shapes=[pltpu.VMEM((HEADS, GLA_DV, DK), F32)],
        compiler_params=_params("parallel", "arbitrary"),
        name="gla",
    )(proj, proj, proj, proj, proj, w_alpha, b_alpha.reshape(1, qk_w), o_norm.reshape(1, GLA_DV), a3, lvl)


def _qk_norm(x, w2):
    lane = lax.broadcasted_iota(jnp.int32, x.shape, 1)
    first = lane < DA_HEAD_DIM
    sq = x * x
    ms1 = jnp.sum(jnp.where(first, sq, 0.0), axis=-1, keepdims=True) * (1.0 / DA_HEAD_DIM)
    ms2 = jnp.sum(jnp.where(first, 0.0, sq), axis=-1, keepdims=True) * (1.0 / DA_HEAD_DIM)
    r = jnp.where(first, lax.rsqrt(ms1 + EPS), lax.rsqrt(ms2 + EPS))
    return x * r * w2, first


def _score_bound(q_norm_w, k_norm_w):
    return (jnp.max(jnp.abs(q_norm_w)) * jnp.max(jnp.abs(k_norm_w))
            * (DA_HEAD_DIM * DA_HEAD_DIM ** -0.5 * LOG2_E * SCORE_BOUND_MARGIN))


def _attn_kernel(qi_tab, ki_tab, q_ref, k_ref, v_ref, qn_ref, kn_ref, lam_ref, sub_ref, o_ref,
                 qs_ref, m_ref, l_ref, acc_ref, *, lam_init, bounded):
    p = pl.program_id(2)
    qi = qi_tab[p]
    ki = ki_tab[p]
    tq = ATTN_BLOCK
    dv = 2 * DA_HEAD_DIM
    heads = [slice(h * dv, (h + 1) * dv) for h in range(ATTN_HEADS)]

    @pl.when(ki == 0)
    def _():
        for h, sl in enumerate(heads):
            qn, first = _qk_norm(q_ref[:, sl], qn_ref[...])
            qn = qn * (DA_HEAD_DIM ** -0.5 * LOG2_E)
            q1 = jnp.where(first, qn, 0.0).T.astype(BF16)
            q2 = jnp.where(first, 0.0, qn).T.astype(BF16)
            for copy in range(1 if bounded else 2):
                qs_ref[h, copy, :, :tq] = q1
                qs_ref[h, copy, :, tq:] = q2
            if bounded:
                bound = _score_bound(qn_ref[...], kn_ref[...])
                row = lax.broadcasted_iota(jnp.int32, (dv, 2 * tq), 0)
                qs_ref[h, 1] = jnp.where(row == 0, -bound, 0.0).astype(BF16)
        if not bounded:
            m_ref[...] = jnp.full_like(m_ref, -jnp.inf)
        l_ref[...] = jnp.zeros_like(l_ref)
        acc_ref[...] = jnp.zeros_like(acc_ref)

    def step(masked):
        tk = k_ref.shape[0]
        ones = jnp.ones((16, tk), BF16)
        unit = jnp.where(lax.broadcasted_iota(jnp.int32, (tk, dv), 1) == 0, 1.0, 0.0).astype(BF16)
        for h, sl in enumerate(heads):
            kn, _ = _qk_norm(k_ref[:, sl], kn_ref[...])
            kn = kn.astype(BF16)

            def mask(s):
                if not masked:
                    return s
                key = lax.broadcasted_iota(jnp.int32, s.shape, 0)
                qry = lax.broadcasted_iota(jnp.int32, s.shape, 1)
                qry = jnp.where(qry >= tq, qry - tq, qry)
                return jnp.where(key <= qry, s, -jnp.inf)

            if bounded:
                shifted = mask(_dot(jnp.concatenate([kn, unit], axis=1), qs_ref[h].reshape(2 * dv, 2 * tq)))
                alpha = 1.0
            else:
                m_prev = m_ref[h]
                m_new = jnp.maximum(m_prev, jnp.max(mask(_dot(kn, qs_ref[h, 0])), axis=0, keepdims=True))
                alpha = jnp.exp2(m_prev - m_new)
                shifted = mask(_dot(kn, qs_ref[h, 1])) - m_new
                m_ref[h] = m_new
            pr = jnp.exp2(shifted).astype(BF16) if bounded else jnp.exp2(shifted.astype(BF16))
            vt = jnp.concatenate([v_ref[:, sl].T.astype(BF16), ones], axis=0)
            pv = _dot(vt, pr)
            l_ref[h] = alpha * l_ref[h] + pv[dv:dv + 1]
            acc_ref[h] = alpha * acc_ref[h] + pv[:dv]

    @pl.when(ki < qi)
    def _():
        step(False)

    @pl.when(ki == qi)
    def _():
        step(True)
        lv = lam_ref[...]
        lam = (jnp.exp(jnp.sum(lv[0:1] * lv[1:2], axis=-1, keepdims=True))
               - jnp.exp(jnp.sum(lv[2:3] * lv[3:4], axis=-1, keepdims=True)) + lam_init)
        for h, sl in enumerate(heads):
            o1 = acc_ref[h, :, :tq] / l_ref[h, :, :tq]
            o2 = acc_ref[h, :, tq:] / l_ref[h, :, tq:]
            o = o1 - lam * o2
            o = o * lax.rsqrt(jnp.mean(o * o, axis=0, keepdims=True) + EPS) * sub_ref[...] * (1.0 - lam_init)
            o_ref[:, sl] = o.T.astype(BF16)


def _diff_attn(proj, q_norm, k_norm, lam_vecs, subln, lam_init, batch, seq):
    n = proj.shape[0]
    blk = ATTN_BLOCK
    nq = seq // blk
    dv = 2 * DA_HEAD_DIM
    pairs = [(qi, ki) for qi in range(nq) for ki in range(qi + 1)]
    qi_tab = jnp.asarray([p[0] for p in pairs], jnp.int32)
    ki_tab = jnp.asarray([p[1] for p in pairs], jnp.int32)
    width = ATTN_HEADS * dv
    groups = HEADS // ATTN_HEADS
    q_col = (4 * HG_WIDTH) // width
    const = lambda shape: pl.BlockSpec(shape, lambda b, g, p, qt, kt: (0, 0))
    grid_spec = pltpu.PrefetchScalarGridSpec(
        num_scalar_prefetch=2,
        grid=(batch, groups, len(pairs)),
        in_specs=[pl.BlockSpec((blk, width), lambda b, g, p, qt, kt: (b * nq + qt[p], q_col + g)),
                  pl.BlockSpec((blk, width), lambda b, g, p, qt, kt: (b * nq + kt[p], q_col + groups + g)),
                  pl.BlockSpec((blk, width), lambda b, g, p, qt, kt: (b * nq + kt[p], q_col + 2 * groups + g)),
                  const((1, dv)), const((1, dv)), const((4, DA_HEAD_DIM)), const((dv, 1))],
        out_specs=pl.BlockSpec((blk, width), lambda b, g, p, qt, kt: (b * nq + qt[p], g)),
        scratch_shapes=[pltpu.VMEM((ATTN_HEADS, 2, dv, 2 * blk), BF16),
                        pltpu.VMEM((ATTN_HEADS, 1, 2 * blk), F32),
                        pltpu.VMEM((ATTN_HEADS, 1, 2 * blk), F32),
                        pltpu.VMEM((ATTN_HEADS, dv, 2 * blk), F32)])
    args = (qi_tab, ki_tab, proj, proj, proj, jnp.tile(q_norm, 2).reshape(1, -1), jnp.tile(k_norm, 2).reshape(1, -1),
            lam_vecs, subln.reshape(-1, 1))

    def run(bounded):
        return pl.pallas_call(
            functools.partial(_attn_kernel, lam_init=lam_init, bounded=bounded),
            grid_spec=grid_spec,
            out_shape=jax.ShapeDtypeStruct((n, HEADS * dv), BF16),
            compiler_params=_params("parallel", "parallel", "arbitrary"),
            name="diff_attn_bounded" if bounded else "diff_attn",
        )(*args)

    return lax.cond(_score_bound(q_norm, k_norm) <= MAX_SCORE_BOUND, lambda: run(True), lambda: run(False))


def _outproj_router_kernel(*refs, n_a):
    a_refs = refs[:n_a]
    (w_ref, x_ref, nw_ref, wr_hi_ref, wr_lo_ref, br_ref, tri_ref,
     x2_ref, meta_ref, cnt_ref, carry_ref) = refs[n_a:]

    @pl.when(pl.program_id(0) == 0)
    def _():
        carry_ref[...] = jnp.zeros_like(carry_ref)

    x2 = _load_tokens(x_ref)
    off = 0
    for a_ref in a_refs:
        kk = a_ref.shape[1]
        x2 = x2 + _dot(a_ref[...], w_ref[off:off + kk, :])
        off += kk
    _store_tokens(x2_ref, x2)

    h = _rms(x2, nw_ref[...])
    h_hi = h.astype(BF16)
    h_lo = (h - h_hi.astype(F32)).astype(BF16)
    wr_hi = wr_hi_ref[...]
    by_hi = _dot_nt(jnp.concatenate([wr_hi, wr_lo_ref[...]], axis=0), h_hi)
    lt = by_hi[:ROUTER_ROWS] + by_hi[ROUTER_ROWS:] + _dot_nt(wr_hi, h_lo) + br_ref[...]
    row = lambda r: lt[r:r + 1, :]

    gl = [row(g) for g in range(N_GROUPS)]
    gmax = functools.reduce(jnp.maximum, gl)
    gsel = jnp.where(gl[0] == gmax, 0.0, jnp.where(gl[1] == gmax, 1.0, jnp.where(gl[2] == gmax, 2.0, 3.0)))
    g_w = 1.0 / functools.reduce(jnp.add, [jnp.exp(x - gmax) for x in gl])

    el = []
    for e in range(EXPERTS_PER_GROUP):
        v = row(N_GROUPS + e)
        for g in range(1, N_GROUPS):
            v = jnp.where(gsel == float(g), row(N_GROUPS + EXPERTS_PER_GROUP * g + e), v)
        el.append(v)
    emax = functools.reduce(jnp.maximum, el)
    sel = []
    for e in range(EXPERTS_PER_GROUP):
        ahead = jnp.zeros_like(emax)
        for o in range(EXPERTS_PER_GROUP):
            if o < e:
                ahead = ahead + jnp.where(el[o] >= el[e], 1.0, 0.0)
            elif o > e:
                ahead = ahead + jnp.where(el[o] > el[e], 1.0, 0.0)
        sel.append(ahead < 2.0)
    pe = [jnp.where(sel[e], jnp.exp(el[e] - emax), 0.0) for e in range(EXPERTS_PER_GROUP)]
    scale = g_w / functools.reduce(jnp.add, pe)
    zero = jnp.zeros_like(emax)
    seen = zero
    lo_idx, hi_idx, c_lo, c_hi = zero, zero, zero, zero
    for e in range(EXPERTS_PER_GROUP):
        is_lo = sel[e] & (seen == 0.0)
        is_hi = sel[e] & (seen == 1.0)
        lo_idx = jnp.where(is_lo, float(e), lo_idx)
        hi_idx = jnp.where(is_hi, float(e), hi_idx)
        c_lo = jnp.where(is_lo, pe[e] * scale, c_lo)
        c_hi = jnp.where(is_hi, pe[e] * scale, c_hi)
        seen = seen + jnp.where(sel[e], 1.0, 0.0)
    pair = lo_idx * (7.0 - lo_idx) * 0.5 + (hi_idx - lo_idx - 1.0)
    bucket = gsel * float(N_PAIRS) + pair

    bid = lax.broadcasted_iota(jnp.int32, (ROUTER_ROWS, bucket.shape[1]), 0).astype(F32)
    onehot = jnp.where(bid == bucket, 1.0, 0.0)
    before = _dot(onehot.astype(BF16), tri_ref[...]) + carry_ref[...]
    rank = jnp.sum(onehot * before, axis=0, keepdims=True)
    carry = carry_ref[...] + jnp.sum(onehot, axis=1, keepdims=True)
    carry_ref[...] = carry
    cnt_ref[...] = carry
    meta_ref[...] = jnp.concatenate([bucket, rank, c_lo, c_hi, zero, zero, zero, zero], axis=0)


def _outproj_router(a_list, w_out, x, n, norm_w, w_group, b_group, w_router, b_router):
    d = D_MODEL
    tm = ROW_TILE
    tok = lambda: pl.BlockSpec((tm * LANE_TILES, LANE), lambda i: (i, 0))
    wr =jnp.concatenate([w_group, w_router.transpose(1, 0, 2).reshape(d, N_EXPERTS)], axis=1)
    wr_t = jnp.zeros((ROUTER_ROWS, d), F32).at[:wr.shape[1]].set(wr.T)
    wr_hi = wr_t.astype(BF16)
    wr_lo = (wr_t - wr_hi.astype(F32)).astype(BF16)
    br = jnp.zeros((ROUTER_ROWS, 1), F32).at[:wr.shape[1], 0].set(
        jnp.concatenate([b_group, b_router.reshape(-1)]))
    tri = jnp.asarray(np.triu(np.ones((tm, tm), np.float32), 1), BF16)
    const = lambda shape: pl.BlockSpec(shape, lambda i: (0, 0))
    in_specs = [pl.BlockSpec((tm, a.shape[1]), lambda i: (i, 0)) for a in a_list]
    in_specs += [const((d, d)), _token_spec(x, tm), const((1, d)),
                 const((ROUTER_ROWS, d)), const((ROUTER_ROWS, d)), const((ROUTER_ROWS, 1)), const((tm, tm))]
    return pl.pallas_call(
        functools.partial(_outproj_router_kernel, n_a=len(a_list)),
        grid=(n // tm,),
        in_specs=in_specs,
        out_specs=[tok(),
                   pl.BlockSpec((8, tm), lambda i: (0, i)),
                   const((ROUTER_ROWS, 1))],
        out_shape=[jax.ShapeDtypeStruct((n * LANE_TILES, LANE), F32),
                   jax.ShapeDtypeStruct((8, n), F32),
                   jax.ShapeDtypeStruct((ROUTER_ROWS, 1), F32)],
        scratch_shapes=[pltpu.VMEM((ROUTER_ROWS, 1), F32)],
        compiler_params=_params("arbitrary"),
        name="outproj_router",
    )(*a_list, w_out, x, norm_w.reshape(1, d), wr_hi, wr_lo, br, tri)


def _gather_tokens(table, idx):
    m = idx.shape[0]
    info = plsc.get_sparse_core_info()
    n_workers = info.num_cores * info.num_subcores
    per_worker = m // n_workers
    n_chunks = per_worker // SC_CHUNK
    assert per_worker * n_workers == m and n_chunks * SC_CHUNK == per_worker and n_chunks % 2 == 0
    mesh = plsc.VectorSubcoreMesh(core_axis_name="core", subcore_axis_name="subcore")
    rows = pltpu.VMEM((SC_CHUNK,) + table.shape[1:], table.dtype)

    @functools.partial(
        pl.kernel, mesh=mesh,
        out_type=jax.ShapeDtypeStruct((m,) + table.shape[1:], table.dtype),
        scratch_types=[pltpu.VMEM((n_chunks, SC_CHUNK), jnp.int32), rows, rows,
                       pltpu.SemaphoreType.DMA, pltpu.SemaphoreType.DMA],
        name="gather_tokens")
    def gather(table_hbm, idx_hbm, out_hbm, idx_v, rows_a, rows_b, sem_a, sem_b):
        worker = lax.axis_index("subcore") * info.num_cores + lax.axis_index("core")
        bufs = ((rows_a, sem_a), (rows_b, sem_b))

        def fetch(j, buf):
            return pltpu.make_async_copy(table_hbm.at[idx_v.at[j]], bufs[buf][0], bufs[buf][1])

        pltpu.sync_copy(idx_hbm.at[worker], idx_v)
        fetch(0, 0).start()

        @pl.loop(0, n_chunks, step=2)
        def _(j0):
            for buf in range(2):
                j = j0 + buf
                fetch(j, buf).wait()

                @pl.when(j + 1 < n_chunks)
                def _():
                    fetch(j + 1, 1 - buf).start()

                off = pl.multiple_of(worker * per_worker + j * SC_CHUNK, SC_CHUNK)
                pltpu.sync_copy(bufs[buf][0], out_hbm.at[pl.ds(off, SC_CHUNK)])

    return gather(table, idx.reshape(n_workers, n_chunks, SC_CHUNK))


def _moe_kernel(tid, e_lo, e_hi, n_tiles, x_ref, cw_ref, nw_ref, g_lo_ref, g_hi_ref, u_lo_ref, u_hi_ref,
                d_lo_ref, d_hi_ref, o_ref):
    del tid, e_lo, e_hi

    @pl.when(pl.program_id(0) < n_tiles[0])
    def _():
        x = _load_tokens(x_ref)
        h = _rms(x, nw_ref[...]).astype(BF16)
        cw = cw_ref[...]

        def expert(g_ref, u_ref, d_ref):
            act = _silu(_dot(h, g_ref[0])) * _dot(h, u_ref[0])
            return _dot(act.astype(BF16), d_ref[0])

        y = (cw[:, 1:2] * expert(g_lo_ref, u_lo_ref, d_lo_ref)
             + cw[:, 2:3] * expert(g_hi_ref, u_hi_ref, d_hi_ref))
        _store_tokens(o_ref, x + y)


def _moe(x2, n, meta, cnt, norm_w, w_gate, w_up, w_down):
    d = D_MODEL
    lt = LANE_TILES
    tm = MOE_TILE
    max_tiles = n // tm + N_BUCKETS
    n_slots = max_tiles * tm
    counts = cnt[:N_BUCKETS, 0].astype(jnp.int32)
    tiles_per_bucket = (counts + tm - 1) // tm
    tile_end = jnp.cumsum(tiles_per_bucket)
    row_start = (tile_end - tiles_per_bucket) * tm
    n_tiles = tile_end[-1]
    in_bucket = meta[0][:, None] == jnp.arange(N_BUCKETS, dtype=F32)[None, :]
    dest = meta[1].astype(jnp.int32) + jnp.sum(jnp.where(in_bucket, row_start[None, :], 0), axis=1)
    packed = jnp.stack([jnp.arange(n, dtype=F32), meta[2], meta[3], jnp.ones((n,), F32)], axis=1)
    slots = jnp.zeros((n_slots, 4), F32).at[dest].set(packed, unique_indices=True, mode="promise_in_bounds")
    src = jnp.where(slots[:, 3] > 0.0, slots[:, 0].astype(jnp.int32), jnp.arange(n_slots, dtype=jnp.int32) % n)
    x_sorted = _gather_tokens(x2.reshape(n, lt, LANE), src).reshape(n_slots * lt, LANE)

    tile_ids = jnp.minimum(jnp.arange(max_tiles, dtype=jnp.int32), n_tiles - 1)
    tile_bucket = jnp.sum((tile_ids[:, None] >= tile_end[None, :]).astype(jnp.int32), axis=1)
    group = tile_bucket // N_PAIRS
    pair = tile_bucket % N_PAIRS
    pair_lo = jnp.sum(jnp.where(pair[:, None] == jnp.arange(N_PAIRS)[None, :], jnp.asarray(PAIR_LO, jnp.int32)[None, :], 0), axis=1)
    pair_hi = jnp.sum(jnp.where(pair[:, None] == jnp.arange(N_PAIRS)[None, :], jnp.asarray(PAIR_HI, jnp.int32)[None, :], 0), axis=1)
    e_lo = group * EXPERTS_PER_GROUP + pair_lo
    e_hi = group * EXPERTS_PER_GROUP + pair_hi

    grid_spec = pltpu.PrefetchScalarGridSpec(
        num_scalar_prefetch=4,
        grid=(max_tiles,),
        in_specs=[pl.BlockSpec((tm * lt, LANE), lambda t, tid, lo, hi, nt: (tid[t], 0)),
                  pl.BlockSpec((tm, 4), lambda t, tid, lo, hi, nt: (tid[t], 0)),
                  pl.BlockSpec((1, d), lambda t, tid, lo, hi, nt: (0, 0)),
                  pl.BlockSpec((1, d, D_EXPERT), lambda t, tid, lo, hi, nt: (lo[t], 0, 0)),
                  pl.BlockSpec((1, d, D_EXPERT), lambda t, tid, lo, hi, nt: (hi[t], 0, 0)),
                  pl.BlockSpec((1, d, D_EXPERT), lambda t, tid, lo, hi, nt: (lo[t], 0, 0)),
                  pl.BlockSpec((1, d, D_EXPERT), lambda t, tid, lo, hi, nt: (hi[t], 0, 0)),
                  pl.BlockSpec((1, D_EXPERT, d), lambda t, tid, lo, hi, nt: (lo[t], 0, 0)),
                  pl.BlockSpec((1, D_EXPERT, d), lambda t, tid, lo, hi, nt: (hi[t], 0, 0))],
        out_specs=pl.BlockSpec((tm * lt, LANE), lambda t, tid, lo, hi, nt: (tid[t], 0)))
    out_sorted = pl.pallas_call(
        _moe_kernel,
        grid_spec=grid_spec,
        out_shape=jax.ShapeDtypeStruct((n_slots * lt, LANE), F32),
        compiler_params=_params("arbitrary"),
        name="moe",
    )(tile_ids, e_lo, e_hi, n_tiles.reshape(1), x_sorted, slots, norm_w.reshape(1, d),
      w_gate, w_gate, w_up, w_up, w_down, w_down)
    return _gather_tokens(out_sorted.reshape(n_slots, lt, LANE), dest).reshape(n * lt, LANE)


def kernel(x, attn_norm, ffn_norm, even_w_in, even_w_out, hg_lb_logits, hg_o_norm, da_q_norm, da_k_norm, da_lambda, da_subln, odd_w_in, gla_w_alpha, gla_b_alpha, gla_o_norm, odd_w_out, moe_w_group, moe_b_group, moe_w_router, moe_b_router, moe_w_gate, moe_w_up, moe_w_down):
    batch, seq, d = x.shape
    depth = attn_norm.shape[0]
    n = batch * seq
    xf = x.reshape(n, d)

    lb_sm = jax.nn.softmax(hg_lb_logits.astype(F32), axis=0)
    lb_all = jnp.cumsum(lb_sm, axis=0) - lb_sm[0:1]
    rank = gla_w_alpha.shape[1]

    for layer in range(depth):
        j = layer // 2
        if layer % 2 == 0:
            lam_init = 0.8 - 0.6 * math.exp(-0.3 * layer)
            proj = _inproj(xf, n, attn_norm[layer], even_w_in[j].astype(BF16))
            o_a = _hgrn(proj, lb_all[j], hg_o_norm[j], batch, seq)
            o_b = _diff_attn(proj, da_q_norm[j], da_k_norm[j], da_lambda[j], da_subln[j], lam_init, batch, seq)
            mix = [o_a, o_b]
            w_out = even_w_out[j]
        else:
            w_in = jnp.pad(odd_w_in[j], ((0, 0), (0, GLA_RANK_PAD - rank))).astype(BF16)
            w_alpha = jnp.pad(gla_w_alpha[j], ((0, GLA_RANK_PAD - rank), (0, 0))).astype(BF16)
            proj = _inproj(xf, n, attn_norm[layer], w_in)
            mix = [_gla(proj, w_alpha, gla_b_alpha[j], gla_o_norm[j], batch, seq)]
            w_out = odd_w_out[j]
        x2, meta, cnt = _outproj_router(mix, w_out.astype(BF16), xf, n, ffn_norm[layer], moe_w_group[layer],
                                        moe_b_group[layer], moe_w_router[layer], moe_b_router[layer])
        xf = _moe(x2, n, meta, cnt, ffn_norm[layer], moe_w_gate[layer].astype(BF16), moe_w_up[layer].astype(BF16),
                  moe_w_down[layer].astype(BF16))
    return xf.reshape(batch, seq, d)
```

```python
import functools
import math

import numpy as np
import jax
import jax.numpy as jnp
from jax import lax
from jax.experimental import pallas as pl
from jax.experimental.pallas import tpu as pltpu
from jax.experimental.pallas import tpu_sc as plsc

F32 = jnp.float32
BF16 = jnp.bfloat16
EPS = 1e-6
LOG2_E = 1.4426950408889634

D_MODEL = 1024
LANE = 128
LANE_TILES = D_MODEL // LANE
HEADS = 4
DK = 128
HG_WIDTH = 512
DA_HEAD_DIM = 64
GLA_DV = 256
GLA_RANK_PAD = 128
GLA_TAU = 16.0
N_GROUPS = 4
EXPERTS_PER_GROUP = 4
N_EXPERTS = 16
D_EXPERT = 512
N_PAIRS = 6
N_BUCKETS = N_GROUPS * N_PAIRS
ROUTER_ROWS = 32

CHUNK = 128
GLA_STEP_CHUNKS = 4
LEVELS = 7
ATTN_BLOCK = 512
ATTN_HEADS = 2
SCORE_BOUND_MARGIN = 1.05
MAX_SCORE_BOUND = 60.0
ROW_TILE = 512
MOE_TILE = 256
SC_CHUNK = 32
VMEM_LIMIT = 56 * 1024 * 1024
CAST_BLOCK_BYTES = 8 * 1024 * 1024

PAIR_LO = (0, 0, 0, 1, 1, 2)
PAIR_HI = (1, 2, 3, 2, 3, 3)


def _dot(a, b):
    return jnp.dot(a, b, preferred_element_type=F32)


def _dot_nt(a, b):
    return lax.dot_general(a, b, (((1,), (1,)), ((), ())), preferred_element_type=F32)


def _dot_tn(a, b):
    return lax.dot_general(a, b, (((0,), (0,)), ((), ())), preferred_element_type=F32)


def _rms(x, w):
    return x * lax.rsqrt(jnp.mean(x * x, axis=-1, keepdims=True) + EPS) * w


def _silu(x):
    return x * jax.nn.sigmoid(x)


def _params(*sem):
    return pltpu.CompilerParams(dimension_semantics=sem, vmem_limit_bytes=VMEM_LIMIT)


def _load_tokens(ref):
    if ref.shape[1] == D_MODEL:
        return ref[...]
    rows = ref.shape[0] // LANE_TILES
    return jnp.concatenate([ref[pl.ds(c, rows, stride=LANE_TILES), :] for c in range(LANE_TILES)], axis=1)


def _token_spec(x, rows):
    shape = (rows, D_MODEL) if x.shape[1] == D_MODEL else (rows * LANE_TILES, LANE)
    return pl.BlockSpec(shape, lambda i: (i, 0))


def _store_tokens(ref, val):
    rows = ref.shape[0] // LANE_TILES
    for c in range(LANE_TILES):
        ref[pl.ds(c, rows, stride=LANE_TILES), :] = val[:, c * LANE:(c + 1) * LANE]


def _cast_kernel(w_ref, o_ref):
    n = w_ref.shape[-1]
    o_ref[:, :, :n] = w_ref[...].astype(BF16)
    if o_ref.shape[-1] > n:
        o_ref[:, :, n:] = jnp.zeros(o_ref.shape[:2] + (o_ref.shape[-1] - n,), BF16)


def _to_bf16(w, n_pad=None):
    l, k, n = w.shape
    n_pad = n_pad or n
    per_matrix = k * n * 4
    lb = max([c for c in range(1, l + 1) if l % c == 0 and c * per_matrix <= CAST_BLOCK_BYTES] or [1])
    kb = k
    while lb == 1 and kb * n * 4 > CAST_BLOCK_BYTES and kb % 32 == 0:
        kb //= 2
    return pl.pallas_call(
        _cast_kernel,
        grid=(l // lb, k // kb),
        in_specs=[pl.BlockSpec((lb, kb, n), lambda i, j: (i, j, 0))],
        out_specs=pl.BlockSpec((lb, kb, n_pad), lambda i, j: (i, j, 0)),
        out_shape=jax.ShapeDtypeStruct((l, k, n_pad), BF16),
        compiler_params=_params("parallel", "parallel"),
        name="to_bf16",
    )(w)


def _inproj_kernel(x_ref, nw_ref, w_ref, o_ref):
    h = _rms(_load_tokens(x_ref), nw_ref[...]).astype(BF16)
    o_ref[...] = _dot(h, w_ref[...])


def _inproj(x, n, norm_w, w):
    d = D_MODEL
    n_out = w.shape[1]
    return pl.pallas_call(
        _inproj_kernel,
        grid=(n // ROW_TILE,),
        in_specs=[_token_spec(x, ROW_TILE),
                  pl.BlockSpec((1, d), lambda i: (0, 0)),
                  pl.BlockSpec((d, n_out), lambda i: (0, 0))],
        out_specs=pl.BlockSpec((ROW_TILE, n_out), lambda i: (i, 0)),
        out_shape=jax.ShapeDtypeStruct((n, n_out), F32),
        compiler_params=_params("parallel"),
        name="inproj",
    )(x, norm_w.reshape(1, d), w)


def _decay_tables():
    c = CHUNK
    a = np.zeros((LEVELS + 2, c, c), np.float32)
    t = np.arange(c)
    for l in range(LEVELS):
        s = 1 << l
        for r in range(c):
            mid = (r // (2 * s)) * 2 * s + s - 1
            if (r >> l) & 1:
                a[l, r] = (t > mid) & (t <= r)
            else:
                a[l, r] = (t > r) & (t <= mid)
    a[LEVELS] = t[None, :] <= t[:, None]
    a[LEVELS + 1] = t[None, :] > t[:, None]
    a = a.reshape((LEVELS + 2) * c, c)
    a3 = np.concatenate([a, a], axis=1)
    i, j = np.meshgrid(t, t, indexing="ij")
    x = np.maximum(i ^ j, 1)
    lvl = np.where(j < i, np.floor(np.log2(x)).astype(np.int32), np.where(i == j, LEVELS, -1))
    return jnp.asarray(a3, BF16), jnp.asarray(lvl, jnp.int32)


def _decay_factors(a3, g):
    g_hi = g.astype(BF16)
    g_lo = (g - g_hi.astype(F32)).astype(BF16)
    return jnp.exp(_dot(a3, jnp.concatenate([g_hi, g_lo], axis=0)))


def _gla_chunk(q, k, v, f, st_ref, lvl, rowid):
    c = CHUNK
    sc = jnp.where(lvl == LEVELS, jnp.sum(q * k, axis=-1, keepdims=True), 0.0)
    for l in range(LEVELS):
        is_query = ((rowid >> l) & 1) == 1
        m = (jnp.where(is_query, q, k) * f[l * c:(l + 1) * c]).astype(BF16)
        sc = jnp.where(lvl == l, _dot_nt(m, m), sc)
    qd = (q * f[LEVELS * c:(LEVELS + 1) * c]).astype(BF16)
    kd = (k * f[(LEVELS + 1) * c:(LEVELS + 2) * c]).astype(BF16)
    vb = v.astype(BF16)
    st = st_ref[...]
    o = _dot(sc.astype(BF16), vb) + _dot_nt(qd, st.astype(BF16))
    chunk_decay = f[(LEVELS + 1) * c - 1:(LEVELS + 1) * c, :]
    st_ref[...] = st * chunk_decay + _dot_tn(vb, kd)
    return o


def _hgrn_kernel(q_ref, f_ref, i_ref, g_ref, lb_ref, on_ref, a3_ref, lvl_ref, o_ref, st_ref):
    @pl.when(pl.program_id(1) == 0)
    def _():
        st_ref[...] = jnp.zeros_like(st_ref)

    lvl = lvl_ref[...]
    rowid = lax.broadcasted_iota(jnp.int32, (CHUNK, 1), 0)
    lb = lb_ref[...]
    for c in range(GLA_STEP_CHUNKS):
        rows = slice(c * CHUNK, (c + 1) * CHUNK)
        fl = f_ref[rows, :]
        f = _decay_factors(a3_ref[...], jnp.log(lb + (1.0 - lb) * jax.nn.sigmoid(fl)))
        key_in = (1.0 - lb) * jax.nn.sigmoid(-fl)
        for h in range(HEADS):
            sl = slice(h * DK, (h + 1) * DK)
            q = _silu(q_ref[rows, sl]) * (DK ** -0.5)
            o = _gla_chunk(q, key_in[:, sl], i_ref[rows, sl], f[:, sl], st_ref.at[h], lvl, rowid)
            o = _rms(o, on_ref[...])
            o_ref[rows, sl] = (o * _silu(g_ref[rows, sl])).astype(BF16)


def _hgrn(proj, lb, o_norm, batch, seq):
    n = proj.shape[0]
    rows = CHUNK * GLA_STEP_CHUNKS
    nt = seq // rows
    a3, lvl = _decay_tables()
    col = lambda j: pl.BlockSpec((rows, HG_WIDTH), lambda b, t: (b * nt + t, j))
    const = lambda shape: pl.BlockSpec(shape, lambda b, t: (0, 0))
    return pl.pallas_call(
        _hgrn_kernel,
        grid=(batch, nt),
        in_specs=[col(0), col(1), col(2), col(3), const((1, HG_WIDTH)), const((1, DK)),
                  const(a3.shape), const(lvl.shape)],
        out_specs=pl.BlockSpec((rows, HG_WIDTH), lambda b, t: (b * nt + t, 0)),
        out_shape=jax.ShapeDtypeStruct((n, HG_WIDTH), BF16),
        scratch_shapes=[pltpu.VMEM((HEADS, DK, DK), F32)],
        compiler_params=_params("parallel", "arbitrary"),
        name="hgrn",
    )(proj, proj, proj, proj, lb.reshape(1, HG_WIDTH), o_norm.reshape(1, DK), a3, lvl)


def _log_sigmoid(x):
    return jnp.minimum(x, 0.0) - jnp.log1p(jnp.exp(-jnp.abs(x)))


def _gla_kernel(q_ref, k_ref, v_ref, g_ref, a_ref, wa_ref, ba_ref, on_ref, a3_ref, lvl_ref, o_ref, st_ref):
    @pl.when(pl.program_id(1) == 0)
    def _():
        st_ref[...] = jnp.zeros_like(st_ref)

    lvl = lvl_ref[...]
    rowid = lax.broadcasted_iota(jnp.int32, (CHUNK, 1), 0)
    for c in range(GLA_STEP_CHUNKS):
        rows = slice(c * CHUNK, (c + 1) * CHUNK)
        gate_logit = _dot(a_ref[rows, :].astype(BF16), wa_ref[...]) + ba_ref[...]
        f = _decay_factors(a3_ref[...], _log_sigmoid(gate_logit) * (1.0 / GLA_TAU))
        for h in range(HEADS):
            sl = slice(h * DK, (h + 1) * DK)
            sv = slice(h * GLA_DV, (h + 1) * GLA_DV)
            q = q_ref[rows, sl] * (DK ** -0.5)
            o = _gla_chunk(q, k_ref[rows, sl], v_ref[rows, sv], f[:, sl], st_ref.at[h], lvl, rowid)
            o = _rms(o, on_ref[...])
            o_ref[rows, sv] = (o * _silu(g_ref[rows, sv])).astype(BF16)


def _gla(proj, w_alpha, b_alpha, o_norm, batch, seq):
    n = proj.shape[0]
    rows = CHUNK * GLA_STEP_CHUNKS
    nt = seq // rows
    qk_w = HEADS * DK
    v_w = HEADS * GLA_DV
    a3, lvl = _decay_tables()
    row = lambda b, t: b * nt + t
    const = lambda shape: pl.BlockSpec(shape, lambda b, t: (0, 0))
    return pl.pallas_call(
        _gla_kernel,
        grid=(batch, nt),
        in_specs=[pl.BlockSpec((rows, qk_w), lambda b, t: (row(b, t), 0)),
                  pl.BlockSpec((rows, qk_w), lambda b, t: (row(b, t), 1)),
                  pl.BlockSpec((rows, v_w), lambda b, t: (row(b, t), 1)),
                  pl.BlockSpec((rows, v_w), lambda b, t: (row(b, t), 2)),
                  pl.BlockSpec((rows, GLA_RANK_PAD), lambda b, t: (row(b, t), (2 * qk_w + 2 * v_w) // GLA_RANK_PAD)),
                  const((GLA_RANK_PAD, qk_w)), const((1, qk_w)), const((1, GLA_DV)),
                  const(a3.shape), const(lvl.shape)],
        out_specs=pl.BlockSpec((rows, v_w), lambda b, t: (row(b, t), 0)),
        out_shape=jax.ShapeDtypeStruct((n, v_w), BF16),
        scratch_shapes=[pltpu.VMEM((HEADS, GLA_DV, DK), F32)],
        compiler_params=_params("parallel", "arbitrary"),
        name="gla",
    )(proj, proj, proj, proj, proj, w_alpha, b_alpha.reshape(1, qk_w), o_norm.reshape(1, GLA_DV), a3, lvl)


def _qk_norm(x, w2):
    lane = lax.broadcasted_iota(jnp.int32, x.shape, 1)
    first = lane < DA_HEAD_DIM
    sq = x * x
    ms1 = jnp.sum(jnp.where(first, sq, 0.0), axis=-1, keepdims=True) * (1.0 / DA_HEAD_DIM)
    ms2 = jnp.sum(jnp.where(first, 0.0, sq), axis=-1, keepdims=True) * (1.0 / DA_HEAD_DIM)
    r = jnp.where(first, lax.rsqrt(ms1 + EPS), lax.rsqrt(ms2 + EPS))
    return x * r * w2, first


def _score_bound(q_norm_w, k_norm_w):
    return (jnp.max(jnp.abs(q_norm_w)) * jnp.max(jnp.abs(k_norm_w))
            * (DA_HEAD_DIM * DA_HEAD_DIM ** -0.5 * LOG2_E * SCORE_BOUND_MARGIN))


def _attn_kernel(qi_tab, ki_tab, q_ref, k_ref, v_ref, qn_ref, kn_ref, lam_ref, sub_ref, o_ref,
                 qs_ref, m_ref, l_ref, acc_ref, *, lam_init, bounded):
    p = pl.program_id(2)
    qi = qi_tab[p]
    ki = ki_tab[p]
    tq = ATTN_BLOCK
    dv = 2 * DA_HEAD_DIM
    heads = [slice(h * dv, (h + 1) * dv) for h in range(ATTN_HEADS)]

    @pl.when(ki == 0)
    def _():
        for h, sl in enumerate(heads):
            qn, first = _qk_norm(q_ref[:, sl], qn_ref[...])
            qn = qn * (DA_HEAD_DIM ** -0.5 * LOG2_E)
            q1 = jnp.where(first, qn, 0.0).T.astype(BF16)
            q2 = jnp.where(first, 0.0, qn).T.astype(BF16)
            for copy in range(1 if bounded else 2):
                qs_ref[h, copy, :, :tq] = q1
                qs_ref[h, copy, :, tq:] = q2
            if bounded:
                bound = _score_bound(qn_ref[...], kn_ref[...])
                row = lax.broadcasted_iota(jnp.int32, (dv, 2 * tq), 0)
                qs_ref[h, 1] = jnp.where(row == 0, -bound, 0.0).astype(BF16)
        if not bounded:
            m_ref[...] = jnp.full_like(m_ref, -jnp.inf)
        l_ref[...] = jnp.zeros_like(l_ref)
        acc_ref[...] = jnp.zeros_like(acc_ref)

    def step(masked):
        tk = k_ref.shape[0]
        ones = jnp.ones((16, tk), BF16)
        unit = jnp.where(lax.broadcasted_iota(jnp.int32, (tk, dv), 1) == 0, 1.0, 0.0).astype(BF16)
        for h, sl in enumerate(heads):
            kn, _ = _qk_norm(k_ref[:, sl], kn_ref[...])
            kn = kn.astype(BF16)

            def mask(s):
                if not masked:
                    return s
                key = lax.broadcasted_iota(jnp.int32, s.shape, 0)
                qry = lax.broadcasted_iota(jnp.int32, s.shape, 1)
                qry = jnp.where(qry >= tq, qry - tq, qry)
                return jnp.where(key <= qry, s, -jnp.inf)

            if bounded:
                shifted = mask(_dot(jnp.concatenate([kn, unit], axis=1), qs_ref[h].reshape(2 * dv, 2 * tq)))
                alpha = 1.0
            else:
                m_prev = m_ref[h]
                m_new = jnp.maximum(m_prev, jnp.max(mask(_dot(kn, qs_ref[h, 0])), axis=0, keepdims=True))
                alpha = jnp.exp2(m_prev - m_new)
                shifted = mask(_dot(kn, qs_ref[h, 1])) - m_new
                m_ref[h] = m_new
            pr = jnp.exp2(shifted).astype(BF16) if bounded else jnp.exp2(shifted.astype(BF16))
            vt = jnp.concatenate([v_ref[:, sl].T.astype(BF16), ones], axis=0)
            pv = _dot(vt, pr)
            l_ref[h] = alpha * l_ref[h] + pv[dv:dv + 1]
            acc_ref[h] = alpha * acc_ref[h] + pv[:dv]

    @pl.when(ki < qi)
    def _():
        step(False)

    @pl.when(ki == qi)
    def _():
        step(True)
        lv = lam_ref[...]
        lam = (jnp.exp(jnp.sum(lv[0:1] * lv[1:2], axis=-1, keepdims=True))
               - jnp.exp(jnp.sum(lv[2:3] * lv[3:4], axis=-1, keepdims=True)) + lam_init)
        for h, sl in enumerate(heads):
            o1 = acc_ref[h, :, :tq] / l_ref[h, :, :tq]
            o2 = acc_ref[h, :, tq:] / l_ref[h, :, tq:]
            o = o1 - lam * o2
            o = o * lax.rsqrt(jnp.mean(o * o, axis=0, keepdims=True) + EPS) * sub_ref[...] * (1.0 - lam_init)
            o_ref[:, sl] = o.T.astype(BF16)


def _diff_attn(proj, q_norm, k_norm, lam_vecs, subln, lam_init, batch, seq):
    n = proj.shape[0]
    blk = ATTN_BLOCK
    nq = seq // blk
    dv = 2 * DA_HEAD_DIM
    pairs = [(qi, ki) for qi in range(nq) for ki in range(qi + 1)]
    qi_tab = jnp.asarray([p[0] for p in pairs], jnp.int32)
    ki_tab = jnp.asarray([p[1] for p in pairs], jnp.int32)
    width = ATTN_HEADS * dv
    groups = HEADS // ATTN_HEADS
    q_col = (4 * HG_WIDTH) // width
    const = lambda shape: pl.BlockSpec(shape, lambda b, g, p, qt, kt: (0, 0))
    grid_spec = pltpu.PrefetchScalarGridSpec(
        num_scalar_prefetch=2,
        grid=(batch, groups, len(pairs)),
        in_specs=[pl.BlockSpec((blk, width), lambda b, g, p, qt, kt: (b * nq + qt[p], q_col + g)),
                  pl.BlockSpec((blk, width), lambda b, g, p, qt, kt: (b * nq + kt[p], q_col + groups + g)),
                  pl.BlockSpec((blk, width), lambda b, g, p, qt, kt: (b * nq + kt[p], q_col + 2 * groups + g)),
                  const((1, dv)), const((1, dv)), const((4, DA_HEAD_DIM)), const((dv, 1))],
        out_specs=pl.BlockSpec((blk, width), lambda b, g, p, qt, kt: (b * nq + qt[p], g)),
        scratch_shapes=[pltpu.VMEM((ATTN_HEADS, 2, dv, 2 * blk), BF16),
                        pltpu.VMEM((ATTN_HEADS, 1, 2 * blk), F32),
                        pltpu.VMEM((ATTN_HEADS, 1, 2 * blk), F32),
                        pltpu.VMEM((ATTN_HEADS, dv, 2 * blk), F32)])
    args = (qi_tab, ki_tab, proj, proj, proj, jnp.tile(q_norm, 2).reshape(1, -1), jnp.tile(k_norm, 2).reshape(1, -1),
            lam_vecs, subln.reshape(-1, 1))

    def run(bounded):
        return pl.pallas_call(
            functools.partial(_attn_kernel, lam_init=lam_init, bounded=bounded),
            grid_spec=grid_spec,
            out_shape=jax.ShapeDtypeStruct((n, HEADS * dv), BF16),
            compiler_params=_params("parallel", "parallel", "arbitrary"),
            name="diff_attn_bounded" if bounded else "diff_attn",
        )(*args)

    return lax.cond(_score_bound(q_norm, k_norm) <= MAX_SCORE_BOUND, lambda: run(True), lambda: run(False))


def _outproj_router_kernel(*refs, n_a):
    a_refs = refs[:n_a]
    (w_ref, x_ref, nw_ref, wr_hi_ref, wr_lo_ref, br_ref, tri_ref,
     x2_ref, meta_ref, cnt_ref, carry_ref) = refs[n_a:]

    @pl.when(pl.program_id(0) == 0)
    def _():
        carry_ref[...] = jnp.zeros_like(carry_ref)

    x2 = _load_tokens(x_ref)
    off = 0
    for a_ref in a_refs:
        kk = a_ref.shape[1]
        x2 = x2 + _dot(a_ref[...], w_ref[off:off + kk, :])
        off += kk
    _store_tokens(x2_ref, x2)

    h = _rms(x2, nw_ref[...])
    h_hi = h.astype(BF16)
    h_lo = (h - h_hi.astype(F32)).astype(BF16)
    wr_hi = wr_hi_ref[...]
    by_hi = _dot_nt(jnp.concatenate([wr_hi, wr_lo_ref[...]], axis=0), h_hi)
    lt = by_hi[:ROUTER_ROWS] + by_hi[ROUTER_ROWS:] + _dot_nt(wr_hi, h_lo) + br_ref[...]
    row = lambda r: lt[r:r + 1, :]

    gl = [row(g) for g in range(N_GROUPS)]
    gmax = functools.reduce(jnp.maximum, gl)
    gsel = jnp.where(gl[0] == gmax, 0.0, jnp.where(gl[1] == gmax, 1.0, jnp.where(gl[2] == gmax, 2.0, 3.0)))
    g_w = 1.0 / functools.reduce(jnp.add, [jnp.exp(x - gmax) for x in gl])

    el = []
    for e in range(EXPERTS_PER_GROUP):
        v = row(N_GROUPS + e)
        for g in range(1, N_GROUPS):
            v = jnp.where(gsel == float(g), row(N_GROUPS + EXPERTS_PER_GROUP * g + e), v)
        el.append(v)
    emax = functools.reduce(jnp.maximum, el)
    sel = []
    for e in range(EXPERTS_PER_GROUP):
        ahead = jnp.zeros_like(emax)
        for o in range(EXPERTS_PER_GROUP):
            if o < e:
                ahead = ahead + jnp.where(el[o] >= el[e], 1.0, 0.0)
            elif o > e:
                ahead = ahead + jnp.where(el[o] > el[e], 1.0, 0.0)
        sel.append(ahead < 2.0)
    pe = [jnp.where(sel[e], jnp.exp(el[e] - emax), 0.0) for e in range(EXPERTS_PER_GROUP)]
    scale = g_w / functools.reduce(jnp.add, pe)
    zero = jnp.zeros_like(emax)
    seen = zero
    lo_idx, hi_idx, c_lo, c_hi = zero, zero, zero, zero
    for e in range(EXPERTS_PER_GROUP):
        is_lo = sel[e] & (seen == 0.0)
        is_hi = sel[e] & (seen == 1.0)
        lo_idx = jnp.where(is_lo, float(e), lo_idx)
        hi_idx = jnp.where(is_hi, float(e), hi_idx)
        c_lo = jnp.where(is_lo, pe[e] * scale, c_lo)
        c_hi = jnp.where(is_hi, pe[e] * scale, c_hi)
        seen = seen + jnp.where(sel[e], 1.0, 0.0)
    pair = lo_idx * (7.0 - lo_idx) * 0.5 + (hi_idx - lo_idx - 1.0)
    bucket = gsel * float(N_PAIRS) + pair

    bid = lax.broadcasted_iota(jnp.int32, (ROUTER_ROWS, bucket.shape[1]), 0).astype(F32)
    onehot = jnp.where(bid == bucket, 1.0, 0.0)
    before = _dot(onehot.astype(BF16), tri_ref[...]) + carry_ref[...]
    rank = jnp.sum(onehot * before, axis=0, keepdims=True)
    carry = carry_ref[...] + jnp.sum(onehot, axis=1, keepdims=True)
    carry_ref[...] = carry
    cnt_ref[...] = carry
    meta_ref[...] = jnp.concatenate([bucket, rank, c_lo, c_hi, zero, zero, zero, zero], axis=0)


def _outproj_router(a_list, w_out, x, n, norm_w, w_group, b_group, w_router, b_router):
    d = D_MODEL
    tm = ROW_TILE
    tok = lambda: pl.BlockSpec((tm * LANE_TILES, LANE), lambda i: (i, 0))
    wr =jnp.concatenate([w_group, w_router.transpose(1, 0, 2).reshape(d, N_EXPERTS)], axis=1)
    wr_t = jnp.zeros((ROUTER_ROWS, d), F32).at[:wr.shape[1]].set(wr.T)
    wr_hi = wr_t.astype(BF16)
    wr_lo = (wr_t - wr_hi.astype(F32)).astype(BF16)
    br = jnp.zeros((ROUTER_ROWS, 1), F32).at[:wr.shape[1], 0].set(
        jnp.concatenate([b_group, b_router.reshape(-1)]))
    tri = jnp.asarray(np.triu(np.ones((tm, tm), np.float32), 1), BF16)
    const = lambda shape: pl.BlockSpec(shape, lambda i: (0, 0))
    in_specs = [pl.BlockSpec((tm, a.shape[1]), lambda i: (i, 0)) for a in a_list]
    in_specs += [const((d, d)), _token_spec(x, tm), const((1, d)),
                 const((ROUTER_ROWS, d)), const((ROUTER_ROWS, d)), const((ROUTER_ROWS, 1)), const((tm, tm))]
    return pl.pallas_call(
        functools.partial(_outproj_router_kernel, n_a=len(a_list)),
        grid=(n // tm,),
        in_specs=in_specs,
        out_specs=[tok(),
                   pl.BlockSpec((8, tm), lambda i: (0, i)),
                   const((ROUTER_ROWS, 1))],
        out_shape=[jax.ShapeDtypeStruct((n * LANE_TILES, LANE), F32),
                   jax.ShapeDtypeStruct((8, n), F32),
                   jax.ShapeDtypeStruct((ROUTER_ROWS, 1), F32)],
        scratch_shapes=[pltpu.VMEM((ROUTER_ROWS, 1), F32)],
        compiler_params=_params("arbitrary"),
        name="outproj_router",
    )(*a_list, w_out, x, norm_w.reshape(1, d), wr_hi, wr_lo, br, tri)


def _gather_tokens(table, idx):
    m = idx.shape[0]
    info = plsc.get_sparse_core_info()
    n_workers = info.num_cores * info.num_subcores
    per_worker = m // n_workers
    n_chunks = per_worker // SC_CHUNK
    assert per_worker * n_workers == m and n_chunks * SC_CHUNK == per_worker and n_chunks % 2 == 0
    mesh = plsc.VectorSubcoreMesh(core_axis_name="core", subcore_axis_name="subcore")
    rows = pltpu.VMEM((SC_CHUNK,) + table.shape[1:], table.dtype)

    @functools.partial(
        pl.kernel, mesh=mesh,
        out_type=jax.ShapeDtypeStruct((m,) + table.shape[1:], table.dtype),
        scratch_types=[pltpu.VMEM((n_chunks, SC_CHUNK), jnp.int32), rows, rows,
                       pltpu.SemaphoreType.DMA, pltpu.SemaphoreType.DMA],
        name="gather_tokens")
    def gather(table_hbm, idx_hbm, out_hbm, idx_v, rows_a, rows_b, sem_a, sem_b):
        worker = lax.axis_index("subcore") * info.num_cores + lax.axis_index("core")
        bufs = ((rows_a, sem_a), (rows_b, sem_b))

        def fetch(j, buf):
            return pltpu.make_async_copy(table_hbm.at[idx_v.at[j]], bufs[buf][0], bufs[buf][1])

        pltpu.sync_copy(idx_hbm.at[worker], idx_v)
        fetch(0, 0).start()

        @pl.loop(0, n_chunks, step=2)
        def _(j0):
            for buf in range(2):
                j = j0 + buf
                fetch(j, buf).wait()

                @pl.when(j + 1 < n_chunks)
                def _():
                    fetch(j + 1, 1 - buf).start()

                off = pl.multiple_of(worker * per_worker + j * SC_CHUNK, SC_CHUNK)
                pltpu.sync_copy(bufs[buf][0], out_hbm.at[pl.ds(off, SC_CHUNK)])

    return gather(table, idx.reshape(n_workers, n_chunks, SC_CHUNK))


def _moe_kernel(tid, e_lo, e_hi, n_tiles, x_ref, cw_ref, nw_ref, g_lo_ref, g_hi_ref, u_lo_ref, u_hi_ref,
                d_lo_ref, d_hi_ref, o_ref):
    del tid, e_lo, e_hi

    @pl.when(pl.program_id(0) < n_tiles[0])
    def _():
        x = _load_tokens(x_ref)
        h = _rms(x, nw_ref[...]).astype(BF16)
        cw = cw_ref[...]

        def expert(g_ref, u_ref, d_ref):
            act = _silu(_dot(h, g_ref[0])) * _dot(h, u_ref[0])
            return _dot(act.astype(BF16), d_ref[0])

        y = (cw[:, 1:2] * expert(g_lo_ref, u_lo_ref, d_lo_ref)
             + cw[:, 2:3] * expert(g_hi_ref, u_hi_ref, d_hi_ref))
        _store_tokens(o_ref, _load_tokens(x_ref) + y)


def _moe(x2, n, meta, cnt, norm_w, w_gate, w_up, w_down, layer):
    d = D_MODEL
    lt = LANE_TILES
    tm = MOE_TILE
    max_tiles = n // tm + N_BUCKETS
    n_slots = max_tiles * tm
    counts = cnt[:N_BUCKETS, 0].astype(jnp.int32)
    tiles_per_bucket = (counts + tm - 1) // tm
    tile_end = jnp.cumsum(tiles_per_bucket)
    row_start = (tile_end - tiles_per_bucket) * tm
    n_tiles = tile_end[-1]
    in_bucket = meta[0][:, None] == jnp.arange(N_BUCKETS, dtype=F32)[None, :]
    dest = meta[1].astype(jnp.int32) + jnp.sum(jnp.where(in_bucket, row_start[None, :], 0), axis=1)
    packed = jnp.stack([jnp.arange(n, dtype=F32), meta[2], meta[3], jnp.ones((n,), F32)], axis=1)
    slots = jnp.zeros((n_slots, 4), F32).at[dest].set(packed, unique_indices=True, mode="promise_in_bounds")
    src = jnp.where(slots[:, 3] > 0.0, slots[:, 0].astype(jnp.int32), jnp.arange(n_slots, dtype=jnp.int32) % n)
    x_sorted = _gather_tokens(x2.reshape(n, lt, LANE), src).reshape(n_slots * lt, LANE)

    tile_ids = jnp.minimum(jnp.arange(max_tiles, dtype=jnp.int32), n_tiles - 1)
    tile_bucket = jnp.sum((tile_ids[:, None] >= tile_end[None, :]).astype(jnp.int32), axis=1)
    group = tile_bucket // N_PAIRS
    pair = tile_bucket % N_PAIRS
    pair_lo = jnp.sum(jnp.where(pair[:, None] == jnp.arange(N_PAIRS)[None, :], jnp.asarray(PAIR_LO, jnp.int32)[None, :], 0), axis=1)
    pair_hi = jnp.sum(jnp.where(pair[:, None] == jnp.arange(N_PAIRS)[None, :], jnp.asarray(PAIR_HI, jnp.int32)[None, :], 0), axis=1)
    e_lo = layer * N_EXPERTS + group * EXPERTS_PER_GROUP + pair_lo
    e_hi = layer * N_EXPERTS + group * EXPERTS_PER_GROUP + pair_hi

    grid_spec = pltpu.PrefetchScalarGridSpec(
        num_scalar_prefetch=4,
        grid=(max_tiles,),
        in_specs=[pl.BlockSpec((tm * lt, LANE), lambda t, tid, lo, hi, nt: (tid[t], 0)),
                  pl.BlockSpec((tm, 4), lambda t, tid, lo, hi, nt: (tid[t], 0)),
                  pl.BlockSpec((1, d), lambda t, tid, lo, hi, nt: (0, 0)),
                  pl.BlockSpec((1, d, D_EXPERT), lambda t, tid, lo, hi, nt: (lo[t], 0, 0)),
                  pl.BlockSpec((1, d, D_EXPERT), lambda t, tid, lo, hi, nt: (hi[t], 0, 0)),
                  pl.BlockSpec((1, d, D_EXPERT), lambda t, tid, lo, hi, nt: (lo[t], 0, 0)),
                  pl.BlockSpec((1, d, D_EXPERT), lambda t, tid, lo, hi, nt: (hi[t], 0, 0)),
                  pl.BlockSpec((1, D_EXPERT, d), lambda t, tid, lo, hi, nt: (lo[t], 0, 0)),
                  pl.BlockSpec((1, D_EXPERT, d), lambda t, tid, lo, hi, nt: (hi[t], 0, 0))],
        out_specs=pl.BlockSpec((tm * lt, LANE), lambda t, tid, lo, hi, nt: (tid[t], 0)))
    out_sorted = pl.pallas_call(
        _moe_kernel,
        grid_spec=grid_spec,
        out_shape=jax.ShapeDtypeStruct((n_slots * lt, LANE), F32),
        compiler_params=_params("arbitrary"),
        name="moe",
    )(tile_ids, e_lo, e_hi, n_tiles.reshape(1), x_sorted, slots, norm_w.reshape(1, d),
      w_gate, w_gate, w_up, w_up, w_down, w_down)
    return _gather_tokens(out_sorted.reshape(n_slots, lt, LANE), dest).reshape(n * lt, LANE)


def kernel(x, attn_norm, ffn_norm, even_w_in, even_w_out, hg_lb_logits, hg_o_norm, da_q_norm, da_k_norm, da_lambda, da_subln, odd_w_in, gla_w_alpha, gla_b_alpha, gla_o_norm, odd_w_out, moe_w_group, moe_b_group, moe_w_router, moe_b_router, moe_w_gate, moe_w_up, moe_w_down):
    batch, seq, d = x.shape
    depth = attn_norm.shape[0]
    n = batch * seq
    xf = x.reshape(n, d)

    lb_sm = jax.nn.softmax(hg_lb_logits.astype(F32), axis=0)
    lb_all = jnp.cumsum(lb_sm, axis=0) - lb_sm[0:1]
    rank = gla_w_alpha.shape[1]

    experts = lambda w: _to_bf16(w.reshape((-1,) + w.shape[2:]))
    even_w_in, even_w_out, odd_w_out = _to_bf16(even_w_in), _to_bf16(even_w_out), _to_bf16(odd_w_out)
    odd_w_in = _to_bf16(odd_w_in, odd_w_in.shape[2] + GLA_RANK_PAD - rank)
    moe_w_gate, moe_w_up, moe_w_down = experts(moe_w_gate), experts(moe_w_up), experts(moe_w_down)

    for layer in range(depth):
        j = layer // 2
        if layer % 2 == 0:
            lam_init = 0.8 - 0.6 * math.exp(-0.3 * layer)
            proj = _inproj(xf, n, attn_norm[layer], even_w_in[j])
            o_a = _hgrn(proj, lb_all[j], hg_o_norm[j], batch, seq)
            o_b = _diff_attn(proj, da_q_norm[j], da_k_norm[j], da_lambda[j], da_subln[j], lam_init, batch, seq)
            mix = [o_a, o_b]
            w_out = even_w_out[j]
        else:
            w_alpha = jnp.pad(gla_w_alpha[j], ((0, GLA_RANK_PAD - rank), (0, 0))).astype(BF16)
            proj = _inproj(xf, n, attn_norm[layer], odd_w_in[j])
            mix = [_gla(proj, w_alpha, gla_b_alpha[j], gla_o_norm[j], batch, seq)]
            w_out = odd_w_out[j]
        x2, meta, cnt = _outproj_router(mix, w_out, xf, n, ffn_norm[layer], moe_w_group[layer],
                                        moe_b_group[layer], moe_w_router[layer], moe_b_router[layer])
        xf = _moe(x2, n, meta, cnt, ffn_norm[layer], moe_w_gate, moe_w_up, moe_w_down, layer)
    return xf.reshape(batch, seq, d)
```

```python
import functools
import math

import numpy as np
import jax
import jax.numpy as jnp
from jax import lax
from jax.experimental import pallas as pl
from jax.experimental.pallas import tpu as pltpu
from jax.experimental.pallas import tpu_sc as plsc

F32 = jnp.float32
BF16 = jnp.bfloat16
EPS = 1e-6
LOG2_E = 1.4426950408889634

D_MODEL = 1024
LANE = 128
LANE_TILES = D_MODEL // LANE
HEADS = 4
DK = 128
HG_WIDTH = 512
DA_HEAD_DIM = 64
GLA_DV = 256
GLA_RANK_PAD = 128
GLA_TAU = 16.0
N_GROUPS = 4
EXPERTS_PER_GROUP = 4
N_EXPERTS = 16
D_EXPERT = 512
N_PAIRS = 6
N_BUCKETS = N_GROUPS * N_PAIRS
ROUTER_ROWS = 32

CHUNK = 128
GLA_STEP_CHUNKS = 4
LEVELS = 7
ATTN_BLOCK = 512
ATTN_HEADS = 2
SCORE_BOUND_MARGIN = 1.05
MAX_SCORE_BOUND = 60.0
ROW_TILE = 512
MOE_TILE = 256
SC_CHUNK = 32
VMEM_LIMIT = 56 * 1024 * 1024
CAST_BLOCK_BYTES = 8 * 1024 * 1024

PAIR_LO = (0, 0, 0, 1, 1, 2)
PAIR_HI = (1, 2, 3, 2, 3, 3)


def _dot(a, b):
    return jnp.dot(a, b, preferred_element_type=F32)


def _dot_nt(a, b):
    return lax.dot_general(a, b, (((1,), (1,)), ((), ())), preferred_element_type=F32)


def _dot_tn(a, b):
    return lax.dot_general(a, b, (((0,), (0,)), ((), ())), preferred_element_type=F32)


def _rms(x, w):
    return x * lax.rsqrt(jnp.mean(x * x, axis=-1, keepdims=True) + EPS) * w


def _silu(x):
    return x * jax.nn.sigmoid(x)


def _params(*sem):
    return pltpu.CompilerParams(dimension_semantics=sem, vmem_limit_bytes=VMEM_LIMIT)


def _load_tokens(ref):
    if ref.shape[1] == D_MODEL:
        return ref[...]
    rows = ref.shape[0] // LANE_TILES
    return jnp.concatenate([ref[pl.ds(c, rows, stride=LANE_TILES), :] for c in range(LANE_TILES)], axis=1)


def _token_spec(x, rows):
    shape = (rows, D_MODEL) if x.shape[1] == D_MODEL else (rows * LANE_TILES, LANE)
    return pl.BlockSpec(shape, lambda i: (i, 0))


def _store_tokens(ref, val):
    rows = ref.shape[0] // LANE_TILES
    for c in range(LANE_TILES):
        ref[pl.ds(c, rows, stride=LANE_TILES), :] = val[:, c * LANE:(c + 1) * LANE]


def _cast_kernel(w_ref, o_ref):
    n = w_ref.shape[-1]
    o_ref[:, :, :n] = w_ref[...].astype(BF16)
    if o_ref.shape[-1] > n:
        o_ref[:, :, n:] = jnp.zeros(o_ref.shape[:2] + (o_ref.shape[-1] - n,), BF16)


def _to_bf16(w, n_pad=None):
    l, k, n = w.shape
    n_pad = n_pad or n
    per_matrix = k * n * 4
    lb = max([c for c in range(1, l + 1) if l % c == 0 and c * per_matrix <= CAST_BLOCK_BYTES] or [1])
    kb = k
    while lb == 1 and kb * n * 4 > CAST_BLOCK_BYTES and kb % 32 == 0:
        kb //= 2
    return pl.pallas_call(
        _cast_kernel,
        grid=(l // lb, k // kb),
        in_specs=[pl.BlockSpec((lb, kb, n), lambda i, j: (i, j, 0))],
        out_specs=pl.BlockSpec((lb, kb, n_pad), lambda i, j: (i, j, 0)),
        out_shape=jax.ShapeDtypeStruct((l, k, n_pad), BF16),
        compiler_params=_params("parallel", "parallel"),
        name="to_bf16",
    )(w)


def _inproj_kernel(x_ref, nw_ref, w_ref, o_ref):
    h = _rms(_load_tokens(x_ref), nw_ref[...]).astype(BF16)
    o_ref[...] = _dot(h, w_ref[...])


def _inproj(x, n, norm_w, w):
    d = D_MODEL
    n_out = w.shape[1]
    return pl.pallas_call(
        _inproj_kernel,
        grid=(n // ROW_TILE,),
        in_specs=[_token_spec(x, ROW_TILE),
                  pl.BlockSpec((1, d), lambda i: (0, 0)),
                  pl.BlockSpec((d, n_out), lambda i: (0, 0))],
        out_specs=pl.BlockSpec((ROW_TILE, n_out), lambda i: (i, 0)),
        out_shape=jax.ShapeDtypeStruct((n, n_out), F32),
        compiler_params=_params("parallel"),
        name="inproj",
    )(x, norm_w.reshape(1, d), w)


def _decay_tables():
    c = CHUNK
    a = np.zeros((LEVELS + 2, c, c), np.float32)
    t = np.arange(c)
    for l in range(LEVELS):
        s = 1 << l
        for r in range(c):
            mid = (r // (2 * s)) * 2 * s + s - 1
            if (r >> l) & 1:
                a[l, r] = (t > mid) & (t <= r)
            else:
                a[l, r] = (t > r) & (t <= mid)
    a[LEVELS] = t[None, :] <= t[:, None]
    a[LEVELS + 1] = t[None, :] > t[:, None]
    a = a.reshape((LEVELS + 2) * c, c)
    a3 = np.concatenate([a, a], axis=1)
    i, j = np.meshgrid(t, t, indexing="ij")
    x = np.maximum(i ^ j, 1)
    lvl = np.where(j < i, np.floor(np.log2(x)).astype(np.int32), np.where(i == j, LEVELS, -1))
    return jnp.asarray(a3, BF16), jnp.asarray(lvl, jnp.int32)


def _decay_factors(a3, g):
    g_hi = g.astype(BF16)
    g_lo = (g - g_hi.astype(F32)).astype(BF16)
    return jnp.exp(_dot(a3, jnp.concatenate([g_hi, g_lo], axis=0)))


def _gla_chunk(q, k, v, f, st_ref, lvl, rowid):
    c = CHUNK
    sc = jnp.where(lvl == LEVELS, jnp.sum(q * k, axis=-1, keepdims=True), 0.0)
    for l in range(LEVELS):
        is_query = ((rowid >> l) & 1) == 1
        m = (jnp.where(is_query, q, k) * f[l * c:(l + 1) * c]).astype(BF16)
        sc = jnp.where(lvl == l, _dot_nt(m, m), sc)
    qd = (q * f[LEVELS * c:(LEVELS + 1) * c]).astype(BF16)
    kd = (k * f[(LEVELS + 1) * c:(LEVELS + 2) * c]).astype(BF16)
    vb = v.astype(BF16)
    st = st_ref[...]
    o = _dot(sc.astype(BF16), vb) + _dot_nt(qd, st.astype(BF16))
    chunk_decay = f[(LEVELS + 1) * c - 1:(LEVELS + 1) * c, :]
    st_ref[...] = st * chunk_decay + _dot_tn(vb, kd)
    return o


def _hgrn_kernel(q_ref, f_ref, i_ref, g_ref, lb_ref, on_ref, a3_ref, lvl_ref, o_ref, st_ref):
    @pl.when(pl.program_id(1) == 0)
    def _():
        st_ref[...] = jnp.zeros_like(st_ref)

    lvl = lvl_ref[...]
    rowid = lax.broadcasted_iota(jnp.int32, (CHUNK, 1), 0)
    lb = lb_ref[...]
    for c in range(GLA_STEP_CHUNKS):
        rows = slice(c * CHUNK, (c + 1) * CHUNK)
        fl = f_ref[rows, :]
        f = _decay_factors(a3_ref[...], jnp.log(lb + (1.0 - lb) * jax.nn.sigmoid(fl)))
        key_in = (1.0 - lb) * jax.nn.sigmoid(-fl)
        for h in range(HEADS):
            sl = slice(h * DK, (h + 1) * DK)
            q = _silu(q_ref[rows, sl]) * (DK ** -0.5)
            o = _gla_chunk(q, key_in[:, sl], i_ref[rows, sl], f[:, sl], st_ref.at[h], lvl, rowid)
            o = _rms(o, on_ref[...])
            o_ref[rows, sl] = (o * _silu(g_ref[rows, sl])).astype(BF16)


def _hgrn(proj, lb, o_norm, batch, seq):
    n = proj.shape[0]
    rows = CHUNK * GLA_STEP_CHUNKS
    nt = seq // rows
    a3, lvl = _decay_tables()
    col = lambda j: pl.BlockSpec((rows, HG_WIDTH), lambda b, t: (b * nt + t, j))
    const = lambda shape: pl.BlockSpec(shape, lambda b, t: (0, 0))
    return pl.pallas_call(
        _hgrn_kernel,
        grid=(batch, nt),
        in_specs=[col(0), col(1), col(2), col(3), const((1, HG_WIDTH)), const((1, DK)),
                  const(a3.shape), const(lvl.shape)],
        out_specs=pl.BlockSpec((rows, HG_WIDTH), lambda b, t: (b * nt + t, 0)),
        out_shape=jax.ShapeDtypeStruct((n, HG_WIDTH), BF16),
        scratch_shapes=[pltpu.VMEM((HEADS, DK, DK), F32)],
        compiler_params=_params("parallel", "arbitrary"),
        name="hgrn",
    )(proj, proj, proj, proj, lb.reshape(1, HG_WIDTH), o_norm.reshape(1, DK), a3, lvl)


def _log_sigmoid(x):
    return jnp.minimum(x, 0.0) - jnp.log1p(jnp.exp(-jnp.abs(x)))


def _gla_kernel(q_ref, k_ref, v_ref, g_ref, a_ref, wa_ref, ba_ref, on_ref, a3_ref, lvl_ref, o_ref, st_ref):
    @pl.when(pl.program_id(1) == 0)
    def _():
        st_ref[...] = jnp.zeros_like(st_ref)

    lvl = lvl_ref[...]
    rowid = lax.broadcasted_iota(jnp.int32, (CHUNK, 1), 0)
    for c in range(GLA_STEP_CHUNKS):
        rows = slice(c * CHUNK, (c + 1) * CHUNK)
        gate_logit = _dot(a_ref[rows, :].astype(BF16), wa_ref[...]) + ba_ref[...]
        f = _decay_factors(a3_ref[...], _log_sigmoid(gate_logit) * (1.0 / GLA_TAU))
        for h in range(HEADS):
            sl = slice(h * DK, (h + 1) * DK)
            sv = slice(h * GLA_DV, (h + 1) * GLA_DV)
            q = q_ref[rows, sl] * (DK ** -0.5)
            o = _gla_chunk(q, k_ref[rows, sl], v_ref[rows, sv], f[:, sl], st_ref.at[h], lvl, rowid)
            o = _rms(o, on_ref[...])
            o_ref[rows, sv] = (o * _silu(g_ref[rows, sv])).astype(BF16)


def _gla(proj, w_alpha, b_alpha, o_norm, batch, seq):
    n = proj.shape[0]
    rows = CHUNK * GLA_STEP_CHUNKS
    nt = seq // rows
    qk_w = HEADS * DK
    v_w = HEADS * GLA_DV
    a3, lvl = _decay_tables()
    row = lambda b, t: b * nt + t
    const = lambda shape: pl.BlockSpec(shape, lambda b, t: (0, 0))
    return pl.pallas_call(
        _gla_kernel,
        grid=(batch, nt),
        in_specs=[pl.BlockSpec((rows, qk_w), lambda b, t: (row(b, t), 0)),
                  pl.BlockSpec((rows, qk_w), lambda b, t: (row(b, t), 1)),
                  pl.BlockSpec((rows, v_w), lambda b, t: (row(b, t), 1)),
                  pl.BlockSpec((rows, v_w), lambda b, t: (row(b, t), 2)),
                  pl.BlockSpec((rows, GLA_RANK_PAD), lambda b, t: (row(b, t), (2 * qk_w + 2 * v_w) // GLA_RANK_PAD)),
                  const((GLA_RANK_PAD, qk_w)), const((1, qk_w)), const((1, GLA_DV)),
                  const(a3.shape), const(lvl.shape)],
        out_specs=pl.BlockSpec((rows, v_w), lambda b, t: (row(b, t), 0)),
        out_shape=jax.ShapeDtypeStruct((n, v_w), BF16),
        scratch_shapes=[pltpu.VMEM((HEADS, GLA_DV, DK), F32)],
        compiler_params=_params("parallel", "arbitrary"),
        name="gla",
    )(proj, proj, proj, proj, proj, w_alpha, b_alpha.reshape(1, qk_w), o_norm.reshape(1, GLA_DV), a3, lvl)


def _qk_norm(x, w2):
    lane = lax.broadcasted_iota(jnp.int32, x.shape, 1)
    first = lane < DA_HEAD_DIM
    sq = x * x
    ms1 = jnp.sum(jnp.where(first, sq, 0.0), axis=-1, keepdims=True) * (1.0 / DA_HEAD_DIM)
    ms2 = jnp.sum(jnp.where(first, 0.0, sq), axis=-1, keepdims=True) * (1.0 / DA_HEAD_DIM)
    r = jnp.where(first, lax.rsqrt(ms1 + EPS), lax.rsqrt(ms2 + EPS))
    return x * r * w2, first


def _score_bound(q_norm_w, k_norm_w):
    return (jnp.max(jnp.abs(q_norm_w)) * jnp.max(jnp.abs(k_norm_w))
            * (DA_HEAD_DIM * DA_HEAD_DIM ** -0.5 * LOG2_E * SCORE_BOUND_MARGIN))


def _attn_kernel(qi_tab, ki_tab, q_ref, k_ref, v_ref, qn_ref, kn_ref, lam_ref, sub_ref, o_ref,
                 qs_ref, m_ref, l_ref, acc_ref, *, lam_init, bounded):
    p = pl.program_id(2)
    qi = qi_tab[p]
    ki = ki_tab[p]
    tq = ATTN_BLOCK
    dv = 2 * DA_HEAD_DIM
    heads = [slice(h * dv, (h + 1) * dv) for h in range(ATTN_HEADS)]

    @pl.when(ki == 0)
    def _():
        for h, sl in enumerate(heads):
            qn, first = _qk_norm(q_ref[:, sl], qn_ref[...])
            qn = qn * (DA_HEAD_DIM ** -0.5 * LOG2_E)
            q1 = jnp.where(first, qn, 0.0).T.astype(BF16)
            q2 = jnp.where(first, 0.0, qn).T.astype(BF16)
            for copy in range(1 if bounded else 2):
                qs_ref[h, copy, :, :tq] = q1
                qs_ref[h, copy, :, tq:] = q2
            if bounded:
                bound = _score_bound(qn_ref[...], kn_ref[...])
                row = lax.broadcasted_iota(jnp.int32, (dv, 2 * tq), 0)
                qs_ref[h, 1] = jnp.where(row == 0, -bound, 0.0).astype(BF16)
        if not bounded:
            m_ref[...] = jnp.full_like(m_ref, -jnp.inf)
        l_ref[...] = jnp.zeros_like(l_ref)
        acc_ref[...] = jnp.zeros_like(acc_ref)

    def step(masked):
        tk = k_ref.shape[0]
        ones = jnp.ones((16, tk), BF16)
        unit = jnp.where(lax.broadcasted_iota(jnp.int32, (tk, dv), 1) == 0, 1.0, 0.0).astype(BF16)
        for h, sl in enumerate(heads):
            kn, _ = _qk_norm(k_ref[:, sl], kn_ref[...])
            kn = kn.astype(BF16)

            def mask(s):
                if not masked:
                    return s
                key = lax.broadcasted_iota(jnp.int32, s.shape, 0)
                qry = lax.broadcasted_iota(jnp.int32, s.shape, 1)
                qry = jnp.where(qry >= tq, qry - tq, qry)
                return jnp.where(key <= qry, s, -jnp.inf)

            if bounded:
                shifted = mask(_dot(jnp.concatenate([kn, unit], axis=1), qs_ref[h].reshape(2 * dv, 2 * tq)))
                alpha = 1.0
            else:
                m_prev = m_ref[h]
                m_new = jnp.maximum(m_prev, jnp.max(mask(_dot(kn, qs_ref[h, 0])), axis=0, keepdims=True))
                alpha = jnp.exp2(m_prev - m_new)
                shifted = mask(_dot(kn, qs_ref[h, 1])) - m_new
                m_ref[h] = m_new
            pr = jnp.exp2(shifted).astype(BF16) if bounded else jnp.exp2(shifted.astype(BF16))
            vt = jnp.concatenate([v_ref[:, sl].T.astype(BF16), ones], axis=0)
            pv = _dot(vt, pr)
            l_ref[h] = alpha * l_ref[h] + pv[dv:dv + 1]
            acc_ref[h] = alpha * acc_ref[h] + pv[:dv]

    @pl.when(ki < qi)
    def _():
        step(False)

    @pl.when(ki == qi)
    def _():
        step(True)
        lv = lam_ref[...]
        lam = (jnp.exp(jnp.sum(lv[0:1] * lv[1:2], axis=-1, keepdims=True))
               - jnp.exp(jnp.sum(lv[2:3] * lv[3:4], axis=-1, keepdims=True)) + lam_init)
        for h, sl in enumerate(heads):
            o1 = acc_ref[h, :, :tq] / l_ref[h, :, :tq]
            o2 = acc_ref[h, :, tq:] / l_ref[h, :, tq:]
            o = o1 - lam * o2
            o = o * lax.rsqrt(jnp.mean(o * o, axis=0, keepdims=True) + EPS) * sub_ref[...] * (1.0 - lam_init)
            o_ref[:, sl] = o.T.astype(BF16)


def _diff_attn(proj, q_norm, k_norm, lam_vecs, subln, lam_init, batch, seq):
    n = proj.shape[0]
    blk = ATTN_BLOCK
    nq = seq // blk
    dv = 2 * DA_HEAD_DIM
    pairs = [(qi, ki) for qi in range(nq) for ki in range(qi + 1)]
    qi_tab = jnp.asarray([p[0] for p in pairs], jnp.int32)
    ki_tab = jnp.asarray([p[1] for p in pairs], jnp.int32)
    width = ATTN_HEADS * dv
    groups = HEADS // ATTN_HEADS
    q_col = (4 * HG_WIDTH) // width
    const = lambda shape: pl.BlockSpec(shape, lambda b, g, p, qt, kt: (0, 0))
    grid_spec = pltpu.PrefetchScalarGridSpec(
        num_scalar_prefetch=2,
        grid=(batch, groups, len(pairs)),
        in_specs=[pl.BlockSpec((blk, width), lambda b, g, p, qt, kt: (b * nq + qt[p], q_col + g)),
                  pl.BlockSpec((blk, width), lambda b, g, p, qt, kt: (b * nq + kt[p], q_col + groups + g)),
                  pl.BlockSpec((blk, width), lambda b, g, p, qt, kt: (b * nq + kt[p], q_col + 2 * groups + g)),
                  const((1, dv)), const((1, dv)), const((4, DA_HEAD_DIM)), const((dv, 1))],
        out_specs=pl.BlockSpec((blk, width), lambda b, g, p, qt, kt: (b * nq + qt[p], g)),
        scratch_shapes=[pltpu.VMEM((ATTN_HEADS, 2, dv, 2 * blk), BF16),
                        pltpu.VMEM((ATTN_HEADS, 1, 2 * blk), F32),
                        pltpu.VMEM((ATTN_HEADS, 1, 2 * blk), F32),
                        pltpu.VMEM((ATTN_HEADS, dv, 2 * blk), F32)])
    args = (qi_tab, ki_tab, proj, proj, proj, jnp.tile(q_norm, 2).reshape(1, -1), jnp.tile(k_norm, 2).reshape(1, -1),
            lam_vecs, subln.reshape(-1, 1))

    def run(bounded):
        return pl.pallas_call(
            functools.partial(_attn_kernel, lam_init=lam_init, bounded=bounded),
            grid_spec=grid_spec,
            out_shape=jax.ShapeDtypeStruct((n, HEADS * dv), BF16),
            compiler_params=_params("parallel", "parallel", "arbitrary"),
            name="diff_attn_bounded" if bounded else "diff_attn",
        )(*args)

    return lax.cond(_score_bound(q_norm, k_norm) <= MAX_SCORE_BOUND, lambda: run(True), lambda: run(False))


def _outproj_router_kernel(*refs, n_a):
    a_refs = refs[:n_a]
    (w_ref, x_ref, nw_ref, wr_hi_ref, wr_lo_ref, br_ref, tri_ref,
     x2_ref, meta_ref, rec_ref, cnt_ref, carry_ref) = refs[n_a:]

    @pl.when(pl.program_id(0) == 0)
    def _():
        carry_ref[...] = jnp.zeros_like(carry_ref)

    x2 = _load_tokens(x_ref)
    off = 0
    for a_ref in a_refs:
        kk = a_ref.shape[1]
        x2 = x2 + _dot(a_ref[...], w_ref[off:off + kk, :])
        off += kk
    _store_tokens(x2_ref, x2)

    h = _rms(x2, nw_ref[...])
    h_hi = h.astype(BF16)
    h_lo = (h - h_hi.astype(F32)).astype(BF16)
    wr_hi = wr_hi_ref[...]
    by_hi = _dot_nt(jnp.concatenate([wr_hi, wr_lo_ref[...]], axis=0), h_hi)
    lt = by_hi[:ROUTER_ROWS] + by_hi[ROUTER_ROWS:] + _dot_nt(wr_hi, h_lo) + br_ref[...]
    row = lambda r: lt[r:r + 1, :]

    gl = [row(g) for g in range(N_GROUPS)]
    gmax = functools.reduce(jnp.maximum, gl)
    gsel = jnp.where(gl[0] == gmax, 0.0, jnp.where(gl[1] == gmax, 1.0, jnp.where(gl[2] == gmax, 2.0, 3.0)))
    g_w = 1.0 / functools.reduce(jnp.add, [jnp.exp(x - gmax) for x in gl])

    el = []
    for e in range(EXPERTS_PER_GROUP):
        v = row(N_GROUPS + e)
        for g in range(1, N_GROUPS):
            v = jnp.where(gsel == float(g), row(N_GROUPS + EXPERTS_PER_GROUP * g + e), v)
        el.append(v)
    emax = functools.reduce(jnp.maximum, el)
    sel = []
    for e in range(EXPERTS_PER_GROUP):
        ahead = jnp.zeros_like(emax)
        for o in range(EXPERTS_PER_GROUP):
            if o < e:
                ahead = ahead + jnp.where(el[o] >= el[e], 1.0, 0.0)
            elif o > e:
                ahead = ahead + jnp.where(el[o] > el[e], 1.0, 0.0)
        sel.append(ahead < 2.0)
    pe = [jnp.where(sel[e], jnp.exp(el[e] - emax), 0.0) for e in range(EXPERTS_PER_GROUP)]
    scale = g_w / functools.reduce(jnp.add, pe)
    zero = jnp.zeros_like(emax)
    seen = zero
    lo_idx, hi_idx, c_lo, c_hi = zero, zero, zero, zero
    for e in range(EXPERTS_PER_GROUP):
        is_lo = sel[e] & (seen == 0.0)
        is_hi = sel[e] & (seen == 1.0)
        lo_idx = jnp.where(is_lo, float(e), lo_idx)
        hi_idx = jnp.where(is_hi, float(e), hi_idx)
        c_lo = jnp.where(is_lo, pe[e] * scale, c_lo)
        c_hi = jnp.where(is_hi, pe[e] * scale, c_hi)
        seen = seen + jnp.where(sel[e], 1.0, 0.0)
    pair = lo_idx * (7.0 - lo_idx) * 0.5 + (hi_idx - lo_idx - 1.0)
    bucket = gsel * float(N_PAIRS) + pair

    bid = lax.broadcasted_iota(jnp.int32, (ROUTER_ROWS, bucket.shape[1]), 0).astype(F32)
    onehot = jnp.where(bid == bucket, 1.0, 0.0)
    before = _dot(onehot.astype(BF16), tri_ref[...]) + carry_ref[...]
    rank = jnp.sum(onehot * before, axis=0, keepdims=True)
    carry = carry_ref[...] + jnp.sum(onehot, axis=1, keepdims=True)
    carry_ref[...] = carry
    cnt_ref[...] = carry
    meta = jnp.concatenate([bucket, rank, c_lo, c_hi, zero, zero, zero, zero], axis=0)
    meta_ref[...] = meta
    rec_ref[...] = jnp.concatenate([meta, jnp.zeros((LANE - meta.shape[0], meta.shape[1]), F32)], axis=0).T


def _outproj_router(a_list, w_out, x, n, norm_w, w_group, b_group, w_router, b_router):
    d = D_MODEL
    tm = ROW_TILE
    tok = lambda: pl.BlockSpec((tm * LANE_TILES, LANE), lambda i: (i, 0))
    wr =jnp.concatenate([w_group, w_router.transpose(1, 0, 2).reshape(d, N_EXPERTS)], axis=1)
    wr_t = jnp.zeros((ROUTER_ROWS, d), F32).at[:wr.shape[1]].set(wr.T)
    wr_hi = wr_t.astype(BF16)
    wr_lo = (wr_t - wr_hi.astype(F32)).astype(BF16)
    br = jnp.zeros((ROUTER_ROWS, 1), F32).at[:wr.shape[1], 0].set(
        jnp.concatenate([b_group, b_router.reshape(-1)]))
    tri = jnp.asarray(np.triu(np.ones((tm, tm), np.float32), 1), BF16)
    const = lambda shape: pl.BlockSpec(shape, lambda i: (0, 0))
    in_specs = [pl.BlockSpec((tm, a.shape[1]), lambda i: (i, 0)) for a in a_list]
    in_specs += [const((d, d)), _token_spec(x, tm), const((1, d)),
                 const((ROUTER_ROWS, d)), const((ROUTER_ROWS, d)), const((ROUTER_ROWS, 1)), const((tm, tm))]
    return pl.pallas_call(
        functools.partial(_outproj_router_kernel, n_a=len(a_list)),
        grid=(n // tm,),
        in_specs=in_specs,
        out_specs=[tok(),
                   pl.BlockSpec((8, tm), lambda i: (0, i)),
                   pl.BlockSpec((tm, LANE), lambda i: (i, 0)),
                   const((ROUTER_ROWS, 1))],
        out_shape=[jax.ShapeDtypeStruct((n * LANE_TILES, LANE), F32),
                   jax.ShapeDtypeStruct((8, n), F32),
                   jax.ShapeDtypeStruct((n, LANE), F32),
                   jax.ShapeDtypeStruct((ROUTER_ROWS, 1), F32)],
        scratch_shapes=[pltpu.VMEM((ROUTER_ROWS, 1), F32)],
        compiler_params=_params("arbitrary"),
        name="outproj_router",
    )(*a_list, w_out, x, norm_w.reshape(1, d), wr_hi, wr_lo, br, tri)


def _gather_tokens(table, idx):
    m = idx.shape[0]
    info = plsc.get_sparse_core_info()
    n_workers = info.num_cores * info.num_subcores
    per_worker = m // n_workers
    n_chunks = per_worker // SC_CHUNK
    assert per_worker * n_workers == m and n_chunks * SC_CHUNK == per_worker and n_chunks % 2 == 0
    mesh = plsc.VectorSubcoreMesh(core_axis_name="core", subcore_axis_name="subcore")
    rows = pltpu.VMEM((SC_CHUNK,) + table.shape[1:], table.dtype)

    @functools.partial(
        pl.kernel, mesh=mesh,
        out_type=jax.ShapeDtypeStruct((m,) + table.shape[1:], table.dtype),
        scratch_types=[pltpu.VMEM((n_chunks, SC_CHUNK), jnp.int32), rows, rows,
                       pltpu.SemaphoreType.DMA, pltpu.SemaphoreType.DMA],
        name="gather_tokens")
    def gather(table_hbm, idx_hbm, out_hbm, idx_v, rows_a, rows_b, sem_a, sem_b):
        worker = lax.axis_index("subcore") * info.num_cores + lax.axis_index("core")
        bufs = ((rows_a, sem_a), (rows_b, sem_b))

        def fetch(j, buf):
            return pltpu.make_async_copy(table_hbm.at[idx_v.at[j]], bufs[buf][0], bufs[buf][1])

        pltpu.sync_copy(idx_hbm.at[worker], idx_v)
        fetch(0, 0).start()

        @pl.loop(0, n_chunks, step=2)
        def _(j0):
            for buf in range(2):
                j = j0 + buf
                fetch(j, buf).wait()

                @pl.when(j + 1 < n_chunks)
                def _():
                    fetch(j + 1, 1 - buf).start()

                off = pl.multiple_of(worker * per_worker + j * SC_CHUNK, SC_CHUNK)
                pltpu.sync_copy(bufs[buf][0], out_hbm.at[pl.ds(off, SC_CHUNK)])

    return gather(table, idx.reshape(n_workers, n_chunks, SC_CHUNK))


def _scatter_tokens(rows, idx, m):
    n = idx.shape[0]
    info = plsc.get_sparse_core_info()
    n_workers = info.num_cores * info.num_subcores
    per_worker = n // n_workers
    n_chunks = per_worker // SC_CHUNK
    assert per_worker * n_workers == n and n_chunks * SC_CHUNK == per_worker and n_chunks % 2 == 0
    mesh = plsc.VectorSubcoreMesh(core_axis_name="core", subcore_axis_name="subcore")
    buf = pltpu.VMEM((SC_CHUNK,) + rows.shape[1:], rows.dtype)

    @functools.partial(
        pl.kernel, mesh=mesh,
        out_type=jax.ShapeDtypeStruct((m,) + rows.shape[1:], rows.dtype),
        scratch_types=[pltpu.VMEM((n_chunks, SC_CHUNK), jnp.int32), buf, buf,
                       pltpu.SemaphoreType.DMA, pltpu.SemaphoreType.DMA],
        name="scatter_tokens")
    def scatter(rows_hbm, idx_hbm, out_hbm, idx_v, buf_a, buf_b, sem_a, sem_b):
        worker = lax.axis_index("subcore") * info.num_cores + lax.axis_index("core")
        bufs = ((buf_a, sem_a), (buf_b, sem_b))

        def load(j, b):
            off = pl.multiple_of(worker * per_worker + j * SC_CHUNK, SC_CHUNK)
            return pltpu.make_async_copy(rows_hbm.at[pl.ds(off, SC_CHUNK)], bufs[b][0], bufs[b][1])

        pltpu.sync_copy(idx_hbm.at[worker], idx_v)
        load(0, 0).start()

        @pl.loop(0, n_chunks, step=2)
        def _(j0):
            for b in range(2):
                j = j0 + b
                load(j, b).wait()

                @pl.when(j + 1 < n_chunks)
                def _():
                    load(j + 1, 1 - b).start()

                pltpu.sync_copy(bufs[b][0], out_hbm.at[idx_v.at[j]])

    return scatter(rows, idx.reshape(n_workers, n_chunks, SC_CHUNK))


def _moe_kernel(tid, e_lo, e_hi, n_valid, n_tiles, x_ref, rec_ref, nw_ref, g_lo_ref, g_hi_ref, u_lo_ref, u_hi_ref,
                d_lo_ref, d_hi_ref, o_ref):
    del tid, e_lo, e_hi
    t = pl.program_id(0)

    @pl.when(t < n_tiles[0])
    def _():
        valid = lax.broadcasted_iota(jnp.int32, (MOE_TILE, 1), 0) < n_valid[t]
        x = jnp.where(valid, _load_tokens(x_ref), 0.0)
        h = _rms(x, nw_ref[...]).astype(BF16)
        rec = jnp.where(valid, rec_ref[...], 0.0)

        def expert(g_ref, u_ref, d_ref):
            act = _silu(_dot(h, g_ref[0])) * _dot(h, u_ref[0])
            return _dot(act.astype(BF16), d_ref[0])

        y = (rec[:, 2:3] * expert(g_lo_ref, u_lo_ref, d_lo_ref)
             + rec[:, 3:4] * expert(g_hi_ref, u_hi_ref, d_hi_ref))
        _store_tokens(o_ref, jnp.where(valid, _load_tokens(x_ref), 0.0) + y)


def _moe(x2, n, meta, rec, cnt, norm_w, w_gate, w_up, w_down, layer):
    d = D_MODEL
    lt = LANE_TILES
    tm = MOE_TILE
    max_tiles = n // tm + N_BUCKETS
    n_slots = max_tiles * tm
    counts = cnt[:N_BUCKETS, 0].astype(jnp.int32)
    tiles_per_bucket = (counts + tm - 1) // tm
    tile_end = jnp.cumsum(tiles_per_bucket)
    tile_start = tile_end - tiles_per_bucket
    n_tiles = tile_end[-1]
    in_bucket = meta[0][:, None] == jnp.arange(N_BUCKETS, dtype=F32)[None, :]
    dest = meta[1].astype(jnp.int32) + jnp.sum(jnp.where(in_bucket, tile_start[None, :] * tm, 0), axis=1)
    x_sorted = _scatter_tokens(x2.reshape(n, lt, LANE), dest, n_slots).reshape(n_slots * lt, LANE)
    rec_sorted = _scatter_tokens(rec, dest, n_slots)

    tile_ids = jnp.minimum(jnp.arange(max_tiles, dtype=jnp.int32), n_tiles - 1)
    of_bucket = lambda table: jnp.sum(jnp.where(tile_bucket[:, None] == jnp.arange(N_BUCKETS)[None, :], table[None, :], 0), axis=1)
    tile_bucket = jnp.sum((tile_ids[:, None] >= tile_end[None, :]).astype(jnp.int32), axis=1)
    n_valid = jnp.clip(of_bucket(counts) - (tile_ids - of_bucket(tile_start)) * tm, 0, tm)
    group = tile_bucket // N_PAIRS
    pair = tile_bucket % N_PAIRS
    pair_lo = jnp.sum(jnp.where(pair[:, None] == jnp.arange(N_PAIRS)[None, :], jnp.asarray(PAIR_LO, jnp.int32)[None, :], 0), axis=1)
    pair_hi = jnp.sum(jnp.where(pair[:, None] == jnp.arange(N_PAIRS)[None, :], jnp.asarray(PAIR_HI, jnp.int32)[None, :], 0), axis=1)
    e_lo = layer * N_EXPERTS + group * EXPERTS_PER_GROUP + pair_lo
    e_hi = layer * N_EXPERTS + group * EXPERTS_PER_GROUP + pair_hi

    grid_spec = pltpu.PrefetchScalarGridSpec(
        num_scalar_prefetch=5,
        grid=(max_tiles,),
        in_specs=[pl.BlockSpec((tm * lt, LANE), lambda t, tid, lo, hi, nv, nt: (tid[t], 0)),
                  pl.BlockSpec((tm, LANE), lambda t, tid, lo, hi, nv, nt: (tid[t], 0)),
                  pl.BlockSpec((1, d), lambda t, tid, lo, hi, nv, nt: (0, 0)),
                  pl.BlockSpec((1, d, D_EXPERT), lambda t, tid, lo, hi, nv, nt: (lo[t], 0, 0)),
                  pl.BlockSpec((1, d, D_EXPERT), lambda t, tid, lo, hi, nv, nt: (hi[t], 0, 0)),
                  pl.BlockSpec((1, d, D_EXPERT), lambda t, tid, lo, hi, nv, nt: (lo[t], 0, 0)),
                  pl.BlockSpec((1, d, D_EXPERT), lambda t, tid, lo, hi, nv, nt: (hi[t], 0, 0)),
                  pl.BlockSpec((1, D_EXPERT, d), lambda t, tid, lo, hi, nv, nt: (lo[t], 0, 0)),
                  pl.BlockSpec((1, D_EXPERT, d), lambda t, tid, lo, hi, nv, nt: (hi[t], 0, 0))],
        out_specs=pl.BlockSpec((tm * lt, LANE), lambda t, tid, lo, hi, nv, nt: (tid[t], 0)))
    out_sorted = pl.pallas_call(
        _moe_kernel,
        grid_spec=grid_spec,
        out_shape=jax.ShapeDtypeStruct((n_slots * lt, LANE), F32),
        compiler_params=_params("arbitrary"),
        name="moe",
    )(tile_ids, e_lo, e_hi, n_valid, n_tiles.reshape(1), x_sorted, rec_sorted, norm_w.reshape(1, d),
      w_gate, w_gate, w_up, w_up, w_down, w_down)
    return _gather_tokens(out_sorted.reshape(n_slots, lt, LANE), dest).reshape(n * lt, LANE)


def kernel(x, attn_norm, ffn_norm, even_w_in, even_w_out, hg_lb_logits, hg_o_norm, da_q_norm, da_k_norm, da_lambda, da_subln, odd_w_in, gla_w_alpha, gla_b_alpha, gla_o_norm, odd_w_out, moe_w_group, moe_b_group, moe_w_router, moe_b_router, moe_w_gate, moe_w_up, moe_w_down):
    batch, seq, d = x.shape
    depth = attn_norm.shape[0]
    n = batch * seq
    xf = x.reshape(n, d)

    lb_sm = jax.nn.softmax(hg_lb_logits.astype(F32), axis=0)
    lb_all = jnp.cumsum(lb_sm, axis=0) - lb_sm[0:1]
    rank = gla_w_alpha.shape[1]

    experts = lambda w: _to_bf16(w.reshape((-1,) + w.shape[2:]))
    even_w_in, even_w_out, odd_w_out = _to_bf16(even_w_in), _to_bf16(even_w_out), _to_bf16(odd_w_out)
    odd_w_in = _to_bf16(odd_w_in, odd_w_in.shape[2] + GLA_RANK_PAD - rank)
    moe_w_gate, moe_w_up, moe_w_down = experts(moe_w_gate), experts(moe_w_up), experts(moe_w_down)

    for layer in range(depth):
        j = layer // 2
        if layer % 2 == 0:
            lam_init = 0.8 - 0.6 * math.exp(-0.3 * layer)
            proj = _inproj(xf, n, attn_norm[layer], even_w_in[j])
            o_a = _hgrn(proj, lb_all[j], hg_o_norm[j], batch, seq)
            o_b = _diff_attn(proj, da_q_norm[j], da_k_norm[j], da_lambda[j], da_subln[j], lam_init, batch, seq)
            mix = [o_a, o_b]
            w_out = even_w_out[j]
        else:
            w_alpha = jnp.pad(gla_w_alpha[j], ((0, GLA_RANK_PAD - rank), (0, 0))).astype(BF16)
            proj = _inproj(xf, n, attn_norm[layer], odd_w_in[j])
            mix = [_gla(proj, w_alpha, gla_b_alpha[j], gla_o_norm[j], batch, seq)]
            w_out = odd_w_out[j]
        x2, meta, rec, cnt = _outproj_router(mix, w_out, xf, n, ffn_norm[layer], moe_w_group[layer],
                                        moe_b_group[layer], moe_w_router[layer], moe_b_router[layer])
        xf = _moe(x2, n, meta, rec, cnt, ffn_norm[layer], moe_w_gate, moe_w_up, moe_w_down, layer)
    return xf.reshape(batch, seq, d)
```

```python
import functools
import math

import numpy as np
import jax
import jax.numpy as jnp
from jax import lax
from jax.experimental import pallas as pl
from jax.experimental.pallas import tpu as pltpu
from jax.experimental.pallas import tpu_sc as plsc

F32 = jnp.float32
BF16 = jnp.bfloat16
EPS = 1e-6
LOG2_E = 1.4426950408889634

D_MODEL = 1024
LANE = 128
LANE_TILES = D_MODEL // LANE
HEADS = 4
DK = 128
HG_WIDTH = 512
DA_HEAD_DIM = 64
GLA_DV = 256
GLA_RANK_PAD = 128
GLA_TAU = 16.0
N_GROUPS = 4
EXPERTS_PER_GROUP = 4
N_EXPERTS = 16
D_EXPERT = 512
N_PAIRS = 6
N_BUCKETS = N_GROUPS * N_PAIRS
ROUTER_ROWS = 32

CHUNK = 128
GLA_STEP_CHUNKS = 4
LEVELS = 7
ATTN_BLOCK = 512
ATTN_HEADS = 2
SCORE_BOUND_MARGIN = 1.05
MAX_SCORE_BOUND = 60.0
ROW_TILE = 512
MOE_TILE = 256
SC_CHUNK = 32
VMEM_LIMIT = 56 * 1024 * 1024
CAST_BLOCK_BYTES = 8 * 1024 * 1024

PAIR_LO = (0, 0, 0, 1, 1, 2)
PAIR_HI = (1, 2, 3, 2, 3, 3)


def _dot(a, b):
    return jnp.dot(a, b, preferred_element_type=F32)


def _dot_nt(a, b):
    return lax.dot_general(a, b, (((1,), (1,)), ((), ())), preferred_element_type=F32)


def _dot_tn(a, b):
    return lax.dot_general(a, b, (((0,), (0,)), ((), ())), preferred_element_type=F32)


def _rms(x, w):
    return x * lax.rsqrt(jnp.mean(x * x, axis=-1, keepdims=True) + EPS) * w


def _silu(x):
    return x * jax.nn.sigmoid(x)


def _params(*sem):
    return pltpu.CompilerParams(dimension_semantics=sem, vmem_limit_bytes=VMEM_LIMIT)


def _load_tokens(ref):
    if ref.shape[1] == D_MODEL:
        return ref[...]
    rows = ref.shape[0] // LANE_TILES
    return jnp.concatenate([ref[pl.ds(c, rows, stride=LANE_TILES), :] for c in range(LANE_TILES)], axis=1)


def _token_spec(x, rows):
    shape = (rows, D_MODEL) if x.shape[1] == D_MODEL else (rows * LANE_TILES, LANE)
    return pl.BlockSpec(shape, lambda i: (i, 0))


def _store_tokens(ref, val):
    rows = ref.shape[0] // LANE_TILES
    for c in range(LANE_TILES):
        ref[pl.ds(c, rows, stride=LANE_TILES), :] = val[:, c * LANE:(c + 1) * LANE]


def _cast_kernel(w_ref, o_ref):
    k = w_ref.shape[1]
    o_ref[:, :k, :] = w_ref[...].astype(BF16)
    if o_ref.shape[1] > k:
        o_ref[:, k:, :] = jnp.zeros((o_ref.shape[0], o_ref.shape[1] - k, o_ref.shape[2]), BF16)


def _to_bf16(w, k_pad=None):
    l, k, n = w.shape
    k_pad = k_pad or k
    per_matrix = k * n * 4
    lb = max([c for c in range(1, l + 1) if l % c == 0 and c * per_matrix <= CAST_BLOCK_BYTES] or [1])
    nb = n
    while lb == 1 and k * nb * 4 > CAST_BLOCK_BYTES and nb % (2 * LANE) == 0:
        nb //= 2
    return pl.pallas_call(
        _cast_kernel,
        grid=(l // lb, n // nb),
        in_specs=[pl.BlockSpec((lb, k, nb), lambda i, j: (i, 0, j))],
        out_specs=pl.BlockSpec((lb, k_pad, nb), lambda i, j: (i, 0, j)),
        out_shape=jax.ShapeDtypeStruct((l, k_pad, n), BF16),
        compiler_params=_params("parallel", "parallel"),
        name="to_bf16",
    )(w)


def _inproj_kernel(x_ref, nw_ref, w_ref, o_ref, *, w_transposed):
    h = _rms(_load_tokens(x_ref), nw_ref[...]).astype(BF16)
    o_ref[...] = _dot_nt(h, w_ref[...]) if w_transposed else _dot(h, w_ref[...])


def _inproj(x, n, norm_w, w, w_transposed=False):
    d = D_MODEL
    n_out = w.shape[0] if w_transposed else w.shape[1]
    return pl.pallas_call(
        functools.partial(_inproj_kernel, w_transposed=w_transposed),
        grid=(n // ROW_TILE,),
        in_specs=[_token_spec(x, ROW_TILE),
                  pl.BlockSpec((1, d), lambda i: (0, 0)),
                  pl.BlockSpec(w.shape, lambda i: (0, 0))],
        out_specs=pl.BlockSpec((ROW_TILE, n_out), lambda i: (i, 0)),
        out_shape=jax.ShapeDtypeStruct((n, n_out), F32),
        compiler_params=_params("parallel"),
        name="inproj",
    )(x, norm_w.reshape(1, d), w)


def _decay_tables():
    c = CHUNK
    a = np.zeros((LEVELS + 2, c, c), np.float32)
    t = np.arange(c)
    for l in range(LEVELS):
        s = 1 << l
        for r in range(c):
            mid = (r // (2 * s)) * 2 * s + s - 1
            if (r >> l) & 1:
                a[l, r] = (t > mid) & (t <= r)
            else:
                a[l, r] = (t > r) & (t <= mid)
    a[LEVELS] = t[None, :] <= t[:, None]
    a[LEVELS + 1] = t[None, :] > t[:, None]
    a = a.reshape((LEVELS + 2) * c, c)
    a3 = np.concatenate([a, a], axis=1)
    i, j = np.meshgrid(t, t, indexing="ij")
    x = np.maximum(i ^ j, 1)
    lvl = np.where(j < i, np.floor(np.log2(x)).astype(np.int32), np.where(i == j, LEVELS, -1))
    return jnp.asarray(a3, BF16), jnp.asarray(lvl, jnp.int32)


def _decay_factors(a3, g):
    g_hi = g.astype(BF16)
    g_lo = (g - g_hi.astype(F32)).astype(BF16)
    return jnp.exp(_dot(a3, jnp.concatenate([g_hi, g_lo], axis=0)))


def _gla_chunk(q, k, v, f, st_ref, lvl, rowid):
    c = CHUNK
    sc = jnp.where(lvl == LEVELS, jnp.sum(q * k, axis=-1, keepdims=True), 0.0)
    for l in range(LEVELS):
        is_query = ((rowid >> l) & 1) == 1
        m = (jnp.where(is_query, q, k) * f[l * c:(l + 1) * c]).astype(BF16)
        sc = jnp.where(lvl == l, _dot_nt(m, m), sc)
    qd = (q * f[LEVELS * c:(LEVELS + 1) * c]).astype(BF16)
    kd = (k * f[(LEVELS + 1) * c:(LEVELS + 2) * c]).astype(BF16)
    vb = v.astype(BF16)
    st = st_ref[...]
    o = _dot(sc.astype(BF16), vb) + _dot_nt(qd, st.astype(BF16))
    chunk_decay = f[(LEVELS + 1) * c - 1:(LEVELS + 1) * c, :]
    st_ref[...] = st * chunk_decay + _dot_tn(vb, kd)
    return o


def _hgrn_kernel(q_ref, f_ref, i_ref, g_ref, lb_ref, on_ref, a3_ref, lvl_ref, o_ref, st_ref):
    @pl.when(pl.program_id(1) == 0)
    def _():
        st_ref[...] = jnp.zeros_like(st_ref)

    lvl = lvl_ref[...]
    rowid = lax.broadcasted_iota(jnp.int32, (CHUNK, 1), 0)
    lb = lb_ref[...]
    for c in range(GLA_STEP_CHUNKS):
        rows = slice(c * CHUNK, (c + 1) * CHUNK)
        fl = f_ref[rows, :]
        f = _decay_factors(a3_ref[...], jnp.log(lb + (1.0 - lb) * jax.nn.sigmoid(fl)))
        key_in = (1.0 - lb) * jax.nn.sigmoid(-fl)
        for h in range(HEADS):
            sl = slice(h * DK, (h + 1) * DK)
            q = _silu(q_ref[rows, sl]) * (DK ** -0.5)
            o = _gla_chunk(q, key_in[:, sl], i_ref[rows, sl], f[:, sl], st_ref.at[h], lvl, rowid)
            o = _rms(o, on_ref[...])
            o_ref[rows, sl] = (o * _silu(g_ref[rows, sl])).astype(BF16)


def _hgrn(proj, lb, o_norm, batch, seq):
    n = proj.shape[0]
    rows = CHUNK * GLA_STEP_CHUNKS
    nt = seq // rows
    a3, lvl = _decay_tables()
    col = lambda j: pl.BlockSpec((rows, HG_WIDTH), lambda b, t: (b * nt + t, j))
    const = lambda shape: pl.BlockSpec(shape, lambda b, t: (0, 0))
    return pl.pallas_call(
        _hgrn_kernel,
        grid=(batch, nt),
        in_specs=[col(0), col(1), col(2), col(3), const((1, HG_WIDTH)), const((1, DK)),
                  const(a3.shape), const(lvl.shape)],
        out_specs=pl.BlockSpec((rows, HG_WIDTH), lambda b, t: (b * nt + t, 0)),
        out_shape=jax.ShapeDtypeStruct((n, HG_WIDTH), BF16),
        scratch_shapes=[pltpu.VMEM((HEADS, DK, DK), F32)],
        compiler_params=_params("parallel", "arbitrary"),
        name="hgrn",
    )(proj, proj, proj, proj, lb.reshape(1, HG_WIDTH), o_norm.reshape(1, DK), a3, lvl)


def _log_sigmoid(x):
    return jnp.minimum(x, 0.0) - jnp.log1p(jnp.exp(-jnp.abs(x)))


def _gla_kernel(q_ref, k_ref, v_ref, g_ref, a_ref, wa_ref, ba_ref, on_ref, a3_ref, lvl_ref, o_ref, st_ref):
    @pl.when(pl.program_id(1) == 0)
    def _():
        st_ref[...] = jnp.zeros_like(st_ref)

    lvl = lvl_ref[...]
    rowid = lax.broadcasted_iota(jnp.int32, (CHUNK, 1), 0)
    for c in range(GLA_STEP_CHUNKS):
        rows = slice(c * CHUNK, (c + 1) * CHUNK)
        gate_logit = _dot(a_ref[rows, :].astype(BF16), wa_ref[...]) + ba_ref[...]
        f = _decay_factors(a3_ref[...], _log_sigmoid(gate_logit) * (1.0 / GLA_TAU))
        for h in range(HEADS):
            sl = slice(h * DK, (h + 1) * DK)
            sv = slice(h * GLA_DV, (h + 1) * GLA_DV)
            q = q_ref[rows, sl] * (DK ** -0.5)
            o = _gla_chunk(q, k_ref[rows, sl], v_ref[rows, sv], f[:, sl], st_ref.at[h], lvl, rowid)
            o = _rms(o, on_ref[...])
            o_ref[rows, sv] = (o * _silu(g_ref[rows, sv])).astype(BF16)


def _gla(proj, w_alpha, b_alpha, o_norm, batch, seq):
    n = proj.shape[0]
    rows = CHUNK * GLA_STEP_CHUNKS
    nt = seq // rows
    qk_w = HEADS * DK
    v_w = HEADS * GLA_DV
    a3, lvl = _decay_tables()
    row = lambda b, t: b * nt + t
    const = lambda shape: pl.BlockSpec(shape, lambda b, t: (0, 0))
    return pl.pallas_call(
        _gla_kernel,
        grid=(batch, nt),
        in_specs=[pl.BlockSpec((rows, qk_w), lambda b, t: (row(b, t), 0)),
                  pl.BlockSpec((rows, qk_w), lambda b, t: (row(b, t), 1)),
                  pl.BlockSpec((rows, v_w), lambda b, t: (row(b, t), 1)),
                  pl.BlockSpec((rows, v_w), lambda b, t: (row(b, t), 2)),
                  pl.BlockSpec((rows, GLA_RANK_PAD), lambda b, t: (row(b, t), (2 * qk_w + 2 * v_w) // GLA_RANK_PAD)),
                  const((GLA_RANK_PAD, qk_w)), const((1, qk_w)), const((1, GLA_DV)),
                  const(a3.shape), const(lvl.shape)],
        out_specs=pl.BlockSpec((rows, v_w), lambda b, t: (row(b, t), 0)),
        out_shape=jax.ShapeDtypeStruct((n, v_w), BF16),
        scratch_shapes=[pltpu.VMEM((HEADS, GLA_DV, DK), F32)],
        compiler_params=_params("parallel", "arbitrary"),
        name="gla",
    )(proj, proj, proj, proj, proj, w_alpha, b_alpha.reshape(1, qk_w), o_norm.reshape(1, GLA_DV), a3, lvl)


def _qk_norm(x, w2):
    lane = lax.broadcasted_iota(jnp.int32, x.shape, 1)
    first = lane < DA_HEAD_DIM
    sq = x * x
    ms1 = jnp.sum(jnp.where(first, sq, 0.0), axis=-1, keepdims=True) * (1.0 / DA_HEAD_DIM)
    ms2 = jnp.sum(jnp.where(first, 0.0, sq), axis=-1, keepdims=True) * (1.0 / DA_HEAD_DIM)
    r = jnp.where(first, lax.rsqrt(ms1 + EPS), lax.rsqrt(ms2 + EPS))
    return x * r * w2, first


def _score_bound(q_norm_w, k_norm_w):
    return (jnp.max(jnp.abs(q_norm_w)) * jnp.max(jnp.abs(k_norm_w))
            * (DA_HEAD_DIM * DA_HEAD_DIM ** -0.5 * LOG2_E * SCORE_BOUND_MARGIN))


def _attn_kernel(qi_tab, ki_tab, q_ref, k_ref, v_ref, qn_ref, kn_ref, lam_ref, sub_ref, o_ref,
                 qs_ref, m_ref, l_ref, acc_ref, *, lam_init, bounded):
    p = pl.program_id(2)
    qi = qi_tab[p]
    ki = ki_tab[p]
    tq = ATTN_BLOCK
    dv = 2 * DA_HEAD_DIM
    heads = [slice(h * dv, (h + 1) * dv) for h in range(ATTN_HEADS)]

    @pl.when(ki == 0)
    def _():
        for h, sl in enumerate(heads):
            qn, first = _qk_norm(q_ref[:, sl], qn_ref[...])
            qn = qn * (DA_HEAD_DIM ** -0.5 * LOG2_E)
            q1 = jnp.where(first, qn, 0.0).T.astype(BF16)
            q2 = jnp.where(first, 0.0, qn).T.astype(BF16)
            for copy in range(1 if bounded else 2):
                qs_ref[h, copy, :, :tq] = q1
                qs_ref[h, copy, :, tq:] = q2
            if bounded:
                bound = _score_bound(qn_ref[...], kn_ref[...])
                row = lax.broadcasted_iota(jnp.int32, (dv, 2 * tq), 0)
                qs_ref[h, 1] = jnp.where(row == 0, -bound, 0.0).astype(BF16)
        if not bounded:
            m_ref[...] = jnp.full_like(m_ref, -jnp.inf)
        l_ref[...] = jnp.zeros_like(l_ref)
        acc_ref[...] = jnp.zeros_like(acc_ref)

    def step(masked):
        tk = k_ref.shape[0]
        ones = jnp.ones((16, tk), BF16)
        unit = jnp.where(lax.broadcasted_iota(jnp.int32, (tk, dv), 1) == 0, 1.0, 0.0).astype(BF16)
        for h, sl in enumerate(heads):
            kn, _ = _qk_norm(k_ref[:, sl], kn_ref[...])
            kn = kn.astype(BF16)

            def mask(s):
                if not masked:
                    return s
                key = lax.broadcasted_iota(jnp.int32, s.shape, 0)
                qry = lax.broadcasted_iota(jnp.int32, s.shape, 1)
                qry = jnp.where(qry >= tq, qry - tq, qry)
                return jnp.where(key <= qry, s, -jnp.inf)

            if bounded:
                shifted = mask(_dot(jnp.concatenate([kn, unit], axis=1), qs_ref[h].reshape(2 * dv, 2 * tq)))
                alpha = 1.0
            else:
                m_prev = m_ref[h]
                m_new = jnp.maximum(m_prev, jnp.max(mask(_dot(kn, qs_ref[h, 0])), axis=0, keepdims=True))
                alpha = jnp.exp2(m_prev - m_new)
                shifted = mask(_dot(kn, qs_ref[h, 1])) - m_new
                m_ref[h] = m_new
            pr = jnp.exp2(shifted).astype(BF16) if bounded else jnp.exp2(shifted.astype(BF16))
            vt = jnp.concatenate([v_ref[:, sl].T.astype(BF16), ones], axis=0)
            pv = _dot(vt, pr)
            l_ref[h] = alpha * l_ref[h] + pv[dv:dv + 1]
            acc_ref[h] = alpha * acc_ref[h] + pv[:dv]

    @pl.when(ki < qi)
    def _():
        step(False)

    @pl.when(ki == qi)
    def _():
        step(True)
        lv = lam_ref[...]
        lam = (jnp.exp(jnp.sum(lv[0:1] * lv[1:2], axis=-1, keepdims=True))
               - jnp.exp(jnp.sum(lv[2:3] * lv[3:4], axis=-1, keepdims=True)) + lam_init)
        for h, sl in enumerate(heads):
            o1 = acc_ref[h, :, :tq] / l_ref[h, :, :tq]
            o2 = acc_ref[h, :, tq:] / l_ref[h, :, tq:]
            o = o1 - lam * o2
            o = o * lax.rsqrt(jnp.mean(o * o, axis=0, keepdims=True) + EPS) * sub_ref[...] * (1.0 - lam_init)
            o_ref[:, sl] = o.T.astype(BF16)


def _diff_attn(proj, q_norm, k_norm, lam_vecs, subln, lam_init, batch, seq):
    n = proj.shape[0]
    blk = ATTN_BLOCK
    nq = seq // blk
    dv = 2 * DA_HEAD_DIM
    pairs = [(qi, ki) for qi in range(nq) for ki in range(qi + 1)]
    qi_tab = jnp.asarray([p[0] for p in pairs], jnp.int32)
    ki_tab = jnp.asarray([p[1] for p in pairs], jnp.int32)
    width = ATTN_HEADS * dv
    groups = HEADS // ATTN_HEADS
    q_col = (4 * HG_WIDTH) // width
    const = lambda shape: pl.BlockSpec(shape, lambda b, g, p, qt, kt: (0, 0))
    grid_spec = pltpu.PrefetchScalarGridSpec(
        num_scalar_prefetch=2,
        grid=(batch, groups, len(pairs)),
        in_specs=[pl.BlockSpec((blk, width), lambda b, g, p, qt, kt: (b * nq + qt[p], q_col + g)),
                  pl.BlockSpec((blk, width), lambda b, g, p, qt, kt: (b * nq + kt[p], q_col + groups + g)),
                  pl.BlockSpec((blk, width), lambda b, g, p, qt, kt: (b * nq + kt[p], q_col + 2 * groups + g)),
                  const((1, dv)), const((1, dv)), const((4, DA_HEAD_DIM)), const((dv, 1))],
        out_specs=pl.BlockSpec((blk, width), lambda b, g, p, qt, kt: (b * nq + qt[p], g)),
        scratch_shapes=[pltpu.VMEM((ATTN_HEADS, 2, dv, 2 * blk), BF16),
                        pltpu.VMEM((ATTN_HEADS, 1, 2 * blk), F32),
                        pltpu.VMEM((ATTN_HEADS, 1, 2 * blk), F32),
                        pltpu.VMEM((ATTN_HEADS, dv, 2 * blk), F32)])
    args = (qi_tab, ki_tab, proj, proj, proj, jnp.tile(q_norm, 2).reshape(1, -1), jnp.tile(k_norm, 2).reshape(1, -1),
            lam_vecs, subln.reshape(-1, 1))

    def run(bounded):
        return pl.pallas_call(
            functools.partial(_attn_kernel, lam_init=lam_init, bounded=bounded),
            grid_spec=grid_spec,
            out_shape=jax.ShapeDtypeStruct((n, HEADS * dv), BF16),
            compiler_params=_params("parallel", "parallel", "arbitrary"),
            name="diff_attn_bounded" if bounded else "diff_attn",
        )(*args)

    return lax.cond(_score_bound(q_norm, k_norm) <= MAX_SCORE_BOUND, lambda: run(True), lambda: run(False))


def _outproj_router_kernel(*refs, n_a):
    a_refs = refs[:n_a]
    (w_ref, x_ref, nw_ref, wr_hi_ref, wr_lo_ref, br_ref, tri_ref,
     x2_ref, meta_ref, rec_ref, cnt_ref, carry_ref) = refs[n_a:]

    @pl.when(pl.program_id(0) == 0)
    def _():
        carry_ref[...] = jnp.zeros_like(carry_ref)

    x2 = _load_tokens(x_ref)
    off = 0
    for a_ref in a_refs:
        kk = a_ref.shape[1]
        x2 = x2 + _dot(a_ref[...], w_ref[off:off + kk, :])
        off += kk
    _store_tokens(x2_ref, x2)

    h = _rms(x2, nw_ref[...])
    h_hi = h.astype(BF16)
    h_lo = (h - h_hi.astype(F32)).astype(BF16)
    wr_hi = wr_hi_ref[...]
    by_hi = _dot_nt(jnp.concatenate([wr_hi, wr_lo_ref[...]], axis=0), h_hi)
    lt = by_hi[:ROUTER_ROWS] + by_hi[ROUTER_ROWS:] + _dot_nt(wr_hi, h_lo) + br_ref[...]
    row = lambda r: lt[r:r + 1, :]

    gl = [row(g) for g in range(N_GROUPS)]
    gmax = functools.reduce(jnp.maximum, gl)
    gsel = jnp.where(gl[0] == gmax, 0.0, jnp.where(gl[1] == gmax, 1.0, jnp.where(gl[2] == gmax, 2.0, 3.0)))
    g_w = 1.0 / functools.reduce(jnp.add, [jnp.exp(x - gmax) for x in gl])

    el = []
    for e in range(EXPERTS_PER_GROUP):
        v = row(N_GROUPS + e)
        for g in range(1, N_GROUPS):
            v = jnp.where(gsel == float(g), row(N_GROUPS + EXPERTS_PER_GROUP * g + e), v)
        el.append(v)
    emax = functools.reduce(jnp.maximum, el)
    sel = []
    for e in range(EXPERTS_PER_GROUP):
        ahead = jnp.zeros_like(emax)
        for o in range(EXPERTS_PER_GROUP):
            if o < e:
                ahead = ahead + jnp.where(el[o] >= el[e], 1.0, 0.0)
            elif o > e:
                ahead = ahead + jnp.where(el[o] > el[e], 1.0, 0.0)
        sel.append(ahead < 2.0)
    pe = [jnp.where(sel[e], jnp.exp(el[e] - emax), 0.0) for e in range(EXPERTS_PER_GROUP)]
    scale = g_w / functools.reduce(jnp.add, pe)
    zero = jnp.zeros_like(emax)
    seen = zero
    lo_idx, hi_idx, c_lo, c_hi = zero, zero, zero, zero
    for e in range(EXPERTS_PER_GROUP):
        is_lo = sel[e] & (seen == 0.0)
        is_hi = sel[e] & (seen == 1.0)
        lo_idx = jnp.where(is_lo, float(e), lo_idx)
        hi_idx = jnp.where(is_hi, float(e), hi_idx)
        c_lo = jnp.where(is_lo, pe[e] * scale, c_lo)
        c_hi = jnp.where(is_hi, pe[e] * scale, c_hi)
        seen = seen + jnp.where(sel[e], 1.0, 0.0)
    pair = lo_idx * (7.0 - lo_idx) * 0.5 + (hi_idx - lo_idx - 1.0)
    bucket = gsel * float(N_PAIRS) + pair

    bid = lax.broadcasted_iota(jnp.int32, (ROUTER_ROWS, bucket.shape[1]), 0).astype(F32)
    onehot = jnp.where(bid == bucket, 1.0, 0.0)
    before = _dot(onehot.astype(BF16), tri_ref[...]) + carry_ref[...]
    rank = jnp.sum(onehot * before, axis=0, keepdims=True)
    carry = carry_ref[...] + jnp.sum(onehot, axis=1, keepdims=True)
    carry_ref[...] = carry
    cnt_ref[...] = carry
    meta = jnp.concatenate([bucket, rank, c_lo, c_hi, zero, zero, zero, zero], axis=0)
    meta_ref[...] = meta
    rec_ref[...] = jnp.concatenate([meta, jnp.zeros((LANE - meta.shape[0], meta.shape[1]), F32)], axis=0).T


def _outproj_router(a_list, w_out, x, n, norm_w, w_group, b_group, w_router, b_router):
    d = D_MODEL
    tm = ROW_TILE
    tok = lambda: pl.BlockSpec((tm * LANE_TILES, LANE), lambda i: (i, 0))
    wr =jnp.concatenate([w_group, w_router.transpose(1, 0, 2).reshape(d, N_EXPERTS)], axis=1)
    wr_t = jnp.zeros((ROUTER_ROWS, d), F32).at[:wr.shape[1]].set(wr.T)
    wr_hi = wr_t.astype(BF16)
    wr_lo = (wr_t - wr_hi.astype(F32)).astype(BF16)
    br = jnp.zeros((ROUTER_ROWS, 1), F32).at[:wr.shape[1], 0].set(
        jnp.concatenate([b_group, b_router.reshape(-1)]))
    tri = jnp.asarray(np.triu(np.ones((tm, tm), np.float32), 1), BF16)
    const = lambda shape: pl.BlockSpec(shape, lambda i: (0, 0))
    in_specs = [pl.BlockSpec((tm, a.shape[1]), lambda i: (i, 0)) for a in a_list]
    in_specs += [const((d, d)), _token_spec(x, tm), const((1, d)),
                 const((ROUTER_ROWS, d)), const((ROUTER_ROWS, d)), const((ROUTER_ROWS, 1)), const((tm, tm))]
    return pl.pallas_call(
        functools.partial(_outproj_router_kernel, n_a=len(a_list)),
        grid=(n // tm,),
        in_specs=in_specs,
        out_specs=[tok(),
                   pl.BlockSpec((8, tm), lambda i: (0, i)),
                   pl.BlockSpec((tm, LANE), lambda i: (i, 0)),
                   const((ROUTER_ROWS, 1))],
        out_shape=[jax.ShapeDtypeStruct((n * LANE_TILES, LANE), F32),
                   jax.ShapeDtypeStruct((8, n), F32),
                   jax.ShapeDtypeStruct((n, LANE), F32),
                   jax.ShapeDtypeStruct((ROUTER_ROWS, 1), F32)],
        scratch_shapes=[pltpu.VMEM((ROUTER_ROWS, 1), F32)],
        compiler_params=_params("arbitrary"),
        name="outproj_router",
    )(*a_list, w_out, x, norm_w.reshape(1, d), wr_hi, wr_lo, br, tri)


def _gather_tokens(table, idx):
    m = idx.shape[0]
    info = plsc.get_sparse_core_info()
    n_workers = info.num_cores * info.num_subcores
    per_worker = m // n_workers
    n_chunks = per_worker // SC_CHUNK
    assert per_worker * n_workers == m and n_chunks * SC_CHUNK == per_worker and n_chunks % 2 == 0
    mesh = plsc.VectorSubcoreMesh(core_axis_name="core", subcore_axis_name="subcore")
    rows = pltpu.VMEM((SC_CHUNK,) + table.shape[1:], table.dtype)

    @functools.partial(
        pl.kernel, mesh=mesh,
        out_type=jax.ShapeDtypeStruct((m,) + table.shape[1:], table.dtype),
        scratch_types=[pltpu.VMEM((n_chunks, SC_CHUNK), jnp.int32), rows, rows,
                       pltpu.SemaphoreType.DMA, pltpu.SemaphoreType.DMA],
        name="gather_tokens")
    def gather(table_hbm, idx_hbm, out_hbm, idx_v, rows_a, rows_b, sem_a, sem_b):
        worker = lax.axis_index("subcore") * info.num_cores + lax.axis_index("core")
        bufs = ((rows_a, sem_a), (rows_b, sem_b))

        def fetch(j, buf):
            return pltpu.make_async_copy(table_hbm.at[idx_v.at[j]], bufs[buf][0], bufs[buf][1])

        pltpu.sync_copy(idx_hbm.at[worker], idx_v)
        fetch(0, 0).start()

        @pl.loop(0, n_chunks, step=2)
        def _(j0):
            for buf in range(2):
                j = j0 + buf
                fetch(j, buf).wait()

                @pl.when(j + 1 < n_chunks)
                def _():
                    fetch(j + 1, 1 - buf).start()

                off = pl.multiple_of(worker * per_worker + j * SC_CHUNK, SC_CHUNK)
                pltpu.sync_copy(bufs[buf][0], out_hbm.at[pl.ds(off, SC_CHUNK)])

    return gather(table, idx.reshape(n_workers, n_chunks, SC_CHUNK))


def _scatter_tokens(rows, idx, m):
    n = idx.shape[0]
    info = plsc.get_sparse_core_info()
    n_workers = info.num_cores * info.num_subcores
    per_worker = n // n_workers
    n_chunks = per_worker // SC_CHUNK
    assert per_worker * n_workers == n and n_chunks * SC_CHUNK == per_worker and n_chunks % 2 == 0
    mesh = plsc.VectorSubcoreMesh(core_axis_name="core", subcore_axis_name="subcore")
    buf = pltpu.VMEM((SC_CHUNK,) + rows.shape[1:], rows.dtype)

    @functools.partial(
        pl.kernel, mesh=mesh,
        out_type=jax.ShapeDtypeStruct((m,) + rows.shape[1:], rows.dtype),
        scratch_types=[pltpu.VMEM((n_chunks, SC_CHUNK), jnp.int32), buf, buf,
                       pltpu.SemaphoreType.DMA, pltpu.SemaphoreType.DMA],
        name="scatter_tokens")
    def scatter(rows_hbm, idx_hbm, out_hbm, idx_v, buf_a, buf_b, sem_a, sem_b):
        worker = lax.axis_index("subcore") * info.num_cores + lax.axis_index("core")
        bufs = ((buf_a, sem_a), (buf_b, sem_b))

        def load(j, b):
            off = pl.multiple_of(worker * per_worker + j * SC_CHUNK, SC_CHUNK)
            return pltpu.make_async_copy(rows_hbm.at[pl.ds(off, SC_CHUNK)], bufs[b][0], bufs[b][1])

        pltpu.sync_copy(idx_hbm.at[worker], idx_v)
        load(0, 0).start()

        @pl.loop(0, n_chunks, step=2)
        def _(j0):
            for b in range(2):
                j = j0 + b
                load(j, b).wait()

                @pl.when(j + 1 < n_chunks)
                def _():
                    load(j + 1, 1 - b).start()

                pltpu.sync_copy(bufs[b][0], out_hbm.at[idx_v.at[j]])

    return scatter(rows, idx.reshape(n_workers, n_chunks, SC_CHUNK))


def _moe_kernel(tid, e_lo, e_hi, n_valid, n_tiles, x_ref, rec_ref, nw_ref, g_lo_ref, g_hi_ref, u_lo_ref, u_hi_ref,
                d_lo_ref, d_hi_ref, o_ref):
    del tid, e_lo, e_hi
    t = pl.program_id(0)

    @pl.when(t < n_tiles[0])
    def _():
        valid = lax.broadcasted_iota(jnp.int32, (MOE_TILE, 1), 0) < n_valid[t]
        x = jnp.where(valid, _load_tokens(x_ref), 0.0)
        h = _rms(x, nw_ref[...]).astype(BF16)
        rec = jnp.where(valid, rec_ref[...], 0.0)

        def expert(g_ref, u_ref, d_ref):
            act = _silu(_dot(h, g_ref[0])) * _dot(h, u_ref[0])
            return _dot(act.astype(BF16), d_ref[0])

        y = (rec[:, 2:3] * expert(g_lo_ref, u_lo_ref, d_lo_ref)
             + rec[:, 3:4] * expert(g_hi_ref, u_hi_ref, d_hi_ref))
        _store_tokens(o_ref, jnp.where(valid, _load_tokens(x_ref), 0.0) + y)


def _moe(x2, n, meta, rec, cnt, norm_w, w_gate, w_up, w_down, layer):
    d = D_MODEL
    lt = LANE_TILES
    tm = MOE_TILE
    max_tiles = n // tm + N_BUCKETS
    n_slots = max_tiles * tm
    counts = cnt[:N_BUCKETS, 0].astype(jnp.int32)
    tiles_per_bucket = (counts + tm - 1) // tm
    tile_end = jnp.cumsum(tiles_per_bucket)
    tile_start = tile_end - tiles_per_bucket
    n_tiles = tile_end[-1]
    in_bucket = meta[0][:, None] == jnp.arange(N_BUCKETS, dtype=F32)[None, :]
    dest = meta[1].astype(jnp.int32) + jnp.sum(jnp.where(in_bucket, tile_start[None, :] * tm, 0), axis=1)
    x_sorted = _scatter_tokens(x2.reshape(n, lt, LANE), dest, n_slots).reshape(n_slots * lt, LANE)
    rec_sorted = _scatter_tokens(rec, dest, n_slots)

    tile_ids = jnp.minimum(jnp.arange(max_tiles, dtype=jnp.int32), n_tiles - 1)
    of_bucket = lambda table: jnp.sum(jnp.where(tile_bucket[:, None] == jnp.arange(N_BUCKETS)[None, :], table[None, :], 0), axis=1)
    tile_bucket = jnp.sum((tile_ids[:, None] >= tile_end[None, :]).astype(jnp.int32), axis=1)
    n_valid = jnp.clip(of_bucket(counts) - (tile_ids - of_bucket(tile_start)) * tm, 0, tm)
    group = tile_bucket // N_PAIRS
    pair = tile_bucket % N_PAIRS
    pair_lo = jnp.sum(jnp.where(pair[:, None] == jnp.arange(N_PAIRS)[None, :], jnp.asarray(PAIR_LO, jnp.int32)[None, :], 0), axis=1)
    pair_hi = jnp.sum(jnp.where(pair[:, None] == jnp.arange(N_PAIRS)[None, :], jnp.asarray(PAIR_HI, jnp.int32)[None, :], 0), axis=1)
    e_lo = layer * N_EXPERTS + group * EXPERTS_PER_GROUP + pair_lo
    e_hi = layer * N_EXPERTS + group * EXPERTS_PER_GROUP + pair_hi

    grid_spec = pltpu.PrefetchScalarGridSpec(
        num_scalar_prefetch=5,
        grid=(max_tiles,),
        in_specs=[pl.BlockSpec((tm * lt, LANE), lambda t, tid, lo, hi, nv, nt: (tid[t], 0)),
                  pl.BlockSpec((tm, LANE), lambda t, tid, lo, hi, nv, nt: (tid[t], 0)),
                  pl.BlockSpec((1, d), lambda t, tid, lo, hi, nv, nt: (0, 0)),
                  pl.BlockSpec((1, d, D_EXPERT), lambda t, tid, lo, hi, nv, nt: (lo[t], 0, 0)),
                  pl.BlockSpec((1, d, D_EXPERT), lambda t, tid, lo, hi, nv, nt: (hi[t], 0, 0)),
                  pl.BlockSpec((1, d, D_EXPERT), lambda t, tid, lo, hi, nv, nt: (lo[t], 0, 0)),
                  pl.BlockSpec((1, d, D_EXPERT), lambda t, tid, lo, hi, nv, nt: (hi[t], 0, 0)),
                  pl.BlockSpec((1, D_EXPERT, d), lambda t, tid, lo, hi, nv, nt: (lo[t], 0, 0)),
                  pl.BlockSpec((1, D_EXPERT, d), lambda t, tid, lo, hi, nv, nt: (hi[t], 0, 0))],
        out_specs=pl.BlockSpec((tm * lt, LANE), lambda t, tid, lo, hi, nv, nt: (tid[t], 0)))
    out_sorted = pl.pallas_call(
        _moe_kernel,
        grid_spec=grid_spec,
        out_shape=jax.ShapeDtypeStruct((n_slots * lt, LANE), F32),
        compiler_params=_params("arbitrary"),
        name="moe",
    )(tile_ids, e_lo, e_hi, n_valid, n_tiles.reshape(1), x_sorted, rec_sorted, norm_w.reshape(1, d),
      w_gate, w_gate, w_up, w_up, w_down, w_down)
    return _gather_tokens(out_sorted.reshape(n_slots, lt, LANE), dest).reshape(n * lt, LANE)


def kernel(x, attn_norm, ffn_norm, even_w_in, even_w_out, hg_lb_logits, hg_o_norm, da_q_norm, da_k_norm, da_lambda, da_subln, odd_w_in, gla_w_alpha, gla_b_alpha, gla_o_norm, odd_w_out, moe_w_group, moe_b_group, moe_w_router, moe_b_router, moe_w_gate, moe_w_up, moe_w_down):
    batch, seq, d = x.shape
    depth = attn_norm.shape[0]
    n = batch * seq
    xf = x.reshape(n, d)

    lb_sm = jax.nn.softmax(hg_lb_logits.astype(F32), axis=0)
    lb_all = jnp.cumsum(lb_sm, axis=0) - lb_sm[0:1]
    rank = gla_w_alpha.shape[1]

    experts = lambda w: _to_bf16(w.reshape((-1,) + w.shape[2:]))
    even_w_in, even_w_out, odd_w_out = _to_bf16(even_w_in), _to_bf16(even_w_out), _to_bf16(odd_w_out)
    odd_w_in = _to_bf16(jnp.swapaxes(odd_w_in, 1, 2), odd_w_in.shape[2] + GLA_RANK_PAD - rank)
    moe_w_gate, moe_w_up, moe_w_down = experts(moe_w_gate), experts(moe_w_up), experts(moe_w_down)

    for layer in range(depth):
        j = layer // 2
        if layer % 2 == 0:
            lam_init = 0.8 - 0.6 * math.exp(-0.3 * layer)
            proj = _inproj(xf, n, attn_norm[layer], even_w_in[j])
            o_a = _hgrn(proj, lb_all[j], hg_o_norm[j], batch, seq)
            o_b = _diff_attn(proj, da_q_norm[j], da_k_norm[j], da_lambda[j], da_subln[j], lam_init, batch, seq)
            mix = [o_a, o_b]
            w_out = even_w_out[j]
        else:
            w_alpha = jnp.pad(gla_w_alpha[j], ((0, GLA_RANK_PAD - rank), (0, 0))).astype(BF16)
            proj = _inproj(xf, n, attn_norm[layer], odd_w_in[j], w_transposed=True)
            mix = [_gla(proj, w_alpha, gla_b_alpha[j], gla_o_norm[j], batch, seq)]
            w_out = odd_w_out[j]
        x2, meta, rec, cnt = _outproj_router(mix, w_out, xf, n, ffn_norm[layer], moe_w_group[layer],
                                        moe_b_group[layer], moe_w_router[layer], moe_b_router[layer])
        xf = _moe(x2, n, meta, rec, cnt, ffn_norm[layer], moe_w_gate, moe_w_up, moe_w_down, layer)
    return xf.reshape(batch, seq, d)
```

```python
import functools
import math

import numpy as np
import jax
import jax.numpy as jnp
from jax import lax
from jax.experimental import pallas as pl
from jax.experimental.pallas import tpu as pltpu
from jax.experimental.pallas import tpu_sc as plsc

F32 = jnp.float32
BF16 = jnp.bfloat16
EPS = 1e-6
LOG2_E = 1.4426950408889634

D_MODEL = 1024
LANE = 128
LANE_TILES = D_MODEL // LANE
HEADS = 4
DK = 128
HG_WIDTH = 512
DA_HEAD_DIM = 64
GLA_DV = 256
GLA_RANK_PAD = 128
GLA_TAU = 16.0
N_GROUPS = 4
EXPERTS_PER_GROUP = 4
N_EXPERTS = 16
D_EXPERT = 512
N_PAIRS = 6
N_BUCKETS = N_GROUPS * N_PAIRS
ROUTER_ROWS = 32

CHUNK = 128
GLA_STEP_CHUNKS = 4
LEVELS = 7
ATTN_BLOCK = 512
ATTN_HEADS = 4
SCORE_BOUND_MARGIN = 1.05
MAX_SCORE_BOUND = 60.0
ROW_TILE = 512
MOE_TILE = 256
MOE_STEP_TILES = 2
SC_CHUNK = 32
VMEM_LIMIT = 56 * 1024 * 1024
CAST_BLOCK_BYTES = 8 * 1024 * 1024

PAIR_LO = (0, 0, 0, 1, 1, 2)
PAIR_HI = (1, 2, 3, 2, 3, 3)


def _dot(a, b):
    return jnp.dot(a, b, preferred_element_type=F32)


def _dot_nt(a, b):
    return lax.dot_general(a, b, (((1,), (1,)), ((), ())), preferred_element_type=F32)


def _dot_tn(a, b):
    return lax.dot_general(a, b, (((0,), (0,)), ((), ())), preferred_element_type=F32)


def _rms(x, w):
    return x * lax.rsqrt(jnp.mean(x * x, axis=-1, keepdims=True) + EPS) * w


def _silu(x):
    return x * jax.nn.sigmoid(x)


def _params(*sem):
    return pltpu.CompilerParams(dimension_semantics=sem, vmem_limit_bytes=VMEM_LIMIT)


def _load_tokens(ref):
    if ref.shape[1] == D_MODEL:
        return ref[...]
    rows = ref.shape[0] // LANE_TILES
    return jnp.concatenate([ref[pl.ds(c, rows, stride=LANE_TILES), :] for c in range(LANE_TILES)], axis=1)


def _token_spec(x, rows):
    shape = (rows, D_MODEL) if x.shape[1] == D_MODEL else (rows * LANE_TILES, LANE)
    return pl.BlockSpec(shape, lambda i: (i, 0))


def _store_tokens(ref, val):
    rows = ref.shape[0] // LANE_TILES
    for c in range(LANE_TILES):
        ref[pl.ds(c, rows, stride=LANE_TILES), :] = val[:, c * LANE:(c + 1) * LANE]


def _cast_kernel(w_ref, o_ref):
    k = w_ref.shape[1]
    o_ref[:, :k, :] = w_ref[...].astype(BF16)
    if o_ref.shape[1] > k:
        o_ref[:, k:, :] = jnp.zeros((o_ref.shape[0], o_ref.shape[1] - k, o_ref.shape[2]), BF16)


def _to_bf16(w, k_pad=None):
    l, k, n = w.shape
    k_pad = k_pad or k
    per_matrix = k * n * 4
    lb = max([c for c in range(1, l + 1) if l % c == 0 and c * per_matrix <= CAST_BLOCK_BYTES] or [1])
    nb = n
    while lb == 1 and k * nb * 4 > CAST_BLOCK_BYTES and nb % (2 * LANE) == 0:
        nb //= 2
    return pl.pallas_call(
        _cast_kernel,
        grid=(l // lb, n // nb),
        in_specs=[pl.BlockSpec((lb, k, nb), lambda i, j: (i, 0, j))],
        out_specs=pl.BlockSpec((lb, k_pad, nb), lambda i, j: (i, 0, j)),
        out_shape=jax.ShapeDtypeStruct((l, k_pad, n), BF16),
        compiler_params=_params("parallel", "parallel"),
        name="to_bf16",
    )(w)


def _inproj_kernel(x_ref, nw_ref, w_ref, o_ref, *, w_transposed):
    h = _rms(_load_tokens(x_ref), nw_ref[...]).astype(BF16)
    o_ref[...] = _dot_nt(h, w_ref[...]) if w_transposed else _dot(h, w_ref[...])


def _inproj(x, n, norm_w, w, w_transposed=False):
    d = D_MODEL
    n_out = w.shape[0] if w_transposed else w.shape[1]
    return pl.pallas_call(
        functools.partial(_inproj_kernel, w_transposed=w_transposed),
        grid=(n // ROW_TILE,),
        in_specs=[_token_spec(x, ROW_TILE),
                  pl.BlockSpec((1, d), lambda i: (0, 0)),
                  pl.BlockSpec(w.shape, lambda i: (0, 0))],
        out_specs=pl.BlockSpec((ROW_TILE, n_out), lambda i: (i, 0)),
        out_shape=jax.ShapeDtypeStruct((n, n_out), F32),
        compiler_params=_params("parallel"),
        name="inproj",
    )(x, norm_w.reshape(1, d), w)


def _decay_tables():
    c = CHUNK
    a = np.zeros((LEVELS + 2, c, c), np.float32)
    t = np.arange(c)
    for l in range(LEVELS):
        s = 1 << l
        for r in range(c):
            mid = (r // (2 * s)) * 2 * s + s - 1
            if (r >> l) & 1:
                a[l, r] = (t > mid) & (t <= r)
            else:
                a[l, r] = (t > r) & (t <= mid)
    a[LEVELS] = t[None, :] <= t[:, None]
    a[LEVELS + 1] = t[None, :] > t[:, None]
    a = a.reshape((LEVELS + 2) * c, c)
    a3 = np.concatenate([a, a], axis=1)
    i, j = np.meshgrid(t, t, indexing="ij")
    x = np.maximum(i ^ j, 1)
    lvl = np.where(j < i, np.floor(np.log2(x)).astype(np.int32), np.where(i == j, LEVELS, -1))
    return jnp.asarray(a3, BF16), jnp.asarray(lvl, jnp.int32)


def _decay_factors(a3, g):
    g_hi = g.astype(BF16)
    g_lo = (g - g_hi.astype(F32)).astype(BF16)
    return jnp.exp(_dot(a3, jnp.concatenate([g_hi, g_lo], axis=0)))


def _gla_chunk(q, k, v, f, st_ref, lvl, rowid):
    c = CHUNK
    sc = jnp.where(lvl == LEVELS, jnp.sum(q * k, axis=-1, keepdims=True), 0.0)
    for l in range(LEVELS):
        is_query = ((rowid >> l) & 1) == 1
        m = (jnp.where(is_query, q, k) * f[l * c:(l + 1) * c]).astype(BF16)
        sc = jnp.where(lvl == l, _dot_nt(m, m), sc)
    qd = (q * f[LEVELS * c:(LEVELS + 1) * c]).astype(BF16)
    kd = (k * f[(LEVELS + 1) * c:(LEVELS + 2) * c]).astype(BF16)
    vb = v.astype(BF16)
    st = st_ref[...]
    o = _dot(sc.astype(BF16), vb) + _dot_nt(qd, st.astype(BF16))
    chunk_decay = f[(LEVELS + 1) * c - 1:(LEVELS + 1) * c, :]
    st_ref[...] = st * chunk_decay + _dot_tn(vb, kd)
    return o


def _hgrn_kernel(q_ref, f_ref, i_ref, g_ref, lb_ref, on_ref, a3_ref, lvl_ref, o_ref, st_ref):
    @pl.when(pl.program_id(1) == 0)
    def _():
        st_ref[...] = jnp.zeros_like(st_ref)

    lvl = lvl_ref[...]
    rowid = lax.broadcasted_iota(jnp.int32, (CHUNK, 1), 0)
    lb = lb_ref[...]
    for c in range(GLA_STEP_CHUNKS):
        rows = slice(c * CHUNK, (c + 1) * CHUNK)
        fl = f_ref[rows, :]
        f = _decay_factors(a3_ref[...], jnp.log(lb + (1.0 - lb) * jax.nn.sigmoid(fl)))
        key_in = (1.0 - lb) * jax.nn.sigmoid(-fl)
        for h in range(HEADS):
            sl = slice(h * DK, (h + 1) * DK)
            q = _silu(q_ref[rows, sl]) * (DK ** -0.5)
            o = _gla_chunk(q, key_in[:, sl], i_ref[rows, sl], f[:, sl], st_ref.at[h], lvl, rowid)
            o = _rms(o, on_ref[...])
            o_ref[rows, sl] = (o * _silu(g_ref[rows, sl])).astype(BF16)


def _hgrn(proj, lb, o_norm, batch, seq):
    n = proj.shape[0]
    rows = CHUNK * GLA_STEP_CHUNKS
    nt = seq // rows
    a3, lvl = _decay_tables()
    col = lambda j: pl.BlockSpec((rows, HG_WIDTH), lambda b, t: (b * nt + t, j))
    const = lambda shape: pl.BlockSpec(shape, lambda b, t: (0, 0))
    return pl.pallas_call(
        _hgrn_kernel,
        grid=(batch, nt),
        in_specs=[col(0), col(1), col(2), col(3), const((1, HG_WIDTH)), const((1, DK)),
                  const(a3.shape), const(lvl.shape)],
        out_specs=pl.BlockSpec((rows, HG_WIDTH), lambda b, t: (b * nt + t, 0)),
        out_shape=jax.ShapeDtypeStruct((n, HG_WIDTH), BF16),
        scratch_shapes=[pltpu.VMEM((HEADS, DK, DK), F32)],
        compiler_params=_params("parallel", "arbitrary"),
        name="hgrn",
    )(proj, proj, proj, proj, lb.reshape(1, HG_WIDTH), o_norm.reshape(1, DK), a3, lvl)


def _log_sigmoid(x):
    return jnp.minimum(x, 0.0) - jnp.log1p(jnp.exp(-jnp.abs(x)))


def _gla_kernel(q_ref, k_ref, v_ref, g_ref, a_ref, wa_ref, ba_ref, on_ref, a3_ref, lvl_ref, o_ref, st_ref):
    @pl.when(pl.program_id(1) == 0)
    def _():
        st_ref[...] = jnp.zeros_like(st_ref)

    lvl = lvl_ref[...]
    rowid = lax.broadcasted_iota(jnp.int32, (CHUNK, 1), 0)
    for c in range(GLA_STEP_CHUNKS):
        rows = slice(c * CHUNK, (c + 1) * CHUNK)
        gate_logit = _dot(a_ref[rows, :].astype(BF16), wa_ref[...]) + ba_ref[...]
        f = _decay_factors(a3_ref[...], _log_sigmoid(gate_logit) * (1.0 / GLA_TAU))
        for h in range(HEADS):
            sl = slice(h * DK, (h + 1) * DK)
            sv = slice(h * GLA_DV, (h + 1) * GLA_DV)
            q = q_ref[rows, sl] * (DK ** -0.5)
            o = _gla_chunk(q, k_ref[rows, sl], v_ref[rows, sv], f[:, sl], st_ref.at[h], lvl, rowid)
            o = _rms(o, on_ref[...])
            o_ref[rows, sv] = (o * _silu(g_ref[rows, sv])).astype(BF16)


def _gla(proj, w_alpha, b_alpha, o_norm, batch, seq):
    n = proj.shape[0]
    rows = CHUNK * GLA_STEP_CHUNKS
    nt = seq // rows
    qk_w = HEADS * DK
    v_w = HEADS * GLA_DV
    a3, lvl = _decay_tables()
    row = lambda b, t: b * nt + t
    const = lambda shape: pl.BlockSpec(shape, lambda b, t: (0, 0))
    return pl.pallas_call(
        _gla_kernel,
        grid=(batch, nt),
        in_specs=[pl.BlockSpec((rows, qk_w), lambda b, t: (row(b, t), 0)),
                  pl.BlockSpec((rows, qk_w), lambda b, t: (row(b, t), 1)),
                  pl.BlockSpec((rows, v_w), lambda b, t: (row(b, t), 1)),
                  pl.BlockSpec((rows, v_w), lambda b, t: (row(b, t), 2)),
                  pl.BlockSpec((rows, GLA_RANK_PAD), lambda b, t: (row(b, t), (2 * qk_w + 2 * v_w) // GLA_RANK_PAD)),
                  const((GLA_RANK_PAD, qk_w)), const((1, qk_w)), const((1, GLA_DV)),
                  const(a3.shape), const(lvl.shape)],
        out_specs=pl.BlockSpec((rows, v_w), lambda b, t: (row(b, t), 0)),
        out_shape=jax.ShapeDtypeStruct((n, v_w), BF16),
        scratch_shapes=[pltpu.VMEM((HEADS, GLA_DV, DK), F32)],
        compiler_params=_params("parallel", "arbitrary"),
        name="gla",
    )(proj, proj, proj, proj, proj, w_alpha, b_alpha.reshape(1, qk_w), o_norm.reshape(1, GLA_DV), a3, lvl)


def _qk_norm(x, w2):
    lane = lax.broadcasted_iota(jnp.int32, x.shape, 1)
    first = lane < DA_HEAD_DIM
    sq = x * x
    ms1 = jnp.sum(jnp.where(first, sq, 0.0), axis=-1, keepdims=True) * (1.0 / DA_HEAD_DIM)
    ms2 = jnp.sum(jnp.where(first, 0.0, sq), axis=-1, keepdims=True) * (1.0 / DA_HEAD_DIM)
    r = jnp.where(first, lax.rsqrt(ms1 + EPS), lax.rsqrt(ms2 + EPS))
    return x * r * w2, first


def _score_bound(q_norm_w, k_norm_w):
    return (jnp.max(jnp.abs(q_norm_w)) * jnp.max(jnp.abs(k_norm_w))
            * (DA_HEAD_DIM * DA_HEAD_DIM ** -0.5 * LOG2_E * SCORE_BOUND_MARGIN))


def _attn_kernel(qi_tab, ki_tab, q_ref, k_ref, v_ref, qn_ref, kn_ref, lam_ref, sub_ref, o_ref,
                 qs_ref, m_ref, l_ref, acc_ref, *, lam_init, bounded):
    p = pl.program_id(2)
    qi = qi_tab[p]
    ki = ki_tab[p]
    tq = ATTN_BLOCK
    dv = 2 * DA_HEAD_DIM
    heads = [slice(h * dv, (h + 1) * dv) for h in range(ATTN_HEADS)]

    @pl.when(ki == 0)
    def _():
        for h, sl in enumerate(heads):
            qn, first = _qk_norm(q_ref[:, sl], qn_ref[...])
            qn = qn * (DA_HEAD_DIM ** -0.5 * LOG2_E)
            q1 = jnp.where(first, qn, 0.0).T.astype(BF16)
            q2 = jnp.where(first, 0.0, qn).T.astype(BF16)
            for copy in range(1 if bounded else 2):
                qs_ref[h, copy, :, :tq] = q1
                qs_ref[h, copy, :, tq:] = q2
            if bounded:
                bound = _score_bound(qn_ref[...], kn_ref[...])
                row = lax.broadcasted_iota(jnp.int32, (dv, 2 * tq), 0)
                qs_ref[h, 1] = jnp.where(row == 0, -bound, 0.0).astype(BF16)
        if not bounded:
            m_ref[...] = jnp.full_like(m_ref, -jnp.inf)
        l_ref[...] = jnp.zeros_like(l_ref)
        acc_ref[...] = jnp.zeros_like(acc_ref)

    def step(masked):
        tk = k_ref.shape[0]
        ones = jnp.ones((16, tk), BF16)
        unit = jnp.where(lax.broadcasted_iota(jnp.int32, (tk, dv), 1) == 0, 1.0, 0.0).astype(BF16)
        for h, sl in enumerate(heads):
            kn, _ = _qk_norm(k_ref[:, sl], kn_ref[...])
            kn = kn.astype(BF16)

            def mask(s):
                if not masked:
                    return s
                key = lax.broadcasted_iota(jnp.int32, s.shape, 0)
                qry = lax.broadcasted_iota(jnp.int32, s.shape, 1)
                qry = jnp.where(qry >= tq, qry - tq, qry)
                return jnp.where(key <= qry, s, -jnp.inf)

            if bounded:
                shifted = mask(_dot(jnp.concatenate([kn, unit], axis=1), qs_ref[h].reshape(2 * dv, 2 * tq)))
                alpha = 1.0
            else:
                m_prev = m_ref[h]
                m_new = jnp.maximum(m_prev, jnp.max(mask(_dot(kn, qs_ref[h, 0])), axis=0, keepdims=True))
                alpha = jnp.exp2(m_prev - m_new)
                shifted = mask(_dot(kn, qs_ref[h, 1])) - m_new
                m_ref[h] = m_new
            pr = jnp.exp2(shifted).astype(BF16) if bounded else jnp.exp2(shifted.astype(BF16))
            vt = jnp.concatenate([v_ref[:, sl].T.astype(BF16), ones], axis=0)
            pv = _dot(vt, pr)
            l_ref[h] = alpha * l_ref[h] + pv[dv:dv + 1]
            acc_ref[h] = alpha * acc_ref[h] + pv[:dv]

    @pl.when(ki < qi)
    def _():
        step(False)

    @pl.when(ki == qi)
    def _():
        step(True)
        lv = lam_ref[...]
        lam = (jnp.exp(jnp.sum(lv[0:1] * lv[1:2], axis=-1, keepdims=True))
               - jnp.exp(jnp.sum(lv[2:3] * lv[3:4], axis=-1, keepdims=True)) + lam_init)
        for h, sl in enumerate(heads):
            o1 = acc_ref[h, :, :tq] / l_ref[h, :, :tq]
            o2 = acc_ref[h, :, tq:] / l_ref[h, :, tq:]
            o = o1 - lam * o2
            o = o * lax.rsqrt(jnp.mean(o * o, axis=0, keepdims=True) + EPS) * sub_ref[...] * (1.0 - lam_init)
            o_ref[:, sl] = o.T.astype(BF16)


def _diff_attn(proj, q_norm, k_norm, lam_vecs, subln, lam_init, batch, seq):
    n = proj.shape[0]
    blk = ATTN_BLOCK
    nq = seq // blk
    dv = 2 * DA_HEAD_DIM
    pairs = [(qi, ki) for qi in range(nq) for ki in range(qi + 1)]
    qi_tab = jnp.asarray([p[0] for p in pairs], jnp.int32)
    ki_tab = jnp.asarray([p[1] for p in pairs], jnp.int32)
    width = ATTN_HEADS * dv
    groups = HEADS // ATTN_HEADS
    q_col = (4 * HG_WIDTH) // width
    const = lambda shape: pl.BlockSpec(shape, lambda b, g, p, qt, kt: (0, 0))
    grid_spec = pltpu.PrefetchScalarGridSpec(
        num_scalar_prefetch=2,
        grid=(batch, groups, len(pairs)),
        in_specs=[pl.BlockSpec((blk, width), lambda b, g, p, qt, kt: (b * nq + qt[p], q_col + g)),
                  pl.BlockSpec((blk, width), lambda b, g, p, qt, kt: (b * nq + kt[p], q_col + groups + g)),
                  pl.BlockSpec((blk, width), lambda b, g, p, qt, kt: (b * nq + kt[p], q_col + 2 * groups + g)),
                  const((1, dv)), const((1, dv)), const((4, DA_HEAD_DIM)), const((dv, 1))],
        out_specs=pl.BlockSpec((blk, width), lambda b, g, p, qt, kt: (b * nq + qt[p], g)),
        scratch_shapes=[pltpu.VMEM((ATTN_HEADS, 2, dv, 2 * blk), BF16),
                        pltpu.VMEM((ATTN_HEADS, 1, 2 * blk), F32),
                        pltpu.VMEM((ATTN_HEADS, 1, 2 * blk), F32),
                        pltpu.VMEM((ATTN_HEADS, dv, 2 * blk), F32)])
    args = (qi_tab, ki_tab, proj, proj, proj, jnp.tile(q_norm, 2).reshape(1, -1), jnp.tile(k_norm, 2).reshape(1, -1),
            lam_vecs, subln.reshape(-1, 1))

    def run(bounded):
        return pl.pallas_call(
            functools.partial(_attn_kernel, lam_init=lam_init, bounded=bounded),
            grid_spec=grid_spec,
            out_shape=jax.ShapeDtypeStruct((n, HEADS * dv), BF16),
            compiler_params=_params("parallel", "parallel", "arbitrary"),
            name="diff_attn_bounded" if bounded else "diff_attn",
        )(*args)

    return lax.cond(_score_bound(q_norm, k_norm) <= MAX_SCORE_BOUND, lambda: run(True), lambda: run(False))


def _outproj_router_kernel(*refs, n_a):
    a_refs = refs[:n_a]
    (w_ref, x_ref, nw_ref, wr_hi_ref, wr_lo_ref, br_ref, tri_ref,
     x2_ref, meta_ref, rec_ref, cnt_ref, carry_ref) = refs[n_a:]

    @pl.when(pl.program_id(0) == 0)
    def _():
        carry_ref[...] = jnp.zeros_like(carry_ref)

    x2 = _load_tokens(x_ref)
    off = 0
    for a_ref in a_refs:
        kk = a_ref.shape[1]
        x2 = x2 + _dot(a_ref[...], w_ref[off:off + kk, :])
        off += kk
    _store_tokens(x2_ref, x2)

    h = _rms(x2, nw_ref[...])
    h_hi = h.astype(BF16)
    h_lo = (h - h_hi.astype(F32)).astype(BF16)
    wr_hi = wr_hi_ref[...]
    by_hi = _dot_nt(jnp.concatenate([wr_hi, wr_lo_ref[...]], axis=0), h_hi)
    lt = by_hi[:ROUTER_ROWS] + by_hi[ROUTER_ROWS:] + _dot_nt(wr_hi, h_lo) + br_ref[...]
    row = lambda r: lt[r:r + 1, :]

    gl = [row(g) for g in range(N_GROUPS)]
    gmax = functools.reduce(jnp.maximum, gl)
    gsel = jnp.where(gl[0] == gmax, 0.0, jnp.where(gl[1] == gmax, 1.0, jnp.where(gl[2] == gmax, 2.0, 3.0)))
    g_w = 1.0 / functools.reduce(jnp.add, [jnp.exp(x - gmax) for x in gl])

    el = []
    for e in range(EXPERTS_PER_GROUP):
        v = row(N_GROUPS + e)
        for g in range(1, N_GROUPS):
            v = jnp.where(gsel == float(g), row(N_GROUPS + EXPERTS_PER_GROUP * g + e), v)
        el.append(v)
    emax = functools.reduce(jnp.maximum, el)
    sel = []
    for e in range(EXPERTS_PER_GROUP):
        ahead = jnp.zeros_like(emax)
        for o in range(EXPERTS_PER_GROUP):
            if o < e:
                ahead = ahead + jnp.where(el[o] >= el[e], 1.0, 0.0)
            elif o > e:
                ahead = ahead + jnp.where(el[o] > el[e], 1.0, 0.0)
        sel.append(ahead < 2.0)
    pe = [jnp.where(sel[e], jnp.exp(el[e] - emax), 0.0) for e in range(EXPERTS_PER_GROUP)]
    scale = g_w / functools.reduce(jnp.add, pe)
    zero = jnp.zeros_like(emax)
    seen = zero
    lo_idx, hi_idx, c_lo, c_hi = zero, zero, zero, zero
    for e in range(EXPERTS_PER_GROUP):
        is_lo = sel[e] & (seen == 0.0)
        is_hi = sel[e] & (seen == 1.0)
        lo_idx = jnp.where(is_lo, float(e), lo_idx)
        hi_idx = jnp.where(is_hi, float(e), hi_idx)
        c_lo = jnp.where(is_lo, pe[e] * scale, c_lo)
        c_hi = jnp.where(is_hi, pe[e] * scale, c_hi)
        seen = seen + jnp.where(sel[e], 1.0, 0.0)
    pair = lo_idx * (7.0 - lo_idx) * 0.5 + (hi_idx - lo_idx - 1.0)
    bucket = gsel * float(N_PAIRS) + pair

    bid = lax.broadcasted_iota(jnp.int32, (ROUTER_ROWS, bucket.shape[1]), 0).astype(F32)
    onehot = jnp.where(bid == bucket, 1.0, 0.0)
    before = _dot(onehot.astype(BF16), tri_ref[...]) + carry_ref[...]
    rank = jnp.sum(onehot * before, axis=0, keepdims=True)
    carry = carry_ref[...] + jnp.sum(onehot, axis=1, keepdims=True)
    carry_ref[...] = carry
    cnt_ref[...] = carry
    meta = jnp.concatenate([bucket, rank, c_lo, c_hi, zero, zero, zero, zero], axis=0)
    meta_ref[...] = meta
    rec_ref[...] = jnp.concatenate([meta, jnp.zeros((LANE - meta.shape[0], meta.shape[1]), F32)], axis=0).T


def _outproj_router(a_list, w_out, x, n, norm_w, w_group, b_group, w_router, b_router):
    d = D_MODEL
    tm = ROW_TILE
    tok = lambda: pl.BlockSpec((tm * LANE_TILES, LANE), lambda i: (i, 0))
    wr =jnp.concatenate([w_group, w_router.transpose(1, 0, 2).reshape(d, N_EXPERTS)], axis=1)
    wr_t = jnp.zeros((ROUTER_ROWS, d), F32).at[:wr.shape[1]].set(wr.T)
    wr_hi = wr_t.astype(BF16)
    wr_lo = (wr_t - wr_hi.astype(F32)).astype(BF16)
    br = jnp.zeros((ROUTER_ROWS, 1), F32).at[:wr.shape[1], 0].set(
        jnp.concatenate([b_group, b_router.reshape(-1)]))
    tri = jnp.asarray(np.triu(np.ones((tm, tm), np.float32), 1), BF16)
    const = lambda shape: pl.BlockSpec(shape, lambda i: (0, 0))
    in_specs = [pl.BlockSpec((tm, a.shape[1]), lambda i: (i, 0)) for a in a_list]
    in_specs += [const((d, d)), _token_spec(x, tm), const((1, d)),
                 const((ROUTER_ROWS, d)), const((ROUTER_ROWS, d)), const((ROUTER_ROWS, 1)), const((tm, tm))]
    return pl.pallas_call(
        functools.partial(_outproj_router_kernel, n_a=len(a_list)),
        grid=(n // tm,),
        in_specs=in_specs,
        out_specs=[tok(),
                   pl.BlockSpec((8, tm), lambda i: (0, i)),
                   pl.BlockSpec((tm, LANE), lambda i: (i, 0)),
                   const((ROUTER_ROWS, 1))],
        out_shape=[jax.ShapeDtypeStruct((n * LANE_TILES, LANE), F32),
                   jax.ShapeDtypeStruct((8, n), F32),
                   jax.ShapeDtypeStruct((n, LANE), F32),
                   jax.ShapeDtypeStruct((ROUTER_ROWS, 1), F32)],
        scratch_shapes=[pltpu.VMEM((ROUTER_ROWS, 1), F32)],
        compiler_params=_params("arbitrary"),
        name="outproj_router",
    )(*a_list, w_out, x, norm_w.reshape(1, d), wr_hi, wr_lo, br, tri)


def _gather_tokens(table, idx):
    m = idx.shape[0]
    info = plsc.get_sparse_core_info()
    n_workers = info.num_cores * info.num_subcores
    per_worker = m // n_workers
    n_chunks = per_worker // SC_CHUNK
    assert per_worker * n_workers == m and n_chunks * SC_CHUNK == per_worker and n_chunks % 2 == 0
    mesh = plsc.VectorSubcoreMesh(core_axis_name="core", subcore_axis_name="subcore")
    rows = pltpu.VMEM((SC_CHUNK,) + table.shape[1:], table.dtype)

    @functools.partial(
        pl.kernel, mesh=mesh,
        out_type=jax.ShapeDtypeStruct((m,) + table.shape[1:], table.dtype),
        scratch_types=[pltpu.VMEM((n_chunks, SC_CHUNK), jnp.int32), rows, rows,
                       pltpu.SemaphoreType.DMA, pltpu.SemaphoreType.DMA],
        name="gather_tokens")
    def gather(table_hbm, idx_hbm, out_hbm, idx_v, rows_a, rows_b, sem_a, sem_b):
        worker = lax.axis_index("subcore") * info.num_cores + lax.axis_index("core")
        bufs = ((rows_a, sem_a), (rows_b, sem_b))

        def fetch(j, buf):
            return pltpu.make_async_copy(table_hbm.at[idx_v.at[j]], bufs[buf][0], bufs[buf][1])

        pltpu.sync_copy(idx_hbm.at[worker], idx_v)
        fetch(0, 0).start()

        @pl.loop(0, n_chunks, step=2)
        def _(j0):
            for buf in range(2):
                j = j0 + buf
                fetch(j, buf).wait()

                @pl.when(j + 1 < n_chunks)
                def _():
                    fetch(j + 1, 1 - buf).start()

                off = pl.multiple_of(worker * per_worker + j * SC_CHUNK, SC_CHUNK)
                pltpu.sync_copy(bufs[buf][0], out_hbm.at[pl.ds(off, SC_CHUNK)])

    return gather(table, idx.reshape(n_workers, n_chunks, SC_CHUNK))


def _scatter_tokens(rows, idx, m):
    n = idx.shape[0]
    info = plsc.get_sparse_core_info()
    n_workers = info.num_cores * info.num_subcores
    per_worker = n // n_workers
    n_chunks = per_worker // SC_CHUNK
    assert per_worker * n_workers == n and n_chunks * SC_CHUNK == per_worker and n_chunks % 2 == 0
    mesh = plsc.VectorSubcoreMesh(core_axis_name="core", subcore_axis_name="subcore")
    buf = pltpu.VMEM((SC_CHUNK,) + rows.shape[1:], rows.dtype)

    @functools.partial(
        pl.kernel, mesh=mesh,
        out_type=jax.ShapeDtypeStruct((m,) + rows.shape[1:], rows.dtype),
        scratch_types=[pltpu.VMEM((n_chunks, SC_CHUNK), jnp.int32), buf, buf,
                       pltpu.SemaphoreType.DMA, pltpu.SemaphoreType.DMA],
        name="scatter_tokens")
    def scatter(rows_hbm, idx_hbm, out_hbm, idx_v, buf_a, buf_b, sem_a, sem_b):
        worker = lax.axis_index("subcore") * info.num_cores + lax.axis_index("core")
        bufs = ((buf_a, sem_a), (buf_b, sem_b))

        def load(j, b):
            off = pl.multiple_of(worker * per_worker + j * SC_CHUNK, SC_CHUNK)
            return pltpu.make_async_copy(rows_hbm.at[pl.ds(off, SC_CHUNK)], bufs[b][0], bufs[b][1])

        pltpu.sync_copy(idx_hbm.at[worker], idx_v)
        load(0, 0).start()

        @pl.loop(0, n_chunks, step=2)
        def _(j0):
            for b in range(2):
                j = j0 + b
                load(j, b).wait()

                @pl.when(j + 1 < n_chunks)
                def _():
                    load(j + 1, 1 - b).start()

                pltpu.sync_copy(bufs[b][0], out_hbm.at[idx_v.at[j]])

    return scatter(rows, idx.reshape(n_workers, n_chunks, SC_CHUNK))


def _moe_kernel(step_ids, e_lo, e_hi, n_valid, n_tiles, x_ref, rec_ref, nw_ref, *refs):
    del step_ids, e_lo, e_hi
    s = pl.program_id(0)
    o_ref = refs[-1]
    lt = LANE_TILES

    @pl.when(s * MOE_STEP_TILES < n_tiles[0])
    def _():
        for i in range(MOE_STEP_TILES):
            g_lo_ref, g_hi_ref, u_lo_ref, u_hi_ref, d_lo_ref, d_hi_ref = refs[6 * i:6 * (i + 1)]
            x_tile = x_ref.at[pl.ds(i * MOE_TILE * lt, MOE_TILE * lt)]
            valid = lax.broadcasted_iota(jnp.int32, (MOE_TILE, 1), 0) < n_valid[s * MOE_STEP_TILES + i]
            x = jnp.where(valid, _load_tokens(x_tile), 0.0)
            h = _rms(x, nw_ref[...]).astype(BF16)
            rec = jnp.where(valid, rec_ref[pl.ds(i * MOE_TILE, MOE_TILE), :], 0.0)

            def expert(g_ref, u_ref, d_ref):
                act = _silu(_dot(h, g_ref[0])) * _dot(h, u_ref[0])
                return _dot(act.astype(BF16), d_ref[0])

            y = (rec[:, 2:3] * expert(g_lo_ref, u_lo_ref, d_lo_ref)
                 + rec[:, 3:4] * expert(g_hi_ref, u_hi_ref, d_hi_ref))
            _store_tokens(o_ref.at[pl.ds(i * MOE_TILE * lt, MOE_TILE * lt)],
                          jnp.where(valid, _load_tokens(x_tile), 0.0) + y)


def _moe(x2, n, meta, rec, cnt, norm_w, w_gate, w_up, w_down, layer):
    d = D_MODEL
    lt = LANE_TILES
    tm = MOE_TILE
    max_tiles = n // tm + N_BUCKETS
    n_slots = max_tiles * tm
    counts = cnt[:N_BUCKETS, 0].astype(jnp.int32)
    tiles_per_bucket = (counts + tm - 1) // tm
    tile_end = jnp.cumsum(tiles_per_bucket)
    tile_start = tile_end - tiles_per_bucket
    n_tiles = tile_end[-1]
    in_bucket = meta[0][:, None] == jnp.arange(N_BUCKETS, dtype=F32)[None, :]
    dest = meta[1].astype(jnp.int32) + jnp.sum(jnp.where(in_bucket, tile_start[None, :] * tm, 0), axis=1)
    x_sorted = _scatter_tokens(x2.reshape(n, lt, LANE), dest, n_slots).reshape(n_slots * lt, LANE)
    rec_sorted = _scatter_tokens(rec, dest, n_slots)

    k = MOE_STEP_TILES
    tile_raw = jnp.arange(max_tiles, dtype=jnp.int32)
    tile_ids = jnp.minimum(tile_raw, n_tiles - 1)
    step_ids = jnp.minimum(jnp.arange(max_tiles // k, dtype=jnp.int32), (n_tiles - 1) // k)
    of_bucket = lambda table: jnp.sum(jnp.where(tile_bucket[:, None] == jnp.arange(N_BUCKETS)[None, :], table[None, :], 0), axis=1)
    tile_bucket = jnp.sum((tile_ids[:, None] >= tile_end[None, :]).astype(jnp.int32), axis=1)
    n_valid = jnp.clip(of_bucket(counts) - (tile_ids - of_bucket(tile_start)) * tm, 0, tm)
    n_valid = jnp.where(tile_raw < n_tiles, n_valid, 0)
    group = tile_bucket // N_PAIRS
    pair = tile_bucket % N_PAIRS
    pair_lo = jnp.sum(jnp.where(pair[:, None] == jnp.arange(N_PAIRS)[None, :], jnp.asarray(PAIR_LO, jnp.int32)[None, :], 0), axis=1)
    pair_hi = jnp.sum(jnp.where(pair[:, None] == jnp.arange(N_PAIRS)[None, :], jnp.asarray(PAIR_HI, jnp.int32)[None, :], 0), axis=1)
    e_lo = layer * N_EXPERTS + group * EXPERTS_PER_GROUP + pair_lo
    e_hi = layer * N_EXPERTS + group * EXPERTS_PER_GROUP + pair_hi

    step_rows = lambda s, sid, lo, hi, nv, nt: (sid[s], 0)
    weight_specs, weight_args = [], []
    for i in range(k):
        wlo = lambda s, sid, lo, hi, nv, nt, i=i: (lo[s * k + i], 0, 0)
        whi = lambda s, sid, lo, hi, nv, nt, i=i: (hi[s * k + i], 0, 0)
        weight_specs += [pl.BlockSpec((1, d, D_EXPERT), wlo), pl.BlockSpec((1, d, D_EXPERT), whi),
                         pl.BlockSpec((1, d, D_EXPERT), wlo), pl.BlockSpec((1, d, D_EXPERT), whi),
                         pl.BlockSpec((1, D_EXPERT, d), wlo), pl.BlockSpec((1, D_EXPERT, d), whi)]
        weight_args += [w_gate, w_gate, w_up, w_up, w_down, w_down]
    grid_spec = pltpu.PrefetchScalarGridSpec(
        num_scalar_prefetch=5,
        grid=(max_tiles // k,),
        in_specs=[pl.BlockSpec((k * tm * lt, LANE), step_rows), pl.BlockSpec((k * tm, LANE), step_rows),
                  pl.BlockSpec((1, d), lambda s, sid, lo, hi, nv, nt: (0, 0))] + weight_specs,
        out_specs=pl.BlockSpec((k * tm * lt, LANE), step_rows))
    out_sorted = pl.pallas_call(
        _moe_kernel,
        grid_spec=grid_spec,
        out_shape=jax.ShapeDtypeStruct((n_slots * lt, LANE), F32),
        compiler_params=_params("arbitrary"),
        name="moe",
    )(step_ids, e_lo, e_hi, n_valid, n_tiles.reshape(1), x_sorted, rec_sorted, norm_w.reshape(1, d), *weight_args)
    return _gather_tokens(out_sorted.reshape(n_slots, lt, LANE), dest).reshape(n * lt, LANE)


def kernel(x, attn_norm, ffn_norm, even_w_in, even_w_out, hg_lb_logits, hg_o_norm, da_q_norm, da_k_norm, da_lambda, da_subln, odd_w_in, gla_w_alpha, gla_b_alpha, gla_o_norm, odd_w_out, moe_w_group, moe_b_group, moe_w_router, moe_b_router, moe_w_gate, moe_w_up, moe_w_down):
    batch, seq, d = x.shape
    depth = attn_norm.shape[0]
    n = batch * seq
    xf = x.reshape(n, d)

    lb_sm = jax.nn.softmax(hg_lb_logits.astype(F32), axis=0)
    lb_all = jnp.cumsum(lb_sm, axis=0) - lb_sm[0:1]
    rank = gla_w_alpha.shape[1]

    experts = lambda w: _to_bf16(w.reshape((-1,) + w.shape[2:]))
    even_w_in, even_w_out, odd_w_out = _to_bf16(even_w_in), _to_bf16(even_w_out), _to_bf16(odd_w_out)
    odd_w_in = _to_bf16(jnp.swapaxes(odd_w_in, 1, 2), odd_w_in.shape[2] + GLA_RANK_PAD - rank)
    moe_w_gate, moe_w_up, moe_w_down = experts(moe_w_gate), experts(moe_w_up), experts(moe_w_down)

    for layer in range(depth):
        j = layer // 2
        if layer % 2 == 0:
            lam_init = 0.8 - 0.6 * math.exp(-0.3 * layer)
            proj = _inproj(xf, n, attn_norm[layer], even_w_in[j])
            o_a = _hgrn(proj, lb_all[j], hg_o_norm[j], batch, seq)
            o_b = _diff_attn(proj, da_q_norm[j], da_k_norm[j], da_lambda[j], da_subln[j], lam_init, batch, seq)
            mix = [o_a, o_b]
            w_out = even_w_out[j]
        else:
            w_alpha = jnp.pad(gla_w_alpha[j], ((0, GLA_RANK_PAD - rank), (0, 0))).astype(BF16)
            proj = _inproj(xf, n, attn_norm[layer], odd_w_in[j], w_transposed=True)
            mix = [_gla(proj, w_alpha, gla_b_alpha[j], gla_o_norm[j], batch, seq)]
            w_out = odd_w_out[j]
        x2, meta, rec, cnt = _outproj_router(mix, w_out, xf, n, ffn_norm[layer], moe_w_group[layer],
                                        moe_b_group[layer], moe_w_router[layer], moe_b_router[layer])
        xf = _moe(x2, n, meta, rec, cnt, ffn_norm[layer], moe_w_gate, moe_w_up, moe_w_down, layer)
    return xf.reshape(batch, seq, d)
```

```python
import functools
import math

import numpy as np
import jax
import jax.numpy as jnp
from jax import lax
from jax.experimental import pallas as pl
from jax.experimental.pallas import tpu as pltpu
from jax.experimental.pallas import tpu_sc as plsc

F32 = jnp.float32
BF16 = jnp.bfloat16
EPS = 1e-6
LOG2_E = 1.4426950408889634

D_MODEL = 1024
LANE = 128
LANE_TILES = D_MODEL // LANE
HEADS = 4
DK = 128
HG_WIDTH = 512
DA_HEAD_DIM = 64
GLA_DV = 256
GLA_RANK_PAD = 128
GLA_TAU = 16.0
N_GROUPS = 4
EXPERTS_PER_GROUP = 4
N_EXPERTS = 16
D_EXPERT = 512
N_PAIRS = 6
N_BUCKETS = N_GROUPS * N_PAIRS
ROUTER_ROWS = 32

CHUNK = 128
GLA_STEP_CHUNKS = 4
LEVELS = 7
ATTN_BLOCK = 512
ATTN_HEADS = 4
SCORE_BOUND_MARGIN = 1.05
MAX_SCORE_BOUND = 60.0
ROW_TILE = 512
MOE_TILE = 256
MOE_STEP_TILES = 2
SC_CHUNK = 32
VMEM_LIMIT = 56 * 1024 * 1024
CAST_BLOCK_BYTES = 8 * 1024 * 1024

PAIR_LO = (0, 0, 0, 1, 1, 2)
PAIR_HI = (1, 2, 3, 2, 3, 3)


def _dot(a, b):
    return jnp.dot(a, b, preferred_element_type=F32)


def _dot_nt(a, b):
    return lax.dot_general(a, b, (((1,), (1,)), ((), ())), preferred_element_type=F32)


def _dot_tn(a, b):
    return lax.dot_general(a, b, (((0,), (0,)), ((), ())), preferred_element_type=F32)


def _rms(x, w):
    return x * lax.rsqrt(jnp.mean(x * x, axis=-1, keepdims=True) + EPS) * w


def _silu(x):
    return x * jax.nn.sigmoid(x)


def _params(*sem):
    return pltpu.CompilerParams(dimension_semantics=sem, vmem_limit_bytes=VMEM_LIMIT)


def _load_tokens(ref):
    if ref.shape[1] == D_MODEL:
        return ref[...]
    rows = ref.shape[0] // LANE_TILES
    return jnp.concatenate([ref[pl.ds(c, rows, stride=LANE_TILES), :] for c in range(LANE_TILES)], axis=1)


def _token_spec(x, rows):
    shape = (rows, D_MODEL) if x.shape[1] == D_MODEL else (rows * LANE_TILES, LANE)
    return pl.BlockSpec(shape, lambda i: (i, 0))


def _store_tokens(ref, val):
    if ref.shape[1] == D_MODEL:
        ref[...] = val
        return
    rows = ref.shape[0] // LANE_TILES
    for c in range(LANE_TILES):
        ref[pl.ds(c, rows, stride=LANE_TILES), :] = val[:, c * LANE:(c + 1) * LANE]


def _cast_kernel(w_ref, o_ref):
    k = w_ref.shape[1]
    o_ref[:, :k, :] = w_ref[...].astype(BF16)
    if o_ref.shape[1] > k:
        o_ref[:, k:, :] = jnp.zeros((o_ref.shape[0], o_ref.shape[1] - k, o_ref.shape[2]), BF16)


def _to_bf16(w, k_pad=None):
    l, k, n = w.shape
    k_pad = k_pad or k
    per_matrix = k * n * 4
    lb = max([c for c in range(1, l + 1) if l % c == 0 and c * per_matrix <= CAST_BLOCK_BYTES] or [1])
    nb = n
    while lb == 1 and k * nb * 4 > CAST_BLOCK_BYTES and nb % (2 * LANE) == 0:
        nb //= 2
    return pl.pallas_call(
        _cast_kernel,
        grid=(l // lb, n // nb),
        in_specs=[pl.BlockSpec((lb, k, nb), lambda i, j: (i, 0, j))],
        out_specs=pl.BlockSpec((lb, k_pad, nb), lambda i, j: (i, 0, j)),
        out_shape=jax.ShapeDtypeStruct((l, k_pad, n), BF16),
        compiler_params=_params("parallel", "parallel"),
        name="to_bf16",
    )(w)


def _inproj_kernel(x_ref, nw_ref, w_ref, o_ref, *, w_transposed):
    h = _rms(_load_tokens(x_ref), nw_ref[...]).astype(BF16)
    o_ref[...] = _dot_nt(h, w_ref[...]) if w_transposed else _dot(h, w_ref[...])


def _inproj(x, n, norm_w, w, w_transposed=False):
    d = D_MODEL
    n_out = w.shape[0] if w_transposed else w.shape[1]
    return pl.pallas_call(
        functools.partial(_inproj_kernel, w_transposed=w_transposed),
        grid=(n // ROW_TILE,),
        in_specs=[_token_spec(x, ROW_TILE),
                  pl.BlockSpec((1, d), lambda i: (0, 0)),
                  pl.BlockSpec(w.shape, lambda i: (0, 0))],
        out_specs=pl.BlockSpec((ROW_TILE, n_out), lambda i: (i, 0)),
        out_shape=jax.ShapeDtypeStruct((n, n_out), F32),
        compiler_params=_params("parallel"),
        name="inproj",
    )(x, norm_w.reshape(1, d), w)


def _decay_tables():
    c = CHUNK
    a = np.zeros((LEVELS + 2, c, c), np.float32)
    t = np.arange(c)
    for l in range(LEVELS):
        s = 1 << l
        for r in range(c):
            mid = (r // (2 * s)) * 2 * s + s - 1
            if (r >> l) & 1:
                a[l, r] = (t > mid) & (t <= r)
            else:
                a[l, r] = (t > r) & (t <= mid)
    a[LEVELS] = t[None, :] <= t[:, None]
    a[LEVELS + 1] = t[None, :] > t[:, None]
    a = a.reshape((LEVELS + 2) * c, c)
    a3 = np.concatenate([a, a], axis=1)
    i, j = np.meshgrid(t, t, indexing="ij")
    x = np.maximum(i ^ j, 1)
    lvl = np.where(j < i, np.floor(np.log2(x)).astype(np.int32), np.where(i == j, LEVELS, -1))
    return jnp.asarray(a3, BF16), jnp.asarray(lvl, jnp.int32)


def _decay_factors(a3, g):
    g_hi = g.astype(BF16)
    g_lo = (g - g_hi.astype(F32)).astype(BF16)
    return jnp.exp(_dot(a3, jnp.concatenate([g_hi, g_lo], axis=0)))


def _gla_chunk(q, k, v, f, st_ref, lvl, rowid):
    c = CHUNK
    sc = jnp.where(lvl == LEVELS, jnp.sum(q * k, axis=-1, keepdims=True), 0.0)
    for l in range(LEVELS):
        is_query = ((rowid >> l) & 1) == 1
        m = (jnp.where(is_query, q, k) * f[l * c:(l + 1) * c]).astype(BF16)
        sc = jnp.where(lvl == l, _dot_nt(m, m), sc)
    qd = (q * f[LEVELS * c:(LEVELS + 1) * c]).astype(BF16)
    kd = (k * f[(LEVELS + 1) * c:(LEVELS + 2) * c]).astype(BF16)
    vb = v.astype(BF16)
    st = st_ref[...]
    o = _dot(sc.astype(BF16), vb) + _dot_nt(qd, st.astype(BF16))
    chunk_decay = f[(LEVELS + 1) * c - 1:(LEVELS + 1) * c, :]
    st_ref[...] = st * chunk_decay + _dot_tn(vb, kd)
    return o


def _hgrn_kernel(q_ref, f_ref, i_ref, g_ref, lb_ref, on_ref, a3_ref, lvl_ref, o_ref, st_ref):
    @pl.when(pl.program_id(1) == 0)
    def _():
        st_ref[...] = jnp.zeros_like(st_ref)

    lvl = lvl_ref[...]
    rowid = lax.broadcasted_iota(jnp.int32, (CHUNK, 1), 0)
    lb = lb_ref[...]
    for c in range(GLA_STEP_CHUNKS):
        rows = slice(c * CHUNK, (c + 1) * CHUNK)
        fl = f_ref[rows, :]
        f = _decay_factors(a3_ref[...], jnp.log(lb + (1.0 - lb) * jax.nn.sigmoid(fl)))
        key_in = (1.0 - lb) * jax.nn.sigmoid(-fl)
        for h in range(HEADS):
            sl = slice(h * DK, (h + 1) * DK)
            q = _silu(q_ref[rows, sl]) * (DK ** -0.5)
            o = _gla_chunk(q, key_in[:, sl], i_ref[rows, sl], f[:, sl], st_ref.at[h], lvl, rowid)
            o = _rms(o, on_ref[...])
            o_ref[rows, sl] = (o * _silu(g_ref[rows, sl])).astype(BF16)


def _hgrn(proj, lb, o_norm, batch, seq):
    n = proj.shape[0]
    rows = CHUNK * GLA_STEP_CHUNKS
    nt = seq // rows
    a3, lvl = _decay_tables()
    col = lambda j: pl.BlockSpec((rows, HG_WIDTH), lambda b, t: (b * nt + t, j))
    const = lambda shape: pl.BlockSpec(shape, lambda b, t: (0, 0))
    return pl.pallas_call(
        _hgrn_kernel,
        grid=(batch, nt),
        in_specs=[col(0), col(1), col(2), col(3), const((1, HG_WIDTH)), const((1, DK)),
                  const(a3.shape), const(lvl.shape)],
        out_specs=pl.BlockSpec((rows, HG_WIDTH), lambda b, t: (b * nt + t, 0)),
        out_shape=jax.ShapeDtypeStruct((n, HG_WIDTH), BF16),
        scratch_shapes=[pltpu.VMEM((HEADS, DK, DK), F32)],
        compiler_params=_params("parallel", "arbitrary"),
        name="hgrn",
    )(proj, proj, proj, proj, lb.reshape(1, HG_WIDTH), o_norm.reshape(1, DK), a3, lvl)


def _log_sigmoid(x):
    return jnp.minimum(x, 0.0) - jnp.log1p(jnp.exp(-jnp.abs(x)))


def _gla_kernel(q_ref, k_ref, v_ref, g_ref, a_ref, wa_ref, ba_ref, on_ref, a3_ref, lvl_ref, o_ref, st_ref):
    @pl.when(pl.program_id(1) == 0)
    def _():
        st_ref[...] = jnp.zeros_like(st_ref)

    lvl = lvl_ref[...]
    rowid = lax.broadcasted_iota(jnp.int32, (CHUNK, 1), 0)
    for c in range(GLA_STEP_CHUNKS):
        rows = slice(c * CHUNK, (c + 1) * CHUNK)
        gate_logit = _dot(a_ref[rows, :].astype(BF16), wa_ref[...]) + ba_ref[...]
        f = _decay_factors(a3_ref[...], _log_sigmoid(gate_logit) * (1.0 / GLA_TAU))
        for h in range(HEADS):
            sl = slice(h * DK, (h + 1) * DK)
            sv = slice(h * GLA_DV, (h + 1) * GLA_DV)
            q = q_ref[rows, sl] * (DK ** -0.5)
            o = _gla_chunk(q, k_ref[rows, sl], v_ref[rows, sv], f[:, sl], st_ref.at[h], lvl, rowid)
            o = _rms(o, on_ref[...])
            o_ref[rows, sv] = (o * _silu(g_ref[rows, sv])).astype(BF16)


def _gla(proj, w_alpha, b_alpha, o_norm, batch, seq):
    n = proj.shape[0]
    rows = CHUNK * GLA_STEP_CHUNKS
    nt = seq // rows
    qk_w = HEADS * DK
    v_w = HEADS * GLA_DV
    a3, lvl = _decay_tables()
    row = lambda b, t: b * nt + t
    const = lambda shape: pl.BlockSpec(shape, lambda b, t: (0, 0))
    return pl.pallas_call(
        _gla_kernel,
        grid=(batch, nt),
        in_specs=[pl.BlockSpec((rows, qk_w), lambda b, t: (row(b, t), 0)),
                  pl.BlockSpec((rows, qk_w), lambda b, t: (row(b, t), 1)),
                  pl.BlockSpec((rows, v_w), lambda b, t: (row(b, t), 1)),
                  pl.BlockSpec((rows, v_w), lambda b, t: (row(b, t), 2)),
                  pl.BlockSpec((rows, GLA_RANK_PAD), lambda b, t: (row(b, t), (2 * qk_w + 2 * v_w) // GLA_RANK_PAD)),
                  const((GLA_RANK_PAD, qk_w)), const((1, qk_w)), const((1, GLA_DV)),
                  const(a3.shape), const(lvl.shape)],
        out_specs=pl.BlockSpec((rows, v_w), lambda b, t: (row(b, t), 0)),
        out_shape=jax.ShapeDtypeStruct((n, v_w), BF16),
        scratch_shapes=[pltpu.VMEM((HEADS, GLA_DV, DK), F32)],
        compiler_params=_params("parallel", "arbitrary"),
        name="gla",
    )(proj, proj, proj, proj, proj, w_alpha, b_alpha.reshape(1, qk_w), o_norm.reshape(1, GLA_DV), a3, lvl)


def _qk_norm(x, w2):
    lane = lax.broadcasted_iota(jnp.int32, x.shape, 1)
    first = lane < DA_HEAD_DIM
    sq = x * x
    ms1 = jnp.sum(jnp.where(first, sq, 0.0), axis=-1, keepdims=True) * (1.0 / DA_HEAD_DIM)
    ms2 = jnp.sum(jnp.where(first, 0.0, sq), axis=-1, keepdims=True) * (1.0 / DA_HEAD_DIM)
    r = jnp.where(first, lax.rsqrt(ms1 + EPS), lax.rsqrt(ms2 + EPS))
    return x * r * w2, first


def _score_bound(q_norm_w, k_norm_w):
    return (jnp.max(jnp.abs(q_norm_w)) * jnp.max(jnp.abs(k_norm_w))
            * (DA_HEAD_DIM * DA_HEAD_DIM ** -0.5 * LOG2_E * SCORE_BOUND_MARGIN))


def _attn_kernel(qi_tab, ki_tab, q_ref, k_ref, v_ref, qn_ref, kn_ref, lam_ref, sub_ref, o_ref,
                 qs_ref, m_ref, l_ref, acc_ref, *, lam_init, bounded):
    p = pl.program_id(2)
    qi = qi_tab[p]
    ki = ki_tab[p]
    tq = ATTN_BLOCK
    dv = 2 * DA_HEAD_DIM
    heads = [slice(h * dv, (h + 1) * dv) for h in range(ATTN_HEADS)]

    @pl.when(ki == 0)
    def _():
        for h, sl in enumerate(heads):
            qn, first = _qk_norm(q_ref[:, sl], qn_ref[...])
            qn = qn * (DA_HEAD_DIM ** -0.5 * LOG2_E)
            q1 = jnp.where(first, qn, 0.0).T.astype(BF16)
            q2 = jnp.where(first, 0.0, qn).T.astype(BF16)
            for copy in range(1 if bounded else 2):
                qs_ref[h, copy, :, :tq] = q1
                qs_ref[h, copy, :, tq:] = q2
            if bounded:
                bound = _score_bound(qn_ref[...], kn_ref[...])
                row = lax.broadcasted_iota(jnp.int32, (dv, 2 * tq), 0)
                qs_ref[h, 1] = jnp.where(row == 0, -bound, 0.0).astype(BF16)
        if not bounded:
            m_ref[...] = jnp.full_like(m_ref, -jnp.inf)
        l_ref[...] = jnp.zeros_like(l_ref)
        acc_ref[...] = jnp.zeros_like(acc_ref)

    def step(masked):
        tk = k_ref.shape[0]
        ones = jnp.ones((16, tk), BF16)
        unit = jnp.where(lax.broadcasted_iota(jnp.int32, (tk, dv), 1) == 0, 1.0, 0.0).astype(BF16)
        for h, sl in enumerate(heads):
            kn, _ = _qk_norm(k_ref[:, sl], kn_ref[...])
            kn = kn.astype(BF16)

            def mask(s):
                if not masked:
                    return s
                key = lax.broadcasted_iota(jnp.int32, s.shape, 0)
                qry = lax.broadcasted_iota(jnp.int32, s.shape, 1)
                qry = jnp.where(qry >= tq, qry - tq, qry)
                return jnp.where(key <= qry, s, -jnp.inf)

            if bounded:
                shifted = mask(_dot(jnp.concatenate([kn, unit], axis=1), qs_ref[h].reshape(2 * dv, 2 * tq)))
                alpha = 1.0
            else:
                m_prev = m_ref[h]
                m_new = jnp.maximum(m_prev, jnp.max(mask(_dot(kn, qs_ref[h, 0])), axis=0, keepdims=True))
                alpha = jnp.exp2(m_prev - m_new)
                shifted = mask(_dot(kn, qs_ref[h, 1])) - m_new
                m_ref[h] = m_new
            pr = jnp.exp2(shifted).astype(BF16) if bounded else jnp.exp2(shifted.astype(BF16))
            vt = jnp.concatenate([v_ref[:, sl].T.astype(BF16), ones], axis=0)
            pv = _dot(vt, pr)
            l_ref[h] = alpha * l_ref[h] + pv[dv:dv + 1]
            acc_ref[h] = alpha * acc_ref[h] + pv[:dv]

    @pl.when(ki < qi)
    def _():
        step(False)

    @pl.when(ki == qi)
    def _():
        step(True)
        lv = lam_ref[...]
        lam = (jnp.exp(jnp.sum(lv[0:1] * lv[1:2], axis=-1, keepdims=True))
               - jnp.exp(jnp.sum(lv[2:3] * lv[3:4], axis=-1, keepdims=True)) + lam_init)
        for h, sl in enumerate(heads):
            o1 = acc_ref[h, :, :tq] / l_ref[h, :, :tq]
            o2 = acc_ref[h, :, tq:] / l_ref[h, :, tq:]
            o = o1 - lam * o2
            o = o * lax.rsqrt(jnp.mean(o * o, axis=0, keepdims=True) + EPS) * sub_ref[...] * (1.0 - lam_init)
            o_ref[:, sl] = o.T.astype(BF16)


def _diff_attn(proj, q_norm, k_norm, lam_vecs, subln, lam_init, batch, seq):
    n = proj.shape[0]
    blk = ATTN_BLOCK
    nq = seq // blk
    dv = 2 * DA_HEAD_DIM
    pairs = [(qi, ki) for qi in range(nq) for ki in range(qi + 1)]
    qi_tab = jnp.asarray([p[0] for p in pairs], jnp.int32)
    ki_tab = jnp.asarray([p[1] for p in pairs], jnp.int32)
    width = ATTN_HEADS * dv
    groups = HEADS // ATTN_HEADS
    q_col = (4 * HG_WIDTH) // width
    const = lambda shape: pl.BlockSpec(shape, lambda b, g, p, qt, kt: (0, 0))
    grid_spec = pltpu.PrefetchScalarGridSpec(
        num_scalar_prefetch=2,
        grid=(batch, groups, len(pairs)),
        in_specs=[pl.BlockSpec((blk, width), lambda b, g, p, qt, kt: (b * nq + qt[p], q_col + g)),
                  pl.BlockSpec((blk, width), lambda b, g, p, qt, kt: (b * nq + kt[p], q_col + groups + g)),
                  pl.BlockSpec((blk, width), lambda b, g, p, qt, kt: (b * nq + kt[p], q_col + 2 * groups + g)),
                  const((1, dv)), const((1, dv)), const((4, DA_HEAD_DIM)), const((dv, 1))],
        out_specs=pl.BlockSpec((blk, width), lambda b, g, p, qt, kt: (b * nq + qt[p], g)),
        scratch_shapes=[pltpu.VMEM((ATTN_HEADS, 2, dv, 2 * blk), BF16),
                        pltpu.VMEM((ATTN_HEADS, 1, 2 * blk), F32),
                        pltpu.VMEM((ATTN_HEADS, 1, 2 * blk), F32),
                        pltpu.VMEM((ATTN_HEADS, dv, 2 * blk), F32)])
    args = (qi_tab, ki_tab, proj, proj, proj, jnp.tile(q_norm, 2).reshape(1, -1), jnp.tile(k_norm, 2).reshape(1, -1),
            lam_vecs, subln.reshape(-1, 1))

    def run(bounded):
        return pl.pallas_call(
            functools.partial(_attn_kernel, lam_init=lam_init, bounded=bounded),
            grid_spec=grid_spec,
            out_shape=jax.ShapeDtypeStruct((n, HEADS * dv), BF16),
            compiler_params=_params("parallel", "parallel", "arbitrary"),
            name="diff_attn_bounded" if bounded else "diff_attn",
        )(*args)

    return lax.cond(_score_bound(q_norm, k_norm) <= MAX_SCORE_BOUND, lambda: run(True), lambda: run(False))


def _outproj_router_kernel(*refs, n_a):
    a_refs = refs[:n_a]
    (w_ref, x_ref, nw_ref, wr_hi_ref, wr_lo_ref, br_ref, tri_ref,
     x2_ref, meta_ref, rec_ref, cnt_ref, carry_ref) = refs[n_a:]

    @pl.when(pl.program_id(0) == 0)
    def _():
        carry_ref[...] = jnp.zeros_like(carry_ref)

    x2 = _load_tokens(x_ref)
    off = 0
    for a_ref in a_refs:
        kk = a_ref.shape[1]
        x2 = x2 + _dot(a_ref[...], w_ref[off:off + kk, :])
        off += kk
    _store_tokens(x2_ref, x2)

    h = _rms(x2, nw_ref[...])
    h_hi = h.astype(BF16)
    h_lo = (h - h_hi.astype(F32)).astype(BF16)
    wr_hi = wr_hi_ref[...]
    by_hi = _dot_nt(jnp.concatenate([wr_hi, wr_lo_ref[...]], axis=0), h_hi)
    lt = by_hi[:ROUTER_ROWS] + by_hi[ROUTER_ROWS:] + _dot_nt(wr_hi, h_lo) + br_ref[...]
    row = lambda r: lt[r:r + 1, :]

    gl = [row(g) for g in range(N_GROUPS)]
    gmax = functools.reduce(jnp.maximum, gl)
    gsel = jnp.where(gl[0] == gmax, 0.0, jnp.where(gl[1] == gmax, 1.0, jnp.where(gl[2] == gmax, 2.0, 3.0)))
    g_w = 1.0 / functools.reduce(jnp.add, [jnp.exp(x - gmax) for x in gl])

    el = []
    for e in range(EXPERTS_PER_GROUP):
        v = row(N_GROUPS + e)
        for g in range(1, N_GROUPS):
            v = jnp.where(gsel == float(g), row(N_GROUPS + EXPERTS_PER_GROUP * g + e), v)
        el.append(v)
    emax = functools.reduce(jnp.maximum, el)
    sel = []
    for e in range(EXPERTS_PER_GROUP):
        ahead = jnp.zeros_like(emax)
        for o in range(EXPERTS_PER_GROUP):
            if o < e:
                ahead = ahead + jnp.where(el[o] >= el[e], 1.0, 0.0)
            elif o > e:
                ahead = ahead + jnp.where(el[o] > el[e], 1.0, 0.0)
        sel.append(ahead < 2.0)
    pe = [jnp.where(sel[e], jnp.exp(el[e] - emax), 0.0) for e in range(EXPERTS_PER_GROUP)]
    scale = g_w / functools.reduce(jnp.add, pe)
    zero = jnp.zeros_like(emax)
    seen = zero
    lo_idx, hi_idx, c_lo, c_hi = zero, zero, zero, zero
    for e in range(EXPERTS_PER_GROUP):
        is_lo = sel[e] & (seen == 0.0)
        is_hi = sel[e] & (seen == 1.0)
        lo_idx = jnp.where(is_lo, float(e), lo_idx)
        hi_idx = jnp.where(is_hi, float(e), hi_idx)
        c_lo = jnp.where(is_lo, pe[e] * scale, c_lo)
        c_hi = jnp.where(is_hi, pe[e] * scale, c_hi)
        seen = seen + jnp.where(sel[e], 1.0, 0.0)
    pair = lo_idx * (7.0 - lo_idx) * 0.5 + (hi_idx - lo_idx - 1.0)
    bucket = gsel * float(N_PAIRS) + pair

    bid = lax.broadcasted_iota(jnp.int32, (ROUTER_ROWS, bucket.shape[1]), 0).astype(F32)
    onehot = jnp.where(bid == bucket, 1.0, 0.0)
    before = _dot(onehot.astype(BF16), tri_ref[...]) + carry_ref[...]
    rank = jnp.sum(onehot * before, axis=0, keepdims=True)
    carry = carry_ref[...] + jnp.sum(onehot, axis=1, keepdims=True)
    carry_ref[...] = carry
    cnt_ref[...] = carry
    meta = jnp.concatenate([bucket, rank, c_lo, c_hi, zero, zero, zero, zero], axis=0)
    meta_ref[...] = meta
    rec_ref[...] = jnp.concatenate([meta, jnp.zeros((LANE - meta.shape[0], meta.shape[1]), F32)], axis=0).T


def _outproj_router(a_list, w_out, x, n, norm_w, w_group, b_group, w_router, b_router):
    d = D_MODEL
    tm = ROW_TILE
    tok = lambda: pl.BlockSpec((tm, d), lambda i: (i, 0))
    wr =jnp.concatenate([w_group, w_router.transpose(1, 0, 2).reshape(d, N_EXPERTS)], axis=1)
    wr_t = jnp.zeros((ROUTER_ROWS, d), F32).at[:wr.shape[1]].set(wr.T)
    wr_hi = wr_t.astype(BF16)
    wr_lo = (wr_t - wr_hi.astype(F32)).astype(BF16)
    br = jnp.zeros((ROUTER_ROWS, 1), F32).at[:wr.shape[1], 0].set(
        jnp.concatenate([b_group, b_router.reshape(-1)]))
    tri = jnp.asarray(np.triu(np.ones((tm, tm), np.float32), 1), BF16)
    const = lambda shape: pl.BlockSpec(shape, lambda i: (0, 0))
    in_specs = [pl.BlockSpec((tm, a.shape[1]), lambda i: (i, 0)) for a in a_list]
    in_specs += [const((d, d)), _token_spec(x, tm), const((1, d)),
                 const((ROUTER_ROWS, d)), const((ROUTER_ROWS, d)), const((ROUTER_ROWS, 1)), const((tm, tm))]
    return pl.pallas_call(
        functools.partial(_outproj_router_kernel, n_a=len(a_list)),
        grid=(n // tm,),
        in_specs=in_specs,
        out_specs=[tok(),
                   pl.BlockSpec((8, tm), lambda i: (0, i)),
                   pl.BlockSpec((tm, LANE), lambda i: (i, 0)),
                   const((ROUTER_ROWS, 1))],
        out_shape=[jax.ShapeDtypeStruct((n, d), F32),
                   jax.ShapeDtypeStruct((8, n), F32),
                   jax.ShapeDtypeStruct((n, LANE), F32),
                   jax.ShapeDtypeStruct((ROUTER_ROWS, 1), F32)],
        scratch_shapes=[pltpu.VMEM((ROUTER_ROWS, 1), F32)],
        compiler_params=_params("arbitrary"),
        name="outproj_router",
    )(*a_list, w_out, x, norm_w.reshape(1, d), wr_hi, wr_lo, br, tri)


def _gather_tokens(table, idx):
    m = idx.shape[0]
    info = plsc.get_sparse_core_info()
    n_workers = info.num_cores * info.num_subcores
    per_worker = m // n_workers
    n_chunks = per_worker // SC_CHUNK
    assert per_worker * n_workers == m and n_chunks * SC_CHUNK == per_worker and n_chunks % 2 == 0
    mesh = plsc.VectorSubcoreMesh(core_axis_name="core", subcore_axis_name="subcore")
    rows = pltpu.VMEM((SC_CHUNK,) + table.shape[1:], table.dtype)

    @functools.partial(
        pl.kernel, mesh=mesh,
        out_type=jax.ShapeDtypeStruct((m,) + table.shape[1:], table.dtype),
        scratch_types=[pltpu.VMEM((n_chunks, SC_CHUNK), jnp.int32), rows, rows,
                       pltpu.SemaphoreType.DMA, pltpu.SemaphoreType.DMA],
        compiler_params=pltpu.CompilerParams(use_tc_tiling_on_sc=True),
        name="gather_tokens")
    def gather(table_hbm, idx_hbm, out_hbm, idx_v, rows_a, rows_b, sem_a, sem_b):
        worker = lax.axis_index("subcore") * info.num_cores + lax.axis_index("core")
        bufs = ((rows_a, sem_a), (rows_b, sem_b))

        def fetch(j, buf):
            return pltpu.make_async_copy(table_hbm.at[idx_v.at[j]], bufs[buf][0], bufs[buf][1])

        pltpu.sync_copy(idx_hbm.at[worker], idx_v)
        fetch(0, 0).start()

        @pl.loop(0, n_chunks, step=2)
        def _(j0):
            for buf in range(2):
                j = j0 + buf
                fetch(j, buf).wait()

                @pl.when(j + 1 < n_chunks)
                def _():
                    fetch(j + 1, 1 - buf).start()

                off = pl.multiple_of(worker * per_worker + j * SC_CHUNK, SC_CHUNK)
                pltpu.sync_copy(bufs[buf][0], out_hbm.at[pl.ds(off, SC_CHUNK)])

    return gather(table, idx.reshape(n_workers, n_chunks, SC_CHUNK))


def _scatter_tokens(rows, idx, m):
    n = idx.shape[0]
    info = plsc.get_sparse_core_info()
    n_workers = info.num_cores * info.num_subcores
    per_worker = n // n_workers
    n_chunks = per_worker // SC_CHUNK
    assert per_worker * n_workers == n and n_chunks * SC_CHUNK == per_worker and n_chunks % 2 == 0
    mesh = plsc.VectorSubcoreMesh(core_axis_name="core", subcore_axis_name="subcore")
    buf = pltpu.VMEM((SC_CHUNK,) + rows.shape[1:], rows.dtype)

    @functools.partial(
        pl.kernel, mesh=mesh,
        out_type=jax.ShapeDtypeStruct((m,) + rows.shape[1:], rows.dtype),
        scratch_types=[pltpu.VMEM((n_chunks, SC_CHUNK), jnp.int32), buf, buf,
                       pltpu.SemaphoreType.DMA, pltpu.SemaphoreType.DMA],
        compiler_params=pltpu.CompilerParams(use_tc_tiling_on_sc=True),
        name="scatter_tokens")
    def scatter(rows_hbm, idx_hbm, out_hbm, idx_v, buf_a, buf_b, sem_a, sem_b):
        worker = lax.axis_index("subcore") * info.num_cores + lax.axis_index("core")
        bufs = ((buf_a, sem_a), (buf_b, sem_b))

        def load(j, b):
            off = pl.multiple_of(worker * per_worker + j * SC_CHUNK, SC_CHUNK)
            return pltpu.make_async_copy(rows_hbm.at[pl.ds(off, SC_CHUNK)], bufs[b][0], bufs[b][1])

        pltpu.sync_copy(idx_hbm.at[worker], idx_v)
        load(0, 0).start()

        @pl.loop(0, n_chunks, step=2)
        def _(j0):
            for b in range(2):
                j = j0 + b
                load(j, b).wait()

                @pl.when(j + 1 < n_chunks)
                def _():
                    load(j + 1, 1 - b).start()

                pltpu.sync_copy(bufs[b][0], out_hbm.at[idx_v.at[j]])

    return scatter(rows, idx.reshape(n_workers, n_chunks, SC_CHUNK))


def _moe_kernel(step_ids, e_lo, e_hi, n_valid, n_tiles, x_ref, rec_ref, nw_ref, *refs):
    del step_ids, e_lo, e_hi
    s = pl.program_id(0)
    o_ref = refs[-1]
    lt = LANE_TILES

    @pl.when(s * MOE_STEP_TILES < n_tiles[0])
    def _():
        for i in range(MOE_STEP_TILES):
            g_lo_ref, g_hi_ref, u_lo_ref, u_hi_ref, d_lo_ref, d_hi_ref = refs[6 * i:6 * (i + 1)]
            x_tile = x_ref.at[pl.ds(i * MOE_TILE, MOE_TILE)]
            valid = lax.broadcasted_iota(jnp.int32, (MOE_TILE, 1), 0) < n_valid[s * MOE_STEP_TILES + i]
            x = jnp.where(valid, _load_tokens(x_tile), 0.0)
            h = _rms(x, nw_ref[...]).astype(BF16)
            rec = jnp.where(valid, rec_ref[pl.ds(i * MOE_TILE, MOE_TILE), :], 0.0)

            def expert(g_ref, u_ref, d_ref):
                act = _silu(_dot(h, g_ref[0])) * _dot(h, u_ref[0])
                return _dot(act.astype(BF16), d_ref[0])

            y = (rec[:, 2:3] * expert(g_lo_ref, u_lo_ref, d_lo_ref)
                 + rec[:, 3:4] * expert(g_hi_ref, u_hi_ref, d_hi_ref))
            _store_tokens(o_ref.at[pl.ds(i * MOE_TILE, MOE_TILE)],
                          jnp.where(valid, _load_tokens(x_tile), 0.0) + y)


def _moe(x2, n, meta, rec, cnt, norm_w, w_gate, w_up, w_down, layer):
    d = D_MODEL
    lt = LANE_TILES
    tm = MOE_TILE
    max_tiles = n // tm + N_BUCKETS
    n_slots = max_tiles * tm
    counts = cnt[:N_BUCKETS, 0].astype(jnp.int32)
    tiles_per_bucket = (counts + tm - 1) // tm
    tile_end = jnp.cumsum(tiles_per_bucket)
    tile_start = tile_end - tiles_per_bucket
    n_tiles = tile_end[-1]
    in_bucket = meta[0][:, None] == jnp.arange(N_BUCKETS, dtype=F32)[None, :]
    dest = meta[1].astype(jnp.int32) + jnp.sum(jnp.where(in_bucket, tile_start[None, :] * tm, 0), axis=1)
    x_sorted = _scatter_tokens(x2, dest, n_slots)
    rec_sorted = _scatter_tokens(rec, dest, n_slots)

    k = MOE_STEP_TILES
    tile_raw = jnp.arange(max_tiles, dtype=jnp.int32)
    tile_ids = jnp.minimum(tile_raw, n_tiles - 1)
    step_ids = jnp.minimum(jnp.arange(max_tiles // k, dtype=jnp.int32), (n_tiles - 1) // k)
    of_bucket = lambda table: jnp.sum(jnp.where(tile_bucket[:, None] == jnp.arange(N_BUCKETS)[None, :], table[None, :], 0), axis=1)
    tile_bucket = jnp.sum((tile_ids[:, None] >= tile_end[None, :]).astype(jnp.int32), axis=1)
    n_valid = jnp.clip(of_bucket(counts) - (tile_ids - of_bucket(tile_start)) * tm, 0, tm)
    n_valid = jnp.where(tile_raw < n_tiles, n_valid, 0)
    group = tile_bucket // N_PAIRS
    pair = tile_bucket % N_PAIRS
    pair_lo = jnp.sum(jnp.where(pair[:, None] == jnp.arange(N_PAIRS)[None, :], jnp.asarray(PAIR_LO, jnp.int32)[None, :], 0), axis=1)
    pair_hi = jnp.sum(jnp.where(pair[:, None] == jnp.arange(N_PAIRS)[None, :], jnp.asarray(PAIR_HI, jnp.int32)[None, :], 0), axis=1)
    e_lo = layer * N_EXPERTS + group * EXPERTS_PER_GROUP + pair_lo
    e_hi = layer * N_EXPERTS + group * EXPERTS_PER_GROUP + pair_hi

    step_rows = lambda s, sid, lo, hi, nv, nt: (sid[s], 0)
    weight_specs, weight_args = [], []
    for i in range(k):
        wlo = lambda s, sid, lo, hi, nv, nt, i=i: (lo[s * k + i], 0, 0)
        whi = lambda s, sid, lo, hi, nv, nt, i=i: (hi[s * k + i], 0, 0)
        weight_specs += [pl.BlockSpec((1, d, D_EXPERT), wlo), pl.BlockSpec((1, d, D_EXPERT), whi),
                         pl.BlockSpec((1, d, D_EXPERT), wlo), pl.BlockSpec((1, d, D_EXPERT), whi),
                         pl.BlockSpec((1, D_EXPERT, d), wlo), pl.BlockSpec((1, D_EXPERT, d), whi)]
        weight_args += [w_gate, w_gate, w_up, w_up, w_down, w_down]
    grid_spec = pltpu.PrefetchScalarGridSpec(
        num_scalar_prefetch=5,
        grid=(max_tiles // k,),
        in_specs=[pl.BlockSpec((k * tm, d), step_rows), pl.BlockSpec((k * tm, LANE), step_rows),
                  pl.BlockSpec((1, d), lambda s, sid, lo, hi, nv, nt: (0, 0))] + weight_specs,
        out_specs=pl.BlockSpec((k * tm, d), step_rows))
    out_sorted = pl.pallas_call(
        _moe_kernel,
        grid_spec=grid_spec,
        out_shape=jax.ShapeDtypeStruct((n_slots, d), F32),
        compiler_params=_params("arbitrary"),
        name="moe",
    )(step_ids, e_lo, e_hi, n_valid, n_tiles.reshape(1), x_sorted, rec_sorted, norm_w.reshape(1, d), *weight_args)
    return _gather_tokens(out_sorted, dest)


def kernel(x, attn_norm, ffn_norm, even_w_in, even_w_out, hg_lb_logits, hg_o_norm, da_q_norm, da_k_norm, da_lambda, da_subln, odd_w_in, gla_w_alpha, gla_b_alpha, gla_o_norm, odd_w_out, moe_w_group, moe_b_group, moe_w_router, moe_b_router, moe_w_gate, moe_w_up, moe_w_down):
    batch, seq, d = x.shape
    depth = attn_norm.shape[0]
    n = batch * seq
    xf = x.reshape(n, d)

    lb_sm = jax.nn.softmax(hg_lb_logits.astype(F32), axis=0)
    lb_all = jnp.cumsum(lb_sm, axis=0) - lb_sm[0:1]
    rank = gla_w_alpha.shape[1]

    experts = lambda w: _to_bf16(w.reshape((-1,) + w.shape[2:]))
    even_w_in, even_w_out, odd_w_out = _to_bf16(even_w_in), _to_bf16(even_w_out), _to_bf16(odd_w_out)
    odd_w_in = _to_bf16(jnp.swapaxes(odd_w_in, 1, 2), odd_w_in.shape[2] + GLA_RANK_PAD - rank)
    moe_w_gate, moe_w_up, moe_w_down = experts(moe_w_gate), experts(moe_w_up), experts(moe_w_down)

    for layer in range(depth):
        j = layer // 2
        if layer % 2 == 0:
            lam_init = 0.8 - 0.6 * math.exp(-0.3 * layer)
            proj = _inproj(xf, n, attn_norm[layer], even_w_in[j])
            o_a = _hgrn(proj, lb_all[j], hg_o_norm[j], batch, seq)
            o_b = _diff_attn(proj, da_q_norm[j], da_k_norm[j], da_lambda[j], da_subln[j], lam_init, batch, seq)
            mix = [o_a, o_b]
            w_out = even_w_out[j]
        else:
            w_alpha = jnp.pad(gla_w_alpha[j], ((0, GLA_RANK_PAD - rank), (0, 0))).astype(BF16)
            proj = _inproj(xf, n, attn_norm[layer], odd_w_in[j], w_transposed=True)
            mix = [_gla(proj, w_alpha, gla_b_alpha[j], gla_o_norm[j], batch, seq)]
            w_out = odd_w_out[j]
        x2, meta, rec, cnt = _outproj_router(mix, w_out, xf, n, ffn_norm[layer], moe_w_group[layer],
                                        moe_b_group[layer], moe_w_router[layer], moe_b_router[layer])
        xf = _moe(x2, n, meta, rec, cnt, ffn_norm[layer], moe_w_gate, moe_w_up, moe_w_down, layer)
    return xf.reshape(batch, seq, d)
```

```python
import functools
import math

import numpy as np
import jax
import jax.numpy as jnp
from jax import lax
from jax.experimental import pallas as pl
from jax.experimental.pallas import tpu as pltpu
from jax.experimental.pallas import tpu_sc as plsc

F32 = jnp.float32
BF16 = jnp.bfloat16
EPS = 1e-6
LOG2_E = 1.4426950408889634

D_MODEL = 1024
LANE = 128
HEADS = 4
DK = 128
HG_WIDTH = 512
DA_HEAD_DIM = 64
GLA_DV = 256
GLA_RANK_PAD = 128
GLA_TAU = 16.0
N_GROUPS = 4
EXPERTS_PER_GROUP = 4
N_EXPERTS = 16
D_EXPERT = 512
N_PAIRS = 6
N_BUCKETS = N_GROUPS * N_PAIRS
ROUTER_ROWS = 32

CHUNK = 128
GLA_STEP_CHUNKS = 4
LEVELS = 7
ATTN_BLOCK = 512
ATTN_HEADS = 4
SCORE_BOUND_MARGIN = 1.05
MAX_SCORE_BOUND = 60.0
ROW_TILE = 512
MOE_TILE = 256
MOE_STEP_TILES = 2
SC_CHUNK = 32
VMEM_LIMIT = 56 * 1024 * 1024
CAST_BLOCK_BYTES = 8 * 1024 * 1024

PAIR_LO = (0, 0, 0, 1, 1, 2)
PAIR_HI = (1, 2, 3, 2, 3, 3)


def _dot(a, b):
    return jnp.dot(a, b, preferred_element_type=F32)


def _dot_nt(a, b):
    return lax.dot_general(a, b, (((1,), (1,)), ((), ())), preferred_element_type=F32)


def _dot_tn(a, b):
    return lax.dot_general(a, b, (((0,), (0,)), ((), ())), preferred_element_type=F32)


def _rms(x, w):
    return x * lax.rsqrt(jnp.mean(x * x, axis=-1, keepdims=True) + EPS) * w


def _silu(x):
    return x * jax.nn.sigmoid(x)


def _params(*sem):
    return pltpu.CompilerParams(dimension_semantics=sem, vmem_limit_bytes=VMEM_LIMIT)


def _cast_kernel(w_ref, o_ref):
    k = w_ref.shape[1]
    o_ref[:, :k, :] = w_ref[...].astype(BF16)
    if o_ref.shape[1] > k:
        o_ref[:, k:, :] = jnp.zeros((o_ref.shape[0], o_ref.shape[1] - k, o_ref.shape[2]), BF16)


def _to_bf16(w, k_pad=None, first=0, count=None):
    _, k, n = w.shape
    l = count or w.shape[0]
    k_pad = k_pad or k
    per_matrix = k * n * 4
    lb = max([c for c in range(1, l + 1) if l % c == 0 and first % c == 0 and c * per_matrix <= CAST_BLOCK_BYTES] or [1])
    nb = n
    while lb == 1 and k * nb * 4 > CAST_BLOCK_BYTES and nb % (2 * LANE) == 0:
        nb //= 2
    return pl.pallas_call(
        _cast_kernel,
        grid=(l // lb, n // nb),
        in_specs=[pl.BlockSpec((lb, k, nb), lambda i, j: (first // lb + i, 0, j))],
        out_specs=pl.BlockSpec((lb, k_pad, nb), lambda i, j: (i, 0, j)),
        out_shape=jax.ShapeDtypeStruct((l, k_pad, n), BF16),
        compiler_params=_params("parallel", "parallel"),
        name="to_bf16",
    )(w)


def _inproj_kernel(x_ref, nw_ref, w_ref, o_ref, *, w_transposed):
    h = _rms(x_ref[...], nw_ref[...]).astype(BF16)
    o_ref[...] = _dot_nt(h, w_ref[...]) if w_transposed else _dot(h, w_ref[...])


def _inproj(x, n, norm_w, w, w_transposed=False):
    d = D_MODEL
    n_out = w.shape[0] if w_transposed else w.shape[1]
    return pl.pallas_call(
        functools.partial(_inproj_kernel, w_transposed=w_transposed),
        grid=(n // ROW_TILE,),
        in_specs=[pl.BlockSpec((ROW_TILE, d), lambda i: (i, 0)),
                  pl.BlockSpec((1, d), lambda i: (0, 0)),
                  pl.BlockSpec(w.shape, lambda i: (0, 0))],
        out_specs=pl.BlockSpec((ROW_TILE, n_out), lambda i: (i, 0)),
        out_shape=jax.ShapeDtypeStruct((n, n_out), F32),
        compiler_params=_params("parallel"),
        name="inproj",
    )(x, norm_w.reshape(1, d), w)


def _decay_tables():
    c = CHUNK
    a = np.zeros((LEVELS + 2, c, c), np.float32)
    t = np.arange(c)
    for l in range(LEVELS):
        s = 1 << l
        for r in range(c):
            mid = (r // (2 * s)) * 2 * s + s - 1
            if (r >> l) & 1:
                a[l, r] = (t > mid) & (t <= r)
            else:
                a[l, r] = (t > r) & (t <= mid)
    a[LEVELS] = t[None, :] <= t[:, None]
    a[LEVELS + 1] = t[None, :] > t[:, None]
    a = a.reshape((LEVELS + 2) * c, c)
    a3 = np.concatenate([a, a], axis=1)
    i, j = np.meshgrid(t, t, indexing="ij")
    x = np.maximum(i ^ j, 1)
    lvl = np.where(j < i, np.floor(np.log2(x)).astype(np.int32), np.where(i == j, LEVELS, -1))
    return jnp.asarray(a3, BF16), jnp.asarray(lvl, jnp.int32)


def _decay_factors(a3, g):
    g_hi = g.astype(BF16)
    g_lo = (g - g_hi.astype(F32)).astype(BF16)
    return jnp.exp(_dot(a3, jnp.concatenate([g_hi, g_lo], axis=0)))


def _gla_chunk(q, k, v, f, st_ref, lvl, rowid):
    c = CHUNK
    sc = jnp.where(lvl == LEVELS, jnp.sum(q * k, axis=-1, keepdims=True), 0.0)
    for l in range(LEVELS):
        is_query = ((rowid >> l) & 1) == 1
        m = (jnp.where(is_query, q, k) * f[l * c:(l + 1) * c]).astype(BF16)
        sc = jnp.where(lvl == l, _dot_nt(m, m), sc)
    qd = (q * f[LEVELS * c:(LEVELS + 1) * c]).astype(BF16)
    kd = (k * f[(LEVELS + 1) * c:(LEVELS + 2) * c]).astype(BF16)
    vb = v.astype(BF16)
    st = st_ref[...]
    o = _dot(sc.astype(BF16), vb) + _dot_nt(qd, st.astype(BF16))
    chunk_decay = f[(LEVELS + 1) * c - 1:(LEVELS + 1) * c, :]
    st_ref[...] = st * chunk_decay + _dot_tn(vb, kd)
    return o


def _hgrn_kernel(q_ref, f_ref, i_ref, g_ref, lb_ref, on_ref, a3_ref, lvl_ref, o_ref, st_ref):
    @pl.when(pl.program_id(1) == 0)
    def _():
        st_ref[...] = jnp.zeros_like(st_ref)

    lvl = lvl_ref[...]
    rowid = lax.broadcasted_iota(jnp.int32, (CHUNK, 1), 0)
    lb = lb_ref[...]
    for c in range(GLA_STEP_CHUNKS):
        rows = slice(c * CHUNK, (c + 1) * CHUNK)
        fl = f_ref[rows, :]
        f = _decay_factors(a3_ref[...], jnp.log(lb + (1.0 - lb) * jax.nn.sigmoid(fl)))
        key_in = (1.0 - lb) * jax.nn.sigmoid(-fl)
        for h in range(HEADS):
            sl = slice(h * DK, (h + 1) * DK)
            q = _silu(q_ref[rows, sl]) * (DK ** -0.5)
            o = _gla_chunk(q, key_in[:, sl], i_ref[rows, sl], f[:, sl], st_ref.at[h], lvl, rowid)
            o = _rms(o, on_ref[...])
            o_ref[rows, sl] = (o * _silu(g_ref[rows, sl])).astype(BF16)


def _hgrn(proj, lb, o_norm, batch, seq):
    n = proj.shape[0]
    rows = CHUNK * GLA_STEP_CHUNKS
    nt = seq // rows
    a3, lvl = _decay_tables()
    col = lambda j: pl.BlockSpec((rows, HG_WIDTH), lambda b, t: (b * nt + t, j))
    const = lambda shape: pl.BlockSpec(shape, lambda b, t: (0, 0))
    return pl.pallas_call(
        _hgrn_kernel,
        grid=(batch, nt),
        in_specs=[col(0), col(1), col(2), col(3), const((1, HG_WIDTH)), const((1, DK)),
                  const(a3.shape), const(lvl.shape)],
        out_specs=pl.BlockSpec((rows, HG_WIDTH), lambda b, t: (b * nt + t, 0)),
        out_shape=jax.ShapeDtypeStruct((n, HG_WIDTH), BF16),
        scratch_shapes=[pltpu.VMEM((HEADS, DK, DK), F32)],
        compiler_params=_params("parallel", "arbitrary"),
        name="hgrn",
    )(proj, proj, proj, proj, lb.reshape(1, HG_WIDTH), o_norm.reshape(1, DK), a3, lvl)


def _log_sigmoid(x):
    return jnp.minimum(x, 0.0) - jnp.log1p(jnp.exp(-jnp.abs(x)))


def _gla_kernel(q_ref, k_ref, v_ref, g_ref, a_ref, wa_ref, ba_ref, on_ref, a3_ref, lvl_ref, o_ref, st_ref):
    @pl.when(pl.program_id(1) == 0)
    def _():
        st_ref[...] = jnp.zeros_like(st_ref)

    lvl = lvl_ref[...]
    rowid = lax.broadcasted_iota(jnp.int32, (CHUNK, 1), 0)
    for c in range(GLA_STEP_CHUNKS):
        rows = slice(c * CHUNK, (c + 1) * CHUNK)
        gate_logit = _dot(a_ref[rows, :].astype(BF16), wa_ref[...]) + ba_ref[...]
        f = _decay_factors(a3_ref[...], _log_sigmoid(gate_logit) * (1.0 / GLA_TAU))
        for h in range(HEADS):
            sl = slice(h * DK, (h + 1) * DK)
            sv = slice(h * GLA_DV, (h + 1) * GLA_DV)
            q = q_ref[rows, sl] * (DK ** -0.5)
            o = _gla_chunk(q, k_ref[rows, sl], v_ref[rows, sv], f[:, sl], st_ref.at[h], lvl, rowid)
            o = _rms(o, on_ref[...])
            o_ref[rows, sv] = (o * _silu(g_ref[rows, sv])).astype(BF16)


def _gla(proj, w_alpha, b_alpha, o_norm, batch, seq):
    n = proj.shape[0]
    rows = CHUNK * GLA_STEP_CHUNKS
    nt = seq // rows
    qk_w = HEADS * DK
    v_w = HEADS * GLA_DV
    a3, lvl = _decay_tables()
    row = lambda b, t: b * nt + t
    const = lambda shape: pl.BlockSpec(shape, lambda b, t: (0, 0))
    return pl.pallas_call(
        _gla_kernel,
        grid=(batch, nt),
        in_specs=[pl.BlockSpec((rows, qk_w), lambda b, t: (row(b, t), 0)),
                  pl.BlockSpec((rows, qk_w), lambda b, t: (row(b, t), 1)),
                  pl.BlockSpec((rows, v_w), lambda b, t: (row(b, t), 1)),
                  pl.BlockSpec((rows, v_w), lambda b, t: (row(b, t), 2)),
                  pl.BlockSpec((rows, GLA_RANK_PAD), lambda b, t: (row(b, t), (2 * qk_w + 2 * v_w) // GLA_RANK_PAD)),
                  const((GLA_RANK_PAD, qk_w)), const((1, qk_w)), const((1, GLA_DV)),
                  const(a3.shape), const(lvl.shape)],
        out_specs=pl.BlockSpec((rows, v_w), lambda b, t: (row(b, t), 0)),
        out_shape=jax.ShapeDtypeStruct((n, v_w), BF16),
        scratch_shapes=[pltpu.VMEM((HEADS, GLA_DV, DK), F32)],
        compiler_params=_params("parallel", "arbitrary"),
        name="gla",
    )(proj, proj, proj, proj, proj, w_alpha, b_alpha.reshape(1, qk_w), o_norm.reshape(1, GLA_DV), a3, lvl)


def _qk_norm(x, w2):
    lane = lax.broadcasted_iota(jnp.int32, x.shape, 1)
    first = lane < DA_HEAD_DIM
    sq = x * x
    ms1 = jnp.sum(jnp.where(first, sq, 0.0), axis=-1, keepdims=True) * (1.0 / DA_HEAD_DIM)
    ms2 = jnp.sum(jnp.where(first, 0.0, sq), axis=-1, keepdims=True) * (1.0 / DA_HEAD_DIM)
    r = jnp.where(first, lax.rsqrt(ms1 + EPS), lax.rsqrt(ms2 + EPS))
    return x * r * w2, first


def _score_bound(q_norm_w, k_norm_w):
    return (jnp.max(jnp.abs(q_norm_w)) * jnp.max(jnp.abs(k_norm_w))
            * (DA_HEAD_DIM * DA_HEAD_DIM ** -0.5 * LOG2_E * SCORE_BOUND_MARGIN))


def _attn_kernel(qi_tab, ki_tab, q_ref, k_ref, v_ref, qn_ref, kn_ref, lam_ref, sub_ref, o_ref,
                 qs_ref, m_ref, l_ref, acc_ref, *, lam_init, bounded):
    p = pl.program_id(2)
    qi = qi_tab[p]
    ki = ki_tab[p]
    tq = ATTN_BLOCK
    dv = 2 * DA_HEAD_DIM
    heads = [slice(h * dv, (h + 1) * dv) for h in range(ATTN_HEADS)]

    @pl.when(ki == 0)
    def _():
        for h, sl in enumerate(heads):
            qn, first = _qk_norm(q_ref[:, sl], qn_ref[...])
            qn = qn * (DA_HEAD_DIM ** -0.5 * LOG2_E)
            q1 = jnp.where(first, qn, 0.0).T.astype(BF16)
            q2 = jnp.where(first, 0.0, qn).T.astype(BF16)
            for copy in range(1 if bounded else 2):
                qs_ref[h, copy, :, :tq] = q1
                qs_ref[h, copy, :, tq:] = q2
            if bounded:
                bound = _score_bound(qn_ref[...], kn_ref[...])
                row = lax.broadcasted_iota(jnp.int32, (dv, 2 * tq), 0)
                qs_ref[h, 1] = jnp.where(row == 0, -bound, 0.0).astype(BF16)
        if not bounded:
            m_ref[...] = jnp.full_like(m_ref, -jnp.inf)
        l_ref[...] = jnp.zeros_like(l_ref)
        acc_ref[...] = jnp.zeros_like(acc_ref)

    def step(masked):
        tk = k_ref.shape[0]
        ones = jnp.ones((16, tk), BF16)
        unit = jnp.where(lax.broadcasted_iota(jnp.int32, (tk, dv), 1) == 0, 1.0, 0.0).astype(BF16)
        for h, sl in enumerate(heads):
            kn, _ = _qk_norm(k_ref[:, sl], kn_ref[...])
            kn = kn.astype(BF16)

            def mask(s):
                if not masked:
                    return s
                key = lax.broadcasted_iota(jnp.int32, s.shape, 0)
                qry = lax.broadcasted_iota(jnp.int32, s.shape, 1)
                qry = jnp.where(qry >= tq, qry - tq, qry)
                return jnp.where(key <= qry, s, -jnp.inf)

            if bounded:
                shifted = mask(_dot(jnp.concatenate([kn, unit], axis=1), qs_ref[h].reshape(2 * dv, 2 * tq)))
                alpha = 1.0
            else:
                m_prev = m_ref[h]
                m_new = jnp.maximum(m_prev, jnp.max(mask(_dot(kn, qs_ref[h, 0])), axis=0, keepdims=True))
                alpha = jnp.exp2(m_prev - m_new)
                shifted = mask(_dot(kn, qs_ref[h, 1])) - m_new
                m_ref[h] = m_new
            pr = jnp.exp2(shifted).astype(BF16) if bounded else jnp.exp2(shifted.astype(BF16))
            vt = jnp.concatenate([v_ref[:, sl].T.astype(BF16), ones], axis=0)
            pv = _dot(vt, pr)
            l_ref[h] = alpha * l_ref[h] + pv[dv:dv + 1]
            acc_ref[h] = alpha * acc_ref[h] + pv[:dv]

    @pl.when(ki < qi)
    def _():
        step(False)

    @pl.when(ki == qi)
    def _():
        step(True)
        lv = lam_ref[...]
        lam = (jnp.exp(jnp.sum(lv[0:1] * lv[1:2], axis=-1, keepdims=True))
               - jnp.exp(jnp.sum(lv[2:3] * lv[3:4], axis=-1, keepdims=True)) + lam_init)
        for h, sl in enumerate(heads):
            o1 = acc_ref[h, :, :tq] / l_ref[h, :, :tq]
            o2 = acc_ref[h, :, tq:] / l_ref[h, :, tq:]
            o = o1 - lam * o2
            o = o * lax.rsqrt(jnp.mean(o * o, axis=0, keepdims=True) + EPS) * sub_ref[...] * (1.0 - lam_init)
            o_ref[:, sl] = o.T.astype(BF16)


def _diff_attn(proj, q_norm, k_norm, lam_vecs, subln, lam_init, batch, seq):
    n = proj.shape[0]
    blk = ATTN_BLOCK
    nq = seq // blk
    dv = 2 * DA_HEAD_DIM
    pairs = [(qi, ki) for qi in range(nq) for ki in range(qi + 1)]
    qi_tab = jnp.asarray([p[0] for p in pairs], jnp.int32)
    ki_tab = jnp.asarray([p[1] for p in pairs], jnp.int32)
    width = ATTN_HEADS * dv
    groups = HEADS // ATTN_HEADS
    q_col = (4 * HG_WIDTH) // width
    const = lambda shape: pl.BlockSpec(shape, lambda b, g, p, qt, kt: (0, 0))
    grid_spec = pltpu.PrefetchScalarGridSpec(
        num_scalar_prefetch=2,
        grid=(batch, groups, len(pairs)),
        in_specs=[pl.BlockSpec((blk, width), lambda b, g, p, qt, kt: (b * nq + qt[p], q_col + g)),
                  pl.BlockSpec((blk, width), lambda b, g, p, qt, kt: (b * nq + kt[p], q_col + groups + g)),
                  pl.BlockSpec((blk, width), lambda b, g, p, qt, kt: (b * nq + kt[p], q_col + 2 * groups + g)),
                  const((1, dv)), const((1, dv)), const((4, DA_HEAD_DIM)), const((dv, 1))],
        out_specs=pl.BlockSpec((blk, width), lambda b, g, p, qt, kt: (b * nq + qt[p], g)),
        scratch_shapes=[pltpu.VMEM((ATTN_HEADS, 2, dv, 2 * blk), BF16),
                        pltpu.VMEM((ATTN_HEADS, 1, 2 * blk), F32),
                        pltpu.VMEM((ATTN_HEADS, 1, 2 * blk), F32),
                        pltpu.VMEM((ATTN_HEADS, dv, 2 * blk), F32)])
    args = (qi_tab, ki_tab, proj, proj, proj, jnp.tile(q_norm, 2).reshape(1, -1), jnp.tile(k_norm, 2).reshape(1, -1),
            lam_vecs, subln.reshape(-1, 1))

    def run(bounded):
        return pl.pallas_call(
            functools.partial(_attn_kernel, lam_init=lam_init, bounded=bounded),
            grid_spec=grid_spec,
            out_shape=jax.ShapeDtypeStruct((n, HEADS * dv), BF16),
            compiler_params=_params("parallel", "parallel", "arbitrary"),
            name="diff_attn_bounded" if bounded else "diff_attn",
        )(*args)

    return lax.cond(_score_bound(q_norm, k_norm) <= MAX_SCORE_BOUND, lambda: run(True), lambda: run(False))


def _outproj_router_kernel(*refs, n_a):
    a_refs = refs[:n_a]
    (w_ref, x_ref, nw_ref, wr_hi_ref, wr_lo_ref, br_ref, tri_ref,
     x2_ref, meta_ref, rec_ref, cnt_ref, carry_ref) = refs[n_a:]

    @pl.when(pl.program_id(0) == 0)
    def _():
        carry_ref[...] = jnp.zeros_like(carry_ref)

    x2 = x_ref[...]
    off = 0
    for a_ref in a_refs:
        kk = a_ref.shape[1]
        x2 = x2 + _dot(a_ref[...], w_ref[off:off + kk, :])
        off += kk
    x2_ref[...] = x2

    h = _rms(x2, nw_ref[...])
    h_hi = h.astype(BF16)
    h_lo = (h - h_hi.astype(F32)).astype(BF16)
    wr_hi = wr_hi_ref[...]
    by_hi = _dot_nt(jnp.concatenate([wr_hi, wr_lo_ref[...]], axis=0), h_hi)
    lt = by_hi[:ROUTER_ROWS] + by_hi[ROUTER_ROWS:] + _dot_nt(wr_hi, h_lo) + br_ref[...]
    row = lambda r: lt[r:r + 1, :]

    gl = [row(g) for g in range(N_GROUPS)]
    gmax = functools.reduce(jnp.maximum, gl)
    gsel = jnp.where(gl[0] == gmax, 0.0, jnp.where(gl[1] == gmax, 1.0, jnp.where(gl[2] == gmax, 2.0, 3.0)))
    g_w = 1.0 / functools.reduce(jnp.add, [jnp.exp(x - gmax) for x in gl])

    el = []
    for e in range(EXPERTS_PER_GROUP):
        v = row(N_GROUPS + e)
        for g in range(1, N_GROUPS):
            v = jnp.where(gsel == float(g), row(N_GROUPS + EXPERTS_PER_GROUP * g + e), v)
        el.append(v)
    emax = functools.reduce(jnp.maximum, el)
    sel = []
    for e in range(EXPERTS_PER_GROUP):
        ahead = jnp.zeros_like(emax)
        for o in range(EXPERTS_PER_GROUP):
            if o < e:
                ahead = ahead + jnp.where(el[o] >= el[e], 1.0, 0.0)
            elif o > e:
                ahead = ahead + jnp.where(el[o] > el[e], 1.0, 0.0)
        sel.append(ahead < 2.0)
    pe = [jnp.where(sel[e], jnp.exp(el[e] - emax), 0.0) for e in range(EXPERTS_PER_GROUP)]
    scale = g_w / functools.reduce(jnp.add, pe)
    zero = jnp.zeros_like(emax)
    seen = zero
    lo_idx, hi_idx, c_lo, c_hi = zero, zero, zero, zero
    for e in range(EXPERTS_PER_GROUP):
        is_lo = sel[e] & (seen == 0.0)
        is_hi = sel[e] & (seen == 1.0)
        lo_idx = jnp.where(is_lo, float(e), lo_idx)
        hi_idx = jnp.where(is_hi, float(e), hi_idx)
        c_lo = jnp.where(is_lo, pe[e] * scale, c_lo)
        c_hi = jnp.where(is_hi, pe[e] * scale, c_hi)
        seen = seen + jnp.where(sel[e], 1.0, 0.0)
    pair = lo_idx * (7.0 - lo_idx) * 0.5 + (hi_idx - lo_idx - 1.0)
    bucket = gsel * float(N_PAIRS) + pair

    bid = lax.broadcasted_iota(jnp.int32, (ROUTER_ROWS, bucket.shape[1]), 0).astype(F32)
    onehot = jnp.where(bid == bucket, 1.0, 0.0)
    before = _dot(onehot.astype(BF16), tri_ref[...]) + carry_ref[...]
    rank = jnp.sum(onehot * before, axis=0, keepdims=True)
    carry = carry_ref[...] + jnp.sum(onehot, axis=1, keepdims=True)
    carry_ref[...] = carry
    cnt_ref[...] = carry
    meta = jnp.concatenate([bucket, rank, c_lo, c_hi, zero, zero, zero, zero], axis=0)
    meta_ref[...] = meta
    rec_ref[...] = jnp.concatenate([meta, jnp.zeros((LANE - meta.shape[0], meta.shape[1]), F32)], axis=0).T


def _outproj_router(a_list, w_out, x, n, norm_w, w_group, b_group, w_router, b_router):
    d = D_MODEL
    tm = ROW_TILE
    tok = lambda: pl.BlockSpec((tm, d), lambda i: (i, 0))
    wr = jnp.concatenate([w_group, w_router.transpose(1, 0, 2).reshape(d, N_EXPERTS)], axis=1)
    wr_t = jnp.zeros((ROUTER_ROWS, d), F32).at[:wr.shape[1]].set(wr.T)
    wr_hi = wr_t.astype(BF16)
    wr_lo = (wr_t - wr_hi.astype(F32)).astype(BF16)
    br = jnp.zeros((ROUTER_ROWS, 1), F32).at[:wr.shape[1], 0].set(
        jnp.concatenate([b_group, b_router.reshape(-1)]))
    tri = jnp.asarray(np.triu(np.ones((tm, tm), np.float32), 1), BF16)
    const = lambda shape: pl.BlockSpec(shape, lambda i: (0, 0))
    in_specs = [pl.BlockSpec((tm, a.shape[1]), lambda i: (i, 0)) for a in a_list]
    in_specs += [const((d, d)), tok(), const((1, d)),
                 const((ROUTER_ROWS, d)), const((ROUTER_ROWS, d)), const((ROUTER_ROWS, 1)), const((tm, tm))]
    return pl.pallas_call(
        functools.partial(_outproj_router_kernel, n_a=len(a_list)),
        grid=(n // tm,),
        in_specs=in_specs,
        out_specs=[tok(),
                   pl.BlockSpec((8, tm), lambda i: (0, i)),
                   pl.BlockSpec((tm, LANE), lambda i: (i, 0)),
                   const((ROUTER_ROWS, 1))],
        out_shape=[jax.ShapeDtypeStruct((n, d), F32),
                   jax.ShapeDtypeStruct((8, n), F32),
                   jax.ShapeDtypeStruct((n, LANE), F32),
                   jax.ShapeDtypeStruct((ROUTER_ROWS, 1), F32)],
        scratch_shapes=[pltpu.VMEM((ROUTER_ROWS, 1), F32)],
        compiler_params=_params("arbitrary"),
        name="outproj_router",
    )(*a_list, w_out, x, norm_w.reshape(1, d), wr_hi, wr_lo, br, tri)


def _gather_tokens(table, idx):
    m = idx.shape[0]
    info = plsc.get_sparse_core_info()
    n_workers = info.num_cores * info.num_subcores
    per_worker = m // n_workers
    n_chunks = per_worker // SC_CHUNK
    assert per_worker * n_workers == m and n_chunks * SC_CHUNK == per_worker and n_chunks % 2 == 0
    mesh = plsc.VectorSubcoreMesh(core_axis_name="core", subcore_axis_name="subcore")
    rows = pltpu.VMEM((SC_CHUNK,) + table.shape[1:], table.dtype)

    @functools.partial(
        pl.kernel, mesh=mesh,
        out_type=jax.ShapeDtypeStruct((m,) + table.shape[1:], table.dtype),
        scratch_types=[pltpu.VMEM((n_chunks, SC_CHUNK), jnp.int32), rows, rows,
                       pltpu.SemaphoreType.DMA, pltpu.SemaphoreType.DMA],
        compiler_params=pltpu.CompilerParams(use_tc_tiling_on_sc=True),
        name="gather_tokens")
    def gather(table_hbm, idx_hbm, out_hbm, idx_v, rows_a, rows_b, sem_a, sem_b):
        worker = lax.axis_index("subcore") * info.num_cores + lax.axis_index("core")
        bufs = ((rows_a, sem_a), (rows_b, sem_b))

        def fetch(j, buf):
            return pltpu.make_async_copy(table_hbm.at[idx_v.at[j]], bufs[buf][0], bufs[buf][1])

        pltpu.sync_copy(idx_hbm.at[worker], idx_v)
        fetch(0, 0).start()

        @pl.loop(0, n_chunks, step=2)
        def _(j0):
            for buf in range(2):
                j = j0 + buf
                fetch(j, buf).wait()

                @pl.when(j + 1 < n_chunks)
                def _():
                    fetch(j + 1, 1 - buf).start()

                off = pl.multiple_of(worker * per_worker + j * SC_CHUNK, SC_CHUNK)
                pltpu.sync_copy(bufs[buf][0], out_hbm.at[pl.ds(off, SC_CHUNK)])

    return gather(table, idx.reshape(n_workers, n_chunks, SC_CHUNK))


def _scatter_tokens(arrays, idx, m):
    n = idx.shape[0]
    na = len(arrays)
    info = plsc.get_sparse_core_info()
    n_workers = info.num_cores * info.num_subcores
    per_worker = n // n_workers
    n_chunks = per_worker // SC_CHUNK
    assert per_worker * n_workers == n and n_chunks * SC_CHUNK == per_worker and n_chunks % 2 == 0
    mesh = plsc.VectorSubcoreMesh(core_axis_name="core", subcore_axis_name="subcore")
    bufs = [pltpu.VMEM((SC_CHUNK,) + a.shape[1:], a.dtype) for a in arrays for _ in range(2)]

    @functools.partial(
        pl.kernel, mesh=mesh,
        out_type=[jax.ShapeDtypeStruct((m,) + a.shape[1:], a.dtype) for a in arrays],
        scratch_types=[pltpu.VMEM((n_chunks, SC_CHUNK), jnp.int32)] + bufs + [pltpu.SemaphoreType.DMA] * (2 * na),
        compiler_params=pltpu.CompilerParams(use_tc_tiling_on_sc=True),
        name="scatter_tokens")
    def scatter(*refs):
        ins, idx_hbm, outs = refs[:na], refs[na], refs[na + 1:2 * na + 1]
        idx_v = refs[2 * na + 1]
        buf_v = refs[2 * na + 2:4 * na + 2]
        sem_v = refs[4 * na + 2:]
        worker = lax.axis_index("subcore") * info.num_cores + lax.axis_index("core")

        def load(a, j, b):
            off = pl.multiple_of(worker * per_worker + j * SC_CHUNK, SC_CHUNK)
            return pltpu.make_async_copy(ins[a].at[pl.ds(off, SC_CHUNK)], buf_v[2 * a + b], sem_v[2 * a + b])

        pltpu.sync_copy(idx_hbm.at[worker], idx_v)
        for a in range(na):
            load(a, 0, 0).start()

        @pl.loop(0, n_chunks, step=2)
        def _(j0):
            for b in range(2):
                j = j0 + b
                for a in range(na):
                    load(a, j, b).wait()

                @pl.when(j + 1 < n_chunks)
                def _():
                    for a in range(na):
                        load(a, j + 1, 1 - b).start()

                for a in range(na):
                    pltpu.sync_copy(buf_v[2 * a + b], outs[a].at[idx_v.at[j]])

    return scatter(*arrays, idx.reshape(n_workers, n_chunks, SC_CHUNK))


def _moe_kernel(step_ids, e_lo, e_hi, n_valid, n_tiles, x_ref, rec_ref, nw_ref, *refs):
    del step_ids, e_lo, e_hi
    s = pl.program_id(0)
    o_ref = refs[-1]

    @pl.when(s * MOE_STEP_TILES < n_tiles[0])
    def _():
        for i in range(MOE_STEP_TILES):
            g_lo_ref, g_hi_ref, u_lo_ref, u_hi_ref, d_lo_ref, d_hi_ref = refs[6 * i:6 * (i + 1)]
            rows = pl.ds(i * MOE_TILE, MOE_TILE)
            valid = lax.broadcasted_iota(jnp.int32, (MOE_TILE, 1), 0) < n_valid[s * MOE_STEP_TILES + i]
            x = jnp.where(valid, x_ref[rows, :], 0.0)
            h = _rms(x, nw_ref[...]).astype(BF16)
            rec = jnp.where(valid, rec_ref[rows, :], 0.0)

            def expert(g_ref, u_ref, d_ref):
                act = _silu(_dot(h, g_ref[0])) * _dot(h, u_ref[0])
                return _dot(act.astype(BF16), d_ref[0])

            y = (rec[:, 2:3] * expert(g_lo_ref, u_lo_ref, d_lo_ref)
                 + rec[:, 3:4] * expert(g_hi_ref, u_hi_ref, d_hi_ref))
            o_ref[rows, :] = jnp.where(valid, x_ref[rows, :], 0.0) + y


def _moe(x2, n, meta, rec, cnt, norm_w, w_gate, w_up, w_down, layer):
    d = D_MODEL
    tm = MOE_TILE
    max_tiles = n // tm + N_BUCKETS
    n_slots = max_tiles * tm
    counts = cnt[:N_BUCKETS, 0].astype(jnp.int32)
    tiles_per_bucket = (counts + tm - 1) // tm
    tile_end = jnp.cumsum(tiles_per_bucket)
    tile_start = tile_end - tiles_per_bucket
    n_tiles = tile_end[-1]
    in_bucket = meta[0][:, None] == jnp.arange(N_BUCKETS, dtype=F32)[None, :]
    dest = meta[1].astype(jnp.int32) + jnp.sum(jnp.where(in_bucket, tile_start[None, :] * tm, 0), axis=1)
    x_sorted, rec_sorted = _scatter_tokens([x2, rec], dest, n_slots)
    first = layer * N_EXPERTS
    w_gate, w_up, w_down = (_to_bf16(w, first=first, count=N_EXPERTS) for w in (w_gate, w_up, w_down))

    k = MOE_STEP_TILES
    tile_raw = jnp.arange(max_tiles, dtype=jnp.int32)
    tile_ids = jnp.minimum(tile_raw, n_tiles - 1)
    step_ids = jnp.minimum(jnp.arange(max_tiles // k, dtype=jnp.int32), (n_tiles - 1) // k)
    of_bucket = lambda table: jnp.sum(jnp.where(tile_bucket[:, None] == jnp.arange(N_BUCKETS)[None, :], table[None, :], 0), axis=1)
    tile_bucket = jnp.sum((tile_ids[:, None] >= tile_end[None, :]).astype(jnp.int32), axis=1)
    n_valid = jnp.clip(of_bucket(counts) - (tile_ids - of_bucket(tile_start)) * tm, 0, tm)
    n_valid = jnp.where(tile_raw < n_tiles, n_valid, 0)
    group = tile_bucket // N_PAIRS
    pair = tile_bucket % N_PAIRS
    pair_lo = jnp.sum(jnp.where(pair[:, None] == jnp.arange(N_PAIRS)[None, :], jnp.asarray(PAIR_LO, jnp.int32)[None, :], 0), axis=1)
    pair_hi = jnp.sum(jnp.where(pair[:, None] == jnp.arange(N_PAIRS)[None, :], jnp.asarray(PAIR_HI, jnp.int32)[None, :], 0), axis=1)
    e_lo = group * EXPERTS_PER_GROUP + pair_lo
    e_hi = group * EXPERTS_PER_GROUP + pair_hi

    step_rows = lambda s, sid, lo, hi, nv, nt: (sid[s], 0)
    weight_specs, weight_args = [], []
    for i in range(k):
        wlo = lambda s, sid, lo, hi, nv, nt, i=i: (lo[s * k + i], 0, 0)
        whi = lambda s, sid, lo, hi, nv, nt, i=i: (hi[s * k + i], 0, 0)
        weight_specs += [pl.BlockSpec((1, d, D_EXPERT), wlo), pl.BlockSpec((1, d, D_EXPERT), whi),
                         pl.BlockSpec((1, d, D_EXPERT), wlo), pl.BlockSpec((1, d, D_EXPERT), whi),
                         pl.BlockSpec((1, D_EXPERT, d), wlo), pl.BlockSpec((1, D_EXPERT, d), whi)]
        weight_args += [w_gate, w_gate, w_up, w_up, w_down, w_down]
    grid_spec = pltpu.PrefetchScalarGridSpec(
        num_scalar_prefetch=5,
        grid=(max_tiles // k,),
        in_specs=[pl.BlockSpec((k * tm, d), step_rows), pl.BlockSpec((k * tm, LANE), step_rows),
                  pl.BlockSpec((1, d), lambda s, sid, lo, hi, nv, nt: (0, 0))] + weight_specs,
        out_specs=pl.BlockSpec((k * tm, d), step_rows))
    out_sorted = pl.pallas_call(
        _moe_kernel,
        grid_spec=grid_spec,
        out_shape=jax.ShapeDtypeStruct((n_slots, d), F32),
        compiler_params=_params("arbitrary"),
        name="moe",
    )(step_ids, e_lo, e_hi, n_valid, n_tiles.reshape(1), x_sorted, rec_sorted, norm_w.reshape(1, d), *weight_args)
    return _gather_tokens(out_sorted, dest)


def kernel(x, attn_norm, ffn_norm, even_w_in, even_w_out, hg_lb_logits, hg_o_norm, da_q_norm, da_k_norm, da_lambda, da_subln, odd_w_in, gla_w_alpha, gla_b_alpha, gla_o_norm, odd_w_out, moe_w_group, moe_b_group, moe_w_router, moe_b_router, moe_w_gate, moe_w_up, moe_w_down):
    batch, seq, d = x.shape
    depth = attn_norm.shape[0]
    n = batch * seq
    xf = x.reshape(n, d)

    lb_sm = jax.nn.softmax(hg_lb_logits.astype(F32), axis=0)
    lb_all = jnp.cumsum(lb_sm, axis=0) - lb_sm[0:1]
    rank = gla_w_alpha.shape[1]

    experts = lambda w: w.reshape((-1,) + w.shape[2:])
    moe_w_gate, moe_w_up, moe_w_down = experts(moe_w_gate), experts(moe_w_up), experts(moe_w_down)

    for layer in range(depth):
        j = layer // 2
        if layer % 2 == 0:
            lam_init = 0.8 - 0.6 * math.exp(-0.3 * layer)
            proj = _inproj(xf, n, attn_norm[layer], _to_bf16(even_w_in, first=j, count=1)[0])
            o_a = _hgrn(proj, lb_all[j], hg_o_norm[j], batch, seq)
            o_b = _diff_attn(proj, da_q_norm[j], da_k_norm[j], da_lambda[j], da_subln[j], lam_init, batch, seq)
            mix = [o_a, o_b]
            w_out = _to_bf16(even_w_out, first=j, count=1)[0]
        else:
            w_alpha = jnp.pad(gla_w_alpha[j], ((0, GLA_RANK_PAD - rank), (0, 0))).astype(BF16)
            w_in = _to_bf16(jnp.swapaxes(odd_w_in, 1, 2), odd_w_in.shape[2] + GLA_RANK_PAD - rank, first=j, count=1)[0]
            proj = _inproj(xf, n, attn_norm[layer], w_in, w_transposed=True)
            mix = [_gla(proj, w_alpha, gla_b_alpha[j], gla_o_norm[j], batch, seq)]
            w_out = _to_bf16(odd_w_out, first=j, count=1)[0]
        x2, meta, rec, cnt = _outproj_router(mix, w_out, xf, n, ffn_norm[layer], moe_w_group[layer],
                                        moe_b_group[layer], moe_w_router[layer], moe_b_router[layer])
        xf = _moe(x2, n, meta, rec, cnt, ffn_norm[layer], moe_w_gate, moe_w_up, moe_w_down, layer)
    return xf.reshape(batch, seq, d)
```

```python
import functools
import math

import numpy as np
import jax
import jax.numpy as jnp
from jax import lax
from jax.experimental import pallas as pl
from jax.experimental.pallas import tpu as pltpu
from jax.experimental.pallas import tpu_sc as plsc

F32 = jnp.float32
BF16 = jnp.bfloat16
EPS = 1e-6
LOG2_E = 1.4426950408889634

D_MODEL = 1024
LANE = 128
HEADS = 4
DK = 128
HG_WIDTH = 512
DA_HEAD_DIM = 64
GLA_DV = 256
GLA_RANK_PAD = 128
GLA_TAU = 16.0
N_GROUPS = 4
EXPERTS_PER_GROUP = 4
N_EXPERTS = 16
D_EXPERT = 512
N_PAIRS = 6
N_BUCKETS = N_GROUPS * N_PAIRS
ROUTER_ROWS = 32

CHUNK = 128
GLA_STEP_CHUNKS = 4
LEVELS = 7
ATTN_BLOCK = 512
ATTN_HEADS = 4
SCORE_BOUND_MARGIN = 1.05
MAX_SCORE_BOUND = 60.0
ROW_TILE = 512
ROUTER_TILE = 1024
MOE_TILE = 256
MOE_STEP_TILES = 2
SC_CHUNK = 32
VMEM_LIMIT = 56 * 1024 * 1024
CAST_BLOCK_BYTES = 8 * 1024 * 1024

PAIR_LO = (0, 0, 0, 1, 1, 2)
PAIR_HI = (1, 2, 3, 2, 3, 3)


def _dot(a, b):
    return jnp.dot(a, b, preferred_element_type=F32)


def _dot_nt(a, b):
    return lax.dot_general(a, b, (((1,), (1,)), ((), ())), preferred_element_type=F32)


def _dot_tn(a, b):
    return lax.dot_general(a, b, (((0,), (0,)), ((), ())), preferred_element_type=F32)


def _rms(x, w):
    return x * lax.rsqrt(jnp.mean(x * x, axis=-1, keepdims=True) + EPS) * w


def _silu(x):
    return x * jax.nn.sigmoid(x)


def _params(*sem):
    return pltpu.CompilerParams(dimension_semantics=sem, vmem_limit_bytes=VMEM_LIMIT)


def _cast_kernel(w_ref, o_ref):
    k = w_ref.shape[1]
    o_ref[:, :k, :] = w_ref[...].astype(BF16)
    if o_ref.shape[1] > k:
        o_ref[:, k:, :] = jnp.zeros((o_ref.shape[0], o_ref.shape[1] - k, o_ref.shape[2]), BF16)


def _to_bf16(w, k_pad=None, first=0, count=None):
    _, k, n = w.shape
    l = count or w.shape[0]
    k_pad = k_pad or k
    per_matrix = k * n * 4
    lb = max([c for c in range(1, l + 1) if l % c == 0 and first % c == 0 and c * per_matrix <= CAST_BLOCK_BYTES] or [1])
    nb = n
    while lb == 1 and k * nb * 4 > CAST_BLOCK_BYTES and nb % (2 * LANE) == 0:
        nb //= 2
    return pl.pallas_call(
        _cast_kernel,
        grid=(l // lb, n // nb),
        in_specs=[pl.BlockSpec((lb, k, nb), lambda i, j: (first // lb + i, 0, j))],
        out_specs=pl.BlockSpec((lb, k_pad, nb), lambda i, j: (i, 0, j)),
        out_shape=jax.ShapeDtypeStruct((l, k_pad, n), BF16),
        compiler_params=_params("parallel", "parallel"),
        name="to_bf16",
    )(w)


def _inproj_kernel(x_ref, nw_ref, w_ref, o_ref, *, w_transposed):
    h = _rms(x_ref[...], nw_ref[...]).astype(BF16)
    o_ref[...] = _dot_nt(h, w_ref[...]) if w_transposed else _dot(h, w_ref[...])


def _inproj(x, n, norm_w, w, w_transposed=False):
    d = D_MODEL
    n_out = w.shape[0] if w_transposed else w.shape[1]
    return pl.pallas_call(
        functools.partial(_inproj_kernel, w_transposed=w_transposed),
        grid=(n // ROW_TILE,),
        in_specs=[pl.BlockSpec((ROW_TILE, d), lambda i: (i, 0)),
                  pl.BlockSpec((1, d), lambda i: (0, 0)),
                  pl.BlockSpec(w.shape, lambda i: (0, 0))],
        out_specs=pl.BlockSpec((ROW_TILE, n_out), lambda i: (i, 0)),
        out_shape=jax.ShapeDtypeStruct((n, n_out), F32),
        compiler_params=_params("parallel"),
        name="inproj",
    )(x, norm_w.reshape(1, d), w)


def _decay_tables():
    c = CHUNK
    a = np.zeros((LEVELS + 2, c, c), np.float32)
    t = np.arange(c)
    for l in range(LEVELS):
        s = 1 << l
        for r in range(c):
            mid = (r // (2 * s)) * 2 * s + s - 1
            if (r >> l) & 1:
                a[l, r] = (t > mid) & (t <= r)
            else:
                a[l, r] = (t > r) & (t <= mid)
    a[LEVELS] = t[None, :] <= t[:, None]
    a[LEVELS + 1] = t[None, :] > t[:, None]
    a = a.reshape((LEVELS + 2) * c, c)
    a3 = np.concatenate([a, a], axis=1)
    i, j = np.meshgrid(t, t, indexing="ij")
    x = np.maximum(i ^ j, 1)
    lvl = np.where(j < i, np.floor(np.log2(x)).astype(np.int32), np.where(i == j, LEVELS, -1))
    return jnp.asarray(a3, BF16), jnp.asarray(lvl, jnp.int32)


def _decay_factors(a3, g):
    g_hi = g.astype(BF16)
    g_lo = (g - g_hi.astype(F32)).astype(BF16)
    return jnp.exp(_dot(a3, jnp.concatenate([g_hi, g_lo], axis=0)))


def _gla_chunk(q, k, v, f, st_ref, lvl, rowid):
    c = CHUNK
    sc = jnp.where(lvl == LEVELS, jnp.sum(q * k, axis=-1, keepdims=True), 0.0)
    for l in range(LEVELS):
        s = 1 << l
        if s % 8 == 0:
            qk = jnp.concatenate([(q if b & 1 else k)[b * s:(b + 1) * s] for b in range(c // s)], axis=0)
        else:
            qk = jnp.where(((rowid >> l) & 1) == 1, q, k)
        m = (qk * f[l * c:(l + 1) * c]).astype(BF16)
        sc = jnp.where(lvl == l, _dot_nt(m, m), sc)
    qd = (q * f[LEVELS * c:(LEVELS + 1) * c]).astype(BF16)
    kd = (k * f[(LEVELS + 1) * c:(LEVELS + 2) * c]).astype(BF16)
    vb = v.astype(BF16)
    st = st_ref[...]
    o = _dot(sc.astype(BF16), vb) + _dot_nt(qd, st.astype(BF16))
    chunk_decay = f[(LEVELS + 1) * c - 1:(LEVELS + 1) * c, :]
    st_ref[...] = st * chunk_decay + _dot_tn(vb, kd)
    return o


def _hgrn_kernel(q_ref, f_ref, i_ref, g_ref, lb_ref, on_ref, a3_ref, lvl_ref, o_ref, st_ref):
    @pl.when(pl.program_id(1) == 0)
    def _():
        st_ref[...] = jnp.zeros_like(st_ref)

    lvl = lvl_ref[...]
    rowid = lax.broadcasted_iota(jnp.int32, (CHUNK, 1), 0)
    lb = lb_ref[...]
    for c in range(GLA_STEP_CHUNKS):
        rows = slice(c * CHUNK, (c + 1) * CHUNK)
        fl = f_ref[rows, :]
        f = _decay_factors(a3_ref[...], jnp.log(lb + (1.0 - lb) * jax.nn.sigmoid(fl)))
        key_in = (1.0 - lb) * jax.nn.sigmoid(-fl)
        for h in range(HEADS):
            sl = slice(h * DK, (h + 1) * DK)
            q = _silu(q_ref[rows, sl]) * (DK ** -0.5)
            o = _gla_chunk(q, key_in[:, sl], i_ref[rows, sl], f[:, sl], st_ref.at[h], lvl, rowid)
            o = _rms(o, on_ref[...])
            o_ref[rows, sl] = (o * _silu(g_ref[rows, sl])).astype(BF16)


def _hgrn(proj, lb, o_norm, batch, seq):
    n = proj.shape[0]
    rows = CHUNK * GLA_STEP_CHUNKS
    nt = seq // rows
    a3, lvl = _decay_tables()
    col = lambda j: pl.BlockSpec((rows, HG_WIDTH), lambda b, t: (b * nt + t, j))
    const = lambda shape: pl.BlockSpec(shape, lambda b, t: (0, 0))
    return pl.pallas_call(
        _hgrn_kernel,
        grid=(batch, nt),
        in_specs=[col(0), col(1), col(2), col(3), const((1, HG_WIDTH)), const((1, DK)),
                  const(a3.shape), const(lvl.shape)],
        out_specs=pl.BlockSpec((rows, HG_WIDTH), lambda b, t: (b * nt + t, 0)),
        out_shape=jax.ShapeDtypeStruct((n, HG_WIDTH), BF16),
        scratch_shapes=[pltpu.VMEM((HEADS, DK, DK), F32)],
        compiler_params=_params("parallel", "arbitrary"),
        name="hgrn",
    )(proj, proj, proj, proj, lb.reshape(1, HG_WIDTH), o_norm.reshape(1, DK), a3, lvl)


def _log_sigmoid(x):
    return jnp.minimum(x, 0.0) - jnp.log1p(jnp.exp(-jnp.abs(x)))


def _gla_kernel(q_ref, k_ref, v_ref, g_ref, a_ref, wa_ref, ba_ref, on_ref, a3_ref, lvl_ref, o_ref, st_ref):
    @pl.when(pl.program_id(1) == 0)
    def _():
        st_ref[...] = jnp.zeros_like(st_ref)

    lvl = lvl_ref[...]
    rowid = lax.broadcasted_iota(jnp.int32, (CHUNK, 1), 0)
    for c in range(GLA_STEP_CHUNKS):
        rows = slice(c * CHUNK, (c + 1) * CHUNK)
        gate_logit = _dot(a_ref[rows, :].astype(BF16), wa_ref[...]) + ba_ref[...]
        f = _decay_factors(a3_ref[...], _log_sigmoid(gate_logit) * (1.0 / GLA_TAU))
        for h in range(HEADS):
            sl = slice(h * DK, (h + 1) * DK)
            sv = slice(h * GLA_DV, (h + 1) * GLA_DV)
            q = q_ref[rows, sl] * (DK ** -0.5)
            o = _gla_chunk(q, k_ref[rows, sl], v_ref[rows, sv], f[:, sl], st_ref.at[h], lvl, rowid)
            o = _rms(o, on_ref[...])
            o_ref[rows, sv] = (o * _silu(g_ref[rows, sv])).astype(BF16)


def _gla(proj, w_alpha, b_alpha, o_norm, batch, seq):
    n = proj.shape[0]
    rows = CHUNK * GLA_STEP_CHUNKS
    nt = seq // rows
    qk_w = HEADS * DK
    v_w = HEADS * GLA_DV
    a3, lvl = _decay_tables()
    row = lambda b, t: b * nt + t
    const = lambda shape: pl.BlockSpec(shape, lambda b, t: (0, 0))
    return pl.pallas_call(
        _gla_kernel,
        grid=(batch, nt),
        in_specs=[pl.BlockSpec((rows, qk_w), lambda b, t: (row(b, t), 0)),
                  pl.BlockSpec((rows, qk_w), lambda b, t: (row(b, t), 1)),
                  pl.BlockSpec((rows, v_w), lambda b, t: (row(b, t), 1)),
                  pl.BlockSpec((rows, v_w), lambda b, t: (row(b, t), 2)),
                  pl.BlockSpec((rows, GLA_RANK_PAD), lambda b, t: (row(b, t), (2 * qk_w + 2 * v_w) // GLA_RANK_PAD)),
                  const((GLA_RANK_PAD, qk_w)), const((1, qk_w)), const((1, GLA_DV)),
                  const(a3.shape), const(lvl.shape)],
        out_specs=pl.BlockSpec((rows, v_w), lambda b, t: (row(b, t), 0)),
        out_shape=jax.ShapeDtypeStruct((n, v_w), BF16),
        scratch_shapes=[pltpu.VMEM((HEADS, GLA_DV, DK), F32)],
        compiler_params=_params("parallel", "arbitrary"),
        name="gla",
    )(proj, proj, proj, proj, proj, w_alpha, b_alpha.reshape(1, qk_w), o_norm.reshape(1, GLA_DV), a3, lvl)


def _qk_norm(x, w2):
    lane = lax.broadcasted_iota(jnp.int32, x.shape, 1)
    first = lane < DA_HEAD_DIM
    sq = x * x
    ms1 = jnp.sum(jnp.where(first, sq, 0.0), axis=-1, keepdims=True) * (1.0 / DA_HEAD_DIM)
    ms2 = jnp.sum(jnp.where(first, 0.0, sq), axis=-1, keepdims=True) * (1.0 / DA_HEAD_DIM)
    r = jnp.where(first, lax.rsqrt(ms1 + EPS), lax.rsqrt(ms2 + EPS))
    return x * r * w2, first


def _score_bound(q_norm_w, k_norm_w):
    return (jnp.max(jnp.abs(q_norm_w)) * jnp.max(jnp.abs(k_norm_w))
            * (DA_HEAD_DIM * DA_HEAD_DIM ** -0.5 * LOG2_E * SCORE_BOUND_MARGIN))


def _attn_kernel(qi_tab, ki_tab, q_ref, k_ref, v_ref, qn_ref, kn_ref, lam_ref, sub_ref, o_ref,
                 qs_ref, m_ref, l_ref, acc_ref, *, lam_init, bounded):
    p = pl.program_id(2)
    qi = qi_tab[p]
    ki = ki_tab[p]
    tq = ATTN_BLOCK
    dv = 2 * DA_HEAD_DIM
    heads = [slice(h * dv, (h + 1) * dv) for h in range(ATTN_HEADS)]

    @pl.when(ki == 0)
    def _():
        for h, sl in enumerate(heads):
            qn, first = _qk_norm(q_ref[:, sl], qn_ref[...])
            qn = qn * (DA_HEAD_DIM ** -0.5 * LOG2_E)
            q1 = jnp.where(first, qn, 0.0).T.astype(BF16)
            q2 = jnp.where(first, 0.0, qn).T.astype(BF16)
            for copy in range(1 if bounded else 2):
                qs_ref[h, copy, :, :tq] = q1
                qs_ref[h, copy, :, tq:] = q2
            if bounded:
                bound = _score_bound(qn_ref[...], kn_ref[...])
                row = lax.broadcasted_iota(jnp.int32, (dv, 2 * tq), 0)
                qs_ref[h, 1] = jnp.where(row == 0, -bound, 0.0).astype(BF16)
        if not bounded:
            m_ref[...] = jnp.full_like(m_ref, -jnp.inf)
        l_ref[...] = jnp.zeros_like(l_ref)
        acc_ref[...] = jnp.zeros_like(acc_ref)

    def step(masked):
        tk = k_ref.shape[0]
        ones = jnp.ones((16, tk), BF16)
        unit = jnp.where(lax.broadcasted_iota(jnp.int32, (tk, dv), 1) == 0, 1.0, 0.0).astype(BF16)
        for h, sl in enumerate(heads):
            kn, _ = _qk_norm(k_ref[:, sl], kn_ref[...])
            kn = kn.astype(BF16)

            def mask(s):
                if not masked:
                    return s
                key = lax.broadcasted_iota(jnp.int32, s.shape, 0)
                qry = lax.broadcasted_iota(jnp.int32, s.shape, 1)
                qry = jnp.where(qry >= tq, qry - tq, qry)
                return jnp.where(key <= qry, s, -jnp.inf)

            if bounded:
                shifted = mask(_dot(jnp.concatenate([kn, unit], axis=1), qs_ref[h].reshape(2 * dv, 2 * tq)))
                alpha = 1.0
            else:
                m_prev = m_ref[h]
                m_new = jnp.maximum(m_prev, jnp.max(mask(_dot(kn, qs_ref[h, 0])), axis=0, keepdims=True))
                alpha = jnp.exp2(m_prev - m_new)
                shifted = mask(_dot(kn, qs_ref[h, 1])) - m_new
                m_ref[h] = m_new
            pr = jnp.exp2(shifted).astype(BF16) if bounded else jnp.exp2(shifted.astype(BF16))
            vt = jnp.concatenate([v_ref[:, sl].T.astype(BF16), ones], axis=0)
            pv = _dot(vt, pr)
            l_ref[h] = alpha * l_ref[h] + pv[dv:dv + 1]
            acc_ref[h] = alpha * acc_ref[h] + pv[:dv]

    @pl.when(ki < qi)
    def _():
        step(False)

    @pl.when(ki == qi)
    def _():
        step(True)
        lv = lam_ref[...]
        lam = (jnp.exp(jnp.sum(lv[0:1] * lv[1:2], axis=-1, keepdims=True))
               - jnp.exp(jnp.sum(lv[2:3] * lv[3:4], axis=-1, keepdims=True)) + lam_init)
        for h, sl in enumerate(heads):
            o1 = acc_ref[h, :, :tq] / l_ref[h, :, :tq]
            o2 = acc_ref[h, :, tq:] / l_ref[h, :, tq:]
            o = o1 - lam * o2
            o = o * lax.rsqrt(jnp.mean(o * o, axis=0, keepdims=True) + EPS) * sub_ref[...] * (1.0 - lam_init)
            o_ref[:, sl] = o.T.astype(BF16)


def _diff_attn(proj, q_norm, k_norm, lam_vecs, subln, lam_init, batch, seq):
    n = proj.shape[0]
    blk = ATTN_BLOCK
    nq = seq // blk
    dv = 2 * DA_HEAD_DIM
    pairs = [(qi, ki) for qi in range(nq) for ki in range(qi + 1)]
    qi_tab = jnp.asarray([p[0] for p in pairs], jnp.int32)
    ki_tab = jnp.asarray([p[1] for p in pairs], jnp.int32)
    width = ATTN_HEADS * dv
    groups = HEADS // ATTN_HEADS
    q_col = (4 * HG_WIDTH) // width
    const = lambda shape: pl.BlockSpec(shape, lambda b, g, p, qt, kt: (0, 0))
    grid_spec = pltpu.PrefetchScalarGridSpec(
        num_scalar_prefetch=2,
        grid=(batch, groups, len(pairs)),
        in_specs=[pl.BlockSpec((blk, width), lambda b, g, p, qt, kt: (b * nq + qt[p], q_col + g)),
                  pl.BlockSpec((blk, width), lambda b, g, p, qt, kt: (b * nq + kt[p], q_col + groups + g)),
                  pl.BlockSpec((blk, width), lambda b, g, p, qt, kt: (b * nq + kt[p], q_col + 2 * groups + g)),
                  const((1, dv)), const((1, dv)), const((4, DA_HEAD_DIM)), const((dv, 1))],
        out_specs=pl.BlockSpec((blk, width), lambda b, g, p, qt, kt: (b * nq + qt[p], g)),
        scratch_shapes=[pltpu.VMEM((ATTN_HEADS, 2, dv, 2 * blk), BF16),
                        pltpu.VMEM((ATTN_HEADS, 1, 2 * blk), F32),
                        pltpu.VMEM((ATTN_HEADS, 1, 2 * blk), F32),
                        pltpu.VMEM((ATTN_HEADS, dv, 2 * blk), F32)])
    args = (qi_tab, ki_tab, proj, proj, proj, jnp.tile(q_norm, 2).reshape(1, -1), jnp.tile(k_norm, 2).reshape(1, -1),
            lam_vecs, subln.reshape(-1, 1))

    def run(bounded):
        return pl.pallas_call(
            functools.partial(_attn_kernel, lam_init=lam_init, bounded=bounded),
            grid_spec=grid_spec,
            out_shape=jax.ShapeDtypeStruct((n, HEADS * dv), BF16),
            compiler_params=_params("parallel", "parallel", "arbitrary"),
            name="diff_attn_bounded" if bounded else "diff_attn",
        )(*args)

    return lax.cond(_score_bound(q_norm, k_norm) <= MAX_SCORE_BOUND, lambda: run(True), lambda: run(False))


def _outproj_router_kernel(*refs, n_a):
    a_refs = refs[:n_a]
    (w_ref, x_ref, nw_ref, wr_hi_ref, wr_lo_ref, br_ref, tri_ref,
     x2_ref, meta_ref, rec_ref, cnt_ref, carry_ref) = refs[n_a:]

    @pl.when(pl.program_id(0) == 0)
    def _():
        carry_ref[...] = jnp.zeros_like(carry_ref)

    x2 = x_ref[...]
    off = 0
    for a_ref in a_refs:
        kk = a_ref.shape[1]
        x2 = x2 + _dot(a_ref[...], w_ref[off:off + kk, :])
        off += kk
    x2_ref[...] = x2

    h = _rms(x2, nw_ref[...])
    h_hi = h.astype(BF16)
    h_lo = (h - h_hi.astype(F32)).astype(BF16)
    wr_hi = wr_hi_ref[...]
    by_hi = _dot_nt(jnp.concatenate([wr_hi, wr_lo_ref[...]], axis=0), h_hi)
    lt = by_hi[:ROUTER_ROWS] + by_hi[ROUTER_ROWS:] + _dot_nt(wr_hi, h_lo) + br_ref[...]
    row = lambda r: lt[r:r + 1, :]

    gl = [row(g) for g in range(N_GROUPS)]
    gmax = functools.reduce(jnp.maximum, gl)
    gsel = jnp.where(gl[0] == gmax, 0.0, jnp.where(gl[1] == gmax, 1.0, jnp.where(gl[2] == gmax, 2.0, 3.0)))
    g_w = 1.0 / functools.reduce(jnp.add, [jnp.exp(x - gmax) for x in gl])

    el = []
    for e in range(EXPERTS_PER_GROUP):
        v = row(N_GROUPS + e)
        for g in range(1, N_GROUPS):
            v = jnp.where(gsel == float(g), row(N_GROUPS + EXPERTS_PER_GROUP * g + e), v)
        el.append(v)
    emax = functools.reduce(jnp.maximum, el)
    sel = []
    for e in range(EXPERTS_PER_GROUP):
        ahead = jnp.zeros_like(emax)
        for o in range(EXPERTS_PER_GROUP):
            if o < e:
                ahead = ahead + jnp.where(el[o] >= el[e], 1.0, 0.0)
            elif o > e:
                ahead = ahead + jnp.where(el[o] > el[e], 1.0, 0.0)
        sel.append(ahead < 2.0)
    pe = [jnp.where(sel[e], jnp.exp(el[e] - emax), 0.0) for e in range(EXPERTS_PER_GROUP)]
    scale = g_w / functools.reduce(jnp.add, pe)
    zero = jnp.zeros_like(emax)
    seen = zero
    lo_idx, hi_idx, c_lo, c_hi = zero, zero, zero, zero
    for e in range(EXPERTS_PER_GROUP):
        is_lo = sel[e] & (seen == 0.0)
        is_hi = sel[e] & (seen == 1.0)
        lo_idx = jnp.where(is_lo, float(e), lo_idx)
        hi_idx = jnp.where(is_hi, float(e), hi_idx)
        c_lo = jnp.where(is_lo, pe[e] * scale, c_lo)
        c_hi = jnp.where(is_hi, pe[e] * scale, c_hi)
        seen = seen + jnp.where(sel[e], 1.0, 0.0)
    pair = lo_idx * (7.0 - lo_idx) * 0.5 + (hi_idx - lo_idx - 1.0)
    bucket = gsel * float(N_PAIRS) + pair

    bid = lax.broadcasted_iota(jnp.int32, (ROUTER_ROWS, bucket.shape[1]), 0).astype(F32)
    onehot = jnp.where(bid == bucket, 1.0, 0.0)
    before = _dot(onehot.astype(BF16), tri_ref[...]) + carry_ref[...]
    rank = jnp.sum(onehot * before, axis=0, keepdims=True)
    carry = carry_ref[...] + jnp.sum(onehot, axis=1, keepdims=True)
    carry_ref[...] = carry
    cnt_ref[...] = carry
    meta = jnp.concatenate([bucket, rank, c_lo, c_hi, zero, zero, zero, zero], axis=0)
    meta_ref[...] = meta
    rec_ref[...] = jnp.concatenate([meta, jnp.zeros((LANE - meta.shape[0], meta.shape[1]), F32)], axis=0).T


def _outproj_router(a_list, w_out, x, n, norm_w, w_group, b_group, w_router, b_router):
    d = D_MODEL
    tm = ROUTER_TILE
    tok = lambda: pl.BlockSpec((tm, d), lambda i: (i, 0))
    wr = jnp.concatenate([w_group, w_router.transpose(1, 0, 2).reshape(d, N_EXPERTS)], axis=1)
    wr_t = jnp.zeros((ROUTER_ROWS, d), F32).at[:wr.shape[1]].set(wr.T)
    wr_hi = wr_t.astype(BF16)
    wr_lo = (wr_t - wr_hi.astype(F32)).astype(BF16)
    br = jnp.zeros((ROUTER_ROWS, 1), F32).at[:wr.shape[1], 0].set(
        jnp.concatenate([b_group, b_router.reshape(-1)]))
    tri = jnp.asarray(np.triu(np.ones((tm, tm), np.float32), 1), BF16)
    const = lambda shape: pl.BlockSpec(shape, lambda i: (0, 0))
    in_specs = [pl.BlockSpec((tm, a.shape[1]), lambda i: (i, 0)) for a in a_list]
    in_specs += [const((d, d)), tok(), const((1, d)),
                 const((ROUTER_ROWS, d)), const((ROUTER_ROWS, d)), const((ROUTER_ROWS, 1)), const((tm, tm))]
    return pl.pallas_call(
        functools.partial(_outproj_router_kernel, n_a=len(a_list)),
        grid=(n // tm,),
        in_specs=in_specs,
        out_specs=[tok(),
                   pl.BlockSpec((8, tm), lambda i: (0, i)),
                   pl.BlockSpec((tm, LANE), lambda i: (i, 0)),
                   const((ROUTER_ROWS, 1))],
        out_shape=[jax.ShapeDtypeStruct((n, d), F32),
                   jax.ShapeDtypeStruct((8, n), F32),
                   jax.ShapeDtypeStruct((n, LANE), F32),
                   jax.ShapeDtypeStruct((ROUTER_ROWS, 1), F32)],
        scratch_shapes=[pltpu.VMEM((ROUTER_ROWS, 1), F32)],
        compiler_params=_params("arbitrary"),
        name="outproj_router",
    )(*a_list, w_out, x, norm_w.reshape(1, d), wr_hi, wr_lo, br, tri)


def _gather_tokens(table, idx):
    m = idx.shape[0]
    info = plsc.get_sparse_core_info()
    n_workers = info.num_cores * info.num_subcores
    per_worker = m // n_workers
    n_chunks = per_worker // SC_CHUNK
    assert per_worker * n_workers == m and n_chunks * SC_CHUNK == per_worker and n_chunks % 2 == 0
    mesh = plsc.VectorSubcoreMesh(core_axis_name="core", subcore_axis_name="subcore")
    rows = pltpu.VMEM((SC_CHUNK,) + table.shape[1:], table.dtype)

    @functools.partial(
        pl.kernel, mesh=mesh,
        out_type=jax.ShapeDtypeStruct((m,) + table.shape[1:], table.dtype),
        scratch_types=[pltpu.VMEM((n_chunks, SC_CHUNK), jnp.int32), rows, rows,
                       pltpu.SemaphoreType.DMA, pltpu.SemaphoreType.DMA],
        compiler_params=pltpu.CompilerParams(use_tc_tiling_on_sc=True),
        name="gather_tokens")
    def gather(table_hbm, idx_hbm, out_hbm, idx_v, rows_a, rows_b, sem_a, sem_b):
        worker = lax.axis_index("subcore") * info.num_cores + lax.axis_index("core")
        bufs = ((rows_a, sem_a), (rows_b, sem_b))

        def fetch(j, buf):
            return pltpu.make_async_copy(table_hbm.at[idx_v.at[j]], bufs[buf][0], bufs[buf][1])

        pltpu.sync_copy(idx_hbm.at[worker], idx_v)
        fetch(0, 0).start()

        @pl.loop(0, n_chunks, step=2)
        def _(j0):
            for buf in range(2):
                j = j0 + buf
                fetch(j, buf).wait()

                @pl.when(j + 1 < n_chunks)
                def _():
                    fetch(j + 1, 1 - buf).start()

                off = pl.multiple_of(worker * per_worker + j * SC_CHUNK, SC_CHUNK)
                pltpu.sync_copy(bufs[buf][0], out_hbm.at[pl.ds(off, SC_CHUNK)])

    return gather(table, idx.reshape(n_workers, n_chunks, SC_CHUNK))


def _scatter_tokens(arrays, idx, m):
    n = idx.shape[0]
    na = len(arrays)
    info = plsc.get_sparse_core_info()
    n_workers = info.num_cores * info.num_subcores
    per_worker = n // n_workers
    n_chunks = per_worker // SC_CHUNK
    assert per_worker * n_workers == n and n_chunks * SC_CHUNK == per_worker and n_chunks % 2 == 0
    mesh = plsc.VectorSubcoreMesh(core_axis_name="core", subcore_axis_name="subcore")
    bufs = [pltpu.VMEM((SC_CHUNK,) + a.shape[1:], a.dtype) for a in arrays for _ in range(2)]

    @functools.partial(
        pl.kernel, mesh=mesh,
        out_type=[jax.ShapeDtypeStruct((m,) + a.shape[1:], a.dtype) for a in arrays],
        scratch_types=[pltpu.VMEM((n_chunks, SC_CHUNK), jnp.int32)] + bufs + [pltpu.SemaphoreType.DMA] * (2 * na),
        compiler_params=pltpu.CompilerParams(use_tc_tiling_on_sc=True),
        name="scatter_tokens")
    def scatter(*refs):
        ins, idx_hbm, outs = refs[:na], refs[na], refs[na + 1:2 * na + 1]
        idx_v = refs[2 * na + 1]
        buf_v = refs[2 * na + 2:4 * na + 2]
        sem_v = refs[4 * na + 2:]
        worker = lax.axis_index("subcore") * info.num_cores + lax.axis_index("core")

        def load(a, j, b):
            off = pl.multiple_of(worker * per_worker + j * SC_CHUNK, SC_CHUNK)
            return pltpu.make_async_copy(ins[a].at[pl.ds(off, SC_CHUNK)], buf_v[2 * a + b], sem_v[2 * a + b])

        pltpu.sync_copy(idx_hbm.at[worker], idx_v)
        for a in range(na):
            load(a, 0, 0).start()

        @pl.loop(0, n_chunks, step=2)
        def _(j0):
            for b in range(2):
                j = j0 + b
                for a in range(na):
                    load(a, j, b).wait()

                @pl.when(j + 1 < n_chunks)
                def _():
                    for a in range(na):
                        load(a, j + 1, 1 - b).start()

                for a in range(na):
                    pltpu.sync_copy(buf_v[2 * a + b], outs[a].at[idx_v.at[j]])

    return scatter(*arrays, idx.reshape(n_workers, n_chunks, SC_CHUNK))


def _moe_kernel(step_ids, e_lo, e_hi, n_valid, n_tiles, x_ref, rec_ref, nw_ref, *refs):
    del step_ids, e_lo, e_hi
    s = pl.program_id(0)
    o_ref = refs[-1]

    @pl.when(s * MOE_STEP_TILES < n_tiles[0])
    def _():
        for i in range(MOE_STEP_TILES):
            g_lo_ref, g_hi_ref, u_lo_ref, u_hi_ref, d_lo_ref, d_hi_ref = refs[6 * i:6 * (i + 1)]
            rows = pl.ds(i * MOE_TILE, MOE_TILE)
            valid = lax.broadcasted_iota(jnp.int32, (MOE_TILE, 1), 0) < n_valid[s * MOE_STEP_TILES + i]
            x = jnp.where(valid, x_ref[rows, :], 0.0)
            h = _rms(x, nw_ref[...]).astype(BF16)
            rec = jnp.where(valid, rec_ref[rows, :], 0.0)

            def expert(g_ref, u_ref, d_ref):
                act = _silu(_dot(h, g_ref[0])) * _dot(h, u_ref[0])
                return _dot(act.astype(BF16), d_ref[0])

            y = (rec[:, 2:3] * expert(g_lo_ref, u_lo_ref, d_lo_ref)
                 + rec[:, 3:4] * expert(g_hi_ref, u_hi_ref, d_hi_ref))
            o_ref[rows, :] = jnp.where(valid, x_ref[rows, :], 0.0) + y


def _moe(x2, n, meta, rec, cnt, norm_w, w_gate, w_up, w_down, layer):
    d = D_MODEL
    tm = MOE_TILE
    max_tiles = n // tm + N_BUCKETS
    n_slots = max_tiles * tm
    counts = cnt[:N_BUCKETS, 0].astype(jnp.int32)
    tiles_per_bucket = (counts + tm - 1) // tm
    tile_end = jnp.cumsum(tiles_per_bucket)
    tile_start = tile_end - tiles_per_bucket
    n_tiles = tile_end[-1]
    in_bucket = meta[0][:, None] == jnp.arange(N_BUCKETS, dtype=F32)[None, :]
    dest = meta[1].astype(jnp.int32) + jnp.sum(jnp.where(in_bucket, tile_start[None, :] * tm, 0), axis=1)
    x_sorted, rec_sorted = _scatter_tokens([x2, rec], dest, n_slots)
    first = layer * N_EXPERTS
    w_gate, w_up, w_down = (_to_bf16(w, first=first, count=N_EXPERTS) for w in (w_gate, w_up, w_down))

    k = MOE_STEP_TILES
    tile_raw = jnp.arange(max_tiles, dtype=jnp.int32)
    tile_ids = jnp.minimum(tile_raw, n_tiles - 1)
    step_ids = jnp.minimum(jnp.arange(max_tiles // k, dtype=jnp.int32), (n_tiles - 1) // k)
    of_bucket = lambda table: jnp.sum(jnp.where(tile_bucket[:, None] == jnp.arange(N_BUCKETS)[None, :], table[None, :], 0), axis=1)
    tile_bucket = jnp.sum((tile_ids[:, None] >= tile_end[None, :]).astype(jnp.int32), axis=1)
    n_valid = jnp.clip(of_bucket(counts) - (tile_ids - of_bucket(tile_start)) * tm, 0, tm)
    n_valid = jnp.where(tile_raw < n_tiles, n_valid, 0)
    group = tile_bucket // N_PAIRS
    pair = tile_bucket % N_PAIRS
    pair_lo = jnp.sum(jnp.where(pair[:, None] == jnp.arange(N_PAIRS)[None, :], jnp.asarray(PAIR_LO, jnp.int32)[None, :], 0), axis=1)
    pair_hi = jnp.sum(jnp.where(pair[:, None] == jnp.arange(N_PAIRS)[None, :], jnp.asarray(PAIR_HI, jnp.int32)[None, :], 0), axis=1)
    e_lo = group * EXPERTS_PER_GROUP + pair_lo
    e_hi = group * EXPERTS_PER_GROUP + pair_hi

    step_rows = lambda s, sid, lo, hi, nv, nt: (sid[s], 0)
    weight_specs, weight_args = [], []
    for i in range(k):
        wlo = lambda s, sid, lo, hi, nv, nt, i=i: (lo[s * k + i], 0, 0)
        whi = lambda s, sid, lo, hi, nv, nt, i=i: (hi[s * k + i], 0, 0)
        weight_specs += [pl.BlockSpec((1, d, D_EXPERT), wlo), pl.BlockSpec((1, d, D_EXPERT), whi),
                         pl.BlockSpec((1, d, D_EXPERT), wlo), pl.BlockSpec((1, d, D_EXPERT), whi),
                         pl.BlockSpec((1, D_EXPERT, d), wlo), pl.BlockSpec((1, D_EXPERT, d), whi)]
        weight_args += [w_gate, w_gate, w_up, w_up, w_down, w_down]
    grid_spec = pltpu.PrefetchScalarGridSpec(
        num_scalar_prefetch=5,
        grid=(max_tiles // k,),
        in_specs=[pl.BlockSpec((k * tm, d), step_rows), pl.BlockSpec((k * tm, LANE), step_rows),
                  pl.BlockSpec((1, d), lambda s, sid, lo, hi, nv, nt: (0, 0))] + weight_specs,
        out_specs=pl.BlockSpec((k * tm, d), step_rows))
    out_sorted = pl.pallas_call(
        _moe_kernel,
        grid_spec=grid_spec,
        out_shape=jax.ShapeDtypeStruct((n_slots, d), F32),
        compiler_params=_params("arbitrary"),
        name="moe",
    )(step_ids, e_lo, e_hi, n_valid, n_tiles.reshape(1), x_sorted, rec_sorted, norm_w.reshape(1, d), *weight_args)
    return _gather_tokens(out_sorted, dest)


def kernel(x, attn_norm, ffn_norm, even_w_in, even_w_out, hg_lb_logits, hg_o_norm, da_q_norm, da_k_norm, da_lambda, da_subln, odd_w_in, gla_w_alpha, gla_b_alpha, gla_o_norm, odd_w_out, moe_w_group, moe_b_group, moe_w_router, moe_b_router, moe_w_gate, moe_w_up, moe_w_down):
    batch, seq, d = x.shape
    depth = attn_norm.shape[0]
    n = batch * seq
    xf = x.reshape(n, d)

    lb_sm = jax.nn.softmax(hg_lb_logits.astype(F32), axis=0)
    lb_all = jnp.cumsum(lb_sm, axis=0) - lb_sm[0:1]
    rank = gla_w_alpha.shape[1]

    experts = lambda w: w.reshape((-1,) + w.shape[2:])
    moe_w_gate, moe_w_up, moe_w_down = experts(moe_w_gate), experts(moe_w_up), experts(moe_w_down)

    for layer in range(depth):
        j = layer // 2
        if layer % 2 == 0:
            lam_init = 0.8 - 0.6 * math.exp(-0.3 * layer)
            proj = _inproj(xf, n, attn_norm[layer], _to_bf16(even_w_in, first=j, count=1)[0])
            o_a = _hgrn(proj, lb_all[j], hg_o_norm[j], batch, seq)
            o_b = _diff_attn(proj, da_q_norm[j], da_k_norm[j], da_lambda[j], da_subln[j], lam_init, batch, seq)
            mix = [o_a, o_b]
            w_out = _to_bf16(even_w_out, first=j, count=1)[0]
        else:
            w_alpha = jnp.pad(gla_w_alpha[j], ((0, GLA_RANK_PAD - rank), (0, 0))).astype(BF16)
            w_in = _to_bf16(jnp.swapaxes(odd_w_in, 1, 2), odd_w_in.shape[2] + GLA_RANK_PAD - rank, first=j, count=1)[0]
            proj = _inproj(xf, n, attn_norm[layer], w_in, w_transposed=True)
            mix = [_gla(proj, w_alpha, gla_b_alpha[j], gla_o_norm[j], batch, seq)]
            w_out = _to_bf16(odd_w_out, first=j, count=1)[0]
        x2, meta, rec, cnt = _outproj_router(mix, w_out, xf, n, ffn_norm[layer], moe_w_group[layer],
                                        moe_b_group[layer], moe_w_router[layer], moe_b_router[layer])
        xf = _moe(x2, n, meta, rec, cnt, ffn_norm[layer], moe_w_gate, moe_w_up, moe_w_down, layer)
    return xf.reshape(batch, seq, d)
```

```python
import functools
import math

import numpy as np
import jax
import jax.numpy as jnp
from jax import lax
from jax.experimental import pallas as pl
from jax.experimental.pallas import tpu as pltpu
from jax.experimental.pallas import tpu_sc as plsc

F32 = jnp.float32
BF16 = jnp.bfloat16
EPS = 1e-6
LOG2_E = 1.4426950408889634

D_MODEL = 1024
LANE = 128
HEADS = 4
DK = 128
HG_WIDTH = 512
DA_HEAD_DIM = 64
GLA_DV = 256
GLA_RANK_PAD = 128
GLA_TAU = 16.0
N_GROUPS = 4
EXPERTS_PER_GROUP = 4
N_EXPERTS = 16
D_EXPERT = 512
N_PAIRS = 6
N_BUCKETS = N_GROUPS * N_PAIRS
ROUTER_ROWS = 32

CHUNK = 128
GLA_STEP_CHUNKS = 8
LEVELS = 7
ATTN_BLOCK = 512
ATTN_HEADS = 4
SCORE_BOUND_MARGIN = 1.05
MAX_SCORE_BOUND = 60.0
ROW_TILE = 512
ROUTER_TILE = 1024
MOE_TILE = 256
MOE_STEP_TILES = 2
SC_CHUNK = 32
VMEM_LIMIT = 56 * 1024 * 1024
CAST_BLOCK_BYTES = 8 * 1024 * 1024

PAIR_LO = (0, 0, 0, 1, 1, 2)
PAIR_HI = (1, 2, 3, 2, 3, 3)


def _dot(a, b):
    return jnp.dot(a, b, preferred_element_type=F32)


def _dot_nt(a, b):
    return lax.dot_general(a, b, (((1,), (1,)), ((), ())), preferred_element_type=F32)


def _dot_tn(a, b):
    return lax.dot_general(a, b, (((0,), (0,)), ((), ())), preferred_element_type=F32)


def _rms(x, w):
    return x * lax.rsqrt(jnp.mean(x * x, axis=-1, keepdims=True) + EPS) * w


def _silu(x):
    return x * jax.nn.sigmoid(x)


def _params(*sem):
    return pltpu.CompilerParams(dimension_semantics=sem, vmem_limit_bytes=VMEM_LIMIT)


def _cast_kernel(w_ref, o_ref):
    k = w_ref.shape[1]
    o_ref[:, :k, :] = w_ref[...].astype(BF16)
    if o_ref.shape[1] > k:
        o_ref[:, k:, :] = jnp.zeros((o_ref.shape[0], o_ref.shape[1] - k, o_ref.shape[2]), BF16)


def _to_bf16(w, k_pad=None, first=0, count=None):
    _, k, n = w.shape
    l = count or w.shape[0]
    k_pad = k_pad or k
    per_matrix = k * n * 4
    lb = max([c for c in range(1, l + 1) if l % c == 0 and first % c == 0 and c * per_matrix <= CAST_BLOCK_BYTES] or [1])
    nb = n
    while lb == 1 and k * nb * 4 > CAST_BLOCK_BYTES and nb % (2 * LANE) == 0:
        nb //= 2
    return pl.pallas_call(
        _cast_kernel,
        grid=(l // lb, n // nb),
        in_specs=[pl.BlockSpec((lb, k, nb), lambda i, j: (first // lb + i, 0, j))],
        out_specs=pl.BlockSpec((lb, k_pad, nb), lambda i, j: (i, 0, j)),
        out_shape=jax.ShapeDtypeStruct((l, k_pad, n), BF16),
        compiler_params=_params("parallel", "parallel"),
        name="to_bf16",
    )(w)


def _inproj_kernel(x_ref, nw_ref, w_ref, o_ref, *, w_transposed):
    h = _rms(x_ref[...], nw_ref[...]).astype(BF16)
    o_ref[...] = _dot_nt(h, w_ref[...]) if w_transposed else _dot(h, w_ref[...])


def _inproj(x, n, norm_w, w, w_transposed=False):
    d = D_MODEL
    n_out = w.shape[0] if w_transposed else w.shape[1]
    return pl.pallas_call(
        functools.partial(_inproj_kernel, w_transposed=w_transposed),
        grid=(n // ROW_TILE,),
        in_specs=[pl.BlockSpec((ROW_TILE, d), lambda i: (i, 0)),
                  pl.BlockSpec((1, d), lambda i: (0, 0)),
                  pl.BlockSpec(w.shape, lambda i: (0, 0))],
        out_specs=pl.BlockSpec((ROW_TILE, n_out), lambda i: (i, 0)),
        out_shape=jax.ShapeDtypeStruct((n, n_out), F32),
        compiler_params=_params("parallel"),
        name="inproj",
    )(x, norm_w.reshape(1, d), w)


def _decay_tables():
    c = CHUNK
    a = np.zeros((LEVELS + 2, c, c), np.float32)
    t = np.arange(c)
    for l in range(LEVELS):
        s = 1 << l
        for r in range(c):
            mid = (r // (2 * s)) * 2 * s + s - 1
            if (r >> l) & 1:
                a[l, r] = (t > mid) & (t <= r)
            else:
                a[l, r] = (t > r) & (t <= mid)
    a[LEVELS] = t[None, :] <= t[:, None]
    a[LEVELS + 1] = t[None, :] > t[:, None]
    a = a.reshape((LEVELS + 2) * c, c)
    a3 = np.concatenate([a, a], axis=1)
    i, j = np.meshgrid(t, t, indexing="ij")
    x = np.maximum(i ^ j, 1)
    lvl = np.where(j < i, np.floor(np.log2(x)).astype(np.int32), np.where(i == j, LEVELS, -1))
    return jnp.asarray(a3, BF16), jnp.asarray(lvl, jnp.int32)


def _decay_factors(a3, g):
    g_hi = g.astype(BF16)
    g_lo = (g - g_hi.astype(F32)).astype(BF16)
    return jnp.exp(_dot(a3, jnp.concatenate([g_hi, g_lo], axis=0)))


def _gla_chunk(q, k, v, f, st_ref, lvl, rowid):
    c = CHUNK
    sc = jnp.where(lvl == LEVELS, jnp.sum(q * k, axis=-1, keepdims=True), 0.0)
    for l in range(LEVELS):
        s = 1 << l
        if s % 8 == 0:
            qk = jnp.concatenate([(q if b & 1 else k)[b * s:(b + 1) * s] for b in range(c // s)], axis=0)
        else:
            qk = jnp.where(((rowid >> l) & 1) == 1, q, k)
        m = (qk * f[l * c:(l + 1) * c]).astype(BF16)
        sc = jnp.where(lvl == l, _dot_nt(m, m), sc)
    qd = (q * f[LEVELS * c:(LEVELS + 1) * c]).astype(BF16)
    kd = (k * f[(LEVELS + 1) * c:(LEVELS + 2) * c]).astype(BF16)
    vb = v.astype(BF16)
    st = st_ref[...]
    o = _dot(sc.astype(BF16), vb) + _dot_nt(qd, st.astype(BF16))
    chunk_decay = f[(LEVELS + 1) * c - 1:(LEVELS + 1) * c, :]
    st_ref[...] = st * chunk_decay + _dot_tn(vb, kd)
    return o


def _hgrn_kernel(q_ref, f_ref, i_ref, g_ref, lb_ref, on_ref, a3_ref, lvl_ref, o_ref, st_ref):
    @pl.when(pl.program_id(1) == 0)
    def _():
        st_ref[...] = jnp.zeros_like(st_ref)

    lvl = lvl_ref[...]
    rowid = lax.broadcasted_iota(jnp.int32, (CHUNK, 1), 0)
    lb = lb_ref[...]
    for c in range(GLA_STEP_CHUNKS):
        rows = slice(c * CHUNK, (c + 1) * CHUNK)
        fl = f_ref[rows, :]
        f = _decay_factors(a3_ref[...], jnp.log(lb + (1.0 - lb) * jax.nn.sigmoid(fl)))
        key_in = (1.0 - lb) * jax.nn.sigmoid(-fl)
        for h in range(HEADS):
            sl = slice(h * DK, (h + 1) * DK)
            q = _silu(q_ref[rows, sl]) * (DK ** -0.5)
            o = _gla_chunk(q, key_in[:, sl], i_ref[rows, sl], f[:, sl], st_ref.at[h], lvl, rowid)
            o = _rms(o, on_ref[...])
            o_ref[rows, sl] = (o * _silu(g_ref[rows, sl])).astype(BF16)


def _hgrn(proj, lb, o_norm, batch, seq):
    n = proj.shape[0]
    rows = CHUNK * GLA_STEP_CHUNKS
    nt = seq // rows
    a3, lvl = _decay_tables()
    col = lambda j: pl.BlockSpec((rows, HG_WIDTH), lambda b, t: (b * nt + t, j))
    const = lambda shape: pl.BlockSpec(shape, lambda b, t: (0, 0))
    return pl.pallas_call(
        _hgrn_kernel,
        grid=(batch, nt),
        in_specs=[col(0), col(1), col(2), col(3), const((1, HG_WIDTH)), const((1, DK)),
                  const(a3.shape), const(lvl.shape)],
        out_specs=pl.BlockSpec((rows, HG_WIDTH), lambda b, t: (b * nt + t, 0)),
        out_shape=jax.ShapeDtypeStruct((n, HG_WIDTH), BF16),
        scratch_shapes=[pltpu.VMEM((HEADS, DK, DK), F32)],
        compiler_params=_params("parallel", "arbitrary"),
        name="hgrn",
    )(proj, proj, proj, proj, lb.reshape(1, HG_WIDTH), o_norm.reshape(1, DK), a3, lvl)


def _log_sigmoid(x):
    return jnp.minimum(x, 0.0) - jnp.log1p(jnp.exp(-jnp.abs(x)))


def _gla_kernel(q_ref, k_ref, v_ref, g_ref, a_ref, wa_ref, ba_ref, on_ref, a3_ref, lvl_ref, o_ref, st_ref):
    @pl.when(pl.program_id(1) == 0)
    def _():
        st_ref[...] = jnp.zeros_like(st_ref)

    lvl = lvl_ref[...]
    rowid = lax.broadcasted_iota(jnp.int32, (CHUNK, 1), 0)
    for c in range(GLA_STEP_CHUNKS):
        rows = slice(c * CHUNK, (c + 1) * CHUNK)
        gate_logit = _dot(a_ref[rows, :].astype(BF16), wa_ref[...]) + ba_ref[...]
        f = _decay_factors(a3_ref[...], _log_sigmoid(gate_logit) * (1.0 / GLA_TAU))
        for h in range(HEADS):
            sl = slice(h * DK, (h + 1) * DK)
            sv = slice(h * GLA_DV, (h + 1) * GLA_DV)
            q = q_ref[rows, sl] * (DK ** -0.5)
            o = _gla_chunk(q, k_ref[rows, sl], v_ref[rows, sv], f[:, sl], st_ref.at[h], lvl, rowid)
            o = _rms(o, on_ref[...])
            o_ref[rows, sv] = (o * _silu(g_ref[rows, sv])).astype(BF16)


def _gla(proj, w_alpha, b_alpha, o_norm, batch, seq):
    n = proj.shape[0]
    rows = CHUNK * GLA_STEP_CHUNKS
    nt = seq // rows
    qk_w = HEADS * DK
    v_w = HEADS * GLA_DV
    a3, lvl = _decay_tables()
    row = lambda b, t: b * nt + t
    const = lambda shape: pl.BlockSpec(shape, lambda b, t: (0, 0))
    return pl.pallas_call(
        _gla_kernel,
        grid=(batch, nt),
        in_specs=[pl.BlockSpec((rows, qk_w), lambda b, t: (row(b, t), 0)),
                  pl.BlockSpec((rows, qk_w), lambda b, t: (row(b, t), 1)),
                  pl.BlockSpec((rows, v_w), lambda b, t: (row(b, t), 1)),
                  pl.BlockSpec((rows, v_w), lambda b, t: (row(b, t), 2)),
                  pl.BlockSpec((rows, GLA_RANK_PAD), lambda b, t: (row(b, t), (2 * qk_w + 2 * v_w) // GLA_RANK_PAD)),
                  const((GLA_RANK_PAD, qk_w)), const((1, qk_w)), const((1, GLA_DV)),
                  const(a3.shape), const(lvl.shape)],
        out_specs=pl.BlockSpec((rows, v_w), lambda b, t: (row(b, t), 0)),
        out_shape=jax.ShapeDtypeStruct((n, v_w), BF16),
        scratch_shapes=[pltpu.VMEM((HEADS, GLA_DV, DK), F32)],
        compiler_params=_params("parallel", "arbitrary"),
        name="gla",
    )(proj, proj, proj, proj, proj, w_alpha, b_alpha.reshape(1, qk_w), o_norm.reshape(1, GLA_DV), a3, lvl)


def _qk_norm(x, w2):
    lane = lax.broadcasted_iota(jnp.int32, x.shape, 1)
    first = lane < DA_HEAD_DIM
    sq = x * x
    ms1 = jnp.sum(jnp.where(first, sq, 0.0), axis=-1, keepdims=True) * (1.0 / DA_HEAD_DIM)
    ms2 = jnp.sum(jnp.where(first, 0.0, sq), axis=-1, keepdims=True) * (1.0 / DA_HEAD_DIM)
    r = jnp.where(first, lax.rsqrt(ms1 + EPS), lax.rsqrt(ms2 + EPS))
    return x * r * w2, first


def _score_bound(q_norm_w, k_norm_w):
    return (jnp.max(jnp.abs(q_norm_w)) * jnp.max(jnp.abs(k_norm_w))
            * (DA_HEAD_DIM * DA_HEAD_DIM ** -0.5 * LOG2_E * SCORE_BOUND_MARGIN))


def _attn_kernel(qi_tab, ki_tab, q_ref, k_ref, v_ref, qn_ref, kn_ref, lam_ref, sub_ref, o_ref,
                 qs_ref, m_ref, l_ref, acc_ref, *, lam_init, bounded):
    p = pl.program_id(2)
    qi = qi_tab[p]
    ki = ki_tab[p]
    tq = ATTN_BLOCK
    dv = 2 * DA_HEAD_DIM
    heads = [slice(h * dv, (h + 1) * dv) for h in range(ATTN_HEADS)]

    @pl.when(ki == 0)
    def _():
        for h, sl in enumerate(heads):
            qn, first = _qk_norm(q_ref[:, sl], qn_ref[...])
            qn = qn * (DA_HEAD_DIM ** -0.5 * LOG2_E)
            q1 = jnp.where(first, qn, 0.0).T.astype(BF16)
            q2 = jnp.where(first, 0.0, qn).T.astype(BF16)
            for copy in range(1 if bounded else 2):
                qs_ref[h, copy, :, :tq] = q1
                qs_ref[h, copy, :, tq:] = q2
            if bounded:
                bound = _score_bound(qn_ref[...], kn_ref[...])
                row = lax.broadcasted_iota(jnp.int32, (dv, 2 * tq), 0)
                qs_ref[h, 1] = jnp.where(row == 0, -bound, 0.0).astype(BF16)
        if not bounded:
            m_ref[...] = jnp.full_like(m_ref, -jnp.inf)
        l_ref[...] = jnp.zeros_like(l_ref)
        acc_ref[...] = jnp.zeros_like(acc_ref)

    def step(masked):
        tk = k_ref.shape[0]
        ones = jnp.ones((16, tk), BF16)
        unit = jnp.where(lax.broadcasted_iota(jnp.int32, (tk, dv), 1) == 0, 1.0, 0.0).astype(BF16)
        for h, sl in enumerate(heads):
            kn, _ = _qk_norm(k_ref[:, sl], kn_ref[...])
            kn = kn.astype(BF16)

            def mask(s):
                if not masked:
                    return s
                key = lax.broadcasted_iota(jnp.int32, s.shape, 0)
                qry = lax.broadcasted_iota(jnp.int32, s.shape, 1)
                qry = jnp.where(qry >= tq, qry - tq, qry)
                return jnp.where(key <= qry, s, -jnp.inf)

            if bounded:
                shifted = mask(_dot(jnp.concatenate([kn, unit], axis=1), qs_ref[h].reshape(2 * dv, 2 * tq)))
                alpha = 1.0
            else:
                m_prev = m_ref[h]
                m_new = jnp.maximum(m_prev, jnp.max(mask(_dot(kn, qs_ref[h, 0])), axis=0, keepdims=True))
                alpha = jnp.exp2(m_prev - m_new)
                shifted = mask(_dot(kn, qs_ref[h, 1])) - m_new
                m_ref[h] = m_new
            pr = jnp.exp2(shifted).astype(BF16) if bounded else jnp.exp2(shifted.astype(BF16))
            vt = jnp.concatenate([v_ref[:, sl].T.astype(BF16), ones], axis=0)
            pv = _dot(vt, pr)
            l_ref[h] = alpha * l_ref[h] + pv[dv:dv + 1]
            acc_ref[h] = alpha * acc_ref[h] + pv[:dv]

    @pl.when(ki < qi)
    def _():
        step(False)

    @pl.when(ki == qi)
    def _():
        step(True)
        lv = lam_ref[...]
        lam = (jnp.exp(jnp.sum(lv[0:1] * lv[1:2], axis=-1, keepdims=True))
               - jnp.exp(jnp.sum(lv[2:3] * lv[3:4], axis=-1, keepdims=True)) + lam_init)
        for h, sl in enumerate(heads):
            o1 = acc_ref[h, :, :tq] / l_ref[h, :, :tq]
            o2 = acc_ref[h, :, tq:] / l_ref[h, :, tq:]
            o = o1 - lam * o2
            o = o * lax.rsqrt(jnp.mean(o * o, axis=0, keepdims=True) + EPS) * sub_ref[...] * (1.0 - lam_init)
            o_ref[:, sl] = o.T.astype(BF16)


def _diff_attn(proj, q_norm, k_norm, lam_vecs, subln, lam_init, batch, seq):
    n = proj.shape[0]
    blk = ATTN_BLOCK
    nq = seq // blk
    dv = 2 * DA_HEAD_DIM
    pairs = [(qi, ki) for qi in range(nq) for ki in range(qi + 1)]
    qi_tab = jnp.asarray([p[0] for p in pairs], jnp.int32)
    ki_tab = jnp.asarray([p[1] for p in pairs], jnp.int32)
    width = ATTN_HEADS * dv
    groups = HEADS // ATTN_HEADS
    q_col = (4 * HG_WIDTH) // width
    const = lambda shape: pl.BlockSpec(shape, lambda b, g, p, qt, kt: (0, 0))
    grid_spec = pltpu.PrefetchScalarGridSpec(
        num_scalar_prefetch=2,
        grid=(batch, groups, len(pairs)),
        in_specs=[pl.BlockSpec((blk, width), lambda b, g, p, qt, kt: (b * nq + qt[p], q_col + g)),
                  pl.BlockSpec((blk, width), lambda b, g, p, qt, kt: (b * nq + kt[p], q_col + groups + g)),
                  pl.BlockSpec((blk, width), lambda b, g, p, qt, kt: (b * nq + kt[p], q_col + 2 * groups + g)),
                  const((1, dv)), const((1, dv)), const((4, DA_HEAD_DIM)), const((dv, 1))],
        out_specs=pl.BlockSpec((blk, width), lambda b, g, p, qt, kt: (b * nq + qt[p], g)),
        scratch_shapes=[pltpu.VMEM((ATTN_HEADS, 2, dv, 2 * blk), BF16),
                        pltpu.VMEM((ATTN_HEADS, 1, 2 * blk), F32),
                        pltpu.VMEM((ATTN_HEADS, 1, 2 * blk), F32),
                        pltpu.VMEM((ATTN_HEADS, dv, 2 * blk), F32)])
    args = (qi_tab, ki_tab, proj, proj, proj, jnp.tile(q_norm, 2).reshape(1, -1), jnp.tile(k_norm, 2).reshape(1, -1),
            lam_vecs, subln.reshape(-1, 1))

    def run(bounded):
        return pl.pallas_call(
            functools.partial(_attn_kernel, lam_init=lam_init, bounded=bounded),
            grid_spec=grid_spec,
            out_shape=jax.ShapeDtypeStruct((n, HEADS * dv), BF16),
            compiler_params=_params("parallel", "parallel", "arbitrary"),
            name="diff_attn_bounded" if bounded else "diff_attn",
        )(*args)

    return lax.cond(_score_bound(q_norm, k_norm) <= MAX_SCORE_BOUND, lambda: run(True), lambda: run(False))


def _outproj_router_kernel(*refs, n_a):
    a_refs = refs[:n_a]
    (w_ref, x_ref, nw_ref, wr_hi_ref, wr_lo_ref, br_ref, tri_ref,
     x2_ref, meta_ref, rec_ref, cnt_ref, carry_ref) = refs[n_a:]

    @pl.when(pl.program_id(0) == 0)
    def _():
        carry_ref[...] = jnp.zeros_like(carry_ref)

    x2 = x_ref[...]
    off = 0
    for a_ref in a_refs:
        kk = a_ref.shape[1]
        x2 = x2 + _dot(a_ref[...], w_ref[off:off + kk, :])
        off += kk
    x2_ref[...] = x2

    h = _rms(x2, nw_ref[...])
    h_hi = h.astype(BF16)
    h_lo = (h - h_hi.astype(F32)).astype(BF16)
    wr_hi = wr_hi_ref[...]
    by_hi = _dot_nt(jnp.concatenate([wr_hi, wr_lo_ref[...]], axis=0), h_hi)
    lt = by_hi[:ROUTER_ROWS] + by_hi[ROUTER_ROWS:] + _dot_nt(wr_hi, h_lo) + br_ref[...]
    row = lambda r: lt[r:r + 1, :]

    gl = [row(g) for g in range(N_GROUPS)]
    gmax = functools.reduce(jnp.maximum, gl)
    gsel = jnp.where(gl[0] == gmax, 0.0, jnp.where(gl[1] == gmax, 1.0, jnp.where(gl[2] == gmax, 2.0, 3.0)))
    g_w = 1.0 / functools.reduce(jnp.add, [jnp.exp(x - gmax) for x in gl])

    el = []
    for e in range(EXPERTS_PER_GROUP):
        v = row(N_GROUPS + e)
        for g in range(1, N_GROUPS):
            v = jnp.where(gsel == float(g), row(N_GROUPS + EXPERTS_PER_GROUP * g + e), v)
        el.append(v)
    emax = functools.reduce(jnp.maximum, el)
    sel = []
    for e in range(EXPERTS_PER_GROUP):
        ahead = jnp.zeros_like(emax)
        for o in range(EXPERTS_PER_GROUP):
            if o < e:
                ahead = ahead + jnp.where(el[o] >= el[e], 1.0, 0.0)
            elif o > e:
                ahead = ahead + jnp.where(el[o] > el[e], 1.0, 0.0)
        sel.append(ahead < 2.0)
    pe = [jnp.where(sel[e], jnp.exp(el[e] - emax), 0.0) for e in range(EXPERTS_PER_GROUP)]
    scale = g_w / functools.reduce(jnp.add, pe)
    zero = jnp.zeros_like(emax)
    seen = zero
    lo_idx, hi_idx, c_lo, c_hi = zero, zero, zero, zero
    for e in range(EXPERTS_PER_GROUP):
        is_lo = sel[e] & (seen == 0.0)
        is_hi = sel[e] & (seen == 1.0)
        lo_idx = jnp.where(is_lo, float(e), lo_idx)
        hi_idx = jnp.where(is_hi, float(e), hi_idx)
        c_lo = jnp.where(is_lo, pe[e] * scale, c_lo)
        c_hi = jnp.where(is_hi, pe[e] * scale, c_hi)
        seen = seen + jnp.where(sel[e], 1.0, 0.0)
    pair = lo_idx * (7.0 - lo_idx) * 0.5 + (hi_idx - lo_idx - 1.0)
    bucket = gsel * float(N_PAIRS) + pair

    bid = lax.broadcasted_iota(jnp.int32, (ROUTER_ROWS, bucket.shape[1]), 0).astype(F32)
    onehot = jnp.where(bid == bucket, 1.0, 0.0)
    before = _dot(onehot.astype(BF16), tri_ref[...]) + carry_ref[...]
    rank = jnp.sum(onehot * before, axis=0, keepdims=True)
    carry = carry_ref[...] + jnp.sum(onehot, axis=1, keepdims=True)
    carry_ref[...] = carry
    cnt_ref[...] = carry
    meta = jnp.concatenate([bucket, rank, c_lo, c_hi, zero, zero, zero, zero], axis=0)
    meta_ref[...] = meta
    rec_ref[...] = jnp.concatenate([meta, jnp.zeros((LANE - meta.shape[0], meta.shape[1]), F32)], axis=0).T


def _outproj_router(a_list, w_out, x, n, norm_w, w_group, b_group, w_router, b_router):
    d = D_MODEL
    tm = ROUTER_TILE
    tok = lambda: pl.BlockSpec((tm, d), lambda i: (i, 0))
    wr = jnp.concatenate([w_group, w_router.transpose(1, 0, 2).reshape(d, N_EXPERTS)], axis=1)
    wr_t = jnp.zeros((ROUTER_ROWS, d), F32).at[:wr.shape[1]].set(wr.T)
    wr_hi = wr_t.astype(BF16)
    wr_lo = (wr_t - wr_hi.astype(F32)).astype(BF16)
    br = jnp.zeros((ROUTER_ROWS, 1), F32).at[:wr.shape[1], 0].set(
        jnp.concatenate([b_group, b_router.reshape(-1)]))
    tri = jnp.asarray(np.triu(np.ones((tm, tm), np.float32), 1), BF16)
    const = lambda shape: pl.BlockSpec(shape, lambda i: (0, 0))
    in_specs = [pl.BlockSpec((tm, a.shape[1]), lambda i: (i, 0)) for a in a_list]
    in_specs += [const((d, d)), tok(), const((1, d)),
                 const((ROUTER_ROWS, d)), const((ROUTER_ROWS, d)), const((ROUTER_ROWS, 1)), const((tm, tm))]
    return pl.pallas_call(
        functools.partial(_outproj_router_kernel, n_a=len(a_list)),
        grid=(n // tm,),
        in_specs=in_specs,
        out_specs=[tok(),
                   pl.BlockSpec((8, tm), lambda i: (0, i)),
                   pl.BlockSpec((tm, LANE), lambda i: (i, 0)),
                   const((ROUTER_ROWS, 1))],
        out_shape=[jax.ShapeDtypeStruct((n, d), F32),
                   jax.ShapeDtypeStruct((8, n), F32),
                   jax.ShapeDtypeStruct((n, LANE), F32),
                   jax.ShapeDtypeStruct((ROUTER_ROWS, 1), F32)],
        scratch_shapes=[pltpu.VMEM((ROUTER_ROWS, 1), F32)],
        compiler_params=_params("arbitrary"),
        name="outproj_router",
    )(*a_list, w_out, x, norm_w.reshape(1, d), wr_hi, wr_lo, br, tri)


def _gather_tokens(table, idx):
    m = idx.shape[0]
    info = plsc.get_sparse_core_info()
    n_workers = info.num_cores * info.num_subcores
    per_worker = m // n_workers
    n_chunks = per_worker // SC_CHUNK
    assert per_worker * n_workers == m and n_chunks * SC_CHUNK == per_worker and n_chunks % 2 == 0
    mesh = plsc.VectorSubcoreMesh(core_axis_name="core", subcore_axis_name="subcore")
    rows = pltpu.VMEM((SC_CHUNK,) + table.shape[1:], table.dtype)

    @functools.partial(
        pl.kernel, mesh=mesh,
        out_type=jax.ShapeDtypeStruct((m,) + table.shape[1:], table.dtype),
        scratch_types=[pltpu.VMEM((n_chunks, SC_CHUNK), jnp.int32), rows, rows,
                       pltpu.SemaphoreType.DMA, pltpu.SemaphoreType.DMA],
        compiler_params=pltpu.CompilerParams(use_tc_tiling_on_sc=True),
        name="gather_tokens")
    def gather(table_hbm, idx_hbm, out_hbm, idx_v, rows_a, rows_b, sem_a, sem_b):
        worker = lax.axis_index("subcore") * info.num_cores + lax.axis_index("core")
        bufs = ((rows_a, sem_a), (rows_b, sem_b))

        def fetch(j, buf):
            return pltpu.make_async_copy(table_hbm.at[idx_v.at[j]], bufs[buf][0], bufs[buf][1])

        pltpu.sync_copy(idx_hbm.at[worker], idx_v)
        fetch(0, 0).start()

        @pl.loop(0, n_chunks, step=2)
        def _(j0):
            for buf in range(2):
                j = j0 + buf
                fetch(j, buf).wait()

                @pl.when(j + 1 < n_chunks)
                def _():
                    fetch(j + 1, 1 - buf).start()

                off = pl.multiple_of(worker * per_worker + j * SC_CHUNK, SC_CHUNK)
                pltpu.sync_copy(bufs[buf][0], out_hbm.at[pl.ds(off, SC_CHUNK)])

    return gather(table, idx.reshape(n_workers, n_chunks, SC_CHUNK))


def _scatter_tokens(arrays, idx, m):
    n = idx.shape[0]
    na = len(arrays)
    info = plsc.get_sparse_core_info()
    n_workers = info.num_cores * info.num_subcores
    per_worker = n // n_workers
    n_chunks = per_worker // SC_CHUNK
    assert per_worker * n_workers == n and n_chunks * SC_CHUNK == per_worker and n_chunks % 2 == 0
    mesh = plsc.VectorSubcoreMesh(core_axis_name="core", subcore_axis_name="subcore")
    bufs = [pltpu.VMEM((SC_CHUNK,) + a.shape[1:], a.dtype) for a in arrays for _ in range(2)]

    @functools.partial(
        pl.kernel, mesh=mesh,
        out_type=[jax.ShapeDtypeStruct((m,) + a.shape[1:], a.dtype) for a in arrays],
        scratch_types=[pltpu.VMEM((n_chunks, SC_CHUNK), jnp.int32)] + bufs + [pltpu.SemaphoreType.DMA] * (2 * na),
        compiler_params=pltpu.CompilerParams(use_tc_tiling_on_sc=True),
        name="scatter_tokens")
    def scatter(*refs):
        ins, idx_hbm, outs = refs[:na], refs[na], refs[na + 1:2 * na + 1]
        idx_v = refs[2 * na + 1]
        buf_v = refs[2 * na + 2:4 * na + 2]
        sem_v = refs[4 * na + 2:]
        worker = lax.axis_index("subcore") * info.num_cores + lax.axis_index("core")

        def load(a, j, b):
            off = pl.multiple_of(worker * per_worker + j * SC_CHUNK, SC_CHUNK)
            return pltpu.make_async_copy(ins[a].at[pl.ds(off, SC_CHUNK)], buf_v[2 * a + b], sem_v[2 * a + b])

        pltpu.sync_copy(idx_hbm.at[worker], idx_v)
        for a in range(na):
            load(a, 0, 0).start()

        @pl.loop(0, n_chunks, step=2)
        def _(j0):
            for b in range(2):
                j = j0 + b
                for a in range(na):
                    load(a, j, b).wait()

                @pl.when(j + 1 < n_chunks)
                def _():
                    for a in range(na):
                        load(a, j + 1, 1 - b).start()

                for a in range(na):
                    pltpu.sync_copy(buf_v[2 * a + b], outs[a].at[idx_v.at[j]])

    return scatter(*arrays, idx.reshape(n_workers, n_chunks, SC_CHUNK))


def _moe_kernel(step_ids, e_lo, e_hi, n_valid, n_tiles, x_ref, rec_ref, nw_ref, *refs):
    del step_ids, e_lo, e_hi
    s = pl.program_id(0)
    o_ref = refs[-1]

    @pl.when(s * MOE_STEP_TILES < n_tiles[0])
    def _():
        for i in range(MOE_STEP_TILES):
            g_lo_ref, g_hi_ref, u_lo_ref, u_hi_ref, d_lo_ref, d_hi_ref = refs[6 * i:6 * (i + 1)]
            rows = pl.ds(i * MOE_TILE, MOE_TILE)
            valid = lax.broadcasted_iota(jnp.int32, (MOE_TILE, 1), 0) < n_valid[s * MOE_STEP_TILES + i]
            x = jnp.where(valid, x_ref[rows, :], 0.0)
            h = _rms(x, nw_ref[...]).astype(BF16)
            rec = jnp.where(valid, rec_ref[rows, :], 0.0)

            def expert(g_ref, u_ref, d_ref):
                act = _silu(_dot(h, g_ref[0])) * _dot(h, u_ref[0])
                return _dot(act.astype(BF16), d_ref[0])

            y = (rec[:, 2:3] * expert(g_lo_ref, u_lo_ref, d_lo_ref)
                 + rec[:, 3:4] * expert(g_hi_ref, u_hi_ref, d_hi_ref))
            o_ref[rows, :] = jnp.where(valid, x_ref[rows, :], 0.0) + y


def _moe(x2, n, meta, rec, cnt, norm_w, w_gate, w_up, w_down, layer):
    d = D_MODEL
    tm = MOE_TILE
    max_tiles = n // tm + N_BUCKETS
    n_slots = max_tiles * tm
    counts = cnt[:N_BUCKETS, 0].astype(jnp.int32)
    tiles_per_bucket = (counts + tm - 1) // tm
    tile_end = jnp.cumsum(tiles_per_bucket)
    tile_start = tile_end - tiles_per_bucket
    n_tiles = tile_end[-1]
    in_bucket = meta[0][:, None] == jnp.arange(N_BUCKETS, dtype=F32)[None, :]
    dest = meta[1].astype(jnp.int32) + jnp.sum(jnp.where(in_bucket, tile_start[None, :] * tm, 0), axis=1)
    x_sorted, rec_sorted = _scatter_tokens([x2, rec], dest, n_slots)
    first = layer * N_EXPERTS
    w_gate, w_up, w_down = (_to_bf16(w, first=first, count=N_EXPERTS) for w in (w_gate, w_up, w_down))

    k = MOE_STEP_TILES
    tile_raw = jnp.arange(max_tiles, dtype=jnp.int32)
    tile_ids = jnp.minimum(tile_raw, n_tiles - 1)
    step_ids = jnp.minimum(jnp.arange(max_tiles // k, dtype=jnp.int32), (n_tiles - 1) // k)
    of_bucket = lambda table: jnp.sum(jnp.where(tile_bucket[:, None] == jnp.arange(N_BUCKETS)[None, :], table[None, :], 0), axis=1)
    tile_bucket = jnp.sum((tile_ids[:, None] >= tile_end[None, :]).astype(jnp.int32), axis=1)
    n_valid = jnp.clip(of_bucket(counts) - (tile_ids - of_bucket(tile_start)) * tm, 0, tm)
    n_valid = jnp.where(tile_raw < n_tiles, n_valid, 0)
    group = tile_bucket // N_PAIRS
    pair = tile_bucket % N_PAIRS
    pair_lo = jnp.sum(jnp.where(pair[:, None] == jnp.arange(N_PAIRS)[None, :], jnp.asarray(PAIR_LO, jnp.int32)[None, :], 0), axis=1)
    pair_hi = jnp.sum(jnp.where(pair[:, None] == jnp.arange(N_PAIRS)[None, :], jnp.asarray(PAIR_HI, jnp.int32)[None, :], 0), axis=1)
    e_lo = group * EXPERTS_PER_GROUP + pair_lo
    e_hi = group * EXPERTS_PER_GROUP + pair_hi

    step_rows = lambda s, sid, lo, hi, nv, nt: (sid[s], 0)
    weight_specs, weight_args = [], []
    for i in range(k):
        wlo = lambda s, sid, lo, hi, nv, nt, i=i: (lo[s * k + i], 0, 0)
        whi = lambda s, sid, lo, hi, nv, nt, i=i: (hi[s * k + i], 0, 0)
        weight_specs += [pl.BlockSpec((1, d, D_EXPERT), wlo), pl.BlockSpec((1, d, D_EXPERT), whi),
                         pl.BlockSpec((1, d, D_EXPERT), wlo), pl.BlockSpec((1, d, D_EXPERT), whi),
                         pl.BlockSpec((1, D_EXPERT, d), wlo), pl.BlockSpec((1, D_EXPERT, d), whi)]
        weight_args += [w_gate, w_gate, w_up, w_up, w_down, w_down]
    grid_spec = pltpu.PrefetchScalarGridSpec(
        num_scalar_prefetch=5,
        grid=(max_tiles // k,),
        in_specs=[pl.BlockSpec((k * tm, d), step_rows), pl.BlockSpec((k * tm, LANE), step_rows),
                  pl.BlockSpec((1, d), lambda s, sid, lo, hi, nv, nt: (0, 0))] + weight_specs,
        out_specs=pl.BlockSpec((k * tm, d), step_rows))
    out_sorted = pl.pallas_call(
        _moe_kernel,
        grid_spec=grid_spec,
        out_shape=jax.ShapeDtypeStruct((n_slots, d), F32),
        compiler_params=_params("arbitrary"),
        name="moe",
    )(step_ids, e_lo, e_hi, n_valid, n_tiles.reshape(1), x_sorted, rec_sorted, norm_w.reshape(1, d), *weight_args)
    return _gather_tokens(out_sorted, dest)


def kernel(x, attn_norm, ffn_norm, even_w_in, even_w_out, hg_lb_logits, hg_o_norm, da_q_norm, da_k_norm, da_lambda, da_subln, odd_w_in, gla_w_alpha, gla_b_alpha, gla_o_norm, odd_w_out, moe_w_group, moe_b_group, moe_w_router, moe_b_router, moe_w_gate, moe_w_up, moe_w_down):
    batch, seq, d = x.shape
    depth = attn_norm.shape[0]
    n = batch * seq
    xf = x.reshape(n, d)

    lb_sm = jax.nn.softmax(hg_lb_logits.astype(F32), axis=0)
    lb_all = jnp.cumsum(lb_sm, axis=0) - lb_sm[0:1]
    rank = gla_w_alpha.shape[1]

    experts = lambda w: w.reshape((-1,) + w.shape[2:])
    moe_w_gate, moe_w_up, moe_w_down = experts(moe_w_gate), experts(moe_w_up), experts(moe_w_down)

    for layer in range(depth):
        j = layer // 2
        if layer % 2 == 0:
            lam_init = 0.8 - 0.6 * math.exp(-0.3 * layer)
            proj = _inproj(xf, n, attn_norm[layer], _to_bf16(even_w_in, first=j, count=1)[0])
            o_a = _hgrn(proj, lb_all[j], hg_o_norm[j], batch, seq)
            o_b = _diff_attn(proj, da_q_norm[j], da_k_norm[j], da_lambda[j], da_subln[j], lam_init, batch, seq)
            mix = [o_a, o_b]
            w_out = _to_bf16(even_w_out, first=j, count=1)[0]
        else:
            w_alpha = jnp.pad(gla_w_alpha[j], ((0, GLA_RANK_PAD - rank), (0, 0))).astype(BF16)
            w_in = _to_bf16(jnp.swapaxes(odd_w_in, 1, 2), odd_w_in.shape[2] + GLA_RANK_PAD - rank, first=j, count=1)[0]
            proj = _inproj(xf, n, attn_norm[layer], w_in, w_transposed=True)
            mix = [_gla(proj, w_alpha, gla_b_alpha[j], gla_o_norm[j], batch, seq)]
            w_out = _to_bf16(odd_w_out, first=j, count=1)[0]
        x2, meta, rec, cnt = _outproj_router(mix, w_out, xf, n, ffn_norm[layer], moe_w_group[layer],
                                        moe_b_group[layer], moe_w_router[layer], moe_b_router[layer])
        xf = _moe(x2, n, meta, rec, cnt, ffn_norm[layer], moe_w_gate, moe_w_up, moe_w_down, layer)
    return xf.reshape(batch, seq, d)
```

```python
import functools
import math

import numpy as np
import jax
import jax.numpy as jnp
from jax import lax
from jax.experimental import pallas as pl
from jax.experimental.pallas import tpu as pltpu
from jax.experimental.pallas import tpu_sc as plsc

F32 = jnp.float32
BF16 = jnp.bfloat16
EPS = 1e-6
LOG2_E = 1.4426950408889634

D_MODEL = 1024
LANE = 128
HEADS = 4
DK = 128
HG_WIDTH = 512
DA_HEAD_DIM = 64
GLA_DV = 256
GLA_RANK_PAD = 128
GLA_TAU = 16.0
N_GROUPS = 4
EXPERTS_PER_GROUP = 4
N_EXPERTS = 16
D_EXPERT = 512
N_PAIRS = 6
N_BUCKETS = N_GROUPS * N_PAIRS
ROUTER_ROWS = 32

CHUNK = 128
GLA_STEP_CHUNKS = 4
LEVELS = 7
ATTN_BLOCK = 512
ATTN_HEADS = 4
SCORE_BOUND_MARGIN = 1.05
MAX_SCORE_BOUND = 60.0
ROW_TILE = 512
ROUTER_TILE = 1024
MOE_TILE = 256
MOE_STEP_TILES = 2
SC_CHUNK = 32
VMEM_LIMIT = 56 * 1024 * 1024
CAST_BLOCK_BYTES = 8 * 1024 * 1024

PAIR_LO = (0, 0, 0, 1, 1, 2)
PAIR_HI = (1, 2, 3, 2, 3, 3)


def _dot(a, b):
    return jnp.dot(a, b, preferred_element_type=F32)


def _dot_nt(a, b):
    return lax.dot_general(a, b, (((1,), (1,)), ((), ())), preferred_element_type=F32)


def _dot_tn(a, b):
    return lax.dot_general(a, b, (((0,), (0,)), ((), ())), preferred_element_type=F32)


def _rms(x, w):
    return x * lax.rsqrt(jnp.mean(x * x, axis=-1, keepdims=True) + EPS) * w


def _silu(x):
    return x * jax.nn.sigmoid(x)


def _params(*sem):
    return pltpu.CompilerParams(dimension_semantics=sem, vmem_limit_bytes=VMEM_LIMIT)


def _cast_kernel(w_ref, o_ref):
    k = w_ref.shape[1]
    o_ref[:, :k, :] = w_ref[...].astype(BF16)
    if o_ref.shape[1] > k:
        o_ref[:, k:, :] = jnp.zeros((o_ref.shape[0], o_ref.shape[1] - k, o_ref.shape[2]), BF16)


def _to_bf16(w, k_pad=None, first=0, count=None):
    _, k, n = w.shape
    l = count or w.shape[0]
    k_pad = k_pad or k
    per_matrix = k * n * 4
    lb = max([c for c in range(1, l + 1) if l % c == 0 and first % c == 0 and c * per_matrix <= CAST_BLOCK_BYTES] or [1])
    nb = n
    while lb == 1 and k * nb * 4 > CAST_BLOCK_BYTES and nb % (2 * LANE) == 0:
        nb //= 2
    return pl.pallas_call(
        _cast_kernel,
        grid=(l // lb, n // nb),
        in_specs=[pl.BlockSpec((lb, k, nb), lambda i, j: (first // lb + i, 0, j))],
        out_specs=pl.BlockSpec((lb, k_pad, nb), lambda i, j: (i, 0, j)),
        out_shape=jax.ShapeDtypeStruct((l, k_pad, n), BF16),
        compiler_params=_params("parallel", "parallel"),
        name="to_bf16",
    )(w)


def _inproj_kernel(x_ref, nw_ref, w_ref, o_ref, *, w_transposed):
    h = _rms(x_ref[...], nw_ref[...]).astype(BF16)
    o_ref[...] = _dot_nt(h, w_ref[...]) if w_transposed else _dot(h, w_ref[...])


def _inproj(x, n, norm_w, w, w_transposed=False):
    d = D_MODEL
    n_out = w.shape[0] if w_transposed else w.shape[1]
    return pl.pallas_call(
        functools.partial(_inproj_kernel, w_transposed=w_transposed),
        grid=(n // ROW_TILE,),
        in_specs=[pl.BlockSpec((ROW_TILE, d), lambda i: (i, 0)),
                  pl.BlockSpec((1, d), lambda i: (0, 0)),
                  pl.BlockSpec(w.shape, lambda i: (0, 0))],
        out_specs=pl.BlockSpec((ROW_TILE, n_out), lambda i: (i, 0)),
        out_shape=jax.ShapeDtypeStruct((n, n_out), F32),
        compiler_params=_params("parallel"),
        name="inproj",
    )(x, norm_w.reshape(1, d), w)


def _decay_tables():
    c = CHUNK
    a = np.zeros((LEVELS + 2, c, c), np.float32)
    t = np.arange(c)
    for l in range(LEVELS):
        s = 1 << l
        for r in range(c):
            mid = (r // (2 * s)) * 2 * s + s - 1
            if (r >> l) & 1:
                a[l, r] = (t > mid) & (t <= r)
            else:
                a[l, r] = (t > r) & (t <= mid)
    a[LEVELS] = t[None, :] <= t[:, None]
    a[LEVELS + 1] = t[None, :] > t[:, None]
    a = a.reshape((LEVELS + 2) * c, c)
    a3 = np.concatenate([a, a], axis=1)
    i, j = np.meshgrid(t, t, indexing="ij")
    x = np.maximum(i ^ j, 1)
    lvl = np.where(j < i, np.floor(np.log2(x)).astype(np.int32), np.where(i == j, LEVELS, -1))
    return jnp.asarray(a3, BF16), jnp.asarray(lvl, jnp.int32)


def _decay_factors(a3, g):
    g_hi = g.astype(BF16)
    g_lo = (g - g_hi.astype(F32)).astype(BF16)
    return jnp.exp(_dot(a3, jnp.concatenate([g_hi, g_lo], axis=0)))


def _gla_chunk(q, k, v, f, st_ref, lvl, rowid):
    c = CHUNK
    sc = jnp.where(lvl == LEVELS, jnp.sum(q * k, axis=-1, keepdims=True), 0.0)
    for l in range(LEVELS):
        s = 1 << l
        if s % 8 == 0:
            qk = jnp.concatenate([(q if b & 1 else k)[b * s:(b + 1) * s] for b in range(c // s)], axis=0)
        else:
            qk = jnp.where(((rowid >> l) & 1) == 1, q, k)
        m = (qk * f[l * c:(l + 1) * c]).astype(BF16)
        sc = jnp.where(lvl == l, _dot_nt(m, m), sc)
    qd = (q * f[LEVELS * c:(LEVELS + 1) * c]).astype(BF16)
    kd = (k * f[(LEVELS + 1) * c:(LEVELS + 2) * c]).astype(BF16)
    vb = v.astype(BF16)
    st = st_ref[...]
    o = _dot(sc.astype(BF16), vb) + _dot_nt(qd, st.astype(BF16))
    chunk_decay = f[(LEVELS + 1) * c - 1:(LEVELS + 1) * c, :]
    st_ref[...] = st * chunk_decay + _dot_tn(vb, kd)
    return o


def _hgrn_kernel(q_ref, f_ref, i_ref, g_ref, lb_ref, on_ref, a3_ref, lvl_ref, o_ref, st_ref):
    @pl.when(pl.program_id(1) == 0)
    def _():
        st_ref[...] = jnp.zeros_like(st_ref)

    lvl = lvl_ref[...]
    rowid = lax.broadcasted_iota(jnp.int32, (CHUNK, 1), 0)
    lb = lb_ref[...]
    for c in range(GLA_STEP_CHUNKS):
        rows = slice(c * CHUNK, (c + 1) * CHUNK)
        fl = f_ref[rows, :]
        f = _decay_factors(a3_ref[...], jnp.log(lb + (1.0 - lb) * jax.nn.sigmoid(fl)))
        key_in = (1.0 - lb) * jax.nn.sigmoid(-fl)
        for h in range(HEADS):
            sl = slice(h * DK, (h + 1) * DK)
            q = _silu(q_ref[rows, sl]) * (DK ** -0.5)
            o = _gla_chunk(q, key_in[:, sl], i_ref[rows, sl], f[:, sl], st_ref.at[h], lvl, rowid)
            o = _rms(o, on_ref[...])
            o_ref[rows, sl] = (o * _silu(g_ref[rows, sl])).astype(BF16)


def _hgrn(proj, lb, o_norm, batch, seq):
    n = proj.shape[0]
    rows = CHUNK * GLA_STEP_CHUNKS
    nt = seq // rows
    a3, lvl = _decay_tables()
    col = lambda j: pl.BlockSpec((rows, HG_WIDTH), lambda b, t: (b * nt + t, j))
    const = lambda shape: pl.BlockSpec(shape, lambda b, t: (0, 0))
    return pl.pallas_call(
        _hgrn_kernel,
        grid=(batch, nt),
        in_specs=[col(0), col(1), col(2), col(3), const((1, HG_WIDTH)), const((1, DK)),
                  const(a3.shape), const(lvl.shape)],
        out_specs=pl.BlockSpec((rows, HG_WIDTH), lambda b, t: (b * nt + t, 0)),
        out_shape=jax.ShapeDtypeStruct((n, HG_WIDTH), BF16),
        scratch_shapes=[pltpu.VMEM((HEADS, DK, DK), F32)],
        compiler_params=_params("parallel", "arbitrary"),
        name="hgrn",
    )(proj, proj, proj, proj, lb.reshape(1, HG_WIDTH), o_norm.reshape(1, DK), a3, lvl)


def _log_sigmoid(x):
    return jnp.minimum(x, 0.0) - jnp.log1p(jnp.exp(-jnp.abs(x)))


def _gla_kernel(q_ref, k_ref, v_ref, g_ref, a_ref, wa_ref, ba_ref, on_ref, a3_ref, lvl_ref, o_ref, st_ref):
    @pl.when(pl.program_id(1) == 0)
    def _():
        st_ref[...] = jnp.zeros_like(st_ref)

    lvl = lvl_ref[...]
    rowid = lax.broadcasted_iota(jnp.int32, (CHUNK, 1), 0)
    for c in range(GLA_STEP_CHUNKS):
        rows = slice(c * CHUNK, (c + 1) * CHUNK)
        gate_logit = _dot(a_ref[rows, :].astype(BF16), wa_ref[...]) + ba_ref[...]
        f = _decay_factors(a3_ref[...], _log_sigmoid(gate_logit) * (1.0 / GLA_TAU))
        for h in range(HEADS):
            sl = slice(h * DK, (h + 1) * DK)
            sv = slice(h * GLA_DV, (h + 1) * GLA_DV)
            q = q_ref[rows, sl] * (DK ** -0.5)
            o = _gla_chunk(q, k_ref[rows, sl], v_ref[rows, sv], f[:, sl], st_ref.at[h], lvl, rowid)
            o = _rms(o, on_ref[...])
            o_ref[rows, sv] = (o * _silu(g_ref[rows, sv])).astype(BF16)


def _gla(proj, w_alpha, b_alpha, o_norm, batch, seq):
    n = proj.shape[0]
    rows = CHUNK * GLA_STEP_CHUNKS
    nt = seq // rows
    qk_w = HEADS * DK
    v_w = HEADS * GLA_DV
    a3, lvl = _decay_tables()
    row = lambda b, t: b * nt + t
    const = lambda shape: pl.BlockSpec(shape, lambda b, t: (0, 0))
    return pl.pallas_call(
        _gla_kernel,
        grid=(batch, nt),
        in_specs=[pl.BlockSpec((rows, qk_w), lambda b, t: (row(b, t), 0)),
                  pl.BlockSpec((rows, qk_w), lambda b, t: (row(b, t), 1)),
                  pl.BlockSpec((rows, v_w), lambda b, t: (row(b, t), 1)),
                  pl.BlockSpec((rows, v_w), lambda b, t: (row(b, t), 2)),
                  pl.BlockSpec((rows, GLA_RANK_PAD), lambda b, t: (row(b, t), (2 * qk_w + 2 * v_w) // GLA_RANK_PAD)),
                  const((GLA_RANK_PAD, qk_w)), const((1, qk_w)), const((1, GLA_DV)),
                  const(a3.shape), const(lvl.shape)],
        out_specs=pl.BlockSpec((rows, v_w), lambda b, t: (row(b, t), 0)),
        out_shape=jax.ShapeDtypeStruct((n, v_w), BF16),
        scratch_shapes=[pltpu.VMEM((HEADS, GLA_DV, DK), F32)],
        compiler_params=_params("parallel", "arbitrary"),
        name="gla",
    )(proj, proj, proj, proj, proj, w_alpha, b_alpha.reshape(1, qk_w), o_norm.reshape(1, GLA_DV), a3, lvl)


def _qk_norm(x, w2):
    lane = lax.broadcasted_iota(jnp.int32, x.shape, 1)
    first = lane < DA_HEAD_DIM
    sq = x * x
    ms1 = jnp.sum(jnp.where(first, sq, 0.0), axis=-1, keepdims=True) * (1.0 / DA_HEAD_DIM)
    ms2 = jnp.sum(jnp.where(first, 0.0, sq), axis=-1, keepdims=True) * (1.0 / DA_HEAD_DIM)
    r = jnp.where(first, lax.rsqrt(ms1 + EPS), lax.rsqrt(ms2 + EPS))
    return x * r * w2


def _score_bound(q_norm_w, k_norm_w):
    return (jnp.max(jnp.abs(q_norm_w)) * jnp.max(jnp.abs(k_norm_w))
            * (DA_HEAD_DIM * DA_HEAD_DIM ** -0.5 * LOG2_E * SCORE_BOUND_MARGIN))


def _attn_kernel(qi_tab, ki_tab, q_ref, k_ref, v_ref, qn_ref, kn_ref, lam_ref, sub_ref, o_ref,
                 qs_ref, m_ref, l_ref, acc_ref, *, lam_init, bounded):
    p = pl.program_id(2)
    qi = qi_tab[p]
    ki = ki_tab[p]
    tq = ATTN_BLOCK
    dv = 2 * DA_HEAD_DIM
    heads = [slice(h * dv, (h + 1) * dv) for h in range(ATTN_HEADS)]

    @pl.when(ki == 0)
    def _():
        for h, sl in enumerate(heads):
            qn = _qk_norm(q_ref[:, sl], qn_ref[...])
            qt = (qn * (DA_HEAD_DIM ** -0.5 * LOG2_E)).T
            top = lax.broadcasted_iota(jnp.int32, qt.shape, 0) < DA_HEAD_DIM
            q1 = jnp.where(top, qt, 0.0).astype(BF16)
            q2 = jnp.where(top, 0.0, qt).astype(BF16)
            for copy in range(1 if bounded else 2):
                qs_ref[h, copy, :, :tq] = q1
                qs_ref[h, copy, :, tq:] = q2
            if bounded:
                bound = _score_bound(qn_ref[...], kn_ref[...])
                row = lax.broadcasted_iota(jnp.int32, (dv, 2 * tq), 0)
                qs_ref[h, 1] = jnp.where(row == 0, -bound, 0.0).astype(BF16)
        if not bounded:
            m_ref[...] = jnp.full_like(m_ref, -jnp.inf)
        l_ref[...] = jnp.zeros_like(l_ref)
        acc_ref[...] = jnp.zeros_like(acc_ref)

    def step(masked):
        tk = k_ref.shape[0]
        ones = jnp.ones((16, tk), BF16)
        unit = jnp.where(lax.broadcasted_iota(jnp.int32, (tk, dv), 1) == 0, 1.0, 0.0).astype(BF16)
        for h, sl in enumerate(heads):
            kn = _qk_norm(k_ref[:, sl], kn_ref[...]).astype(BF16)

            def mask(s):
                if not masked:
                    return s
                key = lax.broadcasted_iota(jnp.int32, s.shape, 0)
                qry = lax.broadcasted_iota(jnp.int32, s.shape, 1)
                qry = jnp.where(qry >= tq, qry - tq, qry)
                return jnp.where(key <= qry, s, -jnp.inf)

            if bounded:
                shifted = mask(_dot(jnp.concatenate([kn, unit], axis=1), qs_ref[h].reshape(2 * dv, 2 * tq)))
                alpha = 1.0
            else:
                m_prev = m_ref[h]
                m_new = jnp.maximum(m_prev, jnp.max(mask(_dot(kn, qs_ref[h, 0])), axis=0, keepdims=True))
                alpha = jnp.exp2(m_prev - m_new)
                shifted = mask(_dot(kn, qs_ref[h, 1])) - m_new
                m_ref[h] = m_new
            pr = jnp.exp2(shifted).astype(BF16) if bounded else jnp.exp2(shifted.astype(BF16))
            vt = jnp.concatenate([v_ref[:, sl].T.astype(BF16), ones], axis=0)
            pv = _dot(vt, pr)
            l_ref[h] = alpha * l_ref[h] + pv[dv:dv + 1]
            acc_ref[h] = alpha * acc_ref[h] + pv[:dv]

    @pl.when(ki < qi)
    def _():
        step(False)

    @pl.when(ki == qi)
    def _():
        step(True)
        lv = lam_ref[...]
        lam = (jnp.exp(jnp.sum(lv[0:1] * lv[1:2], axis=-1, keepdims=True))
               - jnp.exp(jnp.sum(lv[2:3] * lv[3:4], axis=-1, keepdims=True)) + lam_init)
        for h, sl in enumerate(heads):
            o1 = acc_ref[h, :, :tq] / l_ref[h, :, :tq]
            o2 = acc_ref[h, :, tq:] / l_ref[h, :, tq:]
            o = o1 - lam * o2
            o = o * lax.rsqrt(jnp.mean(o * o, axis=0, keepdims=True) + EPS) * sub_ref[...] * (1.0 - lam_init)
            o_ref[:, sl] = o.T.astype(BF16)


def _diff_attn(proj, q_norm, k_norm, lam_vecs, subln, lam_init, batch, seq):
    n = proj.shape[0]
    blk = ATTN_BLOCK
    nq = seq // blk
    dv = 2 * DA_HEAD_DIM
    pairs = [(qi, ki) for qi in range(nq) for ki in range(qi + 1)]
    qi_tab = jnp.asarray([p[0] for p in pairs], jnp.int32)
    ki_tab = jnp.asarray([p[1] for p in pairs], jnp.int32)
    width = ATTN_HEADS * dv
    groups = HEADS // ATTN_HEADS
    q_col = (4 * HG_WIDTH) // width
    const = lambda shape: pl.BlockSpec(shape, lambda b, g, p, qt, kt: (0, 0))
    grid_spec = pltpu.PrefetchScalarGridSpec(
        num_scalar_prefetch=2,
        grid=(batch, groups, len(pairs)),
        in_specs=[pl.BlockSpec((blk, width), lambda b, g, p, qt, kt: (b * nq + qt[p], q_col + g)),
                  pl.BlockSpec((blk, width), lambda b, g, p, qt, kt: (b * nq + kt[p], q_col + groups + g)),
                  pl.BlockSpec((blk, width), lambda b, g, p, qt, kt: (b * nq + kt[p], q_col + 2 * groups + g)),
                  const((1, dv)), const((1, dv)), const((4, DA_HEAD_DIM)), const((dv, 1))],
        out_specs=pl.BlockSpec((blk, width), lambda b, g, p, qt, kt: (b * nq + qt[p], g)),
        scratch_shapes=[pltpu.VMEM((ATTN_HEADS, 2, dv, 2 * blk), BF16),
                        pltpu.VMEM((ATTN_HEADS, 1, 2 * blk), F32),
                        pltpu.VMEM((ATTN_HEADS, 1, 2 * blk), F32),
                        pltpu.VMEM((ATTN_HEADS, dv, 2 * blk), F32)])
    args = (qi_tab, ki_tab, proj, proj, proj, jnp.tile(q_norm, 2).reshape(1, -1), jnp.tile(k_norm, 2).reshape(1, -1),
            lam_vecs, subln.reshape(-1, 1))

    def run(bounded):
        return pl.pallas_call(
            functools.partial(_attn_kernel, lam_init=lam_init, bounded=bounded),
            grid_spec=grid_spec,
            out_shape=jax.ShapeDtypeStruct((n, HEADS * dv), BF16),
            compiler_params=_params("parallel", "parallel", "arbitrary"),
            name="diff_attn_bounded" if bounded else "diff_attn",
        )(*args)

    return lax.cond(_score_bound(q_norm, k_norm) <= MAX_SCORE_BOUND, lambda: run(True), lambda: run(False))


def _outproj_router_kernel(*refs, n_a):
    a_refs = refs[:n_a]
    (w_ref, x_ref, nw_ref, wr_hi_ref, wr_lo_ref, br_ref, tri_ref,
     x2_ref, meta_ref, rec_ref, cnt_ref, carry_ref) = refs[n_a:]

    @pl.when(pl.program_id(0) == 0)
    def _():
        carry_ref[...] = jnp.zeros_like(carry_ref)

    x2 = x_ref[...]
    off = 0
    for a_ref in a_refs:
        kk = a_ref.shape[1]
        x2 = x2 + _dot(a_ref[...], w_ref[off:off + kk, :])
        off += kk
    x2_ref[...] = x2

    h = _rms(x2, nw_ref[...])
    h_hi = h.astype(BF16)
    h_lo = (h - h_hi.astype(F32)).astype(BF16)
    wr_hi = wr_hi_ref[...]
    by_hi = _dot_nt(jnp.concatenate([wr_hi, wr_lo_ref[...]], axis=0), h_hi)
    lt = by_hi[:ROUTER_ROWS] + by_hi[ROUTER_ROWS:] + _dot_nt(wr_hi, h_lo) + br_ref[...]
    row = lambda r: lt[r:r + 1, :]

    gl = [row(g) for g in range(N_GROUPS)]
    gmax = functools.reduce(jnp.maximum, gl)
    gsel = jnp.where(gl[0] == gmax, 0.0, jnp.where(gl[1] == gmax, 1.0, jnp.where(gl[2] == gmax, 2.0, 3.0)))
    g_w = 1.0 / functools.reduce(jnp.add, [jnp.exp(x - gmax) for x in gl])

    el = []
    for e in range(EXPERTS_PER_GROUP):
        v = row(N_GROUPS + e)
        for g in range(1, N_GROUPS):
            v = jnp.where(gsel == float(g), row(N_GROUPS + EXPERTS_PER_GROUP * g + e), v)
        el.append(v)
    emax = functools.reduce(jnp.maximum, el)
    sel = []
    for e in range(EXPERTS_PER_GROUP):
        ahead = jnp.zeros_like(emax)
        for o in range(EXPERTS_PER_GROUP):
            if o < e:
                ahead = ahead + jnp.where(el[o] >= el[e], 1.0, 0.0)
            elif o > e:
                ahead = ahead + jnp.where(el[o] > el[e], 1.0, 0.0)
        sel.append(ahead < 2.0)
    pe = [jnp.where(sel[e], jnp.exp(el[e] - emax), 0.0) for e in range(EXPERTS_PER_GROUP)]
    scale = g_w / functools.reduce(jnp.add, pe)
    zero = jnp.zeros_like(emax)
    seen = zero
    lo_idx, hi_idx, c_lo, c_hi = zero, zero, zero, zero
    for e in range(EXPERTS_PER_GROUP):
        is_lo = sel[e] & (seen == 0.0)
        is_hi = sel[e] & (seen == 1.0)
        lo_idx = jnp.where(is_lo, float(e), lo_idx)
        hi_idx = jnp.where(is_hi, float(e), hi_idx)
        c_lo = jnp.where(is_lo, pe[e] * scale, c_lo)
        c_hi = jnp.where(is_hi, pe[e] * scale, c_hi)
        seen = seen + jnp.where(sel[e], 1.0, 0.0)
    pair = lo_idx * (7.0 - lo_idx) * 0.5 + (hi_idx - lo_idx - 1.0)
    bucket = gsel * float(N_PAIRS) + pair

    bid = lax.broadcasted_iota(jnp.int32, (ROUTER_ROWS, bucket.shape[1]), 0).astype(F32)
    onehot = jnp.where(bid == bucket, 1.0, 0.0)
    before = _dot(onehot.astype(BF16), tri_ref[...]) + carry_ref[...]
    rank = jnp.sum(onehot * before, axis=0, keepdims=True)
    carry = carry_ref[...] + jnp.sum(onehot, axis=1, keepdims=True)
    carry_ref[...] = carry
    cnt_ref[...] = carry
    meta = jnp.concatenate([bucket, rank, c_lo, c_hi, zero, zero, zero, zero], axis=0)
    meta_ref[...] = meta
    rec_ref[...] = jnp.concatenate([meta, jnp.zeros((LANE - meta.shape[0], meta.shape[1]), F32)], axis=0).T


def _outproj_router(a_list, w_out, x, n, norm_w, w_group, b_group, w_router, b_router):
    d = D_MODEL
    tm = ROUTER_TILE
    tok = lambda: pl.BlockSpec((tm, d), lambda i: (i, 0))
    wr = jnp.concatenate([w_group, w_router.transpose(1, 0, 2).reshape(d, N_EXPERTS)], axis=1)
    wr_t = jnp.zeros((ROUTER_ROWS, d), F32).at[:wr.shape[1]].set(wr.T)
    wr_hi = wr_t.astype(BF16)
    wr_lo = (wr_t - wr_hi.astype(F32)).astype(BF16)
    br = jnp.zeros((ROUTER_ROWS, 1), F32).at[:wr.shape[1], 0].set(
        jnp.concatenate([b_group, b_router.reshape(-1)]))
    tri = jnp.asarray(np.triu(np.ones((tm, tm), np.float32), 1), BF16)
    const = lambda shape: pl.BlockSpec(shape, lambda i: (0, 0))
    in_specs = [pl.BlockSpec((tm, a.shape[1]), lambda i: (i, 0)) for a in a_list]
    in_specs += [const((d, d)), tok(), const((1, d)),
                 const((ROUTER_ROWS, d)), const((ROUTER_ROWS, d)), const((ROUTER_ROWS, 1)), const((tm, tm))]
    return pl.pallas_call(
        functools.partial(_outproj_router_kernel, n_a=len(a_list)),
        grid=(n // tm,),
        in_specs=in_specs,
        out_specs=[tok(),
                   pl.BlockSpec((8, tm), lambda i: (0, i)),
                   pl.BlockSpec((tm, LANE), lambda i: (i, 0)),
                   const((ROUTER_ROWS, 1))],
        out_shape=[jax.ShapeDtypeStruct((n, d), F32),
                   jax.ShapeDtypeStruct((8, n), F32),
                   jax.ShapeDtypeStruct((n, LANE), F32),
                   jax.ShapeDtypeStruct((ROUTER_ROWS, 1), F32)],
        scratch_shapes=[pltpu.VMEM((ROUTER_ROWS, 1), F32)],
        compiler_params=_params("arbitrary"),
        name="outproj_router",
    )(*a_list, w_out, x, norm_w.reshape(1, d), wr_hi, wr_lo, br, tri)


def _gather_tokens(table, idx):
    m = idx.shape[0]
    info = plsc.get_sparse_core_info()
    n_workers = info.num_cores * info.num_subcores
    per_worker = m // n_workers
    n_chunks = per_worker // SC_CHUNK
    assert per_worker * n_workers == m and n_chunks * SC_CHUNK == per_worker and n_chunks % 2 == 0
    mesh = plsc.VectorSubcoreMesh(core_axis_name="core", subcore_axis_name="subcore")
    rows = pltpu.VMEM((SC_CHUNK,) + table.shape[1:], table.dtype)

    @functools.partial(
        pl.kernel, mesh=mesh,
        out_type=jax.ShapeDtypeStruct((m,) + table.shape[1:], table.dtype),
        scratch_types=[pltpu.VMEM((n_chunks, SC_CHUNK), jnp.int32), rows, rows,
                       pltpu.SemaphoreType.DMA, pltpu.SemaphoreType.DMA],
        compiler_params=pltpu.CompilerParams(use_tc_tiling_on_sc=True),
        name="gather_tokens")
    def gather(table_hbm, idx_hbm, out_hbm, idx_v, rows_a, rows_b, sem_a, sem_b):
        worker = lax.axis_index("subcore") * info.num_cores + lax.axis_index("core")
        bufs = ((rows_a, sem_a), (rows_b, sem_b))

        def fetch(j, buf):
            return pltpu.make_async_copy(table_hbm.at[idx_v.at[j]], bufs[buf][0], bufs[buf][1])

        pltpu.sync_copy(idx_hbm.at[worker], idx_v)
        fetch(0, 0).start()

        @pl.loop(0, n_chunks, step=2)
        def _(j0):
            for buf in range(2):
                j = j0 + buf
                fetch(j, buf).wait()

                @pl.when(j + 1 < n_chunks)
                def _():
                    fetch(j + 1, 1 - buf).start()

                off = pl.multiple_of(worker * per_worker + j * SC_CHUNK, SC_CHUNK)
                pltpu.sync_copy(bufs[buf][0], out_hbm.at[pl.ds(off, SC_CHUNK)])

    return gather(table, idx.reshape(n_workers, n_chunks, SC_CHUNK))


def _scatter_tokens(arrays, idx, m):
    n = idx.shape[0]
    na = len(arrays)
    info = plsc.get_sparse_core_info()
    n_workers = info.num_cores * info.num_subcores
    per_worker = n // n_workers
    n_chunks = per_worker // SC_CHUNK
    assert per_worker * n_workers == n and n_chunks * SC_CHUNK == per_worker and n_chunks % 2 == 0
    mesh = plsc.VectorSubcoreMesh(core_axis_name="core", subcore_axis_name="subcore")
    bufs = [pltpu.VMEM((SC_CHUNK,) + a.shape[1:], a.dtype) for a in arrays for _ in range(2)]

    @functools.partial(
        pl.kernel, mesh=mesh,
        out_type=[jax.ShapeDtypeStruct((m,) + a.shape[1:], a.dtype) for a in arrays],
        scratch_types=[pltpu.VMEM((n_chunks, SC_CHUNK), jnp.int32)] + bufs + [pltpu.SemaphoreType.DMA] * (2 * na),
        compiler_params=pltpu.CompilerParams(use_tc_tiling_on_sc=True),
        name="scatter_tokens")
    def scatter(*refs):
        ins, idx_hbm, outs = refs[:na], refs[na], refs[na + 1:2 * na + 1]
        idx_v = refs[2 * na + 1]
        buf_v = refs[2 * na + 2:4 * na + 2]
        sem_v = refs[4 * na + 2:]
        worker = lax.axis_index("subcore") * info.num_cores + lax.axis_index("core")

        def load(a, j, b):
            off = pl.multiple_of(worker * per_worker + j * SC_CHUNK, SC_CHUNK)
            return pltpu.make_async_copy(ins[a].at[pl.ds(off, SC_CHUNK)], buf_v[2 * a + b], sem_v[2 * a + b])

        pltpu.sync_copy(idx_hbm.at[worker], idx_v)
        for a in range(na):
            load(a, 0, 0).start()

        @pl.loop(0, n_chunks, step=2)
        def _(j0):
            for b in range(2):
                j = j0 + b
                for a in range(na):
                    load(a, j, b).wait()

                @pl.when(j + 1 < n_chunks)
                def _():
                    for a in range(na):
                        load(a, j + 1, 1 - b).start()

                for a in range(na):
                    pltpu.sync_copy(buf_v[2 * a + b], outs[a].at[idx_v.at[j]])

    return scatter(*arrays, idx.reshape(n_workers, n_chunks, SC_CHUNK))


def _moe_kernel(step_ids, e_lo, e_hi, n_valid, n_tiles, x_ref, rec_ref, nw_ref, *refs):
    del step_ids, e_lo, e_hi
    s = pl.program_id(0)
    o_ref = refs[-1]

    @pl.when(s * MOE_STEP_TILES < n_tiles[0])
    def _():
        for i in range(MOE_STEP_TILES):
            g_lo_ref, g_hi_ref, u_lo_ref, u_hi_ref, d_lo_ref, d_hi_ref = refs[6 * i:6 * (i + 1)]
            rows = pl.ds(i * MOE_TILE, MOE_TILE)
            valid = lax.broadcasted_iota(jnp.int32, (MOE_TILE, 1), 0) < n_valid[s * MOE_STEP_TILES + i]
            x = jnp.where(valid, x_ref[rows, :], 0.0)
            h = _rms(x, nw_ref[...]).astype(BF16)
            rec = jnp.where(valid, rec_ref[rows, :], 0.0)

            def expert(g_ref, u_ref, d_ref):
                act = _silu(_dot(h, g_ref[0])) * _dot(h, u_ref[0])
                return _dot(act.astype(BF16), d_ref[0])

            y = (rec[:, 2:3] * expert(g_lo_ref, u_lo_ref, d_lo_ref)
                 + rec[:, 3:4] * expert(g_hi_ref, u_hi_ref, d_hi_ref))
            o_ref[rows, :] = jnp.where(valid, x_ref[rows, :], 0.0) + y


def _moe(x2, n, meta, rec, cnt, norm_w, w_gate, w_up, w_down, layer):
    d = D_MODEL
    tm = MOE_TILE
    max_tiles = n // tm + N_BUCKETS
    n_slots = max_tiles * tm
    counts = cnt[:N_BUCKETS, 0].astype(jnp.int32)
    tiles_per_bucket = (counts + tm - 1) // tm
    tile_end = jnp.cumsum(tiles_per_bucket)
    tile_start = tile_end - tiles_per_bucket
    n_tiles = tile_end[-1]
    in_bucket = meta[0][:, None] == jnp.arange(N_BUCKETS, dtype=F32)[None, :]
    dest = meta[1].astype(jnp.int32) + jnp.sum(jnp.where(in_bucket, tile_start[None, :] * tm, 0), axis=1)
    x_sorted, rec_sorted = _scatter_tokens([x2, rec], dest, n_slots)
    first = layer * N_EXPERTS
    w_gate, w_up, w_down = (_to_bf16(w, first=first, count=N_EXPERTS) for w in (w_gate, w_up, w_down))

    k = MOE_STEP_TILES
    tile_raw = jnp.arange(max_tiles, dtype=jnp.int32)
    tile_ids = jnp.minimum(tile_raw, n_tiles - 1)
    step_ids = jnp.minimum(jnp.arange(max_tiles // k, dtype=jnp.int32), (n_tiles - 1) // k)
    of_bucket = lambda table: jnp.sum(jnp.where(tile_bucket[:, None] == jnp.arange(N_BUCKETS)[None, :], table[None, :], 0), axis=1)
    tile_bucket = jnp.sum((tile_ids[:, None] >= tile_end[None, :]).astype(jnp.int32), axis=1)
    n_valid = jnp.clip(of_bucket(counts) - (tile_ids - of_bucket(tile_start)) * tm, 0, tm)
    n_valid = jnp.where(tile_raw < n_tiles, n_valid, 0)
    group = tile_bucket // N_PAIRS
    pair = tile_bucket % N_PAIRS
    pair_lo = jnp.sum(jnp.where(pair[:, None] == jnp.arange(N_PAIRS)[None, :], jnp.asarray(PAIR_LO, jnp.int32)[None, :], 0), axis=1)
    pair_hi = jnp.sum(jnp.where(pair[:, None] == jnp.arange(N_PAIRS)[None, :], jnp.asarray(PAIR_HI, jnp.int32)[None, :], 0), axis=1)
    e_lo = group * EXPERTS_PER_GROUP + pair_lo
    e_hi = group * EXPERTS_PER_GROUP + pair_hi

    step_rows = lambda s, sid, lo, hi, nv, nt: (sid[s], 0)
    weight_specs, weight_args = [], []
    for i in range(k):
        wlo = lambda s, sid, lo, hi, nv, nt, i=i: (lo[s * k + i], 0, 0)
        whi = lambda s, sid, lo, hi, nv, nt, i=i: (hi[s * k + i], 0, 0)
        weight_specs += [pl.BlockSpec((1, d, D_EXPERT), wlo), pl.BlockSpec((1, d, D_EXPERT), whi),
                         pl.BlockSpec((1, d, D_EXPERT), wlo), pl.BlockSpec((1, d, D_EXPERT), whi),
                         pl.BlockSpec((1, D_EXPERT, d), wlo), pl.BlockSpec((1, D_EXPERT, d), whi)]
        weight_args += [w_gate, w_gate, w_up, w_up, w_down, w_down]
    grid_spec = pltpu.PrefetchScalarGridSpec(
        num_scalar_prefetch=5,
        grid=(max_tiles // k,),
        in_specs=[pl.BlockSpec((k * tm, d), step_rows), pl.BlockSpec((k * tm, LANE), step_rows),
                  pl.BlockSpec((1, d), lambda s, sid, lo, hi, nv, nt: (0, 0))] + weight_specs,
        out_specs=pl.BlockSpec((k * tm, d), step_rows))
    out_sorted = pl.pallas_call(
        _moe_kernel,
        grid_spec=grid_spec,
        out_shape=jax.ShapeDtypeStruct((n_slots, d), F32),
        compiler_params=_params("arbitrary"),
        name="moe",
    )(step_ids, e_lo, e_hi, n_valid, n_tiles.reshape(1), x_sorted, rec_sorted, norm_w.reshape(1, d), *weight_args)
    return _gather_tokens(out_sorted, dest)


def kernel(x, attn_norm, ffn_norm, even_w_in, even_w_out, hg_lb_logits, hg_o_norm, da_q_norm, da_k_norm, da_lambda, da_subln, odd_w_in, gla_w_alpha, gla_b_alpha, gla_o_norm, odd_w_out, moe_w_group, moe_b_group, moe_w_router, moe_b_router, moe_w_gate, moe_w_up, moe_w_down):
    batch, seq, d = x.shape
    depth = attn_norm.shape[0]
    n = batch * seq
    xf = x.reshape(n, d)

    lb_sm = jax.nn.softmax(hg_lb_logits.astype(F32), axis=0)
    lb_all = jnp.cumsum(lb_sm, axis=0) - lb_sm[0:1]
    rank = gla_w_alpha.shape[1]

    experts = lambda w: w.reshape((-1,) + w.shape[2:])
    moe_w_gate, moe_w_up, moe_w_down = experts(moe_w_gate), experts(moe_w_up), experts(moe_w_down)

    for layer in range(depth):
        j = layer // 2
        if layer % 2 == 0:
            lam_init = 0.8 - 0.6 * math.exp(-0.3 * layer)
            proj = _inproj(xf, n, attn_norm[layer], _to_bf16(even_w_in, first=j, count=1)[0])
            o_a = _hgrn(proj, lb_all[j], hg_o_norm[j], batch, seq)
            o_b = _diff_attn(proj, da_q_norm[j], da_k_norm[j], da_lambda[j], da_subln[j], lam_init, batch, seq)
            mix = [o_a, o_b]
            w_out = _to_bf16(even_w_out, first=j, count=1)[0]
        else:
            w_alpha = jnp.pad(gla_w_alpha[j], ((0, GLA_RANK_PAD - rank), (0, 0))).astype(BF16)
            w_in =_to_bf16(jnp.swapaxes(odd_w_in, 1, 2), odd_w_in.shape[2] + GLA_RANK_PAD - rank, first=j, count=1)[0]
            proj = _inproj(xf, n, attn_norm[layer], w_in, w_transposed=True)
            mix = [_gla(proj, w_alpha, gla_b_alpha[j], gla_o_norm[j], batch, seq)]
            w_out = _to_bf16(odd_w_out, first=j, count=1)[0]
        x2, meta, rec, cnt = _outproj_router(mix, w_out, xf, n, ffn_norm[layer], moe_w_group[layer],
                                        moe_b_group[layer], moe_w_router[layer], moe_b_router[layer])
        xf = _moe(x2, n, meta, rec, cnt, ffn_norm[layer], moe_w_gate, moe_w_up, moe_w_down, layer)
    return xf.reshape(batch, seq, d)
```

```python
import functools
import math

import numpy as np
import jax
import jax.numpy as jnp
from jax import lax
from jax.experimental import pallas as pl
from jax.experimental.pallas import tpu as pltpu
from jax.experimental.pallas import tpu_sc as plsc

F32 = jnp.float32
BF16 = jnp.bfloat16
EPS = 1e-6
LOG2_E = 1.4426950408889634

D_MODEL = 1024
LANE = 128
HEADS = 4
DK = 128
HG_WIDTH = 512
DA_HEAD_DIM = 64
GLA_DV = 256
GLA_RANK_PAD = 128
GLA_TAU = 16.0
N_GROUPS = 4
EXPERTS_PER_GROUP = 4
N_EXPERTS = 16
D_EXPERT = 512
N_PAIRS = 6
N_BUCKETS = N_GROUPS * N_PAIRS
ROUTER_ROWS = 32

CHUNK = 128
GLA_STEP_CHUNKS = 4
LEVELS = 7
ATTN_BLOCK = 512
ATTN_HEADS = 4
SCORE_BOUND_MARGIN = 1.05
MAX_SCORE_BOUND = 60.0
ROW_TILE = 1024
ROUTER_TILE = 1024
MOE_TILE = 256
MOE_STEP_TILES = 2
SC_CHUNK = 32
VMEM_LIMIT = 56 * 1024 * 1024
CAST_BLOCK_BYTES = 8 * 1024 * 1024

PAIR_LO = (0, 0, 0, 1, 1, 2)
PAIR_HI = (1, 2, 3, 2, 3, 3)


def _dot(a, b):
    return jnp.dot(a, b, preferred_element_type=F32)


def _dot_nt(a, b):
    return lax.dot_general(a, b, (((1,), (1,)), ((), ())), preferred_element_type=F32)


def _dot_tn(a, b):
    return lax.dot_general(a, b, (((0,), (0,)), ((), ())), preferred_element_type=F32)


def _rms(x, w):
    return x * lax.rsqrt(jnp.mean(x * x, axis=-1, keepdims=True) + EPS) * w


def _silu(x):
    return x * jax.nn.sigmoid(x)


def _params(*sem):
    return pltpu.CompilerParams(dimension_semantics=sem, vmem_limit_bytes=VMEM_LIMIT)


def _cast_kernel(w_ref, o_ref):
    k = w_ref.shape[1]
    o_ref[:, :k, :] = w_ref[...].astype(BF16)
    if o_ref.shape[1] > k:
        o_ref[:, k:, :] = jnp.zeros((o_ref.shape[0], o_ref.shape[1] - k, o_ref.shape[2]), BF16)


def _to_bf16(w, k_pad=None, first=0, count=None):
    _, k, n = w.shape
    l = count or w.shape[0]
    k_pad = k_pad or k
    per_matrix = k * n * 4
    lb = max([c for c in range(1, l + 1) if l % c == 0 and first % c == 0 and c * per_matrix <= CAST_BLOCK_BYTES] or [1])
    nb = n
    while lb == 1 and k * nb * 4 > CAST_BLOCK_BYTES and nb % (2 * LANE) == 0:
        nb //= 2
    return pl.pallas_call(
        _cast_kernel,
        grid=(l // lb, n // nb),
        in_specs=[pl.BlockSpec((lb, k, nb), lambda i, j: (first // lb + i, 0, j))],
        out_specs=pl.BlockSpec((lb, k_pad, nb), lambda i, j: (i, 0, j)),
        out_shape=jax.ShapeDtypeStruct((l, k_pad, n), BF16),
        compiler_params=_params("parallel", "parallel"),
        name="to_bf16",
    )(w)


def _inproj_kernel(x_ref, nw_ref, w_ref, o_ref, *, w_transposed):
    h = _rms(x_ref[...], nw_ref[...]).astype(BF16)
    o_ref[...] = _dot_nt(h, w_ref[...]) if w_transposed else _dot(h, w_ref[...])


def _inproj(x, n, norm_w, w, w_transposed=False):
    d = D_MODEL
    n_out = w.shape[0] if w_transposed else w.shape[1]
    return pl.pallas_call(
        functools.partial(_inproj_kernel, w_transposed=w_transposed),
        grid=(n // ROW_TILE,),
        in_specs=[pl.BlockSpec((ROW_TILE, d), lambda i: (i, 0)),
                  pl.BlockSpec((1, d), lambda i: (0, 0)),
                  pl.BlockSpec(w.shape, lambda i: (0, 0))],
        out_specs=pl.BlockSpec((ROW_TILE, n_out), lambda i: (i, 0)),
        out_shape=jax.ShapeDtypeStruct((n, n_out), F32),
        compiler_params=_params("parallel"),
        name="inproj",
    )(x, norm_w.reshape(1, d), w)


def _decay_tables():
    c = CHUNK
    a = np.zeros((LEVELS + 2, c, c), np.float32)
    t = np.arange(c)
    for l in range(LEVELS):
        s = 1 << l
        for r in range(c):
            mid = (r // (2 * s)) * 2 * s + s - 1
            if (r >> l) & 1:
                a[l, r] = (t > mid) & (t <= r)
            else:
                a[l, r] = (t > r) & (t <= mid)
    a[LEVELS] = t[None, :] <= t[:, None]
    a[LEVELS + 1] = t[None, :] > t[:, None]
    a = a.reshape((LEVELS + 2) * c, c)
    a3 = np.concatenate([a, a], axis=1)
    i, j = np.meshgrid(t, t, indexing="ij")
    x = np.maximum(i ^ j, 1)
    lvl = np.where(j < i, np.floor(np.log2(x)).astype(np.int32), np.where(i == j, LEVELS, -1))
    return jnp.asarray(a3, BF16), jnp.asarray(lvl, jnp.int32)


def _decay_factors(a3, g):
    g_hi = g.astype(BF16)
    g_lo = (g - g_hi.astype(F32)).astype(BF16)
    return jnp.exp(_dot(a3, jnp.concatenate([g_hi, g_lo], axis=0)))


def _gla_chunk(q, k, v, f, st_ref, lvl, rowid):
    c = CHUNK
    sc = jnp.where(lvl == LEVELS, jnp.sum(q * k, axis=-1, keepdims=True), 0.0)
    for l in range(LEVELS):
        s = 1 << l
        if s % 8 == 0:
            qk = jnp.concatenate([(q if b & 1 else k)[b * s:(b + 1) * s] for b in range(c // s)], axis=0)
        else:
            qk = jnp.where(((rowid >> l) & 1) == 1, q, k)
        m = (qk * f[l * c:(l + 1) * c]).astype(BF16)
        sc = jnp.where(lvl == l, _dot_nt(m, m), sc)
    qd = (q * f[LEVELS * c:(LEVELS + 1) * c]).astype(BF16)
    kd = (k * f[(LEVELS + 1) * c:(LEVELS + 2) * c]).astype(BF16)
    vb = v.astype(BF16)
    st = st_ref[...]
    o = _dot(sc.astype(BF16), vb) + _dot_nt(qd, st.astype(BF16))
    chunk_decay = f[(LEVELS + 1) * c - 1:(LEVELS + 1) * c, :]
    st_ref[...] = st * chunk_decay + _dot_tn(vb, kd)
    return o


def _hgrn_kernel(q_ref, f_ref, i_ref, g_ref, lb_ref, on_ref, a3_ref, lvl_ref, o_ref, st_ref):
    @pl.when(pl.program_id(1) == 0)
    def _():
        st_ref[...] = jnp.zeros_like(st_ref)

    lvl = lvl_ref[...]
    rowid = lax.broadcasted_iota(jnp.int32, (CHUNK, 1), 0)
    lb = lb_ref[...]
    for c in range(GLA_STEP_CHUNKS):
        rows = slice(c * CHUNK, (c + 1) * CHUNK)
        fl = f_ref[rows, :]
        f = _decay_factors(a3_ref[...], jnp.log(lb + (1.0 - lb) * jax.nn.sigmoid(fl)))
        key_in = (1.0 - lb) * jax.nn.sigmoid(-fl)
        for h in range(HEADS):
            sl = slice(h * DK, (h + 1) * DK)
            q = _silu(q_ref[rows, sl]) * (DK ** -0.5)
            o = _gla_chunk(q, key_in[:, sl], i_ref[rows, sl], f[:, sl], st_ref.at[h], lvl, rowid)
            o = _rms(o, on_ref[...])
            o_ref[rows, sl] = (o * _silu(g_ref[rows, sl])).astype(BF16)


def _hgrn(proj, lb, o_norm, batch, seq):
    n = proj.shape[0]
    rows = CHUNK * GLA_STEP_CHUNKS
    nt = seq // rows
    a3, lvl = _decay_tables()
    col = lambda j: pl.BlockSpec((rows, HG_WIDTH), lambda b, t: (b * nt + t, j))
    const = lambda shape: pl.BlockSpec(shape, lambda b, t: (0, 0))
    return pl.pallas_call(
        _hgrn_kernel,
        grid=(batch, nt),
        in_specs=[col(0), col(1), col(2), col(3), const((1, HG_WIDTH)), const((1, DK)),
                  const(a3.shape), const(lvl.shape)],
        out_specs=pl.BlockSpec((rows, HG_WIDTH), lambda b, t: (b * nt + t, 0)),
        out_shape=jax.ShapeDtypeStruct((n, HG_WIDTH), BF16),
        scratch_shapes=[pltpu.VMEM((HEADS, DK, DK), F32)],
        compiler_params=_params("parallel", "arbitrary"),
        name="hgrn",
    )(proj, proj, proj, proj, lb.reshape(1, HG_WIDTH), o_norm.reshape(1, DK), a3, lvl)


def _log_sigmoid(x):
    return jnp.minimum(x, 0.0) - jnp.log1p(jnp.exp(-jnp.abs(x)))


def _gla_kernel(q_ref, k_ref, v_ref, g_ref, a_ref, wa_ref, ba_ref, on_ref, a3_ref, lvl_ref, o_ref, st_ref):
    @pl.when(pl.program_id(1) == 0)
    def _():
        st_ref[...] = jnp.zeros_like(st_ref)

    lvl = lvl_ref[...]
    rowid = lax.broadcasted_iota(jnp.int32, (CHUNK, 1), 0)
    for c in range(GLA_STEP_CHUNKS):
        rows = slice(c * CHUNK, (c + 1) * CHUNK)
        gate_logit = _dot(a_ref[rows, :].astype(BF16), wa_ref[...]) + ba_ref[...]
        f = _decay_factors(a3_ref[...], _log_sigmoid(gate_logit) * (1.0 / GLA_TAU))
        for h in range(HEADS):
            sl = slice(h * DK, (h + 1) * DK)
            sv = slice(h * GLA_DV, (h + 1) * GLA_DV)
            q = q_ref[rows, sl] * (DK ** -0.5)
            o = _gla_chunk(q, k_ref[rows, sl], v_ref[rows, sv], f[:, sl], st_ref.at[h], lvl, rowid)
            o = _rms(o, on_ref[...])
            o_ref[rows, sv] = (o * _silu(g_ref[rows, sv])).astype(BF16)


def _gla(proj, w_alpha, b_alpha, o_norm, batch, seq):
    n = proj.shape[0]
    rows = CHUNK * GLA_STEP_CHUNKS
    nt = seq // rows
    qk_w = HEADS * DK
    v_w = HEADS * GLA_DV
    a3, lvl = _decay_tables()
    row = lambda b, t: b * nt + t
    const = lambda shape: pl.BlockSpec(shape, lambda b, t: (0, 0))
    return pl.pallas_call(
        _gla_kernel,
        grid=(batch, nt),
        in_specs=[pl.BlockSpec((rows, qk_w), lambda b, t: (row(b, t), 0)),
                  pl.BlockSpec((rows, qk_w), lambda b, t: (row(b, t), 1)),
                  pl.BlockSpec((rows, v_w), lambda b, t: (row(b, t), 1)),
                  pl.BlockSpec((rows, v_w), lambda b, t: (row(b, t), 2)),
                  pl.BlockSpec((rows, GLA_RANK_PAD), lambda b, t: (row(b, t), (2 * qk_w + 2 * v_w) // GLA_RANK_PAD)),
                  const((GLA_RANK_PAD, qk_w)), const((1, qk_w)), const((1, GLA_DV)),
                  const(a3.shape), const(lvl.shape)],
        out_specs=pl.BlockSpec((rows, v_w), lambda b, t: (row(b, t), 0)),
        out_shape=jax.ShapeDtypeStruct((n, v_w), BF16),
        scratch_shapes=[pltpu.VMEM((HEADS, GLA_DV, DK), F32)],
        compiler_params=_params("parallel", "arbitrary"),
        name="gla",
    )(proj, proj, proj, proj, proj, w_alpha, b_alpha.reshape(1, qk_w), o_norm.reshape(1, GLA_DV), a3, lvl)


def _qk_norm(x, w2):
    lane = lax.broadcasted_iota(jnp.int32, x.shape, 1)
    first = lane < DA_HEAD_DIM
    sq = x * x
    ms1 = jnp.sum(jnp.where(first, sq, 0.0), axis=-1, keepdims=True) * (1.0 / DA_HEAD_DIM)
    ms2 = jnp.sum(jnp.where(first, 0.0, sq), axis=-1, keepdims=True) * (1.0 / DA_HEAD_DIM)
    r = jnp.where(first, lax.rsqrt(ms1 + EPS), lax.rsqrt(ms2 + EPS))
    return x * r * w2


def _score_bound(q_norm_w, k_norm_w):
    return (jnp.max(jnp.abs(q_norm_w)) * jnp.max(jnp.abs(k_norm_w))
            * (DA_HEAD_DIM * DA_HEAD_DIM ** -0.5 * LOG2_E * SCORE_BOUND_MARGIN))


def _attn_kernel(qi_tab, ki_tab, q_ref, k_ref, v_ref, qn_ref, kn_ref, lam_ref, sub_ref, o_ref,
                 qs_ref, m_ref, l_ref, acc_ref, *, lam_init, bounded):
    p = pl.program_id(2)
    qi = qi_tab[p]
    ki = ki_tab[p]
    tq = ATTN_BLOCK
    dv = 2 * DA_HEAD_DIM
    heads = [slice(h * dv, (h + 1) * dv) for h in range(ATTN_HEADS)]

    @pl.when(ki == 0)
    def _():
        for h, sl in enumerate(heads):
            qn = _qk_norm(q_ref[:, sl], qn_ref[...])
            qt = (qn * (DA_HEAD_DIM ** -0.5 * LOG2_E)).T
            top = lax.broadcasted_iota(jnp.int32, qt.shape, 0) < DA_HEAD_DIM
            q1 = jnp.where(top, qt, 0.0).astype(BF16)
            q2 = jnp.where(top, 0.0, qt).astype(BF16)
            for copy in range(1 if bounded else 2):
                qs_ref[h, copy, :, :tq] = q1
                qs_ref[h, copy, :, tq:] = q2
            if bounded:
                bound = _score_bound(qn_ref[...], kn_ref[...])
                row = lax.broadcasted_iota(jnp.int32, (dv, 2 * tq), 0)
                qs_ref[h, 1] = jnp.where(row == 0, -bound, 0.0).astype(BF16)
        if not bounded:
            m_ref[...] = jnp.full_like(m_ref, -jnp.inf)
        l_ref[...] = jnp.zeros_like(l_ref)
        acc_ref[...] = jnp.zeros_like(acc_ref)

    def step(masked):
        tk = k_ref.shape[0]
        ones = jnp.ones((16, tk), BF16)
        unit = jnp.where(lax.broadcasted_iota(jnp.int32, (tk, dv), 1) == 0, 1.0, 0.0).astype(BF16)
        for h, sl in enumerate(heads):
            kn = _qk_norm(k_ref[:, sl], kn_ref[...]).astype(BF16)

            def mask(s):
                if not masked:
                    return s
                key = lax.broadcasted_iota(jnp.int32, s.shape, 0)
                qry = lax.broadcasted_iota(jnp.int32, s.shape, 1)
                qry = jnp.where(qry >= tq, qry - tq, qry)
                return jnp.where(key <= qry, s, -jnp.inf)

            if bounded:
                shifted = mask(_dot(jnp.concatenate([kn, unit], axis=1), qs_ref[h].reshape(2 * dv, 2 * tq)))
                alpha = 1.0
            else:
                m_prev = m_ref[h]
                m_new = jnp.maximum(m_prev, jnp.max(mask(_dot(kn, qs_ref[h, 0])), axis=0, keepdims=True))
                alpha = jnp.exp2(m_prev - m_new)
                shifted = mask(_dot(kn, qs_ref[h, 1])) - m_new
                m_ref[h] = m_new
            pr = jnp.exp2(shifted).astype(BF16) if bounded else jnp.exp2(shifted.astype(BF16))
            vt = jnp.concatenate([v_ref[:, sl].T.astype(BF16), ones], axis=0)
            pv = _dot(vt, pr)
            l_ref[h] = alpha * l_ref[h] + pv[dv:dv + 1]
            acc_ref[h] = alpha * acc_ref[h] + pv[:dv]

    @pl.when(ki < qi)
    def _():
        step(False)

    @pl.when(ki == qi)
    def _():
        step(True)
        lv = lam_ref[...]
        lam = (jnp.exp(jnp.sum(lv[0:1] * lv[1:2], axis=-1, keepdims=True))
               - jnp.exp(jnp.sum(lv[2:3] * lv[3:4], axis=-1, keepdims=True)) + lam_init)
        for h, sl in enumerate(heads):
            o1 = acc_ref[h, :, :tq] / l_ref[h, :, :tq]
            o2 = acc_ref[h, :, tq:] / l_ref[h, :, tq:]
            o = o1 - lam * o2
            o = o * lax.rsqrt(jnp.mean(o * o, axis=0, keepdims=True) + EPS) * sub_ref[...] * (1.0 - lam_init)
            o_ref[:, sl] = o.T.astype(BF16)


def _diff_attn(proj, q_norm, k_norm, lam_vecs, subln, lam_init, batch, seq):
    n = proj.shape[0]
    blk = ATTN_BLOCK
    nq = seq // blk
    dv = 2 * DA_HEAD_DIM
    pairs = [(qi, ki) for qi in range(nq) for ki in range(qi + 1)]
    qi_tab = jnp.asarray([p[0] for p in pairs], jnp.int32)
    ki_tab = jnp.asarray([p[1] for p in pairs], jnp.int32)
    width = ATTN_HEADS * dv
    groups = HEADS // ATTN_HEADS
    q_col = (4 * HG_WIDTH) // width
    const = lambda shape: pl.BlockSpec(shape, lambda b, g, p, qt, kt: (0, 0))
    grid_spec = pltpu.PrefetchScalarGridSpec(
        num_scalar_prefetch=2,
        grid=(batch, groups, len(pairs)),
        in_specs=[pl.BlockSpec((blk, width), lambda b, g, p, qt, kt: (b * nq + qt[p], q_col + g)),
                  pl.BlockSpec((blk, width), lambda b, g, p, qt, kt: (b * nq + kt[p], q_col + groups + g)),
                  pl.BlockSpec((blk, width), lambda b, g, p, qt, kt: (b * nq + kt[p], q_col + 2 * groups + g)),
                  const((1, dv)), const((1, dv)), const((4, DA_HEAD_DIM)), const((dv, 1))],
        out_specs=pl.BlockSpec((blk, width), lambda b, g, p, qt, kt: (b * nq + qt[p], g)),
        scratch_shapes=[pltpu.VMEM((ATTN_HEADS, 2, dv, 2 * blk), BF16),
                        pltpu.VMEM((ATTN_HEADS, 1, 2 * blk), F32),
                        pltpu.VMEM((ATTN_HEADS, 1, 2 * blk), F32),
                        pltpu.VMEM((ATTN_HEADS, dv, 2 * blk), F32)])
    args = (qi_tab, ki_tab, proj, proj, proj, jnp.tile(q_norm, 2).reshape(1, -1), jnp.tile(k_norm, 2).reshape(1, -1),
            lam_vecs, subln.reshape(-1, 1))

    def run(bounded):
        return pl.pallas_call(
            functools.partial(_attn_kernel, lam_init=lam_init, bounded=bounded),
            grid_spec=grid_spec,
            out_shape=jax.ShapeDtypeStruct((n, HEADS * dv), BF16),
            compiler_params=_params("parallel", "parallel", "arbitrary"),
            name="diff_attn_bounded" if bounded else "diff_attn",
        )(*args)

    return lax.cond(_score_bound(q_norm, k_norm) <= MAX_SCORE_BOUND, lambda: run(True), lambda: run(False))


def _outproj_router_kernel(*refs, n_a):
    a_refs = refs[:n_a]
    (w_ref, x_ref, nw_ref, wr_hi_ref, wr_lo_ref, br_ref, tri_ref,
     x2_ref, meta_ref, rec_ref, cnt_ref, carry_ref) = refs[n_a:]

    @pl.when(pl.program_id(0) == 0)
    def _():
        carry_ref[...] = jnp.zeros_like(carry_ref)

    x2 = x_ref[...]
    off = 0
    for a_ref in a_refs:
        kk = a_ref.shape[1]
        x2 = x2 + _dot(a_ref[...], w_ref[off:off + kk, :])
        off += kk
    x2_ref[...] = x2

    h = _rms(x2, nw_ref[...])
    h_hi = h.astype(BF16)
    h_lo = (h - h_hi.astype(F32)).astype(BF16)
    wr_hi = wr_hi_ref[...]
    by_hi = _dot_nt(jnp.concatenate([wr_hi, wr_lo_ref[...]], axis=0), h_hi)
    lt = by_hi[:ROUTER_ROWS] + by_hi[ROUTER_ROWS:] + _dot_nt(wr_hi, h_lo) + br_ref[...]
    row = lambda r: lt[r:r + 1, :]

    gl = [row(g) for g in range(N_GROUPS)]
    gmax = functools.reduce(jnp.maximum, gl)
    gsel = jnp.where(gl[0] == gmax, 0.0, jnp.where(gl[1] == gmax, 1.0, jnp.where(gl[2] == gmax, 2.0, 3.0)))
    g_w = 1.0 / functools.reduce(jnp.add, [jnp.exp(x - gmax) for x in gl])

    el = []
    for e in range(EXPERTS_PER_GROUP):
        v = row(N_GROUPS + e)
        for g in range(1, N_GROUPS):
            v = jnp.where(gsel == float(g), row(N_GROUPS + EXPERTS_PER_GROUP * g + e), v)
        el.append(v)
    emax = functools.reduce(jnp.maximum, el)
    sel = []
    for e in range(EXPERTS_PER_GROUP):
        ahead = jnp.zeros_like(emax)
        for o in range(EXPERTS_PER_GROUP):
            if o < e:
                ahead = ahead + jnp.where(el[o] >= el[e], 1.0, 0.0)
            elif o > e:
                ahead = ahead + jnp.where(el[o] > el[e], 1.0, 0.0)
        sel.append(ahead < 2.0)
    pe = [jnp.where(sel[e], jnp.exp(el[e] - emax), 0.0) for e in range(EXPERTS_PER_GROUP)]
    scale = g_w / functools.reduce(jnp.add, pe)
    zero = jnp.zeros_like(emax)
    seen = zero
    lo_idx, hi_idx, c_lo, c_hi = zero, zero, zero, zero
    for e in range(EXPERTS_PER_GROUP):
        is_lo = sel[e] & (seen == 0.0)
        is_hi = sel[e] & (seen == 1.0)
        lo_idx = jnp.where(is_lo, float(e), lo_idx)
        hi_idx = jnp.where(is_hi, float(e), hi_idx)
        c_lo = jnp.where(is_lo, pe[e] * scale, c_lo)
        c_hi = jnp.where(is_hi, pe[e] * scale, c_hi)
        seen = seen + jnp.where(sel[e], 1.0, 0.0)
    pair = lo_idx * (7.0 - lo_idx) * 0.5 + (hi_idx - lo_idx - 1.0)
    bucket = gsel * float(N_PAIRS) + pair

    bid = lax.broadcasted_iota(jnp.int32, (ROUTER_ROWS, bucket.shape[1]), 0).astype(F32)
    onehot = jnp.where(bid == bucket, 1.0, 0.0)
    before = _dot(onehot.astype(BF16), tri_ref[...]) + carry_ref[...]
    rank = jnp.sum(onehot * before, axis=0, keepdims=True)
    carry = carry_ref[...] + jnp.sum(onehot, axis=1, keepdims=True)
    carry_ref[...] = carry
    cnt_ref[...] = carry
    meta = jnp.concatenate([bucket, rank, c_lo, c_hi, zero, zero, zero, zero], axis=0)
    meta_ref[...] = meta
    rec_ref[...] = jnp.concatenate([meta, jnp.zeros((LANE - meta.shape[0], meta.shape[1]), F32)], axis=0).T


def _outproj_router(a_list, w_out, x, n, norm_w, w_group, b_group, w_router, b_router):
    d = D_MODEL
    tm = ROUTER_TILE
    tok = lambda: pl.BlockSpec((tm, d), lambda i: (i, 0))
    wr = jnp.concatenate([w_group, w_router.transpose(1, 0, 2).reshape(d, N_EXPERTS)], axis=1)
    wr_t = jnp.zeros((ROUTER_ROWS, d), F32).at[:wr.shape[1]].set(wr.T)
    wr_hi = wr_t.astype(BF16)
    wr_lo = (wr_t - wr_hi.astype(F32)).astype(BF16)
    br = jnp.zeros((ROUTER_ROWS, 1), F32).at[:wr.shape[1], 0].set(
        jnp.concatenate([b_group, b_router.reshape(-1)]))
    tri = jnp.asarray(np.triu(np.ones((tm, tm), np.float32), 1), BF16)
    const = lambda shape: pl.BlockSpec(shape, lambda i: (0, 0))
    in_specs = [pl.BlockSpec((tm, a.shape[1]), lambda i: (i, 0)) for a in a_list]
    in_specs += [const((d, d)), tok(), const((1, d)),
                 const((ROUTER_ROWS, d)), const((ROUTER_ROWS, d)), const((ROUTER_ROWS, 1)), const((tm, tm))]
    return pl.pallas_call(
        functools.partial(_outproj_router_kernel, n_a=len(a_list)),
        grid=(n // tm,),
        in_specs=in_specs,
        out_specs=[tok(),
                   pl.BlockSpec((8, tm), lambda i: (0, i)),
                   pl.BlockSpec((tm, LANE), lambda i: (i, 0)),
                   const((ROUTER_ROWS, 1))],
        out_shape=[jax.ShapeDtypeStruct((n, d), F32),
                   jax.ShapeDtypeStruct((8, n), F32),
                   jax.ShapeDtypeStruct((n, LANE), F32),
                   jax.ShapeDtypeStruct((ROUTER_ROWS, 1), F32)],
        scratch_shapes=[pltpu.VMEM((ROUTER_ROWS, 1), F32)],
        compiler_params=_params("arbitrary"),
        name="outproj_router",
    )(*a_list, w_out, x, norm_w.reshape(1, d), wr_hi, wr_lo, br, tri)


def _gather_tokens(table, idx):
    m = idx.shape[0]
    info = plsc.get_sparse_core_info()
    n_workers = info.num_cores * info.num_subcores
    per_worker = m // n_workers
    n_chunks = per_worker // SC_CHUNK
    assert per_worker * n_workers == m and n_chunks * SC_CHUNK == per_worker and n_chunks % 2 == 0
    mesh = plsc.VectorSubcoreMesh(core_axis_name="core", subcore_axis_name="subcore")
    rows = pltpu.VMEM((SC_CHUNK,) + table.shape[1:], table.dtype)

    @functools.partial(
        pl.kernel, mesh=mesh,
        out_type=jax.ShapeDtypeStruct((m,) + table.shape[1:], table.dtype),
        scratch_types=[pltpu.VMEM((n_chunks, SC_CHUNK), jnp.int32), rows, rows,
                       pltpu.SemaphoreType.DMA, pltpu.SemaphoreType.DMA],
        compiler_params=pltpu.CompilerParams(use_tc_tiling_on_sc=True),
        name="gather_tokens")
    def gather(table_hbm, idx_hbm, out_hbm, idx_v, rows_a, rows_b, sem_a, sem_b):
        worker = lax.axis_index("subcore") * info.num_cores + lax.axis_index("core")
        bufs = ((rows_a, sem_a), (rows_b, sem_b))

        def fetch(j, buf):
            return pltpu.make_async_copy(table_hbm.at[idx_v.at[j]], bufs[buf][0], bufs[buf][1])

        pltpu.sync_copy(idx_hbm.at[worker], idx_v)
        fetch(0, 0).start()

        @pl.loop(0, n_chunks, step=2)
        def _(j0):
            for buf in range(2):
                j = j0 + buf
                fetch(j, buf).wait()

                @pl.when(j + 1 < n_chunks)
                def _():
                    fetch(j + 1, 1 - buf).start()

                off = pl.multiple_of(worker * per_worker + j * SC_CHUNK, SC_CHUNK)
                pltpu.sync_copy(bufs[buf][0], out_hbm.at[pl.ds(off, SC_CHUNK)])

    return gather(table, idx.reshape(n_workers, n_chunks, SC_CHUNK))


def _scatter_tokens(arrays, idx, m):
    n = idx.shape[0]
    na = len(arrays)
    info = plsc.get_sparse_core_info()
    n_workers = info.num_cores * info.num_subcores
    per_worker = n // n_workers
    n_chunks = per_worker // SC_CHUNK
    assert per_worker * n_workers == n and n_chunks * SC_CHUNK == per_worker and n_chunks % 2 == 0
    mesh = plsc.VectorSubcoreMesh(core_axis_name="core", subcore_axis_name="subcore")
    bufs = [pltpu.VMEM((SC_CHUNK,) + a.shape[1:], a.dtype) for a in arrays for _ in range(2)]

    @functools.partial(
        pl.kernel, mesh=mesh,
        out_type=[jax.ShapeDtypeStruct((m,) + a.shape[1:], a.dtype) for a in arrays],
        scratch_types=[pltpu.VMEM((n_chunks, SC_CHUNK), jnp.int32)] + bufs + [pltpu.SemaphoreType.DMA] * (2 * na),
        compiler_params=pltpu.CompilerParams(use_tc_tiling_on_sc=True),
        name="scatter_tokens")
    def scatter(*refs):
        ins, idx_hbm, outs = refs[:na], refs[na], refs[na + 1:2 * na + 1]
        idx_v = refs[2 * na + 1]
        buf_v = refs[2 * na + 2:4 * na + 2]
        sem_v = refs[4 * na + 2:]
        worker = lax.axis_index("subcore") * info.num_cores + lax.axis_index("core")

        def load(a, j, b):
            off = pl.multiple_of(worker * per_worker + j * SC_CHUNK, SC_CHUNK)
            return pltpu.make_async_copy(ins[a].at[pl.ds(off, SC_CHUNK)], buf_v[2 * a + b], sem_v[2 * a + b])

        pltpu.sync_copy(idx_hbm.at[worker], idx_v)
        for a in range(na):
            load(a, 0, 0).start()

        @pl.loop(0, n_chunks, step=2)
        def _(j0):
            for b in range(2):
                j = j0 + b
                for a in range(na):
                    load(a, j, b).wait()

                @pl.when(j + 1 < n_chunks)
                def _():
                    for a in range(na):
                        load(a, j + 1, 1 - b).start()

                for a in range(na):
                    pltpu.sync_copy(buf_v[2 * a + b], outs[a].at[idx_v.at[j]])

    return scatter(*arrays, idx.reshape(n_workers, n_chunks, SC_CHUNK))


def _moe_kernel(step_ids, e_lo, e_hi, n_valid, n_tiles, x_ref, rec_ref, nw_ref, *refs):
    del step_ids, e_lo, e_hi
    s = pl.program_id(0)
    o_ref = refs[-1]

    @pl.when(s * MOE_STEP_TILES < n_tiles[0])
    def _():
        for i in range(MOE_STEP_TILES):
            g_lo_ref, g_hi_ref, u_lo_ref, u_hi_ref, d_lo_ref, d_hi_ref = refs[6 * i:6 * (i + 1)]
            rows = pl.ds(i * MOE_TILE, MOE_TILE)
            valid = lax.broadcasted_iota(jnp.int32, (MOE_TILE, 1), 0) < n_valid[s * MOE_STEP_TILES + i]
            x = jnp.where(valid, x_ref[rows, :], 0.0)
            h = _rms(x, nw_ref[...]).astype(BF16)
            rec = jnp.where(valid, rec_ref[rows, :], 0.0)

            def expert(g_ref, u_ref, d_ref):
                act = _silu(_dot(h, g_ref[0])) * _dot(h, u_ref[0])
                return _dot(act.astype(BF16), d_ref[0])

            y = (rec[:, 2:3] * expert(g_lo_ref, u_lo_ref, d_lo_ref)
                 + rec[:, 3:4] * expert(g_hi_ref, u_hi_ref, d_hi_ref))
            o_ref[rows, :] = jnp.where(valid, x_ref[rows, :], 0.0) + y


def _moe(x2, n, meta, rec, cnt, norm_w, w_gate, w_up, w_down, layer):
    d = D_MODEL
    tm = MOE_TILE
    max_tiles = n // tm + N_BUCKETS
    n_slots = max_tiles * tm
    counts = cnt[:N_BUCKETS, 0].astype(jnp.int32)
    tiles_per_bucket = (counts + tm - 1) // tm
    tile_end = jnp.cumsum(tiles_per_bucket)
    tile_start = tile_end - tiles_per_bucket
    n_tiles = tile_end[-1]
    in_bucket = meta[0][:, None] == jnp.arange(N_BUCKETS, dtype=F32)[None, :]
    dest = meta[1].astype(jnp.int32) + jnp.sum(jnp.where(in_bucket, tile_start[None, :] * tm, 0), axis=1)
    x_sorted, rec_sorted = _scatter_tokens([x2, rec], dest, n_slots)
    first = layer * N_EXPERTS
    w_gate, w_up, w_down = (_to_bf16(w, first=first, count=N_EXPERTS) for w in (w_gate, w_up, w_down))

    k = MOE_STEP_TILES
    tile_raw = jnp.arange(max_tiles, dtype=jnp.int32)
    tile_ids = jnp.minimum(tile_raw, n_tiles - 1)
    step_ids = jnp.minimum(jnp.arange(max_tiles // k, dtype=jnp.int32), (n_tiles - 1) // k)
    of_bucket = lambda table: jnp.sum(jnp.where(tile_bucket[:, None] == jnp.arange(N_BUCKETS)[None, :], table[None, :], 0), axis=1)
    tile_bucket = jnp.sum((tile_ids[:, None] >= tile_end[None, :]).astype(jnp.int32), axis=1)
    n_valid = jnp.clip(of_bucket(counts) - (tile_ids - of_bucket(tile_start)) * tm, 0, tm)
    n_valid = jnp.where(tile_raw < n_tiles, n_valid, 0)
    group = tile_bucket // N_PAIRS
    pair = tile_bucket % N_PAIRS
    pair_lo = jnp.sum(jnp.where(pair[:, None] == jnp.arange(N_PAIRS)[None, :], jnp.asarray(PAIR_LO, jnp.int32)[None, :], 0), axis=1)
    pair_hi = jnp.sum(jnp.where(pair[:, None] == jnp.arange(N_PAIRS)[None, :], jnp.asarray(PAIR_HI, jnp.int32)[None, :], 0), axis=1)
    e_lo = group * EXPERTS_PER_GROUP + pair_lo
    e_hi = group * EXPERTS_PER_GROUP + pair_hi

    step_rows = lambda s, sid, lo, hi, nv, nt: (sid[s], 0)
    weight_specs, weight_args = [], []
    for i in range(k):
        wlo = lambda s, sid, lo, hi, nv, nt, i=i: (lo[s * k + i], 0, 0)
        whi = lambda s, sid, lo, hi, nv, nt, i=i: (hi[s * k + i], 0, 0)
        weight_specs += [pl.BlockSpec((1, d, D_EXPERT), wlo), pl.BlockSpec((1, d, D_EXPERT), whi),
                         pl.BlockSpec((1, d, D_EXPERT), wlo), pl.BlockSpec((1, d, D_EXPERT), whi),
                         pl.BlockSpec((1, D_EXPERT, d), wlo), pl.BlockSpec((1, D_EXPERT, d), whi)]
        weight_args += [w_gate, w_gate, w_up, w_up, w_down, w_down]
    grid_spec = pltpu.PrefetchScalarGridSpec(
        num_scalar_prefetch=5,
        grid=(max_tiles // k,),
        in_specs=[pl.BlockSpec((k * tm, d), step_rows), pl.BlockSpec((k * tm, LANE), step_rows),
                  pl.BlockSpec((1, d), lambda s, sid, lo, hi, nv, nt: (0, 0))] + weight_specs,
        out_specs=pl.BlockSpec((k * tm, d), step_rows))
    out_sorted = pl.pallas_call(
        _moe_kernel,
        grid_spec=grid_spec,
        out_shape=jax.ShapeDtypeStruct((n_slots, d), F32),
        compiler_params=_params("arbitrary"),
        name="moe",
    )(step_ids, e_lo, e_hi, n_valid, n_tiles.reshape(1), x_sorted, rec_sorted, norm_w.reshape(1, d), *weight_args)
    return _gather_tokens(out_sorted, dest)


def kernel(x, attn_norm, ffn_norm, even_w_in, even_w_out, hg_lb_logits, hg_o_norm, da_q_norm, da_k_norm, da_lambda, da_subln, odd_w_in, gla_w_alpha, gla_b_alpha, gla_o_norm, odd_w_out, moe_w_group, moe_b_group, moe_w_router, moe_b_router, moe_w_gate, moe_w_up, moe_w_down):
    batch, seq, d = x.shape
    depth = attn_norm.shape[0]
    n = batch * seq
    xf = x.reshape(n, d)

    lb_sm = jax.nn.softmax(hg_lb_logits.astype(F32), axis=0)
    lb_all = jnp.cumsum(lb_sm, axis=0) - lb_sm[0:1]
    rank = gla_w_alpha.shape[1]

    experts = lambda w: w.reshape((-1,) + w.shape[2:])
    moe_w_gate, moe_w_up, moe_w_down = experts(moe_w_gate), experts(moe_w_up), experts(moe_w_down)

    for layer in range(depth):
        j = layer // 2
        if layer % 2 == 0:
            lam_init = 0.8 - 0.6 * math.exp(-0.3 * layer)
            proj = _inproj(xf, n, attn_norm[layer], _to_bf16(even_w_in, first=j, count=1)[0])
            o_a = _hgrn(proj, lb_all[j], hg_o_norm[j], batch, seq)
            o_b = _diff_attn(proj, da_q_norm[j], da_k_norm[j], da_lambda[j], da_subln[j], lam_init, batch, seq)
            mix = [o_a, o_b]
            w_out = _to_bf16(even_w_out, first=j, count=1)[0]
        else:
            w_alpha = jnp.pad(gla_w_alpha[j], ((0, GLA_RANK_PAD - rank), (0, 0))).astype(BF16)
            w_in =_to_bf16(jnp.swapaxes(odd_w_in, 1, 2), odd_w_in.shape[2] + GLA_RANK_PAD - rank, first=j, count=1)[0]
            proj = _inproj(xf, n, attn_norm[layer], w_in, w_transposed=True)
            mix = [_gla(proj, w_alpha, gla_b_alpha[j], gla_o_norm[j], batch, seq)]
            w_out = _to_bf16(odd_w_out, first=j, count=1)[0]
        x2, meta, rec, cnt = _outproj_router(mix, w_out, xf, n, ffn_norm[layer], moe_w_group[layer],
                                        moe_b_group[layer], moe_w_router[layer], moe_b_router[layer])
        xf = _moe(x2, n, meta, rec, cnt, ffn_norm[layer], moe_w_gate, moe_w_up, moe_w_down, layer)
    return xf.reshape(batch, seq, d)
```

```python
import functools
import math

import numpy as np
import jax
import jax.numpy as jnp
from jax import lax
from jax.experimental import pallas as pl
from jax.experimental.pallas import tpu as pltpu
from jax.experimental.pallas import tpu_sc as plsc

F32 = jnp.float32
BF16 = jnp.bfloat16
EPS = 1e-6
LOG2_E = 1.4426950408889634

D_MODEL = 1024
LANE = 128
HEADS = 4
DK = 128
HG_WIDTH = 512
DA_HEAD_DIM = 64
GLA_DV = 256
GLA_RANK_PAD = 128
GLA_TAU = 16.0
N_GROUPS = 4
EXPERTS_PER_GROUP = 4
N_EXPERTS = 16
D_EXPERT = 512
N_PAIRS = 6
N_BUCKETS = N_GROUPS * N_PAIRS
ROUTER_ROWS = 32

CHUNK = 128
GLA_STEP_CHUNKS = 4
LEVELS = 7
ATTN_BLOCK = 512
ATTN_HEADS = 4
SCORE_BOUND_MARGIN = 1.05
MAX_SCORE_BOUND = 60.0
ROW_TILE = 1024
ROUTER_TILE = 1024
MOE_TILE = 256
MOE_STEP_TILES = 2
SC_CHUNK = 32
VMEM_LIMIT = 56 * 1024 * 1024
CAST_BLOCK_BYTES = 8 * 1024 * 1024

PAIR_LO = (0, 0, 0, 1, 1, 2)
PAIR_HI = (1, 2, 3, 2, 3, 3)


def _dot(a, b):
    return jnp.dot(a, b, preferred_element_type=F32)


def _dot_nt(a, b):
    return lax.dot_general(a, b, (((1,), (1,)), ((), ())), preferred_element_type=F32)


def _dot_tn(a, b):
    return lax.dot_general(a, b, (((0,), (0,)), ((), ())), preferred_element_type=F32)


def _rms(x, w):
    return x * lax.rsqrt(jnp.mean(x * x, axis=-1, keepdims=True) + EPS) * w


def _silu(x):
    return x * jax.nn.sigmoid(x)


def _params(*sem):
    return pltpu.CompilerParams(dimension_semantics=sem, vmem_limit_bytes=VMEM_LIMIT)


def _cast_kernel(w_ref, o_ref):
    k = w_ref.shape[1]
    o_ref[:, :k, :] = w_ref[...].astype(BF16)
    if o_ref.shape[1] > k:
        o_ref[:, k:, :] = jnp.zeros((o_ref.shape[0], o_ref.shape[1] - k, o_ref.shape[2]), BF16)


def _to_bf16(w, k_pad=None, first=0, count=None):
    _, k, n = w.shape
    l = count or w.shape[0]
    k_pad = k_pad or k
    per_matrix = k * n * 4
    lb = max([c for c in range(1, l + 1) if l % c == 0 and first % c == 0 and c * per_matrix <= CAST_BLOCK_BYTES] or [1])
    nb = n
    while lb == 1 and k * nb * 4 > CAST_BLOCK_BYTES and nb % (2 * LANE) == 0:
        nb //= 2
    return pl.pallas_call(
        _cast_kernel,
        grid=(l // lb, n // nb),
        in_specs=[pl.BlockSpec((lb, k, nb), lambda i, j: (first // lb + i, 0, j))],
        out_specs=pl.BlockSpec((lb, k_pad, nb), lambda i, j: (i, 0, j)),
        out_shape=jax.ShapeDtypeStruct((l, k_pad, n), BF16),
        compiler_params=_params("parallel", "parallel"),
        name="to_bf16",
    )(w)


def _inproj_kernel(x_ref, nw_ref, w_ref, o_ref, *, w_transposed):
    h = _rms(x_ref[...], nw_ref[...]).astype(BF16)
    o_ref[...] = _dot_nt(h, w_ref[...]) if w_transposed else _dot(h, w_ref[...])


def _inproj(x, n, norm_w, w, w_transposed=False):
    d = D_MODEL
    n_out = w.shape[0] if w_transposed else w.shape[1]
    return pl.pallas_call(
        functools.partial(_inproj_kernel, w_transposed=w_transposed),
        grid=(n // ROW_TILE,),
        in_specs=[pl.BlockSpec((ROW_TILE, d), lambda i: (i, 0)),
                  pl.BlockSpec((1, d), lambda i: (0, 0)),
                  pl.BlockSpec(w.shape, lambda i: (0, 0))],
        out_specs=pl.BlockSpec((ROW_TILE, n_out), lambda i: (i, 0)),
        out_shape=jax.ShapeDtypeStruct((n, n_out), F32),
        compiler_params=_params("parallel"),
        name="inproj",
    )(x, norm_w.reshape(1, d), w)


def _decay_tables():
    c = CHUNK
    a = np.zeros((LEVELS + 2, c, c), np.float32)
    t = np.arange(c)
    for l in range(LEVELS):
        s = 1 << l
        for r in range(c):
            mid = (r // (2 * s)) * 2 * s + s - 1
            if (r >> l) & 1:
                a[l, r] = (t > mid) & (t <= r)
            else:
                a[l, r] = (t > r) & (t <= mid)
    a[LEVELS] = t[None, :] <= t[:, None]
    a[LEVELS + 1] = t[None, :] > t[:, None]
    a = a.reshape((LEVELS + 2) * c, c)
    a3 = np.concatenate([a, a], axis=1)
    i, j = np.meshgrid(t, t, indexing="ij")
    x = np.maximum(i ^ j, 1)
    lvl = np.where(j < i, np.floor(np.log2(x)).astype(np.int32), np.where(i == j, LEVELS, -1))
    return jnp.asarray(a3, BF16), jnp.asarray(lvl, jnp.int32)


def _decay_factors(a3, g):
    g_hi = g.astype(BF16)
    g_lo = (g - g_hi.astype(F32)).astype(BF16)
    return jnp.exp(_dot(a3, jnp.concatenate([g_hi, g_lo], axis=0)))


def _gla_chunk(q, k, v, f, st_ref, lvl, rowid):
    c = CHUNK
    sc = jnp.where(lvl == LEVELS, jnp.sum(q * k, axis=-1, keepdims=True), 0.0)
    for l in range(LEVELS):
        s = 1 << l
        if s % 8 == 0:
            qk = jnp.concatenate([(q if b & 1 else k)[b * s:(b + 1) * s] for b in range(c // s)], axis=0)
        else:
            qk = jnp.where(((rowid >> l) & 1) == 1, q, k)
        m = (qk * f[l * c:(l + 1) * c]).astype(BF16)
        sc = jnp.where(lvl == l, _dot_nt(m, m), sc)
    qd = (q * f[LEVELS * c:(LEVELS + 1) * c]).astype(BF16)
    kd = (k * f[(LEVELS + 1) * c:(LEVELS + 2) * c]).astype(BF16)
    vb = v.astype(BF16)
    st = st_ref[...]
    o = _dot(sc.astype(BF16), vb) + _dot_nt(qd, st.astype(BF16))
    chunk_decay = f[(LEVELS + 1) * c - 1:(LEVELS + 1) * c, :]
    st_ref[...] = st * chunk_decay + _dot_tn(vb, kd)
    return o


def _hgrn_kernel(q_ref, f_ref, i_ref, g_ref, lb_ref, on_ref, a3_ref, lvl_ref, o_ref, st_ref):
    @pl.when(pl.program_id(1) == 0)
    def _():
        st_ref[...] = jnp.zeros_like(st_ref)

    lvl = lvl_ref[...]
    rowid = lax.broadcasted_iota(jnp.int32, (CHUNK, 1), 0)
    lb = lb_ref[...]
    for c in range(GLA_STEP_CHUNKS):
        rows = slice(c * CHUNK, (c + 1) * CHUNK)
        fl = f_ref[rows, :]
        f = _decay_factors(a3_ref[...], jnp.log(lb + (1.0 - lb) * jax.nn.sigmoid(fl)))
        key_in = (1.0 - lb) * jax.nn.sigmoid(-fl)
        for h in range(HEADS):
            sl = slice(h * DK, (h + 1) * DK)
            q = _silu(q_ref[rows, sl]) * (DK ** -0.5)
            o = _gla_chunk(q, key_in[:, sl], i_ref[rows, sl], f[:, sl], st_ref.at[h], lvl, rowid)
            o = _rms(o, on_ref[...])
            o_ref[rows, sl] = (o * _silu(g_ref[rows, sl])).astype(BF16)


def _hgrn(proj, lb, o_norm, batch, seq):
    n = proj.shape[0]
    rows = CHUNK * GLA_STEP_CHUNKS
    nt = seq // rows
    a3, lvl = _decay_tables()
    col = lambda j: pl.BlockSpec((rows, HG_WIDTH), lambda b, t: (b * nt + t, j))
    const = lambda shape: pl.BlockSpec(shape, lambda b, t: (0, 0))
    return pl.pallas_call(
        _hgrn_kernel,
        grid=(batch, nt),
        in_specs=[col(0), col(1), col(2), col(3), const((1, HG_WIDTH)), const((1, DK)),
                  const(a3.shape), const(lvl.shape)],
        out_specs=pl.BlockSpec((rows, HG_WIDTH), lambda b, t: (b * nt + t, 0)),
        out_shape=jax.ShapeDtypeStruct((n, HG_WIDTH), BF16),
        scratch_shapes=[pltpu.VMEM((HEADS, DK, DK), F32)],
        compiler_params=_params("parallel", "arbitrary"),
        name="hgrn",
    )(proj, proj, proj, proj, lb.reshape(1, HG_WIDTH), o_norm.reshape(1, DK), a3, lvl)


def _log_sigmoid(x):
    return jnp.minimum(x, 0.0) - jnp.log1p(jnp.exp(-jnp.abs(x)))


def _gla_kernel(q_ref, k_ref, v_ref, g_ref, a_ref, wa_ref, ba_ref, on_ref, a3_ref, lvl_ref, o_ref, st_ref):
    @pl.when(pl.program_id(1) == 0)
    def _():
        st_ref[...] = jnp.zeros_like(st_ref)

    lvl = lvl_ref[...]
    rowid = lax.broadcasted_iota(jnp.int32, (CHUNK, 1), 0)
    for c in range(GLA_STEP_CHUNKS):
        rows = slice(c * CHUNK, (c + 1) * CHUNK)
        gate_logit = _dot(a_ref[rows, :].astype(BF16), wa_ref[...]) + ba_ref[...]
        f = _decay_factors(a3_ref[...], _log_sigmoid(gate_logit) * (1.0 / GLA_TAU))
        for h in range(HEADS):
            sl = slice(h * DK, (h + 1) * DK)
            sv = slice(h * GLA_DV, (h + 1) * GLA_DV)
            q = q_ref[rows, sl] * (DK ** -0.5)
            o = _gla_chunk(q, k_ref[rows, sl], v_ref[rows, sv], f[:, sl], st_ref.at[h], lvl, rowid)
            o = _rms(o, on_ref[...])
            o_ref[rows, sv] = (o * _silu(g_ref[rows, sv])).astype(BF16)


def _gla(proj, w_alpha, b_alpha, o_norm, batch, seq):
    n = proj.shape[0]
    rows = CHUNK * GLA_STEP_CHUNKS
    nt = seq // rows
    qk_w = HEADS * DK
    v_w = HEADS * GLA_DV
    a3, lvl = _decay_tables()
    row = lambda b, t: b * nt + t
    const = lambda shape: pl.BlockSpec(shape, lambda b, t: (0, 0))
    return pl.pallas_call(
        _gla_kernel,
        grid=(batch, nt),
        in_specs=[pl.BlockSpec((rows, qk_w), lambda b, t: (row(b, t), 0)),
                  pl.BlockSpec((rows, qk_w), lambda b, t: (row(b, t), 1)),
                  pl.BlockSpec((rows, v_w), lambda b, t: (row(b, t), 1)),
                  pl.BlockSpec((rows, v_w), lambda b, t: (row(b, t), 2)),
                  pl.BlockSpec((rows, GLA_RANK_PAD), lambda b, t: (row(b, t), (2 * qk_w + 2 * v_w) // GLA_RANK_PAD)),
                  const((GLA_RANK_PAD, qk_w)), const((1, qk_w)), const((1, GLA_DV)),
                  const(a3.shape), const(lvl.shape)],
        out_specs=pl.BlockSpec((rows, v_w), lambda b, t: (row(b, t), 0)),
        out_shape=jax.ShapeDtypeStruct((n, v_w), BF16),
        scratch_shapes=[pltpu.VMEM((HEADS, GLA_DV, DK), F32)],
        compiler_params=_params("parallel", "arbitrary"),
        name="gla",
    )(proj, proj, proj, proj, proj, w_alpha, b_alpha.reshape(1, qk_w), o_norm.reshape(1, GLA_DV), a3, lvl)


def _qk_norm(x, w2):
    lane = lax.broadcasted_iota(jnp.int32, x.shape, 1)
    first = lane < DA_HEAD_DIM
    sq = x * x
    ms1 = jnp.sum(jnp.where(first, sq, 0.0), axis=-1, keepdims=True) * (1.0 / DA_HEAD_DIM)
    ms2 = jnp.sum(jnp.where(first, 0.0, sq), axis=-1, keepdims=True) * (1.0 / DA_HEAD_DIM)
    r = jnp.where(first, lax.rsqrt(ms1 + EPS), lax.rsqrt(ms2 + EPS))
    return x * r * w2


def _score_bound(q_norm_w, k_norm_w):
    return (jnp.max(jnp.abs(q_norm_w)) * jnp.max(jnp.abs(k_norm_w))
            * (DA_HEAD_DIM * DA_HEAD_DIM ** -0.5 * LOG2_E * SCORE_BOUND_MARGIN))


def _attn_kernel(qi_tab, ki_tab, q_ref, k_ref, v_ref, qn_ref, kn_ref, lam_ref, sub_ref, o_ref,
                 qs_ref, m_ref, l_ref, acc_ref, *, lam_init, bounded):
    p = pl.program_id(2)
    qi = qi_tab[p]
    ki = ki_tab[p]
    tq = ATTN_BLOCK
    dv = 2 * DA_HEAD_DIM
    heads = [slice(h * dv, (h + 1) * dv) for h in range(ATTN_HEADS)]

    @pl.when(ki == 0)
    def _():
        for h, sl in enumerate(heads):
            qn = _qk_norm(q_ref[:, sl], qn_ref[...])
            qt = (qn * (DA_HEAD_DIM ** -0.5 * LOG2_E)).T
            top = lax.broadcasted_iota(jnp.int32, qt.shape, 0) < DA_HEAD_DIM
            q1 = jnp.where(top, qt, 0.0).astype(BF16)
            q2 = jnp.where(top, 0.0, qt).astype(BF16)
            for copy in range(1 if bounded else 2):
                qs_ref[h, copy, :, :tq] = q1
                qs_ref[h, copy, :, tq:] = q2
            if bounded:
                bound = _score_bound(qn_ref[...], kn_ref[...])
                row = lax.broadcasted_iota(jnp.int32, (dv, 2 * tq), 0)
                qs_ref[h, 1] = jnp.where(row == 0, -bound, 0.0).astype(BF16)
        if not bounded:
            m_ref[...] = jnp.full_like(m_ref, -jnp.inf)
        l_ref[...] = jnp.zeros_like(l_ref)
        acc_ref[...] = jnp.zeros_like(acc_ref)

    def step(masked):
        tk = k_ref.shape[0]
        ones = jnp.ones((16, tk), BF16)
        unit = jnp.where(lax.broadcasted_iota(jnp.int32, (tk, dv), 1) == 0, 1.0, 0.0).astype(BF16)
        for h, sl in enumerate(heads):
            kn = _qk_norm(k_ref[:, sl], kn_ref[...]).astype(BF16)

            def mask(s):
                if not masked:
                    return s
                key = lax.broadcasted_iota(jnp.int32, s.shape, 0)
                qry = lax.broadcasted_iota(jnp.int32, s.shape, 1)
                qry = jnp.where(qry >= tq, qry - tq, qry)
                return jnp.where(key <= qry, s, -jnp.inf)

            if bounded:
                shifted = mask(_dot(jnp.concatenate([kn, unit], axis=1), qs_ref[h].reshape(2 * dv, 2 * tq)))
                alpha = 1.0
            else:
                m_prev = m_ref[h]
                m_new = jnp.maximum(m_prev, jnp.max(mask(_dot(kn, qs_ref[h, 0])), axis=0, keepdims=True))
                alpha = jnp.exp2(m_prev - m_new)
                shifted = mask(_dot(kn, qs_ref[h, 1])) - m_new
                m_ref[h] = m_new
            pr = jnp.exp2(shifted).astype(BF16) if bounded else jnp.exp2(shifted.astype(BF16))
            vt = jnp.concatenate([v_ref[:, sl].T.astype(BF16), ones], axis=0)
            pv = _dot(vt, pr)
            l_ref[h] = alpha * l_ref[h] + pv[dv:dv + 1]
            acc_ref[h] = alpha * acc_ref[h] + pv[:dv]

    @pl.when(ki < qi)
    def _():
        step(False)

    @pl.when(ki == qi)
    def _():
        step(True)
        lv = lam_ref[...]
        lam = (jnp.exp(jnp.sum(lv[0:1] * lv[1:2], axis=-1, keepdims=True))
               - jnp.exp(jnp.sum(lv[2:3] * lv[3:4], axis=-1, keepdims=True)) + lam_init)
        for h, sl in enumerate(heads):
            o1 = acc_ref[h, :, :tq] / l_ref[h, :, :tq]
            o2 = acc_ref[h, :, tq:] / l_ref[h, :, tq:]
            o = o1 - lam * o2
            o = o * lax.rsqrt(jnp.mean(o * o, axis=0, keepdims=True) + EPS) * sub_ref[...] * (1.0 - lam_init)
            o_ref[:, sl] = o.T.astype(BF16)


def _diff_attn(proj, q_norm, k_norm, lam_vecs, subln, lam_init, batch, seq):
    n = proj.shape[0]
    blk = ATTN_BLOCK
    nq = seq // blk
    dv = 2 * DA_HEAD_DIM
    pairs = [(qi, ki) for qi in range(nq) for ki in range(qi + 1)]
    qi_tab = jnp.asarray([p[0] for p in pairs], jnp.int32)
    ki_tab = jnp.asarray([p[1] for p in pairs], jnp.int32)
    width = ATTN_HEADS * dv
    groups = HEADS // ATTN_HEADS
    q_col = (4 * HG_WIDTH) // width
    const = lambda shape: pl.BlockSpec(shape, lambda b, g, p, qt, kt: (0, 0))
    grid_spec = pltpu.PrefetchScalarGridSpec(
        num_scalar_prefetch=2,
        grid=(batch, groups, len(pairs)),
        in_specs=[pl.BlockSpec((blk, width), lambda b, g, p, qt, kt: (b * nq + qt[p], q_col + g)),
                  pl.BlockSpec((blk, width), lambda b, g, p, qt, kt: (b * nq + kt[p], q_col + groups + g)),
                  pl.BlockSpec((blk, width), lambda b, g, p, qt, kt: (b * nq + kt[p], q_col + 2 * groups + g)),
                  const((1, dv)), const((1, dv)), const((4, DA_HEAD_DIM)), const((dv, 1))],
        out_specs=pl.BlockSpec((blk, width), lambda b, g, p, qt, kt: (b * nq + qt[p], g)),
        scratch_shapes=[pltpu.VMEM((ATTN_HEADS, 2, dv, 2 * blk), BF16),
                        pltpu.VMEM((ATTN_HEADS, 1, 2 * blk), F32),
                        pltpu.VMEM((ATTN_HEADS, 1, 2 * blk), F32),
                        pltpu.VMEM((ATTN_HEADS, dv, 2 * blk), F32)])
    args = (qi_tab, ki_tab, proj, proj, proj, jnp.tile(q_norm, 2).reshape(1, -1), jnp.tile(k_norm, 2).reshape(1, -1),
            lam_vecs, subln.reshape(-1, 1))

    def run(bounded):
        return pl.pallas_call(
            functools.partial(_attn_kernel, lam_init=lam_init, bounded=bounded),
            grid_spec=grid_spec,
            out_shape=jax.ShapeDtypeStruct((n, HEADS * dv), BF16),
            compiler_params=_params("parallel", "parallel", "arbitrary"),
            name="diff_attn_bounded" if bounded else "diff_attn",
        )(*args)

    return lax.cond(_score_bound(q_norm, k_norm) <= MAX_SCORE_BOUND, lambda: run(True), lambda: run(False))


def _outproj_router_kernel(*refs, n_a):
    a_refs = refs[:n_a]
    (w_ref, x_ref, nw_ref, wr_hi_ref, wr_lo_ref, br_ref, tri_ref,
     x2_ref, meta_ref, rec_ref, cnt_ref, carry_ref) = refs[n_a:]

    @pl.when(pl.program_id(0) == 0)
    def _():
        carry_ref[...] = jnp.zeros_like(carry_ref)

    x2 = x_ref[...]
    off = 0
    for a_ref in a_refs:
        kk = a_ref.shape[1]
        x2 = x2 + _dot(a_ref[...], w_ref[off:off + kk, :])
        off += kk
    x2_ref[...] = x2

    h = _rms(x2, nw_ref[...])
    h_hi = h.astype(BF16)
    h_lo = (h - h_hi.astype(F32)).astype(BF16)
    wr_hi = wr_hi_ref[...]
    by_hi = _dot_nt(jnp.concatenate([wr_hi, wr_lo_ref[...]], axis=0), h_hi)
    lt = by_hi[:ROUTER_ROWS] + by_hi[ROUTER_ROWS:] + _dot_nt(wr_hi, h_lo) + br_ref[...]
    row = lambda r: lt[r:r + 1, :]

    gl = [row(g) for g in range(N_GROUPS)]
    gmax = functools.reduce(jnp.maximum, gl)
    gsel = jnp.where(gl[0] == gmax, 0.0, jnp.where(gl[1] == gmax, 1.0, jnp.where(gl[2] == gmax, 2.0, 3.0)))
    g_w = 1.0 / functools.reduce(jnp.add, [jnp.exp(x - gmax) for x in gl])

    el = []
    for e in range(EXPERTS_PER_GROUP):
        v = row(N_GROUPS + e)
        for g in range(1, N_GROUPS):
            v = jnp.where(gsel == float(g), row(N_GROUPS + EXPERTS_PER_GROUP * g + e), v)
        el.append(v)
    emax = functools.reduce(jnp.maximum, el)
    sel = []
    for e in range(EXPERTS_PER_GROUP):
        ahead = jnp.zeros_like(emax)
        for o in range(EXPERTS_PER_GROUP):
            if o < e:
                ahead = ahead + jnp.where(el[o] >= el[e], 1.0, 0.0)
            elif o > e:
                ahead = ahead + jnp.where(el[o] > el[e], 1.0, 0.0)
        sel.append(ahead < 2.0)
    pe = [jnp.where(sel[e], jnp.exp(el[e] - emax), 0.0) for e in range(EXPERTS_PER_GROUP)]
    scale = g_w / functools.reduce(jnp.add, pe)
    zero = jnp.zeros_like(emax)
    seen = zero
    lo_idx, hi_idx, c_lo, c_hi = zero, zero, zero, zero
    for e in range(EXPERTS_PER_GROUP):
        is_lo = sel[e] & (seen == 0.0)
        is_hi = sel[e] & (seen == 1.0)
        lo_idx = jnp.where(is_lo, float(e), lo_idx)
        hi_idx = jnp.where(is_hi, float(e), hi_idx)
        c_lo = jnp.where(is_lo, pe[e] * scale, c_lo)
        c_hi = jnp.where(is_hi, pe[e] * scale, c_hi)
        seen = seen + jnp.where(sel[e], 1.0, 0.0)
    pair = lo_idx * (7.0 - lo_idx) * 0.5 + (hi_idx - lo_idx - 1.0)
    bucket = gsel * float(N_PAIRS) + pair

    bid = lax.broadcasted_iota(jnp.int32, (ROUTER_ROWS, bucket.shape[1]), 0).astype(F32)
    onehot = jnp.where(bid == bucket, 1.0, 0.0)
    before = _dot(onehot.astype(BF16), tri_ref[...]) + carry_ref[...]
    rank = jnp.sum(onehot * before, axis=0, keepdims=True)
    carry = carry_ref[...] + jnp.sum(onehot, axis=1, keepdims=True)
    carry_ref[...] = carry
    cnt_ref[...] = carry
    meta = jnp.concatenate([bucket, rank, c_lo, c_hi, zero, zero, zero, zero], axis=0)
    meta_ref[...] = meta
    rec_ref[...] = jnp.concatenate([meta, jnp.zeros((LANE - meta.shape[0], meta.shape[1]), F32)], axis=0).T


def _outproj_router(a_list, w_out, x, n, norm_w, w_group, b_group, w_router, b_router):
    d = D_MODEL
    tm = ROUTER_TILE
    tok = lambda: pl.BlockSpec((tm, d), lambda i: (i, 0))
    wr = jnp.concatenate([w_group, w_router.transpose(1, 0, 2).reshape(d, N_EXPERTS)], axis=1)
    wr_t = jnp.zeros((ROUTER_ROWS, d), F32).at[:wr.shape[1]].set(wr.T)
    wr_hi = wr_t.astype(BF16)
    wr_lo = (wr_t - wr_hi.astype(F32)).astype(BF16)
    br = jnp.zeros((ROUTER_ROWS, 1), F32).at[:wr.shape[1], 0].set(
        jnp.concatenate([b_group, b_router.reshape(-1)]))
    tri = jnp.asarray(np.triu(np.ones((tm, tm), np.float32), 1), BF16)
    const = lambda shape: pl.BlockSpec(shape, lambda i: (0, 0))
    in_specs = [pl.BlockSpec((tm, a.shape[1]), lambda i: (i, 0)) for a in a_list]
    in_specs += [const((d, d)), tok(), const((1, d)),
                 const((ROUTER_ROWS, d)), const((ROUTER_ROWS, d)), const((ROUTER_ROWS, 1)), const((tm, tm))]
    return pl.pallas_call(
        functools.partial(_outproj_router_kernel, n_a=len(a_list)),
        grid=(n // tm,),
        in_specs=in_specs,
        out_specs=[tok(),
                   pl.BlockSpec((8, tm), lambda i: (0, i)),
                   pl.BlockSpec((tm, LANE), lambda i: (i, 0)),
                   const((ROUTER_ROWS, 1))],
        out_shape=[jax.ShapeDtypeStruct((n, d), F32),
                   jax.ShapeDtypeStruct((8, n), F32),
                   jax.ShapeDtypeStruct((n, LANE), F32),
                   jax.ShapeDtypeStruct((ROUTER_ROWS, 1), F32)],
        scratch_shapes=[pltpu.VMEM((ROUTER_ROWS, 1), F32)],
        compiler_params=_params("arbitrary"),
        name="outproj_router",
    )(*a_list, w_out, x, norm_w.reshape(1, d), wr_hi, wr_lo, br, tri)


def _gather_tokens(table, idx):
    m = idx.shape[0]
    info = plsc.get_sparse_core_info()
    n_workers = info.num_cores * info.num_subcores
    per_worker = m // n_workers
    n_chunks = per_worker // SC_CHUNK
    assert per_worker * n_workers == m and n_chunks * SC_CHUNK == per_worker and n_chunks % 2 == 0
    mesh = plsc.VectorSubcoreMesh(core_axis_name="core", subcore_axis_name="subcore")
    rows = pltpu.VMEM((SC_CHUNK,) + table.shape[1:], table.dtype)

    @functools.partial(
        pl.kernel, mesh=mesh,
        out_type=jax.ShapeDtypeStruct((m,) + table.shape[1:], table.dtype),
        scratch_types=[pltpu.VMEM((n_chunks, SC_CHUNK), jnp.int32), rows, rows,
                       pltpu.SemaphoreType.DMA, pltpu.SemaphoreType.DMA],
        compiler_params=pltpu.CompilerParams(use_tc_tiling_on_sc=True),
        name="gather_tokens")
    def gather(table_hbm, idx_hbm, out_hbm, idx_v, rows_a, rows_b, sem_a, sem_b):
        worker = lax.axis_index("subcore") * info.num_cores + lax.axis_index("core")
        bufs = ((rows_a, sem_a), (rows_b, sem_b))

        def fetch(j, buf):
            return pltpu.make_async_copy(table_hbm.at[idx_v.at[j]], bufs[buf][0], bufs[buf][1])

        pltpu.sync_copy(idx_hbm.at[worker], idx_v)
        fetch(0, 0).start()

        @pl.loop(0, n_chunks, step=2)
        def _(j0):
            for buf in range(2):
                j = j0 + buf
                fetch(j, buf).wait()

                @pl.when(j + 1 < n_chunks)
                def _():
                    fetch(j + 1, 1 - buf).start()

                off = pl.multiple_of(worker * per_worker + j * SC_CHUNK, SC_CHUNK)
                pltpu.sync_copy(bufs[buf][0], out_hbm.at[pl.ds(off, SC_CHUNK)])

    return gather(table, idx.reshape(n_workers, n_chunks, SC_CHUNK))


def _scatter_tokens(arrays, idx, m):
    n = idx.shape[0]
    na = len(arrays)
    info = plsc.get_sparse_core_info()
    n_workers = info.num_cores * info.num_subcores
    per_worker = n // n_workers
    n_chunks = per_worker // SC_CHUNK
    assert per_worker * n_workers == n and n_chunks * SC_CHUNK == per_worker and n_chunks % 2 == 0
    mesh = plsc.VectorSubcoreMesh(core_axis_name="core", subcore_axis_name="subcore")
    bufs = [pltpu.VMEM((SC_CHUNK,) + a.shape[1:], a.dtype) for a in arrays for _ in range(2)]

    @functools.partial(
        pl.kernel, mesh=mesh,
        out_type=[jax.ShapeDtypeStruct((m,) + a.shape[1:], a.dtype) for a in arrays],
        scratch_types=[pltpu.VMEM((n_chunks, SC_CHUNK), jnp.int32)] + bufs + [pltpu.SemaphoreType.DMA] * (2 * na),
        compiler_params=pltpu.CompilerParams(use_tc_tiling_on_sc=True),
        name="scatter_tokens")
    def scatter(*refs):
        ins, idx_hbm, outs = refs[:na], refs[na], refs[na + 1:2 * na + 1]
        idx_v = refs[2 * na + 1]
        buf_v = refs[2 * na + 2:4 * na + 2]
        sem_v = refs[4 * na + 2:]
        worker = lax.axis_index("subcore") * info.num_cores + lax.axis_index("core")

        def load(a, j, b):
            off = pl.multiple_of(worker * per_worker + j * SC_CHUNK, SC_CHUNK)
            return pltpu.make_async_copy(ins[a].at[pl.ds(off, SC_CHUNK)], buf_v[2 * a + b], sem_v[2 * a + b])

        pltpu.sync_copy(idx_hbm.at[worker], idx_v)
        for a in range(na):
            load(a, 0, 0).start()

        @pl.loop(0, n_chunks, step=2)
        def _(j0):
            for b in range(2):
                j = j0 + b
                for a in range(na):
                    load(a, j, b).wait()

                @pl.when(j + 1 < n_chunks)
                def _():
                    for a in range(na):
                        load(a, j + 1, 1 - b).start()

                for a in range(na):
                    pltpu.sync_copy(buf_v[2 * a + b], outs[a].at[idx_v.at[j]])

    return scatter(*arrays, idx.reshape(n_workers, n_chunks, SC_CHUNK))


def _moe_kernel(step_ids, e_lo, e_hi, n_valid, n_tiles, x_ref, rec_ref, nw_ref, *refs):
    del step_ids, e_lo, e_hi
    s = pl.program_id(0)
    o_ref = refs[-1]

    @pl.when(s * MOE_STEP_TILES < n_tiles[0])
    def _():
        for i in range(MOE_STEP_TILES):
            g_lo_ref, g_hi_ref, u_lo_ref, u_hi_ref, d_lo_ref, d_hi_ref = refs[6 * i:6 * (i + 1)]
            rows = pl.ds(i * MOE_TILE, MOE_TILE)
            valid = lax.broadcasted_iota(jnp.int32, (MOE_TILE, 1), 0) < n_valid[s * MOE_STEP_TILES + i]
            x = jnp.where(valid, x_ref[rows, :], 0.0)
            h = _rms(x, nw_ref[...]).astype(BF16)
            rec = jnp.where(valid, rec_ref[rows, :], 0.0)

            def expert(g_ref, u_ref, d_ref):
                act = _silu(_dot(h, g_ref[0])) * _dot(h, u_ref[0])
                return _dot(act.astype(BF16), d_ref[0])

            y = (rec[:, 2:3] * expert(g_lo_ref, u_lo_ref, d_lo_ref)
                 + rec[:, 3:4] * expert(g_hi_ref, u_hi_ref, d_hi_ref))
            o_ref[rows, :] = jnp.where(valid, x_ref[rows, :], 0.0) + y


def _moe(x2, n, meta, rec, cnt, norm_w, w_gate, w_up, w_down, layer, w_down_bf16=None):
    d = D_MODEL
    tm = MOE_TILE
    max_tiles = n // tm + N_BUCKETS
    n_slots = max_tiles * tm
    counts = cnt[:N_BUCKETS, 0].astype(jnp.int32)
    tiles_per_bucket = (counts + tm - 1) // tm
    tile_end = jnp.cumsum(tiles_per_bucket)
    tile_start = tile_end - tiles_per_bucket
    n_tiles = tile_end[-1]
    in_bucket = meta[0][:, None] == jnp.arange(N_BUCKETS, dtype=F32)[None, :]
    dest = meta[1].astype(jnp.int32) + jnp.sum(jnp.where(in_bucket, tile_start[None, :] * tm, 0), axis=1)
    x_sorted, rec_sorted = _scatter_tokens([x2, rec], dest, n_slots)
    first = layer * N_EXPERTS
    w_gate, w_up = (_to_bf16(w, first=first, count=N_EXPERTS) for w in (w_gate, w_up))
    w_down = _to_bf16(w_down, first=first, count=N_EXPERTS) if w_down_bf16 is None else w_down_bf16

    k = MOE_STEP_TILES
    tile_raw = jnp.arange(max_tiles, dtype=jnp.int32)
    tile_ids = jnp.minimum(tile_raw, n_tiles - 1)
    step_ids = jnp.minimum(jnp.arange(max_tiles // k, dtype=jnp.int32), (n_tiles - 1) // k)
    of_bucket = lambda table: jnp.sum(jnp.where(tile_bucket[:, None] == jnp.arange(N_BUCKETS)[None, :], table[None, :], 0), axis=1)
    tile_bucket = jnp.sum((tile_ids[:, None] >= tile_end[None, :]).astype(jnp.int32), axis=1)
    n_valid = jnp.clip(of_bucket(counts) - (tile_ids - of_bucket(tile_start)) * tm, 0, tm)
    n_valid = jnp.where(tile_raw < n_tiles, n_valid, 0)
    group = tile_bucket // N_PAIRS
    pair = tile_bucket % N_PAIRS
    pair_lo = jnp.sum(jnp.where(pair[:, None] == jnp.arange(N_PAIRS)[None, :], jnp.asarray(PAIR_LO, jnp.int32)[None, :], 0), axis=1)
    pair_hi = jnp.sum(jnp.where(pair[:, None] == jnp.arange(N_PAIRS)[None, :], jnp.asarray(PAIR_HI, jnp.int32)[None, :], 0), axis=1)
    e_lo = group * EXPERTS_PER_GROUP + pair_lo
    e_hi = group * EXPERTS_PER_GROUP + pair_hi

    step_rows = lambda s, sid, lo, hi, nv, nt: (sid[s], 0)
    weight_specs, weight_args = [], []
    for i in range(k):
        wlo = lambda s, sid, lo, hi, nv, nt, i=i: (lo[s * k + i], 0, 0)
        whi = lambda s, sid, lo, hi, nv, nt, i=i: (hi[s * k + i], 0, 0)
        weight_specs += [pl.BlockSpec((1, d, D_EXPERT), wlo), pl.BlockSpec((1, d, D_EXPERT), whi),
                         pl.BlockSpec((1, d, D_EXPERT), wlo), pl.BlockSpec((1, d, D_EXPERT), whi),
                         pl.BlockSpec((1, D_EXPERT, d), wlo), pl.BlockSpec((1, D_EXPERT, d), whi)]
        weight_args += [w_gate, w_gate, w_up, w_up, w_down, w_down]
    grid_spec = pltpu.PrefetchScalarGridSpec(
        num_scalar_prefetch=5,
        grid=(max_tiles // k,),
        in_specs=[pl.BlockSpec((k * tm, d), step_rows), pl.BlockSpec((k * tm, LANE), step_rows),
                  pl.BlockSpec((1, d), lambda s, sid, lo, hi, nv, nt: (0, 0))] + weight_specs,
        out_specs=pl.BlockSpec((k * tm, d), step_rows))
    out_sorted = pl.pallas_call(
        _moe_kernel,
        grid_spec=grid_spec,
        out_shape=jax.ShapeDtypeStruct((n_slots, d), F32),
        compiler_params=_params("arbitrary"),
        name="moe",
    )(step_ids, e_lo, e_hi, n_valid, n_tiles.reshape(1), x_sorted, rec_sorted, norm_w.reshape(1, d), *weight_args)
    return _gather_tokens(out_sorted, dest)


def kernel(x, attn_norm, ffn_norm, even_w_in, even_w_out, hg_lb_logits, hg_o_norm, da_q_norm, da_k_norm, da_lambda, da_subln, odd_w_in, gla_w_alpha, gla_b_alpha, gla_o_norm, odd_w_out, moe_w_group, moe_b_group, moe_w_router, moe_b_router, moe_w_gate, moe_w_up, moe_w_down):
    batch, seq, d = x.shape
    depth = attn_norm.shape[0]
    n = batch * seq
    xf = x.reshape(n, d)

    lb_sm = jax.nn.softmax(hg_lb_logits.astype(F32), axis=0)
    lb_all = jnp.cumsum(lb_sm, axis=0) - lb_sm[0:1]
    rank = gla_w_alpha.shape[1]

    experts = lambda w: w.reshape((-1,) + w.shape[2:])
    moe_w_gate, moe_w_up, moe_w_down = experts(moe_w_gate), experts(moe_w_up), experts(moe_w_down)

    down_ahead = None
    for layer in range(depth):
        j = layer // 2
        if layer % 2 == 0:
            lam_init = 0.8 - 0.6 * math.exp(-0.3 * layer)
            proj = _inproj(xf, n, attn_norm[layer], _to_bf16(even_w_in, first=j, count=1)[0])
            o_a = _hgrn(proj, lb_all[j], hg_o_norm[j], batch, seq)
            o_b = _diff_attn(proj, da_q_norm[j], da_k_norm[j], da_lambda[j], da_subln[j], lam_init, batch, seq)
            mix = [o_a, o_b]
            w_out = _to_bf16(even_w_out, first=j, count=1)[0]
        else:
            w_alpha = jnp.pad(gla_w_alpha[j], ((0, GLA_RANK_PAD - rank), (0, 0))).astype(BF16)
            w_in =_to_bf16(jnp.swapaxes(odd_w_in, 1, 2), odd_w_in.shape[2] + GLA_RANK_PAD - rank, first=j, count=1)[0]
            proj = _inproj(xf, n, attn_norm[layer], w_in, w_transposed=True)
            mix = [_gla(proj, w_alpha, gla_b_alpha[j], gla_o_norm[j], batch, seq)]
            w_out = _to_bf16(odd_w_out, first=j, count=1)[0]
        x2, meta, rec, cnt = _outproj_router(mix, w_out, xf, n, ffn_norm[layer], moe_w_group[layer],
                                        moe_b_group[layer], moe_w_router[layer], moe_b_router[layer])
        xf = _moe(x2, n, meta, rec, cnt, ffn_norm[layer], moe_w_gate, moe_w_up, moe_w_down, layer, down_ahead)
        down_ahead = (_to_bf16(moe_w_down, first=(layer + 1) * N_EXPERTS, count=N_EXPERTS)
                      if layer + 1 < depth else None)
    return xf.reshape(batch, seq, d)
```

```python
import functools
import math

import numpy as np
import jax
import jax.numpy as jnp
from jax import lax
from jax.experimental import pallas as pl
from jax.experimental.pallas import tpu as pltpu
from jax.experimental.pallas import tpu_sc as plsc

F32 = jnp.float32
BF16 = jnp.bfloat16
EPS = 1e-6
LOG2_E = 1.4426950408889634

D_MODEL = 1024
LANE = 128
HEADS = 4
DK = 128
HG_WIDTH = 512
DA_HEAD_DIM = 64
GLA_DV = 256
GLA_RANK_PAD = 128
GLA_TAU = 16.0
N_GROUPS = 4
EXPERTS_PER_GROUP = 4
N_EXPERTS = 16
D_EXPERT = 512
N_PAIRS = 6
N_BUCKETS = N_GROUPS * N_PAIRS
ROUTER_ROWS = 32

CHUNK = 128
GLA_STEP_CHUNKS = 4
LEVELS = 7
ATTN_BLOCK = 512
ATTN_HEADS = 4
SCORE_BOUND_MARGIN = 1.05
MAX_SCORE_BOUND = 60.0
ROW_TILE = 1024
ROUTER_TILE = 1024
MOE_TILE = 256
MOE_STEP_TILES = 2
SC_CHUNK = 32
VMEM_LIMIT = 56 * 1024 * 1024
CAST_BLOCK_BYTES = 8 * 1024 * 1024

PAIR_LO = (0, 0, 0, 1, 1, 2)
PAIR_HI = (1, 2, 3, 2, 3, 3)


def _dot(a, b):
    return jnp.dot(a, b, preferred_element_type=F32)


def _dot_nt(a, b):
    return lax.dot_general(a, b, (((1,), (1,)), ((), ())), preferred_element_type=F32)


def _dot_tn(a, b):
    return lax.dot_general(a, b, (((0,), (0,)), ((), ())), preferred_element_type=F32)


def _rms(x, w):
    return x * lax.rsqrt(jnp.mean(x * x, axis=-1, keepdims=True) + EPS) * w


def _silu(x):
    return x * jax.nn.sigmoid(x)


def _params(*sem):
    return pltpu.CompilerParams(dimension_semantics=sem, vmem_limit_bytes=VMEM_LIMIT)


def _cast_kernel(w_ref, o_ref):
    k = w_ref.shape[1]
    o_ref[:, :k, :] = w_ref[...].astype(BF16)
    if o_ref.shape[1] > k:
        o_ref[:, k:, :] = jnp.zeros((o_ref.shape[0], o_ref.shape[1] - k, o_ref.shape[2]), BF16)


def _to_bf16(w, k_pad=None, first=0, count=None):
    _, k, n = w.shape
    l = count or w.shape[0]
    k_pad = k_pad or k
    per_matrix = k * n * 4
    lb = max([c for c in range(1, l + 1) if l % c == 0 and first % c == 0 and c * per_matrix <= CAST_BLOCK_BYTES] or [1])
    nb = n
    while lb == 1 and k * nb * 4 > CAST_BLOCK_BYTES and nb % (2 * LANE) == 0:
        nb //= 2
    return pl.pallas_call(
        _cast_kernel,
        grid=(l // lb, n // nb),
        in_specs=[pl.BlockSpec((lb, k, nb), lambda i, j: (first // lb + i, 0, j))],
        out_specs=pl.BlockSpec((lb, k_pad, nb), lambda i, j: (i, 0, j)),
        out_shape=jax.ShapeDtypeStruct((l, k_pad, n), BF16),
        compiler_params=_params("parallel", "parallel"),
        name="to_bf16",
    )(w)


def _inproj_kernel(x_ref, nw_ref, w_ref, *refs, w_transposed):
    o_ref = refs[-1]
    h = _rms(x_ref[...], nw_ref[...]).astype(BF16)
    o_ref[...] = _dot_nt(h, w_ref[...]) if w_transposed else _dot(h, w_ref[...])


def _inproj(x, n, norm_w, w, w_transposed=False, after=()):
    d = D_MODEL
    n_out = w.shape[0] if w_transposed else w.shape[1]
    return pl.pallas_call(
        functools.partial(_inproj_kernel, w_transposed=w_transposed),
        grid=(n // ROW_TILE,),
        in_specs=[pl.BlockSpec((ROW_TILE, d), lambda i: (i, 0)),
                  pl.BlockSpec((1, d), lambda i: (0, 0)),
                  pl.BlockSpec(w.shape, lambda i: (0, 0))] + [pl.BlockSpec(memory_space=pl.ANY) for _ in after],
        out_specs=pl.BlockSpec((ROW_TILE, n_out), lambda i: (i, 0)),
        out_shape=jax.ShapeDtypeStruct((n, n_out), F32),
        compiler_params=_params("parallel"),
        name="inproj",
    )(x, norm_w.reshape(1, d), w, *after)


def _decay_tables():
    c = CHUNK
    a = np.zeros((LEVELS + 2, c, c), np.float32)
    t = np.arange(c)
    for l in range(LEVELS):
        s = 1 << l
        for r in range(c):
            mid = (r // (2 * s)) * 2 * s + s - 1
            if (r >> l) & 1:
                a[l, r] = (t > mid) & (t <= r)
            else:
                a[l, r] = (t > r) & (t <= mid)
    a[LEVELS] = t[None, :] <= t[:, None]
    a[LEVELS + 1] = t[None, :] > t[:, None]
    a = a.reshape((LEVELS + 2) * c, c)
    a3 = np.concatenate([a, a], axis=1)
    i, j = np.meshgrid(t, t, indexing="ij")
    x = np.maximum(i ^ j, 1)
    lvl = np.where(j < i, np.floor(np.log2(x)).astype(np.int32), np.where(i == j, LEVELS, -1))
    return jnp.asarray(a3, BF16), jnp.asarray(lvl, jnp.int32)


def _decay_factors(a3, g):
    g_hi = g.astype(BF16)
    g_lo = (g - g_hi.astype(F32)).astype(BF16)
    return jnp.exp(_dot(a3, jnp.concatenate([g_hi, g_lo], axis=0)))


def _gla_chunk(q, k, v, f, st_ref, lvl, rowid):
    c = CHUNK
    sc = jnp.where(lvl == LEVELS, jnp.sum(q * k, axis=-1, keepdims=True), 0.0)
    for l in range(LEVELS):
        s = 1 << l
        if s % 8 == 0:
            qk = jnp.concatenate([(q if b & 1 else k)[b * s:(b + 1) * s] for b in range(c // s)], axis=0)
        else:
            qk = jnp.where(((rowid >> l) & 1) == 1, q, k)
        m = (qk * f[l * c:(l + 1) * c]).astype(BF16)
        sc = jnp.where(lvl == l, _dot_nt(m, m), sc)
    qd = (q * f[LEVELS * c:(LEVELS + 1) * c]).astype(BF16)
    kd = (k * f[(LEVELS + 1) * c:(LEVELS + 2) * c]).astype(BF16)
    vb = v.astype(BF16)
    st = st_ref[...]
    o = _dot(sc.astype(BF16), vb) + _dot_nt(qd, st.astype(BF16))
    chunk_decay = f[(LEVELS + 1) * c - 1:(LEVELS + 1) * c, :]
    st_ref[...] = st * chunk_decay + _dot_tn(vb, kd)
    return o


def _hgrn_kernel(q_ref, f_ref, i_ref, g_ref, lb_ref, on_ref, a3_ref, lvl_ref, o_ref, st_ref):
    @pl.when(pl.program_id(1) == 0)
    def _():
        st_ref[...] = jnp.zeros_like(st_ref)

    lvl = lvl_ref[...]
    rowid = lax.broadcasted_iota(jnp.int32, (CHUNK, 1), 0)
    lb = lb_ref[...]
    for c in range(GLA_STEP_CHUNKS):
        rows = slice(c * CHUNK, (c + 1) * CHUNK)
        fl = f_ref[rows, :]
        f = _decay_factors(a3_ref[...], jnp.log(lb + (1.0 - lb) * jax.nn.sigmoid(fl)))
        key_in = (1.0 - lb) * jax.nn.sigmoid(-fl)
        for h in range(HEADS):
            sl = slice(h * DK, (h + 1) * DK)
            q = _silu(q_ref[rows, sl]) * (DK ** -0.5)
            o = _gla_chunk(q, key_in[:, sl], i_ref[rows, sl], f[:, sl], st_ref.at[h], lvl, rowid)
            o = _rms(o, on_ref[...])
            o_ref[rows, sl] = (o * _silu(g_ref[rows, sl])).astype(BF16)


def _hgrn(proj, lb, o_norm, batch, seq):
    n = proj.shape[0]
    rows = CHUNK * GLA_STEP_CHUNKS
    nt = seq // rows
    a3, lvl = _decay_tables()
    col = lambda j: pl.BlockSpec((rows, HG_WIDTH), lambda b, t: (b * nt + t, j))
    const = lambda shape: pl.BlockSpec(shape, lambda b, t: (0, 0))
    return pl.pallas_call(
        _hgrn_kernel,
        grid=(batch, nt),
        in_specs=[col(0), col(1), col(2), col(3), const((1, HG_WIDTH)), const((1, DK)),
                  const(a3.shape), const(lvl.shape)],
        out_specs=pl.BlockSpec((rows, HG_WIDTH), lambda b, t: (b * nt + t, 0)),
        out_shape=jax.ShapeDtypeStruct((n, HG_WIDTH), BF16),
        scratch_shapes=[pltpu.VMEM((HEADS, DK, DK), F32)],
        compiler_params=_params("parallel", "arbitrary"),
        name="hgrn",
    )(proj, proj, proj, proj, lb.reshape(1, HG_WIDTH), o_norm.reshape(1, DK), a3, lvl)


def _log_sigmoid(x):
    return jnp.minimum(x, 0.0) - jnp.log1p(jnp.exp(-jnp.abs(x)))


def _gla_kernel(q_ref, k_ref, v_ref, g_ref, a_ref, wa_ref, ba_ref, on_ref, a3_ref, lvl_ref, o_ref, st_ref):
    @pl.when(pl.program_id(1) == 0)
    def _():
        st_ref[...] = jnp.zeros_like(st_ref)

    lvl = lvl_ref[...]
    rowid = lax.broadcasted_iota(jnp.int32, (CHUNK, 1), 0)
    for c in range(GLA_STEP_CHUNKS):
        rows = slice(c * CHUNK, (c + 1) * CHUNK)
        gate_logit = _dot(a_ref[rows, :].astype(BF16), wa_ref[...]) + ba_ref[...]
        f = _decay_factors(a3_ref[...], _log_sigmoid(gate_logit) * (1.0 / GLA_TAU))
        for h in range(HEADS):
            sl = slice(h * DK, (h + 1) * DK)
            sv = slice(h * GLA_DV, (h + 1) * GLA_DV)
            q = q_ref[rows, sl] * (DK ** -0.5)
            o = _gla_chunk(q, k_ref[rows, sl], v_ref[rows, sv], f[:, sl], st_ref.at[h], lvl, rowid)
            o = _rms(o, on_ref[...])
            o_ref[rows, sv] = (o * _silu(g_ref[rows, sv])).astype(BF16)


def _gla(proj, w_alpha, b_alpha, o_norm, batch, seq):
    n = proj.shape[0]
    rows = CHUNK * GLA_STEP_CHUNKS
    nt = seq // rows
    qk_w = HEADS * DK
    v_w = HEADS * GLA_DV
    a3, lvl = _decay_tables()
    row = lambda b, t: b * nt + t
    const = lambda shape: pl.BlockSpec(shape, lambda b, t: (0, 0))
    return pl.pallas_call(
        _gla_kernel,
        grid=(batch, nt),
        in_specs=[pl.BlockSpec((rows, qk_w), lambda b, t: (row(b, t), 0)),
                  pl.BlockSpec((rows, qk_w), lambda b, t: (row(b, t), 1)),
                  pl.BlockSpec((rows, v_w), lambda b, t: (row(b, t), 1)),
                  pl.BlockSpec((rows, v_w), lambda b, t: (row(b, t), 2)),
                  pl.BlockSpec((rows, GLA_RANK_PAD), lambda b, t: (row(b, t), (2 * qk_w + 2 * v_w) // GLA_RANK_PAD)),
                  const((GLA_RANK_PAD, qk_w)), const((1, qk_w)), const((1, GLA_DV)),
                  const(a3.shape), const(lvl.shape)],
        out_specs=pl.BlockSpec((rows, v_w), lambda b, t: (row(b, t), 0)),
        out_shape=jax.ShapeDtypeStruct((n, v_w), BF16),
        scratch_shapes=[pltpu.VMEM((HEADS, GLA_DV, DK), F32)],
        compiler_params=_params("parallel", "arbitrary"),
        name="gla",
    )(proj, proj, proj, proj, proj, w_alpha, b_alpha.reshape(1, qk_w), o_norm.reshape(1, GLA_DV), a3, lvl)


def _qk_norm(x, w2):
    lane = lax.broadcasted_iota(jnp.int32, x.shape, 1)
    first = lane < DA_HEAD_DIM
    sq = x * x
    ms1 = jnp.sum(jnp.where(first, sq, 0.0), axis=-1, keepdims=True) * (1.0 / DA_HEAD_DIM)
    ms2 = jnp.sum(jnp.where(first, 0.0, sq), axis=-1, keepdims=True) * (1.0 / DA_HEAD_DIM)
    r = jnp.where(first, lax.rsqrt(ms1 + EPS), lax.rsqrt(ms2 + EPS))
    return x * r * w2


def _score_bound(q_norm_w, k_norm_w):
    return (jnp.max(jnp.abs(q_norm_w)) * jnp.max(jnp.abs(k_norm_w))
            * (DA_HEAD_DIM * DA_HEAD_DIM ** -0.5 * LOG2_E * SCORE_BOUND_MARGIN))


def _attn_kernel(qi_tab, ki_tab, q_ref, k_ref, v_ref, qn_ref, kn_ref, lam_ref, sub_ref, o_ref,
                 qs_ref, m_ref, l_ref, acc_ref, *, lam_init, bounded):
    p = pl.program_id(2)
    qi = qi_tab[p]
    ki = ki_tab[p]
    tq = ATTN_BLOCK
    dv = 2 * DA_HEAD_DIM
    heads = [slice(h * dv, (h + 1) * dv) for h in range(ATTN_HEADS)]

    @pl.when(ki == 0)
    def _():
        for h, sl in enumerate(heads):
            qn = _qk_norm(q_ref[:, sl], qn_ref[...])
            qt = (qn * (DA_HEAD_DIM ** -0.5 * LOG2_E)).T
            top = lax.broadcasted_iota(jnp.int32, qt.shape, 0) < DA_HEAD_DIM
            q1 = jnp.where(top, qt, 0.0).astype(BF16)
            q2 = jnp.where(top, 0.0, qt).astype(BF16)
            for copy in range(1 if bounded else 2):
                qs_ref[h, copy, :, :tq] = q1
                qs_ref[h, copy, :, tq:] = q2
            if bounded:
                bound = _score_bound(qn_ref[...], kn_ref[...])
                row = lax.broadcasted_iota(jnp.int32, (dv, 2 * tq), 0)
                qs_ref[h, 1] = jnp.where(row == 0, -bound, 0.0).astype(BF16)
        if not bounded:
            m_ref[...] = jnp.full_like(m_ref, -jnp.inf)
        l_ref[...] = jnp.zeros_like(l_ref)
        acc_ref[...] = jnp.zeros_like(acc_ref)

    def step(masked):
        tk = k_ref.shape[0]
        ones = jnp.ones((16, tk), BF16)
        unit = jnp.where(lax.broadcasted_iota(jnp.int32, (tk, dv), 1) == 0, 1.0, 0.0).astype(BF16)
        for h, sl in enumerate(heads):
            kn = _qk_norm(k_ref[:, sl], kn_ref[...]).astype(BF16)

            def mask(s):
                if not masked:
                    return s
                key = lax.broadcasted_iota(jnp.int32, s.shape, 0)
                qry = lax.broadcasted_iota(jnp.int32, s.shape, 1)
                qry = jnp.where(qry >= tq, qry - tq, qry)
                return jnp.where(key <= qry, s, -jnp.inf)

            if bounded:
                shifted = mask(_dot(jnp.concatenate([kn, unit], axis=1), qs_ref[h].reshape(2 * dv, 2 * tq)))
                alpha = 1.0
            else:
                m_prev = m_ref[h]
                m_new = jnp.maximum(m_prev, jnp.max(mask(_dot(kn, qs_ref[h, 0])), axis=0, keepdims=True))
                alpha = jnp.exp2(m_prev - m_new)
                shifted = mask(_dot(kn, qs_ref[h, 1])) - m_new
                m_ref[h] = m_new
            pr = jnp.exp2(shifted).astype(BF16) if bounded else jnp.exp2(shifted.astype(BF16))
            vt = jnp.concatenate([v_ref[:, sl].T.astype(BF16), ones], axis=0)
            pv = _dot(vt, pr)
            l_ref[h] = alpha * l_ref[h] + pv[dv:dv + 1]
            acc_ref[h] = alpha * acc_ref[h] + pv[:dv]

    @pl.when(ki < qi)
    def _():
        step(False)

    @pl.when(ki == qi)
    def _():
        step(True)
        lv = lam_ref[...]
        lam = (jnp.exp(jnp.sum(lv[0:1] * lv[1:2], axis=-1, keepdims=True))
               - jnp.exp(jnp.sum(lv[2:3] * lv[3:4], axis=-1, keepdims=True)) + lam_init)
        for h, sl in enumerate(heads):
            o1 = acc_ref[h, :, :tq] / l_ref[h, :, :tq]
            o2 = acc_ref[h, :, tq:] / l_ref[h, :, tq:]
            o = o1 - lam * o2
            o = o * lax.rsqrt(jnp.mean(o * o, axis=0, keepdims=True) + EPS) * sub_ref[...] * (1.0 - lam_init)
            o_ref[:, sl] = o.T.astype(BF16)


def _diff_attn(proj, q_norm, k_norm, lam_vecs, subln, lam_init, batch, seq):
    n = proj.shape[0]
    blk = ATTN_BLOCK
    nq = seq // blk
    dv = 2 * DA_HEAD_DIM
    pairs = [(qi, ki) for qi in range(nq) for ki in range(qi + 1)]
    qi_tab = jnp.asarray([p[0] for p in pairs], jnp.int32)
    ki_tab = jnp.asarray([p[1] for p in pairs], jnp.int32)
    width = ATTN_HEADS * dv
    groups = HEADS // ATTN_HEADS
    q_col = (4 * HG_WIDTH) // width
    const = lambda shape: pl.BlockSpec(shape, lambda b, g, p, qt, kt: (0, 0))
    grid_spec = pltpu.PrefetchScalarGridSpec(
        num_scalar_prefetch=2,
        grid=(batch, groups, len(pairs)),
        in_specs=[pl.BlockSpec((blk, width), lambda b, g, p, qt, kt: (b * nq + qt[p], q_col + g)),
                  pl.BlockSpec((blk, width), lambda b, g, p, qt, kt: (b * nq + kt[p], q_col + groups + g)),
                  pl.BlockSpec((blk, width), lambda b, g, p, qt, kt: (b * nq + kt[p], q_col + 2 * groups + g)),
                  const((1, dv)), const((1, dv)), const((4, DA_HEAD_DIM)), const((dv, 1))],
        out_specs=pl.BlockSpec((blk, width), lambda b, g, p, qt, kt: (b * nq + qt[p], g)),
        scratch_shapes=[pltpu.VMEM((ATTN_HEADS, 2, dv, 2 * blk), BF16),
                        pltpu.VMEM((ATTN_HEADS, 1, 2 * blk), F32),
                        pltpu.VMEM((ATTN_HEADS, 1, 2 * blk), F32),
                        pltpu.VMEM((ATTN_HEADS, dv, 2 * blk), F32)])
    args = (qi_tab, ki_tab, proj, proj, proj, jnp.tile(q_norm, 2).reshape(1, -1), jnp.tile(k_norm, 2).reshape(1, -1),
            lam_vecs, subln.reshape(-1, 1))

    def run(bounded):
        return pl.pallas_call(
            functools.partial(_attn_kernel, lam_init=lam_init, bounded=bounded),
            grid_spec=grid_spec,
            out_shape=jax.ShapeDtypeStruct((n, HEADS * dv), BF16),
            compiler_params=_params("parallel", "parallel", "arbitrary"),
            name="diff_attn_bounded" if bounded else "diff_attn",
        )(*args)

    return lax.cond(_score_bound(q_norm, k_norm) <= MAX_SCORE_BOUND, lambda: run(True), lambda: run(False))


def _outproj_router_kernel(*refs, n_a):
    a_refs = refs[:n_a]
    (w_ref, x_ref, nw_ref, wr_hi_ref, wr_lo_ref, br_ref, tri_ref,
     x2_ref, meta_ref, rec_ref, cnt_ref, carry_ref) = refs[n_a:]

    @pl.when(pl.program_id(0) == 0)
    def _():
        carry_ref[...] = jnp.zeros_like(carry_ref)

    x2 = x_ref[...]
    off = 0
    for a_ref in a_refs:
        kk = a_ref.shape[1]
        x2 = x2 + _dot(a_ref[...], w_ref[off:off + kk, :])
        off += kk
    x2_ref[...] = x2

    h = _rms(x2, nw_ref[...])
    h_hi = h.astype(BF16)
    h_lo = (h - h_hi.astype(F32)).astype(BF16)
    wr_hi = wr_hi_ref[...]
    by_hi = _dot_nt(jnp.concatenate([wr_hi, wr_lo_ref[...]], axis=0), h_hi)
    lt = by_hi[:ROUTER_ROWS] + by_hi[ROUTER_ROWS:] + _dot_nt(wr_hi, h_lo) + br_ref[...]
    row = lambda r: lt[r:r + 1, :]

    gl = [row(g) for g in range(N_GROUPS)]
    gmax = functools.reduce(jnp.maximum, gl)
    gsel = jnp.where(gl[0] == gmax, 0.0, jnp.where(gl[1] == gmax, 1.0, jnp.where(gl[2] == gmax, 2.0, 3.0)))
    g_w = 1.0 / functools.reduce(jnp.add, [jnp.exp(x - gmax) for x in gl])

    el = []
    for e in range(EXPERTS_PER_GROUP):
        v = row(N_GROUPS + e)
        for g in range(1, N_GROUPS):
            v = jnp.where(gsel == float(g), row(N_GROUPS + EXPERTS_PER_GROUP * g + e), v)
        el.append(v)
    emax = functools.reduce(jnp.maximum, el)
    sel = []
    for e in range(EXPERTS_PER_GROUP):
        ahead = jnp.zeros_like(emax)
        for o in range(EXPERTS_PER_GROUP):
            if o < e:
                ahead = ahead + jnp.where(el[o] >= el[e], 1.0, 0.0)
            elif o > e:
                ahead = ahead + jnp.where(el[o] > el[e], 1.0, 0.0)
        sel.append(ahead < 2.0)
    pe = [jnp.where(sel[e], jnp.exp(el[e] - emax), 0.0) for e in range(EXPERTS_PER_GROUP)]
    scale = g_w / functools.reduce(jnp.add, pe)
    zero = jnp.zeros_like(emax)
    seen = zero
    lo_idx, hi_idx, c_lo, c_hi = zero, zero, zero, zero
    for e in range(EXPERTS_PER_GROUP):
        is_lo = sel[e] & (seen == 0.0)
        is_hi = sel[e] & (seen == 1.0)
        lo_idx = jnp.where(is_lo, float(e), lo_idx)
        hi_idx = jnp.where(is_hi, float(e), hi_idx)
        c_lo = jnp.where(is_lo, pe[e] * scale, c_lo)
        c_hi = jnp.where(is_hi, pe[e] * scale, c_hi)
        seen = seen + jnp.where(sel[e], 1.0, 0.0)
    pair = lo_idx * (7.0 - lo_idx) * 0.5 + (hi_idx - lo_idx - 1.0)
    bucket = gsel * float(N_PAIRS) + pair

    bid = lax.broadcasted_iota(jnp.int32, (ROUTER_ROWS, bucket.shape[1]), 0).astype(F32)
    onehot = jnp.where(bid == bucket, 1.0, 0.0)
    before = _dot(onehot.astype(BF16), tri_ref[...]) + carry_ref[...]
    rank = jnp.sum(onehot * before, axis=0, keepdims=True)
    carry = carry_ref[...] + jnp.sum(onehot, axis=1, keepdims=True)
    carry_ref[...] = carry
    cnt_ref[...] = carry
    meta = jnp.concatenate([bucket, rank, c_lo, c_hi, zero, zero, zero, zero], axis=0)
    meta_ref[...] = meta
    rec_ref[...] = jnp.concatenate([meta, jnp.zeros((LANE - meta.shape[0], meta.shape[1]), F32)], axis=0).T


def _outproj_router(a_list, w_out, x, n, norm_w, w_group, b_group, w_router, b_router):
    d = D_MODEL
    tm = ROUTER_TILE
    tok = lambda: pl.BlockSpec((tm, d), lambda i: (i, 0))
    wr = jnp.concatenate([w_group, w_router.transpose(1, 0, 2).reshape(d, N_EXPERTS)], axis=1)
    wr_t = jnp.zeros((ROUTER_ROWS, d), F32).at[:wr.shape[1]].set(wr.T)
    wr_hi = wr_t.astype(BF16)
    wr_lo = (wr_t - wr_hi.astype(F32)).astype(BF16)
    br = jnp.zeros((ROUTER_ROWS, 1), F32).at[:wr.shape[1], 0].set(
        jnp.concatenate([b_group, b_router.reshape(-1)]))
    tri = jnp.asarray(np.triu(np.ones((tm, tm), np.float32), 1), BF16)
    const = lambda shape: pl.BlockSpec(shape, lambda i: (0, 0))
    in_specs = [pl.BlockSpec((tm, a.shape[1]), lambda i: (i, 0)) for a in a_list]
    in_specs += [const((d, d)), tok(), const((1, d)),
                 const((ROUTER_ROWS, d)), const((ROUTER_ROWS, d)), const((ROUTER_ROWS, 1)), const((tm, tm))]
    return pl.pallas_call(
        functools.partial(_outproj_router_kernel, n_a=len(a_list)),
        grid=(n // tm,),
        in_specs=in_specs,
        out_specs=[tok(),
                   pl.BlockSpec((8, tm), lambda i: (0, i)),
                   pl.BlockSpec((tm, LANE), lambda i: (i, 0)),
                   const((ROUTER_ROWS, 1))],
        out_shape=[jax.ShapeDtypeStruct((n, d), F32),
                   jax.ShapeDtypeStruct((8, n), F32),
                   jax.ShapeDtypeStruct((n, LANE), F32),
                   jax.ShapeDtypeStruct((ROUTER_ROWS, 1), F32)],
        scratch_shapes=[pltpu.VMEM((ROUTER_ROWS, 1), F32)],
        compiler_params=_params("arbitrary"),
        name="outproj_router",
    )(*a_list, w_out, x, norm_w.reshape(1, d), wr_hi, wr_lo, br, tri)


def _gather_tokens(table, idx):
    m = idx.shape[0]
    info = plsc.get_sparse_core_info()
    n_workers = info.num_cores * info.num_subcores
    per_worker = m // n_workers
    n_chunks = per_worker // SC_CHUNK
    assert per_worker * n_workers == m and n_chunks * SC_CHUNK == per_worker and n_chunks % 2 == 0
    mesh = plsc.VectorSubcoreMesh(core_axis_name="core", subcore_axis_name="subcore")
    rows = pltpu.VMEM((SC_CHUNK,) + table.shape[1:], table.dtype)

    @functools.partial(
        pl.kernel, mesh=mesh,
        out_type=jax.ShapeDtypeStruct((m,) + table.shape[1:], table.dtype),
        scratch_types=[pltpu.VMEM((n_chunks, SC_CHUNK), jnp.int32), rows, rows,
                       pltpu.SemaphoreType.DMA, pltpu.SemaphoreType.DMA],
        compiler_params=pltpu.CompilerParams(use_tc_tiling_on_sc=True),
        name="gather_tokens")
    def gather(table_hbm, idx_hbm, out_hbm, idx_v, rows_a, rows_b, sem_a, sem_b):
        worker = lax.axis_index("subcore") * info.num_cores + lax.axis_index("core")
        bufs = ((rows_a, sem_a), (rows_b, sem_b))

        def fetch(j, buf):
            return pltpu.make_async_copy(table_hbm.at[idx_v.at[j]], bufs[buf][0], bufs[buf][1])

        pltpu.sync_copy(idx_hbm.at[worker], idx_v)
        fetch(0, 0).start()

        @pl.loop(0, n_chunks, step=2)
        def _(j0):
            for buf in range(2):
                j = j0 + buf
                fetch(j, buf).wait()

                @pl.when(j + 1 < n_chunks)
                def _():
                    fetch(j + 1, 1 - buf).start()

                off = pl.multiple_of(worker * per_worker + j * SC_CHUNK, SC_CHUNK)
                pltpu.sync_copy(bufs[buf][0], out_hbm.at[pl.ds(off, SC_CHUNK)])

    return gather(table, idx.reshape(n_workers, n_chunks, SC_CHUNK))


def _scatter_tokens(arrays, idx, m):
    n = idx.shape[0]
    na = len(arrays)
    info = plsc.get_sparse_core_info()
    n_workers = info.num_cores * info.num_subcores
    per_worker = n // n_workers
    n_chunks = per_worker // SC_CHUNK
    assert per_worker * n_workers == n and n_chunks * SC_CHUNK == per_worker and n_chunks % 2 == 0
    mesh = plsc.VectorSubcoreMesh(core_axis_name="core", subcore_axis_name="subcore")
    bufs = [pltpu.VMEM((SC_CHUNK,) + a.shape[1:], a.dtype) for a in arrays for _ in range(2)]

    @functools.partial(
        pl.kernel, mesh=mesh,
        out_type=[jax.ShapeDtypeStruct((m,) + a.shape[1:], a.dtype) for a in arrays],
        scratch_types=[pltpu.VMEM((n_chunks, SC_CHUNK), jnp.int32)] + bufs + [pltpu.SemaphoreType.DMA] * (2 * na),
        compiler_params=pltpu.CompilerParams(use_tc_tiling_on_sc=True),
        name="scatter_tokens")
    def scatter(*refs):
        ins, idx_hbm, outs = refs[:na], refs[na], refs[na + 1:2 * na + 1]
        idx_v = refs[2 * na + 1]
        buf_v = refs[2 * na + 2:4 * na + 2]
        sem_v = refs[4 * na + 2:]
        worker = lax.axis_index("subcore") * info.num_cores + lax.axis_index("core")

        def load(a, j, b):
            off = pl.multiple_of(worker * per_worker + j * SC_CHUNK, SC_CHUNK)
            return pltpu.make_async_copy(ins[a].at[pl.ds(off, SC_CHUNK)], buf_v[2 * a + b], sem_v[2 * a + b])

        pltpu.sync_copy(idx_hbm.at[worker], idx_v)
        for a in range(na):
            load(a, 0, 0).start()

        @pl.loop(0, n_chunks, step=2)
        def _(j0):
            for b in range(2):
                j = j0 + b
                for a in range(na):
                    load(a, j, b).wait()

                @pl.when(j + 1 < n_chunks)
                def _():
                    for a in range(na):
                        load(a, j + 1, 1 - b).start()

                for a in range(na):
                    pltpu.sync_copy(buf_v[2 * a + b], outs[a].at[idx_v.at[j]])

    return scatter(*arrays, idx.reshape(n_workers, n_chunks, SC_CHUNK))


def _moe_kernel(step_ids, e_lo, e_hi, n_valid, n_tiles, x_ref, rec_ref, nw_ref, *refs):
    del step_ids, e_lo, e_hi
    s = pl.program_id(0)
    o_ref = refs[-1]

    @pl.when(s * MOE_STEP_TILES < n_tiles[0])
    def _():
        for i in range(MOE_STEP_TILES):
            g_lo_ref, g_hi_ref, u_lo_ref, u_hi_ref, d_lo_ref, d_hi_ref = refs[6 * i:6 * (i + 1)]
            rows = pl.ds(i * MOE_TILE, MOE_TILE)
            valid = lax.broadcasted_iota(jnp.int32, (MOE_TILE, 1), 0) < n_valid[s * MOE_STEP_TILES + i]
            x = jnp.where(valid, x_ref[rows, :], 0.0)
            h = _rms(x, nw_ref[...]).astype(BF16)
            rec = jnp.where(valid, rec_ref[rows, :], 0.0)

            def expert(g_ref, u_ref, d_ref):
                act = _silu(_dot(h, g_ref[0])) * _dot(h, u_ref[0])
                return _dot(act.astype(BF16), d_ref[0])

            y = (rec[:, 2:3] * expert(g_lo_ref, u_lo_ref, d_lo_ref)
                 + rec[:, 3:4] * expert(g_hi_ref, u_hi_ref, d_hi_ref))
            o_ref[rows, :] = jnp.where(valid, x_ref[rows, :], 0.0) + y


def _moe(x2, n, meta, rec, cnt, norm_w, w_gate, w_up, w_down, layer, w_down_bf16=None):
    d = D_MODEL
    tm = MOE_TILE
    max_tiles = n // tm + N_BUCKETS
    n_slots = max_tiles * tm
    counts = cnt[:N_BUCKETS, 0].astype(jnp.int32)
    tiles_per_bucket = (counts + tm - 1) // tm
    tile_end = jnp.cumsum(tiles_per_bucket)
    tile_start = tile_end - tiles_per_bucket
    n_tiles = tile_end[-1]
    in_bucket = meta[0][:, None] == jnp.arange(N_BUCKETS, dtype=F32)[None, :]
    dest = meta[1].astype(jnp.int32) + jnp.sum(jnp.where(in_bucket, tile_start[None, :] * tm, 0), axis=1)
    x_sorted, rec_sorted = _scatter_tokens([x2, rec], dest, n_slots)
    first = layer * N_EXPERTS
    w_gate, w_up = (_to_bf16(w, first=first, count=N_EXPERTS) for w in (w_gate, w_up))
    w_down = _to_bf16(w_down, first=first, count=N_EXPERTS) if w_down_bf16 is None else w_down_bf16

    k = MOE_STEP_TILES
    tile_raw = jnp.arange(max_tiles, dtype=jnp.int32)
    tile_ids = jnp.minimum(tile_raw, n_tiles - 1)
    step_ids = jnp.minimum(jnp.arange(max_tiles // k, dtype=jnp.int32), (n_tiles - 1) // k)
    of_bucket = lambda table: jnp.sum(jnp.where(tile_bucket[:, None] == jnp.arange(N_BUCKETS)[None, :], table[None, :], 0), axis=1)
    tile_bucket = jnp.sum((tile_ids[:, None] >= tile_end[None, :]).astype(jnp.int32), axis=1)
    n_valid = jnp.clip(of_bucket(counts) - (tile_ids - of_bucket(tile_start)) * tm, 0, tm)
    n_valid = jnp.where(tile_raw < n_tiles, n_valid, 0)
    group = tile_bucket // N_PAIRS
    pair = tile_bucket % N_PAIRS
    pair_lo = jnp.sum(jnp.where(pair[:, None] == jnp.arange(N_PAIRS)[None, :], jnp.asarray(PAIR_LO, jnp.int32)[None, :], 0), axis=1)
    pair_hi = jnp.sum(jnp.where(pair[:, None] == jnp.arange(N_PAIRS)[None, :], jnp.asarray(PAIR_HI, jnp.int32)[None, :], 0), axis=1)
    e_lo = group * EXPERTS_PER_GROUP + pair_lo
    e_hi = group * EXPERTS_PER_GROUP + pair_hi

    step_rows = lambda s, sid, lo, hi, nv, nt: (sid[s], 0)
    weight_specs, weight_args = [], []
    for i in range(k):
        wlo = lambda s, sid, lo, hi, nv, nt, i=i: (lo[s * k + i], 0, 0)
        whi = lambda s, sid, lo, hi, nv, nt, i=i: (hi[s * k + i], 0, 0)
        weight_specs += [pl.BlockSpec((1, d, D_EXPERT), wlo), pl.BlockSpec((1, d, D_EXPERT), whi),
                         pl.BlockSpec((1, d, D_EXPERT), wlo), pl.BlockSpec((1, d, D_EXPERT), whi),
                         pl.BlockSpec((1, D_EXPERT, d), wlo), pl.BlockSpec((1, D_EXPERT, d), whi)]
        weight_args += [w_gate, w_gate, w_up, w_up, w_down, w_down]
    grid_spec = pltpu.PrefetchScalarGridSpec(
        num_scalar_prefetch=5,
        grid=(max_tiles // k,),
        in_specs=[pl.BlockSpec((k * tm, d), step_rows), pl.BlockSpec((k * tm, LANE), step_rows),
                  pl.BlockSpec((1, d), lambda s, sid, lo, hi, nv, nt: (0, 0))] + weight_specs,
        out_specs=pl.BlockSpec((k * tm, d), step_rows))
    out_sorted = pl.pallas_call(
        _moe_kernel,
        grid_spec=grid_spec,
        out_shape=jax.ShapeDtypeStruct((n_slots, d), F32),
        compiler_params=_params("arbitrary"),
        name="moe",
    )(step_ids, e_lo, e_hi, n_valid, n_tiles.reshape(1), x_sorted, rec_sorted, norm_w.reshape(1, d), *weight_args)
    return _gather_tokens(out_sorted, dest)


def kernel(x, attn_norm, ffn_norm, even_w_in, even_w_out, hg_lb_logits, hg_o_norm, da_q_norm, da_k_norm, da_lambda, da_subln, odd_w_in, gla_w_alpha, gla_b_alpha, gla_o_norm, odd_w_out, moe_w_group, moe_b_group, moe_w_router, moe_b_router, moe_w_gate, moe_w_up, moe_w_down):
    batch, seq, d = x.shape
    depth = attn_norm.shape[0]
    n = batch * seq
    xf = x.reshape(n, d)

    lb_sm = jax.nn.softmax(hg_lb_logits.astype(F32), axis=0)
    lb_all = jnp.cumsum(lb_sm, axis=0) - lb_sm[0:1]
    rank = gla_w_alpha.shape[1]

    experts = lambda w: w.reshape((-1,) + w.shape[2:])
    moe_w_gate, moe_w_up, moe_w_down = experts(moe_w_gate), experts(moe_w_up), experts(moe_w_down)

    down_ahead = None
    ahead = ()
    for layer in range(depth):
        j = layer // 2
        if layer % 2 == 0:
            lam_init = 0.8 - 0.6 * math.exp(-0.3 * layer)
            proj = _inproj(xf, n, attn_norm[layer], _to_bf16(even_w_in, first=j, count=1)[0], after=ahead)
            o_a = _hgrn(proj, lb_all[j], hg_o_norm[j], batch, seq)
            o_b = _diff_attn(proj, da_q_norm[j], da_k_norm[j], da_lambda[j], da_subln[j], lam_init, batch, seq)
            mix = [o_a, o_b]
            w_out = _to_bf16(even_w_out, first=j, count=1)[0]
        else:
            w_alpha = jnp.pad(gla_w_alpha[j], ((0, GLA_RANK_PAD - rank), (0, 0))).astype(BF16)
            w_in = _to_bf16(jnp.swapaxes(odd_w_in, 1, 2), odd_w_in.shape[2] + GLA_RANK_PAD - rank, first=j, count=1)[0]
            proj = _inproj(xf, n, attn_norm[layer], w_in, w_transposed=True, after=ahead)
            mix = [_gla(proj, w_alpha, gla_b_alpha[j], gla_o_norm[j], batch, seq)]
            w_out = _to_bf16(odd_w_out, first=j, count=1)[0]
        x2, meta, rec, cnt = _outproj_router(mix, w_out, xf, n, ffn_norm[layer], moe_w_group[layer],
                                        moe_b_group[layer], moe_w_router[layer], moe_b_router[layer])
        xf = _moe(x2, n, meta, rec, cnt, ffn_norm[layer], moe_w_gate, moe_w_up, moe_w_down, layer, down_ahead)
        down_ahead = (_to_bf16(moe_w_down, first=(layer + 1) * N_EXPERTS, count=N_EXPERTS)
                      if layer + 1 < depth else None)
        ahead = () if down_ahead is None else (down_ahead,)
    return xf.reshape(batch, seq, d)
```

```python
import functools
import math

import numpy as np
import jax
import jax.numpy as jnp
from jax import lax
from jax.experimental import pallas as pl
from jax.experimental.pallas import tpu as pltpu
from jax.experimental.pallas import tpu_sc as plsc

F32 = jnp.float32
BF16 = jnp.bfloat16
EPS = 1e-6
LOG2_E = 1.4426950408889634

D_MODEL = 1024
LANE = 128
HEADS = 4
DK = 128
HG_WIDTH = 512
DA_HEAD_DIM = 64
GLA_DV = 256
GLA_RANK_PAD = 128
GLA_TAU = 16.0
N_GROUPS = 4
EXPERTS_PER_GROUP = 4
N_EXPERTS = 16
D_EXPERT = 512
N_PAIRS = 6
N_BUCKETS = N_GROUPS * N_PAIRS
ROUTER_ROWS = 32

CHUNK = 128
GLA_STEP_CHUNKS = 4
LEVELS = 7
ATTN_BLOCK = 512
ATTN_HEADS = 4
SCORE_BOUND_MARGIN = 1.05
MAX_SCORE_BOUND = 60.0
ROW_TILE = 1024
ROUTER_TILE = 1024
MOE_TILE = 256
MOE_STEP_TILES = 2
SC_CHUNK = 32
VMEM_LIMIT = 56 * 1024 * 1024
CAST_BLOCK_BYTES = 8 * 1024 * 1024

PAIR_LO = (0, 0, 1, 1, 0, 2)
PAIR_HI = (1, 2, 2, 3, 3, 3)


def _dot(a, b):
    return jnp.dot(a, b, preferred_element_type=F32)


def _dot_nt(a, b):
    return lax.dot_general(a, b, (((1,), (1,)), ((), ())), preferred_element_type=F32)


def _dot_tn(a, b):
    return lax.dot_general(a, b, (((0,), (0,)), ((), ())), preferred_element_type=F32)


def _rms(x, w):
    return x * lax.rsqrt(jnp.mean(x * x, axis=-1, keepdims=True) + EPS) * w


def _silu(x):
    return x * jax.nn.sigmoid(x)


def _params(*sem):
    return pltpu.CompilerParams(dimension_semantics=sem, vmem_limit_bytes=VMEM_LIMIT)


def _cast_kernel(w_ref, o_ref):
    k = w_ref.shape[1]
    o_ref[:, :k, :] = w_ref[...].astype(BF16)
    if o_ref.shape[1] > k:
        o_ref[:, k:, :] = jnp.zeros((o_ref.shape[0], o_ref.shape[1] - k, o_ref.shape[2]), BF16)


def _to_bf16(w, k_pad=None, first=0, count=None):
    _, k, n = w.shape
    l = count or w.shape[0]
    k_pad = k_pad or k
    per_matrix = k * n * 4
    lb = max([c for c in range(1, l + 1) if l % c == 0 and first % c == 0 and c * per_matrix <= CAST_BLOCK_BYTES] or [1])
    nb = n
    while lb == 1 and k * nb * 4 > CAST_BLOCK_BYTES and nb % (2 * LANE) == 0:
        nb //= 2
    return pl.pallas_call(
        _cast_kernel,
        grid=(l // lb, n // nb),
        in_specs=[pl.BlockSpec((lb, k, nb), lambda i, j: (first // lb + i, 0, j))],
        out_specs=pl.BlockSpec((lb, k_pad, nb), lambda i, j: (i, 0, j)),
        out_shape=jax.ShapeDtypeStruct((l, k_pad, n), BF16),
        compiler_params=_params("parallel", "parallel"),
        name="to_bf16",
    )(w)


def _inproj_kernel(x_ref, nw_ref, w_ref, *refs, w_transposed):
    o_ref = refs[-1]
    h = _rms(x_ref[...], nw_ref[...]).astype(BF16)
    o_ref[...] = _dot_nt(h, w_ref[...]) if w_transposed else _dot(h, w_ref[...])


def _inproj(x, n, norm_w, w, w_transposed=False, after=()):
    d = D_MODEL
    n_out = w.shape[0] if w_transposed else w.shape[1]
    return pl.pallas_call(
        functools.partial(_inproj_kernel, w_transposed=w_transposed),
        grid=(n // ROW_TILE,),
        in_specs=[pl.BlockSpec((ROW_TILE, d), lambda i: (i, 0)),
                  pl.BlockSpec((1, d), lambda i: (0, 0)),
                  pl.BlockSpec(w.shape, lambda i: (0, 0))] + [pl.BlockSpec(memory_space=pl.ANY) for _ in after],
        out_specs=pl.BlockSpec((ROW_TILE, n_out), lambda i: (i, 0)),
        out_shape=jax.ShapeDtypeStruct((n, n_out), F32),
        compiler_params=_params("parallel"),
        name="inproj",
    )(x, norm_w.reshape(1, d), w, *after)


def _decay_tables():
    c = CHUNK
    a = np.zeros((LEVELS + 2, c, c), np.float32)
    t = np.arange(c)
    for l in range(LEVELS):
        s = 1 << l
        for r in range(c):
            mid = (r // (2 * s)) * 2 * s + s - 1
            if (r >> l) & 1:
                a[l, r] = (t > mid) & (t <= r)
            else:
                a[l, r] = (t > r) & (t <= mid)
    a[LEVELS] = t[None, :] <= t[:, None]
    a[LEVELS + 1] = t[None, :] > t[:, None]
    a = a.reshape((LEVELS + 2) * c, c)
    a3 = np.concatenate([a, a], axis=1)
    i, j = np.meshgrid(t, t, indexing="ij")
    x = np.maximum(i ^ j, 1)
    lvl = np.where(j < i, np.floor(np.log2(x)).astype(np.int32), np.where(i == j, LEVELS, -1))
    return jnp.asarray(a3, BF16), jnp.asarray(lvl, jnp.int32)


def _decay_factors(a3, g):
    g_hi = g.astype(BF16)
    g_lo = (g - g_hi.astype(F32)).astype(BF16)
    return jnp.exp(_dot(a3, jnp.concatenate([g_hi, g_lo], axis=0)))


def _gla_chunk(q, k, v, f, st_ref, lvl, rowid):
    c = CHUNK
    sc = jnp.where(lvl == LEVELS, jnp.sum(q * k, axis=-1, keepdims=True), 0.0)
    for l in range(LEVELS):
        s = 1 << l
        if s % 8 == 0:
            qk = jnp.concatenate([(q if b & 1 else k)[b * s:(b + 1) * s] for b in range(c // s)], axis=0)
        else:
            qk = jnp.where(((rowid >> l) & 1) == 1, q, k)
        m = (qk * f[l * c:(l + 1) * c]).astype(BF16)
        sc = jnp.where(lvl == l, _dot_nt(m, m), sc)
    qd = (q * f[LEVELS * c:(LEVELS + 1) * c]).astype(BF16)
    kd = (k * f[(LEVELS + 1) * c:(LEVELS + 2) * c]).astype(BF16)
    vb = v.astype(BF16)
    st = st_ref[...]
    o = _dot(sc.astype(BF16), vb) + _dot_nt(qd, st.astype(BF16))
    chunk_decay = f[(LEVELS + 1) * c - 1:(LEVELS + 1) * c, :]
    st_ref[...] = st * chunk_decay + _dot_tn(vb, kd)
    return o


def _hgrn_kernel(q_ref, f_ref, i_ref, g_ref, lb_ref, on_ref, a3_ref, lvl_ref, o_ref, st_ref):
    @pl.when(pl.program_id(1) == 0)
    def _():
        st_ref[...] = jnp.zeros_like(st_ref)

    lvl = lvl_ref[...]
    rowid = lax.broadcasted_iota(jnp.int32, (CHUNK, 1), 0)
    lb = lb_ref[...]
    for c in range(GLA_STEP_CHUNKS):
        rows = slice(c * CHUNK, (c + 1) * CHUNK)
        fl = f_ref[rows, :]
        f = _decay_factors(a3_ref[...], jnp.log(lb + (1.0 - lb) * jax.nn.sigmoid(fl)))
        key_in = (1.0 - lb) * jax.nn.sigmoid(-fl)
        for h in range(HEADS):
            sl = slice(h * DK, (h + 1) * DK)
            q = _silu(q_ref[rows, sl]) * (DK ** -0.5)
            o = _gla_chunk(q, key_in[:, sl], i_ref[rows, sl], f[:, sl], st_ref.at[h], lvl, rowid)
            o = _rms(o, on_ref[...])
            o_ref[rows, sl] = (o * _silu(g_ref[rows, sl])).astype(BF16)


def _hgrn(proj, lb, o_norm, batch, seq):
    n = proj.shape[0]
    rows = CHUNK * GLA_STEP_CHUNKS
    nt = seq // rows
    a3, lvl = _decay_tables()
    col = lambda j: pl.BlockSpec((rows, HG_WIDTH), lambda b, t: (b * nt + t, j))
    const = lambda shape: pl.BlockSpec(shape, lambda b, t: (0, 0))
    return pl.pallas_call(
        _hgrn_kernel,
        grid=(batch, nt),
        in_specs=[col(0), col(1), col(2), col(3), const((1, HG_WIDTH)), const((1, DK)),
                  const(a3.shape), const(lvl.shape)],
        out_specs=pl.BlockSpec((rows, HG_WIDTH), lambda b, t: (b * nt + t, 0)),
        out_shape=jax.ShapeDtypeStruct((n, HG_WIDTH), BF16),
        scratch_shapes=[pltpu.VMEM((HEADS, DK, DK), F32)],
        compiler_params=_params("parallel", "arbitrary"),
        name="hgrn",
    )(proj, proj, proj, proj, lb.reshape(1, HG_WIDTH), o_norm.reshape(1, DK), a3, lvl)


def _log_sigmoid(x):
    return jnp.minimum(x, 0.0) - jnp.log1p(jnp.exp(-jnp.abs(x)))


def _gla_kernel(q_ref, k_ref, v_ref, g_ref, a_ref, wa_ref, ba_ref, on_ref, a3_ref, lvl_ref, o_ref, st_ref):
    @pl.when(pl.program_id(1) == 0)
    def _():
        st_ref[...] = jnp.zeros_like(st_ref)

    lvl = lvl_ref[...]
    rowid = lax.broadcasted_iota(jnp.int32, (CHUNK, 1), 0)
    for c in range(GLA_STEP_CHUNKS):
        rows = slice(c * CHUNK, (c + 1) * CHUNK)
        gate_logit = _dot(a_ref[rows, :].astype(BF16), wa_ref[...]) + ba_ref[...]
        f = _decay_factors(a3_ref[...], _log_sigmoid(gate_logit) * (1.0 / GLA_TAU))
        for h in range(HEADS):
            sl = slice(h * DK, (h + 1) * DK)
            sv = slice(h * GLA_DV, (h + 1) * GLA_DV)
            q = q_ref[rows, sl] * (DK ** -0.5)
            o = _gla_chunk(q, k_ref[rows, sl], v_ref[rows, sv], f[:, sl], st_ref.at[h], lvl, rowid)
            o = _rms(o, on_ref[...])
            o_ref[rows, sv] = (o * _silu(g_ref[rows, sv])).astype(BF16)


def _gla(proj, w_alpha, b_alpha, o_norm, batch, seq):
    n = proj.shape[0]
    rows = CHUNK * GLA_STEP_CHUNKS
    nt = seq // rows
    qk_w = HEADS * DK
    v_w = HEADS * GLA_DV
    a3, lvl = _decay_tables()
    row = lambda b, t: b * nt + t
    const = lambda shape: pl.BlockSpec(shape, lambda b, t: (0, 0))
    return pl.pallas_call(
        _gla_kernel,
        grid=(batch, nt),
        in_specs=[pl.BlockSpec((rows, qk_w), lambda b, t: (row(b, t), 0)),
                  pl.BlockSpec((rows, qk_w), lambda b, t: (row(b, t), 1)),
                  pl.BlockSpec((rows, v_w), lambda b, t: (row(b, t), 1)),
                  pl.BlockSpec((rows, v_w), lambda b, t: (row(b, t), 2)),
                  pl.BlockSpec((rows, GLA_RANK_PAD), lambda b, t: (row(b, t), (2 * qk_w + 2 * v_w) // GLA_RANK_PAD)),
                  const((GLA_RANK_PAD, qk_w)), const((1, qk_w)), const((1, GLA_DV)),
                  const(a3.shape), const(lvl.shape)],
        out_specs=pl.BlockSpec((rows, v_w), lambda b, t: (row(b, t), 0)),
        out_shape=jax.ShapeDtypeStruct((n, v_w), BF16),
        scratch_shapes=[pltpu.VMEM((HEADS, GLA_DV, DK), F32)],
        compiler_params=_params("parallel", "arbitrary"),
        name="gla",
    )(proj, proj, proj, proj, proj, w_alpha, b_alpha.reshape(1, qk_w), o_norm.reshape(1, GLA_DV), a3, lvl)


def _qk_norm(x, w2):
    lane = lax.broadcasted_iota(jnp.int32, x.shape, 1)
    first = lane < DA_HEAD_DIM
    sq = x * x
    ms1 = jnp.sum(jnp.where(first, sq, 0.0), axis=-1, keepdims=True) * (1.0 / DA_HEAD_DIM)
    ms2 = jnp.sum(jnp.where(first, 0.0, sq), axis=-1, keepdims=True) * (1.0 / DA_HEAD_DIM)
    r = jnp.where(first, lax.rsqrt(ms1 + EPS), lax.rsqrt(ms2 + EPS))
    return x * r * w2


def _score_bound(q_norm_w, k_norm_w):
    return (jnp.max(jnp.abs(q_norm_w)) * jnp.max(jnp.abs(k_norm_w))
            * (DA_HEAD_DIM * DA_HEAD_DIM ** -0.5 * LOG2_E * SCORE_BOUND_MARGIN))


def _attn_kernel(qi_tab, ki_tab, q_ref, k_ref, v_ref, qn_ref, kn_ref, lam_ref, sub_ref, o_ref,
                 qs_ref, m_ref, l_ref, acc_ref, *, lam_init, bounded):
    p = pl.program_id(2)
    qi = qi_tab[p]
    ki = ki_tab[p]
    tq = ATTN_BLOCK
    dv = 2 * DA_HEAD_DIM
    heads = [slice(h * dv, (h + 1) * dv) for h in range(ATTN_HEADS)]

    @pl.when(ki == 0)
    def _():
        for h, sl in enumerate(heads):
            qn = _qk_norm(q_ref[:, sl], qn_ref[...])
            qt = (qn * (DA_HEAD_DIM ** -0.5 * LOG2_E)).T
            top = lax.broadcasted_iota(jnp.int32, qt.shape, 0) < DA_HEAD_DIM
            q1 = jnp.where(top, qt, 0.0).astype(BF16)
            q2 = jnp.where(top, 0.0, qt).astype(BF16)
            for copy in range(1 if bounded else 2):
                qs_ref[h, copy, :, :tq] = q1
                qs_ref[h, copy, :, tq:] = q2
            if bounded:
                bound = _score_bound(qn_ref[...], kn_ref[...])
                row = lax.broadcasted_iota(jnp.int32, (dv, 2 * tq), 0)
                qs_ref[h, 1] = jnp.where(row == 0, -bound, 0.0).astype(BF16)
        if not bounded:
            m_ref[...] = jnp.full_like(m_ref, -jnp.inf)
        l_ref[...] = jnp.zeros_like(l_ref)
        acc_ref[...] = jnp.zeros_like(acc_ref)

    def step(masked):
        tk = k_ref.shape[0]
        ones = jnp.ones((16, tk), BF16)
        unit = jnp.where(lax.broadcasted_iota(jnp.int32, (tk, dv), 1) == 0, 1.0, 0.0).astype(BF16)
        for h, sl in enumerate(heads):
            kn = _qk_norm(k_ref[:, sl], kn_ref[...]).astype(BF16)

            def mask(s):
                if not masked:
                    return s
                key = lax.broadcasted_iota(jnp.int32, s.shape, 0)
                qry = lax.broadcasted_iota(jnp.int32, s.shape, 1)
                qry = jnp.where(qry >= tq, qry - tq, qry)
                return jnp.where(key <= qry, s, -jnp.inf)

            if bounded:
                shifted = mask(_dot(jnp.concatenate([kn, unit], axis=1), qs_ref[h].reshape(2 * dv, 2 * tq)))
                alpha = 1.0
            else:
                m_prev = m_ref[h]
                m_new = jnp.maximum(m_prev, jnp.max(mask(_dot(kn, qs_ref[h, 0])), axis=0, keepdims=True))
                alpha = jnp.exp2(m_prev - m_new)
                shifted = mask(_dot(kn, qs_ref[h, 1])) - m_new
                m_ref[h] = m_new
            pr = jnp.exp2(shifted).astype(BF16) if bounded else jnp.exp2(shifted.astype(BF16))
            vt = jnp.concatenate([v_ref[:, sl].T.astype(BF16), ones], axis=0)
            pv = _dot(vt, pr)
            l_ref[h] = alpha * l_ref[h] + pv[dv:dv + 1]
            acc_ref[h] = alpha * acc_ref[h] + pv[:dv]

    @pl.when(ki < qi)
    def _():
        step(False)

    @pl.when(ki == qi)
    def _():
        step(True)
        lv = lam_ref[...]
        lam = (jnp.exp(jnp.sum(lv[0:1] * lv[1:2], axis=-1, keepdims=True))
               - jnp.exp(jnp.sum(lv[2:3] * lv[3:4], axis=-1, keepdims=True)) + lam_init)
        for h, sl in enumerate(heads):
            o1 = acc_ref[h, :, :tq] / l_ref[h, :, :tq]
            o2 = acc_ref[h, :, tq:] / l_ref[h, :, tq:]
            o = o1 - lam * o2
            o = o * lax.rsqrt(jnp.mean(o * o, axis=0, keepdims=True) + EPS) * sub_ref[...] * (1.0 - lam_init)
            o_ref[:, sl] = o.T.astype(BF16)


def _diff_attn(proj, q_norm, k_norm, lam_vecs, subln, lam_init, batch, seq):
    n = proj.shape[0]
    blk = ATTN_BLOCK
    nq = seq // blk
    dv = 2 * DA_HEAD_DIM
    pairs = [(qi, ki) for qi in range(nq) for ki in range(qi + 1)]
    qi_tab = jnp.asarray([p[0] for p in pairs], jnp.int32)
    ki_tab = jnp.asarray([p[1] for p in pairs], jnp.int32)
    width = ATTN_HEADS * dv
    groups = HEADS // ATTN_HEADS
    q_col = (4 * HG_WIDTH) // width
    const = lambda shape: pl.BlockSpec(shape, lambda b, g, p, qt, kt: (0, 0))
    grid_spec = pltpu.PrefetchScalarGridSpec(
        num_scalar_prefetch=2,
        grid=(batch, groups, len(pairs)),
        in_specs=[pl.BlockSpec((blk, width), lambda b, g, p, qt, kt: (b * nq + qt[p], q_col + g)),
                  pl.BlockSpec((blk, width), lambda b, g, p, qt, kt: (b * nq + kt[p], q_col + groups + g)),
                  pl.BlockSpec((blk, width), lambda b, g, p, qt, kt: (b * nq + kt[p], q_col + 2 * groups + g)),
                  const((1, dv)), const((1, dv)), const((4, DA_HEAD_DIM)), const((dv, 1))],
        out_specs=pl.BlockSpec((blk, width), lambda b, g, p, qt, kt: (b * nq + qt[p], g)),
        scratch_shapes=[pltpu.VMEM((ATTN_HEADS, 2, dv, 2 * blk), BF16),
                        pltpu.VMEM((ATTN_HEADS, 1, 2 * blk), F32),
                        pltpu.VMEM((ATTN_HEADS, 1, 2 * blk), F32),
                        pltpu.VMEM((ATTN_HEADS, dv, 2 * blk), F32)])
    args = (qi_tab, ki_tab, proj, proj, proj, jnp.tile(q_norm, 2).reshape(1, -1), jnp.tile(k_norm, 2).reshape(1, -1),
            lam_vecs, subln.reshape(-1, 1))

    def run(bounded):
        return pl.pallas_call(
            functools.partial(_attn_kernel, lam_init=lam_init, bounded=bounded),
            grid_spec=grid_spec,
            out_shape=jax.ShapeDtypeStruct((n, HEADS * dv), BF16),
            compiler_params=_params("parallel", "parallel", "arbitrary"),
            name="diff_attn_bounded" if bounded else "diff_attn",
        )(*args)

    return lax.cond(_score_bound(q_norm, k_norm) <= MAX_SCORE_BOUND, lambda: run(True), lambda: run(False))


def _outproj_router_kernel(*refs, n_a):
    a_refs = refs[:n_a]
    (w_ref, x_ref, nw_ref, wr_hi_ref, wr_lo_ref, br_ref, tri_ref,
     x2_ref, meta_ref, rec_ref, cnt_ref, carry_ref) = refs[n_a:]

    @pl.when(pl.program_id(0) == 0)
    def _():
        carry_ref[...] = jnp.zeros_like(carry_ref)

    x2 = x_ref[...]
    off = 0
    for a_ref in a_refs:
        kk = a_ref.shape[1]
        x2 = x2 + _dot(a_ref[...], w_ref[off:off + kk, :])
        off += kk
    x2_ref[...] = x2

    h = _rms(x2, nw_ref[...])
    h_hi = h.astype(BF16)
    h_lo = (h - h_hi.astype(F32)).astype(BF16)
    wr_hi = wr_hi_ref[...]
    by_hi = _dot_nt(jnp.concatenate([wr_hi, wr_lo_ref[...]], axis=0), h_hi)
    lt = by_hi[:ROUTER_ROWS] + by_hi[ROUTER_ROWS:] + _dot_nt(wr_hi, h_lo) + br_ref[...]
    row = lambda r: lt[r:r + 1, :]

    gl = [row(g) for g in range(N_GROUPS)]
    gmax = functools.reduce(jnp.maximum, gl)
    gsel = jnp.where(gl[0] == gmax, 0.0, jnp.where(gl[1] == gmax, 1.0, jnp.where(gl[2] == gmax, 2.0, 3.0)))
    g_w = 1.0 / functools.reduce(jnp.add, [jnp.exp(x - gmax) for x in gl])

    el = []
    for e in range(EXPERTS_PER_GROUP):
        v = row(N_GROUPS + e)
        for g in range(1, N_GROUPS):
            v = jnp.where(gsel == float(g), row(N_GROUPS + EXPERTS_PER_GROUP * g + e), v)
        el.append(v)
    emax = functools.reduce(jnp.maximum, el)
    sel = []
    for e in range(EXPERTS_PER_GROUP):
        ahead = jnp.zeros_like(emax)
        for o in range(EXPERTS_PER_GROUP):
            if o < e:
                ahead = ahead + jnp.where(el[o] >= el[e], 1.0, 0.0)
            elif o > e:
                ahead = ahead + jnp.where(el[o] > el[e], 1.0, 0.0)
        sel.append(ahead < 2.0)
    pe = [jnp.where(sel[e], jnp.exp(el[e] - emax), 0.0) for e in range(EXPERTS_PER_GROUP)]
    scale = g_w / functools.reduce(jnp.add, pe)
    zero = jnp.zeros_like(emax)
    seen = zero
    lo_idx, hi_idx, c_lo, c_hi = zero, zero, zero, zero
    for e in range(EXPERTS_PER_GROUP):
        is_lo = sel[e] & (seen == 0.0)
        is_hi = sel[e] & (seen == 1.0)
        lo_idx = jnp.where(is_lo, float(e), lo_idx)
        hi_idx = jnp.where(is_hi, float(e), hi_idx)
        c_lo = jnp.where(is_lo, pe[e] * scale, c_lo)
        c_hi = jnp.where(is_hi, pe[e] * scale, c_hi)
        seen = seen + jnp.where(sel[e], 1.0, 0.0)
    pair = zero
    for p in range(1, N_PAIRS):
        pair = jnp.where((lo_idx == float(PAIR_LO[p])) & (hi_idx == float(PAIR_HI[p])), float(p), pair)
    bucket = gsel * float(N_PAIRS) + pair

    bid = lax.broadcasted_iota(jnp.int32, (ROUTER_ROWS, bucket.shape[1]), 0).astype(F32)
    onehot = jnp.where(bid == bucket, 1.0, 0.0)
    before = _dot(onehot.astype(BF16), tri_ref[...]) + carry_ref[...]
    rank = jnp.sum(onehot * before, axis=0, keepdims=True)
    carry = carry_ref[...] + jnp.sum(onehot, axis=1, keepdims=True)
    carry_ref[...] = carry
    cnt_ref[...] = carry
    meta = jnp.concatenate([bucket, rank, c_lo, c_hi, zero, zero, zero, zero], axis=0)
    meta_ref[...] = meta
    rec_ref[...] = jnp.concatenate([meta, jnp.zeros((LANE - meta.shape[0], meta.shape[1]), F32)], axis=0).T


def _outproj_router(a_list, w_out, x, n, norm_w, w_group, b_group, w_router, b_router):
    d = D_MODEL
    tm = ROUTER_TILE
    tok = lambda: pl.BlockSpec((tm, d), lambda i: (i, 0))
    wr = jnp.concatenate([w_group, w_router.transpose(1, 0, 2).reshape(d, N_EXPERTS)], axis=1)
    wr_t = jnp.zeros((ROUTER_ROWS, d), F32).at[:wr.shape[1]].set(wr.T)
    wr_hi = wr_t.astype(BF16)
    wr_lo = (wr_t - wr_hi.astype(F32)).astype(BF16)
    br = jnp.zeros((ROUTER_ROWS, 1), F32).at[:wr.shape[1], 0].set(
        jnp.concatenate([b_group, b_router.reshape(-1)]))
    tri = jnp.asarray(np.triu(np.ones((tm, tm), np.float32), 1), BF16)
    const = lambda shape: pl.BlockSpec(shape, lambda i: (0, 0))
    in_specs = [pl.BlockSpec((tm, a.shape[1]), lambda i: (i, 0)) for a in a_list]
    in_specs += [const((d, d)), tok(), const((1, d)),
                 const((ROUTER_ROWS, d)), const((ROUTER_ROWS, d)), const((ROUTER_ROWS, 1)), const((tm, tm))]
    return pl.pallas_call(
        functools.partial(_outproj_router_kernel, n_a=len(a_list)),
        grid=(n // tm,),
        in_specs=in_specs,
        out_specs=[tok(),
                   pl.BlockSpec((8, tm), lambda i: (0, i)),
                   pl.BlockSpec((tm, LANE), lambda i: (i, 0)),
                   const((ROUTER_ROWS, 1))],
        out_shape=[jax.ShapeDtypeStruct((n, d), F32),
                   jax.ShapeDtypeStruct((8, n), F32),
                   jax.ShapeDtypeStruct((n, LANE), F32),
                   jax.ShapeDtypeStruct((ROUTER_ROWS, 1), F32)],
        scratch_shapes=[pltpu.VMEM((ROUTER_ROWS, 1), F32)],
        compiler_params=_params("arbitrary"),
        name="outproj_router",
    )(*a_list, w_out, x, norm_w.reshape(1, d), wr_hi, wr_lo, br, tri)


def _gather_tokens(table, idx):
    m = idx.shape[0]
    info = plsc.get_sparse_core_info()
    n_workers = info.num_cores * info.num_subcores
    per_worker = m // n_workers
    n_chunks = per_worker // SC_CHUNK
    assert per_worker * n_workers == m and n_chunks * SC_CHUNK == per_worker and n_chunks % 2 == 0
    mesh = plsc.VectorSubcoreMesh(core_axis_name="core", subcore_axis_name="subcore")
    rows = pltpu.VMEM((SC_CHUNK,) + table.shape[1:], table.dtype)

    @functools.partial(
        pl.kernel, mesh=mesh,
        out_type=jax.ShapeDtypeStruct((m,) + table.shape[1:], table.dtype),
        scratch_types=[pltpu.VMEM((n_chunks, SC_CHUNK), jnp.int32), rows, rows,
                       pltpu.SemaphoreType.DMA, pltpu.SemaphoreType.DMA],
        compiler_params=pltpu.CompilerParams(use_tc_tiling_on_sc=True),
        name="gather_tokens")
    def gather(table_hbm, idx_hbm, out_hbm, idx_v, rows_a, rows_b, sem_a, sem_b):
        worker = lax.axis_index("subcore") * info.num_cores + lax.axis_index("core")
        bufs = ((rows_a, sem_a), (rows_b, sem_b))

        def fetch(j, buf):
            return pltpu.make_async_copy(table_hbm.at[idx_v.at[j]], bufs[buf][0], bufs[buf][1])

        pltpu.sync_copy(idx_hbm.at[worker], idx_v)
        fetch(0, 0).start()

        @pl.loop(0, n_chunks, step=2)
        def _(j0):
            for buf in range(2):
                j = j0 + buf
                fetch(j, buf).wait()

                @pl.when(j + 1 < n_chunks)
                def _():
                    fetch(j + 1, 1 - buf).start()

                off = pl.multiple_of(worker * per_worker + j * SC_CHUNK, SC_CHUNK)
                pltpu.sync_copy(bufs[buf][0], out_hbm.at[pl.ds(off, SC_CHUNK)])

    return gather(table, idx.reshape(n_workers, n_chunks, SC_CHUNK))


def _scatter_tokens(arrays, idx, m):
    n = idx.shape[0]
    na = len(arrays)
    info = plsc.get_sparse_core_info()
    n_workers = info.num_cores * info.num_subcores
    per_worker = n // n_workers
    n_chunks = per_worker // SC_CHUNK
    assert per_worker * n_workers == n and n_chunks * SC_CHUNK == per_worker and n_chunks % 2 == 0
    mesh = plsc.VectorSubcoreMesh(core_axis_name="core", subcore_axis_name="subcore")
    bufs = [pltpu.VMEM((SC_CHUNK,) + a.shape[1:], a.dtype) for a in arrays for _ in range(2)]

    @functools.partial(
        pl.kernel, mesh=mesh,
        out_type=[jax.ShapeDtypeStruct((m,) + a.shape[1:], a.dtype) for a in arrays],
        scratch_types=[pltpu.VMEM((n_chunks, SC_CHUNK), jnp.int32)] + bufs + [pltpu.SemaphoreType.DMA] * (2 * na),
        compiler_params=pltpu.CompilerParams(use_tc_tiling_on_sc=True),
        name="scatter_tokens")
    def scatter(*refs):
        ins, idx_hbm, outs = refs[:na], refs[na], refs[na + 1:2 * na + 1]
        idx_v = refs[2 * na + 1]
        buf_v = refs[2 * na + 2:4 * na + 2]
        sem_v = refs[4 * na + 2:]
        worker = lax.axis_index("subcore") * info.num_cores + lax.axis_index("core")

        def load(a, j, b):
            off = pl.multiple_of(worker * per_worker + j * SC_CHUNK, SC_CHUNK)
            return pltpu.make_async_copy(ins[a].at[pl.ds(off, SC_CHUNK)], buf_v[2 * a + b], sem_v[2 * a + b])

        pltpu.sync_copy(idx_hbm.at[worker], idx_v)
        for a in range(na):
            load(a, 0, 0).start()

        @pl.loop(0, n_chunks, step=2)
        def _(j0):
            for b in range(2):
                j = j0 + b
                for a in range(na):
                    load(a, j, b).wait()

                @pl.when(j + 1 < n_chunks)
                def _():
                    for a in range(na):
                        load(a, j + 1, 1 - b).start()

                for a in range(na):
                    pltpu.sync_copy(buf_v[2 * a + b], outs[a].at[idx_v.at[j]])

    return scatter(*arrays, idx.reshape(n_workers, n_chunks, SC_CHUNK))


def _moe_kernel(step_ids, e_lo, e_hi, n_valid, n_tiles, x_ref, rec_ref, nw_ref, *refs):
    del step_ids, e_lo, e_hi
    s = pl.program_id(0)
    o_ref = refs[-1]

    @pl.when(s * MOE_STEP_TILES < n_tiles[0])
    def _():
        for i in range(MOE_STEP_TILES):
            g_lo_ref, g_hi_ref, u_lo_ref, u_hi_ref, d_lo_ref, d_hi_ref = refs[6 * i:6 * (i + 1)]
            rows = pl.ds(i * MOE_TILE, MOE_TILE)
            valid = lax.broadcasted_iota(jnp.int32, (MOE_TILE, 1), 0) < n_valid[s * MOE_STEP_TILES + i]
            x = jnp.where(valid, x_ref[rows, :], 0.0)
            h = _rms(x, nw_ref[...]).astype(BF16)
            rec = jnp.where(valid, rec_ref[rows, :], 0.0)

            def expert(g_ref, u_ref, d_ref):
                act = _silu(_dot(h, g_ref[0])) * _dot(h, u_ref[0])
                return _dot(act.astype(BF16), d_ref[0])

            y = (rec[:, 2:3] * expert(g_lo_ref, u_lo_ref, d_lo_ref)
                 + rec[:, 3:4] * expert(g_hi_ref, u_hi_ref, d_hi_ref))
            o_ref[rows, :] = jnp.where(valid, x_ref[rows, :], 0.0) + y


def _moe(x2, n, meta, rec, cnt, norm_w, w_gate, w_up, w_down, layer, w_down_bf16=None):
    d = D_MODEL
    tm = MOE_TILE
    max_tiles = n // tm + N_BUCKETS
    n_slots = max_tiles * tm
    counts = cnt[:N_BUCKETS, 0].astype(jnp.int32)
    tiles_per_bucket = (counts + tm - 1) // tm
    tile_end = jnp.cumsum(tiles_per_bucket)
    tile_start = tile_end - tiles_per_bucket
    n_tiles = tile_end[-1]
    in_bucket = meta[0][:, None] == jnp.arange(N_BUCKETS, dtype=F32)[None, :]
    dest = meta[1].astype(jnp.int32) + jnp.sum(jnp.where(in_bucket, tile_start[None, :] * tm, 0), axis=1)
    x_sorted, rec_sorted = _scatter_tokens([x2, rec], dest, n_slots)
    first = layer * N_EXPERTS
    w_gate, w_up = (_to_bf16(w, first=first, count=N_EXPERTS) for w in (w_gate, w_up))
    w_down = _to_bf16(w_down, first=first, count=N_EXPERTS) if w_down_bf16 is None else w_down_bf16

    k = MOE_STEP_TILES
    tile_raw = jnp.arange(max_tiles, dtype=jnp.int32)
    tile_ids = jnp.minimum(tile_raw, n_tiles - 1)
    step_ids = jnp.minimum(jnp.arange(max_tiles // k, dtype=jnp.int32), (n_tiles - 1) // k)
    of_bucket = lambda table: jnp.sum(jnp.where(tile_bucket[:, None] == jnp.arange(N_BUCKETS)[None, :], table[None, :], 0), axis=1)
    tile_bucket = jnp.sum((tile_ids[:, None] >= tile_end[None, :]).astype(jnp.int32), axis=1)
    n_valid = jnp.clip(of_bucket(counts) - (tile_ids - of_bucket(tile_start)) * tm, 0, tm)
    n_valid = jnp.where(tile_raw < n_tiles, n_valid, 0)
    group = tile_bucket // N_PAIRS
    pair = tile_bucket % N_PAIRS
    pair_lo = jnp.sum(jnp.where(pair[:, None] == jnp.arange(N_PAIRS)[None, :], jnp.asarray(PAIR_LO, jnp.int32)[None, :], 0), axis=1)
    pair_hi = jnp.sum(jnp.where(pair[:, None] == jnp.arange(N_PAIRS)[None, :], jnp.asarray(PAIR_HI, jnp.int32)[None, :], 0), axis=1)
    e_lo = group * EXPERTS_PER_GROUP + pair_lo
    e_hi = group * EXPERTS_PER_GROUP + pair_hi

    step_rows = lambda s, sid, lo, hi, nv, nt: (sid[s], 0)
    weight_specs, weight_args = [], []
    for i in range(k):
        wlo = lambda s, sid, lo, hi, nv, nt, i=i: (lo[s * k + i], 0, 0)
        whi = lambda s, sid, lo, hi, nv, nt, i=i: (hi[s * k + i], 0, 0)
        weight_specs += [pl.BlockSpec((1, d, D_EXPERT), wlo), pl.BlockSpec((1, d, D_EXPERT), whi),
                         pl.BlockSpec((1, d, D_EXPERT), wlo), pl.BlockSpec((1, d, D_EXPERT), whi),
                         pl.BlockSpec((1, D_EXPERT, d), wlo), pl.BlockSpec((1, D_EXPERT, d), whi)]
        weight_args += [w_gate, w_gate, w_up, w_up, w_down, w_down]
    grid_spec = pltpu.PrefetchScalarGridSpec(
        num_scalar_prefetch=5,
        grid=(max_tiles // k,),
        in_specs=[pl.BlockSpec((k * tm, d), step_rows), pl.BlockSpec((k * tm, LANE), step_rows),
                  pl.BlockSpec((1, d), lambda s, sid, lo, hi, nv, nt: (0, 0))] + weight_specs,
        out_specs=pl.BlockSpec((k * tm, d), step_rows))
    out_sorted = pl.pallas_call(
        _moe_kernel,
        grid_spec=grid_spec,
        out_shape=jax.ShapeDtypeStruct((n_slots, d), F32),
        compiler_params=_params("arbitrary"),
        name="moe",
    )(step_ids, e_lo, e_hi, n_valid, n_tiles.reshape(1), x_sorted, rec_sorted, norm_w.reshape(1, d), *weight_args)
    return _gather_tokens(out_sorted, dest)


def kernel(x, attn_norm, ffn_norm, even_w_in, even_w_out, hg_lb_logits, hg_o_norm, da_q_norm, da_k_norm, da_lambda, da_subln, odd_w_in, gla_w_alpha, gla_b_alpha, gla_o_norm, odd_w_out, moe_w_group, moe_b_group, moe_w_router, moe_b_router, moe_w_gate, moe_w_up, moe_w_down):
    batch, seq, d = x.shape
    depth = attn_norm.shape[0]
    n = batch * seq
    xf = x.reshape(n, d)

    lb_sm = jax.nn.softmax(hg_lb_logits.astype(F32), axis=0)
    lb_all = jnp.cumsum(lb_sm, axis=0) - lb_sm[0:1]
    rank = gla_w_alpha.shape[1]

    experts = lambda w: w.reshape((-1,) + w.shape[2:])
    moe_w_gate, moe_w_up, moe_w_down = experts(moe_w_gate), experts(moe_w_up), experts(moe_w_down)

    down_ahead = None
    ahead = ()
    for layer in range(depth):
        j = layer // 2
        if layer % 2 == 0:
            lam_init = 0.8 - 0.6 * math.exp(-0.3 * layer)
            proj = _inproj(xf, n, attn_norm[layer], _to_bf16(even_w_in, first=j, count=1)[0], after=ahead)
            o_a = _hgrn(proj, lb_all[j], hg_o_norm[j], batch, seq)
            o_b = _diff_attn(proj, da_q_norm[j], da_k_norm[j], da_lambda[j], da_subln[j], lam_init, batch, seq)
            mix = [o_a, o_b]
            w_out = _to_bf16(even_w_out, first=j, count=1)[0]
        else:
            w_alpha = jnp.pad(gla_w_alpha[j], ((0, GLA_RANK_PAD - rank), (0, 0))).astype(BF16)
            w_in = _to_bf16(jnp.swapaxes(odd_w_in, 1, 2), odd_w_in.shape[2] + GLA_RANK_PAD - rank, first=j, count=1)[0]
            proj = _inproj(xf, n, attn_norm[layer], w_in, w_transposed=True, after=ahead)
            mix = [_gla(proj, w_alpha, gla_b_alpha[j], gla_o_norm[j], batch, seq)]
            w_out = _to_bf16(odd_w_out, first=j, count=1)[0]
        x2, meta, rec, cnt = _outproj_router(mix, w_out, xf, n, ffn_norm[layer], moe_w_group[layer],
                                        moe_b_group[layer], moe_w_router[layer], moe_b_router[layer])
        xf = _moe(x2, n, meta, rec, cnt, ffn_norm[layer], moe_w_gate, moe_w_up, moe_w_down, layer, down_ahead)
        down_ahead = (_to_bf16(moe_w_down, first=(layer + 1) * N_EXPERTS, count=N_EXPERTS)
                      if layer + 1 < depth else None)
        ahead = () if down_ahead is None else (down_ahead,)
    return xf.reshape(batch, seq, d)
```

```python
import functools
import math

import numpy as np
import jax
import jax.numpy as jnp
from jax import lax
from jax.experimental import pallas as pl
from jax.experimental.pallas import tpu as pltpu
from jax.experimental.pallas import tpu_sc as plsc

F32 = jnp.float32
BF16 = jnp.bfloat16
EPS = 1e-6
LOG2_E = 1.4426950408889634

D_MODEL = 1024
LANE = 128
HEADS = 4
DK = 128
HG_WIDTH = 512
DA_HEAD_DIM = 64
GLA_DV = 256
GLA_RANK_PAD = 128
GLA_TAU = 16.0
N_GROUPS = 4
EXPERTS_PER_GROUP = 4
N_EXPERTS = 16
D_EXPERT = 512
N_PAIRS = 6
N_BUCKETS = N_GROUPS * N_PAIRS
ROUTER_ROWS = 32

CHUNK = 128
GLA_STEP_CHUNKS = 4
LEVELS = 7
ATTN_BLOCK = 512
ATTN_HEADS = 4
SCORE_BOUND_MARGIN = 1.05
MAX_SCORE_BOUND = 60.0
ROW_TILE = 1024
ROUTER_TILE = 1024
MOE_TILE = 256
MOE_STEP_TILES = 2
SC_CHUNK = 32
VMEM_LIMIT = 56 * 1024 * 1024
CAST_BLOCK_BYTES = 8 * 1024 * 1024

PAIR_LO = (0, 0, 1, 1, 0, 2)
PAIR_HI = (1, 2, 2, 3, 3, 3)


def _dot(a, b):
    return jnp.dot(a, b, preferred_element_type=F32)


def _dot_nt(a, b):
    return lax.dot_general(a, b, (((1,), (1,)), ((), ())), preferred_element_type=F32)


def _dot_tn(a, b):
    return lax.dot_general(a, b, (((0,), (0,)), ((), ())), preferred_element_type=F32)


def _rms(x, w):
    return x * lax.rsqrt(jnp.mean(x * x, axis=-1, keepdims=True) + EPS) * w


def _silu(x):
    return x * jax.nn.sigmoid(x)


def _params(*sem):
    return pltpu.CompilerParams(dimension_semantics=sem, vmem_limit_bytes=VMEM_LIMIT)


def _cast_kernel(w_ref, o_ref):
    k = w_ref.shape[1]
    o_ref[:, :k, :] = w_ref[...].astype(BF16)
    if o_ref.shape[1] > k:
        o_ref[:, k:, :] = jnp.zeros((o_ref.shape[0], o_ref.shape[1] - k, o_ref.shape[2]), BF16)


def _to_bf16(w, k_pad=None, first=0, count=None):
    _, k, n = w.shape
    l = count or w.shape[0]
    k_pad = k_pad or k
    per_matrix = k * n * 4
    lb = max([c for c in range(1, l + 1) if l % c == 0 and first % c == 0 and c * per_matrix <= CAST_BLOCK_BYTES] or [1])
    nb = n
    while lb == 1 and k * nb * 4 > CAST_BLOCK_BYTES and nb % (2 * LANE) == 0:
        nb //= 2
    return pl.pallas_call(
        _cast_kernel,
        grid=(l // lb, n // nb),
        in_specs=[pl.BlockSpec((lb, k, nb), lambda i, j: (first // lb + i, 0, j))],
        out_specs=pl.BlockSpec((lb, k_pad, nb), lambda i, j: (i, 0, j)),
        out_shape=jax.ShapeDtypeStruct((l, k_pad, n), BF16),
        compiler_params=_params("parallel", "parallel"),
        name="to_bf16",
    )(w)


def _inproj_kernel(x_ref, nw_ref, w_ref, *refs, w_transposed):
    o_ref = refs[-1]
    h = _rms(x_ref[...], nw_ref[...]).astype(BF16)
    o_ref[...] = _dot_nt(h, w_ref[...]) if w_transposed else _dot(h, w_ref[...])


def _inproj(x, n, norm_w, w, w_transposed=False, after=()):
    d = D_MODEL
    n_out = w.shape[0] if w_transposed else w.shape[1]
    return pl.pallas_call(
        functools.partial(_inproj_kernel, w_transposed=w_transposed),
        grid=(n // ROW_TILE,),
        in_specs=[pl.BlockSpec((ROW_TILE, d), lambda i: (i, 0)),
                  pl.BlockSpec((1, d), lambda i: (0, 0)),
                  pl.BlockSpec(w.shape, lambda i: (0, 0))] + [pl.BlockSpec(memory_space=pl.ANY) for _ in after],
        out_specs=pl.BlockSpec((ROW_TILE, n_out), lambda i: (i, 0)),
        out_shape=jax.ShapeDtypeStruct((n, n_out), F32),
        compiler_params=_params("parallel"),
        name="inproj",
    )(x, norm_w.reshape(1, d), w, *after)


def _decay_tables():
    c = CHUNK
    a = np.zeros((LEVELS + 2, c, c), np.float32)
    t = np.arange(c)
    for l in range(LEVELS):
        s = 1 << l
        for r in range(c):
            mid = (r // (2 * s)) * 2 * s + s - 1
            if (r >> l) & 1:
                a[l, r] = (t > mid) & (t <= r)
            else:
                a[l, r] = (t > r) & (t <= mid)
    a[LEVELS] = t[None, :] <= t[:, None]
    a[LEVELS + 1] = t[None, :] > t[:, None]
    a = a.reshape((LEVELS + 2) * c, c)
    a3 = np.concatenate([a, a], axis=1)
    i, j = np.meshgrid(t, t, indexing="ij")
    x = np.maximum(i ^ j, 1)
    lvl = np.where(j < i, np.floor(np.log2(x)).astype(np.int32), np.where(i == j, LEVELS, -1))
    return jnp.asarray(a3, BF16), jnp.asarray(lvl, jnp.int32)


def _decay_factors(a3, g):
    g_hi = g.astype(BF16)
    g_lo = (g - g_hi.astype(F32)).astype(BF16)
    return jnp.exp(_dot(a3, jnp.concatenate([g_hi, g_lo], axis=0)))


def _gla_chunk(q, k, v, f, st_ref, lvl, rowid):
    c = CHUNK
    sc = jnp.where(lvl == LEVELS, jnp.sum(q * k, axis=-1, keepdims=True), 0.0)
    for l in range(LEVELS):
        s = 1 << l
        if s % 8 == 0:
            qk = jnp.concatenate([(q if b & 1 else k)[b * s:(b + 1) * s] for b in range(c // s)], axis=0)
        else:
            qk = jnp.where(((rowid >> l) & 1) == 1, q, k)
        m = (qk * f[l * c:(l + 1) * c]).astype(BF16)
        sc = jnp.where(lvl == l, _dot_nt(m, m), sc)
    qd = (q * f[LEVELS * c:(LEVELS + 1) * c]).astype(BF16)
    kd = (k * f[(LEVELS + 1) * c:(LEVELS + 2) * c]).astype(BF16)
    vb = v.astype(BF16)
    st = st_ref[...]
    o = _dot(sc.astype(BF16), vb) + _dot_nt(qd, st.astype(BF16))
    chunk_decay = f[(LEVELS + 1) * c - 1:(LEVELS + 1) * c, :]
    st_ref[...] = st * chunk_decay + _dot_tn(vb, kd)
    return o


def _hgrn_kernel(q_ref, f_ref, i_ref, g_ref, lb_ref, on_ref, a3_ref, lvl_ref, o_ref, st_ref):
    @pl.when(pl.program_id(1) == 0)
    def _():
        st_ref[...] = jnp.zeros_like(st_ref)

    lvl = lvl_ref[...]
    rowid = lax.broadcasted_iota(jnp.int32, (CHUNK, 1), 0)
    lb = lb_ref[...]
    for c in range(GLA_STEP_CHUNKS):
        rows = slice(c * CHUNK, (c + 1) * CHUNK)
        fl = f_ref[rows, :]
        f = _decay_factors(a3_ref[...], jnp.log(lb + (1.0 - lb) * jax.nn.sigmoid(fl)))
        key_in = (1.0 - lb) * jax.nn.sigmoid(-fl)
        for h in range(HEADS):
            sl = slice(h * DK, (h + 1) * DK)
            q = _silu(q_ref[rows, sl]) * (DK ** -0.5)
            o = _gla_chunk(q, key_in[:, sl], i_ref[rows, sl], f[:, sl], st_ref.at[h], lvl, rowid)
            o = _rms(o, on_ref[...])
            o_ref[rows, sl] = (o * _silu(g_ref[rows, sl])).astype(BF16)


def _hgrn(proj, lb, o_norm, batch, seq):
    n = proj.shape[0]
    rows = CHUNK * GLA_STEP_CHUNKS
    nt = seq // rows
    a3, lvl = _decay_tables()
    col = lambda j: pl.BlockSpec((rows, HG_WIDTH), lambda b, t: (b * nt + t, j))
    const = lambda shape: pl.BlockSpec(shape, lambda b, t: (0, 0))
    return pl.pallas_call(
        _hgrn_kernel,
        grid=(batch, nt),
        in_specs=[col(0), col(1), col(2), col(3), const((1, HG_WIDTH)), const((1, DK)),
                  const(a3.shape), const(lvl.shape)],
        out_specs=pl.BlockSpec((rows, HG_WIDTH), lambda b, t: (b * nt + t, 0)),
        out_shape=jax.ShapeDtypeStruct((n, HG_WIDTH), BF16),
        scratch_shapes=[pltpu.VMEM((HEADS, DK, DK), F32)],
        compiler_params=_params("parallel", "arbitrary"),
        name="hgrn",
    )(proj, proj, proj, proj, lb.reshape(1, HG_WIDTH), o_norm.reshape(1, DK), a3, lvl)


def _log_sigmoid(x):
    return jnp.minimum(x, 0.0) - jnp.log1p(jnp.exp(-jnp.abs(x)))


def _gla_kernel(q_ref, k_ref, v_ref, g_ref, a_ref, wa_ref, ba_ref, on_ref, a3_ref, lvl_ref, o_ref, st_ref):
    @pl.when(pl.program_id(1) == 0)
    def _():
        st_ref[...] = jnp.zeros_like(st_ref)

    lvl = lvl_ref[...]
    rowid = lax.broadcasted_iota(jnp.int32, (CHUNK, 1), 0)
    for c in range(GLA_STEP_CHUNKS):
        rows = slice(c * CHUNK, (c + 1) * CHUNK)
        gate_logit = _dot(a_ref[rows, :].astype(BF16), wa_ref[...]) + ba_ref[...]
        f = _decay_factors(a3_ref[...], _log_sigmoid(gate_logit) * (1.0 / GLA_TAU))
        for h in range(HEADS):
            sl = slice(h * DK, (h + 1) * DK)
            sv = slice(h * GLA_DV, (h + 1) * GLA_DV)
            q = q_ref[rows, sl] * (DK ** -0.5)
            o = _gla_chunk(q, k_ref[rows, sl], v_ref[rows, sv], f[:, sl], st_ref.at[h], lvl, rowid)
            o = _rms(o, on_ref[...])
            o_ref[rows, sv] = (o * _silu(g_ref[rows, sv])).astype(BF16)


def _gla(proj, w_alpha, b_alpha, o_norm, batch, seq):
    n = proj.shape[0]
    rows = CHUNK * GLA_STEP_CHUNKS
    nt = seq // rows
    qk_w = HEADS * DK
    v_w = HEADS * GLA_DV
    a3, lvl = _decay_tables()
    row = lambda b, t: b * nt + t
    const = lambda shape: pl.BlockSpec(shape, lambda b, t: (0, 0))
    return pl.pallas_call(
        _gla_kernel,
        grid=(batch, nt),
        in_specs=[pl.BlockSpec((rows, qk_w), lambda b, t: (row(b, t), 0)),
                  pl.BlockSpec((rows, qk_w), lambda b, t: (row(b, t), 1)),
                  pl.BlockSpec((rows, v_w), lambda b, t: (row(b, t), 1)),
                  pl.BlockSpec((rows, v_w), lambda b, t: (row(b, t), 2)),
                  pl.BlockSpec((rows, GLA_RANK_PAD), lambda b, t: (row(b, t), (2 * qk_w + 2 * v_w) // GLA_RANK_PAD)),
                  const((GLA_RANK_PAD, qk_w)), const((1, qk_w)), const((1, GLA_DV)),
                  const(a3.shape), const(lvl.shape)],
        out_specs=pl.BlockSpec((rows, v_w), lambda b, t: (row(b, t), 0)),
        out_shape=jax.ShapeDtypeStruct((n, v_w), BF16),
        scratch_shapes=[pltpu.VMEM((HEADS, GLA_DV, DK), F32)],
        compiler_params=_params("parallel", "arbitrary"),
        name="gla",
    )(proj, proj, proj, proj, proj, w_alpha, b_alpha.reshape(1, qk_w), o_norm.reshape(1, GLA_DV), a3, lvl)


def _qk_norm(x, w2):
    lane = lax.broadcasted_iota(jnp.int32, x.shape, 1)
    first = lane < DA_HEAD_DIM
    sq = x * x
    ms1 = jnp.sum(jnp.where(first, sq, 0.0), axis=-1, keepdims=True) * (1.0 / DA_HEAD_DIM)
    ms2 = jnp.sum(jnp.where(first, 0.0, sq), axis=-1, keepdims=True) * (1.0 / DA_HEAD_DIM)
    r = jnp.where(first, lax.rsqrt(ms1 + EPS), lax.rsqrt(ms2 + EPS))
    return x * r * w2


def _score_bound(q_norm_w, k_norm_w):
    return (jnp.max(jnp.abs(q_norm_w)) * jnp.max(jnp.abs(k_norm_w))
            * (DA_HEAD_DIM * DA_HEAD_DIM ** -0.5 * LOG2_E * SCORE_BOUND_MARGIN))


def _attn_kernel(qi_tab, ki_tab, q_ref, k_ref, v_ref, qn_ref, kn_ref, lam_ref, sub_ref, o_ref,
                 qs_ref, m_ref, l_ref, acc_ref, *, lam_init, bounded):
    p = pl.program_id(2)
    qi = qi_tab[p]
    ki = ki_tab[p]
    tq = ATTN_BLOCK
    dv = 2 * DA_HEAD_DIM
    heads = [slice(h * dv, (h + 1) * dv) for h in range(ATTN_HEADS)]

    @pl.when(ki == 0)
    def _():
        for h, sl in enumerate(heads):
            qn = _qk_norm(q_ref[:, sl], qn_ref[...])
            qt = (qn * (DA_HEAD_DIM ** -0.5 * LOG2_E)).T
            top = lax.broadcasted_iota(jnp.int32, qt.shape, 0) < DA_HEAD_DIM
            q1 = jnp.where(top, qt, 0.0).astype(BF16)
            q2 = jnp.where(top, 0.0, qt).astype(BF16)
            for copy in range(1 if bounded else 2):
                qs_ref[h, copy, :, :tq] = q1
                qs_ref[h, copy, :, tq:] = q2
            if bounded:
                bound = _score_bound(qn_ref[...], kn_ref[...])
                row = lax.broadcasted_iota(jnp.int32, (dv, 2 * tq), 0)
                qs_ref[h, 1] = jnp.where(row == 0, -bound, 0.0).astype(BF16)
        if not bounded:
            m_ref[...] = jnp.full_like(m_ref, -jnp.inf)
        l_ref[...] = jnp.zeros_like(l_ref)
        acc_ref[...] = jnp.zeros_like(acc_ref)

    def step(masked):
        tk = k_ref.shape[0]
        ones = jnp.ones((16, tk), BF16)
        unit = jnp.where(lax.broadcasted_iota(jnp.int32, (tk, dv), 1) == 0, 1.0, 0.0).astype(BF16)
        for h, sl in enumerate(heads):
            kn = _qk_norm(k_ref[:, sl], kn_ref[...]).astype(BF16)

            def mask(s):
                if not masked:
                    return s
                key = lax.broadcasted_iota(jnp.int32, s.shape, 0)
                qry = lax.broadcasted_iota(jnp.int32, s.shape, 1)
                qry = jnp.where(qry >= tq, qry - tq, qry)
                return jnp.where(key <= qry, s, -jnp.inf)

            if bounded:
                shifted = mask(_dot(jnp.concatenate([kn, unit], axis=1), qs_ref[h].reshape(2 * dv, 2 * tq)))
                alpha = 1.0
            else:
                m_prev = m_ref[h]
                m_new = jnp.maximum(m_prev, jnp.max(mask(_dot(kn, qs_ref[h, 0])), axis=0, keepdims=True))
                alpha = jnp.exp2(m_prev - m_new)
                shifted = mask(_dot(kn, qs_ref[h, 1])) - m_new
                m_ref[h] = m_new
            pr = jnp.exp2(shifted).astype(BF16) if bounded else jnp.exp2(shifted.astype(BF16))
            vt = jnp.concatenate([v_ref[:, sl].T.astype(BF16), ones], axis=0)
            pv = _dot(vt, pr)
            l_ref[h] = alpha * l_ref[h] + pv[dv:dv + 1]
            acc_ref[h] = alpha * acc_ref[h] + pv[:dv]

    @pl.when(ki < qi)
    def _():
        step(False)

    @pl.when(ki == qi)
    def _():
        step(True)
        lv = lam_ref[...]
        lam = (jnp.exp(jnp.sum(lv[0:1] * lv[1:2], axis=-1, keepdims=True))
               - jnp.exp(jnp.sum(lv[2:3] * lv[3:4], axis=-1, keepdims=True)) + lam_init)
        for h, sl in enumerate(heads):
            o1 = acc_ref[h, :, :tq] / l_ref[h, :, :tq]
            o2 = acc_ref[h, :, tq:] / l_ref[h, :, tq:]
            o = o1 - lam * o2
            o = o * lax.rsqrt(jnp.mean(o * o, axis=0, keepdims=True) + EPS) * sub_ref[...] * (1.0 - lam_init)
            o_ref[:, sl] = o.T.astype(BF16)


def _diff_attn(proj, q_norm, k_norm, lam_vecs, subln, lam_init, batch, seq):
    n = proj.shape[0]
    blk = ATTN_BLOCK
    nq = seq // blk
    dv = 2 * DA_HEAD_DIM
    pairs = [(qi, ki) for qi in range(nq) for ki in range(qi + 1)]
    qi_tab = jnp.asarray([p[0] for p in pairs], jnp.int32)
    ki_tab = jnp.asarray([p[1] for p in pairs], jnp.int32)
    width = ATTN_HEADS * dv
    groups = HEADS // ATTN_HEADS
    q_col = (4 * HG_WIDTH) // width
    const = lambda shape: pl.BlockSpec(shape, lambda b, g, p, qt, kt: (0, 0))
    grid_spec = pltpu.PrefetchScalarGridSpec(
        num_scalar_prefetch=2,
        grid=(batch, groups, len(pairs)),
        in_specs=[pl.BlockSpec((blk, width), lambda b, g, p, qt, kt: (b * nq + qt[p], q_col + g)),
                  pl.BlockSpec((blk, width), lambda b, g, p, qt, kt: (b * nq + kt[p], q_col + groups + g)),
                  pl.BlockSpec((blk, width), lambda b, g, p, qt, kt: (b * nq + kt[p], q_col + 2 * groups + g)),
                  const((1, dv)), const((1, dv)), const((4, DA_HEAD_DIM)), const((dv, 1))],
        out_specs=pl.BlockSpec((blk, width), lambda b, g, p, qt, kt: (b * nq + qt[p], g)),
        scratch_shapes=[pltpu.VMEM((ATTN_HEADS, 2, dv, 2 * blk), BF16),
                        pltpu.VMEM((ATTN_HEADS, 1, 2 * blk), F32),
                        pltpu.VMEM((ATTN_HEADS, 1, 2 * blk), F32),
                        pltpu.VMEM((ATTN_HEADS, dv, 2 * blk), F32)])
    args = (qi_tab, ki_tab, proj, proj, proj, jnp.tile(q_norm, 2).reshape(1, -1), jnp.tile(k_norm, 2).reshape(1, -1),
            lam_vecs, subln.reshape(-1, 1))

    def run(bounded):
        return pl.pallas_call(
            functools.partial(_attn_kernel, lam_init=lam_init, bounded=bounded),
            grid_spec=grid_spec,
            out_shape=jax.ShapeDtypeStruct((n, HEADS * dv), BF16),
            compiler_params=_params("parallel", "parallel", "arbitrary"),
            name="diff_attn_bounded" if bounded else "diff_attn",
        )(*args)

    return lax.cond(_score_bound(q_norm, k_norm) <= MAX_SCORE_BOUND, lambda: run(True), lambda: run(False))


def _outproj_router_kernel(*refs, n_a):
    a_refs = refs[:n_a]
    (w_ref, x_ref, nw_ref, wr_hi_ref, wr_lo_ref, br_ref, tri_ref,
     x2_ref, meta_ref, rec_ref, cnt_ref, carry_ref) = refs[n_a:]

    @pl.when(pl.program_id(0) == 0)
    def _():
        carry_ref[...] = jnp.zeros_like(carry_ref)

    x2 = x_ref[...]
    off = 0
    for a_ref in a_refs:
        kk = a_ref.shape[1]
        x2 = x2 + _dot(a_ref[...], w_ref[off:off + kk, :])
        off += kk
    x2_ref[...] = x2

    h = _rms(x2, nw_ref[...])
    h_hi = h.astype(BF16)
    h_lo = (h - h_hi.astype(F32)).astype(BF16)
    wr_hi = wr_hi_ref[...]
    by_hi = _dot_nt(jnp.concatenate([wr_hi, wr_lo_ref[...]], axis=0), h_hi)
    lt = by_hi[:ROUTER_ROWS] + by_hi[ROUTER_ROWS:] + _dot_nt(wr_hi, h_lo) + br_ref[...]
    row = lambda r: lt[r:r + 1, :]

    gl = [row(g) for g in range(N_GROUPS)]
    gmax = functools.reduce(jnp.maximum, gl)
    gsel = jnp.where(gl[0] == gmax, 0.0, jnp.where(gl[1] == gmax, 1.0, jnp.where(gl[2] == gmax, 2.0, 3.0)))
    g_w = 1.0 / functools.reduce(jnp.add, [jnp.exp(x - gmax) for x in gl])

    el = []
    for e in range(EXPERTS_PER_GROUP):
        v = row(N_GROUPS + e)
        for g in range(1, N_GROUPS):
            v = jnp.where(gsel == float(g), row(N_GROUPS + EXPERTS_PER_GROUP * g + e), v)
        el.append(v)
    emax = functools.reduce(jnp.maximum, el)
    sel = []
    for e in range(EXPERTS_PER_GROUP):
        ahead = jnp.zeros_like(emax)
        for o in range(EXPERTS_PER_GROUP):
            if o < e:
                ahead = ahead + jnp.where(el[o] >= el[e], 1.0, 0.0)
            elif o > e:
                ahead = ahead + jnp.where(el[o] > el[e], 1.0, 0.0)
        sel.append(ahead < 2.0)
    pe = [jnp.where(sel[e], jnp.exp(el[e] - emax), 0.0) for e in range(EXPERTS_PER_GROUP)]
    scale = g_w / functools.reduce(jnp.add, pe)
    zero = jnp.zeros_like(emax)
    seen = zero
    lo_idx, hi_idx, c_lo, c_hi = zero, zero, zero, zero
    for e in range(EXPERTS_PER_GROUP):
        is_lo = sel[e] & (seen == 0.0)
        is_hi = sel[e] & (seen == 1.0)
        lo_idx = jnp.where(is_lo, float(e), lo_idx)
        hi_idx = jnp.where(is_hi, float(e), hi_idx)
        c_lo = jnp.where(is_lo, pe[e] * scale, c_lo)
        c_hi = jnp.where(is_hi, pe[e] * scale, c_hi)
        seen = seen + jnp.where(sel[e], 1.0, 0.0)
    pair = zero
    for p in range(1, N_PAIRS):
        pair = jnp.where((lo_idx == float(PAIR_LO[p])) & (hi_idx == float(PAIR_HI[p])), float(p), pair)
    bucket = gsel * float(N_PAIRS) + pair

    bid = lax.broadcasted_iota(jnp.int32, (ROUTER_ROWS, bucket.shape[1]), 0).astype(F32)
    onehot = jnp.where(bid == bucket, 1.0, 0.0)
    before = _dot(onehot.astype(BF16), tri_ref[...]) + carry_ref[...]
    rank = jnp.sum(onehot * before, axis=0, keepdims=True)
    carry = carry_ref[...] + jnp.sum(onehot, axis=1, keepdims=True)
    carry_ref[...] = carry
    cnt_ref[...] = carry
    meta = jnp.concatenate([bucket, rank, c_lo, c_hi, zero, zero, zero, zero], axis=0)
    meta_ref[...] = meta
    rec_ref[...] = jnp.concatenate([meta, jnp.zeros((LANE - meta.shape[0], meta.shape[1]), F32)], axis=0).T


def _outproj_router(a_list, w_out, x, n, norm_w, w_group, b_group, w_router, b_router):
    d = D_MODEL
    tm = ROUTER_TILE
    tok = lambda: pl.BlockSpec((tm, d), lambda i: (i, 0))
    wr = jnp.concatenate([w_group, w_router.transpose(1, 0, 2).reshape(d, N_EXPERTS)], axis=1)
    wr_t = jnp.zeros((ROUTER_ROWS, d), F32).at[:wr.shape[1]].set(wr.T)
    wr_hi = wr_t.astype(BF16)
    wr_lo = (wr_t - wr_hi.astype(F32)).astype(BF16)
    br = jnp.zeros((ROUTER_ROWS, 1), F32).at[:wr.shape[1], 0].set(
        jnp.concatenate([b_group, b_router.reshape(-1)]))
    tri = jnp.asarray(np.triu(np.ones((tm, tm), np.float32), 1), BF16)
    const = lambda shape: pl.BlockSpec(shape, lambda i: (0, 0))
    in_specs = [pl.BlockSpec((tm, a.shape[1]), lambda i: (i, 0)) for a in a_list]
    in_specs += [const((d, d)), tok(), const((1, d)),
                 const((ROUTER_ROWS, d)), const((ROUTER_ROWS, d)), const((ROUTER_ROWS, 1)), const((tm, tm))]
    return pl.pallas_call(
        functools.partial(_outproj_router_kernel, n_a=len(a_list)),
        grid=(n // tm,),
        in_specs=in_specs,
        out_specs=[tok(),
                   pl.BlockSpec((8, tm), lambda i: (0, i)),
                   pl.BlockSpec((tm, LANE), lambda i: (i, 0)),
                   const((ROUTER_ROWS, 1))],
        out_shape=[jax.ShapeDtypeStruct((n, d), F32),
                   jax.ShapeDtypeStruct((8, n), F32),
                   jax.ShapeDtypeStruct((n, LANE), F32),
                   jax.ShapeDtypeStruct((ROUTER_ROWS, 1), F32)],
        scratch_shapes=[pltpu.VMEM((ROUTER_ROWS, 1), F32)],
        compiler_params=_params("arbitrary"),
        name="outproj_router",
    )(*a_list, w_out, x, norm_w.reshape(1, d), wr_hi, wr_lo, br, tri)


def _gather_tokens(table, idx):
    m = idx.shape[0]
    info = plsc.get_sparse_core_info()
    n_workers = info.num_cores * info.num_subcores
    per_worker = m // n_workers
    n_chunks = per_worker // SC_CHUNK
    assert per_worker * n_workers == m and n_chunks * SC_CHUNK == per_worker and n_chunks % 2 == 0
    mesh = plsc.VectorSubcoreMesh(core_axis_name="core", subcore_axis_name="subcore")
    rows = pltpu.VMEM((SC_CHUNK,) + table.shape[1:], table.dtype)

    @functools.partial(
        pl.kernel, mesh=mesh,
        out_type=jax.ShapeDtypeStruct((m,) + table.shape[1:], table.dtype),
        scratch_types=[pltpu.VMEM((n_chunks, SC_CHUNK), jnp.int32), rows, rows,
                       pltpu.SemaphoreType.DMA, pltpu.SemaphoreType.DMA],
        compiler_params=pltpu.CompilerParams(use_tc_tiling_on_sc=True),
        name="gather_tokens")
    def gather(table_hbm, idx_hbm, out_hbm, idx_v, rows_a, rows_b, sem_a, sem_b):
        worker = lax.axis_index("subcore") * info.num_cores + lax.axis_index("core")
        bufs = ((rows_a, sem_a), (rows_b, sem_b))

        def fetch(j, buf):
            return pltpu.make_async_copy(table_hbm.at[idx_v.at[j]], bufs[buf][0], bufs[buf][1])

        pltpu.sync_copy(idx_hbm.at[worker], idx_v)
        fetch(0, 0).start()

        @pl.loop(0, n_chunks, step=2)
        def _(j0):
            for buf in range(2):
                j = j0 + buf
                fetch(j, buf).wait()

                @pl.when(j + 1 < n_chunks)
                def _():
                    fetch(j + 1, 1 - buf).start()

                off = pl.multiple_of(worker * per_worker + j * SC_CHUNK, SC_CHUNK)
                pltpu.sync_copy(bufs[buf][0], out_hbm.at[pl.ds(off, SC_CHUNK)])

    return gather(table, idx.reshape(n_workers, n_chunks, SC_CHUNK))


def _scatter_tokens(arrays, idx, m):
    n = idx.shape[0]
    na = len(arrays)
    info = plsc.get_sparse_core_info()
    n_workers = info.num_cores * info.num_subcores
    per_worker = n // n_workers
    n_chunks = per_worker // SC_CHUNK
    assert per_worker * n_workers == n and n_chunks * SC_CHUNK == per_worker and n_chunks % 2 == 0
    mesh = plsc.VectorSubcoreMesh(core_axis_name="core", subcore_axis_name="subcore")
    bufs = [pltpu.VMEM((SC_CHUNK,) + a.shape[1:], a.dtype) for a in arrays for _ in range(2)]

    @functools.partial(
        pl.kernel, mesh=mesh,
        out_type=[jax.ShapeDtypeStruct((m,) + a.shape[1:], a.dtype) for a in arrays],
        scratch_types=[pltpu.VMEM((n_chunks, SC_CHUNK), jnp.int32)] + bufs + [pltpu.SemaphoreType.DMA] * (2 * na),
        compiler_params=pltpu.CompilerParams(use_tc_tiling_on_sc=True),
        name="scatter_tokens")
    def scatter(*refs):
        ins, idx_hbm, outs = refs[:na], refs[na], refs[na + 1:2 * na + 1]
        idx_v = refs[2 * na + 1]
        buf_v = refs[2 * na + 2:4 * na + 2]
        sem_v = refs[4 * na + 2:]
        worker = lax.axis_index("subcore") * info.num_cores + lax.axis_index("core")

        def load(a, j, b):
            off = pl.multiple_of(worker * per_worker + j * SC_CHUNK, SC_CHUNK)
            return pltpu.make_async_copy(ins[a].at[pl.ds(off, SC_CHUNK)], buf_v[2 * a + b], sem_v[2 * a + b])

        pltpu.sync_copy(idx_hbm.at[worker], idx_v)
        for a in range(na):
            load(a, 0, 0).start()

        @pl.loop(0, n_chunks, step=2)
        def _(j0):
            for b in range(2):
                j = j0 + b
                for a in range(na):
                    load(a, j, b).wait()

                @pl.when(j + 1 < n_chunks)
                def _():
                    for a in range(na):
                        load(a, j + 1, 1 - b).start()

                for a in range(na):
                    pltpu.sync_copy(buf_v[2 * a + b], outs[a].at[idx_v.at[j]])

    return scatter(*arrays, idx.reshape(n_workers, n_chunks, SC_CHUNK))


def _moe_kernel(step_ids, e_lo, e_hi, n_valid, n_tiles, x_ref, rec_ref, nw_ref,
                g_lo_ref, g_hi_ref, u_lo_ref, u_hi_ref, d_lo_ref, d_hi_ref, o_ref):
    del step_ids, e_lo, e_hi, n_tiles
    s = pl.program_id(0)
    live_tiles = sum((n_valid[s * MOE_STEP_TILES + i] > 0).astype(jnp.int32) for i in range(MOE_STEP_TILES))

    def run(tiles):
        for i in range(tiles):
            rows = pl.ds(i * MOE_TILE, MOE_TILE)
            valid = lax.broadcasted_iota(jnp.int32, (MOE_TILE, 1), 0) < n_valid[s * MOE_STEP_TILES + i]
            x = jnp.where(valid, x_ref[rows, :], 0.0)
            h = _rms(x, nw_ref[...]).astype(BF16)
            rec = jnp.where(valid, rec_ref[rows, :], 0.0)

            def expert(g_ref, u_ref, d_ref):
                act = _silu(_dot(h, g_ref[0])) * _dot(h, u_ref[0])
                return _dot(act.astype(BF16), d_ref[0])

            y = (rec[:, 2:3] * expert(g_lo_ref, u_lo_ref, d_lo_ref)
                 + rec[:, 3:4] * expert(g_hi_ref, u_hi_ref, d_hi_ref))
            o_ref[rows, :] = jnp.where(valid, x_ref[rows, :], 0.0) + y

    for tiles in range(1, MOE_STEP_TILES + 1):
        pl.when(live_tiles == tiles)(functools.partial(run, tiles))


def _moe(x2, n, meta, rec, cnt, norm_w, w_gate, w_up, w_down, layer, w_down_bf16=None):
    d = D_MODEL
    tm = MOE_TILE
    k = MOE_STEP_TILES
    max_tiles = n // tm + N_BUCKETS * k
    n_slots = max_tiles * tm
    counts = cnt[:N_BUCKETS, 0].astype(jnp.int32)
    tiles_per_bucket = (counts + k * tm - 1) // (k * tm) * k
    tile_end = jnp.cumsum(tiles_per_bucket)
    tile_start = tile_end - tiles_per_bucket
    n_tiles = tile_end[-1]
    in_bucket = meta[0][:, None] == jnp.arange(N_BUCKETS, dtype=F32)[None, :]
    dest = meta[1].astype(jnp.int32) + jnp.sum(jnp.where(in_bucket, tile_start[None, :] * tm, 0), axis=1)
    x_sorted, rec_sorted = _scatter_tokens([x2, rec], dest, n_slots)
    first = layer * N_EXPERTS
    w_gate, w_up = (_to_bf16(w, first=first, count=N_EXPERTS) for w in (w_gate, w_up))
    w_down = _to_bf16(w_down, first=first, count=N_EXPERTS) if w_down_bf16 is None else w_down_bf16

    tile_raw = jnp.arange(max_tiles, dtype=jnp.int32)
    tile_ids = jnp.minimum(tile_raw, n_tiles - 1)
    step_ids = jnp.minimum(jnp.arange(max_tiles // k, dtype=jnp.int32), (n_tiles - 1) // k)
    of_bucket = lambda table: jnp.sum(jnp.where(tile_bucket[:, None] == jnp.arange(N_BUCKETS)[None, :], table[None, :], 0), axis=1)
    tile_bucket = jnp.sum((tile_ids[:, None] >= tile_end[None, :]).astype(jnp.int32), axis=1)
    n_valid = jnp.clip(of_bucket(counts) - (tile_ids - of_bucket(tile_start)) * tm, 0, tm)
    n_valid = jnp.where(tile_raw < n_tiles, n_valid, 0)
    group = tile_bucket // N_PAIRS
    pair = tile_bucket % N_PAIRS
    pair_lo = jnp.sum(jnp.where(pair[:, None] == jnp.arange(N_PAIRS)[None, :], jnp.asarray(PAIR_LO, jnp.int32)[None, :], 0), axis=1)
    pair_hi = jnp.sum(jnp.where(pair[:, None] == jnp.arange(N_PAIRS)[None, :], jnp.asarray(PAIR_HI, jnp.int32)[None, :], 0), axis=1)
    e_lo = group * EXPERTS_PER_GROUP + pair_lo
    e_hi = group * EXPERTS_PER_GROUP + pair_hi

    step_rows = lambda s, sid, lo, hi, nv, nt: (sid[s], 0)
    wlo = lambda s, sid, lo, hi, nv, nt: (lo[s * k], 0, 0)
    whi = lambda s, sid, lo, hi, nv, nt: (hi[s * k], 0, 0)
    weight_specs = [pl.BlockSpec((1, d, D_EXPERT), wlo), pl.BlockSpec((1, d, D_EXPERT), whi),
                    pl.BlockSpec((1, d, D_EXPERT), wlo), pl.BlockSpec((1, d, D_EXPERT), whi),
                    pl.BlockSpec((1, D_EXPERT, d), wlo), pl.BlockSpec((1, D_EXPERT, d), whi)]
    weight_args = [w_gate, w_gate, w_up, w_up, w_down, w_down]
    grid_spec = pltpu.PrefetchScalarGridSpec(
        num_scalar_prefetch=5,
        grid=(max_tiles // k,),
        in_specs=[pl.BlockSpec((k * tm, d), step_rows), pl.BlockSpec((k * tm, LANE), step_rows),
                  pl.BlockSpec((1, d), lambda s, sid, lo, hi, nv, nt: (0, 0))] + weight_specs,
        out_specs=pl.BlockSpec((k * tm, d), step_rows))
    out_sorted = pl.pallas_call(
        _moe_kernel,
        grid_spec=grid_spec,
        out_shape=jax.ShapeDtypeStruct((n_slots, d), F32),
        compiler_params=_params("arbitrary"),
        name="moe",
    )(step_ids, e_lo, e_hi, n_valid, n_tiles.reshape(1), x_sorted, rec_sorted, norm_w.reshape(1, d), *weight_args)
    return _gather_tokens(out_sorted, dest)


def kernel(x, attn_norm, ffn_norm, even_w_in, even_w_out, hg_lb_logits, hg_o_norm, da_q_norm, da_k_norm, da_lambda, da_subln, odd_w_in, gla_w_alpha, gla_b_alpha, gla_o_norm, odd_w_out, moe_w_group, moe_b_group, moe_w_router, moe_b_router, moe_w_gate, moe_w_up, moe_w_down):
    batch, seq, d = x.shape
    depth = attn_norm.shape[0]
    n = batch * seq
    xf = x.reshape(n, d)

    lb_sm = jax.nn.softmax(hg_lb_logits.astype(F32), axis=0)
    lb_all = jnp.cumsum(lb_sm, axis=0) - lb_sm[0:1]
    rank = gla_w_alpha.shape[1]

    experts = lambda w: w.reshape((-1,) + w.shape[2:])
    moe_w_gate, moe_w_up, moe_w_down = experts(moe_w_gate), experts(moe_w_up), experts(moe_w_down)

    down_ahead = None
    ahead = ()
    for layer in range(depth):
        j = layer // 2
        if layer % 2 == 0:
            lam_init = 0.8 - 0.6 * math.exp(-0.3 * layer)
            proj = _inproj(xf, n, attn_norm[layer], _to_bf16(even_w_in, first=j, count=1)[0], after=ahead)
            o_a = _hgrn(proj, lb_all[j], hg_o_norm[j], batch, seq)
            o_b = _diff_attn(proj, da_q_norm[j], da_k_norm[j], da_lambda[j], da_subln[j], lam_init, batch, seq)
            mix = [o_a, o_b]
            w_out = _to_bf16(even_w_out, first=j, count=1)[0]
        else:
            w_alpha = jnp.pad(gla_w_alpha[j], ((0, GLA_RANK_PAD - rank), (0, 0))).astype(BF16)
            w_in = _to_bf16(jnp.swapaxes(odd_w_in, 1, 2), odd_w_in.shape[2] + GLA_RANK_PAD - rank, first=j, count=1)[0]
            proj = _inproj(xf, n, attn_norm[layer], w_in, w_transposed=True, after=ahead)
            mix = [_gla(proj, w_alpha, gla_b_alpha[j], gla_o_norm[j], batch, seq)]
            w_out = _to_bf16(odd_w_out, first=j, count=1)[0]
        x2, meta, rec, cnt = _outproj_router(mix, w_out, xf, n, ffn_norm[layer], moe_w_group[layer],
                                        moe_b_group[layer], moe_w_router[layer], moe_b_router[layer])
        xf = _moe(x2, n, meta, rec, cnt, ffn_norm[layer], moe_w_gate, moe_w_up, moe_w_down, layer, down_ahead)
        down_ahead = (_to_bf16(moe_w_down, first=(layer + 1) * N_EXPERTS, count=N_EXPERTS)
                      if layer + 1 < depth else None)
        ahead = () if down_ahead is None else (down_ahead,)
    return xf.reshape(batch, seq, d)
```

```python
import functools
import math

import numpy as np
import jax
import jax.numpy as jnp
from jax import lax
from jax.experimental import pallas as pl
from jax.experimental.pallas import tpu as pltpu
from jax.experimental.pallas import tpu_sc as plsc

F32 = jnp.float32
BF16 = jnp.bfloat16
EPS = 1e-6
LOG2_E = 1.4426950408889634

D_MODEL = 1024
LANE = 128
HEADS = 4
DK = 128
HG_WIDTH = 512
DA_HEAD_DIM = 64
GLA_DV = 256
GLA_RANK_PAD = 128
GLA_TAU = 16.0
N_GROUPS = 4
EXPERTS_PER_GROUP = 4
N_EXPERTS = 16
D_EXPERT = 512
N_PAIRS = 6
N_BUCKETS = N_GROUPS * N_PAIRS
ROUTER_ROWS = 32

CHUNK = 128
GLA_STEP_CHUNKS = 4
LEVELS = 7
ATTN_BLOCK = 512
ATTN_HEADS = 4
SCORE_BOUND_MARGIN = 1.05
MAX_SCORE_BOUND = 60.0
ROW_TILE = 1024
ROUTER_TILE = 1024
MOE_TILE = 256
MOE_STEP_TILES = 2
MOE_X_SLOTS = 3
SC_CHUNK = 32
VMEM_LIMIT = 56 * 1024 * 1024
CAST_BLOCK_BYTES = 8 * 1024 * 1024

PAIR_LO = (0, 0, 1, 1, 0, 2)
PAIR_HI = (1, 2, 2, 3, 3, 3)


def _dot(a, b):
    return jnp.dot(a, b, preferred_element_type=F32)


def _dot_nt(a, b):
    return lax.dot_general(a, b, (((1,), (1,)), ((), ())), preferred_element_type=F32)


def _dot_tn(a, b):
    return lax.dot_general(a, b, (((0,), (0,)), ((), ())), preferred_element_type=F32)


def _rms(x, w):
    return x * lax.rsqrt(jnp.mean(x * x, axis=-1, keepdims=True) + EPS) * w


def _silu(x):
    return x * jax.nn.sigmoid(x)


def _params(*sem):
    return pltpu.CompilerParams(dimension_semantics=sem, vmem_limit_bytes=VMEM_LIMIT)


def _cast_kernel(w_ref, o_ref):
    k = w_ref.shape[1]
    o_ref[:, :k, :] = w_ref[...].astype(BF16)
    if o_ref.shape[1] > k:
        o_ref[:, k:, :] = jnp.zeros((o_ref.shape[0], o_ref.shape[1] - k, o_ref.shape[2]), BF16)


def _to_bf16(w, k_pad=None, first=0, count=None):
    _, k, n = w.shape
    l = count or w.shape[0]
    k_pad = k_pad or k
    per_matrix = k * n * 4
    lb = max([c for c in range(1, l + 1) if l % c == 0 and first % c == 0 and c * per_matrix <= CAST_BLOCK_BYTES] or [1])
    nb = n
    while lb == 1 and k * nb * 4 > CAST_BLOCK_BYTES and nb % (2 * LANE) == 0:
        nb //= 2
    return pl.pallas_call(
        _cast_kernel,
        grid=(l // lb, n // nb),
        in_specs=[pl.BlockSpec((lb, k, nb), lambda i, j: (first // lb + i, 0, j))],
        out_specs=pl.BlockSpec((lb, k_pad, nb), lambda i, j: (i, 0, j)),
        out_shape=jax.ShapeDtypeStruct((l, k_pad, n), BF16),
        compiler_params=_params("parallel", "parallel"),
        name="to_bf16",
    )(w)


def _inproj_kernel(x_ref, nw_ref, w_ref, *refs, w_transposed):
    o_ref = refs[-1]
    h = _rms(x_ref[...], nw_ref[...]).astype(BF16)
    o_ref[...] = _dot_nt(h, w_ref[...]) if w_transposed else _dot(h, w_ref[...])


def _inproj(x, n, norm_w, w, w_transposed=False, after=()):
    d = D_MODEL
    n_out = w.shape[0] if w_transposed else w.shape[1]
    return pl.pallas_call(
        functools.partial(_inproj_kernel, w_transposed=w_transposed),
        grid=(n // ROW_TILE,),
        in_specs=[pl.BlockSpec((ROW_TILE, d), lambda i: (i, 0)),
                  pl.BlockSpec((1, d), lambda i: (0, 0)),
                  pl.BlockSpec(w.shape, lambda i: (0, 0))] + [pl.BlockSpec(memory_space=pl.ANY) for _ in after],
        out_specs=pl.BlockSpec((ROW_TILE, n_out), lambda i: (i, 0)),
        out_shape=jax.ShapeDtypeStruct((n, n_out), F32),
        compiler_params=_params("parallel"),
        name="inproj",
    )(x, norm_w.reshape(1, d), w, *after)


def _decay_tables():
    c = CHUNK
    a = np.zeros((LEVELS + 2, c, c), np.float32)
    t = np.arange(c)
    for l in range(LEVELS):
        s = 1 << l
        for r in range(c):
            mid = (r // (2 * s)) * 2 * s + s - 1
            if (r >> l) & 1:
                a[l, r] = (t > mid) & (t <= r)
            else:
                a[l, r] = (t > r) & (t <= mid)
    a[LEVELS] = t[None, :] <= t[:, None]
    a[LEVELS + 1] = t[None, :] > t[:, None]
    a = a.reshape((LEVELS + 2) * c, c)
    a3 = np.concatenate([a, a], axis=1)
    i, j = np.meshgrid(t, t, indexing="ij")
    x = np.maximum(i ^ j, 1)
    lvl = np.where(j < i, np.floor(np.log2(x)).astype(np.int32), np.where(i == j, LEVELS, -1))
    return jnp.asarray(a3, BF16), jnp.asarray(lvl, jnp.int32)


def _decay_factors(a3, g):
    g_hi = g.astype(BF16)
    g_lo = (g - g_hi.astype(F32)).astype(BF16)
    return jnp.exp(_dot(a3, jnp.concatenate([g_hi, g_lo], axis=0)))


def _gla_chunk(q, k, v, f, st_ref, lvl, rowid):
    c = CHUNK
    sc = jnp.where(lvl == LEVELS, jnp.sum(q * k, axis=-1, keepdims=True), 0.0)
    for l in range(LEVELS):
        s = 1 << l
        if s % 8 == 0:
            qk = jnp.concatenate([(q if b & 1 else k)[b * s:(b + 1) * s] for b in range(c // s)], axis=0)
        else:
            qk = jnp.where(((rowid >> l) & 1) == 1, q, k)
        m = (qk * f[l * c:(l + 1) * c]).astype(BF16)
        sc = jnp.where(lvl == l, _dot_nt(m, m), sc)
    qd = (q * f[LEVELS * c:(LEVELS + 1) * c]).astype(BF16)
    kd = (k * f[(LEVELS + 1) * c:(LEVELS + 2) * c]).astype(BF16)
    vb = v.astype(BF16)
    st = st_ref[...]
    o = _dot(sc.astype(BF16), vb) + _dot_nt(qd, st.astype(BF16))
    chunk_decay = f[(LEVELS + 1) * c - 1:(LEVELS + 1) * c, :]
    st_ref[...] = st * chunk_decay + _dot_tn(vb, kd)
    return o


def _hgrn_kernel(q_ref, f_ref, i_ref, g_ref, lb_ref, on_ref, a3_ref, lvl_ref, o_ref, st_ref):
    @pl.when(pl.program_id(1) == 0)
    def _():
        st_ref[...] = jnp.zeros_like(st_ref)

    lvl = lvl_ref[...]
    rowid = lax.broadcasted_iota(jnp.int32, (CHUNK, 1), 0)
    lb = lb_ref[...]
    for c in range(GLA_STEP_CHUNKS):
        rows = slice(c * CHUNK, (c + 1) * CHUNK)
        fl = f_ref[rows, :]
        f = _decay_factors(a3_ref[...], jnp.log(lb + (1.0 - lb) * jax.nn.sigmoid(fl)))
        key_in = (1.0 - lb) * jax.nn.sigmoid(-fl)
        for h in range(HEADS):
            sl = slice(h * DK, (h + 1) * DK)
            q = _silu(q_ref[rows, sl]) * (DK ** -0.5)
            o = _gla_chunk(q, key_in[:, sl], i_ref[rows, sl], f[:, sl], st_ref.at[h], lvl, rowid)
            o = _rms(o, on_ref[...])
            o_ref[rows, sl] = (o * _silu(g_ref[rows, sl])).astype(BF16)


def _hgrn(proj, lb, o_norm, batch, seq):
    n = proj.shape[0]
    rows = CHUNK * GLA_STEP_CHUNKS
    nt = seq // rows
    a3, lvl = _decay_tables()
    col = lambda j: pl.BlockSpec((rows, HG_WIDTH), lambda b, t: (b * nt + t, j))
    const = lambda shape: pl.BlockSpec(shape, lambda b, t: (0, 0))
    return pl.pallas_call(
        _hgrn_kernel,
        grid=(batch, nt),
        in_specs=[col(0), col(1), col(2), col(3), const((1, HG_WIDTH)), const((1, DK)),
                  const(a3.shape), const(lvl.shape)],
        out_specs=pl.BlockSpec((rows, HG_WIDTH), lambda b, t: (b * nt + t, 0)),
        out_shape=jax.ShapeDtypeStruct((n, HG_WIDTH), BF16),
        scratch_shapes=[pltpu.VMEM((HEADS, DK, DK), F32)],
        compiler_params=_params("parallel", "arbitrary"),
        name="hgrn",
    )(proj, proj, proj, proj, lb.reshape(1, HG_WIDTH), o_norm.reshape(1, DK), a3, lvl)


def _log_sigmoid(x):
    return jnp.minimum(x, 0.0) - jnp.log1p(jnp.exp(-jnp.abs(x)))


def _gla_kernel(q_ref, k_ref, v_ref, g_ref, a_ref, wa_ref, ba_ref, on_ref, a3_ref, lvl_ref, o_ref, st_ref):
    @pl.when(pl.program_id(1) == 0)
    def _():
        st_ref[...] = jnp.zeros_like(st_ref)

    lvl = lvl_ref[...]
    rowid = lax.broadcasted_iota(jnp.int32, (CHUNK, 1), 0)
    for c in range(GLA_STEP_CHUNKS):
        rows = slice(c * CHUNK, (c + 1) * CHUNK)
        gate_logit = _dot(a_ref[rows, :].astype(BF16), wa_ref[...]) + ba_ref[...]
        f = _decay_factors(a3_ref[...], _log_sigmoid(gate_logit) * (1.0 / GLA_TAU))
        for h in range(HEADS):
            sl = slice(h * DK, (h + 1) * DK)
            sv = slice(h * GLA_DV, (h + 1) * GLA_DV)
            q = q_ref[rows, sl] * (DK ** -0.5)
            o = _gla_chunk(q, k_ref[rows, sl], v_ref[rows, sv], f[:, sl], st_ref.at[h], lvl, rowid)
            o = _rms(o, on_ref[...])
            o_ref[rows, sv] = (o * _silu(g_ref[rows, sv])).astype(BF16)


def _gla(proj, w_alpha, b_alpha, o_norm, batch, seq):
    n = proj.shape[0]
    rows = CHUNK * GLA_STEP_CHUNKS
    nt = seq // rows
    qk_w = HEADS * DK
    v_w = HEADS * GLA_DV
    a3, lvl = _decay_tables()
    row = lambda b, t: b * nt + t
    const = lambda shape: pl.BlockSpec(shape, lambda b, t: (0, 0))
    return pl.pallas_call(
        _gla_kernel,
        grid=(batch, nt),
        in_specs=[pl.BlockSpec((rows, qk_w), lambda b, t: (row(b, t), 0)),
                  pl.BlockSpec((rows, qk_w), lambda b, t: (row(b, t), 1)),
                  pl.BlockSpec((rows, v_w), lambda b, t: (row(b, t), 1)),
                  pl.BlockSpec((rows, v_w), lambda b, t: (row(b, t), 2)),
                  pl.BlockSpec((rows, GLA_RANK_PAD), lambda b, t: (row(b, t), (2 * qk_w + 2 * v_w) // GLA_RANK_PAD)),
                  const((GLA_RANK_PAD, qk_w)), const((1, qk_w)), const((1, GLA_DV)),
                  const(a3.shape), const(lvl.shape)],
        out_specs=pl.BlockSpec((rows, v_w), lambda b, t: (row(b, t), 0)),
        out_shape=jax.ShapeDtypeStruct((n, v_w), BF16),
        scratch_shapes=[pltpu.VMEM((HEADS, GLA_DV, DK), F32)],
        compiler_params=_params("parallel", "arbitrary"),
        name="gla",
    )(proj, proj, proj, proj, proj, w_alpha, b_alpha.reshape(1, qk_w), o_norm.reshape(1, GLA_DV), a3, lvl)


def _qk_norm(x, w2):
    lane = lax.broadcasted_iota(jnp.int32, x.shape, 1)
    first = lane < DA_HEAD_DIM
    sq = x * x
    ms1 = jnp.sum(jnp.where(first, sq, 0.0), axis=-1, keepdims=True) * (1.0 / DA_HEAD_DIM)
    ms2 = jnp.sum(jnp.where(first, 0.0, sq), axis=-1, keepdims=True) * (1.0 / DA_HEAD_DIM)
    r = jnp.where(first, lax.rsqrt(ms1 + EPS), lax.rsqrt(ms2 + EPS))
    return x * r * w2


def _score_bound(q_norm_w, k_norm_w):
    return (jnp.max(jnp.abs(q_norm_w)) * jnp.max(jnp.abs(k_norm_w))
            * (DA_HEAD_DIM * DA_HEAD_DIM ** -0.5 * LOG2_E * SCORE_BOUND_MARGIN))


def _attn_kernel(qi_tab, ki_tab, q_ref, k_ref, v_ref, qn_ref, kn_ref, lam_ref, sub_ref, o_ref,
                 qs_ref, m_ref, l_ref, acc_ref, *, lam_init, bounded):
    p = pl.program_id(2)
    qi = qi_tab[p]
    ki = ki_tab[p]
    tq = ATTN_BLOCK
    dv = 2 * DA_HEAD_DIM
    heads = [slice(h * dv, (h + 1) * dv) for h in range(ATTN_HEADS)]

    @pl.when(ki == 0)
    def _():
        for h, sl in enumerate(heads):
            qn = _qk_norm(q_ref[:, sl], qn_ref[...])
            qt = (qn * (DA_HEAD_DIM ** -0.5 * LOG2_E)).T
            top = lax.broadcasted_iota(jnp.int32, qt.shape, 0) < DA_HEAD_DIM
            q1 = jnp.where(top, qt, 0.0).astype(BF16)
            q2 = jnp.where(top, 0.0, qt).astype(BF16)
            for copy in range(1 if bounded else 2):
                qs_ref[h, copy, :, :tq] = q1
                qs_ref[h, copy, :, tq:] = q2
            if bounded:
                bound = _score_bound(qn_ref[...], kn_ref[...])
                row = lax.broadcasted_iota(jnp.int32, (dv, 2 * tq), 0)
                qs_ref[h, 1] = jnp.where(row == 0, -bound, 0.0).astype(BF16)
        if not bounded:
            m_ref[...] = jnp.full_like(m_ref, -jnp.inf)
        l_ref[...] = jnp.zeros_like(l_ref)
        acc_ref[...] = jnp.zeros_like(acc_ref)

    def step(masked):
        tk = k_ref.shape[0]
        ones = jnp.ones((16, tk), BF16)
        unit = jnp.where(lax.broadcasted_iota(jnp.int32, (tk, dv), 1) == 0, 1.0, 0.0).astype(BF16)
        for h, sl in enumerate(heads):
            kn = _qk_norm(k_ref[:, sl], kn_ref[...]).astype(BF16)

            def mask(s):
                if not masked:
                    return s
                key = lax.broadcasted_iota(jnp.int32, s.shape, 0)
                qry = lax.broadcasted_iota(jnp.int32, s.shape, 1)
                qry = jnp.where(qry >= tq, qry - tq, qry)
                return jnp.where(key <= qry, s, -jnp.inf)

            if bounded:
                shifted = mask(_dot(jnp.concatenate([kn, unit], axis=1), qs_ref[h].reshape(2 * dv, 2 * tq)))
                alpha = 1.0
            else:
                m_prev = m_ref[h]
                m_new = jnp.maximum(m_prev, jnp.max(mask(_dot(kn, qs_ref[h, 0])), axis=0, keepdims=True))
                alpha = jnp.exp2(m_prev - m_new)
                shifted = mask(_dot(kn, qs_ref[h, 1])) - m_new
                m_ref[h] = m_new
            pr = jnp.exp2(shifted).astype(BF16) if bounded else jnp.exp2(shifted.astype(BF16))
            vt = jnp.concatenate([v_ref[:, sl].T.astype(BF16), ones], axis=0)
            pv = _dot(vt, pr)
            l_ref[h] = alpha * l_ref[h] + pv[dv:dv + 1]
            acc_ref[h] = alpha * acc_ref[h] + pv[:dv]

    @pl.when(ki < qi)
    def _():
        step(False)

    @pl.when(ki == qi)
    def _():
        step(True)
        lv = lam_ref[...]
        lam = (jnp.exp(jnp.sum(lv[0:1] * lv[1:2], axis=-1, keepdims=True))
               - jnp.exp(jnp.sum(lv[2:3] * lv[3:4], axis=-1, keepdims=True)) + lam_init)
        for h, sl in enumerate(heads):
            o1 = acc_ref[h, :, :tq] / l_ref[h, :, :tq]
            o2 = acc_ref[h, :, tq:] / l_ref[h, :, tq:]
            o = o1 - lam * o2
            o = o * lax.rsqrt(jnp.mean(o * o, axis=0, keepdims=True) + EPS) * sub_ref[...] * (1.0 - lam_init)
            o_ref[:, sl] = o.T.astype(BF16)


def _diff_attn(proj, q_norm, k_norm, lam_vecs, subln, lam_init, batch, seq):
    n = proj.shape[0]
    blk = ATTN_BLOCK
    nq = seq // blk
    dv = 2 * DA_HEAD_DIM
    pairs = [(qi, ki) for qi in range(nq) for ki in range(qi + 1)]
    qi_tab = jnp.asarray([p[0] for p in pairs], jnp.int32)
    ki_tab = jnp.asarray([p[1] for p in pairs], jnp.int32)
    width = ATTN_HEADS * dv
    groups = HEADS // ATTN_HEADS
    q_col = (4 * HG_WIDTH) // width
    const = lambda shape: pl.BlockSpec(shape, lambda b, g, p, qt, kt: (0, 0))
    grid_spec = pltpu.PrefetchScalarGridSpec(
        num_scalar_prefetch=2,
        grid=(batch, groups, len(pairs)),
        in_specs=[pl.BlockSpec((blk, width), lambda b, g, p, qt, kt: (b * nq + qt[p], q_col + g)),
                  pl.BlockSpec((blk, width), lambda b, g, p, qt, kt: (b * nq + kt[p], q_col + groups + g)),
                  pl.BlockSpec((blk, width), lambda b, g, p, qt, kt: (b * nq + kt[p], q_col + 2 * groups + g)),
                  const((1, dv)), const((1, dv)), const((4, DA_HEAD_DIM)), const((dv, 1))],
        out_specs=pl.BlockSpec((blk, width), lambda b, g, p, qt, kt: (b * nq + qt[p], g)),
        scratch_shapes=[pltpu.VMEM((ATTN_HEADS, 2, dv, 2 * blk), BF16),
                        pltpu.VMEM((ATTN_HEADS, 1, 2 * blk), F32),
                        pltpu.VMEM((ATTN_HEADS, 1, 2 * blk), F32),
                        pltpu.VMEM((ATTN_HEADS, dv, 2 * blk), F32)])
    args = (qi_tab, ki_tab, proj, proj, proj, jnp.tile(q_norm, 2).reshape(1, -1), jnp.tile(k_norm, 2).reshape(1, -1),
            lam_vecs, subln.reshape(-1, 1))

    def run(bounded):
        return pl.pallas_call(
            functools.partial(_attn_kernel, lam_init=lam_init, bounded=bounded),
            grid_spec=grid_spec,
            out_shape=jax.ShapeDtypeStruct((n, HEADS * dv), BF16),
            compiler_params=_params("parallel", "parallel", "arbitrary"),
            name="diff_attn_bounded" if bounded else "diff_attn",
        )(*args)

    return lax.cond(_score_bound(q_norm, k_norm) <= MAX_SCORE_BOUND, lambda: run(True), lambda: run(False))


def _outproj_router_kernel(*refs, n_a):
    a_refs = refs[:n_a]
    (w_ref, x_ref, nw_ref, wr_hi_ref, wr_lo_ref, br_ref, tri_ref,
     x2_ref, meta_ref, rec_ref, cnt_ref, carry_ref) = refs[n_a:]

    @pl.when(pl.program_id(0) == 0)
    def _():
        carry_ref[...] = jnp.zeros_like(carry_ref)

    x2 = x_ref[...]
    off = 0
    for a_ref in a_refs:
        kk = a_ref.shape[1]
        x2 = x2 + _dot(a_ref[...], w_ref[off:off + kk, :])
        off += kk
    x2_ref[...] = x2

    h = _rms(x2, nw_ref[...])
    h_hi = h.astype(BF16)
    h_lo = (h - h_hi.astype(F32)).astype(BF16)
    wr_hi = wr_hi_ref[...]
    by_hi = _dot_nt(jnp.concatenate([wr_hi, wr_lo_ref[...]], axis=0), h_hi)
    lt = by_hi[:ROUTER_ROWS] + by_hi[ROUTER_ROWS:] + _dot_nt(wr_hi, h_lo) + br_ref[...]
    row = lambda r: lt[r:r + 1, :]

    gl = [row(g) for g in range(N_GROUPS)]
    gmax = functools.reduce(jnp.maximum, gl)
    gsel = jnp.where(gl[0] == gmax, 0.0, jnp.where(gl[1] == gmax, 1.0, jnp.where(gl[2] == gmax, 2.0, 3.0)))
    g_w = 1.0 / functools.reduce(jnp.add, [jnp.exp(x - gmax) for x in gl])

    el = []
    for e in range(EXPERTS_PER_GROUP):
        v = row(N_GROUPS + e)
        for g in range(1, N_GROUPS):
            v = jnp.where(gsel == float(g), row(N_GROUPS + EXPERTS_PER_GROUP * g + e), v)
        el.append(v)
    emax = functools.reduce(jnp.maximum, el)
    sel = []
    for e in range(EXPERTS_PER_GROUP):
        ahead = jnp.zeros_like(emax)
        for o in range(EXPERTS_PER_GROUP):
            if o < e:
                ahead = ahead + jnp.where(el[o] >= el[e], 1.0, 0.0)
            elif o > e:
                ahead = ahead + jnp.where(el[o] > el[e], 1.0, 0.0)
        sel.append(ahead < 2.0)
    pe = [jnp.where(sel[e], jnp.exp(el[e] - emax), 0.0) for e in range(EXPERTS_PER_GROUP)]
    scale = g_w / functools.reduce(jnp.add, pe)
    zero = jnp.zeros_like(emax)
    seen = zero
    lo_idx, hi_idx, c_lo, c_hi = zero, zero, zero, zero
    for e in range(EXPERTS_PER_GROUP):
        is_lo = sel[e] & (seen == 0.0)
        is_hi = sel[e] & (seen == 1.0)
        lo_idx = jnp.where(is_lo, float(e), lo_idx)
        hi_idx = jnp.where(is_hi, float(e), hi_idx)
        c_lo = jnp.where(is_lo, pe[e] * scale, c_lo)
        c_hi = jnp.where(is_hi, pe[e] * scale, c_hi)
        seen = seen + jnp.where(sel[e], 1.0, 0.0)
    pair = zero
    for p in range(1, N_PAIRS):
        pair = jnp.where((lo_idx == float(PAIR_LO[p])) & (hi_idx == float(PAIR_HI[p])), float(p), pair)
    bucket = gsel * float(N_PAIRS) + pair

    bid = lax.broadcasted_iota(jnp.int32, (ROUTER_ROWS, bucket.shape[1]), 0).astype(F32)
    onehot = jnp.where(bid == bucket, 1.0, 0.0)
    before = _dot(onehot.astype(BF16), tri_ref[...]) + carry_ref[...]
    rank = jnp.sum(onehot * before, axis=0, keepdims=True)
    carry = carry_ref[...] + jnp.sum(onehot, axis=1, keepdims=True)
    carry_ref[...] = carry
    cnt_ref[...] = carry
    meta = jnp.concatenate([bucket, rank, c_lo, c_hi, zero, zero, zero, zero], axis=0)
    meta_ref[...] = meta
    rec_ref[...] = jnp.concatenate([meta, jnp.zeros((LANE - meta.shape[0], meta.shape[1]), F32)], axis=0).T


def _outproj_router(a_list, w_out, x, n, norm_w, w_group, b_group, w_router, b_router):
    d = D_MODEL
    tm = ROUTER_TILE
    tok = lambda: pl.BlockSpec((tm, d), lambda i: (i, 0))
    wr = jnp.concatenate([w_group, w_router.transpose(1, 0, 2).reshape(d, N_EXPERTS)], axis=1)
    wr_t = jnp.zeros((ROUTER_ROWS, d), F32).at[:wr.shape[1]].set(wr.T)
    wr_hi = wr_t.astype(BF16)
    wr_lo = (wr_t - wr_hi.astype(F32)).astype(BF16)
    br = jnp.zeros((ROUTER_ROWS, 1), F32).at[:wr.shape[1], 0].set(
        jnp.concatenate([b_group, b_router.reshape(-1)]))
    tri = jnp.asarray(np.triu(np.ones((tm, tm), np.float32), 1), BF16)
    const = lambda shape: pl.BlockSpec(shape, lambda i: (0, 0))
    in_specs = [pl.BlockSpec((tm, a.shape[1]), lambda i: (i, 0)) for a in a_list]
    in_specs += [const((d, d)), tok(), const((1, d)),
                 const((ROUTER_ROWS, d)), const((ROUTER_ROWS, d)), const((ROUTER_ROWS, 1)), const((tm, tm))]
    return pl.pallas_call(
        functools.partial(_outproj_router_kernel, n_a=len(a_list)),
        grid=(n // tm,),
        in_specs=in_specs,
        out_specs=[tok(),
                   pl.BlockSpec((8, tm), lambda i: (0, i)),
                   pl.BlockSpec((tm, LANE), lambda i: (i, 0)),
                   const((ROUTER_ROWS, 1))],
        out_shape=[jax.ShapeDtypeStruct((n, d), F32),
                   jax.ShapeDtypeStruct((8, n), F32),
                   jax.ShapeDtypeStruct((n, LANE), F32),
                   jax.ShapeDtypeStruct((ROUTER_ROWS, 1), F32)],
        scratch_shapes=[pltpu.VMEM((ROUTER_ROWS, 1), F32)],
        compiler_params=_params("arbitrary"),
        name="outproj_router",
    )(*a_list, w_out, x, norm_w.reshape(1, d), wr_hi, wr_lo, br, tri)


def _gather_tokens(table, idx):
    m = idx.shape[0]
    info = plsc.get_sparse_core_info()
    n_workers = info.num_cores * info.num_subcores
    per_worker = m // n_workers
    n_chunks = per_worker // SC_CHUNK
    assert per_worker * n_workers == m and n_chunks * SC_CHUNK == per_worker and n_chunks % 2 == 0
    mesh = plsc.VectorSubcoreMesh(core_axis_name="core", subcore_axis_name="subcore")
    rows = pltpu.VMEM((SC_CHUNK,) + table.shape[1:], table.dtype)

    @functools.partial(
        pl.kernel, mesh=mesh,
        out_type=jax.ShapeDtypeStruct((m,) + table.shape[1:], table.dtype),
        scratch_types=[pltpu.VMEM((n_chunks, SC_CHUNK), jnp.int32), rows, rows,
                       pltpu.SemaphoreType.DMA, pltpu.SemaphoreType.DMA],
        compiler_params=pltpu.CompilerParams(use_tc_tiling_on_sc=True),
        name="gather_tokens")
    def gather(table_hbm, idx_hbm, out_hbm, idx_v, rows_a, rows_b, sem_a, sem_b):
        worker = lax.axis_index("subcore") * info.num_cores + lax.axis_index("core")
        bufs = ((rows_a, sem_a), (rows_b, sem_b))

        def fetch(j, buf):
            return pltpu.make_async_copy(table_hbm.at[idx_v.at[j]], bufs[buf][0], bufs[buf][1])

        pltpu.sync_copy(idx_hbm.at[worker], idx_v)
        fetch(0, 0).start()

        @pl.loop(0, n_chunks, step=2)
        def _(j0):
            for buf in range(2):
                j = j0 + buf
                fetch(j, buf).wait()

                @pl.when(j + 1 < n_chunks)
                def _():
                    fetch(j + 1, 1 - buf).start()

                off = pl.multiple_of(worker * per_worker + j * SC_CHUNK, SC_CHUNK)
                pltpu.sync_copy(bufs[buf][0], out_hbm.at[pl.ds(off, SC_CHUNK)])

    return gather(table, idx.reshape(n_workers, n_chunks, SC_CHUNK))


def _scatter_tokens(arrays, idx, m):
    n = idx.shape[0]
    na = len(arrays)
    info = plsc.get_sparse_core_info()
    n_workers = info.num_cores * info.num_subcores
    per_worker = n // n_workers
    n_chunks = per_worker // SC_CHUNK
    assert per_worker * n_workers == n and n_chunks * SC_CHUNK == per_worker and n_chunks % 2 == 0
    mesh = plsc.VectorSubcoreMesh(core_axis_name="core", subcore_axis_name="subcore")
    bufs = [pltpu.VMEM((SC_CHUNK,) + a.shape[1:], a.dtype) for a in arrays for _ in range(2)]

    @functools.partial(
        pl.kernel, mesh=mesh,
        out_type=[jax.ShapeDtypeStruct((m,) + a.shape[1:], a.dtype) for a in arrays],
        scratch_types=[pltpu.VMEM((n_chunks, SC_CHUNK), jnp.int32)] + bufs + [pltpu.SemaphoreType.DMA] * (2 * na),
        compiler_params=pltpu.CompilerParams(use_tc_tiling_on_sc=True),
        name="scatter_tokens")
    def scatter(*refs):
        ins, idx_hbm, outs = refs[:na], refs[na], refs[na + 1:2 * na + 1]
        idx_v = refs[2 * na + 1]
        buf_v = refs[2 * na + 2:4 * na + 2]
        sem_v = refs[4 * na + 2:]
        worker = lax.axis_index("subcore") * info.num_cores + lax.axis_index("core")

        def load(a, j, b):
            off = pl.multiple_of(worker * per_worker + j * SC_CHUNK, SC_CHUNK)
            return pltpu.make_async_copy(ins[a].at[pl.ds(off, SC_CHUNK)], buf_v[2 * a + b], sem_v[2 * a + b])

        pltpu.sync_copy(idx_hbm.at[worker], idx_v)
        for a in range(na):
            load(a, 0, 0).start()

        @pl.loop(0, n_chunks, step=2)
        def _(j0):
            for b in range(2):
                j = j0 + b
                for a in range(na):
                    load(a, j, b).wait()

                @pl.when(j + 1 < n_chunks)
                def _():
                    for a in range(na):
                        load(a, j + 1, 1 - b).start()

                for a in range(na):
                    pltpu.sync_copy(buf_v[2 * a + b], outs[a].at[idx_v.at[j]])

    return scatter(*arrays, idx.reshape(n_workers, n_chunks, SC_CHUNK))


def _moe_kernel(step_ids, e_lo, e_hi, n_valid, n_tiles, x_hbm, rec_ref, nw_ref, *refs):
    del step_ids, e_lo, e_hi
    s = pl.program_id(0)
    o_ref, x_ring, x_sem = refs[-3:]
    step_rows = MOE_STEP_TILES * MOE_TILE
    n_steps = (n_tiles[0] + MOE_STEP_TILES - 1) // MOE_STEP_TILES
    ahead = MOE_X_SLOTS - 1

    def x_copy(t):
        slot = t % MOE_X_SLOTS
        src = x_hbm.at[pl.ds(pl.multiple_of(t * step_rows, step_rows), step_rows), :]
        return pltpu.make_async_copy(src, x_ring.at[slot], x_sem.at[slot])

    @pl.when(s == 0)
    def _():
        for t in range(ahead):
            pl.when(t < n_steps)(lambda t=t: x_copy(t).start())

    pl.when(s + ahead < n_steps)(lambda: x_copy(s + ahead).start())

    @pl.when(s < n_steps)
    def _():
        x_copy(s).wait()
        x_ref = x_ring.at[s % MOE_X_SLOTS]
        for i in range(MOE_STEP_TILES):
            g_lo_ref, g_hi_ref, u_lo_ref, u_hi_ref, d_lo_ref, d_hi_ref = refs[6 * i:6 * (i + 1)]
            rows = pl.ds(i * MOE_TILE, MOE_TILE)
            valid = lax.broadcasted_iota(jnp.int32, (MOE_TILE, 1), 0) < n_valid[s * MOE_STEP_TILES + i]
            x = jnp.where(valid, x_ref[rows, :], 0.0)
            h = _rms(x, nw_ref[...]).astype(BF16)
            rec = jnp.where(valid, rec_ref[rows, :], 0.0)

            def expert(g_ref, u_ref, d_ref):
                act = _silu(_dot(h, g_ref[0])) * _dot(h, u_ref[0])
                return _dot(act.astype(BF16), d_ref[0])

            y = (rec[:, 2:3] * expert(g_lo_ref, u_lo_ref, d_lo_ref)
                 + rec[:, 3:4] * expert(g_hi_ref, u_hi_ref, d_hi_ref))
            o_ref[rows, :] = jnp.where(valid, x_ref[rows, :], 0.0) + y


def _moe(x2, n, meta, rec, cnt, norm_w, w_gate, w_up, w_down, layer, w_down_bf16=None):
    d = D_MODEL
    tm = MOE_TILE
    max_tiles = n // tm + N_BUCKETS
    n_slots = max_tiles * tm
    counts = cnt[:N_BUCKETS, 0].astype(jnp.int32)
    tiles_per_bucket = (counts + tm - 1) // tm
    tile_end = jnp.cumsum(tiles_per_bucket)
    tile_start = tile_end - tiles_per_bucket
    n_tiles = tile_end[-1]
    in_bucket = meta[0][:, None] == jnp.arange(N_BUCKETS, dtype=F32)[None, :]
    dest = meta[1].astype(jnp.int32) + jnp.sum(jnp.where(in_bucket, tile_start[None, :] * tm, 0), axis=1)
    x_sorted, rec_sorted = _scatter_tokens([x2, rec], dest, n_slots)
    first = layer * N_EXPERTS
    w_gate, w_up = (_to_bf16(w, first=first, count=N_EXPERTS) for w in (w_gate, w_up))
    w_down = _to_bf16(w_down, first=first, count=N_EXPERTS) if w_down_bf16 is None else w_down_bf16

    k = MOE_STEP_TILES
    tile_raw = jnp.arange(max_tiles, dtype=jnp.int32)
    tile_ids = jnp.minimum(tile_raw, n_tiles - 1)
    step_ids = jnp.minimum(jnp.arange(max_tiles // k, dtype=jnp.int32), (n_tiles - 1) // k)
    of_bucket = lambda table: jnp.sum(jnp.where(tile_bucket[:, None] == jnp.arange(N_BUCKETS)[None, :], table[None, :], 0), axis=1)
    tile_bucket = jnp.sum((tile_ids[:, None] >= tile_end[None, :]).astype(jnp.int32), axis=1)
    n_valid = jnp.clip(of_bucket(counts) - (tile_ids - of_bucket(tile_start)) * tm, 0, tm)
    n_valid = jnp.where(tile_raw < n_tiles, n_valid, 0)
    group = tile_bucket // N_PAIRS
    pair = tile_bucket % N_PAIRS
    pair_lo = jnp.sum(jnp.where(pair[:, None] == jnp.arange(N_PAIRS)[None, :], jnp.asarray(PAIR_LO, jnp.int32)[None, :], 0), axis=1)
    pair_hi = jnp.sum(jnp.where(pair[:, None] == jnp.arange(N_PAIRS)[None, :], jnp.asarray(PAIR_HI, jnp.int32)[None, :], 0), axis=1)
    e_lo = group * EXPERTS_PER_GROUP + pair_lo
    e_hi = group * EXPERTS_PER_GROUP + pair_hi

    step_rows = lambda s, sid, lo, hi, nv, nt: (sid[s], 0)
    weight_specs, weight_args = [], []
    for i in range(k):
        wlo = lambda s, sid, lo, hi, nv, nt, i=i: (lo[s * k + i], 0, 0)
        whi = lambda s, sid, lo, hi, nv, nt, i=i: (hi[s * k + i], 0, 0)
        weight_specs += [pl.BlockSpec((1, d, D_EXPERT), wlo), pl.BlockSpec((1, d, D_EXPERT), whi),
                         pl.BlockSpec((1, d, D_EXPERT), wlo), pl.BlockSpec((1, d, D_EXPERT), whi),
                         pl.BlockSpec((1, D_EXPERT, d), wlo), pl.BlockSpec((1, D_EXPERT, d), whi)]
        weight_args += [w_gate, w_gate, w_up, w_up, w_down, w_down]
    grid_spec = pltpu.PrefetchScalarGridSpec(
        num_scalar_prefetch=5,
        grid=(max_tiles // k,),
        in_specs=[pl.BlockSpec(memory_space=pl.ANY), pl.BlockSpec((k * tm, LANE), step_rows),
                  pl.BlockSpec((1, d), lambda s, sid, lo, hi, nv, nt: (0, 0))] + weight_specs,
        out_specs=pl.BlockSpec((k * tm, d), step_rows),
        scratch_shapes=[pltpu.VMEM((MOE_X_SLOTS, k * tm, d), F32), pltpu.SemaphoreType.DMA((MOE_X_SLOTS,))])
    out_sorted = pl.pallas_call(
        _moe_kernel,
        grid_spec=grid_spec,
        out_shape=jax.ShapeDtypeStruct((n_slots, d), F32),
        compiler_params=_params("arbitrary"),
        name="moe",
    )(step_ids, e_lo, e_hi, n_valid, n_tiles.reshape(1), x_sorted, rec_sorted, norm_w.reshape(1, d), *weight_args)
    return _gather_tokens(out_sorted, dest)


def kernel(x, attn_norm, ffn_norm, even_w_in, even_w_out, hg_lb_logits, hg_o_norm, da_q_norm, da_k_norm, da_lambda, da_subln, odd_w_in, gla_w_alpha, gla_b_alpha, gla_o_norm, odd_w_out, moe_w_group, moe_b_group, moe_w_router, moe_b_router, moe_w_gate, moe_w_up, moe_w_down):
    batch, seq, d = x.shape
    depth = attn_norm.shape[0]
    n = batch * seq
    xf = x.reshape(n, d)

    lb_sm = jax.nn.softmax(hg_lb_logits.astype(F32), axis=0)
    lb_all = jnp.cumsum(lb_sm, axis=0) - lb_sm[0:1]
    rank = gla_w_alpha.shape[1]

    experts = lambda w: w.reshape((-1,) + w.shape[2:])
    moe_w_gate, moe_w_up, moe_w_down = experts(moe_w_gate), experts(moe_w_up), experts(moe_w_down)

    down_ahead = None
    ahead = ()
    for layer in range(depth):
        j = layer // 2
        if layer % 2 == 0:
            lam_init = 0.8 - 0.6 * math.exp(-0.3 * layer)
            proj = _inproj(xf, n, attn_norm[layer], _to_bf16(even_w_in, first=j, count=1)[0], after=ahead)
            o_a = _hgrn(proj, lb_all[j], hg_o_norm[j], batch, seq)
            o_b = _diff_attn(proj, da_q_norm[j], da_k_norm[j], da_lambda[j], da_subln[j], lam_init, batch, seq)
            mix = [o_a, o_b]
            w_out = _to_bf16(even_w_out, first=j, count=1)[0]
        else:
            w_alpha = jnp.pad(gla_w_alpha[j], ((0, GLA_RANK_PAD - rank), (0, 0))).astype(BF16)
            w_in = _to_bf16(jnp.swapaxes(odd_w_in, 1, 2), odd_w_in.shape[2] + GLA_RANK_PAD - rank, first=j, count=1)[0]
            proj = _inproj(xf, n, attn_norm[layer], w_in, w_transposed=True, after=ahead)
            mix = [_gla(proj, w_alpha, gla_b_alpha[j], gla_o_norm[j], batch, seq)]
            w_out = _to_bf16(odd_w_out, first=j, count=1)[0]
        x2, meta, rec, cnt = _outproj_router(mix, w_out, xf, n, ffn_norm[layer], moe_w_group[layer],
                                        moe_b_group[layer], moe_w_router[layer], moe_b_router[layer])
        xf = _moe(x2, n, meta, rec, cnt, ffn_norm[layer], moe_w_gate, moe_w_up, moe_w_down, layer, down_ahead)
        down_ahead = (_to_bf16(moe_w_down, first=(layer + 1) * N_EXPERTS, count=N_EXPERTS)
                      if layer + 1 < depth else None)
        ahead = () if down_ahead is None else (down_ahead,)
    return xf.reshape(batch, seq, d)
```

```python
import functools
import math

import numpy as np
import jax
import jax.numpy as jnp
from jax import lax
from jax.experimental import pallas as pl
from jax.experimental.pallas import tpu as pltpu
from jax.experimental.pallas import tpu_sc as plsc

F32 = jnp.float32
BF16 = jnp.bfloat16
EPS = 1e-6
LOG2_E = 1.4426950408889634

D_MODEL = 1024
LANE = 128
HEADS = 4
DK = 128
HG_WIDTH = 512
DA_HEAD_DIM = 64
GLA_DV = 256
GLA_RANK_PAD = 128
GLA_TAU = 16.0
N_GROUPS = 4
EXPERTS_PER_GROUP = 4
N_EXPERTS = 16
D_EXPERT = 512
N_PAIRS = 6
N_BUCKETS = N_GROUPS * N_PAIRS
ROUTER_ROWS = 32

CHUNK = 128
GLA_STEP_CHUNKS = 4
LEVELS = 7
ATTN_BLOCK = 512
ATTN_HEADS = 4
SCORE_BOUND_MARGIN = 1.05
MAX_SCORE_BOUND = 60.0
ROW_TILE = 1024
ROUTER_TILE = 1024
MOE_TILE = 256
MOE_STEP_TILES = 2
SC_CHUNK = 32
VMEM_LIMIT = 56 * 1024 * 1024
CAST_BLOCK_BYTES = 8 * 1024 * 1024

PAIR_LO = (0, 0, 1, 1, 0, 2)
PAIR_HI = (1, 2, 2, 3, 3, 3)


def _dot(a, b):
    return jnp.dot(a, b, preferred_element_type=F32)


def _dot_nt(a, b):
    return lax.dot_general(a, b, (((1,), (1,)), ((), ())), preferred_element_type=F32)


def _dot_tn(a, b):
    return lax.dot_general(a, b, (((0,), (0,)), ((), ())), preferred_element_type=F32)


def _rms(x, w):
    return x * lax.rsqrt(jnp.mean(x * x, axis=-1, keepdims=True) + EPS) * w


def _silu(x):
    return x * jax.nn.sigmoid(x)


def _params(*sem):
    return pltpu.CompilerParams(dimension_semantics=sem, vmem_limit_bytes=VMEM_LIMIT)


def _cast_kernel(w_ref, o_ref):
    k = w_ref.shape[1]
    o_ref[:, :k, :] = w_ref[...].astype(BF16)
    if o_ref.shape[1] > k:
        o_ref[:, k:, :] = jnp.zeros((o_ref.shape[0], o_ref.shape[1] - k, o_ref.shape[2]), BF16)


def _to_bf16(w, k_pad=None, first=0, count=None):
    _, k, n = w.shape
    l = count or w.shape[0]
    k_pad = k_pad or k
    per_matrix = k * n * 4
    lb = max([c for c in range(1, l + 1) if l % c == 0 and first % c == 0 and c * per_matrix <= CAST_BLOCK_BYTES] or [1])
    nb = n
    while lb == 1 and k * nb * 4 > CAST_BLOCK_BYTES and nb % (2 * LANE) == 0:
        nb //= 2
    return pl.pallas_call(
        _cast_kernel,
        grid=(l // lb, n // nb),
        in_specs=[pl.BlockSpec((lb, k, nb), lambda i, j: (first // lb + i, 0, j))],
        out_specs=pl.BlockSpec((lb, k_pad, nb), lambda i, j: (i, 0, j)),
        out_shape=jax.ShapeDtypeStruct((l, k_pad, n), BF16),
        compiler_params=_params("parallel", "parallel"),
        name="to_bf16",
    )(w)


def _cast_group_kernel(*refs):
    m = len(refs) // 2
    for w_ref, o_ref in zip(refs[:m], refs[m:]):
        o_ref[...] = w_ref[...].astype(BF16)


def _to_bf16_group(ws, first, count):
    _, k, n = ws[0].shape
    m = len(ws)
    budget = CAST_BLOCK_BYTES // m
    lb = max([c for c in range(1, count + 1) if count % c == 0 and first % c == 0 and c * k * n * 4 <= budget] or [1])
    return pl.pallas_call(
        _cast_group_kernel,
        grid=(count // lb,),
        in_specs=[pl.BlockSpec((lb, k, n), lambda i: (first // lb + i, 0, 0))] * m,
        out_specs=[pl.BlockSpec((lb, k, n), lambda i: (i, 0, 0))] * m,
        out_shape=[jax.ShapeDtypeStruct((count, k, n), BF16)] * m,
        compiler_params=_params("parallel"),
        name="to_bf16_group",
    )(*ws)


def _inproj_kernel(x_ref, nw_ref, w_ref, *refs, w_transposed):
    o_ref = refs[-1]
    h = _rms(x_ref[...], nw_ref[...]).astype(BF16)
    o_ref[...] = _dot_nt(h, w_ref[...]) if w_transposed else _dot(h, w_ref[...])


def _inproj(x, n, norm_w, w, w_transposed=False, after=()):
    d = D_MODEL
    n_out = w.shape[0] if w_transposed else w.shape[1]
    return pl.pallas_call(
        functools.partial(_inproj_kernel, w_transposed=w_transposed),
        grid=(n // ROW_TILE,),
        in_specs=[pl.BlockSpec((ROW_TILE, d), lambda i: (i, 0)),
                  pl.BlockSpec((1, d), lambda i: (0, 0)),
                  pl.BlockSpec(w.shape, lambda i: (0, 0))] + [pl.BlockSpec(memory_space=pl.ANY) for _ in after],
        out_specs=pl.BlockSpec((ROW_TILE, n_out), lambda i: (i, 0)),
        out_shape=jax.ShapeDtypeStruct((n, n_out), F32),
        compiler_params=_params("parallel"),
        name="inproj",
    )(x, norm_w.reshape(1, d), w, *after)


def _decay_tables():
    c = CHUNK
    a = np.zeros((LEVELS + 2, c, c), np.float32)
    t = np.arange(c)
    for l in range(LEVELS):
        s = 1 << l
        for r in range(c):
            mid = (r // (2 * s)) * 2 * s + s - 1
            if (r >> l) & 1:
                a[l, r] = (t > mid) & (t <= r)
            else:
                a[l, r] = (t > r) & (t <= mid)
    a[LEVELS] = t[None, :] <= t[:, None]
    a[LEVELS + 1] = t[None, :] > t[:, None]
    a = a.reshape((LEVELS + 2) * c, c)
    a3 = np.concatenate([a, a], axis=1)
    i, j = np.meshgrid(t, t, indexing="ij")
    x = np.maximum(i ^ j, 1)
    lvl = np.where(j < i, np.floor(np.log2(x)).astype(np.int32), np.where(i == j, LEVELS, -1))
    return jnp.asarray(a3, BF16), jnp.asarray(lvl, jnp.int32)


def _decay_factors(a3, g):
    g_hi = g.astype(BF16)
    g_lo = (g - g_hi.astype(F32)).astype(BF16)
    return jnp.exp(_dot(a3, jnp.concatenate([g_hi, g_lo], axis=0)))


def _gla_chunk(q, k, v, f, st_ref, lvl, rowid):
    c = CHUNK
    sc = jnp.where(lvl == LEVELS, jnp.sum(q * k, axis=-1, keepdims=True), 0.0)
    for l in range(LEVELS):
        s = 1 << l
        if s % 8 == 0:
            qk = jnp.concatenate([(q if b & 1 else k)[b * s:(b + 1) * s] for b in range(c // s)], axis=0)
        else:
            qk = jnp.where(((rowid >> l) & 1) == 1, q, k)
        m = (qk * f[l * c:(l + 1) * c]).astype(BF16)
        sc = jnp.where(lvl == l, _dot_nt(m, m), sc)
    qd = (q * f[LEVELS * c:(LEVELS + 1) * c]).astype(BF16)
    kd = (k * f[(LEVELS + 1) * c:(LEVELS + 2) * c]).astype(BF16)
    vb = v.astype(BF16)
    st = st_ref[...]
    o = _dot(sc.astype(BF16), vb) + _dot_nt(qd, st.astype(BF16))
    chunk_decay = f[(LEVELS + 1) * c - 1:(LEVELS + 1) * c, :]
    st_ref[...] = st * chunk_decay + _dot_tn(vb, kd)
    return o


def _hgrn_kernel(q_ref, f_ref, i_ref, g_ref, lb_ref, on_ref, a3_ref, lvl_ref, o_ref, st_ref):
    @pl.when(pl.program_id(1) == 0)
    def _():
        st_ref[...] = jnp.zeros_like(st_ref)

    lvl = lvl_ref[...]
    rowid = lax.broadcasted_iota(jnp.int32, (CHUNK, 1), 0)
    lb = lb_ref[...]
    for c in range(GLA_STEP_CHUNKS):
        rows = slice(c * CHUNK, (c + 1) * CHUNK)
        fl = f_ref[rows, :]
        f = _decay_factors(a3_ref[...], jnp.log(lb + (1.0 - lb) * jax.nn.sigmoid(fl)))
        key_in = (1.0 - lb) * jax.nn.sigmoid(-fl)
        for h in range(HEADS):
            sl = slice(h * DK, (h + 1) * DK)
            q = _silu(q_ref[rows, sl]) * (DK ** -0.5)
            o = _gla_chunk(q, key_in[:, sl], i_ref[rows, sl], f[:, sl], st_ref.at[h], lvl, rowid)
            o = _rms(o, on_ref[...])
            o_ref[rows, sl] = (o * _silu(g_ref[rows, sl])).astype(BF16)


def _hgrn(proj, lb, o_norm, batch, seq):
    n = proj.shape[0]
    rows = CHUNK * GLA_STEP_CHUNKS
    nt = seq // rows
    a3, lvl = _decay_tables()
    col = lambda j: pl.BlockSpec((rows, HG_WIDTH), lambda b, t: (b * nt + t, j))
    const = lambda shape: pl.BlockSpec(shape, lambda b, t: (0, 0))
    return pl.pallas_call(
        _hgrn_kernel,
        grid=(batch, nt),
        in_specs=[col(0), col(1), col(2), col(3), const((1, HG_WIDTH)), const((1, DK)),
                  const(a3.shape), const(lvl.shape)],
        out_specs=pl.BlockSpec((rows, HG_WIDTH), lambda b, t: (b * nt + t, 0)),
        out_shape=jax.ShapeDtypeStruct((n, HG_WIDTH), BF16),
        scratch_shapes=[pltpu.VMEM((HEADS, DK, DK), F32)],
        compiler_params=_params("parallel", "arbitrary"),
        name="hgrn",
    )(proj, proj, proj, proj, lb.reshape(1, HG_WIDTH), o_norm.reshape(1, DK), a3, lvl)


def _log_sigmoid(x):
    return jnp.minimum(x, 0.0) - jnp.log1p(jnp.exp(-jnp.abs(x)))


def _gla_kernel(q_ref, k_ref, v_ref, g_ref, a_ref, wa_ref, ba_ref, on_ref, a3_ref, lvl_ref, o_ref, st_ref):
    @pl.when(pl.program_id(1) == 0)
    def _():
        st_ref[...] = jnp.zeros_like(st_ref)

    lvl = lvl_ref[...]
    rowid = lax.broadcasted_iota(jnp.int32, (CHUNK, 1), 0)
    for c in range(GLA_STEP_CHUNKS):
        rows = slice(c * CHUNK, (c + 1) * CHUNK)
        gate_logit = _dot(a_ref[rows, :].astype(BF16), wa_ref[...]) + ba_ref[...]
        f = _decay_factors(a3_ref[...], _log_sigmoid(gate_logit) * (1.0 / GLA_TAU))
        for h in range(HEADS):
            sl = slice(h * DK, (h + 1) * DK)
            sv = slice(h * GLA_DV, (h + 1) * GLA_DV)
            q = q_ref[rows, sl] * (DK ** -0.5)
            o = _gla_chunk(q, k_ref[rows, sl], v_ref[rows, sv], f[:, sl], st_ref.at[h], lvl, rowid)
            o = _rms(o, on_ref[...])
            o_ref[rows, sv] = (o * _silu(g_ref[rows, sv])).astype(BF16)


def _gla(proj, w_alpha, b_alpha, o_norm, batch, seq):
    n = proj.shape[0]
    rows = CHUNK * GLA_STEP_CHUNKS
    nt = seq // rows
    qk_w = HEADS * DK
    v_w = HEADS * GLA_DV
    a3, lvl = _decay_tables()
    row = lambda b, t: b * nt + t
    const = lambda shape: pl.BlockSpec(shape, lambda b, t: (0, 0))
    return pl.pallas_call(
        _gla_kernel,
        grid=(batch, nt),
        in_specs=[pl.BlockSpec((rows, qk_w), lambda b, t: (row(b, t), 0)),
                  pl.BlockSpec((rows, qk_w), lambda b, t: (row(b, t), 1)),
                  pl.BlockSpec((rows, v_w), lambda b, t: (row(b, t), 1)),
                  pl.BlockSpec((rows, v_w), lambda b, t: (row(b, t), 2)),
                  pl.BlockSpec((rows, GLA_RANK_PAD), lambda b, t: (row(b, t), (2 * qk_w + 2 * v_w) // GLA_RANK_PAD)),
                  const((GLA_RANK_PAD, qk_w)), const((1, qk_w)), const((1, GLA_DV)),
                  const(a3.shape), const(lvl.shape)],
        out_specs=pl.BlockSpec((rows, v_w), lambda b, t: (row(b, t), 0)),
        out_shape=jax.ShapeDtypeStruct((n, v_w), BF16),
        scratch_shapes=[pltpu.VMEM((HEADS, GLA_DV, DK), F32)],
        compiler_params=_params("parallel", "arbitrary"),
        name="gla",
    )(proj, proj, proj, proj, proj, w_alpha, b_alpha.reshape(1, qk_w), o_norm.reshape(1, GLA_DV), a3, lvl)


def _qk_norm(x, w2):
    lane = lax.broadcasted_iota(jnp.int32, x.shape, 1)
    first = lane < DA_HEAD_DIM
    sq = x * x
    ms1 = jnp.sum(jnp.where(first, sq, 0.0), axis=-1, keepdims=True) * (1.0 / DA_HEAD_DIM)
    ms2 = jnp.sum(jnp.where(first, 0.0, sq), axis=-1, keepdims=True) * (1.0 / DA_HEAD_DIM)
    r = jnp.where(first, lax.rsqrt(ms1 + EPS), lax.rsqrt(ms2 + EPS))
    return x * r * w2


def _score_bound(q_norm_w, k_norm_w):
    return (jnp.max(jnp.abs(q_norm_w)) * jnp.max(jnp.abs(k_norm_w))
            * (DA_HEAD_DIM * DA_HEAD_DIM ** -0.5 * LOG2_E * SCORE_BOUND_MARGIN))


def _attn_kernel(qi_tab, ki_tab, q_ref, k_ref, v_ref, qn_ref, kn_ref, lam_ref, sub_ref, o_ref,
                 qs_ref, m_ref, l_ref, acc_ref, *, lam_init, bounded):
    p = pl.program_id(2)
    qi = qi_tab[p]
    ki = ki_tab[p]
    tq = ATTN_BLOCK
    dv = 2 * DA_HEAD_DIM
    heads = [slice(h * dv, (h + 1) * dv) for h in range(ATTN_HEADS)]

    @pl.when(ki == 0)
    def _():
        for h, sl in enumerate(heads):
            qn = _qk_norm(q_ref[:, sl], qn_ref[...])
            qt = (qn * (DA_HEAD_DIM ** -0.5 * LOG2_E)).T
            top = lax.broadcasted_iota(jnp.int32, qt.shape, 0) < DA_HEAD_DIM
            q1 = jnp.where(top, qt, 0.0).astype(BF16)
            q2 = jnp.where(top, 0.0, qt).astype(BF16)
            for copy in range(1 if bounded else 2):
                qs_ref[h, copy, :, :tq] = q1
                qs_ref[h, copy, :, tq:] = q2
            if bounded:
                bound = _score_bound(qn_ref[...], kn_ref[...])
                row = lax.broadcasted_iota(jnp.int32, (dv, 2 * tq), 0)
                qs_ref[h, 1] = jnp.where(row == 0, -bound, 0.0).astype(BF16)
        if not bounded:
            m_ref[...] = jnp.full_like(m_ref, -jnp.inf)
        l_ref[...] = jnp.zeros_like(l_ref)
        acc_ref[...] = jnp.zeros_like(acc_ref)

    def step(masked):
        tk = k_ref.shape[0]
        ones = jnp.ones((16, tk), BF16)
        unit = jnp.where(lax.broadcasted_iota(jnp.int32, (tk, dv), 1) == 0, 1.0, 0.0).astype(BF16)
        for h, sl in enumerate(heads):
            kn = _qk_norm(k_ref[:, sl], kn_ref[...]).astype(BF16)

            def mask(s):
                if not masked:
                    return s
                key = lax.broadcasted_iota(jnp.int32, s.shape, 0)
                qry = lax.broadcasted_iota(jnp.int32, s.shape, 1)
                qry = jnp.where(qry >= tq, qry - tq, qry)
                return jnp.where(key <= qry, s, -jnp.inf)

            if bounded:
                shifted = mask(_dot(jnp.concatenate([kn, unit], axis=1), qs_ref[h].reshape(2 * dv, 2 * tq)))
                alpha = 1.0
            else:
                m_prev = m_ref[h]
                m_new = jnp.maximum(m_prev, jnp.max(mask(_dot(kn, qs_ref[h, 0])), axis=0, keepdims=True))
                alpha = jnp.exp2(m_prev - m_new)
                shifted = mask(_dot(kn, qs_ref[h, 1])) - m_new
                m_ref[h] = m_new
            pr = jnp.exp2(shifted).astype(BF16) if bounded else jnp.exp2(shifted.astype(BF16))
            vt = jnp.concatenate([v_ref[:, sl].T.astype(BF16), ones], axis=0)
            pv = _dot(vt, pr)
            l_ref[h] = alpha * l_ref[h] + pv[dv:dv + 1]
            acc_ref[h] = alpha * acc_ref[h] + pv[:dv]

    @pl.when(ki < qi)
    def _():
        step(False)

    @pl.when(ki == qi)
    def _():
        step(True)
        lv = lam_ref[...]
        lam = (jnp.exp(jnp.sum(lv[0:1] * lv[1:2], axis=-1, keepdims=True))
               - jnp.exp(jnp.sum(lv[2:3] * lv[3:4], axis=-1, keepdims=True)) + lam_init)
        for h, sl in enumerate(heads):
            o1 = acc_ref[h, :, :tq] / l_ref[h, :, :tq]
            o2 = acc_ref[h, :, tq:] / l_ref[h, :, tq:]
            o = o1 - lam * o2
            o = o * lax.rsqrt(jnp.mean(o * o, axis=0, keepdims=True) + EPS) * sub_ref[...] * (1.0 - lam_init)
            o_ref[:, sl] = o.T.astype(BF16)


def _diff_attn(proj, q_norm, k_norm, lam_vecs, subln, lam_init, batch, seq):
    n = proj.shape[0]
    blk = ATTN_BLOCK
    nq = seq // blk
    dv = 2 * DA_HEAD_DIM
    pairs = [(qi, ki) for qi in range(nq) for ki in range(qi + 1)]
    qi_tab = jnp.asarray([p[0] for p in pairs], jnp.int32)
    ki_tab = jnp.asarray([p[1] for p in pairs], jnp.int32)
    width = ATTN_HEADS * dv
    groups = HEADS // ATTN_HEADS
    q_col = (4 * HG_WIDTH) // width
    const = lambda shape: pl.BlockSpec(shape, lambda b, g, p, qt, kt: (0, 0))
    grid_spec = pltpu.PrefetchScalarGridSpec(
        num_scalar_prefetch=2,
        grid=(batch, groups, len(pairs)),
        in_specs=[pl.BlockSpec((blk, width), lambda b, g, p, qt, kt: (b * nq + qt[p], q_col + g)),
                  pl.BlockSpec((blk, width), lambda b, g, p, qt, kt: (b * nq + kt[p], q_col + groups + g)),
                  pl.BlockSpec((blk, width), lambda b, g, p, qt, kt: (b * nq + kt[p], q_col + 2 * groups + g)),
                  const((1, dv)), const((1, dv)), const((4, DA_HEAD_DIM)), const((dv, 1))],
        out_specs=pl.BlockSpec((blk, width), lambda b, g, p, qt, kt: (b * nq + qt[p], g)),
        scratch_shapes=[pltpu.VMEM((ATTN_HEADS, 2, dv, 2 * blk), BF16),
                        pltpu.VMEM((ATTN_HEADS, 1, 2 * blk), F32),
                        pltpu.VMEM((ATTN_HEADS, 1, 2 * blk), F32),
                        pltpu.VMEM((ATTN_HEADS, dv, 2 * blk), F32)])
    args = (qi_tab, ki_tab, proj, proj, proj, jnp.tile(q_norm, 2).reshape(1, -1), jnp.tile(k_norm, 2).reshape(1, -1),
            lam_vecs, subln.reshape(-1, 1))

    def run(bounded):
        return pl.pallas_call(
            functools.partial(_attn_kernel, lam_init=lam_init, bounded=bounded),
            grid_spec=grid_spec,
            out_shape=jax.ShapeDtypeStruct((n, HEADS * dv), BF16),
            compiler_params=_params("parallel", "parallel", "arbitrary"),
            name="diff_attn_bounded" if bounded else "diff_attn",
        )(*args)

    return lax.cond(_score_bound(q_norm, k_norm) <= MAX_SCORE_BOUND, lambda: run(True), lambda: run(False))


def _outproj_router_kernel(*refs, n_a):
    a_refs = refs[:n_a]
    (w_ref, x_ref, nw_ref, wr_hi_ref, wr_lo_ref, br_ref, tri_ref,
     x2_ref, meta_ref, rec_ref, cnt_ref, carry_ref) = refs[n_a:]

    @pl.when(pl.program_id(0) == 0)
    def _():
        carry_ref[...] = jnp.zeros_like(carry_ref)

    x2 = x_ref[...]
    off = 0
    for a_ref in a_refs:
        kk = a_ref.shape[1]
        x2 = x2 + _dot(a_ref[...], w_ref[off:off + kk, :])
        off += kk
    x2_ref[...] = x2

    h = _rms(x2, nw_ref[...])
    h_hi = h.astype(BF16)
    h_lo = (h - h_hi.astype(F32)).astype(BF16)
    wr_hi = wr_hi_ref[...]
    by_hi = _dot_nt(jnp.concatenate([wr_hi, wr_lo_ref[...]], axis=0), h_hi)
    lt = by_hi[:ROUTER_ROWS] + by_hi[ROUTER_ROWS:] + _dot_nt(wr_hi, h_lo) + br_ref[...]
    row = lambda r: lt[r:r + 1, :]

    gl = [row(g) for g in range(N_GROUPS)]
    gmax = functools.reduce(jnp.maximum, gl)
    gsel = jnp.where(gl[0] == gmax, 0.0, jnp.where(gl[1] == gmax, 1.0, jnp.where(gl[2] == gmax, 2.0, 3.0)))
    g_w = 1.0 / functools.reduce(jnp.add, [jnp.exp(x - gmax) for x in gl])

    el = []
    for e in range(EXPERTS_PER_GROUP):
        v = row(N_GROUPS + e)
        for g in range(1, N_GROUPS):
            v = jnp.where(gsel == float(g), row(N_GROUPS + EXPERTS_PER_GROUP * g + e), v)
        el.append(v)
    emax = functools.reduce(jnp.maximum, el)
    sel = []
    for e in range(EXPERTS_PER_GROUP):
        ahead = jnp.zeros_like(emax)
        for o in range(EXPERTS_PER_GROUP):
            if o < e:
                ahead = ahead + jnp.where(el[o] >= el[e], 1.0, 0.0)
            elif o > e:
                ahead = ahead + jnp.where(el[o] > el[e], 1.0, 0.0)
        sel.append(ahead < 2.0)
    pe = [jnp.where(sel[e], jnp.exp(el[e] - emax), 0.0) for e in range(EXPERTS_PER_GROUP)]
    scale = g_w / functools.reduce(jnp.add, pe)
    zero = jnp.zeros_like(emax)
    seen = zero
    lo_idx, hi_idx, c_lo, c_hi = zero, zero, zero, zero
    for e in range(EXPERTS_PER_GROUP):
        is_lo = sel[e] & (seen == 0.0)
        is_hi = sel[e] & (seen == 1.0)
        lo_idx = jnp.where(is_lo, float(e), lo_idx)
        hi_idx = jnp.where(is_hi, float(e), hi_idx)
        c_lo = jnp.where(is_lo, pe[e] * scale, c_lo)
        c_hi = jnp.where(is_hi, pe[e] * scale, c_hi)
        seen = seen + jnp.where(sel[e], 1.0, 0.0)
    pair = zero
    for p in range(1, N_PAIRS):
        pair = jnp.where((lo_idx == float(PAIR_LO[p])) & (hi_idx == float(PAIR_HI[p])), float(p), pair)
    bucket = gsel * float(N_PAIRS) + pair

    bid = lax.broadcasted_iota(jnp.int32, (ROUTER_ROWS, bucket.shape[1]), 0).astype(F32)
    onehot = jnp.where(bid == bucket, 1.0, 0.0)
    before = _dot(onehot.astype(BF16), tri_ref[...]) + carry_ref[...]
    rank = jnp.sum(onehot * before, axis=0, keepdims=True)
    carry = carry_ref[...] + jnp.sum(onehot, axis=1, keepdims=True)
    carry_ref[...] = carry
    cnt_ref[...] = carry
    meta = jnp.concatenate([bucket, rank, c_lo, c_hi, zero, zero, zero, zero], axis=0)
    meta_ref[...] = meta
    rec_ref[...] = jnp.concatenate([meta, jnp.zeros((LANE - meta.shape[0], meta.shape[1]), F32)], axis=0).T


def _outproj_router(a_list, w_out, x, n, norm_w, w_group, b_group, w_router, b_router):
    d = D_MODEL
    tm = ROUTER_TILE
    tok = lambda: pl.BlockSpec((tm, d), lambda i: (i, 0))
    wr = jnp.concatenate([w_group, w_router.transpose(1, 0, 2).reshape(d, N_EXPERTS)], axis=1)
    wr_t = jnp.zeros((ROUTER_ROWS, d), F32).at[:wr.shape[1]].set(wr.T)
    wr_hi = wr_t.astype(BF16)
    wr_lo = (wr_t - wr_hi.astype(F32)).astype(BF16)
    br = jnp.zeros((ROUTER_ROWS, 1), F32).at[:wr.shape[1], 0].set(
        jnp.concatenate([b_group, b_router.reshape(-1)]))
    tri = jnp.asarray(np.triu(np.ones((tm, tm), np.float32), 1), BF16)
    const = lambda shape: pl.BlockSpec(shape, lambda i: (0, 0))
    in_specs = [pl.BlockSpec((tm, a.shape[1]), lambda i: (i, 0)) for a in a_list]
    in_specs += [const((d, d)), tok(), const((1, d)),
                 const((ROUTER_ROWS, d)), const((ROUTER_ROWS, d)), const((ROUTER_ROWS, 1)), const((tm, tm))]
    return pl.pallas_call(
        functools.partial(_outproj_router_kernel, n_a=len(a_list)),
        grid=(n // tm,),
        in_specs=in_specs,
        out_specs=[tok(),
                   pl.BlockSpec((8, tm), lambda i: (0, i)),
                   pl.BlockSpec((tm, LANE), lambda i: (i, 0)),
                   const((ROUTER_ROWS, 1))],
        out_shape=[jax.ShapeDtypeStruct((n, d), F32),
                   jax.ShapeDtypeStruct((8, n), F32),
                   jax.ShapeDtypeStruct((n, LANE), F32),
                   jax.ShapeDtypeStruct((ROUTER_ROWS, 1), F32)],
        scratch_shapes=[pltpu.VMEM((ROUTER_ROWS, 1), F32)],
        compiler_params=_params("arbitrary"),
        name="outproj_router",
    )(*a_list, w_out, x, norm_w.reshape(1, d), wr_hi, wr_lo, br, tri)


def _gather_tokens(table, idx):
    m = idx.shape[0]
    info = plsc.get_sparse_core_info()
    n_workers = info.num_cores * info.num_subcores
    per_worker = m // n_workers
    n_chunks = per_worker // SC_CHUNK
    assert per_worker * n_workers == m and n_chunks * SC_CHUNK == per_worker and n_chunks % 2 == 0
    mesh = plsc.VectorSubcoreMesh(core_axis_name="core", subcore_axis_name="subcore")
    rows = pltpu.VMEM((SC_CHUNK,) + table.shape[1:], table.dtype)

    @functools.partial(
        pl.kernel, mesh=mesh,
        out_type=jax.ShapeDtypeStruct((m,) + table.shape[1:], table.dtype),
        scratch_types=[pltpu.VMEM((n_chunks, SC_CHUNK), jnp.int32), rows, rows,
                       pltpu.SemaphoreType.DMA, pltpu.SemaphoreType.DMA],
        compiler_params=pltpu.CompilerParams(use_tc_tiling_on_sc=True),
        name="gather_tokens")
    def gather(table_hbm, idx_hbm, out_hbm, idx_v, rows_a, rows_b, sem_a, sem_b):
        worker = lax.axis_index("subcore") * info.num_cores + lax.axis_index("core")
        bufs = ((rows_a, sem_a), (rows_b, sem_b))

        def fetch(j, buf):
            return pltpu.make_async_copy(table_hbm.at[idx_v.at[j]], bufs[buf][0], bufs[buf][1])

        pltpu.sync_copy(idx_hbm.at[worker], idx_v)
        fetch(0, 0).start()

        @pl.loop(0, n_chunks, step=2)
        def _(j0):
            for buf in range(2):
                j = j0 + buf
                fetch(j, buf).wait()

                @pl.when(j + 1 < n_chunks)
                def _():
                    fetch(j + 1, 1 - buf).start()

                off = pl.multiple_of(worker * per_worker + j * SC_CHUNK, SC_CHUNK)
                pltpu.sync_copy(bufs[buf][0], out_hbm.at[pl.ds(off, SC_CHUNK)])

    return gather(table, idx.reshape(n_workers, n_chunks, SC_CHUNK))


def _scatter_tokens(arrays, idx, m):
    n = idx.shape[0]
    na = len(arrays)
    info = plsc.get_sparse_core_info()
    n_workers = info.num_cores * info.num_subcores
    per_worker = n // n_workers
    n_chunks = per_worker // SC_CHUNK
    assert per_worker * n_workers == n and n_chunks * SC_CHUNK == per_worker and n_chunks % 2 == 0
    mesh = plsc.VectorSubcoreMesh(core_axis_name="core", subcore_axis_name="subcore")
    bufs = [pltpu.VMEM((SC_CHUNK,) + a.shape[1:], a.dtype) for a in arrays for _ in range(2)]

    @functools.partial(
        pl.kernel, mesh=mesh,
        out_type=[jax.ShapeDtypeStruct((m,) + a.shape[1:], a.dtype) for a in arrays],
        scratch_types=[pltpu.VMEM((n_chunks, SC_CHUNK), jnp.int32)] + bufs + [pltpu.SemaphoreType.DMA] * (2 * na),
        compiler_params=pltpu.CompilerParams(use_tc_tiling_on_sc=True),
        name="scatter_tokens")
    def scatter(*refs):
        ins, idx_hbm, outs = refs[:na], refs[na], refs[na + 1:2 * na + 1]
        idx_v = refs[2 * na + 1]
        buf_v = refs[2 * na + 2:4 * na + 2]
        sem_v = refs[4 * na + 2:]
        worker = lax.axis_index("subcore") * info.num_cores + lax.axis_index("core")

        def load(a, j, b):
            off = pl.multiple_of(worker * per_worker + j * SC_CHUNK, SC_CHUNK)
            return pltpu.make_async_copy(ins[a].at[pl.ds(off, SC_CHUNK)], buf_v[2 * a + b], sem_v[2 * a + b])

        pltpu.sync_copy(idx_hbm.at[worker], idx_v)
        for a in range(na):
            load(a, 0, 0).start()

        @pl.loop(0, n_chunks, step=2)
        def _(j0):
            for b in range(2):
                j = j0 + b
                for a in range(na):
                    load(a, j, b).wait()

                @pl.when(j + 1 < n_chunks)
                def _():
                    for a in range(na):
                        load(a, j + 1, 1 - b).start()

                for a in range(na):
                    pltpu.sync_copy(buf_v[2 * a + b], outs[a].at[idx_v.at[j]])

    return scatter(*arrays, idx.reshape(n_workers, n_chunks, SC_CHUNK))


def _moe_kernel(step_ids, e_lo, e_hi, n_valid, n_tiles, x_ref, rec_ref, nw_ref, *refs):
    del step_ids, e_lo, e_hi
    s = pl.program_id(0)
    o_ref = refs[-1]

    @pl.when(s * MOE_STEP_TILES < n_tiles[0])
    def _():
        for i in range(MOE_STEP_TILES):
            g_lo_ref, g_hi_ref, u_lo_ref, u_hi_ref, d_lo_ref, d_hi_ref = refs[6 * i:6 * (i + 1)]
            rows = pl.ds(i * MOE_TILE, MOE_TILE)
            valid = lax.broadcasted_iota(jnp.int32, (MOE_TILE, 1), 0) < n_valid[s * MOE_STEP_TILES + i]
            x = jnp.where(valid, x_ref[rows, :], 0.0)
            h = _rms(x, nw_ref[...]).astype(BF16)
            rec = jnp.where(valid, rec_ref[rows, :], 0.0)

            def expert(g_ref, u_ref, d_ref):
                act = _silu(_dot(h, g_ref[0])) * _dot(h, u_ref[0])
                return _dot(act.astype(BF16), d_ref[0])

            y = (rec[:, 2:3] * expert(g_lo_ref, u_lo_ref, d_lo_ref)
                 + rec[:, 3:4] * expert(g_hi_ref, u_hi_ref, d_hi_ref))
            o_ref[rows, :] = jnp.where(valid, x_ref[rows, :], 0.0) + y


def _moe(x2, n, meta, rec, cnt, norm_w, w_gate, w_up, w_down, layer, w_down_bf16=None):
    d = D_MODEL
    tm = MOE_TILE
    max_tiles = n // tm + N_BUCKETS
    n_slots = max_tiles * tm
    counts = cnt[:N_BUCKETS, 0].astype(jnp.int32)
    tiles_per_bucket = (counts + tm - 1) // tm
    tile_end = jnp.cumsum(tiles_per_bucket)
    tile_start = tile_end - tiles_per_bucket
    n_tiles = tile_end[-1]
    in_bucket = meta[0][:, None] == jnp.arange(N_BUCKETS, dtype=F32)[None, :]
    dest = meta[1].astype(jnp.int32) + jnp.sum(jnp.where(in_bucket, tile_start[None, :] * tm, 0), axis=1)
    x_sorted, rec_sorted = _scatter_tokens([x2, rec], dest, n_slots)
    first = layer * N_EXPERTS
    w_gate, w_up = _to_bf16_group((w_gate, w_up), first, N_EXPERTS)
    w_down = _to_bf16(w_down, first=first, count=N_EXPERTS) if w_down_bf16 is None else w_down_bf16

    k = MOE_STEP_TILES
    tile_raw = jnp.arange(max_tiles, dtype=jnp.int32)
    tile_ids = jnp.minimum(tile_raw, n_tiles - 1)
    step_ids = jnp.minimum(jnp.arange(max_tiles // k, dtype=jnp.int32), (n_tiles - 1) // k)
    of_bucket = lambda table: jnp.sum(jnp.where(tile_bucket[:, None] == jnp.arange(N_BUCKETS)[None, :], table[None, :], 0), axis=1)
    tile_bucket = jnp.sum((tile_ids[:, None] >= tile_end[None, :]).astype(jnp.int32), axis=1)
    n_valid = jnp.clip(of_bucket(counts) - (tile_ids - of_bucket(tile_start)) * tm, 0, tm)
    n_valid = jnp.where(tile_raw < n_tiles, n_valid, 0)
    group = tile_bucket // N_PAIRS
    pair = tile_bucket % N_PAIRS
    pair_lo = jnp.sum(jnp.where(pair[:, None] == jnp.arange(N_PAIRS)[None, :], jnp.asarray(PAIR_LO, jnp.int32)[None, :], 0), axis=1)
    pair_hi = jnp.sum(jnp.where(pair[:, None] == jnp.arange(N_PAIRS)[None, :], jnp.asarray(PAIR_HI, jnp.int32)[None, :], 0), axis=1)
    e_lo = group * EXPERTS_PER_GROUP + pair_lo
    e_hi = group * EXPERTS_PER_GROUP + pair_hi

    step_rows = lambda s, sid, lo, hi, nv, nt: (sid[s], 0)
    weight_specs, weight_args = [], []
    for i in range(k):
        wlo = lambda s, sid, lo, hi, nv, nt, i=i: (lo[s * k + i], 0, 0)
        whi = lambda s, sid, lo, hi, nv, nt, i=i: (hi[s * k + i], 0, 0)
        weight_specs += [pl.BlockSpec((1, d, D_EXPERT), wlo), pl.BlockSpec((1, d, D_EXPERT), whi),
                         pl.BlockSpec((1, d, D_EXPERT), wlo), pl.BlockSpec((1, d, D_EXPERT), whi),
                         pl.BlockSpec((1, D_EXPERT, d), wlo), pl.BlockSpec((1, D_EXPERT, d), whi)]
        weight_args += [w_gate, w_gate, w_up, w_up, w_down, w_down]
    grid_spec = pltpu.PrefetchScalarGridSpec(
        num_scalar_prefetch=5,
        grid=(max_tiles // k,),
        in_specs=[pl.BlockSpec((k * tm, d), step_rows), pl.BlockSpec((k * tm, LANE), step_rows),
                  pl.BlockSpec((1, d), lambda s, sid, lo, hi, nv, nt: (0, 0))] + weight_specs,
        out_specs=pl.BlockSpec((k * tm, d), step_rows))
    out_sorted = pl.pallas_call(
        _moe_kernel,
        grid_spec=grid_spec,
        out_shape=jax.ShapeDtypeStruct((n_slots, d), F32),
        compiler_params=_params("arbitrary"),
        name="moe",
    )(step_ids, e_lo, e_hi, n_valid, n_tiles.reshape(1), x_sorted, rec_sorted, norm_w.reshape(1, d), *weight_args)
    return _gather_tokens(out_sorted, dest)


def kernel(x, attn_norm, ffn_norm, even_w_in, even_w_out, hg_lb_logits, hg_o_norm, da_q_norm, da_k_norm, da_lambda, da_subln, odd_w_in, gla_w_alpha, gla_b_alpha, gla_o_norm, odd_w_out, moe_w_group, moe_b_group, moe_w_router, moe_b_router, moe_w_gate, moe_w_up, moe_w_down):
    batch, seq, d = x.shape
    depth = attn_norm.shape[0]
    n = batch * seq
    xf = x.reshape(n, d)

    lb_sm = jax.nn.softmax(hg_lb_logits.astype(F32), axis=0)
    lb_all = jnp.cumsum(lb_sm, axis=0) - lb_sm[0:1]
    rank = gla_w_alpha.shape[1]

    experts = lambda w: w.reshape((-1,) + w.shape[2:])
    moe_w_gate, moe_w_up, moe_w_down = experts(moe_w_gate), experts(moe_w_up), experts(moe_w_down)

    down_ahead = None
    ahead = ()
    for layer in range(depth):
        j = layer // 2
        if layer % 2 == 0:
            lam_init = 0.8 - 0.6 * math.exp(-0.3 * layer)
            proj = _inproj(xf, n, attn_norm[layer], _to_bf16(even_w_in, first=j, count=1)[0], after=ahead)
            o_a = _hgrn(proj, lb_all[j], hg_o_norm[j], batch, seq)
            o_b = _diff_attn(proj, da_q_norm[j], da_k_norm[j], da_lambda[j], da_subln[j], lam_init, batch, seq)
            mix = [o_a, o_b]
            w_out = _to_bf16(even_w_out, first=j, count=1)[0]
        else:
            w_alpha = jnp.pad(gla_w_alpha[j], ((0, GLA_RANK_PAD - rank), (0, 0))).astype(BF16)
            w_in = _to_bf16(jnp.swapaxes(odd_w_in, 1, 2), odd_w_in.shape[2] + GLA_RANK_PAD - rank, first=j, count=1)[0]
            proj = _inproj(xf, n, attn_norm[layer], w_in, w_transposed=True, after=ahead)
            mix = [_gla(proj, w_alpha, gla_b_alpha[j], gla_o_norm[j], batch, seq)]
            w_out = _to_bf16(odd_w_out, first=j, count=1)[0]
        x2, meta, rec, cnt = _outproj_router(mix, w_out, xf, n, ffn_norm[layer], moe_w_group[layer],
                                        moe_b_group[layer], moe_w_router[layer], moe_b_router[layer])
        xf = _moe(x2, n, meta, rec, cnt, ffn_norm[layer], moe_w_gate, moe_w_up, moe_w_down, layer, down_ahead)
        down_ahead = (_to_bf16(moe_w_down, first=(layer + 1) * N_EXPERTS, count=N_EXPERTS)
                      if layer + 1 < depth else None)
        ahead = () if down_ahead is None else (down_ahead,)
    return xf.reshape(batch, seq, d)
```
